```python
import math
import jax, jax.numpy as jnp
from jax import lax
import numpy as np

D_MODEL = 2048
BATCH = 8
SEQ = 4096
DEPTH = 1

HEAD_DIM = 64
N_Q_HEADS = 16
N_KV_HEADS = 2
Q_PER_KV = N_Q_HEADS // N_KV_HEADS
ATTN_WIDTH = N_Q_HEADS * HEAD_DIM
KV_WIDTH = N_KV_HEADS * HEAD_DIM
WINDOW = 128
BLOCK = 128
NEG_INF = -1e30
SSM_WIDTH = D_MODEL // 2
SSM_GROUP = 16
N_SSM_GROUPS = SSM_WIDTH // SSM_GROUP
SSM_STATE = 64
DT_MIN = 1e-3
DT_MAX = 1e-1
D_FF = 5632
CONV_WIDTH = 3
N_BRANCHES = 2
IN_WIDTH = ATTN_WIDTH + 2 * KV_WIDTH + SSM_WIDTH + N_BRANCHES * D_MODEL
RMS_EPS = 1e-6

kernel_name = 'hybrid_swa_s5_convffn_adaln'


def rmsnorm(x, g):
    xf = x.astype(jnp.float32)
    xf = xf * lax.rsqrt(jnp.mean(xf * xf, axis=-1, keepdims=True) + RMS_EPS)
    return xf.astype(x.dtype) * g


def sliding_window_attention(q, k, v, sinks):
    b, l = q.shape[0], q.shape[1]
    nb = l // BLOCK
    qb = q.reshape(b, nb, BLOCK, N_KV_HEADS, Q_PER_KV, HEAD_DIM)
    pad = ((0, 0), (BLOCK, 0), (0, 0), (0, 0))
    kp = jnp.pad(k, pad).reshape(b, nb + 1, BLOCK, N_KV_HEADS, HEAD_DIM)
    vp = jnp.pad(v, pad).reshape(b, nb + 1, BLOCK, N_KV_HEADS, HEAD_DIM)
    kw = jnp.concatenate([kp[:, :-1], kp[:, 1:]], axis=2)
    vw = jnp.concatenate([vp[:, :-1], vp[:, 1:]], axis=2)
    s = jnp.einsum('bnqhgd,bnkhd->bnhgqk', qb, kw).astype(jnp.float32) * (HEAD_DIM ** -0.5)
    qi = jnp.arange(BLOCK)[:, None]
    kj = jnp.arange(2 * BLOCK)[None, :]
    rel = qi + BLOCK - kj
    band = (rel >= 0) & (rel < WINDOW)
    key_pos = jnp.arange(nb)[:, None, None] * BLOCK - BLOCK + kj[None]
    mask = band[None] & (key_pos >= 0)
    s = jnp.where(mask[None, :, None, None], s, NEG_INF)
    sink = sinks.astype(jnp.float32).reshape(N_KV_HEADS, Q_PER_KV)[None, None, :, :, None, None]
    m = jnp.maximum(jnp.max(s, axis=-1, keepdims=True), sink)
    p = jnp.exp(s - m)
    p = p / (jnp.sum(p, axis=-1, keepdims=True) + jnp.exp(sink - m))
    o = jnp.einsum('bnhgqk,bnkhd->bnqhgd', p.astype(v.dtype), vw)
    return o.reshape(b, l, ATTN_WIDTH)


def s5_layer(u, a_re, a_im, log_dt, b_re, b_im, c_re, c_im, d_skip):
    bsz, l = u.shape[0], u.shape[1]
    f32 = jnp.float32
    ug = u.reshape(bsz, l, N_SSM_GROUPS, SSM_GROUP).astype(f32)
    dt = jnp.exp(log_dt.astype(f32))[:, None]
    ar, ai = a_re.astype(f32), a_im.astype(f32)
    mag = jnp.exp(ar * dt)
    lr, li = mag * jnp.cos(ai * dt), mag * jnp.sin(ai * dt)
    den = ar * ar + ai * ai
    zr = ((lr - 1.0) * ar + li * ai) / den
    zi = (li * ar - (lr - 1.0) * ai) / den
    br, bi = b_re.astype(f32), b_im.astype(f32)
    bbar_r = zr[:, :, None] * br - zi[:, :, None] * bi
    bbar_i = zr[:, :, None] * bi + zi[:, :, None] * br
    xr = jnp.einsum('blgp,gnp->blgn', ug, bbar_r)
    xi = jnp.einsum('blgp,gnp->blgn', ug, bbar_i)
    seq_r = jnp.broadcast_to(lr[None, None], (1, l, N_SSM_GROUPS, SSM_STATE))
    seq_i = jnp.broadcast_to(li[None, None], (1, l, N_SSM_GROUPS, SSM_STATE))

    def combine(e1, e2):
        a1r, a1i, b1r, b1i = e1
        a2r, a2i, b2r, b2i = e2
        return (a1r * a2r - a1i * a2i,
                a1r * a2i + a1i * a2r,
                a2r * b1r - a2i * b1i + b2r,
                a2r * b1i + a2i * b1r + b2i)

    _, _, hr, hi = lax.associative_scan(combine, (seq_r, seq_i, xr, xi), axis=1)
    y = (jnp.einsum('blgn,gpn->blgp', hr, c_re.astype(f32))
         - jnp.einsum('blgn,gpn->blgp', hi, c_im.astype(f32))
         + d_skip.astype(f32) * ug)
    return y.reshape(bsz, l, SSM_WIDTH).astype(u.dtype)


def conv_ffn(h, w_up, conv_w, conv_b, w_down):
    up = h @ w_up
    gate, val = jnp.split(up, 2, axis=-1)
    gate = lax.conv_general_dilated(
        gate, conv_w[:, None, :], window_strides=(1,), padding=((CONV_WIDTH - 1, 0),),
        dimension_numbers=('NWC', 'WIO', 'NWC'), feature_group_count=D_FF) + conv_b
    return (jax.nn.silu(gate) * val) @ w_down


def _fwd_setup_inputs(seed: int = 0) -> dict:
    key = jax.random.key(seed)
    ks = jax.random.split(key, 24)
    f32 = jnp.float32

    def nrm(k, shape, scale):
        return jax.random.normal(k, shape, f32) * scale

    G, N, P = N_SSM_GROUPS, SSM_STATE, SSM_GROUP
    a_im0 = jnp.pi * jnp.arange(N, dtype=f32)
    return {
        'x': nrm(ks[0], (BATCH, SEQ, D_MODEL), 1.0),
        'c': nrm(ks[1], (BATCH, D_MODEL), 1.0),
        'ada_w': nrm(ks[2], (DEPTH, D_MODEL, 6 * D_MODEL), D_MODEL ** -0.5),
        'ada_b': nrm(ks[3], (DEPTH, 6 * D_MODEL), 0.01),
        'norm_mix_g': 1.0 + nrm(ks[4], (DEPTH, D_MODEL), 0.01),
        'w_in': nrm(ks[5], (DEPTH, D_MODEL, IN_WIDTH), D_MODEL ** -0.5),
        'attn_sinks': nrm(ks[6], (DEPTH, N_Q_HEADS), 0.5),
        'w_attn_proj': nrm(ks[7], (DEPTH, ATTN_WIDTH, D_MODEL), ATTN_WIDTH ** -0.5),
        'ssm_a_re': -0.5 + nrm(ks[8], (DEPTH, G, N), 0.01),
        'ssm_a_im': a_im0 + nrm(ks[9], (DEPTH, G, N), 0.01),
        'ssm_log_dt': jax.random.uniform(ks[10], (DEPTH, G), f32, math.log(DT_MIN), math.log(DT_MAX)),
        'ssm_b_re': nrm(ks[11], (DEPTH, G, N, P), (2 * P) ** -0.5),
        'ssm_b_im': nrm(ks[12], (DEPTH, G, N, P), (2 * P) ** -0.5),
        'ssm_c_re': nrm(ks[13], (DEPTH, G, P, N), N ** -0.5),
        'ssm_c_im': nrm(ks[14], (DEPTH, G, P, N), N ** -0.5),
        'ssm_d': nrm(ks[15], (DEPTH, G, P), 1.0),
        'w_ssm_glu': nrm(ks[16], (DEPTH, SSM_WIDTH, 2 * D_MODEL), SSM_WIDTH ** -0.5),
        'w_out': nrm(ks[17], (DEPTH, D_MODEL, D_MODEL), D_MODEL ** -0.5),
        'norm_ffn_g': 1.0 + nrm(ks[18], (DEPTH, D_MODEL), 0.01),
        'w_ffn_up': nrm(ks[19], (DEPTH, D_MODEL, 2 * D_FF), D_MODEL ** -0.5),
        'ffn_conv_w': nrm(ks[20], (DEPTH, CONV_WIDTH, D_FF), CONV_WIDTH ** -0.5),
        'ffn_conv_b': nrm(ks[21], (DEPTH, D_FF), 0.01),
        'w_ffn_down': nrm(ks[22], (DEPTH, D_FF, D_MODEL), D_FF ** -0.5),
        'final_g': 1.0 + nrm(ks[23], (D_MODEL,), 0.01),
    }


def _fwd_reference(x, c, ada_w, ada_b, norm_mix_g, w_in, attn_sinks, w_attn_proj,
              ssm_a_re, ssm_a_im, ssm_log_dt, ssm_b_re, ssm_b_im, ssm_c_re, ssm_c_im, ssm_d,
              w_ssm_glu, w_out, norm_ffn_g, w_ffn_up, ffn_conv_w, ffn_conv_b, w_ffn_down, final_g):
    bsz, l = x.shape[0], x.shape[1]
    cond = jax.nn.silu(c)
    splits = [ATTN_WIDTH, ATTN_WIDTH + KV_WIDTH, ATTN_WIDTH + 2 * KV_WIDTH,
              ATTN_WIDTH + 2 * KV_WIDTH + SSM_WIDTH, ATTN_WIDTH + 2 * KV_WIDTH + SSM_WIDTH + D_MODEL]
    for i in range(DEPTH):
        mod = (cond @ ada_w[i] + ada_b[i])[:, None, :]
        sh1, sc1, g1, sh2, sc2, g2 = jnp.split(mod, 6, axis=-1)
        h = rmsnorm(x, norm_mix_g[i]) * (1.0 + sc1) + sh1
        proj = h @ w_in[i]
        q, k, v, u, g_attn, g_ssm = jnp.split(proj, splits, axis=-1)
        q = q.reshape(bsz, l, N_Q_HEADS, HEAD_DIM)
        k = k.reshape(bsz, l, N_KV_HEADS, HEAD_DIM)
        v = v.reshape(bsz, l, N_KV_HEADS, HEAD_DIM)
        attn = sliding_window_attention(q, k, v, attn_sinks[i]) @ w_attn_proj[i]
        y = s5_layer(u, ssm_a_re[i], ssm_a_im[i], ssm_log_dt[i], ssm_b_re[i], ssm_b_im[i],
                     ssm_c_re[i], ssm_c_im[i], ssm_d[i])
        glu_a, glu_b = jnp.split(jax.nn.gelu(y) @ w_ssm_glu[i], 2, axis=-1)
        ssm = glu_a * jax.nn.sigmoid(glu_b)
        mixed = jax.nn.sigmoid(g_attn) * attn + jax.nn.sigmoid(g_ssm) * ssm
        x = x + g1 * (mixed @ w_out[i])
        h = rmsnorm(x, norm_ffn_g[i]) * (1.0 + sc2) + sh2
        x = x + g2 * conv_ffn(h, w_ffn_up[i], ffn_conv_w[i], ffn_conv_b[i], w_ffn_down[i])
    return rmsnorm(x, final_g)


import jax as _jax
import jax.numpy as _jnp

TWIN_FORMAT = 'train_step'
FWD_PARAMS = ['x', 'c', 'ada_w', 'ada_b', 'norm_mix_g', 'w_in', 'attn_sinks', 'w_attn_proj', 'ssm_a_re', 'ssm_a_im', 'ssm_log_dt', 'ssm_b_re', 'ssm_b_im', 'ssm_c_re', 'ssm_c_im', 'ssm_d', 'w_ssm_glu', 'w_out', 'norm_ffn_g', 'w_ffn_up', 'ffn_conv_w', 'ffn_conv_b', 'w_ffn_down', 'final_g']
TWIN_WEIGHTS = ['ada_w', 'ada_b', 'norm_mix_g', 'w_in', 'attn_sinks', 'w_attn_proj', 'ssm_a_re', 'ssm_a_im', 'ssm_log_dt', 'ssm_b_re', 'ssm_b_im', 'ssm_c_re', 'ssm_c_im', 'ssm_d', 'w_ssm_glu', 'w_out', 'norm_ffn_g', 'w_ffn_up', 'ffn_conv_w', 'ffn_conv_b', 'w_ffn_down', 'final_g']
TWIN_DIFF_INPUT = 'x'
TWIN_INPUTS = ['x', 'c', 'ada_w', 'ada_b', 'norm_mix_g', 'w_in', 'attn_sinks', 'w_attn_proj', 'ssm_a_re', 'ssm_a_im', 'ssm_log_dt', 'ssm_b_re', 'ssm_b_im', 'ssm_c_re', 'ssm_c_im', 'ssm_d', 'w_ssm_glu', 'w_out', 'norm_ffn_g', 'w_ffn_up', 'ffn_conv_w', 'ffn_conv_b', 'w_ffn_down', 'final_g', 'loss_target', 'm_ada_w', 'm_ada_b', 'm_norm_mix_g', 'm_w_in', 'm_attn_sinks', 'm_w_attn_proj', 'm_ssm_a_re', 'm_ssm_a_im', 'm_ssm_log_dt', 'm_ssm_b_re', 'm_ssm_b_im', 'm_ssm_c_re', 'm_ssm_c_im', 'm_ssm_d', 'm_w_ssm_glu', 'm_w_out', 'm_norm_ffn_g', 'm_w_ffn_up', 'm_ffn_conv_w', 'm_ffn_conv_b', 'm_w_ffn_down', 'm_final_g', 'v_ada_w', 'v_ada_b', 'v_norm_mix_g', 'v_w_in', 'v_attn_sinks', 'v_w_attn_proj', 'v_ssm_a_re', 'v_ssm_a_im', 'v_ssm_log_dt', 'v_ssm_b_re', 'v_ssm_b_im', 'v_ssm_c_re', 'v_ssm_c_im', 'v_ssm_d', 'v_w_ssm_glu', 'v_w_out', 'v_norm_ffn_g', 'v_w_ffn_up', 'v_ffn_conv_w', 'v_ffn_conv_b', 'v_w_ffn_down', 'v_final_g']
TWIN_OUTPUTS = ['loss', 'grad_x', 'grad_ada_w', 'grad_ada_b', 'grad_norm_mix_g', 'grad_w_in', 'grad_attn_sinks', 'grad_w_attn_proj', 'grad_ssm_a_re', 'grad_ssm_a_im', 'grad_ssm_log_dt', 'grad_ssm_b_re', 'grad_ssm_b_im', 'grad_ssm_c_re', 'grad_ssm_c_im', 'grad_ssm_d', 'grad_w_ssm_glu', 'grad_w_out', 'grad_norm_ffn_g', 'grad_w_ffn_up', 'grad_ffn_conv_w', 'grad_ffn_conv_b', 'grad_w_ffn_down', 'grad_final_g', 'delta_ada_w', 'delta_ada_b', 'delta_norm_mix_g', 'delta_w_in', 'delta_attn_sinks', 'delta_w_attn_proj', 'delta_ssm_a_re', 'delta_ssm_a_im', 'delta_ssm_log_dt', 'delta_ssm_b_re', 'delta_ssm_b_im', 'delta_ssm_c_re', 'delta_ssm_c_im', 'delta_ssm_d', 'delta_w_ssm_glu', 'delta_w_out', 'delta_norm_ffn_g', 'delta_w_ffn_up', 'delta_ffn_conv_w', 'delta_ffn_conv_b', 'delta_w_ffn_down', 'delta_final_g', 'new_m_ada_w', 'new_m_ada_b', 'new_m_norm_mix_g', 'new_m_w_in', 'new_m_attn_sinks', 'new_m_w_attn_proj', 'new_m_ssm_a_re', 'new_m_ssm_a_im', 'new_m_ssm_log_dt', 'new_m_ssm_b_re', 'new_m_ssm_b_im', 'new_m_ssm_c_re', 'new_m_ssm_c_im', 'new_m_ssm_d', 'new_m_w_ssm_glu', 'new_m_w_out', 'new_m_norm_ffn_g', 'new_m_w_ffn_up', 'new_m_ffn_conv_w', 'new_m_ffn_conv_b', 'new_m_w_ffn_down', 'new_m_final_g', 'new_v_ada_w', 'new_v_ada_b', 'new_v_norm_mix_g', 'new_v_w_in', 'new_v_attn_sinks', 'new_v_w_attn_proj', 'new_v_ssm_a_re', 'new_v_ssm_a_im', 'new_v_ssm_log_dt', 'new_v_ssm_b_re', 'new_v_ssm_b_im', 'new_v_ssm_c_re', 'new_v_ssm_c_im', 'new_v_ssm_d', 'new_v_w_ssm_glu', 'new_v_w_out', 'new_v_norm_ffn_g', 'new_v_w_ffn_up', 'new_v_ffn_conv_w', 'new_v_ffn_conv_b', 'new_v_w_ffn_down', 'new_v_final_g']
TWIN_LEAF_KINDS = {'loss': 'loss', 'grad_x': 'grad_x', 'grad_ada_w': 'grad_w', 'grad_ada_b': 'grad_w', 'grad_norm_mix_g': 'grad_w', 'grad_w_in': 'grad_w', 'grad_attn_sinks': 'grad_w', 'grad_w_attn_proj': 'grad_w', 'grad_ssm_a_re': 'grad_w', 'grad_ssm_a_im': 'grad_w', 'grad_ssm_log_dt': 'grad_w', 'grad_ssm_b_re': 'grad_w', 'grad_ssm_b_im': 'grad_w', 'grad_ssm_c_re': 'grad_w', 'grad_ssm_c_im': 'grad_w', 'grad_ssm_d': 'grad_w', 'grad_w_ssm_glu': 'grad_w', 'grad_w_out': 'grad_w', 'grad_norm_ffn_g': 'grad_w', 'grad_w_ffn_up': 'grad_w', 'grad_ffn_conv_w': 'grad_w', 'grad_ffn_conv_b': 'grad_w', 'grad_w_ffn_down': 'grad_w', 'grad_final_g': 'grad_w', 'delta_ada_w': 'delta_w', 'delta_ada_b': 'delta_w', 'delta_norm_mix_g': 'delta_w', 'delta_w_in': 'delta_w', 'delta_attn_sinks': 'delta_w', 'delta_w_attn_proj': 'delta_w', 'delta_ssm_a_re': 'delta_w', 'delta_ssm_a_im': 'delta_w', 'delta_ssm_log_dt': 'delta_w', 'delta_ssm_b_re': 'delta_w', 'delta_ssm_b_im': 'delta_w', 'delta_ssm_c_re': 'delta_w', 'delta_ssm_c_im': 'delta_w', 'delta_ssm_d': 'delta_w', 'delta_w_ssm_glu': 'delta_w', 'delta_w_out': 'delta_w', 'delta_norm_ffn_g': 'delta_w', 'delta_w_ffn_up': 'delta_w', 'delta_ffn_conv_w': 'delta_w', 'delta_ffn_conv_b': 'delta_w', 'delta_w_ffn_down': 'delta_w', 'delta_final_g': 'delta_w', 'new_m_ada_w': 'new_m', 'new_m_ada_b': 'new_m', 'new_m_norm_mix_g': 'new_m', 'new_m_w_in': 'new_m', 'new_m_attn_sinks': 'new_m', 'new_m_w_attn_proj': 'new_m', 'new_m_ssm_a_re': 'new_m', 'new_m_ssm_a_im': 'new_m', 'new_m_ssm_log_dt': 'new_m', 'new_m_ssm_b_re': 'new_m', 'new_m_ssm_b_im': 'new_m', 'new_m_ssm_c_re': 'new_m', 'new_m_ssm_c_im': 'new_m', 'new_m_ssm_d': 'new_m', 'new_m_w_ssm_glu': 'new_m', 'new_m_w_out': 'new_m', 'new_m_norm_ffn_g': 'new_m', 'new_m_w_ffn_up': 'new_m', 'new_m_ffn_conv_w': 'new_m', 'new_m_ffn_conv_b': 'new_m', 'new_m_w_ffn_down': 'new_m', 'new_m_final_g': 'new_m', 'new_v_ada_w': 'new_v', 'new_v_ada_b': 'new_v', 'new_v_norm_mix_g': 'new_v', 'new_v_w_in': 'new_v', 'new_v_attn_sinks': 'new_v', 'new_v_w_attn_proj': 'new_v', 'new_v_ssm_a_re': 'new_v', 'new_v_ssm_a_im': 'new_v', 'new_v_ssm_log_dt': 'new_v', 'new_v_ssm_b_re': 'new_v', 'new_v_ssm_b_im': 'new_v', 'new_v_ssm_c_re': 'new_v', 'new_v_ssm_c_im': 'new_v', 'new_v_ssm_d': 'new_v', 'new_v_w_ssm_glu': 'new_v', 'new_v_w_out': 'new_v', 'new_v_norm_ffn_g': 'new_v', 'new_v_w_ffn_up': 'new_v', 'new_v_ffn_conv_w': 'new_v', 'new_v_ffn_conv_b': 'new_v', 'new_v_w_ffn_down': 'new_v', 'new_v_final_g': 'new_v'}


def _forward(args):
    return _fwd_reference(*[args[k] for k in FWD_PARAMS])


def _output_shape():
    def fwd():
        inp = _fwd_setup_inputs(0)
        return _fwd_reference(*[inp[k] for k in FWD_PARAMS])
    out = _jax.eval_shape(fwd)
    return out.shape, out.dtype

N_MICROBATCH = 1
ADAM_LR = 0.001
ADAM_B1 = 0.9
ADAM_B2 = 0.999
ADAM_EPS = 1e-08
ADAM_WD = 0.01
ADAM_STEP = 10
PER_EXAMPLE_BATCH_AXIS = {'x': 0, 'c': 0, 'loss_target': 0}
SHARED_INPUTS = []
_WEIGHT_DTYPES = {'ada_w': _jnp.float32, 'ada_b': _jnp.float32, 'norm_mix_g': _jnp.float32, 'w_in': _jnp.float32, 'attn_sinks': _jnp.float32, 'w_attn_proj': _jnp.float32, 'ssm_a_re': _jnp.float32, 'ssm_a_im': _jnp.float32, 'ssm_log_dt': _jnp.float32, 'ssm_b_re': _jnp.float32, 'ssm_b_im': _jnp.float32, 'ssm_c_re': _jnp.float32, 'ssm_c_im': _jnp.float32, 'ssm_d': _jnp.float32, 'w_ssm_glu': _jnp.float32, 'w_out': _jnp.float32, 'norm_ffn_g': _jnp.float32, 'w_ffn_up': _jnp.float32, 'ffn_conv_w': _jnp.float32, 'ffn_conv_b': _jnp.float32, 'w_ffn_down': _jnp.float32, 'final_g': _jnp.float32}
MOMENT_SCALE = {'ada_w': 3.168994e-02, 'ada_b': 5.403835e-02, 'norm_mix_g': 2.700145e-02, 'w_in': 1.953491e-02, 'attn_sinks': 1.196849e-02, 'w_attn_proj': 2.107219e-02, 'ssm_a_re': 2.429486e-03, 'ssm_a_im': 3.366238e-03, 'ssm_log_dt': 8.912282e-01, 'ssm_b_re': 1.460479e-03, 'ssm_b_im': 1.427787e-03, 'ssm_c_re': 2.105527e-03, 'ssm_c_im': 2.303570e-03, 'ssm_d': 2.327376e-02, 'w_ssm_glu': 1.209464e-02, 'w_out': 2.614665e-02, 'norm_ffn_g': 6.236515e-02, 'w_ffn_up': 2.945702e-02, 'ffn_conv_w': 3.059286e-02, 'ffn_conv_b': 2.181122e-02, 'w_ffn_down': 4.809893e-02, 'final_g': 1.618579e+01}


def _to_microbatches(a, axis):
    t = _jnp.moveaxis(a, axis, 0)
    t = t.reshape((N_MICROBATCH, t.shape[0] // N_MICROBATCH) + t.shape[1:])
    return _jnp.moveaxis(t, 1, axis + 1)


def setup_inputs(seed: int = 0) -> dict:
    inp = _fwd_setup_inputs(seed)
    key = _jax.random.fold_in(_jax.random.key(seed), 7919)
    shape, _ = _output_shape()
    out = dict(inp)
    out["loss_target"] = _jax.random.normal(_jax.random.fold_in(key, 0), shape, _jnp.float32)
    for i, name in enumerate(TWIN_WEIGHTS):
        w = inp[name].astype(_jnp.float32)
        if MOMENT_SCALE is None:
            s = _jnp.sqrt(_jnp.mean(_jnp.square(w)) + 1e-30)
        else:
            s = MOMENT_SCALE[name]
        km, kv = _jax.random.split(_jax.random.fold_in(key, i + 1))
        out[name] = w
        out["m_" + name] = s * _jax.random.normal(km, w.shape, _jnp.float32)
        out["v_" + name] = (s * s) * _jax.random.uniform(kv, w.shape, _jnp.float32, 0.5, 1.5)
    if N_MICROBATCH > 1:
        for name, axis in PER_EXAMPLE_BATCH_AXIS.items():
            out[name] = _to_microbatches(out[name], axis)
    return {'x': out['x'], 'c': out['c'], 'ada_w': out['ada_w'], 'ada_b': out['ada_b'], 'norm_mix_g': out['norm_mix_g'], 'w_in': out['w_in'], 'attn_sinks': out['attn_sinks'], 'w_attn_proj': out['w_attn_proj'], 'ssm_a_re': out['ssm_a_re'], 'ssm_a_im': out['ssm_a_im'], 'ssm_log_dt': out['ssm_log_dt'], 'ssm_b_re': out['ssm_b_re'], 'ssm_b_im': out['ssm_b_im'], 'ssm_c_re': out['ssm_c_re'], 'ssm_c_im': out['ssm_c_im'], 'ssm_d': out['ssm_d'], 'w_ssm_glu': out['w_ssm_glu'], 'w_out': out['w_out'], 'norm_ffn_g': out['norm_ffn_g'], 'w_ffn_up': out['w_ffn_up'], 'ffn_conv_w': out['ffn_conv_w'], 'ffn_conv_b': out['ffn_conv_b'], 'w_ffn_down': out['w_ffn_down'], 'final_g': out['final_g'], 'loss_target': out['loss_target'], 'm_ada_w': out['m_ada_w'], 'm_ada_b': out['m_ada_b'], 'm_norm_mix_g': out['m_norm_mix_g'], 'm_w_in': out['m_w_in'], 'm_attn_sinks': out['m_attn_sinks'], 'm_w_attn_proj': out['m_w_attn_proj'], 'm_ssm_a_re': out['m_ssm_a_re'], 'm_ssm_a_im': out['m_ssm_a_im'], 'm_ssm_log_dt': out['m_ssm_log_dt'], 'm_ssm_b_re': out['m_ssm_b_re'], 'm_ssm_b_im': out['m_ssm_b_im'], 'm_ssm_c_re': out['m_ssm_c_re'], 'm_ssm_c_im': out['m_ssm_c_im'], 'm_ssm_d': out['m_ssm_d'], 'm_w_ssm_glu': out['m_w_ssm_glu'], 'm_w_out': out['m_w_out'], 'm_norm_ffn_g': out['m_norm_ffn_g'], 'm_w_ffn_up': out['m_w_ffn_up'], 'm_ffn_conv_w': out['m_ffn_conv_w'], 'm_ffn_conv_b': out['m_ffn_conv_b'], 'm_w_ffn_down': out['m_w_ffn_down'], 'm_final_g': out['m_final_g'], 'v_ada_w': out['v_ada_w'], 'v_ada_b': out['v_ada_b'], 'v_norm_mix_g': out['v_norm_mix_g'], 'v_w_in': out['v_w_in'], 'v_attn_sinks': out['v_attn_sinks'], 'v_w_attn_proj': out['v_w_attn_proj'], 'v_ssm_a_re': out['v_ssm_a_re'], 'v_ssm_a_im': out['v_ssm_a_im'], 'v_ssm_log_dt': out['v_ssm_log_dt'], 'v_ssm_b_re': out['v_ssm_b_re'], 'v_ssm_b_im': out['v_ssm_b_im'], 'v_ssm_c_re': out['v_ssm_c_re'], 'v_ssm_c_im': out['v_ssm_c_im'], 'v_ssm_d': out['v_ssm_d'], 'v_w_ssm_glu': out['v_w_ssm_glu'], 'v_w_out': out['v_w_out'], 'v_norm_ffn_g': out['v_norm_ffn_g'], 'v_w_ffn_up': out['v_w_ffn_up'], 'v_ffn_conv_w': out['v_ffn_conv_w'], 'v_ffn_conv_b': out['v_ffn_conv_b'], 'v_w_ffn_down': out['v_w_ffn_down'], 'v_final_g': out['v_final_g']}


def _loss(weights, diff, rest, loss_target):
    with _jax.named_scope("forward"):
        args = {**rest, TWIN_DIFF_INPUT: diff, **{k: w.astype(_WEIGHT_DTYPES[k]) for k, w in weights.items()}}
        y = _forward(args)
    with _jax.named_scope("loss_head"):
        err = _jnp.square(y.astype(_jnp.float32) - loss_target)
        return 0.5 * _jnp.sum(_jnp.mean(err, axis=-1)) if err.ndim else 0.5 * err


def _adamw(w, g, m, v):
    m = ADAM_B1 * m + (1.0 - ADAM_B1) * g
    v = ADAM_B2 * v + (1.0 - ADAM_B2) * _jnp.square(g)
    m_hat = m / (1.0 - ADAM_B1 ** ADAM_STEP)
    v_hat = v / (1.0 - ADAM_B2 ** ADAM_STEP)
    delta = -ADAM_LR * (m_hat / (_jnp.sqrt(v_hat) + ADAM_EPS) + ADAM_WD * w)
    return delta, m, v


def reference(x, c, ada_w, ada_b, norm_mix_g, w_in, attn_sinks, w_attn_proj, ssm_a_re, ssm_a_im, ssm_log_dt, ssm_b_re, ssm_b_im, ssm_c_re, ssm_c_im, ssm_d, w_ssm_glu, w_out, norm_ffn_g, w_ffn_up, ffn_conv_w, ffn_conv_b, w_ffn_down, final_g, loss_target, m_ada_w, m_ada_b, m_norm_mix_g, m_w_in, m_attn_sinks, m_w_attn_proj, m_ssm_a_re, m_ssm_a_im, m_ssm_log_dt, m_ssm_b_re, m_ssm_b_im, m_ssm_c_re, m_ssm_c_im, m_ssm_d, m_w_ssm_glu, m_w_out, m_norm_ffn_g, m_w_ffn_up, m_ffn_conv_w, m_ffn_conv_b, m_w_ffn_down, m_final_g, v_ada_w, v_ada_b, v_norm_mix_g, v_w_in, v_attn_sinks, v_w_attn_proj, v_ssm_a_re, v_ssm_a_im, v_ssm_log_dt, v_ssm_b_re, v_ssm_b_im, v_ssm_c_re, v_ssm_c_im, v_ssm_d, v_w_ssm_glu, v_w_out, v_norm_ffn_g, v_w_ffn_up, v_ffn_conv_w, v_ffn_conv_b, v_w_ffn_down, v_final_g):
    given = dict(x=x, c=c, ada_w=ada_w, ada_b=ada_b, norm_mix_g=norm_mix_g, w_in=w_in, attn_sinks=attn_sinks, w_attn_proj=w_attn_proj, ssm_a_re=ssm_a_re, ssm_a_im=ssm_a_im, ssm_log_dt=ssm_log_dt, ssm_b_re=ssm_b_re, ssm_b_im=ssm_b_im, ssm_c_re=ssm_c_re, ssm_c_im=ssm_c_im, ssm_d=ssm_d, w_ssm_glu=w_ssm_glu, w_out=w_out, norm_ffn_g=norm_ffn_g, w_ffn_up=w_ffn_up, ffn_conv_w=ffn_conv_w, ffn_conv_b=ffn_conv_b, w_ffn_down=w_ffn_down, final_g=final_g, loss_target=loss_target, m_ada_w=m_ada_w, m_ada_b=m_ada_b, m_norm_mix_g=m_norm_mix_g, m_w_in=m_w_in, m_attn_sinks=m_attn_sinks, m_w_attn_proj=m_w_attn_proj, m_ssm_a_re=m_ssm_a_re, m_ssm_a_im=m_ssm_a_im, m_ssm_log_dt=m_ssm_log_dt, m_ssm_b_re=m_ssm_b_re, m_ssm_b_im=m_ssm_b_im, m_ssm_c_re=m_ssm_c_re, m_ssm_c_im=m_ssm_c_im, m_ssm_d=m_ssm_d, m_w_ssm_glu=m_w_ssm_glu, m_w_out=m_w_out, m_norm_ffn_g=m_norm_ffn_g, m_w_ffn_up=m_w_ffn_up, m_ffn_conv_w=m_ffn_conv_w, m_ffn_conv_b=m_ffn_conv_b, m_w_ffn_down=m_w_ffn_down, m_final_g=m_final_g, v_ada_w=v_ada_w, v_ada_b=v_ada_b, v_norm_mix_g=v_norm_mix_g, v_w_in=v_w_in, v_attn_sinks=v_attn_sinks, v_w_attn_proj=v_w_attn_proj, v_ssm_a_re=v_ssm_a_re, v_ssm_a_im=v_ssm_a_im, v_ssm_log_dt=v_ssm_log_dt, v_ssm_b_re=v_ssm_b_re, v_ssm_b_im=v_ssm_b_im, v_ssm_c_re=v_ssm_c_re, v_ssm_c_im=v_ssm_c_im, v_ssm_d=v_ssm_d, v_w_ssm_glu=v_w_ssm_glu, v_w_out=v_w_out, v_norm_ffn_g=v_norm_ffn_g, v_w_ffn_up=v_w_ffn_up, v_ffn_conv_w=v_ffn_conv_w, v_ffn_conv_b=v_ffn_conv_b, v_w_ffn_down=v_w_ffn_down, v_final_g=v_final_g)
    weights = {n: given[n] for n in TWIN_WEIGHTS}
    shared = {n: given[n] for n in SHARED_INPUTS}
    per_example = {n: given[n] for n in ['x', 'c']}
    grad_fn = _jax.value_and_grad(_loss, argnums=(0, 1))

    def one_microbatch(ex, loss_target):
        ex = dict(ex)
        diff = ex.pop(TWIN_DIFF_INPUT)
        return grad_fn(weights, diff, {**shared, **ex}, loss_target)

    if N_MICROBATCH == 1:
        loss, (grad_w, grad_x) = one_microbatch(per_example, given["loss_target"])
    else:
        def body(carry, xs):
            loss_sum, grad_sum = carry
            l_k, (gw_k, gx_k) = one_microbatch(xs[0], xs[1])
            with _jax.named_scope("update"):
                return (loss_sum + l_k, _jax.tree.map(_jnp.add, grad_sum, gw_k)), gx_k

        init = (_jnp.zeros((), _jnp.float32), _jax.tree.map(_jnp.zeros_like, weights))
        (loss, grad_w), grad_x = _jax.lax.scan(body, init, (per_example, given["loss_target"]))
    with _jax.named_scope("update"):
        delta_w, new_m, new_v = {}, {}, {}
        for n in TWIN_WEIGHTS:
            delta_w[n], new_m[n], new_v[n] = _adamw(weights[n], grad_w[n], given["m_" + n], given["v_" + n])
    return (loss, grad_x, *[grad_w[n] for n in TWIN_WEIGHTS], *[delta_w[n] for n in TWIN_WEIGHTS],
            *[new_m[n] for n in TWIN_WEIGHTS], *[new_v[n] for n in TWIN_WEIGHTS])
```

```python
import functools

import jax
import jax.numpy as jnp
from jax import lax
from jax.experimental import pallas as pl
from jax.experimental.pallas import tpu as pltpu

F32, BF16 = jnp.float32, jnp.bfloat16
MESH = pl.DeviceIdType.MESH
N_DEV = 8

HEAD_DIM = 64
N_KV_HEADS = 2
ATT_BLOCK = 128
NEG_INF = -1e30
SSM_P = 16
SSM_N = 64
LANES = 128
TILE_GROUPS = LANES // SSM_P
TILE_STATES = TILE_GROUPS * SSM_N
RMS_EPS = 1e-6
ADAM_LR, ADAM_B1, ADAM_B2, ADAM_EPS, ADAM_WD, ADAM_STEP = 0.001, 0.9, 0.999, 1e-08, 0.01, 10
VMEM_LIMIT = 56 * 1024 * 1024
PACK_W = 1024


def _params(n_axes):
    return pltpu.CompilerParams(dimension_semantics=("arbitrary",) * n_axes, vmem_limit_bytes=VMEM_LIMIT)


def _pick(dim, pref, align=128):
    if dim <= align:
        return dim
    t = (min(pref, dim) // align) * align
    while t > align and dim % t:
        t -= align
    assert dim % t == 0, (dim, pref, align)
    return t


def _dev():
    return lax.axis_index("x"), lax.axis_index("y"), lax.axis_index("c")


def _tile_call(name, fn, grid, ins, in_specs, out_shapes, out_specs, acc=()):
    n_in, n_out = len(ins), len(out_shapes)
    acc_axis = len(grid) - 1

    def body(*refs):
        vals = fn(*[r[...] for r in refs[:n_in]])
        if not isinstance(vals, (tuple, list)):
            vals = (vals,)
        assert len(vals) == n_out
        for i, (r, v) in enumerate(zip(refs[n_in:], vals)):
            v = v.astype(r.dtype)
            if i in acc:
                first = pl.program_id(acc_axis) == 0

                @pl.when(first)
                def _():
                    r[...] = v

                @pl.when(jnp.logical_not(first))
                def _():
                    r[...] += v
            else:
                r[...] = v

    return pl.pallas_call(
        body, grid=grid, in_specs=in_specs, out_specs=out_specs, out_shape=out_shapes, name=name,
        compiler_params=_params(len(grid)),
    )(*ins)


def _t(tr, tc, off=0):
    return pl.BlockSpec((tr, tc), lambda j, i: (i, j + off // tc))


def _v(tc, off=0, rows=1):
    return pl.BlockSpec((rows, tc), lambda j, i: (0, j + off // tc))


def _prev8(tr, tc, off=0):
    return pl.BlockSpec((8, tc), lambda j, i: (jnp.maximum(i * (tr // 8) - 1, 0), j + off // tc))


def _next8(tr, tc, nrows, off=0):
    return pl.BlockSpec((8, tc), lambda j, i: (jnp.minimum((i + 1) * (tr // 8), nrows // 8 - 1), j + off // tc))


def _st(tr, tc):
    return pl.BlockSpec((2, tr, tc), lambda j, i: (0, i, j))


def _bc(v, rows):
    return jnp.broadcast_to(v, (rows, v.shape[-1]))


def _colsum(v):
    return jnp.sum(v, axis=0, keepdims=True)


def _matmul(name, a, b, mode, out_dtype=F32, tm=1024, tn=1024, tk=512):
    def dims(z):
        return (z.shape[-2], z.shape[-1] * (z.shape[0] if z.ndim == 3 else 1))

    ar, ac = dims(a)
    br, bc = dims(b)
    if mode == "nn":
        m, k, n = ar, ac, bc
        assert br == k
    elif mode == "nt":
        m, k, n = ar, ac, br
        assert bc == k
    else:
        m, k, n = ac, ar, bc
        assert br == k
    a_cw = a.shape[-1]
    b_cw = b.shape[-1]
    tm = _pick(m, tm) if mode != "tn" else _pick(m, tm, 128)
    tn = _pick(n, tn)
    tk = _pick(k, tk)
    if mode == "tn" and a.ndim == 3:
        tm = _pick(a_cw, tm)
    if mode in ("nn", "nt") and a.ndim == 3:
        tk = _pick(a_cw, tk)
    if mode in ("nn", "tn") and b.ndim == 3:
        tn = _pick(b_cw, tn)
    nk = k // tk

    def spec(z, brows, bcols, ridx, cidx):
        if z.ndim == 3:
            per = z.shape[-1] // bcols
            return pl.BlockSpec((None, brows, bcols),
                                lambda i, j, kk: (cidx(i, j, kk) // per, ridx(i, j, kk), cidx(i, j, kk) % per))
        return pl.BlockSpec((brows, bcols), lambda i, j, kk: (ridx(i, j, kk), cidx(i, j, kk)))

    gi = lambda i, j, kk: i
    gj = lambda i, j, kk: j
    gk = lambda i, j, kk: kk
    if mode == "nn":
        a_spec, b_spec = spec(a, tm, tk, gi, gk), spec(b, tk, tn, gk, gj)
        dn = (((1,), (0,)), ((), ()))
    elif mode == "nt":
        a_spec, b_spec = spec(a, tm, tk, gi, gk), spec(b, tn, tk, gj, gk)
        dn = (((1,), (1,)), ((), ()))
    else:
        a_spec, b_spec = spec(a, tk, tm, gk, gi), spec(b, tk, tn, gk, gj)
        dn = (((0,), (0,)), ((), ()))

    def body(a_ref, b_ref, o_ref, acc_ref):
        kk = pl.program_id(2)

        @pl.when(kk == 0)
        def _():
            acc_ref[...] = jnp.zeros_like(acc_ref)

        acc_ref[...] += lax.dot_general(a_ref[...].astype(BF16), b_ref[...].astype(BF16), dn,
                                        preferred_element_type=F32)

        @pl.when(kk == nk - 1)
        def _():
            o_ref[...] = acc_ref[...].astype(o_ref.dtype)

    return pl.pallas_call(
        body, grid=(m // tm, n // tn, nk), in_specs=[a_spec, b_spec],
        out_specs=pl.BlockSpec((tm, tn), lambda i, j, kk: (i, j)),
        out_shape=jax.ShapeDtypeStruct((m, n), out_dtype),
        scratch_shapes=[pltpu.VMEM((tm, tn), F32)], name=name, compiler_params=_params(3),
    )(a, b)


def _all_gather(name, arrs):
    n = len(arrs)

    def body(*refs):
        ins, outs = refs[:n], refs[n:2 * n]
        send_sems, recv_sems, local_sems = refs[2 * n:]
        x, y, c = _dev()
        me, sib = (x, y, c), (x, y, 1 - c)
        chips = [(1 - x, y), (x, 1 - y), (1 - x, 1 - y)]

        def slot(p):
            return 4 * p[0] + 2 * p[1] + p[2]

        def copy(a, k, block, to, src=None):
            dst = outs[a].at[slot(block)]
            return pltpu.make_async_remote_copy(
                src_ref=dst if src is None else src, dst_ref=dst,
                send_sem=send_sems.at[7 * a + k], recv_sem=recv_sems.at[7 * a + k],
                device_id=to, device_id_type=MESH)

        mine = [pltpu.make_async_copy(ins[a], outs[a].at[slot(me)], local_sems.at[a]) for a in range(n)]
        for cp in mine:
            cp.start()
        first = []
        for a in range(n):
            first.append(copy(a, 0, me, sib, src=ins[a]))
            first += [copy(a, 1 + j, me, (*chip, c), src=ins[a]) for j, chip in enumerate(chips)]
        for cp in first:
            cp.start()
        passed = []
        for j, chip in enumerate(chips):
            for a in range(n):
                copy(a, 1 + j, (*chip, c), me).wait_recv()
                cp = copy(a, 4 + j, (*chip, c), sib)
                cp.start()
                passed.append(cp)
        for a in range(n):
            copy(a, 0, sib, me).wait_recv()
            for j, chip in enumerate(chips):
                copy(a, 4 + j, (*chip, 1 - c), me).wait_recv()
        for cp in first + passed:
            cp.wait_send()
        for cp in mine:
            cp.wait()

    any_spec = pl.BlockSpec(memory_space=pl.ANY)
    return pl.pallas_call(
        body, in_specs=[any_spec] * n, out_specs=[any_spec] * n,
        out_shape=[jax.ShapeDtypeStruct((N_DEV,) + a.shape, a.dtype) for a in arrs],
        scratch_shapes=[pltpu.SemaphoreType.DMA((7 * n,)), pltpu.SemaphoreType.DMA((7 * n,)),
                        pltpu.SemaphoreType.DMA((n,))],
        name=name,
    )(*arrs)


def _grad_to_sibling(gd):
    _, r, w = gd.shape

    def body(g_ref, r_ref, send_sems, recv_sems):
        x, y, c = _dev()
        cps = []
        for k in range(4):
            cp = pltpu.make_async_remote_copy(
                src_ref=g_ref.at[2 * k + (1 - c)], dst_ref=r_ref.at[k],
                send_sem=send_sems.at[k], recv_sem=recv_sems.at[k],
                device_id=(x, y, 1 - c), device_id_type=MESH)
            cp.start()
            cps.append(cp)
        for cp in cps:
            cp.wait()

    any_spec = pl.BlockSpec(memory_space=pl.ANY)
    return pl.pallas_call(
        body, in_specs=[any_spec], out_specs=any_spec,
        out_shape=jax.ShapeDtypeStruct((4, r, w), gd.dtype),
        scratch_shapes=[pltpu.SemaphoreType.DMA((4,)), pltpu.SemaphoreType.DMA((4,))],
        name="grad_to_sibling",
    )(gd)


def _grad_to_chips(s):
    _, r, w = s.shape

    def body(s_ref, p_ref, send_sems, recv_sems, local_sem):
        x, y, c = _dev()
        my_chip = 2 * x + y
        local = pltpu.make_async_copy(s_ref.at[my_chip], p_ref.at[my_chip], local_sem)
        local.start()
        cps = []
        for j, (px, py) in enumerate([(1 - x, y), (x, 1 - y), (1 - x, 1 - y)]):
            cp = pltpu.make_async_remote_copy(
                src_ref=s_ref.at[2 * px + py], dst_ref=p_ref.at[my_chip],
                send_sem=send_sems.at[j], recv_sem=recv_sems.at[j],
                device_id=(px, py, c), device_id_type=MESH)
            cp.start()
            cps.append(cp)
        for cp in cps:
            cp.wait()
        local.wait()

    any_spec = pl.BlockSpec(memory_space=pl.ANY)
    return pl.pallas_call(
        body, in_specs=[any_spec], out_specs=any_spec,
        out_shape=jax.ShapeDtypeStruct((4, r, w), s.dtype),
        scratch_shapes=[pltpu.SemaphoreType.DMA((3,)), pltpu.SemaphoreType.DMA((3,)), pltpu.SemaphoreType.DMA],
        name="grad_to_chips",
    )(s)


def _pack_rows(sizes, width, row_align):
    offs, r = [], 0
    for s in sizes:
        offs.append(r)
        r += -(-s // width)
    total = -(-r // row_align) * row_align
    return offs, total


def _pack(items, width, row_align, lead=()):
    nl = len(lead)
    sizes = [int(jnp.size(a)) // max(1, functools.reduce(lambda p, q: p * q, lead, 1)) for a in items]
    offs, total = _pack_rows(sizes, width, row_align)
    flat = []
    used = 0
    for a, s in zip(items, sizes):
        f = a.reshape(lead + (s,))
        pad = -(-s // width) * width - s
        if pad:
            f = jnp.pad(f, [(0, 0)] * nl + [(0, pad)])
        flat.append(f)
        used += s + pad
    tail = total * width - used
    if tail:
        flat.append(jnp.zeros(lead + (tail,), items[0].dtype))
    return jnp.concatenate(flat, axis=-1).reshape(lead + (total, width)), offs


def _unpack(packed, off, shape, lead=()):
    nl = len(lead)
    size = functools.reduce(lambda p, q: p * q, shape, 1)
    width = packed.shape[-1]
    rows = -(-size // width)
    blk = lax.slice_in_dim(packed, off, off + rows, axis=nl).reshape(lead + (rows * width,))
    return lax.slice_in_dim(blk, 0, size, axis=nl).reshape(lead + tuple(shape))


def _rms(x, g):
    return (x * lax.rsqrt(jnp.mean(x * x, axis=-1, keepdims=True) + RMS_EPS)) * g


def _norm_mod(x, g, sc, sh):
    return _rms(x, g) * (1.0 + sc) + sh


def _mix_fn(glu_a, glu_b, attn, ga, gs):
    return jax.nn.sigmoid(ga) * attn + jax.nn.sigmoid(gs) * (glu_a * jax.nn.sigmoid(glu_b))


def _s5_disc_fn(a_re, a_im, log_dt, b_re, b_im):
    dt = jnp.exp(log_dt)
    mag = jnp.exp(a_re * dt)
    lr, li = mag * jnp.cos(a_im * dt), mag * jnp.sin(a_im * dt)
    den = a_re * a_re + a_im * a_im
    zr = ((lr - 1.0) * a_re + li * a_im) / den
    zi = (li * a_re - (lr - 1.0) * a_im) / den
    return lr, li, zr[None] * b_re - zi[None] * b_im, zr[None] * b_im + zi[None] * b_re


def _adamw_fn(w, g, m, v):
    m = ADAM_B1 * m + (1.0 - ADAM_B1) * g
    v = ADAM_B2 * v + (1.0 - ADAM_B2) * jnp.square(g)
    m_hat = m / (1.0 - ADAM_B1 ** ADAM_STEP)
    v_hat = v / (1.0 - ADAM_B2 ** ADAM_STEP)
    delta = -ADAM_LR * (m_hat / (jnp.sqrt(v_hat) + ADAM_EPS) + ADAM_WD * w)
    return delta, m, v


def _adamw(name, parts, w, m, v):
    p, r, c = parts.shape
    tr = _pick(r, max(8, (1 << 21) // (4 * c * max(p, 2))), 8)

    def fn(pv, wv, mv, vv):
        g = pv[0]
        for i in range(1, p):
            g = g + pv[i]
        d, m2, v2 = _adamw_fn(wv, g, mv, vv)
        return g, d, m2, v2

    spec = pl.BlockSpec((tr, c), lambda i: (i, 0))
    return _tile_call(
        name, fn, (r // tr,), [parts, w, m, v],
        [pl.BlockSpec((p, tr, c), lambda i: (0, i, 0)), spec, spec, spec],
        [jax.ShapeDtypeStruct((r, c), F32)] * 4, [spec] * 4)


def _attn_mask(n):
    qi = lax.broadcasted_iota(jnp.int32, (ATT_BLOCK, 2 * ATT_BLOCK), 0)
    kj = lax.broadcasted_iota(jnp.int32, (ATT_BLOCK, 2 * ATT_BLOCK), 1)
    rel = qi + ATT_BLOCK - kj
    return (rel >= 0) & (rel < ATT_BLOCK) & ((kj >= ATT_BLOCK) | (n > 0))


def _attn_probs(q, k, sink, mask):
    s = lax.dot_general(q, k, (((1,), (1,)), ((), ())), preferred_element_type=F32) * (HEAD_DIM ** -0.5)
    s = jnp.where(mask, s, NEG_INF)
    m = jnp.maximum(jnp.max(s, axis=-1, keepdims=True), sink)
    p = jnp.exp(s - m)
    e_sink = jnp.exp(sink - m)
    den = jnp.sum(p, axis=-1, keepdims=True) + e_sink
    return p / den, e_sink / den


def _attn_specs(qpk):
    blk = ATT_BLOCK
    q_spec = pl.BlockSpec((qpk, blk, HEAD_DIM), lambda h, n: (h, n, 0))
    cur = pl.BlockSpec((1, blk, HEAD_DIM), lambda h, n: (h, n, 0))
    prev = pl.BlockSpec((1, blk, HEAD_DIM), lambda h, n: (h, jnp.maximum(n - 1, 0), 0))
    sink_spec = pl.BlockSpec((qpk, 1, 1), lambda h, n: (h, 0, 0))
    return q_spec, cur, prev, sink_spec


def _attn_fwd(q, k, v, sinks):
    hq, l, _ = q.shape
    qpk = hq // N_KV_HEADS
    nb = l // ATT_BLOCK
    q_spec, cur, prev, sink_spec = _attn_specs(qpk)

    def body(q_ref, kp_ref, kc_ref, vp_ref, vc_ref, sink_ref, o_ref):
        mask = _attn_mask(pl.program_id(1))
        kk = jnp.concatenate([kp_ref[0], kc_ref[0]], axis=0).astype(BF16)
        vv = jnp.concatenate([vp_ref[0], vc_ref[0]], axis=0).astype(BF16)
        for g in range(qpk):
            p, _ = _attn_probs(q_ref[g].astype(BF16), kk, sink_ref[g], mask)
            o_ref[g] = jnp.dot(p.astype(BF16), vv, preferred_element_type=F32).astype(o_ref.dtype)

    return pl.pallas_call(
        body, grid=(N_KV_HEADS, nb), in_specs=[q_spec, prev, cur, prev, cur, sink_spec],
        out_specs=q_spec, out_shape=jax.ShapeDtypeStruct((hq, l, HEAD_DIM), BF16),
        name="attn_fwd", compiler_params=_params(2),
    )(q, k, k, v, v, sinks)


def _attn_bwd(q, k, v, sinks, do):
    hq, l, _ = q.shape
    qpk = hq // N_KV_HEADS
    nb = l // ATT_BLOCK
    blk = ATT_BLOCK
    q_spec, cur, prev, sink_spec = _attn_specs(qpk)
    part_spec = pl.BlockSpec((1, 1, 2 * blk, HEAD_DIM), lambda h, n: (h, n, 0, 0))
    dsink_spec = pl.BlockSpec((qpk, 1, LANES), lambda h, n: (h, 0, 0))
    tn = (((0,), (0,)), ((), ()))

    def body(q_ref, do_ref, kp_ref, kc_ref, vp_ref, vc_ref, sink_ref, dq_ref, dkp_ref, dvp_ref, dsink_ref):
        n = pl.program_id(1)
        mask = _attn_mask(n)
        kk = jnp.concatenate([kp_ref[0], kc_ref[0]], axis=0).astype(BF16)
        vv = jnp.concatenate([vp_ref[0], vc_ref[0]], axis=0).astype(BF16)
        dk = jnp.zeros((2 * blk, HEAD_DIM), F32)
        dv = jnp.zeros((2 * blk, HEAD_DIM), F32)
        for g in range(qpk):
            qg = q_ref[g].astype(BF16)
            dog = do_ref[g].astype(BF16)
            p, p_sink = _attn_probs(qg, kk, sink_ref[g], mask)
            pb = p.astype(BF16)
            o = jnp.dot(pb, vv, preferred_element_type=F32)
            delta = jnp.sum(do_ref[g].astype(F32) * o, axis=-1, keepdims=True)
            dp = lax.dot_general(dog, vv, (((1,), (1,)), ((), ())), preferred_element_type=F32)
            ds = (p * (dp - delta) * (HEAD_DIM ** -0.5)).astype(BF16)
            dq_ref[g] = jnp.dot(ds, kk, preferred_element_type=F32).astype(dq_ref.dtype)
            dk = dk + lax.dot_general(ds, qg, tn, preferred_element_type=F32)
            dv = dv + lax.dot_general(pb, dog, tn, preferred_element_type=F32)
            dsg = jnp.broadcast_to(-jnp.sum(p_sink * delta, axis=0, keepdims=True), (1, LANES))

            @pl.when(n == 0)
            def _():
                dsink_ref[g] = dsg

            @pl.when(n > 0)
            def _():
                dsink_ref[g] += dsg

        dkp_ref[0, 0] = dk
        dvp_ref[0, 0] = dv

    part_shape = jax.ShapeDtypeStruct((N_KV_HEADS, nb, 2 * blk, HEAD_DIM), F32)
    dq, dkp, dvp, dsink = pl.pallas_call(
        body, grid=(N_KV_HEADS, nb), in_specs=[q_spec, q_spec, prev, cur, prev, cur, sink_spec],
        out_specs=[q_spec, part_spec, part_spec, dsink_spec],
        out_shape=[jax.ShapeDtypeStruct((hq, l, HEAD_DIM), BF16), part_shape, part_shape,
                   jax.ShapeDtypeStruct((hq, 1, LANES), F32)],
        name="attn_bwd", compiler_params=_params(2),
    )(q, do, k, k, v, v, sinks)

    def combine(a_cur, a_nxt, b_cur, b_nxt):
        last = pl.program_id(1) == nb - 1
        keep = jnp.where(last, 0.0, 1.0)
        return (a_cur[0, 0, blk:] + keep * a_nxt[0, 0, :blk])[None], (b_cur[0, 0, blk:] + keep * b_nxt[0, 0, :blk])[None]

    nxt_spec = pl.BlockSpec((1, 1, 2 * blk, HEAD_DIM), lambda h, n: (h, jnp.minimum(n + 1, nb - 1), 0, 0))
    kv_shape = jax.ShapeDtypeStruct((N_KV_HEADS, l, HEAD_DIM), BF16)
    dk, dv = _tile_call("attn_dkv", combine, (N_KV_HEADS, nb), [dkp, dkp, dvp, dvp],
                        [part_spec, nxt_spec, part_spec, nxt_spec], [kv_shape, kv_shape], [cur, cur])
    return dq, dk, dv, dsink


def _block_diag(m):
    j, gl, a, b = m.shape
    eye = jnp.eye(gl, dtype=m.dtype)
    return (m[:, :, :, None, :] * eye[None, :, None, :, None]).reshape(j, gl * a, gl * b)


def _diag_blocks(z, a):
    j = z.shape[0]
    gl = z.shape[1] // a
    b = z.shape[2] // gl
    d = jnp.diagonal(z.reshape(j, gl, a, gl, b), axis1=1, axis2=3)
    return d.transpose(0, 3, 1, 2)


def _s5_scan_fwd(x_ref, base, lr, li, hr, hi, t_len):
    hs = TILE_STATES

    def step(t, carry):
        hr, hi = carry
        row = pl.ds(base + t, 1)
        nr = lr * hr - li * hi + x_ref[row, 0:hs]
        ni = lr * hi + li * hr + x_ref[row, hs:2 * hs]
        x_ref[row, 0:hs] = nr
        x_ref[row, hs:2 * hs] = ni
        return nr, ni

    return lax.fori_loop(0, t_len, step, (hr, hi), unroll=8)


def _s5_fwd(proj, u_off, bd, cbd, lam, dvec, t_len):
    l = proj.shape[0]
    nj = bd.shape[0]
    nch = l // t_len
    hs = TILE_STATES
    ub = u_off // LANES

    def body(u_ref, bd_ref, cbd_ref, lam_ref, d_ref, y_ref, hst_ref, x_ref, h_ref):
        @pl.when(pl.program_id(1) == 0)
        def _():
            h_ref[...] = jnp.zeros_like(h_ref)

        hst_ref[0, 0] = h_ref[...]
        u = u_ref[...]
        x_ref[...] = jnp.dot(u.astype(BF16), bd_ref[0], preferred_element_type=F32)
        lr, li = lam_ref[0, 0:1, :], lam_ref[0, 1:2, :]
        hr, hi = _s5_scan_fwd(x_ref, 0, lr, li, h_ref[:, 0:hs], h_ref[:, hs:2 * hs], t_len)
        h_ref[:, 0:hs] = hr
        h_ref[:, hs:2 * hs] = hi
        y_ref[...] = jnp.dot(x_ref[...].astype(BF16), cbd_ref[0], preferred_element_type=F32) + d_ref[0] * u

    return pl.pallas_call(
        body, grid=(nj, nch),
        in_specs=[pl.BlockSpec((t_len, LANES), lambda j, c: (c, ub + j)),
                  pl.BlockSpec((1, LANES, 2 * hs), lambda j, c: (j, 0, 0)),
                  pl.BlockSpec((1, 2 * hs, LANES), lambda j, c: (j, 0, 0)),
                  pl.BlockSpec((1, 2, hs), lambda j, c: (j, 0, 0)),
                  pl.BlockSpec((1, 1, LANES), lambda j, c: (j, 0, 0))],
        out_specs=[pl.BlockSpec((t_len, LANES), lambda j, c: (c, j)),
                   pl.BlockSpec((1, 1, 1, 2 * hs), lambda j, c: (j, c, 0, 0))],
        out_shape=[jax.ShapeDtypeStruct((l, nj * LANES), F32),
                   jax.ShapeDtypeStruct((nj, nch, 1, 2 * hs), F32)],
        scratch_shapes=[pltpu.VMEM((t_len, 2 * hs), F32), pltpu.VMEM((1, 2 * hs), F32)],
        name="s5_fwd", compiler_params=_params(2),
    )(proj, bd, cbd, lam, dvec)


def _s5_bwd(proj, u_off, dy, hst, bd, bdt, cbdt, lam, dvec, t_len):
    l = proj.shape[0]
    nj = bd.shape[0]
    nch = l // t_len
    hs = TILE_STATES
    ub = u_off // LANES
    tn = (((0,), (0,)), ((), ()))

    def body(u_ref, dy_ref, hst_ref, bd_ref, bdt_ref, cbdt_ref, lam_ref, d_ref,
             du_ref, dbd_ref, dcbdt_ref, dlam_ref, dd_ref, x_ref, g_ref, gc_ref):
        first = pl.program_id(1) == 0

        @pl.when(first)
        def _():
            gc_ref[...] = jnp.zeros_like(gc_ref)

        lr, li = lam_ref[0, 0:1, :], lam_ref[0, 1:2, :]
        u = u_ref[...]
        dy = dy_ref[...]
        ub16, dyb16 = u.astype(BF16), dy.astype(BF16)
        x_ref[pl.ds(8, t_len), :] = jnp.dot(ub16, bd_ref[0], preferred_element_type=F32)
        h0 = hst_ref[0, 0]
        x_ref[7:8, :] = h0
        _s5_scan_fwd(x_ref, 8, lr, li, h0[:, 0:hs], h0[:, hs:2 * hs], t_len)
        g_ref[...] = jnp.dot(dyb16, cbdt_ref[0], preferred_element_type=F32)

        def step(s, carry):
            gr, gi, alr, ali = carry
            t = t_len - 1 - s
            row = pl.ds(t, 1)
            nr = g_ref[row, 0:hs] + lr * gr + li * gi
            ni = g_ref[row, hs:2 * hs] + lr * gi - li * gr
            g_ref[row, 0:hs] = nr
            g_ref[row, hs:2 * hs] = ni
            hpr = x_ref[pl.ds(7 + t, 1), 0:hs]
            hpi = x_ref[pl.ds(7 + t, 1), hs:2 * hs]
            return nr, ni, alr + nr * hpr + ni * hpi, ali + ni * hpr - nr * hpi

        zero = jnp.zeros((1, hs), F32)
        gr, gi, alr, ali = lax.fori_loop(0, t_len, step, (gc_ref[:, 0:hs], gc_ref[:, hs:2 * hs], zero, zero),
                                         unroll=8)
        gc_ref[:, 0:hs] = gr
        gc_ref[:, hs:2 * hs] = gi
        g = g_ref[...].astype(BF16)
        h = x_ref[pl.ds(8, t_len), :].astype(BF16)
        du_ref[...] = (jnp.dot(g, bdt_ref[0], preferred_element_type=F32) + d_ref[0] * dy).astype(du_ref.dtype)
        sign = jnp.where(lax.broadcasted_iota(jnp.int32, (1, 2 * hs), 1) < hs, 1.0, -1.0)
        dbd = lax.dot_general(ub16, g, tn, preferred_element_type=F32)
        dcbdt = lax.dot_general(dyb16, h, tn, preferred_element_type=F32) * sign
        ddv = _colsum(dy * u)

        @pl.when(first)
        def _():
            dbd_ref[0] = dbd
            dcbdt_ref[0] = dcbdt
            dlam_ref[0, 0:1, :] = alr
            dlam_ref[0, 1:2, :] = ali
            dd_ref[0] = ddv

        @pl.when(jnp.logical_not(first))
        def _():
            dbd_ref[0] += dbd
            dcbdt_ref[0] += dcbdt
            dlam_ref[0, 0:1, :] += alr
            dlam_ref[0, 1:2, :] += ali
            dd_ref[0] += ddv

    rev = lambda c: nch - 1 - c
    wide = pl.BlockSpec((1, LANES, 2 * hs), lambda j, c: (j, 0, 0))
    tall = pl.BlockSpec((1, 2 * hs, LANES), lambda j, c: (j, 0, 0))
    return pl.pallas_call(
        body, grid=(nj, nch),
        in_specs=[pl.BlockSpec((t_len, LANES), lambda j, c: (rev(c), ub + j)),
                  pl.BlockSpec((t_len, LANES), lambda j, c: (rev(c), j)),
                  pl.BlockSpec((1, 1, 1, 2 * hs), lambda j, c: (j, rev(c), 0, 0)),
                  wide, tall, wide,
                  pl.BlockSpec((1, 2, hs), lambda j, c: (j, 0, 0)),
                  pl.BlockSpec((1, 1, LANES), lambda j, c: (j, 0, 0))],
        out_specs=[pl.BlockSpec((t_len, LANES), lambda j, c: (rev(c), j)),
                   wide, wide,
                   pl.BlockSpec((1, 2, hs), lambda j, c: (j, 0, 0)),
                   pl.BlockSpec((1, 1, LANES), lambda j, c: (j, 0, 0))],
        out_shape=[jax.ShapeDtypeStruct((l, nj * LANES), BF16),
                   jax.ShapeDtypeStruct((nj, LANES, 2 * hs), F32),
                   jax.ShapeDtypeStruct((nj, LANES, 2 * hs), F32),
                   jax.ShapeDtypeStruct((nj, 2, hs), F32),
                   jax.ShapeDtypeStruct((nj, 1, LANES), F32)],
        scratch_shapes=[pltpu.VMEM((t_len + 8, 2 * hs), F32), pltpu.VMEM((t_len, 2 * hs), F32),
                        pltpu.VMEM((1, 2 * hs), F32)],
        name="s5_bwd", compiler_params=_params(2),
    )(proj, dy, hst, bd, bdt, cbdt, lam, dvec)


def _full_spec(shape):
    nd = len(shape)
    return pl.BlockSpec(tuple(shape), lambda i: (0,) * nd)


def _sds(shape, dtype=F32):
    return jax.ShapeDtypeStruct(tuple(shape), dtype)


def kernel(x, c, ada_w, ada_b, norm_mix_g, w_in, attn_sinks, w_attn_proj, ssm_a_re, ssm_a_im, ssm_log_dt, ssm_b_re, ssm_b_im, ssm_c_re, ssm_c_im, ssm_d, w_ssm_glu, w_out, norm_ffn_g, w_ffn_up, ffn_conv_w, ffn_conv_b, w_ffn_down, final_g, loss_target, m_ada_w, m_ada_b, m_norm_mix_g, m_w_in, m_attn_sinks, m_w_attn_proj, m_ssm_a_re, m_ssm_a_im, m_ssm_log_dt, m_ssm_b_re, m_ssm_b_im, m_ssm_c_re, m_ssm_c_im, m_ssm_d, m_w_ssm_glu, m_w_out, m_norm_ffn_g, m_w_ffn_up, m_ffn_conv_w, m_ffn_conv_b, m_w_ffn_down, m_final_g, v_ada_w, v_ada_b, v_norm_mix_g, v_w_in, v_attn_sinks, v_w_attn_proj, v_ssm_a_re, v_ssm_a_im, v_ssm_log_dt, v_ssm_b_re, v_ssm_b_im, v_ssm_c_re, v_ssm_c_im, v_ssm_d, v_w_ssm_glu, v_w_out, v_norm_ffn_g, v_w_ffn_up, v_ffn_conv_w, v_ffn_conv_b, v_w_ffn_down, v_final_g):
    given = dict(locals())
    names = ['ada_w', 'ada_b', 'norm_mix_g', 'w_in', 'attn_sinks', 'w_attn_proj', 'ssm_a_re', 'ssm_a_im',
             'ssm_log_dt', 'ssm_b_re', 'ssm_b_im', 'ssm_c_re', 'ssm_c_im', 'ssm_d', 'w_ssm_glu', 'w_out',
             'norm_ffn_g', 'w_ffn_up', 'ffn_conv_w', 'ffn_conv_b', 'w_ffn_down', 'final_g']

    xs = x[0]
    tgt = loss_target[0]
    l, d = xs.shape
    attn_w = w_attn_proj.shape[1]
    ssm_w = w_ssm_glu.shape[1]
    hq = attn_sinks.shape[1]
    qpk = hq // N_KV_HEADS
    kv_w = N_KV_HEADS * HEAD_DIM
    n_groups = ssm_a_re.shape[1]
    dff = ffn_conv_b.shape[1]
    in_w = attn_w + 2 * kv_w + ssm_w + 2 * d
    nj = ssm_w // LANES
    off_k, off_v, off_u = attn_w, attn_w + kv_w, attn_w + 2 * kv_w
    off_ga, off_gs = off_u + ssm_w, off_u + ssm_w + d
    assert hq * HEAD_DIM == attn_w and n_groups * SSM_P == ssm_w and l % ATT_BLOCK == 0

    xi, yi, ci = _dev()
    idx = 4 * xi + 2 * yi + ci

    col_sharded = {'w_in': (d, in_w), 'w_attn_proj': (attn_w, d), 'w_ssm_glu': (ssm_w, 2 * d), 'w_ffn_up': (d, 2 * dff)}
    row_sharded = {'w_out': (d, d), 'w_ffn_down': (dff, d)}
    big = ['w_in', 'w_attn_proj', 'w_ssm_glu', 'w_out', 'w_ffn_up', 'w_ffn_down']
    wpack, w_offs = _pack([given[k][0].astype(BF16) for k in big], PACK_W, 16)
    spack, s_offs = _pack([c, ffn_conv_w[0]], LANES, 8)
    wg, sg = _all_gather("gather_weights", [wpack, spack])
    full = {}
    for k, off in zip(big, w_offs):
        shard = given[k].shape[1:]
        st = _unpack(wg, off, shard, lead=(N_DEV,))
        if k in col_sharded:
            full[k] = st.transpose(1, 0, 2).reshape(col_sharded[k])
        else:
            full[k] = st.reshape(row_sharded[k])
    c_all = _unpack(sg, s_offs[0], (d,), lead=(N_DEV,))
    conv_w = _unpack(sg, s_offs[1], ffn_conv_w.shape[1:], lead=(N_DEV,)).transpose(1, 0, 2).reshape(3, dff)
    conv_b = ffn_conv_b

    mod_n = ada_w.shape[2]
    tcm = _pick(mod_n, 512)
    ada_b_mine = lax.dynamic_slice_in_dim(ada_b, idx * mod_n, mod_n, axis=1)

    def modpart_fn(cv, wv, bv):
        cond = cv * jax.nn.sigmoid(cv)
        return jnp.dot(cond.astype(BF16), wv.astype(BF16), preferred_element_type=F32) + bv, cond

    modp, cond_all = _tile_call(
        "ada_rows", modpart_fn, (mod_n // tcm,), [c_all, ada_w[0], ada_b_mine],
        [pl.BlockSpec((N_DEV, d), lambda j: (0, 0)), pl.BlockSpec((d, tcm), lambda j: (0, j)),
         pl.BlockSpec((1, tcm), lambda j: (0, j))],
        [_sds((N_DEV, mod_n)), _sds((N_DEV, d))],
        [pl.BlockSpec((N_DEV, tcm), lambda j: (0, j)), pl.BlockSpec((N_DEV, d), lambda j: (0, 0))])
    (modg,) = _all_gather("gather_ada_rows", [modp])
    mod = lax.dynamic_index_in_dim(modg, idx, axis=1, keepdims=False).reshape(1, N_DEV * mod_n)
    sh1, sc1, g1, sh2, sc2, g2 = [mod[:, i * d:(i + 1) * d] for i in range(6)]

    tr = _pick(l, 256, 8)
    trh = _pick(l, 128, 8)
    nr, nrh = l // tr, l // trh
    g_mix, g_ffn, g_fin = norm_mix_g, norm_ffn_g, final_g.reshape(1, d)

    h1 = _tile_call("norm_mod_mix", _norm_mod, (1, nr), [xs, g_mix, sc1, sh1],
                    [_t(tr, d), _v(d), _v(d), _v(d)], [_sds((l, d), BF16)], [_t(tr, d)])[0]
    proj = _matmul("proj_in", h1, full['w_in'], "nn", tn=1280)

    def heads(z, n):
        return z.reshape(l, n, HEAD_DIM).transpose(1, 0, 2)

    qh = heads(proj[:, :attn_w], hq)
    kh = heads(proj[:, off_k:off_k + kv_w], N_KV_HEADS)
    vh = heads(proj[:, off_v:off_v + kv_w], N_KV_HEADS)
    sinks3 = attn_sinks.reshape(hq, 1, 1)
    o_h = _attn_fwd(qh, kh, vh, sinks3)
    o2 = o_h.transpose(1, 0, 2).reshape(l, attn_w)
    attn = _matmul("attn_proj", o2, full['w_attn_proj'], "nn")

    gn = (n_groups, SSM_N)
    pgn = (SSM_P, n_groups, SSM_N)
    a_re, a_im, log_dt = ssm_a_re[0], ssm_a_im[0], ssm_log_dt[0].reshape(n_groups, 1)
    b_re, b_im = ssm_b_re[0].transpose(2, 0, 1), ssm_b_im[0].transpose(2, 0, 1)
    disc_ins = [a_re, a_im, log_dt, b_re, b_im]
    disc_specs = [_full_spec(gn), _full_spec(gn), _full_spec((n_groups, 1)), _full_spec(pgn), _full_spec(pgn)]
    lam_r, lam_i, bb_r, bb_i = _tile_call(
        "s5_discretise", _s5_disc_fn, (1,), disc_ins, disc_specs,
        [_sds(gn), _sds(gn), _sds(pgn), _sds(pgn)],
        [_full_spec(gn), _full_spec(gn), _full_spec(pgn), _full_spec(pgn)])

    def tiles_gpn(z):
        return z.reshape(SSM_P, nj, TILE_GROUPS, SSM_N).transpose(1, 2, 0, 3)

    bd = jnp.concatenate([_block_diag(tiles_gpn(bb_r)), _block_diag(tiles_gpn(bb_i))], axis=2).astype(BF16)
    c_r = ssm_c_re[0].reshape(nj, TILE_GROUPS, SSM_P, SSM_N).transpose(0, 1, 3, 2)
    c_i = (-ssm_c_im[0]).reshape(nj, TILE_GROUPS, SSM_P, SSM_N).transpose(0, 1, 3, 2)
    cbd = jnp.concatenate([_block_diag(c_r), _block_diag(c_i)], axis=1).astype(BF16)
    bdt, cbdt = bd.transpose(0, 2, 1), cbd.transpose(0, 2, 1)
    lam = jnp.stack([lam_r.reshape(nj, TILE_STATES), lam_i.reshape(nj, TILE_STATES)], axis=1)
    dvec = ssm_d[0].reshape(nj, 1, LANES)
    t_len = _pick(l, 512, 8)
    y, hst = _s5_fwd(proj, off_u, bd, cbd, lam, dvec, t_len)

    tcs = _pick(ssm_w, 512)
    gy = _tile_call("gelu", lambda v: jax.nn.gelu(v), (ssm_w // tcs, nr), [y], [_t(tr, tcs)],
                    [_sds((l, ssm_w), BF16)], [_t(tr, tcs)])[0]
    glu = _matmul("ssm_glu", gy, full['w_ssm_glu'], "nn")

    tcd = 256 if d % 256 == 0 and off_ga % 256 == 0 else LANES
    assert d % tcd == 0 and off_ga % tcd == 0 and off_gs % tcd == 0
    mix_in_specs = [_t(tr, tcd), _t(tr, tcd, d), _t(tr, tcd), _t(tr, tcd, off_ga), _t(tr, tcd, off_gs)]
    mixed = _tile_call("gate_mix", _mix_fn, (d // tcd, nr), [glu, glu, attn, proj, proj], mix_in_specs,
                       [_sds((l, d), BF16)], [_t(tr, tcd)])[0]
    mixout = _matmul("mix_out", mixed, full['w_out'], "nn")

    def res_norm_fn(xv, mo, g1v, gv, scv, shv):
        x2v = xv + g1v * mo
        return x2v, _norm_mod(x2v, gv, scv, shv)

    x2, h2 = _tile_call("residual_norm_mod_ffn", res_norm_fn, (1, nr), [xs, mixout, g1, g_ffn, sc2, sh2],
                        [_t(tr, d), _t(tr, d), _v(d), _v(d), _v(d), _v(d)],
                        [_sds((l, d)), _sds((l, d), BF16)], [_t(tr, d), _t(tr, d)])
    up = _matmul("ffn_up", h2, full['w_ffn_up'], "nn")

    tcf = _pick(dff, 512)
    assert dff % tcf == 0
    ncf = dff // tcf

    taps = [conv_w[i:i + 1] for i in range(3)]

    def conv_gate(gp, gp_prev, w0, w1, w2, bv):
        prev = jnp.where(pl.program_id(1) == 0, 0.0, 1.0) * gp_prev
        ext = jnp.concatenate([prev, gp], axis=0)
        m1 = pltpu.roll(ext, 1, 0)[8:]
        m2 = pltpu.roll(ext, 2, 0)[8:]
        return w0 * m2 + w1 * m1 + w2 * gp + bv, m1, m2

    def convglu_fn(gp, gp_prev, val, w0, w1, w2, bv):
        gate, _, _ = conv_gate(gp, gp_prev, w0, w1, w2, bv)
        return gate * jax.nn.sigmoid(gate) * val

    act = _tile_call("conv_swiglu", convglu_fn, (ncf, nr), [up, up, up] + taps + [conv_b],
                     [_t(tr, tcf), _prev8(tr, tcf), _t(tr, tcf, dff)] + [_v(tcf)] * 4,
                     [_sds((l, dff), BF16)], [_t(tr, tcf)])[0]
    ffn = _matmul("ffn_down", act, full['w_ffn_down'], "nn")

    def final_fn(x2v, fv, g2v, gv, tv):
        rows = x2v.shape[0]

        def loss_of(x2a, fa, g2a, ga):
            out = _rms(x2a + g2a * fa, ga)
            err = out - tv
            return 0.5 * _colsum(jnp.mean(err * err, axis=-1, keepdims=True))

        loss, vjp = jax.vjp(loss_of, x2v, fv, _bc(g2v, rows), _bc(gv, rows))
        dx3, dffn, dg2, dgf = vjp(jnp.ones((1, 1), F32))
        return jnp.broadcast_to(loss, (1, LANES)), dx3, dffn, _colsum(dg2), _colsum(dgf)

    loss_p, dx3, dffn, dg2, dg_fin = _tile_call(
        "loss_final_norm", final_fn, (1, nrh), [x2, ffn, g2, g_fin, tgt],
        [_t(trh, d), _t(trh, d), _v(d), _v(d), _t(trh, d)],
        [_sds((1, LANES)), _sds((l, d)), _sds((l, d), BF16), _sds((1, d)), _sds((1, d))],
        [_v(LANES), _t(trh, d), _t(trh, d), _v(d), _v(d)], acc=(0, 3, 4))
    loss = lax.psum(loss_p[0, 0], ("x", "y", "c"))

    dact = _matmul("d_act", dffn, full['w_ffn_down'], "nt", tn=1408)
    dw_down = _matmul("dw_ffn_down", act, dffn, "tn", tm=1408)

    def convglu_bwd_fn(gp, gp_prev, val, da, w0, w1, w2, bv):
        gate, m1, m2 = conv_gate(gp, gp_prev, w0, w1, w2, bv)
        sg = jax.nn.sigmoid(gate)
        dgate = da * val * (sg * (1.0 + gate * (1.0 - sg)))
        return (dgate, da * (gate * sg), _colsum(dgate), _colsum(dgate * m2), _colsum(dgate * m1),
                _colsum(dgate * gp))

    dgate, dval, dconv_b, dcw0, dcw1, dcw2 = _tile_call(
        "conv_swiglu_bwd", convglu_bwd_fn, (ncf, nr), [up, up, up, dact] + taps + [conv_b],
        [_t(tr, tcf), _prev8(tr, tcf), _t(tr, tcf, dff), _t(tr, tcf)] + [_v(tcf)] * 4,
        [_sds((l, dff)), _sds((l, dff), BF16)] + [_sds((1, dff))] * 4,
        [_t(tr, tcf), _t(tr, tcf)] + [_v(tcf)] * 4, acc=(2, 3, 4, 5))

    def conv_t_fn(dg, dg_next, dv, w0, w1, w2):
        rows = dg.shape[0]
        nxt = jnp.where(pl.program_id(1) == pl.num_programs(1) - 1, 0.0, 1.0) * dg_next
        ext = jnp.concatenate([dg, nxt], axis=0)
        p1 = pltpu.roll(ext, rows + 8 - 1, 0)[:rows]
        p2 = pltpu.roll(ext, rows + 8 - 2, 0)[:rows]
        return jnp.stack([w2 * dg + w1 * p1 + w0 * p2, dv.astype(F32)], axis=0)

    dup = _tile_call("conv_transpose", conv_t_fn, (ncf, nr), [dgate, dgate, dval] + taps,
                     [_t(tr, tcf), _next8(tr, tcf, l), _t(tr, tcf)] + [_v(tcf)] * 3,
                     [_sds((2, l, dff), BF16)], [_st(tr, tcf)])[0]
    dh2 = _matmul("d_h2", dup, full['w_ffn_up'], "nt")
    dw_up = _matmul("dw_ffn_up", h2, dup, "tn", tn=1408)

    def res_norm_bwd_fn(xv, mo, g1v, gv, scv, shv, dhv, dxv):
        rows = xv.shape[0]
        _, vjp = jax.vjp(res_norm_fn, xv, mo, _bc(g1v, rows), _bc(gv, rows), _bc(scv, rows), _bc(shv, rows))
        dx, dmo, dg1v, dgv, dscv, dshv = vjp((dxv, dhv))
        return dx, dmo, _colsum(dg1v), _colsum(dgv), _colsum(dscv), _colsum(dshv)

    dx2, dmixout, dg1, dg_ffn, dsc2, dsh2 = _tile_call(
        "residual_norm_mod_ffn_bwd", res_norm_bwd_fn, (1, nrh), [xs, mixout, g1, g_ffn, sc2, sh2, dh2, dx3],
        [_t(trh, d), _t(trh, d), _v(d), _v(d), _v(d), _v(d), _t(trh, d), _t(trh, d)],
        [_sds((l, d)), _sds((l, d), BF16)] + [_sds((1, d))] * 4,
        [_t(trh, d), _t(trh, d)] + [_v(d)] * 4, acc=(2, 3, 4, 5))

    dmixed = _matmul("d_mixed", dmixout, full['w_out'], "nt")
    dw_out = _matmul("dw_out", mixed, dmixout, "tn")

    def mix_bwd_fn(ga_, gb_, at, pa, ps, dm):
        _, vjp = jax.vjp(_mix_fn, ga_, gb_, at, pa, ps)
        da, db, dat, dpa, dps = vjp(dm)
        return jnp.stack([da, db], axis=0), dat, dpa, dps

    dglu, dattn, dga, dgs = _tile_call(
        "gate_mix_bwd", mix_bwd_fn, (d // tcd, nr), [glu, glu, attn, proj, proj, dmixed],
        mix_in_specs + [_t(tr, tcd)],
        [_sds((2, l, d), BF16)] + [_sds((l, d), BF16)] * 3, [_st(tr, tcd)] + [_t(tr, tcd)] * 3)

    dgy = _matmul("d_gelu_y", dglu, full['w_ssm_glu'], "nt")
    dw_glu = _matmul("dw_ssm_glu", gy, dglu, "tn")

    def gelu_bwd_fn(yv, dv):
        _, vjp = jax.vjp(lambda z: jax.nn.gelu(z), yv)
        return vjp(dv)[0]

    dy = _tile_call("gelu_bwd", gelu_bwd_fn, (ssm_w // tcs, nr), [y, dgy], [_t(tr, tcs), _t(tr, tcs)],
                    [_sds((l, ssm_w))], [_t(tr, tcs)])[0]
    du, dbd, dcbdt, dlam, dd_tiles = _s5_bwd(proj, off_u, dy, hst, bd, bdt, cbdt, lam, dvec, t_len)

    def gpn_of(z):
        return z.transpose(2, 0, 1, 3).reshape(pgn)

    dbb_r = gpn_of(_diag_blocks(dbd[:, :, :TILE_STATES], SSM_P))
    dbb_i = gpn_of(_diag_blocks(dbd[:, :, TILE_STATES:], SSM_P))
    dc_re = _diag_blocks(dcbdt[:, :, :TILE_STATES], SSM_P).reshape(n_groups, SSM_P, SSM_N)
    dc_im = _diag_blocks(dcbdt[:, :, TILE_STATES:], SSM_P).reshape(n_groups, SSM_P, SSM_N)
    dlam_r, dlam_i = dlam[:, 0].reshape(gn), dlam[:, 1].reshape(gn)

    def disc_bwd_fn(ar, ai, ld, br, bi, dlr, dli, dbr, dbi):
        _, vjp = jax.vjp(_s5_disc_fn, ar, ai, ld, br, bi)
        return vjp((dlr, dli, dbr, dbi))

    da_re, da_im, dlog_dt, db_re, db_im = _tile_call(
        "s5_discretise_bwd", disc_bwd_fn, (1,), disc_ins + [dlam_r, dlam_i, dbb_r, dbb_i],
        disc_specs + [_full_spec(gn), _full_spec(gn), _full_spec(pgn), _full_spec(pgn)],
        [_sds(gn), _sds(gn), _sds((n_groups, 1)), _sds(pgn), _sds(pgn)], disc_specs)

    do2 = _matmul("d_attn_heads", dattn, full['w_attn_proj'], "nt")
    dw_ap = _matmul("dw_attn_proj", o2, dattn, "tn")
    do_h = heads(do2.astype(BF16), hq)
    dq_h, dk_h, dv_h, dsink = _attn_bwd(qh, kh, vh, sinks3, do_h)

    def unheads(z):
        return z.transpose(1, 0, 2).reshape(l, z.shape[0] * HEAD_DIM)

    dproj = jnp.concatenate([unheads(dq_h), unheads(dk_h), unheads(dv_h), du, dga, dgs], axis=1)
    dh1 = _matmul("d_h1", dproj, full['w_in'], "nt", tk=640)
    dw_in = _matmul("dw_in", h1, dproj, "tn", tn=1280)

    def norm_bwd_fn(xv, gv, scv, shv, dhv, dxv):
        rows = xv.shape[0]
        _, vjp = jax.vjp(_norm_mod, xv, _bc(gv, rows), _bc(scv, rows), _bc(shv, rows))
        dx, dgv, dscv, dshv = vjp(dhv)
        return dx + dxv, _colsum(dgv), _colsum(dscv), _colsum(dshv)

    grad_x, dg_mix, dsc1, dsh1 = _tile_call(
        "norm_mod_mix_bwd", norm_bwd_fn, (1, nrh), [xs, g_mix, sc1, sh1, dh1, dx2],
        [_t(trh, d), _v(d), _v(d), _v(d), _t(trh, d), _t(trh, d)],
        [_sds((l, d))] + [_sds((1, d))] * 3, [_t(trh, d)] + [_v(d)] * 3, acc=(1, 2, 3))

    dmod = jnp.concatenate([dsh1, dsc1, dg1, dsh2, dsc2, dg2], axis=1)
    small = ['ada_b', 'norm_mix_g', 'attn_sinks', 'ssm_a_re', 'ssm_a_im', 'ssm_log_dt', 'ssm_b_re', 'ssm_b_im',
             'ssm_c_re', 'ssm_c_im', 'ssm_d', 'norm_ffn_g', 'ffn_conv_b', 'final_g']
    small_grads = {
        'ada_b': dmod, 'norm_mix_g': dg_mix, 'attn_sinks': dsink[:, 0, 0], 'ssm_a_re': da_re, 'ssm_a_im': da_im,
        'ssm_log_dt': dlog_dt, 'ssm_b_re': db_re.transpose(1, 2, 0), 'ssm_b_im': db_im.transpose(1, 2, 0),
        'ssm_c_re': dc_re, 'ssm_c_im': dc_im, 'ssm_d': dd_tiles, 'norm_ffn_g': dg_ffn, 'ffn_conv_b': dconv_b,
        'final_g': dg_fin}
    gs_pack, sm_offs = _pack([small_grads[k] for k in small], LANES, 8)
    (gs_all,) = _all_gather("gather_small_grads", [gs_pack])
    ws_pack, _ = _pack([given[k] for k in small], LANES, 8)
    ms_pack, _ = _pack([given['m_' + k] for k in small], LANES, 8)
    vs_pack, _ = _pack([given['v_' + k] for k in small], LANES, 8)
    small_out = _adamw("adamw_replicated", gs_all, ws_pack, ms_pack, vs_pack)

    dmod_all = _unpack(gs_all, sm_offs[0], (N_DEV * mod_n,), lead=(N_DEV,))
    dmod_mine = lax.dynamic_slice_in_dim(dmod_all, idx * mod_n, mod_n, axis=1)
    kpad = LANES - N_DEV
    cond_t = jnp.pad(cond_all.T, ((0, 0), (0, kpad)))
    dmod_pad = jnp.pad(dmod_mine, ((0, kpad), (0, 0)))
    g_ada_w = _matmul("dw_ada", cond_t, dmod_pad, "nn")
    ada_out = _adamw("adamw_ada_w", g_ada_w[None], ada_w[0], m_ada_w[0], v_ada_w[0])

    sharded = big + ['ffn_conv_w']
    dw_full = {'w_in': dw_in, 'w_attn_proj': dw_ap, 'w_ssm_glu': dw_glu, 'w_out': dw_out, 'w_ffn_up': dw_up,
               'w_ffn_down': dw_down, 'ffn_conv_w': jnp.concatenate([dcw0, dcw1, dcw2], axis=0)}
    by_dest = []
    for k in sharded:
        g = dw_full[k]
        shard = given[k].shape[1:]
        if k in row_sharded:
            by_dest.append(g.reshape((N_DEV,) + shard))
        else:
            by_dest.append(g.reshape(shard[0], N_DEV, shard[1]).transpose(1, 0, 2))
    rows_align = 256
    gd, g_offs = _pack(by_dest, PACK_W, rows_align, lead=(N_DEV,))
    from_sib = _grad_to_sibling(gd)
    rg = gd.shape[1]
    trp = _pick(rg, 256, 8)
    c_arr = jnp.reshape(ci, (1,)).astype(jnp.int32)

    def add_body(c_ref, a_ref, b_ref, o_ref):
        o_ref[...] = a_ref[...] + b_ref[...]

    chip_sum = pl.pallas_call(
        add_body,
        grid_spec=pltpu.PrefetchScalarGridSpec(
            num_scalar_prefetch=1, grid=(4, rg // trp),
            in_specs=[pl.BlockSpec((1, trp, PACK_W), lambda k, i, cr: (2 * k + cr[0], i, 0)),
                      pl.BlockSpec((1, trp, PACK_W), lambda k, i, cr: (k, i, 0))],
            out_specs=pl.BlockSpec((1, trp, PACK_W), lambda k, i, cr: (k, i, 0))),
        out_shape=_sds((4, rg, PACK_W)), name="grad_chip_sum", compiler_params=_params(2),
    )(c_arr, gd, from_sib)
    parts = _grad_to_chips(chip_sum)
    wb_pack, _ = _pack([given[k][0] for k in sharded], PACK_W, rows_align)
    mb_pack, _ = _pack([given['m_' + k][0] for k in sharded], PACK_W, rows_align)
    vb_pack, _ = _pack([given['v_' + k][0] for k in sharded], PACK_W, rows_align)
    big_out = _adamw("adamw_sharded", parts, wb_pack, mb_pack, vb_pack)

    results = [{}, {}, {}, {}]
    for which in range(4):
        for k, off in zip(small, sm_offs):
            results[which][k] = _unpack(small_out[which], off, given[k].shape)
        for k, off in zip(sharded, g_offs):
            results[which][k] = _unpack(big_out[which], off, given[k].shape)
        results[which]['ada_w'] = ada_out[which][None]
    outs = [loss, grad_x[None]]
    for which in range(4):
        outs += [results[which][k] for k in names]
    return tuple(outs)
```

```python
import functools
import math

import jax
import jax.numpy as jnp
from jax import lax
from jax.experimental import pallas as pl
from jax.experimental.pallas import tpu as pltpu

F32, BF16 = jnp.float32, jnp.bfloat16
MESH = pl.DeviceIdType.MESH
N_DEV = 8

HEAD_DIM = 64
N_KV_HEADS = 2
ATT_BLOCK = 128
NEG_INF = -1e30
SSM_P = 16
SSM_N = 64
LANES = 128
TILE_GROUPS = LANES // SSM_P
TILE_STATES = TILE_GROUPS * SSM_N
RMS_EPS = 1e-6
ADAM_LR, ADAM_B1, ADAM_B2, ADAM_EPS, ADAM_WD, ADAM_STEP = 0.001, 0.9, 0.999, 1e-08, 0.01, 10
VMEM_LIMIT = 56 * 1024 * 1024
PACK_W = 1024


def _params(n_axes):
    return pltpu.CompilerParams(dimension_semantics=("arbitrary",) * n_axes, vmem_limit_bytes=VMEM_LIMIT)


def _pick(dim, pref, align=128):
    if dim <= align:
        return dim
    t = (min(pref, dim) // align) * align
    while t > align and dim % t:
        t -= align
    assert dim % t == 0, (dim, pref, align)
    return t


def _dev():
    return lax.axis_index("x"), lax.axis_index("y"), lax.axis_index("c")


def _tile_call(name, fn, grid, ins, in_specs, out_shapes, out_specs, acc=()):
    n_in, n_out = len(ins), len(out_shapes)
    acc_axis = len(grid) - 1

    def body(*refs):
        vals = fn(*[r[...] for r in refs[:n_in]])
        if not isinstance(vals, (tuple, list)):
            vals = (vals,)
        assert len(vals) == n_out
        for i, (r, v) in enumerate(zip(refs[n_in:], vals)):
            v = v.astype(r.dtype)
            if i in acc:
                first = pl.program_id(acc_axis) == 0

                @pl.when(first)
                def _():
                    r[...] = v

                @pl.when(jnp.logical_not(first))
                def _():
                    r[...] += v
            else:
                r[...] = v

    return pl.pallas_call(
        body, grid=grid, in_specs=in_specs, out_specs=out_specs, out_shape=out_shapes, name=name,
        compiler_params=_params(len(grid)),
    )(*ins)


def _t(tr, tc, off=0):
    return pl.BlockSpec((tr, tc), lambda j, i: (i, j + off // tc))


def _v(tc, off=0, rows=1):
    return pl.BlockSpec((rows, tc), lambda j, i: (0, j + off // tc))


def _prev8(tr, tc, off=0):
    return pl.BlockSpec((8, tc), lambda j, i: (jnp.maximum(i * (tr // 8) - 1, 0), j + off // tc))


def _next8(tr, tc, nrows, off=0):
    return pl.BlockSpec((8, tc), lambda j, i: (jnp.minimum((i + 1) * (tr // 8), nrows // 8 - 1), j + off // tc))


def _st(tr, tc):
    return pl.BlockSpec((2, tr, tc), lambda j, i: (0, i, j))


def _bc(v, rows):
    return jnp.broadcast_to(v, (rows, v.shape[-1]))


def _colsum(v):
    return jnp.sum(v, axis=0, keepdims=True)


def _matmul(name, a, b, mode, out_dtype=F32, tm=1024, tn=1024, tk=512, out_stack=None):
    def dims(z):
        return (z.shape[-2], z.shape[-1] * (z.shape[0] if z.ndim == 3 else 1))

    ar, ac = dims(a)
    br, bc = dims(b)
    if mode == "nn":
        m, k, n = ar, ac, bc
        assert br == k
    elif mode == "nt":
        m, k, n = ar, ac, br
        assert bc == k
    else:
        m, k, n = ac, ar, bc
        assert br == k
    m_lim, k_lim, n_lim = [m], [k], [n]
    if a.ndim == 3:
        (m_lim if mode == "tn" else k_lim).append(a.shape[-1])
    if b.ndim == 3:
        (k_lim if mode == "nt" else n_lim).append(b.shape[-1])
    if out_stack:
        n_lim.append(n // out_stack)
    tm = _pick(functools.reduce(math.gcd, m_lim), tm)
    tn = _pick(functools.reduce(math.gcd, n_lim), tn)
    tk = _pick(functools.reduce(math.gcd, k_lim), tk)
    nk = k // tk

    def spec(z, brows, bcols, ridx, cidx):
        if z.ndim == 3:
            per = z.shape[-1] // bcols
            return pl.BlockSpec((None, brows, bcols),
                                lambda i, j, kk: (cidx(i, j, kk) // per, ridx(i, j, kk), cidx(i, j, kk) % per))
        return pl.BlockSpec((brows, bcols), lambda i, j, kk: (ridx(i, j, kk), cidx(i, j, kk)))

    gi = lambda i, j, kk: i
    gj = lambda i, j, kk: j
    gk = lambda i, j, kk: kk
    if mode == "nn":
        a_spec, b_spec = spec(a, tm, tk, gi, gk), spec(b, tk, tn, gk, gj)
        dn = (((1,), (0,)), ((), ()))
    elif mode == "nt":
        a_spec, b_spec = spec(a, tm, tk, gi, gk), spec(b, tn, tk, gj, gk)
        dn = (((1,), (1,)), ((), ()))
    else:
        a_spec, b_spec = spec(a, tk, tm, gk, gi), spec(b, tk, tn, gk, gj)
        dn = (((0,), (0,)), ((), ()))

    def body(a_ref, b_ref, o_ref, acc_ref):
        kk = pl.program_id(2)

        @pl.when(kk == 0)
        def _():
            acc_ref[...] = jnp.zeros_like(acc_ref)

        acc_ref[...] += lax.dot_general(a_ref[...].astype(BF16), b_ref[...].astype(BF16), dn,
                                        preferred_element_type=F32)

        @pl.when(kk == nk - 1)
        def _():
            o_ref[...] = acc_ref[...].astype(o_ref.dtype)

    if out_stack:
        per = (n // out_stack) // tn
        out_spec = pl.BlockSpec((None, tm, tn), lambda i, j, kk: (j // per, i, j % per))
        out_shape = jax.ShapeDtypeStruct((out_stack, m, n // out_stack), out_dtype)
    else:
        out_spec = pl.BlockSpec((tm, tn), lambda i, j, kk: (i, j))
        out_shape = jax.ShapeDtypeStruct((m, n), out_dtype)
    return pl.pallas_call(
        body, grid=(m // tm, n // tn, nk), in_specs=[a_spec, b_spec], out_specs=out_spec, out_shape=out_shape,
        scratch_shapes=[pltpu.VMEM((tm, tn), F32)], name=name, compiler_params=_params(3),
    )(a, b)


def _all_gather(name, arrs):
    n = len(arrs)

    def body(*refs):
        ins, outs = refs[:n], refs[n:2 * n]
        send_sems, recv_sems, local_sems = refs[2 * n:]
        x, y, c = _dev()
        me, sib = (x, y, c), (x, y, 1 - c)
        chips = [(1 - x, y), (x, 1 - y), (1 - x, 1 - y)]

        def slot(p):
            return 4 * p[0] + 2 * p[1] + p[2]

        def copy(a, k, block, to, src=None):
            dst = outs[a].at[slot(block)]
            return pltpu.make_async_remote_copy(
                src_ref=dst if src is None else src, dst_ref=dst,
                send_sem=send_sems.at[7 * a + k], recv_sem=recv_sems.at[7 * a + k],
                device_id=to, device_id_type=MESH)

        mine = [pltpu.make_async_copy(ins[a], outs[a].at[slot(me)], local_sems.at[a]) for a in range(n)]
        for cp in mine:
            cp.start()
        first = []
        for a in range(n):
            first.append(copy(a, 0, me, sib, src=ins[a]))
            first += [copy(a, 1 + j, me, (*chip, c), src=ins[a]) for j, chip in enumerate(chips)]
        for cp in first:
            cp.start()
        passed = []
        for j, chip in enumerate(chips):
            for a in range(n):
                copy(a, 1 + j, (*chip, c), me).wait_recv()
                cp = copy(a, 4 + j, (*chip, c), sib)
                cp.start()
                passed.append(cp)
        for a in range(n):
            copy(a, 0, sib, me).wait_recv()
            for j, chip in enumerate(chips):
                copy(a, 4 + j, (*chip, 1 - c), me).wait_recv()
        for cp in first + passed:
            cp.wait_send()
        for cp in mine:
            cp.wait()

    any_spec = pl.BlockSpec(memory_space=pl.ANY)
    return pl.pallas_call(
        body, in_specs=[any_spec] * n, out_specs=[any_spec] * n,
        out_shape=[jax.ShapeDtypeStruct((N_DEV,) + a.shape, a.dtype) for a in arrs],
        scratch_shapes=[pltpu.SemaphoreType.DMA((7 * n,)), pltpu.SemaphoreType.DMA((7 * n,)),
                        pltpu.SemaphoreType.DMA((n,))],
        name=name,
    )(*arrs)


def _grad_to_sibling(gds):
    n = len(gds)

    def body(*refs):
        g_refs, r_refs = refs[:n], refs[n:2 * n]
        send_sems, recv_sems = refs[2 * n:]
        x, y, c = _dev()
        cps = []
        for a in range(n):
            for k in range(4):
                cp = pltpu.make_async_remote_copy(
                    src_ref=g_refs[a].at[2 * k + (1 - c)], dst_ref=r_refs[a].at[k],
                    send_sem=send_sems.at[4 * a + k], recv_sem=recv_sems.at[4 * a + k],
                    device_id=(x, y, 1 - c), device_id_type=MESH)
                cp.start()
                cps.append(cp)
        for cp in cps:
            cp.wait()

    any_spec = pl.BlockSpec(memory_space=pl.ANY)
    return pl.pallas_call(
        body, in_specs=[any_spec] * n, out_specs=[any_spec] * n,
        out_shape=[jax.ShapeDtypeStruct((4,) + g.shape[1:], g.dtype) for g in gds],
        scratch_shapes=[pltpu.SemaphoreType.DMA((4 * n,)), pltpu.SemaphoreType.DMA((4 * n,))],
        name="grad_to_sibling",
    )(*gds)


def _chip_sum(name, gd, from_sib, c_arr):
    _, k, n = gd.shape
    tr = _pick(k, max(16, (1 << 20) // (4 * n)), 16)

    def body(c_ref, a_ref, b_ref, o_ref):
        o_ref[...] = (a_ref[...] + b_ref[...]).astype(o_ref.dtype)

    return pl.pallas_call(
        body,
        grid_spec=pltpu.PrefetchScalarGridSpec(
            num_scalar_prefetch=1, grid=(4, k // tr),
            in_specs=[pl.BlockSpec((1, tr, n), lambda q, i, cr: (2 * q + cr[0], i, 0)),
                      pl.BlockSpec((1, tr, n), lambda q, i, cr: (q, i, 0))],
            out_specs=pl.BlockSpec((1, tr, n), lambda q, i, cr: (q, i, 0))),
        out_shape=jax.ShapeDtypeStruct((4, k, n), BF16), name=name, compiler_params=_params(2),
    )(c_arr, gd, from_sib)


def _grad_to_chips(sums):
    n = len(sums)

    def body(*refs):
        s_refs, p_refs = refs[:n], refs[n:2 * n]
        send_sems, recv_sems, local_sems = refs[2 * n:]
        x, y, c = _dev()
        my_chip = 2 * x + y
        cps = []
        for a in range(n):
            local = pltpu.make_async_copy(s_refs[a].at[my_chip], p_refs[a].at[my_chip], local_sems.at[a])
            local.start()
            cps.append(local)
            for j, (px, py) in enumerate([(1 - x, y), (x, 1 - y), (1 - x, 1 - y)]):
                cp = pltpu.make_async_remote_copy(
                    src_ref=s_refs[a].at[2 * px + py], dst_ref=p_refs[a].at[my_chip],
                    send_sem=send_sems.at[3 * a + j], recv_sem=recv_sems.at[3 * a + j],
                    device_id=(px, py, c), device_id_type=MESH)
                cp.start()
                cps.append(cp)
        for cp in cps:
            cp.wait()

    any_spec = pl.BlockSpec(memory_space=pl.ANY)
    return pl.pallas_call(
        body, in_specs=[any_spec] * n, out_specs=[any_spec] * n,
        out_shape=[jax.ShapeDtypeStruct(s.shape, s.dtype) for s in sums],
        scratch_shapes=[pltpu.SemaphoreType.DMA((3 * n,)), pltpu.SemaphoreType.DMA((3 * n,)),
                        pltpu.SemaphoreType.DMA((n,))],
        name="grad_to_chips",
    )(*sums)


def _pack_rows(sizes, width, row_align):
    offs, r = [], 0
    for s in sizes:
        offs.append(r)
        r += -(-s // width)
    total = -(-r // row_align) * row_align
    return offs, total


def _pack(items, width, row_align, lead=()):
    nl = len(lead)
    sizes = [int(jnp.size(a)) // max(1, functools.reduce(lambda p, q: p * q, lead, 1)) for a in items]
    offs, total = _pack_rows(sizes, width, row_align)
    flat = []
    used = 0
    for a, s in zip(items, sizes):
        f = a.reshape(lead + (s,))
        pad = -(-s // width) * width - s
        if pad:
            f = jnp.pad(f, [(0, 0)] * nl + [(0, pad)])
        flat.append(f)
        used += s + pad
    tail = total * width - used
    if tail:
        flat.append(jnp.zeros(lead + (tail,), items[0].dtype))
    return jnp.concatenate(flat, axis=-1).reshape(lead + (total, width)), offs


def _unpack(packed, off, shape, lead=()):
    nl = len(lead)
    size = functools.reduce(lambda p, q: p * q, shape, 1)
    width = packed.shape[-1]
    rows = -(-size // width)
    blk = lax.slice_in_dim(packed, off, off + rows, axis=nl).reshape(lead + (rows * width,))
    return lax.slice_in_dim(blk, 0, size, axis=nl).reshape(lead + tuple(shape))


def _rms(x, g):
    return (x * lax.rsqrt(jnp.mean(x * x, axis=-1, keepdims=True) + RMS_EPS)) * g


def _norm_mod(x, g, sc, sh):
    return _rms(x, g) * (1.0 + sc) + sh


def _mix_fn(glu_a, glu_b, attn, ga, gs):
    return jax.nn.sigmoid(ga) * attn + jax.nn.sigmoid(gs) * (glu_a * jax.nn.sigmoid(glu_b))


def _s5_disc_fn(a_re, a_im, log_dt, b_re, b_im):
    dt = jnp.exp(log_dt)
    mag = jnp.exp(a_re * dt)
    lr, li = mag * jnp.cos(a_im * dt), mag * jnp.sin(a_im * dt)
    den = a_re * a_re + a_im * a_im
    zr = ((lr - 1.0) * a_re + li * a_im) / den
    zi = (li * a_re - (lr - 1.0) * a_im) / den
    return lr, li, zr[None] * b_re - zi[None] * b_im, zr[None] * b_im + zi[None] * b_re


def _adamw_fn(w, g, m, v):
    m = ADAM_B1 * m + (1.0 - ADAM_B1) * g
    v = ADAM_B2 * v + (1.0 - ADAM_B2) * jnp.square(g)
    m_hat = m / (1.0 - ADAM_B1 ** ADAM_STEP)
    v_hat = v / (1.0 - ADAM_B2 ** ADAM_STEP)
    delta = -ADAM_LR * (m_hat / (jnp.sqrt(v_hat) + ADAM_EPS) + ADAM_WD * w)
    return delta, m, v


def _adamw(name, parts, w, m, v):
    p, r, c = parts.shape
    tr = _pick(r, max(8, (1 << 21) // (4 * c * max(p, 2))), 8)

    def fn(pv, wv, mv, vv):
        g = pv[0]
        for i in range(1, p):
            g = g + pv[i]
        d, m2, v2 = _adamw_fn(wv, g, mv, vv)
        return g, d, m2, v2

    spec = pl.BlockSpec((tr, c), lambda i: (i, 0))
    return _tile_call(
        name, fn, (r // tr,), [parts, w, m, v],
        [pl.BlockSpec((p, tr, c), lambda i: (0, i, 0)), spec, spec, spec],
        [jax.ShapeDtypeStruct((r, c), F32)] * 4, [spec] * 4)


def _adamw_sharded(name, parts, gd, from_sib, w, m, v, place):
    _, k, n = parts.shape
    tr = _pick(k, max(16, (1 << 20) // (4 * n)), 16)

    def body(pl_ref, p_ref, a_ref, b_ref, w_ref, m_ref, v_ref, g_ref, d_ref, m2_ref, v2_ref):
        own = a_ref[0] + b_ref[0]
        g = None
        for q in range(4):
            term = jnp.where(pl_ref[0] == q, own, p_ref[q].astype(F32))
            g = term if g is None else g + term
        d, m2, v2 = _adamw_fn(w_ref[...], g, m_ref[...], v_ref[...])
        g_ref[...] = g
        d_ref[...] = d
        m2_ref[...] = m2
        v2_ref[...] = v2

    spec = pl.BlockSpec((tr, n), lambda i, pr: (i, 0))
    return pl.pallas_call(
        body,
        grid_spec=pltpu.PrefetchScalarGridSpec(
            num_scalar_prefetch=1, grid=(k // tr,),
            in_specs=[pl.BlockSpec((4, tr, n), lambda i, pr: (0, i, 0)),
                      pl.BlockSpec((1, tr, n), lambda i, pr: (2 * pr[0] + pr[1], i, 0)),
                      pl.BlockSpec((1, tr, n), lambda i, pr: (pr[0], i, 0)),
                      spec, spec, spec],
            out_specs=[spec] * 4),
        out_shape=[jax.ShapeDtypeStruct((k, n), F32)] * 4, name=name, compiler_params=_params(1),
    )(place, parts, gd, from_sib, w, m, v)


def _attn_mask(n):
    qi = lax.broadcasted_iota(jnp.int32, (ATT_BLOCK, 2 * ATT_BLOCK), 0)
    kj = lax.broadcasted_iota(jnp.int32, (ATT_BLOCK, 2 * ATT_BLOCK), 1)
    rel = qi + ATT_BLOCK - kj
    return (rel >= 0) & (rel < ATT_BLOCK) & ((kj >= ATT_BLOCK) | (n > 0))


def _attn_probs(q, k, sink, mask):
    s = lax.dot_general(q, k, (((1,), (1,)), ((), ())), preferred_element_type=F32) * (HEAD_DIM ** -0.5)
    s = jnp.where(mask, s, NEG_INF)
    m = jnp.maximum(jnp.max(s, axis=-1, keepdims=True), sink)
    p = jnp.exp(s - m)
    e_sink = jnp.exp(sink - m)
    den = jnp.sum(p, axis=-1, keepdims=True) + e_sink
    return p / den, e_sink / den


def _attn_specs(qpk):
    blk = ATT_BLOCK
    q_spec = pl.BlockSpec((qpk, blk, HEAD_DIM), lambda h, n: (h, n, 0))
    cur = pl.BlockSpec((1, blk, HEAD_DIM), lambda h, n: (h, n, 0))
    prev = pl.BlockSpec((1, blk, HEAD_DIM), lambda h, n: (h, jnp.maximum(n - 1, 0), 0))
    sink_spec = pl.BlockSpec((qpk, 1, 1), lambda h, n: (h, 0, 0))
    return q_spec, cur, prev, sink_spec


def _attn_fwd(q, k, v, sinks):
    hq, l, _ = q.shape
    qpk = hq // N_KV_HEADS
    nb = l // ATT_BLOCK
    q_spec, cur, prev, sink_spec = _attn_specs(qpk)

    def body(q_ref, kp_ref, kc_ref, vp_ref, vc_ref, sink_ref, o_ref):
        mask = _attn_mask(pl.program_id(1))
        kk = jnp.concatenate([kp_ref[0], kc_ref[0]], axis=0).astype(BF16)
        vv = jnp.concatenate([vp_ref[0], vc_ref[0]], axis=0).astype(BF16)
        for g in range(qpk):
            p, _ = _attn_probs(q_ref[g].astype(BF16), kk, sink_ref[g], mask)
            o_ref[g] = jnp.dot(p.astype(BF16), vv, preferred_element_type=F32).astype(o_ref.dtype)

    return pl.pallas_call(
        body, grid=(N_KV_HEADS, nb), in_specs=[q_spec, prev, cur, prev, cur, sink_spec],
        out_specs=q_spec, out_shape=jax.ShapeDtypeStruct((hq, l, HEAD_DIM), BF16),
        name="attn_fwd", compiler_params=_params(2),
    )(q, k, k, v, v, sinks)


def _attn_bwd(q, k, v, sinks, do):
    hq, l, _ = q.shape
    qpk = hq // N_KV_HEADS
    nb = l // ATT_BLOCK
    blk = ATT_BLOCK
    q_spec, cur, prev, sink_spec = _attn_specs(qpk)
    part_spec = pl.BlockSpec((1, 1, 2 * blk, HEAD_DIM), lambda h, n: (h, n, 0, 0))
    dsink_spec = pl.BlockSpec((qpk, 1, LANES), lambda h, n: (h, 0, 0))
    tn = (((0,), (0,)), ((), ()))

    def body(q_ref, do_ref, kp_ref, kc_ref, vp_ref, vc_ref, sink_ref, dq_ref, dkp_ref, dvp_ref, dsink_ref):
        n = pl.program_id(1)
        mask = _attn_mask(n)
        kk = jnp.concatenate([kp_ref[0], kc_ref[0]], axis=0).astype(BF16)
        vv = jnp.concatenate([vp_ref[0], vc_ref[0]], axis=0).astype(BF16)
        dk = jnp.zeros((2 * blk, HEAD_DIM), F32)
        dv = jnp.zeros((2 * blk, HEAD_DIM), F32)
        for g in range(qpk):
            qg = q_ref[g].astype(BF16)
            dog = do_ref[g].astype(BF16)
            p, p_sink = _attn_probs(qg, kk, sink_ref[g], mask)
            pb = p.astype(BF16)
            o = jnp.dot(pb, vv, preferred_element_type=F32)
            delta = jnp.sum(do_ref[g].astype(F32) * o, axis=-1, keepdims=True)
            dp = lax.dot_general(dog, vv, (((1,), (1,)), ((), ())), preferred_element_type=F32)
            ds = (p * (dp - delta) * (HEAD_DIM ** -0.5)).astype(BF16)
            dq_ref[g] = jnp.dot(ds, kk, preferred_element_type=F32).astype(dq_ref.dtype)
            dk = dk + lax.dot_general(ds, qg, tn, preferred_element_type=F32)
            dv = dv + lax.dot_general(pb, dog, tn, preferred_element_type=F32)
            dsg = jnp.broadcast_to(-jnp.sum(p_sink * delta, axis=0, keepdims=True), (1, LANES))

            @pl.when(n == 0)
            def _():
                dsink_ref[g] = dsg

            @pl.when(n > 0)
            def _():
                dsink_ref[g] += dsg

        dkp_ref[0, 0] = dk
        dvp_ref[0, 0] = dv

    part_shape = jax.ShapeDtypeStruct((N_KV_HEADS, nb, 2 * blk, HEAD_DIM), F32)
    dq, dkp, dvp, dsink = pl.pallas_call(
        body, grid=(N_KV_HEADS, nb), in_specs=[q_spec, q_spec, prev, cur, prev, cur, sink_spec],
        out_specs=[q_spec, part_spec, part_spec, dsink_spec],
        out_shape=[jax.ShapeDtypeStruct((hq, l, HEAD_DIM), BF16), part_shape, part_shape,
                   jax.ShapeDtypeStruct((hq, 1, LANES), F32)],
        name="attn_bwd", compiler_params=_params(2),
    )(q, do, k, k, v, v, sinks)

    def combine(a_cur, a_nxt, b_cur, b_nxt):
        last = pl.program_id(1) == nb - 1
        keep = jnp.where(last, 0.0, 1.0)
        return (a_cur[0, 0, blk:] + keep * a_nxt[0, 0, :blk])[None], (b_cur[0, 0, blk:] + keep * b_nxt[0, 0, :blk])[None]

    nxt_spec = pl.BlockSpec((1, 1, 2 * blk, HEAD_DIM), lambda h, n: (h, jnp.minimum(n + 1, nb - 1), 0, 0))
    kv_shape = jax.ShapeDtypeStruct((N_KV_HEADS, l, HEAD_DIM), BF16)
    dk, dv = _tile_call("attn_dkv", combine, (N_KV_HEADS, nb), [dkp, dkp, dvp, dvp],
                        [part_spec, nxt_spec, part_spec, nxt_spec], [kv_shape, kv_shape], [cur, cur])
    return dq, dk, dv, dsink


def _block_diag(m):
    j, gl, a, b = m.shape
    eye = jnp.eye(gl, dtype=m.dtype)
    return (m[:, :, :, None, :] * eye[None, :, None, :, None]).reshape(j, gl * a, gl * b)


def _diag_blocks(z, a):
    j = z.shape[0]
    gl = z.shape[1] // a
    b = z.shape[2] // gl
    d = jnp.diagonal(z.reshape(j, gl, a, gl, b), axis1=1, axis2=3)
    return d.transpose(0, 3, 1, 2)


def _s5_scan_fwd(x_ref, base, lr, li, hr, hi, t_len):
    hs = TILE_STATES

    def step(t, carry):
        hr, hi = carry
        row = pl.ds(base + t, 1)
        nr = lr * hr - li * hi + x_ref[row, 0:hs]
        ni = lr * hi + li * hr + x_ref[row, hs:2 * hs]
        x_ref[row, 0:hs] = nr
        x_ref[row, hs:2 * hs] = ni
        return nr, ni

    return lax.fori_loop(0, t_len, step, (hr, hi), unroll=8)


def _s5_fwd(proj, u_off, bd, cbd, lam, dvec, t_len):
    l = proj.shape[0]
    nj = bd.shape[0]
    nch = l // t_len
    hs = TILE_STATES
    ub = u_off // LANES

    def body(u_ref, bd_ref, cbd_ref, lam_ref, d_ref, y_ref, hst_ref, x_ref, h_ref):
        @pl.when(pl.program_id(1) == 0)
        def _():
            h_ref[...] = jnp.zeros_like(h_ref)

        hst_ref[0, 0] = h_ref[...]
        u = u_ref[...]
        x_ref[...] = jnp.dot(u.astype(BF16), bd_ref[0], preferred_element_type=F32)
        lr, li = lam_ref[0, 0:1, :], lam_ref[0, 1:2, :]
        hr, hi = _s5_scan_fwd(x_ref, 0, lr, li, h_ref[:, 0:hs], h_ref[:, hs:2 * hs], t_len)
        h_ref[:, 0:hs] = hr
        h_ref[:, hs:2 * hs] = hi
        y_ref[...] = jnp.dot(x_ref[...].astype(BF16), cbd_ref[0], preferred_element_type=F32) + d_ref[0] * u

    return pl.pallas_call(
        body, grid=(nj, nch),
        in_specs=[pl.BlockSpec((t_len, LANES), lambda j, c: (c, ub + j)),
                  pl.BlockSpec((1, LANES, 2 * hs), lambda j, c: (j, 0, 0)),
                  pl.BlockSpec((1, 2 * hs, LANES), lambda j, c: (j, 0, 0)),
                  pl.BlockSpec((1, 2, hs), lambda j, c: (j, 0, 0)),
                  pl.BlockSpec((1, 1, LANES), lambda j, c: (j, 0, 0))],
        out_specs=[pl.BlockSpec((t_len, LANES), lambda j, c: (c, j)),
                   pl.BlockSpec((1, 1, 1, 2 * hs), lambda j, c: (j, c, 0, 0))],
        out_shape=[jax.ShapeDtypeStruct((l, nj * LANES), F32),
                   jax.ShapeDtypeStruct((nj, nch, 1, 2 * hs), F32)],
        scratch_shapes=[pltpu.VMEM((t_len, 2 * hs), F32), pltpu.VMEM((1, 2 * hs), F32)],
        name="s5_fwd", compiler_params=_params(2),
    )(proj, bd, cbd, lam, dvec)


def _s5_bwd(proj, u_off, dy, hst, bd, bdt, cbdt, lam, dvec, t_len):
    l = proj.shape[0]
    nj = bd.shape[0]
    nch = l // t_len
    hs = TILE_STATES
    ub = u_off // LANES
    tn = (((0,), (0,)), ((), ()))

    def body(u_ref, dy_ref, hst_ref, bd_ref, bdt_ref, cbdt_ref, lam_ref, d_ref,
             du_ref, dbd_ref, dcbdt_ref, dlam_ref, dd_ref, x_ref, g_ref, gc_ref):
        first = pl.program_id(1) == 0

        @pl.when(first)
        def _():
            gc_ref[...] = jnp.zeros_like(gc_ref)

        lr, li = lam_ref[0, 0:1, :], lam_ref[0, 1:2, :]
        u = u_ref[...]
        dy = dy_ref[...]
        ub16, dyb16 = u.astype(BF16), dy.astype(BF16)
        x_ref[pl.ds(8, t_len), :] = jnp.dot(ub16, bd_ref[0], preferred_element_type=F32)
        h0 = hst_ref[0, 0]
        x_ref[7:8, :] = h0
        _s5_scan_fwd(x_ref, 8, lr, li, h0[:, 0:hs], h0[:, hs:2 * hs], t_len)
        g_ref[...] = jnp.dot(dyb16, cbdt_ref[0], preferred_element_type=F32)

        def step(s, carry):
            gr, gi, alr, ali = carry
            t = t_len - 1 - s
            row = pl.ds(t, 1)
            nr = g_ref[row, 0:hs] + lr * gr + li * gi
            ni = g_ref[row, hs:2 * hs] + lr * gi - li * gr
            g_ref[row, 0:hs] = nr
            g_ref[row, hs:2 * hs] = ni
            hpr = x_ref[pl.ds(7 + t, 1), 0:hs]
            hpi = x_ref[pl.ds(7 + t, 1), hs:2 * hs]
            return nr, ni, alr + nr * hpr + ni * hpi, ali + ni * hpr - nr * hpi

        zero = jnp.zeros((1, hs), F32)
        gr, gi, alr, ali = lax.fori_loop(0, t_len, step, (gc_ref[:, 0:hs], gc_ref[:, hs:2 * hs], zero, zero),
                                         unroll=8)
        gc_ref[:, 0:hs] = gr
        gc_ref[:, hs:2 * hs] = gi
        g = g_ref[...].astype(BF16)
        h = x_ref[pl.ds(8, t_len), :].astype(BF16)
        du_ref[...] = (jnp.dot(g, bdt_ref[0], preferred_element_type=F32) + d_ref[0] * dy).astype(du_ref.dtype)
        sign = jnp.where(lax.broadcasted_iota(jnp.int32, (1, 2 * hs), 1) < hs, 1.0, -1.0)
        dbd = lax.dot_general(ub16, g, tn, preferred_element_type=F32)
        dcbdt = lax.dot_general(dyb16, h, tn, preferred_element_type=F32) * sign
        ddv = _colsum(dy * u)

        @pl.when(first)
        def _():
            dbd_ref[0] = dbd
            dcbdt_ref[0] = dcbdt
            dlam_ref[0, 0:1, :] = alr
            dlam_ref[0, 1:2, :] = ali
            dd_ref[0] = ddv

        @pl.when(jnp.logical_not(first))
        def _():
            dbd_ref[0] += dbd
            dcbdt_ref[0] += dcbdt
            dlam_ref[0, 0:1, :] += alr
            dlam_ref[0, 1:2, :] += ali
            dd_ref[0] += ddv

    rev = lambda c: nch - 1 - c
    wide = pl.BlockSpec((1, LANES, 2 * hs), lambda j, c: (j, 0, 0))
    tall = pl.BlockSpec((1, 2 * hs, LANES), lambda j, c: (j, 0, 0))
    return pl.pallas_call(
        body, grid=(nj, nch),
        in_specs=[pl.BlockSpec((t_len, LANES), lambda j, c: (rev(c), ub + j)),
                  pl.BlockSpec((t_len, LANES), lambda j, c: (rev(c), j)),
                  pl.BlockSpec((1, 1, 1, 2 * hs), lambda j, c: (j, rev(c), 0, 0)),
                  wide, tall, wide,
                  pl.BlockSpec((1, 2, hs), lambda j, c: (j, 0, 0)),
                  pl.BlockSpec((1, 1, LANES), lambda j, c: (j, 0, 0))],
        out_specs=[pl.BlockSpec((t_len, LANES), lambda j, c: (rev(c), j)),
                   wide, wide,
                   pl.BlockSpec((1, 2, hs), lambda j, c: (j, 0, 0)),
                   pl.BlockSpec((1, 1, LANES), lambda j, c: (j, 0, 0))],
        out_shape=[jax.ShapeDtypeStruct((l, nj * LANES), BF16),
                   jax.ShapeDtypeStruct((nj, LANES, 2 * hs), F32),
                   jax.ShapeDtypeStruct((nj, LANES, 2 * hs), F32),
                   jax.ShapeDtypeStruct((nj, 2, hs), F32),
                   jax.ShapeDtypeStruct((nj, 1, LANES), F32)],
        scratch_shapes=[pltpu.VMEM((t_len + 8, 2 * hs), F32), pltpu.VMEM((t_len, 2 * hs), F32),
                        pltpu.VMEM((1, 2 * hs), F32)],
        name="s5_bwd", compiler_params=_params(2),
    )(proj, dy, hst, bd, bdt, cbdt, lam, dvec)


def _full_spec(shape):
    nd = len(shape)
    return pl.BlockSpec(tuple(shape), lambda i: (0,) * nd)


def _sds(shape, dtype=F32):
    return jax.ShapeDtypeStruct(tuple(shape), dtype)


def kernel(x, c, ada_w, ada_b, norm_mix_g, w_in, attn_sinks, w_attn_proj, ssm_a_re, ssm_a_im, ssm_log_dt, ssm_b_re, ssm_b_im, ssm_c_re, ssm_c_im, ssm_d, w_ssm_glu, w_out, norm_ffn_g, w_ffn_up, ffn_conv_w, ffn_conv_b, w_ffn_down, final_g, loss_target, m_ada_w, m_ada_b, m_norm_mix_g, m_w_in, m_attn_sinks, m_w_attn_proj, m_ssm_a_re, m_ssm_a_im, m_ssm_log_dt, m_ssm_b_re, m_ssm_b_im, m_ssm_c_re, m_ssm_c_im, m_ssm_d, m_w_ssm_glu, m_w_out, m_norm_ffn_g, m_w_ffn_up, m_ffn_conv_w, m_ffn_conv_b, m_w_ffn_down, m_final_g, v_ada_w, v_ada_b, v_norm_mix_g, v_w_in, v_attn_sinks, v_w_attn_proj, v_ssm_a_re, v_ssm_a_im, v_ssm_log_dt, v_ssm_b_re, v_ssm_b_im, v_ssm_c_re, v_ssm_c_im, v_ssm_d, v_w_ssm_glu, v_w_out, v_norm_ffn_g, v_w_ffn_up, v_ffn_conv_w, v_ffn_conv_b, v_w_ffn_down, v_final_g):
    given = dict(locals())
    names = ['ada_w', 'ada_b', 'norm_mix_g', 'w_in', 'attn_sinks', 'w_attn_proj', 'ssm_a_re', 'ssm_a_im',
             'ssm_log_dt', 'ssm_b_re', 'ssm_b_im', 'ssm_c_re', 'ssm_c_im', 'ssm_d', 'w_ssm_glu', 'w_out',
             'norm_ffn_g', 'w_ffn_up', 'ffn_conv_w', 'ffn_conv_b', 'w_ffn_down', 'final_g']

    xs = x[0]
    tgt = loss_target[0]
    l, d = xs.shape
    attn_w = w_attn_proj.shape[1]
    ssm_w = w_ssm_glu.shape[1]
    hq = attn_sinks.shape[1]
    qpk = hq // N_KV_HEADS
    kv_w = N_KV_HEADS * HEAD_DIM
    n_groups = ssm_a_re.shape[1]
    dff = ffn_conv_b.shape[1]
    in_w = attn_w + 2 * kv_w + ssm_w + 2 * d
    nj = ssm_w // LANES
    off_k, off_v, off_u = attn_w, attn_w + kv_w, attn_w + 2 * kv_w
    off_ga, off_gs = off_u + ssm_w, off_u + ssm_w + d
    assert hq * HEAD_DIM == attn_w and n_groups * SSM_P == ssm_w and l % ATT_BLOCK == 0

    xi, yi, ci = _dev()
    idx = 4 * xi + 2 * yi + ci

    row_sharded = {'w_out': (d, d), 'w_ffn_down': (dff, d)}
    big = ['w_in', 'w_attn_proj', 'w_ssm_glu', 'w_out', 'w_ffn_up', 'w_ffn_down']
    spack, s_offs = _pack([c, ffn_conv_w[0]], LANES, 8)
    gathered = _all_gather("gather_weights", [given[k][0].astype(BF16) for k in big] + [spack])
    sg = gathered[-1]
    full = dict(zip(big, gathered[:-1]))
    full['w_in'] = full['w_in'].transpose(1, 0, 2).reshape(d, in_w)
    for k in row_sharded:
        full[k] = full[k].reshape(row_sharded[k])
    c_all = _unpack(sg, s_offs[0], (d,), lead=(N_DEV,))
    conv_w = _unpack(sg, s_offs[1], ffn_conv_w.shape[1:], lead=(N_DEV,)).transpose(1, 0, 2).reshape(3, dff)
    conv_b = ffn_conv_b

    mod_n = ada_w.shape[2]
    tcm = _pick(mod_n, 512)
    ada_b_mine = lax.dynamic_slice_in_dim(ada_b, idx * mod_n, mod_n, axis=1)

    def modpart_fn(cv, wv, bv):
        cond = cv * jax.nn.sigmoid(cv)
        return jnp.dot(cond.astype(BF16), wv.astype(BF16), preferred_element_type=F32) + bv, cond

    modp, cond_all = _tile_call(
        "ada_rows", modpart_fn, (mod_n // tcm,), [c_all, ada_w[0], ada_b_mine],
        [pl.BlockSpec((N_DEV, d), lambda j: (0, 0)), pl.BlockSpec((d, tcm), lambda j: (0, j)),
         pl.BlockSpec((1, tcm), lambda j: (0, j))],
        [_sds((N_DEV, mod_n)), _sds((N_DEV, d))],
        [pl.BlockSpec((N_DEV, tcm), lambda j: (0, j)), pl.BlockSpec((N_DEV, d), lambda j: (0, 0))])
    (modg,) = _all_gather("gather_ada_rows", [modp])
    mod = lax.dynamic_index_in_dim(modg, idx, axis=1, keepdims=False).reshape(1, N_DEV * mod_n)
    sh1, sc1, g1, sh2, sc2, g2 = [mod[:, i * d:(i + 1) * d] for i in range(6)]

    tr = _pick(l, 256, 8)
    trh = _pick(l, 128, 8)
    nr, nrh = l // tr, l // trh
    g_mix, g_ffn, g_fin = norm_mix_g, norm_ffn_g, final_g.reshape(1, d)

    h1 = _tile_call("norm_mod_mix", _norm_mod, (1, nr), [xs, g_mix, sc1, sh1],
                    [_t(tr, d), _v(d), _v(d), _v(d)], [_sds((l, d), BF16)], [_t(tr, d)])[0]
    proj = _matmul("proj_in", h1, full['w_in'], "nn", tn=1280)

    def heads(z, n):
        return z.reshape(l, n, HEAD_DIM).transpose(1, 0, 2)

    qh = heads(proj[:, :attn_w], hq)
    kh = heads(proj[:, off_k:off_k + kv_w], N_KV_HEADS)
    vh = heads(proj[:, off_v:off_v + kv_w], N_KV_HEADS)
    sinks3 = attn_sinks.reshape(hq, 1, 1)
    o_h = _attn_fwd(qh, kh, vh, sinks3)
    o2 = o_h.transpose(1, 0, 2).reshape(l, attn_w)
    attn = _matmul("attn_proj", o2, full['w_attn_proj'], "nn")

    gn = (n_groups, SSM_N)
    pgn = (SSM_P, n_groups, SSM_N)
    a_re, a_im, log_dt = ssm_a_re[0], ssm_a_im[0], ssm_log_dt[0].reshape(n_groups, 1)
    b_re, b_im = ssm_b_re[0].transpose(2, 0, 1), ssm_b_im[0].transpose(2, 0, 1)
    disc_ins = [a_re, a_im, log_dt, b_re, b_im]
    disc_specs = [_full_spec(gn), _full_spec(gn), _full_spec((n_groups, 1)), _full_spec(pgn), _full_spec(pgn)]
    lam_r, lam_i, bb_r, bb_i = _tile_call(
        "s5_discretise", _s5_disc_fn, (1,), disc_ins, disc_specs,
        [_sds(gn), _sds(gn), _sds(pgn), _sds(pgn)],
        [_full_spec(gn), _full_spec(gn), _full_spec(pgn), _full_spec(pgn)])

    def tiles_gpn(z):
        return z.reshape(SSM_P, nj, TILE_GROUPS, SSM_N).transpose(1, 2, 0, 3)

    bd = jnp.concatenate([_block_diag(tiles_gpn(bb_r)), _block_diag(tiles_gpn(bb_i))], axis=2).astype(BF16)
    c_r = ssm_c_re[0].reshape(nj, TILE_GROUPS, SSM_P, SSM_N).transpose(0, 1, 3, 2)
    c_i = (-ssm_c_im[0]).reshape(nj, TILE_GROUPS, SSM_P, SSM_N).transpose(0, 1, 3, 2)
    cbd = jnp.concatenate([_block_diag(c_r), _block_diag(c_i)], axis=1).astype(BF16)
    bdt, cbdt = bd.transpose(0, 2, 1), cbd.transpose(0, 2, 1)
    lam = jnp.stack([lam_r.reshape(nj, TILE_STATES), lam_i.reshape(nj, TILE_STATES)], axis=1)
    dvec = ssm_d[0].reshape(nj, 1, LANES)
    t_len = _pick(l, 512, 8)
    y, hst = _s5_fwd(proj, off_u, bd, cbd, lam, dvec, t_len)

    tcs = _pick(ssm_w, 512)
    gy = _tile_call("gelu", lambda v: jax.nn.gelu(v), (ssm_w // tcs, nr), [y], [_t(tr, tcs)],
                    [_sds((l, ssm_w), BF16)], [_t(tr, tcs)])[0]
    glu = _matmul("ssm_glu", gy, full['w_ssm_glu'], "nn")

    tcd = 256 if d % 256 == 0 and off_ga % 256 == 0 else LANES
    assert d % tcd == 0 and off_ga % tcd == 0 and off_gs % tcd == 0
    mix_in_specs = [_t(tr, tcd), _t(tr, tcd, d), _t(tr, tcd), _t(tr, tcd, off_ga), _t(tr, tcd, off_gs)]
    mixed = _tile_call("gate_mix", _mix_fn, (d // tcd, nr), [glu, glu, attn, proj, proj], mix_in_specs,
                       [_sds((l, d), BF16)], [_t(tr, tcd)])[0]
    mixout = _matmul("mix_out", mixed, full['w_out'], "nn")

    def res_norm_fn(xv, mo, g1v, gv, scv, shv):
        x2v = xv + g1v * mo
        return x2v, _norm_mod(x2v, gv, scv, shv)

    x2, h2 = _tile_call("residual_norm_mod_ffn", res_norm_fn, (1, nr), [xs, mixout, g1, g_ffn, sc2, sh2],
                        [_t(tr, d), _t(tr, d), _v(d), _v(d), _v(d), _v(d)],
                        [_sds((l, d)), _sds((l, d), BF16)], [_t(tr, d), _t(tr, d)])
    up = _matmul("ffn_up", h2, full['w_ffn_up'], "nn", tn=1408)

    tcf = _pick(dff, 512)
    assert dff % tcf == 0
    ncf = dff // tcf

    taps = [conv_w[i:i + 1] for i in range(3)]

    def conv_gate(gp, gp_prev, w0, w1, w2, bv):
        prev = jnp.where(pl.program_id(1) == 0, 0.0, 1.0) * gp_prev
        ext = jnp.concatenate([prev, gp], axis=0)
        m1 = pltpu.roll(ext, 1, 0)[8:]
        m2 = pltpu.roll(ext, 2, 0)[8:]
        return w0 * m2 + w1 * m1 + w2 * gp + bv, m1, m2

    def convglu_fn(gp, gp_prev, val, w0, w1, w2, bv):
        gate, _, _ = conv_gate(gp, gp_prev, w0, w1, w2, bv)
        return gate * jax.nn.sigmoid(gate) * val

    act = _tile_call("conv_swiglu", convglu_fn, (ncf, nr), [up, up, up] + taps + [conv_b],
                     [_t(tr, tcf), _prev8(tr, tcf), _t(tr, tcf, dff)] + [_v(tcf)] * 4,
                     [_sds((l, dff), BF16)], [_t(tr, tcf)])[0]
    ffn = _matmul("ffn_down", act, full['w_ffn_down'], "nn")

    def final_fn(x2v, fv, g2v, gv, tv):
        rows = x2v.shape[0]

        def loss_of(x2a, fa, g2a, ga):
            out = _rms(x2a + g2a * fa, ga)
            err = out - tv
            return 0.5 * _colsum(jnp.mean(err * err, axis=-1, keepdims=True))

        loss, vjp = jax.vjp(loss_of, x2v, fv, _bc(g2v, rows), _bc(gv, rows))
        dx3, dffn, dg2, dgf = vjp(jnp.ones((1, 1), F32))
        return jnp.broadcast_to(loss, (1, LANES)), dx3, dffn, _colsum(dg2), _colsum(dgf)

    loss_p, dx3, dffn, dg2, dg_fin = _tile_call(
        "loss_final_norm", final_fn, (1, nrh), [x2, ffn, g2, g_fin, tgt],
        [_t(trh, d), _t(trh, d), _v(d), _v(d), _t(trh, d)],
        [_sds((1, LANES)), _sds((l, d)), _sds((l, d), BF16), _sds((1, d)), _sds((1, d))],
        [_v(LANES), _t(trh, d), _t(trh, d), _v(d), _v(d)], acc=(0, 3, 4))
    loss = lax.psum(loss_p[0, 0], ("x", "y", "c"))

    dact = _matmul("d_act", dffn, full['w_ffn_down'], "nt", tn=1408)
    dw_down = _matmul("dw_ffn_down", act, dffn, "tn", tm=1408)

    def convglu_bwd_fn(gp, gp_prev, val, da, w0, w1, w2, bv):
        gate, m1, m2 = conv_gate(gp, gp_prev, w0, w1, w2, bv)
        sg = jax.nn.sigmoid(gate)
        dgate = da * val * (sg * (1.0 + gate * (1.0 - sg)))
        return (dgate, da * (gate * sg), _colsum(dgate), _colsum(dgate * m2), _colsum(dgate * m1),
                _colsum(dgate * gp))

    dgate, dval, dconv_b, dcw0, dcw1, dcw2 = _tile_call(
        "conv_swiglu_bwd", convglu_bwd_fn, (ncf, nr), [up, up, up, dact] + taps + [conv_b],
        [_t(tr, tcf), _prev8(tr, tcf), _t(tr, tcf, dff), _t(tr, tcf)] + [_v(tcf)] * 4,
        [_sds((l, dff)), _sds((l, dff), BF16)] + [_sds((1, dff))] * 4,
        [_t(tr, tcf), _t(tr, tcf)] + [_v(tcf)] * 4, acc=(2, 3, 4, 5))

    def conv_t_fn(dg, dg_next, dv, w0, w1, w2):
        rows = dg.shape[0]
        nxt = jnp.where(pl.program_id(1) == pl.num_programs(1) - 1, 0.0, 1.0) * dg_next
        ext = jnp.concatenate([dg, nxt], axis=0)
        p1 = pltpu.roll(ext, rows + 8 - 1, 0)[:rows]
        p2 = pltpu.roll(ext, rows + 8 - 2, 0)[:rows]
        return jnp.stack([w2 * dg + w1 * p1 + w0 * p2, dv.astype(F32)], axis=0)

    dup = _tile_call("conv_transpose", conv_t_fn, (ncf, nr), [dgate, dgate, dval] + taps,
                     [_t(tr, tcf), _next8(tr, tcf, l), _t(tr, tcf)] + [_v(tcf)] * 3,
                     [_sds((2, l, dff), BF16)], [_st(tr, tcf)])[0]
    dh2 = _matmul("d_h2", dup, full['w_ffn_up'], "nt", tk=1408)
    dw_up = _matmul("dw_ffn_up", h2, dup, "tn", tn=1408, out_stack=N_DEV)

    def res_norm_bwd_fn(xv, mo, g1v, gv, scv, shv, dhv, dxv):
        rows = xv.shape[0]
        _, vjp = jax.vjp(res_norm_fn, xv, mo, _bc(g1v, rows), _bc(gv, rows), _bc(scv, rows), _bc(shv, rows))
        dx, dmo, dg1v, dgv, dscv, dshv = vjp((dxv, dhv))
        return dx, dmo, _colsum(dg1v), _colsum(dgv), _colsum(dscv), _colsum(dshv)

    dx2, dmixout, dg1, dg_ffn, dsc2, dsh2 = _tile_call(
        "residual_norm_mod_ffn_bwd", res_norm_bwd_fn, (1, nrh), [xs, mixout, g1, g_ffn, sc2, sh2, dh2, dx3],
        [_t(trh, d), _t(trh, d), _v(d), _v(d), _v(d), _v(d), _t(trh, d), _t(trh, d)],
        [_sds((l, d)), _sds((l, d), BF16)] + [_sds((1, d))] * 4,
        [_t(trh, d), _t(trh, d)] + [_v(d)] * 4, acc=(2, 3, 4, 5))

    dmixed = _matmul("d_mixed", dmixout, full['w_out'], "nt")
    dw_out = _matmul("dw_out", mixed, dmixout, "tn")

    def mix_bwd_fn(ga_, gb_, at, pa, ps, dm):
        _, vjp = jax.vjp(_mix_fn, ga_, gb_, at, pa, ps)
        da, db, dat, dpa, dps = vjp(dm)
        return jnp.stack([da, db], axis=0), dat, dpa, dps

    dglu, dattn, dga, dgs = _tile_call(
        "gate_mix_bwd", mix_bwd_fn, (d // tcd, nr), [glu, glu, attn, proj, proj, dmixed],
        mix_in_specs + [_t(tr, tcd)],
        [_sds((2, l, d), BF16)] + [_sds((l, d), BF16)] * 3, [_st(tr, tcd)] + [_t(tr, tcd)] * 3)

    dgy = _matmul("d_gelu_y", dglu, full['w_ssm_glu'], "nt")
    dw_glu = _matmul("dw_ssm_glu", gy, dglu, "tn", out_stack=N_DEV)

    def gelu_bwd_fn(yv, dv):
        _, vjp = jax.vjp(lambda z: jax.nn.gelu(z), yv)
        return vjp(dv)[0]

    dy = _tile_call("gelu_bwd", gelu_bwd_fn, (ssm_w // tcs, nr), [y, dgy], [_t(tr, tcs), _t(tr, tcs)],
                    [_sds((l, ssm_w))], [_t(tr, tcs)])[0]
    du, dbd, dcbdt, dlam, dd_tiles = _s5_bwd(proj, off_u, dy, hst, bd, bdt, cbdt, lam, dvec, t_len)

    def gpn_of(z):
        return z.transpose(2, 0, 1, 3).reshape(pgn)

    dbb_r = gpn_of(_diag_blocks(dbd[:, :, :TILE_STATES], SSM_P))
    dbb_i = gpn_of(_diag_blocks(dbd[:, :, TILE_STATES:], SSM_P))
    dc_re = _diag_blocks(dcbdt[:, :, :TILE_STATES], SSM_P).reshape(n_groups, SSM_P, SSM_N)
    dc_im = _diag_blocks(dcbdt[:, :, TILE_STATES:], SSM_P).reshape(n_groups, SSM_P, SSM_N)
    dlam_r, dlam_i = dlam[:, 0].reshape(gn), dlam[:, 1].reshape(gn)

    def disc_bwd_fn(ar, ai, ld, br, bi, dlr, dli, dbr, dbi):
        _, vjp = jax.vjp(_s5_disc_fn, ar, ai, ld, br, bi)
        return vjp((dlr, dli, dbr, dbi))

    da_re, da_im, dlog_dt, db_re, db_im = _tile_call(
        "s5_discretise_bwd", disc_bwd_fn, (1,), disc_ins + [dlam_r, dlam_i, dbb_r, dbb_i],
        disc_specs + [_full_spec(gn), _full_spec(gn), _full_spec(pgn), _full_spec(pgn)],
        [_sds(gn), _sds(gn), _sds((n_groups, 1)), _sds(pgn), _sds(pgn)], disc_specs)

    do2 = _matmul("d_attn_heads", dattn, full['w_attn_proj'], "nt")
    dw_ap = _matmul("dw_attn_proj", o2, dattn, "tn", out_stack=N_DEV)
    do_h = heads(do2.astype(BF16), hq)
    dq_h, dk_h, dv_h, dsink = _attn_bwd(qh, kh, vh, sinks3, do_h)

    def unheads(z):
        return z.transpose(1, 0, 2).reshape(l, z.shape[0] * HEAD_DIM)

    dproj = jnp.concatenate([unheads(dq_h), unheads(dk_h), unheads(dv_h), du, dga, dgs], axis=1)
    dh1 = _matmul("d_h1", dproj, full['w_in'], "nt", tk=640)
    dw_in = _matmul("dw_in", h1, dproj, "tn", tn=1280)

    def norm_bwd_fn(xv, gv, scv, shv, dhv, dxv):
        rows = xv.shape[0]
        _, vjp = jax.vjp(_norm_mod, xv, _bc(gv, rows), _bc(scv, rows), _bc(shv, rows))
        dx, dgv, dscv, dshv = vjp(dhv)
        return dx + dxv, _colsum(dgv), _colsum(dscv), _colsum(dshv)

    grad_x, dg_mix, dsc1, dsh1 = _tile_call(
        "norm_mod_mix_bwd", norm_bwd_fn, (1, nrh), [xs, g_mix, sc1, sh1, dh1, dx2],
        [_t(trh, d), _v(d), _v(d), _v(d), _t(trh, d), _t(trh, d)],
        [_sds((l, d))] + [_sds((1, d))] * 3, [_t(trh, d)] + [_v(d)] * 3, acc=(1, 2, 3))

    dmod = jnp.concatenate([dsh1, dsc1, dg1, dsh2, dsc2, dg2], axis=1)
    small = ['ada_b', 'norm_mix_g', 'attn_sinks', 'ssm_a_re', 'ssm_a_im', 'ssm_log_dt', 'ssm_b_re', 'ssm_b_im',
             'ssm_c_re', 'ssm_c_im', 'ssm_d', 'norm_ffn_g', 'ffn_conv_b', 'final_g']
    small_grads = {
        'ada_b': dmod, 'norm_mix_g': dg_mix, 'attn_sinks': dsink[:, 0, 0], 'ssm_a_re': da_re, 'ssm_a_im': da_im,
        'ssm_log_dt': dlog_dt, 'ssm_b_re': db_re.transpose(1, 2, 0), 'ssm_b_im': db_im.transpose(1, 2, 0),
        'ssm_c_re': dc_re, 'ssm_c_im': dc_im, 'ssm_d': dd_tiles, 'norm_ffn_g': dg_ffn, 'ffn_conv_b': dconv_b,
        'final_g': dg_fin}
    gs_pack, sm_offs = _pack([small_grads[k] for k in small], LANES, 8)
    (gs_all,) = _all_gather("gather_small_grads", [gs_pack])
    ws_pack, _ = _pack([given[k] for k in small], LANES, 8)
    ms_pack, _ = _pack([given['m_' + k] for k in small], LANES, 8)
    vs_pack, _ = _pack([given['v_' + k] for k in small], LANES, 8)
    small_out = _adamw("adamw_replicated", gs_all, ws_pack, ms_pack, vs_pack)

    dmod_all = _unpack(gs_all, sm_offs[0], (N_DEV * mod_n,), lead=(N_DEV,))
    dmod_mine = lax.dynamic_slice_in_dim(dmod_all, idx * mod_n, mod_n, axis=1)
    kpad = LANES - N_DEV
    cond_t = jnp.pad(cond_all.T, ((0, 0), (0, kpad)))
    dmod_pad = jnp.pad(dmod_mine, ((0, kpad), (0, 0)))
    g_ada_w = _matmul("dw_ada", cond_t, dmod_pad, "nn")
    ada_out = _adamw("adamw_ada_w", g_ada_w[None], ada_w[0], m_ada_w[0], v_ada_w[0])

    sharded = big + ['ffn_conv_w']
    dcw = jnp.concatenate([dcw0, dcw1, dcw2], axis=0)
    shard_in, shard_cw = w_in.shape[1:], ffn_conv_w.shape[1:]
    by_dest = {
        'w_in': dw_in.reshape(shard_in[0], N_DEV, shard_in[1]).transpose(1, 0, 2),
        'w_attn_proj': dw_ap, 'w_ssm_glu': dw_glu, 'w_ffn_up': dw_up,
        'w_out': dw_out.reshape((N_DEV,) + w_out.shape[1:]),
        'w_ffn_down': dw_down.reshape((N_DEV,) + w_ffn_down.shape[1:]),
        'ffn_conv_w': dcw.reshape(shard_cw[0], N_DEV, shard_cw[1]).transpose(1, 0, 2)}
    gds = [by_dest[k] for k in sharded]
    from_sib = _grad_to_sibling(gds)
    c_arr = jnp.reshape(ci, (1,)).astype(jnp.int32)
    place = jnp.stack([2 * xi + yi, ci]).astype(jnp.int32)
    sums = [_chip_sum("chip_sum_" + k, g, f, c_arr) for k, g, f in zip(sharded, gds, from_sib)]
    parts = _grad_to_chips(sums)
    sharded_out = {
        k: _adamw_sharded("adamw_" + k, p, g, f, given[k][0], given['m_' + k][0], given['v_' + k][0], place)
        for k, p, g, f in zip(sharded, parts, gds, from_sib)}

    results = [{}, {}, {}, {}]
    for which in range(4):
        for k, off in zip(small, sm_offs):
            results[which][k] = _unpack(small_out[which], off, given[k].shape)
        for k in sharded:
            results[which][k] = sharded_out[k][which][None]
        results[which]['ada_w'] = ada_out[which][None]
    outs = [loss, grad_x[None]]
    for which in range(4):
        outs += [results[which][k] for k in names]
    return tuple(outs)
```

```python
import functools
import math

import jax
import jax.numpy as jnp
from jax import lax
from jax.experimental import pallas as pl
from jax.experimental.pallas import tpu as pltpu

F32, BF16 = jnp.float32, jnp.bfloat16
MESH = pl.DeviceIdType.MESH
N_DEV = 8

HEAD_DIM = 64
N_KV_HEADS = 2
ATT_BLOCK = 128
NEG_INF = -1e30
SSM_P = 16
SSM_N = 64
LANES = 128
TILE_GROUPS = LANES // SSM_P
TILE_STATES = TILE_GROUPS * SSM_N
RMS_EPS = 1e-6
ADAM_LR, ADAM_B1, ADAM_B2, ADAM_EPS, ADAM_WD, ADAM_STEP = 0.001, 0.9, 0.999, 1e-08, 0.01, 10
VMEM_LIMIT = 56 * 1024 * 1024
MATMUL_VMEM_BUDGET = 40 * 1024 * 1024


def _params(n_axes):
    return pltpu.CompilerParams(dimension_semantics=("arbitrary",) * n_axes, vmem_limit_bytes=VMEM_LIMIT)


def _pick(dim, pref, align=128):
    if dim <= align:
        return dim
    t = (min(pref, dim) // align) * align
    while t > align and dim % t:
        t -= align
    assert dim % t == 0, (dim, pref, align)
    return t


def _dev():
    return lax.axis_index("x"), lax.axis_index("y"), lax.axis_index("c")


def _tile_call(name, fn, grid, ins, in_specs, out_shapes, out_specs, acc=()):
    n_in, n_out = len(ins), len(out_shapes)
    acc_axis = len(grid) - 1

    def body(*refs):
        vals = fn(*[r[...] for r in refs[:n_in]])
        if not isinstance(vals, (tuple, list)):
            vals = (vals,)
        assert len(vals) == n_out
        for i, (r, v) in enumerate(zip(refs[n_in:], vals)):
            v = v.astype(r.dtype)
            if i in acc:
                first = pl.program_id(acc_axis) == 0

                @pl.when(first)
                def _():
                    r[...] = v

                @pl.when(jnp.logical_not(first))
                def _():
                    r[...] += v
            else:
                r[...] = v

    return pl.pallas_call(
        body, grid=grid, in_specs=in_specs, out_specs=out_specs, out_shape=out_shapes, name=name,
        compiler_params=_params(len(grid)),
    )(*ins)


def _t(tr, tc, off=0):
    return pl.BlockSpec((tr, tc), lambda j, i: (i, j + off // tc))


def _v(tc, off=0, rows=1):
    return pl.BlockSpec((rows, tc), lambda j, i: (0, j + off // tc))


def _prev8(tr, tc, off=0):
    return pl.BlockSpec((8, tc), lambda j, i: (jnp.maximum(i * (tr // 8) - 1, 0), j + off // tc))


def _next8(tr, tc, nrows, off=0):
    return pl.BlockSpec((8, tc), lambda j, i: (jnp.minimum((i + 1) * (tr // 8), nrows // 8 - 1), j + off // tc))


def _st(tr, tc):
    return pl.BlockSpec((2, tr, tc), lambda j, i: (0, i, j))


def _bc(v, rows):
    return jnp.broadcast_to(v, (rows, v.shape[-1]))


def _colsum(v):
    return jnp.sum(v, axis=0, keepdims=True)


def _matmul(name, a, b, mode, out_dtype=F32, tm=1024, tn=1024, tk=None, out_stack=None):
    def dims(z):
        return (z.shape[-2], z.shape[-1] * (z.shape[0] if z.ndim == 3 else 1))

    ar, ac = dims(a)
    br, bc = dims(b)
    if mode == "nn":
        m, k, n = ar, ac, bc
        assert br == k
    elif mode == "nt":
        m, k, n = ar, ac, br
        assert bc == k
    else:
        m, k, n = ac, ar, bc
        assert br == k
    m_lim, k_lim, n_lim = [m], [k], [n]
    if a.ndim == 3:
        (m_lim if mode == "tn" else k_lim).append(a.shape[-1])
    if b.ndim == 3:
        (k_lim if mode == "nt" else n_lim).append(b.shape[-1])
    if out_stack:
        n_lim.append(n // out_stack)
    tm = _pick(functools.reduce(math.gcd, m_lim), tm)
    tn = _pick(functools.reduce(math.gcd, n_lim), tn)
    k_unit = functools.reduce(math.gcd, k_lim)
    if tk is None:
        sa, sb, so = a.dtype.itemsize, b.dtype.itemsize, jnp.dtype(out_dtype).itemsize
        fits = [t for t in range(LANES, k_unit + 1, LANES) if k_unit % t == 0 and
                2 * t * (tm * sa + tn * sb) + tm * tn * (2 * so + (4 if t < k else 0)) <= MATMUL_VMEM_BUDGET]
        tk = max(fits) if fits else _pick(k_unit, 512)
    else:
        tk = _pick(k_unit, tk)
    nk = k // tk

    def spec(z, brows, bcols, ridx, cidx):
        if z.ndim == 3:
            per = z.shape[-1] // bcols
            return pl.BlockSpec((None, brows, bcols),
                                lambda i, j, kk: (cidx(i, j, kk) // per, ridx(i, j, kk), cidx(i, j, kk) % per))
        return pl.BlockSpec((brows, bcols), lambda i, j, kk: (ridx(i, j, kk), cidx(i, j, kk)))

    gi = lambda i, j, kk: i
    gj = lambda i, j, kk: j
    gk = lambda i, j, kk: kk
    if mode == "nn":
        a_spec, b_spec = spec(a, tm, tk, gi, gk), spec(b, tk, tn, gk, gj)
        dn = (((1,), (0,)), ((), ()))
    elif mode == "nt":
        a_spec, b_spec = spec(a, tm, tk, gi, gk), spec(b, tn, tk, gj, gk)
        dn = (((1,), (1,)), ((), ()))
    else:
        a_spec, b_spec = spec(a, tk, tm, gk, gi), spec(b, tk, tn, gk, gj)
        dn = (((0,), (0,)), ((), ()))

    def body(a_ref, b_ref, o_ref, *acc):
        part = lax.dot_general(a_ref[...].astype(BF16), b_ref[...].astype(BF16), dn, preferred_element_type=F32)
        if nk == 1:
            o_ref[...] = part.astype(o_ref.dtype)
            return
        acc_ref, = acc
        kk = pl.program_id(2)

        @pl.when(kk == 0)
        def _():
            acc_ref[...] = part

        @pl.when(kk > 0)
        def _():
            acc_ref[...] += part

        @pl.when(kk == nk - 1)
        def _():
            o_ref[...] = acc_ref[...].astype(o_ref.dtype)

    if out_stack:
        per = (n // out_stack) // tn
        out_spec = pl.BlockSpec((None, tm, tn), lambda i, j, kk: (j // per, i, j % per))
        out_shape = jax.ShapeDtypeStruct((out_stack, m, n // out_stack), out_dtype)
    else:
        out_spec = pl.BlockSpec((tm, tn), lambda i, j, kk: (i, j))
        out_shape = jax.ShapeDtypeStruct((m, n), out_dtype)
    return pl.pallas_call(
        body, grid=(m // tm, n // tn, nk), in_specs=[a_spec, b_spec], out_specs=out_spec, out_shape=out_shape,
        scratch_shapes=[pltpu.VMEM((tm, tn), F32)] if nk > 1 else [], name=name, compiler_params=_params(3),
    )(a, b)


def _all_gather(name, arrs):
    n = len(arrs)

    def body(*refs):
        ins, outs = refs[:n], refs[n:2 * n]
        send_sems, recv_sems, local_sems = refs[2 * n:]
        x, y, c = _dev()
        me, sib = (x, y, c), (x, y, 1 - c)
        chips = [(1 - x, y), (x, 1 - y), (1 - x, 1 - y)]

        def slot(p):
            return 4 * p[0] + 2 * p[1] + p[2]

        def copy(a, k, block, to, src=None):
            dst = outs[a].at[slot(block)]
            return pltpu.make_async_remote_copy(
                src_ref=dst if src is None else src, dst_ref=dst,
                send_sem=send_sems.at[7 * a + k], recv_sem=recv_sems.at[7 * a + k],
                device_id=to, device_id_type=MESH)

        mine = [pltpu.make_async_copy(ins[a], outs[a].at[slot(me)], local_sems.at[a]) for a in range(n)]
        for cp in mine:
            cp.start()
        first = []
        for a in range(n):
            first.append(copy(a, 0, me, sib, src=ins[a]))
            first += [copy(a, 1 + j, me, (*chip, c), src=ins[a]) for j, chip in enumerate(chips)]
        for cp in first:
            cp.start()
        passed = []
        for j, chip in enumerate(chips):
            for a in range(n):
                copy(a, 1 + j, (*chip, c), me).wait_recv()
                cp = copy(a, 4 + j, (*chip, c), sib)
                cp.start()
                passed.append(cp)
        for a in range(n):
            copy(a, 0, sib, me).wait_recv()
            for j, chip in enumerate(chips):
                copy(a, 4 + j, (*chip, 1 - c), me).wait_recv()
        for cp in first + passed:
            cp.wait_send()
        for cp in mine:
            cp.wait()

    any_spec = pl.BlockSpec(memory_space=pl.ANY)
    return pl.pallas_call(
        body, in_specs=[any_spec] * n, out_specs=[any_spec] * n,
        out_shape=[jax.ShapeDtypeStruct((N_DEV,) + a.shape, a.dtype) for a in arrs],
        scratch_shapes=[pltpu.SemaphoreType.DMA((7 * n,)), pltpu.SemaphoreType.DMA((7 * n,)),
                        pltpu.SemaphoreType.DMA((n,))],
        name=name,
    )(*arrs)


def _grad_to_sibling(gds):
    n = len(gds)

    def body(*refs):
        g_refs, r_refs = refs[:n], refs[n:2 * n]
        send_sems, recv_sems = refs[2 * n:]
        x, y, c = _dev()
        cps = []
        for a in range(n):
            for k in range(4):
                cp = pltpu.make_async_remote_copy(
                    src_ref=g_refs[a].at[2 * k + (1 - c)], dst_ref=r_refs[a].at[k],
                    send_sem=send_sems.at[4 * a + k], recv_sem=recv_sems.at[4 * a + k],
                    device_id=(x, y, 1 - c), device_id_type=MESH)
                cp.start()
                cps.append(cp)
        for cp in cps:
            cp.wait()

    any_spec = pl.BlockSpec(memory_space=pl.ANY)
    return pl.pallas_call(
        body, in_specs=[any_spec] * n, out_specs=[any_spec] * n,
        out_shape=[jax.ShapeDtypeStruct((4,) + g.shape[1:], g.dtype) for g in gds],
        scratch_shapes=[pltpu.SemaphoreType.DMA((4 * n,)), pltpu.SemaphoreType.DMA((4 * n,))],
        name="grad_to_sibling",
    )(*gds)


def _chip_sum(name, gd, from_sib, c_arr):
    _, k, n = gd.shape
    tr = _pick(k, max(16, (1 << 20) // (4 * n)), 16)

    def body(c_ref, a_ref, b_ref, o_ref):
        o_ref[...] = (a_ref[...] + b_ref[...]).astype(o_ref.dtype)

    return pl.pallas_call(
        body,
        grid_spec=pltpu.PrefetchScalarGridSpec(
            num_scalar_prefetch=1, grid=(4, k // tr),
            in_specs=[pl.BlockSpec((1, tr, n), lambda q, i, cr: (2 * q + cr[0], i, 0)),
                      pl.BlockSpec((1, tr, n), lambda q, i, cr: (q, i, 0))],
            out_specs=pl.BlockSpec((1, tr, n), lambda q, i, cr: (q, i, 0))),
        out_shape=jax.ShapeDtypeStruct((4, k, n), BF16), name=name, compiler_params=_params(2),
    )(c_arr, gd, from_sib)


def _grad_to_chips(sums):
    n = len(sums)

    def body(*refs):
        s_refs, p_refs = refs[:n], refs[n:2 * n]
        send_sems, recv_sems, local_sems = refs[2 * n:]
        x, y, c = _dev()
        my_chip = 2 * x + y
        cps = []
        for a in range(n):
            local = pltpu.make_async_copy(s_refs[a].at[my_chip], p_refs[a].at[my_chip], local_sems.at[a])
            local.start()
            cps.append(local)
            for j, (px, py) in enumerate([(1 - x, y), (x, 1 - y), (1 - x, 1 - y)]):
                cp = pltpu.make_async_remote_copy(
                    src_ref=s_refs[a].at[2 * px + py], dst_ref=p_refs[a].at[my_chip],
                    send_sem=send_sems.at[3 * a + j], recv_sem=recv_sems.at[3 * a + j],
                    device_id=(px, py, c), device_id_type=MESH)
                cp.start()
                cps.append(cp)
        for cp in cps:
            cp.wait()

    any_spec = pl.BlockSpec(memory_space=pl.ANY)
    return pl.pallas_call(
        body, in_specs=[any_spec] * n, out_specs=[any_spec] * n,
        out_shape=[jax.ShapeDtypeStruct(s.shape, s.dtype) for s in sums],
        scratch_shapes=[pltpu.SemaphoreType.DMA((3 * n,)), pltpu.SemaphoreType.DMA((3 * n,)),
                        pltpu.SemaphoreType.DMA((n,))],
        name="grad_to_chips",
    )(*sums)


def _pack_rows(sizes, width, row_align):
    offs, r = [], 0
    for s in sizes:
        offs.append(r)
        r += -(-s // width)
    total = -(-r // row_align) * row_align
    return offs, total


def _pack(items, width, row_align, lead=()):
    nl = len(lead)
    sizes = [int(jnp.size(a)) // max(1, functools.reduce(lambda p, q: p * q, lead, 1)) for a in items]
    offs, total = _pack_rows(sizes, width, row_align)
    flat = []
    used = 0
    for a, s in zip(items, sizes):
        f = a.reshape(lead + (s,))
        pad = -(-s // width) * width - s
        if pad:
            f = jnp.pad(f, [(0, 0)] * nl + [(0, pad)])
        flat.append(f)
        used += s + pad
    tail = total * width - used
    if tail:
        flat.append(jnp.zeros(lead + (tail,), items[0].dtype))
    return jnp.concatenate(flat, axis=-1).reshape(lead + (total, width)), offs


def _unpack(packed, off, shape, lead=()):
    nl = len(lead)
    size = functools.reduce(lambda p, q: p * q, shape, 1)
    width = packed.shape[-1]
    rows = -(-size // width)
    blk = lax.slice_in_dim(packed, off, off + rows, axis=nl).reshape(lead + (rows * width,))
    return lax.slice_in_dim(blk, 0, size, axis=nl).reshape(lead + tuple(shape))


def _rms(x, g):
    return (x * lax.rsqrt(jnp.mean(x * x, axis=-1, keepdims=True) + RMS_EPS)) * g


def _norm_mod(x, g, sc, sh):
    return _rms(x, g) * (1.0 + sc) + sh


def _mix_fn(glu_a, glu_b, attn, ga, gs):
    return jax.nn.sigmoid(ga) * attn + jax.nn.sigmoid(gs) * (glu_a * jax.nn.sigmoid(glu_b))


def _s5_disc_fn(a_re, a_im, log_dt, b_re, b_im):
    dt = jnp.exp(log_dt)
    mag = jnp.exp(a_re * dt)
    lr, li = mag * jnp.cos(a_im * dt), mag * jnp.sin(a_im * dt)
    den = a_re * a_re + a_im * a_im
    zr = ((lr - 1.0) * a_re + li * a_im) / den
    zi = (li * a_re - (lr - 1.0) * a_im) / den
    return lr, li, zr[None] * b_re - zi[None] * b_im, zr[None] * b_im + zi[None] * b_re


def _adamw_fn(w, g, m, v):
    m = ADAM_B1 * m + (1.0 - ADAM_B1) * g
    v = ADAM_B2 * v + (1.0 - ADAM_B2) * jnp.square(g)
    m_hat = m / (1.0 - ADAM_B1 ** ADAM_STEP)
    v_hat = v / (1.0 - ADAM_B2 ** ADAM_STEP)
    delta = -ADAM_LR * (m_hat / (jnp.sqrt(v_hat) + ADAM_EPS) + ADAM_WD * w)
    return delta, m, v


def _adamw(name, parts, w, m, v):
    p, r, c = parts.shape
    tr = _pick(r, max(8, (1 << 21) // (4 * c * max(p, 2))), 8)

    def fn(pv, wv, mv, vv):
        g = pv[0]
        for i in range(1, p):
            g = g + pv[i]
        d, m2, v2 = _adamw_fn(wv, g, mv, vv)
        return g, d, m2, v2

    spec = pl.BlockSpec((tr, c), lambda i: (i, 0))
    return _tile_call(
        name, fn, (r // tr,), [parts, w, m, v],
        [pl.BlockSpec((p, tr, c), lambda i: (0, i, 0)), spec, spec, spec],
        [jax.ShapeDtypeStruct((r, c), F32)] * 4, [spec] * 4)


def _adamw_sharded(name, parts, gd, from_sib, w, m, v, place):
    _, k, n = parts.shape
    tr = _pick(k, max(16, (1 << 20) // (4 * n)), 16)

    def body(pl_ref, p_ref, a_ref, b_ref, w_ref, m_ref, v_ref, g_ref, d_ref, m2_ref, v2_ref):
        own = a_ref[0] + b_ref[0]
        g = None
        for q in range(4):
            term = jnp.where(pl_ref[0] == q, own, p_ref[q].astype(F32))
            g = term if g is None else g + term
        d, m2, v2 = _adamw_fn(w_ref[...], g, m_ref[...], v_ref[...])
        g_ref[...] = g
        d_ref[...] = d
        m2_ref[...] = m2
        v2_ref[...] = v2

    spec = pl.BlockSpec((tr, n), lambda i, pr: (i, 0))
    return pl.pallas_call(
        body,
        grid_spec=pltpu.PrefetchScalarGridSpec(
            num_scalar_prefetch=1, grid=(k // tr,),
            in_specs=[pl.BlockSpec((4, tr, n), lambda i, pr: (0, i, 0)),
                      pl.BlockSpec((1, tr, n), lambda i, pr: (2 * pr[0] + pr[1], i, 0)),
                      pl.BlockSpec((1, tr, n), lambda i, pr: (pr[0], i, 0)),
                      spec, spec, spec],
            out_specs=[spec] * 4),
        out_shape=[jax.ShapeDtypeStruct((k, n), F32)] * 4, name=name, compiler_params=_params(1),
    )(place, parts, gd, from_sib, w, m, v)


def _attn_mask(n, rows):
    qi = lax.broadcasted_iota(jnp.int32, (rows, 2 * ATT_BLOCK), 0) & (ATT_BLOCK - 1)
    kj = lax.broadcasted_iota(jnp.int32, (rows, 2 * ATT_BLOCK), 1)
    rel = qi + ATT_BLOCK - kj
    return (rel >= 0) & (rel < ATT_BLOCK) & ((kj >= ATT_BLOCK) | (n > 0))


def _attn_probs(q, k, sink, mask):
    s = lax.dot_general(q, k, (((1,), (1,)), ((), ())), preferred_element_type=F32) * (HEAD_DIM ** -0.5)
    s = jnp.where(mask, s, NEG_INF)
    m = jnp.maximum(jnp.max(s, axis=-1, keepdims=True), sink)
    p = jnp.exp(s - m)
    e_sink = jnp.exp(sink - m)
    den = jnp.sum(p, axis=-1, keepdims=True) + e_sink
    return p / den, e_sink / den


def _attn_specs(qpk):
    blk = ATT_BLOCK
    q_spec = pl.BlockSpec((qpk, blk, HEAD_DIM), lambda h, n: (h, n, 0))
    cur = pl.BlockSpec((1, blk, HEAD_DIM), lambda h, n: (h, n, 0))
    prev = pl.BlockSpec((1, blk, HEAD_DIM), lambda h, n: (h, jnp.maximum(n - 1, 0), 0))
    sink_spec = pl.BlockSpec((1, qpk * blk, 1), lambda h, n: (h, 0, 0))
    return q_spec, cur, prev, sink_spec


def _attn_fwd(q, k, v, sinks):
    hq, l, _ = q.shape
    qpk = hq // N_KV_HEADS
    nb = l // ATT_BLOCK
    rows = qpk * ATT_BLOCK
    q_spec, cur, prev, sink_spec = _attn_specs(qpk)

    def body(q_ref, kp_ref, kc_ref, vp_ref, vc_ref, sink_ref, o_ref):
        mask = _attn_mask(pl.program_id(1), rows)
        kk = jnp.concatenate([kp_ref[0], kc_ref[0]], axis=0).astype(BF16)
        vv = jnp.concatenate([vp_ref[0], vc_ref[0]], axis=0).astype(BF16)
        p, _ = _attn_probs(q_ref[...].reshape(rows, HEAD_DIM).astype(BF16), kk, sink_ref[0], mask)
        o = jnp.dot(p.astype(BF16), vv, preferred_element_type=F32)
        o_ref[...] = o.reshape(qpk, ATT_BLOCK, HEAD_DIM).astype(o_ref.dtype)

    return pl.pallas_call(
        body, grid=(N_KV_HEADS, nb), in_specs=[q_spec, prev, cur, prev, cur, sink_spec],
        out_specs=q_spec, out_shape=jax.ShapeDtypeStruct((hq, l, HEAD_DIM), BF16),
        name="attn_fwd", compiler_params=_params(2),
    )(q, k, k, v, v, sinks)


def _attn_bwd(q, k, v, sinks, do):
    hq, l, _ = q.shape
    qpk = hq // N_KV_HEADS
    nb = l // ATT_BLOCK
    blk = ATT_BLOCK
    rows = qpk * blk
    q_spec, cur, prev, sink_spec = _attn_specs(qpk)
    part_spec = pl.BlockSpec((1, 1, 2 * blk, HEAD_DIM), lambda h, n: (h, n, 0, 0))
    dsink_spec = pl.BlockSpec((qpk, 1, LANES), lambda h, n: (h, 0, 0))
    tn = (((0,), (0,)), ((), ()))

    def body(q_ref, do_ref, kp_ref, kc_ref, vp_ref, vc_ref, sink_ref, dq_ref, dkp_ref, dvp_ref, dsink_ref):
        n = pl.program_id(1)
        mask = _attn_mask(n, rows)
        kk = jnp.concatenate([kp_ref[0], kc_ref[0]], axis=0).astype(BF16)
        vv = jnp.concatenate([vp_ref[0], vc_ref[0]], axis=0).astype(BF16)
        qb = q_ref[...].reshape(rows, HEAD_DIM).astype(BF16)
        do32 = do_ref[...].astype(F32).reshape(rows, HEAD_DIM)
        dob = do32.astype(BF16)
        p, p_sink = _attn_probs(qb, kk, sink_ref[0], mask)
        pb = p.astype(BF16)
        o = jnp.dot(pb, vv, preferred_element_type=F32)
        delta = jnp.sum(do32 * o, axis=-1, keepdims=True)
        dp = lax.dot_general(dob, vv, (((1,), (1,)), ((), ())), preferred_element_type=F32)
        ds = (p * (dp - delta) * (HEAD_DIM ** -0.5)).astype(BF16)
        dq = jnp.dot(ds, kk, preferred_element_type=F32)
        dq_ref[...] = dq.reshape(qpk, blk, HEAD_DIM).astype(dq_ref.dtype)
        dkp_ref[0, 0] = lax.dot_general(ds, qb, tn, preferred_element_type=F32)
        dvp_ref[0, 0] = lax.dot_general(pb, dob, tn, preferred_element_type=F32)
        dsr = p_sink * delta
        for g in range(qpk):
            dsg = jnp.broadcast_to(-_colsum(dsr[g * blk:(g + 1) * blk]), (1, LANES))

            @pl.when(n == 0)
            def _():
                dsink_ref[g] = dsg

            @pl.when(n > 0)
            def _():
                dsink_ref[g] += dsg


    part_shape = jax.ShapeDtypeStruct((N_KV_HEADS, nb, 2 * blk, HEAD_DIM), F32)
    dq, dkp, dvp, dsink = pl.pallas_call(
        body, grid=(N_KV_HEADS, nb), in_specs=[q_spec, q_spec, prev, cur, prev, cur, sink_spec],
        out_specs=[q_spec, part_spec, part_spec, dsink_spec],
        out_shape=[jax.ShapeDtypeStruct((hq, l, HEAD_DIM), BF16), part_shape, part_shape,
                   jax.ShapeDtypeStruct((hq, 1, LANES), F32)],
        name="attn_bwd", compiler_params=_params(2),
    )(q, do, k, k, v, v, sinks)

    def combine(a_cur, a_nxt, b_cur, b_nxt):
        last = pl.program_id(1) == nb - 1
        keep = jnp.where(last, 0.0, 1.0)
        return (a_cur[0, 0, blk:] + keep * a_nxt[0, 0, :blk])[None], (b_cur[0, 0, blk:] + keep * b_nxt[0, 0, :blk])[None]

    nxt_spec = pl.BlockSpec((1, 1, 2 * blk, HEAD_DIM), lambda h, n: (h, jnp.minimum(n + 1, nb - 1), 0, 0))
    kv_shape = jax.ShapeDtypeStruct((N_KV_HEADS, l, HEAD_DIM), BF16)
    dk, dv = _tile_call("attn_dkv", combine, (N_KV_HEADS, nb), [dkp, dkp, dvp, dvp],
                        [part_spec, nxt_spec, part_spec, nxt_spec], [kv_shape, kv_shape], [cur, cur])
    return dq, dk, dv, dsink


def _block_diag(m):
    j, gl, a, b = m.shape
    eye = jnp.eye(gl, dtype=m.dtype)
    return (m[:, :, :, None, :] * eye[None, :, None, :, None]).reshape(j, gl * a, gl * b)


def _diag_blocks(z, a):
    j = z.shape[0]
    gl = z.shape[1] // a
    b = z.shape[2] // gl
    d = jnp.diagonal(z.reshape(j, gl, a, gl, b), axis1=1, axis2=3)
    return d.transpose(0, 3, 1, 2)


def _s5_permute(src_ref, dst_ref, t_len):
    seg = t_len // 8

    def body(k, carry):
        dst_ref[pl.ds(pl.multiple_of(8 * k, 8), 8), :] = src_ref[pl.ds(k, 8, stride=seg), :]
        return carry

    lax.fori_loop(0, seg, body, 0)


def _s5_unpermuted_rows(perm_ref, m, t_len):
    per_seg = t_len // 64
    return perm_ref[pl.ds(64 * (m % per_seg) + m // per_seg, 8, stride=8), :]


def _s5_powers(p_ref, lr, li, seg):
    hs = TILE_STATES

    def step(k, carry):
        pr, pi = carry
        p_ref[pl.ds(k, 1), 0:hs] = pr
        p_ref[pl.ds(k, 1), hs:2 * hs] = pi
        return lr * pr - li * pi, lr * pi + li * pr

    lax.fori_loop(0, seg, step, (lr, li))


def _s5_local_scan(x_ref, base, lr, li, seg, reverse):
    hs = TILE_STATES
    lr8, li8 = jnp.broadcast_to(lr, (8, hs)), jnp.broadcast_to(li, (8, hs))
    if reverse:
        li8 = -li8

    def step(i, carry):
        hr, hi = carry
        k = seg - 1 - i if reverse else i
        rows = pl.ds(pl.multiple_of(base + 8 * k, 8), 8)
        nr = lr8 * hr - li8 * hi + x_ref[rows, 0:hs]
        ni = lr8 * hi + li8 * hr + x_ref[rows, hs:2 * hs]
        x_ref[rows, 0:hs] = nr
        x_ref[rows, hs:2 * hs] = ni
        return nr, ni

    zero = jnp.zeros((8, hs), F32)
    return lax.fori_loop(0, seg, step, (zero, zero), unroll=2)


def _s5_carries(c_ref, e_ref, ends, start, pw_r, pw_i, reverse):
    hs = TILE_STATES
    e_ref[:, 0:hs] = ends[0]
    e_ref[:, hs:2 * hs] = ends[1]
    cr, ci = start
    if reverse:
        pw_i = -pw_i
    for s in (range(7, -1, -1) if reverse else range(8)):
        c_ref[s:s + 1, 0:hs] = cr
        c_ref[s:s + 1, hs:2 * hs] = ci
        er, ei = e_ref[s:s + 1, 0:hs], e_ref[s:s + 1, hs:2 * hs]
        cr, ci = er + pw_r * cr - pw_i * ci, ei + pw_r * ci + pw_i * cr
    return cr, ci


def _s5_states(u_perm_b16, bd_ref, x_ref, base, c_ref, e_ref, p_ref, lr, li, h_in, t_len):
    hs = TILE_STATES
    seg = t_len // 8
    x_ref[pl.ds(base, t_len), :] = jnp.dot(u_perm_b16, bd_ref[0], preferred_element_type=F32)
    ends = _s5_local_scan(x_ref, base, lr, li, seg, False)
    pw_r, pw_i = p_ref[seg - 1:seg, 0:hs], p_ref[seg - 1:seg, hs:2 * hs]
    h_out = _s5_carries(c_ref, e_ref, ends, h_in, pw_r, pw_i, False)
    cr, ci = c_ref[:, 0:hs], c_ref[:, hs:2 * hs]

    def fix(k, carry):
        rows = pl.ds(pl.multiple_of(base + 8 * k, 8), 8)
        pr, pi = p_ref[pl.ds(k, 1), 0:hs], p_ref[pl.ds(k, 1), hs:2 * hs]
        x_ref[rows, 0:hs] += pr * cr - pi * ci
        x_ref[rows, hs:2 * hs] += pr * ci + pi * cr
        return carry

    lax.fori_loop(0, seg, fix, 0, unroll=2)
    return h_out


def _s5_fwd(proj, u_off, bd, cbd, lam, dvec, t_len):
    l = proj.shape[0]
    nj = bd.shape[0]
    nch = l // t_len
    hs = TILE_STATES
    ub = u_off // LANES
    seg = t_len // 8
    assert t_len % 64 == 0

    def body(u_ref, bd_ref, cbd_ref, lam_ref, d_ref, y_ref, hst_ref, x_ref, h_ref, p_ref, c_ref, e_ref, up_ref, yp_ref):
        lr, li = lam_ref[0, 0:1, :], lam_ref[0, 1:2, :]

        @pl.when(pl.program_id(1) == 0)
        def _():
            h_ref[...] = jnp.zeros_like(h_ref)
            _s5_powers(p_ref, lr, li, seg)

        hst_ref[0, 0] = h_ref[...]
        _s5_permute(u_ref, up_ref, t_len)
        h_out = _s5_states(up_ref[...].astype(BF16), bd_ref, x_ref, 0, c_ref, e_ref, p_ref, lr, li,
                           (h_ref[:, 0:hs], h_ref[:, hs:2 * hs]), t_len)
        h_ref[:, 0:hs] = h_out[0]
        h_ref[:, hs:2 * hs] = h_out[1]
        yp_ref[...] = jnp.dot(x_ref[...].astype(BF16), cbd_ref[0], preferred_element_type=F32)
        dv = d_ref[0]

        def out(m, carry):
            rows = pl.ds(pl.multiple_of(8 * m, 8), 8)
            y_ref[rows, :] = _s5_unpermuted_rows(yp_ref, m, t_len) + dv * u_ref[rows, :]
            return carry

        lax.fori_loop(0, seg, out, 0)

    return pl.pallas_call(
        body, grid=(nj, nch),
        in_specs=[pl.BlockSpec((t_len, LANES), lambda j, c: (c, ub + j)),
                  pl.BlockSpec((1, LANES, 2 * hs), lambda j, c: (j, 0, 0)),
                  pl.BlockSpec((1, 2 * hs, LANES), lambda j, c: (j, 0, 0)),
                  pl.BlockSpec((1, 2, hs), lambda j, c: (j, 0, 0)),
                  pl.BlockSpec((1, 1, LANES), lambda j, c: (j, 0, 0))],
        out_specs=[pl.BlockSpec((t_len, LANES), lambda j, c: (c, j)),
                   pl.BlockSpec((1, 1, 1, 2 * hs), lambda j, c: (j, c, 0, 0))],
        out_shape=[jax.ShapeDtypeStruct((l, nj * LANES), F32),
                   jax.ShapeDtypeStruct((nj, nch, 1, 2 * hs), F32)],
        scratch_shapes=[pltpu.VMEM((t_len, 2 * hs), F32), pltpu.VMEM((1, 2 * hs), F32),
                        pltpu.VMEM((seg, 2 * hs), F32), pltpu.VMEM((8, 2 * hs), F32), pltpu.VMEM((8, 2 * hs), F32),
                        pltpu.VMEM((t_len, LANES), F32), pltpu.VMEM((t_len, LANES), F32)],
        name="s5_fwd", compiler_params=_params(2),
    )(proj, bd, cbd, lam, dvec)


def _s5_bwd(proj, u_off, dy, hst, bd, bdt, cbdt, lam, dvec, t_len):
    l = proj.shape[0]
    nj = bd.shape[0]
    nch = l // t_len
    hs = TILE_STATES
    ub = u_off // LANES
    seg = t_len // 8
    tn = (((0,), (0,)), ((), ()))
    assert t_len % 64 == 0

    def body(u_ref, dy_ref, hst_ref, bd_ref, bdt_ref, cbdt_ref, lam_ref, d_ref,
             du_ref, dbd_ref, dcbdt_ref, dlam_ref, dd_ref,
             x_ref, g_ref, gc_ref, p_ref, c_ref, e_ref, up_ref, dyp_ref, dup_ref):
        first = pl.program_id(1) == 0
        lr, li = lam_ref[0, 0:1, :], lam_ref[0, 1:2, :]

        @pl.when(first)
        def _():
            gc_ref[...] = jnp.zeros_like(gc_ref)
            _s5_powers(p_ref, lr, li, seg)

        _s5_permute(u_ref, up_ref, t_len)
        _s5_permute(dy_ref, dyp_ref, t_len)
        ub16, dyb16 = up_ref[...].astype(BF16), dyp_ref[...].astype(BF16)
        h0 = hst_ref[0, 0]
        _s5_states(ub16, bd_ref, x_ref, 8, c_ref, e_ref, p_ref, lr, li, (h0[:, 0:hs], h0[:, hs:2 * hs]), t_len)
        x_ref[0:8, :] = c_ref[...]
        g_ref[...] = jnp.dot(dyb16, cbdt_ref[0], preferred_element_type=F32)
        starts = _s5_local_scan(g_ref, 0, lr, li, seg, True)
        pw_r, pw_i = p_ref[seg - 1:seg, 0:hs], p_ref[seg - 1:seg, hs:2 * hs]
        g_out = _s5_carries(c_ref, e_ref, starts, (gc_ref[:, 0:hs], gc_ref[:, hs:2 * hs]), pw_r, pw_i, True)
        gc_ref[:, 0:hs] = g_out[0]
        gc_ref[:, hs:2 * hs] = g_out[1]
        cr, ci = c_ref[:, 0:hs], c_ref[:, hs:2 * hs]

        def fix(k, carry):
            alr, ali = carry
            rows = pl.ds(pl.multiple_of(8 * k, 8), 8)
            pr, pi = p_ref[pl.ds(seg - 1 - k, 1), 0:hs], p_ref[pl.ds(seg - 1 - k, 1), hs:2 * hs]
            gr = g_ref[rows, 0:hs] + pr * cr + pi * ci
            gi = g_ref[rows, hs:2 * hs] + pr * ci - pi * cr
            g_ref[rows, 0:hs] = gr
            g_ref[rows, hs:2 * hs] = gi
            hpr, hpi = x_ref[rows, 0:hs], x_ref[rows, hs:2 * hs]
            return alr + gr * hpr + gi * hpi, ali + gi * hpr - gr * hpi

        zero = jnp.zeros((8, hs), F32)
        alr, ali = lax.fori_loop(0, seg, fix, (zero, zero), unroll=2)
        alr, ali = _colsum(alr), _colsum(ali)
        g = g_ref[...].astype(BF16)
        h = x_ref[pl.ds(8, t_len), :].astype(BF16)
        dup_ref[...] = jnp.dot(g, bdt_ref[0], preferred_element_type=F32)
        dv = d_ref[0]

        def out(m, carry):
            rows = pl.ds(pl.multiple_of(8 * m, 8), 8)
            du_ref[rows, :] = (_s5_unpermuted_rows(dup_ref, m, t_len) + dv * dy_ref[rows, :]).astype(du_ref.dtype)
            return carry

        lax.fori_loop(0, seg, out, 0)
        sign = jnp.where(lax.broadcasted_iota(jnp.int32, (1, 2 * hs), 1) < hs, 1.0, -1.0)
        dbd = lax.dot_general(ub16, g, tn, preferred_element_type=F32)
        dcbdt = lax.dot_general(dyb16, h, tn, preferred_element_type=F32) * sign
        ddv = _colsum(dy_ref[...] * u_ref[...])

        @pl.when(first)
        def _():
            dbd_ref[0] = dbd
            dcbdt_ref[0] = dcbdt
            dlam_ref[0, 0:1, :] = alr
            dlam_ref[0, 1:2, :] = ali
            dd_ref[0] = ddv

        @pl.when(jnp.logical_not(first))
        def _():
            dbd_ref[0] += dbd
            dcbdt_ref[0] += dcbdt
            dlam_ref[0, 0:1, :] += alr
            dlam_ref[0, 1:2, :] += ali
            dd_ref[0] += ddv

    rev = lambda c: nch - 1 - c
    wide = pl.BlockSpec((1, LANES, 2 * hs), lambda j, c: (j, 0, 0))
    tall = pl.BlockSpec((1, 2 * hs, LANES), lambda j, c: (j, 0, 0))
    return pl.pallas_call(
        body, grid=(nj, nch),
        in_specs=[pl.BlockSpec((t_len, LANES), lambda j, c: (rev(c), ub + j)),
                  pl.BlockSpec((t_len, LANES), lambda j, c: (rev(c), j)),
                  pl.BlockSpec((1, 1, 1, 2 * hs), lambda j, c: (j, rev(c), 0, 0)),
                  wide, tall, wide,
                  pl.BlockSpec((1, 2, hs), lambda j, c: (j, 0, 0)),
                  pl.BlockSpec((1, 1, LANES), lambda j, c: (j, 0, 0))],
        out_specs=[pl.BlockSpec((t_len, LANES), lambda j, c: (rev(c), j)),
                   wide, wide,
                   pl.BlockSpec((1, 2, hs), lambda j, c: (j, 0, 0)),
                   pl.BlockSpec((1, 1, LANES), lambda j, c: (j, 0, 0))],
        out_shape=[jax.ShapeDtypeStruct((l, nj * LANES), BF16),
                   jax.ShapeDtypeStruct((nj, LANES, 2 * hs), F32),
                   jax.ShapeDtypeStruct((nj, LANES, 2 * hs), F32),
                   jax.ShapeDtypeStruct((nj, 2, hs), F32),
                   jax.ShapeDtypeStruct((nj, 1, LANES), F32)],
        scratch_shapes=[pltpu.VMEM((t_len + 8, 2 * hs), F32), pltpu.VMEM((t_len, 2 * hs), F32),
                        pltpu.VMEM((1, 2 * hs), F32), pltpu.VMEM((seg, 2 * hs), F32),
                        pltpu.VMEM((8, 2 * hs), F32), pltpu.VMEM((8, 2 * hs), F32),
                        pltpu.VMEM((t_len, LANES), F32), pltpu.VMEM((t_len, LANES), F32),
                        pltpu.VMEM((t_len, LANES), F32)],
        name="s5_bwd", compiler_params=_params(2),
    )(proj, dy, hst, bd, bdt, cbdt, lam, dvec)


def _full_spec(shape):
    nd = len(shape)
    return pl.BlockSpec(tuple(shape), lambda i: (0,) * nd)


def _sds(shape, dtype=F32):
    return jax.ShapeDtypeStruct(tuple(shape), dtype)


def kernel(x, c, ada_w, ada_b, norm_mix_g, w_in, attn_sinks, w_attn_proj, ssm_a_re, ssm_a_im, ssm_log_dt, ssm_b_re, ssm_b_im, ssm_c_re, ssm_c_im, ssm_d, w_ssm_glu, w_out, norm_ffn_g, w_ffn_up, ffn_conv_w, ffn_conv_b, w_ffn_down, final_g, loss_target, m_ada_w, m_ada_b, m_norm_mix_g, m_w_in, m_attn_sinks, m_w_attn_proj, m_ssm_a_re, m_ssm_a_im, m_ssm_log_dt, m_ssm_b_re, m_ssm_b_im, m_ssm_c_re, m_ssm_c_im, m_ssm_d, m_w_ssm_glu, m_w_out, m_norm_ffn_g, m_w_ffn_up, m_ffn_conv_w, m_ffn_conv_b, m_w_ffn_down, m_final_g, v_ada_w, v_ada_b, v_norm_mix_g, v_w_in, v_attn_sinks, v_w_attn_proj, v_ssm_a_re, v_ssm_a_im, v_ssm_log_dt, v_ssm_b_re, v_ssm_b_im, v_ssm_c_re, v_ssm_c_im, v_ssm_d, v_w_ssm_glu, v_w_out, v_norm_ffn_g, v_w_ffn_up, v_ffn_conv_w, v_ffn_conv_b, v_w_ffn_down, v_final_g):
    given = dict(locals())
    names = ['ada_w', 'ada_b', 'norm_mix_g', 'w_in', 'attn_sinks', 'w_attn_proj', 'ssm_a_re', 'ssm_a_im',
             'ssm_log_dt', 'ssm_b_re', 'ssm_b_im', 'ssm_c_re', 'ssm_c_im', 'ssm_d', 'w_ssm_glu', 'w_out',
             'norm_ffn_g', 'w_ffn_up', 'ffn_conv_w', 'ffn_conv_b', 'w_ffn_down', 'final_g']

    xs = x[0]
    tgt = loss_target[0]
    l, d = xs.shape
    attn_w = w_attn_proj.shape[1]
    ssm_w = w_ssm_glu.shape[1]
    hq = attn_sinks.shape[1]
    qpk = hq // N_KV_HEADS
    kv_w = N_KV_HEADS * HEAD_DIM
    n_groups = ssm_a_re.shape[1]
    dff = ffn_conv_b.shape[1]
    in_w = attn_w + 2 * kv_w + ssm_w + 2 * d
    nj = ssm_w // LANES
    off_k, off_v, off_u = attn_w, attn_w + kv_w, attn_w + 2 * kv_w
    off_ga, off_gs = off_u + ssm_w, off_u + ssm_w + d
    assert hq * HEAD_DIM == attn_w and n_groups * SSM_P == ssm_w and l % ATT_BLOCK == 0

    xi, yi, ci = _dev()
    idx = 4 * xi + 2 * yi + ci

    row_sharded = {'w_out': (d, d), 'w_ffn_down': (dff, d)}
    big = ['w_in', 'w_attn_proj', 'w_ssm_glu', 'w_out', 'w_ffn_up', 'w_ffn_down']
    spack, s_offs = _pack([c, ffn_conv_w[0]], LANES, 8)
    gathered = _all_gather("gather_weights", [given[k][0].astype(BF16) for k in big] + [spack])
    sg = gathered[-1]
    full = dict(zip(big, gathered[:-1]))
    full['w_in'] = full['w_in'].transpose(1, 0, 2).reshape(d, in_w)
    for k in row_sharded:
        full[k] = full[k].reshape(row_sharded[k])
    c_all = _unpack(sg, s_offs[0], (d,), lead=(N_DEV,))
    conv_w = _unpack(sg, s_offs[1], ffn_conv_w.shape[1:], lead=(N_DEV,)).transpose(1, 0, 2).reshape(3, dff)
    conv_b = ffn_conv_b

    mod_n = ada_w.shape[2]
    tcm = _pick(mod_n, 512)
    ada_b_mine = lax.dynamic_slice_in_dim(ada_b, idx * mod_n, mod_n, axis=1)

    def modpart_fn(cv, wv, bv):
        cond = cv * jax.nn.sigmoid(cv)
        return jnp.dot(cond.astype(BF16), wv.astype(BF16), preferred_element_type=F32) + bv, cond

    modp, cond_all = _tile_call(
        "ada_rows", modpart_fn, (mod_n // tcm,), [c_all, ada_w[0], ada_b_mine],
        [pl.BlockSpec((N_DEV, d), lambda j: (0, 0)), pl.BlockSpec((d, tcm), lambda j: (0, j)),
         pl.BlockSpec((1, tcm), lambda j: (0, j))],
        [_sds((N_DEV, mod_n)), _sds((N_DEV, d))],
        [pl.BlockSpec((N_DEV, tcm), lambda j: (0, j)), pl.BlockSpec((N_DEV, d), lambda j: (0, 0))])
    (modg,) = _all_gather("gather_ada_rows", [modp])
    mod = lax.dynamic_index_in_dim(modg, idx, axis=1, keepdims=False).reshape(1, N_DEV * mod_n)
    sh1, sc1, g1, sh2, sc2, g2 = [mod[:, i * d:(i + 1) * d] for i in range(6)]

    tr = _pick(l, 256, 8)
    trh = _pick(l, 128, 8)
    nr, nrh = l // tr, l // trh
    g_mix, g_ffn, g_fin = norm_mix_g, norm_ffn_g, final_g.reshape(1, d)

    h1 = _tile_call("norm_mod_mix", _norm_mod, (1, nr), [xs, g_mix, sc1, sh1],
                    [_t(tr, d), _v(d), _v(d), _v(d)], [_sds((l, d), BF16)], [_t(tr, d)])[0]
    proj = _matmul("proj_in", h1, full['w_in'], "nn", tn=1280)

    def heads(z, n):
        return z.reshape(l, n, HEAD_DIM).transpose(1, 0, 2)

    qh = heads(proj[:, :attn_w], hq)
    kh = heads(proj[:, off_k:off_k + kv_w], N_KV_HEADS)
    vh = heads(proj[:, off_v:off_v + kv_w], N_KV_HEADS)
    sinks3 = jnp.repeat(attn_sinks.reshape(N_KV_HEADS, qpk), ATT_BLOCK, axis=1)[..., None]
    o_h = _attn_fwd(qh, kh, vh, sinks3)
    o2 = o_h.transpose(1, 0, 2).reshape(l, attn_w)
    attn = _matmul("attn_proj", o2, full['w_attn_proj'], "nn")

    gn = (n_groups, SSM_N)
    pgn = (SSM_P, n_groups, SSM_N)
    a_re, a_im, log_dt = ssm_a_re[0], ssm_a_im[0], ssm_log_dt[0].reshape(n_groups, 1)
    b_re, b_im = ssm_b_re[0].transpose(2, 0, 1), ssm_b_im[0].transpose(2, 0, 1)
    disc_ins = [a_re, a_im, log_dt, b_re, b_im]
    disc_specs = [_full_spec(gn), _full_spec(gn), _full_spec((n_groups, 1)), _full_spec(pgn), _full_spec(pgn)]
    lam_r, lam_i, bb_r, bb_i = _tile_call(
        "s5_discretise", _s5_disc_fn, (1,), disc_ins, disc_specs,
        [_sds(gn), _sds(gn), _sds(pgn), _sds(pgn)],
        [_full_spec(gn), _full_spec(gn), _full_spec(pgn), _full_spec(pgn)])

    def tiles_gpn(z):
        return z.reshape(SSM_P, nj, TILE_GROUPS, SSM_N).transpose(1, 2, 0, 3)

    bd = jnp.concatenate([_block_diag(tiles_gpn(bb_r)), _block_diag(tiles_gpn(bb_i))], axis=2).astype(BF16)
    c_r = ssm_c_re[0].reshape(nj, TILE_GROUPS, SSM_P, SSM_N).transpose(0, 1, 3, 2)
    c_i = (-ssm_c_im[0]).reshape(nj, TILE_GROUPS, SSM_P, SSM_N).transpose(0, 1, 3, 2)
    cbd = jnp.concatenate([_block_diag(c_r), _block_diag(c_i)], axis=1).astype(BF16)
    bdt, cbdt = bd.transpose(0, 2, 1), cbd.transpose(0, 2, 1)
    lam = jnp.stack([lam_r.reshape(nj, TILE_STATES), lam_i.reshape(nj, TILE_STATES)], axis=1)
    dvec = ssm_d[0].reshape(nj, 1, LANES)
    t_len = _pick(l, 512, 8)
    y, hst = _s5_fwd(proj, off_u, bd, cbd, lam, dvec, t_len)

    tcs = _pick(ssm_w, 512)
    gy = _tile_call("gelu", lambda v: jax.nn.gelu(v), (ssm_w // tcs, nr), [y], [_t(tr, tcs)],
                    [_sds((l, ssm_w), BF16)], [_t(tr, tcs)])[0]
    glu = _matmul("ssm_glu", gy, full['w_ssm_glu'], "nn")

    tcd = 256 if d % 256 == 0 and off_ga % 256 == 0 else LANES
    assert d % tcd == 0 and off_ga % tcd == 0 and off_gs % tcd == 0
    mix_in_specs = [_t(tr, tcd), _t(tr, tcd, d), _t(tr, tcd), _t(tr, tcd, off_ga), _t(tr, tcd, off_gs)]
    mixed = _tile_call("gate_mix", _mix_fn, (d // tcd, nr), [glu, glu, attn, proj, proj], mix_in_specs,
                       [_sds((l, d), BF16)], [_t(tr, tcd)])[0]
    mixout = _matmul("mix_out", mixed, full['w_out'], "nn")

    def res_norm_fn(xv, mo, g1v, gv, scv, shv):
        x2v = xv + g1v * mo
        return x2v, _norm_mod(x2v, gv, scv, shv)

    x2, h2 = _tile_call("residual_norm_mod_ffn", res_norm_fn, (1, nr), [xs, mixout, g1, g_ffn, sc2, sh2],
                        [_t(tr, d), _t(tr, d), _v(d), _v(d), _v(d), _v(d)],
                        [_sds((l, d)), _sds((l, d), BF16)], [_t(tr, d), _t(tr, d)])
    up = _matmul("ffn_up", h2, full['w_ffn_up'], "nn", tn=1408)

    tcf = _pick(dff, 512)
    assert dff % tcf == 0
    ncf = dff // tcf

    taps = [conv_w[i:i + 1] for i in range(3)]

    def conv_gate(gp, gp_prev, w0, w1, w2, bv):
        prev = jnp.where(pl.program_id(1) == 0, 0.0, 1.0) * gp_prev
        ext = jnp.concatenate([prev, gp], axis=0)
        m1 = pltpu.roll(ext, 1, 0)[8:]
        m2 = pltpu.roll(ext, 2, 0)[8:]
        return w0 * m2 + w1 * m1 + w2 * gp + bv, m1, m2

    def convglu_fn(gp, gp_prev, val, w0, w1, w2, bv):
        gate, _, _ = conv_gate(gp, gp_prev, w0, w1, w2, bv)
        return gate * jax.nn.sigmoid(gate) * val

    act = _tile_call("conv_swiglu", convglu_fn, (ncf, nr), [up, up, up] + taps + [conv_b],
                     [_t(tr, tcf), _prev8(tr, tcf), _t(tr, tcf, dff)] + [_v(tcf)] * 4,
                     [_sds((l, dff), BF16)], [_t(tr, tcf)])[0]
    ffn = _matmul("ffn_down", act, full['w_ffn_down'], "nn")

    def final_fn(x2v, fv, g2v, gv, tv):
        rows = x2v.shape[0]

        def loss_of(x2a, fa, g2a, ga):
            out = _rms(x2a + g2a * fa, ga)
            err = out - tv
            return 0.5 * _colsum(jnp.mean(err * err, axis=-1, keepdims=True))

        loss, vjp = jax.vjp(loss_of, x2v, fv, _bc(g2v, rows), _bc(gv, rows))
        dx3, dffn, dg2, dgf = vjp(jnp.ones((1, 1), F32))
        return jnp.broadcast_to(loss, (1, LANES)), dx3, dffn, _colsum(dg2), _colsum(dgf)

    loss_p, dx3, dffn, dg2, dg_fin = _tile_call(
        "loss_final_norm", final_fn, (1, nrh), [x2, ffn, g2, g_fin, tgt],
        [_t(trh, d), _t(trh, d), _v(d), _v(d), _t(trh, d)],
        [_sds((1, LANES)), _sds((l, d)), _sds((l, d), BF16), _sds((1, d)), _sds((1, d))],
        [_v(LANES), _t(trh, d), _t(trh, d), _v(d), _v(d)], acc=(0, 3, 4))
    loss = lax.psum(loss_p[0, 0], ("x", "y", "c"))

    dact = _matmul("d_act", dffn, full['w_ffn_down'], "nt", tn=1408)
    dw_down = _matmul("dw_ffn_down", act, dffn, "tn", tm=1408)

    def convglu_bwd_fn(gp, gp_prev, val, da, w0, w1, w2, bv):
        gate, m1, m2 = conv_gate(gp, gp_prev, w0, w1, w2, bv)
        sg = jax.nn.sigmoid(gate)
        dgate = da * val * (sg * (1.0 + gate * (1.0 - sg)))
        return (dgate, da * (gate * sg), _colsum(dgate), _colsum(dgate * m2), _colsum(dgate * m1),
                _colsum(dgate * gp))

    dgate, dval, dconv_b, dcw0, dcw1, dcw2 = _tile_call(
        "conv_swiglu_bwd", convglu_bwd_fn, (ncf, nr), [up, up, up, dact] + taps + [conv_b],
        [_t(tr, tcf), _prev8(tr, tcf), _t(tr, tcf, dff), _t(tr, tcf)] + [_v(tcf)] * 4,
        [_sds((l, dff)), _sds((l, dff), BF16)] + [_sds((1, dff))] * 4,
        [_t(tr, tcf), _t(tr, tcf)] + [_v(tcf)] * 4, acc=(2, 3, 4, 5))

    def conv_t_fn(dg, dg_next, dv, w0, w1, w2):
        rows = dg.shape[0]
        nxt = jnp.where(pl.program_id(1) == pl.num_programs(1) - 1, 0.0, 1.0) * dg_next
        ext = jnp.concatenate([dg, nxt], axis=0)
        p1 = pltpu.roll(ext, rows + 8 - 1, 0)[:rows]
        p2 = pltpu.roll(ext, rows + 8 - 2, 0)[:rows]
        return jnp.stack([w2 * dg + w1 * p1 + w0 * p2, dv.astype(F32)], axis=0)

    dup = _tile_call("conv_transpose", conv_t_fn, (ncf, nr), [dgate, dgate, dval] + taps,
                     [_t(tr, tcf), _next8(tr, tcf, l), _t(tr, tcf)] + [_v(tcf)] * 3,
                     [_sds((2, l, dff), BF16)], [_st(tr, tcf)])[0]
    dh2 = _matmul("d_h2", dup, full['w_ffn_up'], "nt")
    dw_up = _matmul("dw_ffn_up", h2, dup, "tn", tn=1408, out_stack=N_DEV)

    def res_norm_bwd_fn(xv, mo, g1v, gv, scv, shv, dhv, dxv):
        rows = xv.shape[0]
        _, vjp = jax.vjp(res_norm_fn, xv, mo, _bc(g1v, rows), _bc(gv, rows), _bc(scv, rows), _bc(shv, rows))
        dx, dmo, dg1v, dgv, dscv, dshv = vjp((dxv, dhv))
        return dx, dmo, _colsum(dg1v), _colsum(dgv), _colsum(dscv), _colsum(dshv)

    dx2, dmixout, dg1, dg_ffn, dsc2, dsh2 = _tile_call(
        "residual_norm_mod_ffn_bwd", res_norm_bwd_fn, (1, nrh), [xs, mixout, g1, g_ffn, sc2, sh2, dh2, dx3],
        [_t(trh, d), _t(trh, d), _v(d), _v(d), _v(d), _v(d), _t(trh, d), _t(trh, d)],
        [_sds((l, d)), _sds((l, d), BF16)] + [_sds((1, d))] * 4,
        [_t(trh, d), _t(trh, d)] + [_v(d)] * 4, acc=(2, 3, 4, 5))

    dmixed = _matmul("d_mixed", dmixout, full['w_out'], "nt")
    dw_out = _matmul("dw_out", mixed, dmixout, "tn")

    def mix_bwd_fn(ga_, gb_, at, pa, ps, dm):
        _, vjp = jax.vjp(_mix_fn, ga_, gb_, at, pa, ps)
        da, db, dat, dpa, dps = vjp(dm)
        return jnp.stack([da, db], axis=0), dat, dpa, dps

    dglu, dattn, dga, dgs = _tile_call(
        "gate_mix_bwd", mix_bwd_fn, (d // tcd, nr), [glu, glu, attn, proj, proj, dmixed],
        mix_in_specs + [_t(tr, tcd)],
        [_sds((2, l, d), BF16)] + [_sds((l, d), BF16)] * 3, [_st(tr, tcd)] + [_t(tr, tcd)] * 3)

    dgy = _matmul("d_gelu_y", dglu, full['w_ssm_glu'], "nt")
    dw_glu = _matmul("dw_ssm_glu", gy, dglu, "tn", out_stack=N_DEV)

    def gelu_bwd_fn(yv, dv):
        _, vjp = jax.vjp(lambda z: jax.nn.gelu(z), yv)
        return vjp(dv)[0]

    dy = _tile_call("gelu_bwd", gelu_bwd_fn, (ssm_w // tcs, nr), [y, dgy], [_t(tr, tcs), _t(tr, tcs)],
                    [_sds((l, ssm_w))], [_t(tr, tcs)])[0]
    du, dbd, dcbdt, dlam, dd_tiles = _s5_bwd(proj, off_u, dy, hst, bd, bdt, cbdt, lam, dvec, t_len)

    def gpn_of(z):
        return z.transpose(2, 0, 1, 3).reshape(pgn)

    dbb_r = gpn_of(_diag_blocks(dbd[:, :, :TILE_STATES], SSM_P))
    dbb_i = gpn_of(_diag_blocks(dbd[:, :, TILE_STATES:], SSM_P))
    dc_re = _diag_blocks(dcbdt[:, :, :TILE_STATES], SSM_P).reshape(n_groups, SSM_P, SSM_N)
    dc_im = _diag_blocks(dcbdt[:, :, TILE_STATES:], SSM_P).reshape(n_groups, SSM_P, SSM_N)
    dlam_r, dlam_i = dlam[:, 0].reshape(gn), dlam[:, 1].reshape(gn)

    def disc_bwd_fn(ar, ai, ld, br, bi, dlr, dli, dbr, dbi):
        _, vjp = jax.vjp(_s5_disc_fn, ar, ai, ld, br, bi)
        return vjp((dlr, dli, dbr, dbi))

    da_re, da_im, dlog_dt, db_re, db_im = _tile_call(
        "s5_discretise_bwd", disc_bwd_fn, (1,), disc_ins + [dlam_r, dlam_i, dbb_r, dbb_i],
        disc_specs + [_full_spec(gn), _full_spec(gn), _full_spec(pgn), _full_spec(pgn)],
        [_sds(gn), _sds(gn), _sds((n_groups, 1)), _sds(pgn), _sds(pgn)], disc_specs)

    do2 = _matmul("d_attn_heads", dattn, full['w_attn_proj'], "nt")
    dw_ap = _matmul("dw_attn_proj", o2, dattn, "tn", out_stack=N_DEV)
    do_h = heads(do2.astype(BF16), hq)
    dq_h, dk_h, dv_h, dsink = _attn_bwd(qh, kh, vh, sinks3, do_h)

    def unheads(z):
        return z.transpose(1, 0, 2).reshape(l, z.shape[0] * HEAD_DIM)

    dproj = jnp.concatenate([unheads(dq_h), unheads(dk_h), unheads(dv_h), du, dga, dgs], axis=1)
    dh1 = _matmul("d_h1", dproj, full['w_in'], "nt")
    dw_in = _matmul("dw_in", h1, dproj, "tn", tn=1280)

    def norm_bwd_fn(xv, gv, scv, shv, dhv, dxv):
        rows = xv.shape[0]
        _, vjp = jax.vjp(_norm_mod, xv, _bc(gv, rows), _bc(scv, rows), _bc(shv, rows))
        dx, dgv, dscv, dshv = vjp(dhv)
        return dx + dxv, _colsum(dgv), _colsum(dscv), _colsum(dshv)

    grad_x, dg_mix, dsc1, dsh1 = _tile_call(
        "norm_mod_mix_bwd", norm_bwd_fn, (1, nrh), [xs, g_mix, sc1, sh1, dh1, dx2],
        [_t(trh, d), _v(d), _v(d), _v(d), _t(trh, d), _t(trh, d)],
        [_sds((l, d))] + [_sds((1, d))] * 3, [_t(trh, d)] + [_v(d)] * 3, acc=(1, 2, 3))

    dmod = jnp.concatenate([dsh1, dsc1, dg1, dsh2, dsc2, dg2], axis=1)
    small = ['ada_b', 'norm_mix_g', 'attn_sinks', 'ssm_a_re', 'ssm_a_im', 'ssm_log_dt', 'ssm_b_re', 'ssm_b_im',
             'ssm_c_re', 'ssm_c_im', 'ssm_d', 'norm_ffn_g', 'ffn_conv_b', 'final_g']
    small_grads = {
        'ada_b': dmod, 'norm_mix_g': dg_mix, 'attn_sinks': dsink[:, 0, 0], 'ssm_a_re': da_re, 'ssm_a_im': da_im,
        'ssm_log_dt': dlog_dt, 'ssm_b_re': db_re.transpose(1, 2, 0), 'ssm_b_im': db_im.transpose(1, 2, 0),
        'ssm_c_re': dc_re, 'ssm_c_im': dc_im, 'ssm_d': dd_tiles, 'norm_ffn_g': dg_ffn, 'ffn_conv_b': dconv_b,
        'final_g': dg_fin}
    gs_pack, sm_offs = _pack([small_grads[k] for k in small], LANES, 8)
    (gs_all,) = _all_gather("gather_small_grads", [gs_pack])
    ws_pack, _ = _pack([given[k] for k in small], LANES, 8)
    ms_pack, _ = _pack([given['m_' + k] for k in small], LANES, 8)
    vs_pack, _ = _pack([given['v_' + k] for k in small], LANES, 8)
    small_out = _adamw("adamw_replicated", gs_all, ws_pack, ms_pack, vs_pack)

    dmod_all = _unpack(gs_all, sm_offs[0], (N_DEV * mod_n,), lead=(N_DEV,))
    dmod_mine = lax.dynamic_slice_in_dim(dmod_all, idx * mod_n, mod_n, axis=1)
    kpad = LANES - N_DEV
    cond_t = jnp.pad(cond_all.T, ((0, 0), (0, kpad)))
    dmod_pad = jnp.pad(dmod_mine, ((0, kpad), (0, 0)))
    g_ada_w = _matmul("dw_ada", cond_t, dmod_pad, "nn")
    ada_out = _adamw("adamw_ada_w", g_ada_w[None], ada_w[0], m_ada_w[0], v_ada_w[0])

    sharded = big + ['ffn_conv_w']
    dcw = jnp.concatenate([dcw0, dcw1, dcw2], axis=0)
    shard_in, shard_cw = w_in.shape[1:], ffn_conv_w.shape[1:]
    by_dest = {
        'w_in': dw_in.reshape(shard_in[0], N_DEV, shard_in[1]).transpose(1, 0, 2),
        'w_attn_proj': dw_ap, 'w_ssm_glu': dw_glu, 'w_ffn_up': dw_up,
        'w_out': dw_out.reshape((N_DEV,) + w_out.shape[1:]),
        'w_ffn_down': dw_down.reshape((N_DEV,) + w_ffn_down.shape[1:]),
        'ffn_conv_w': dcw.reshape(shard_cw[0], N_DEV, shard_cw[1]).transpose(1, 0, 2)}
    gds = [by_dest[k] for k in sharded]
    from_sib = _grad_to_sibling(gds)
    c_arr = jnp.reshape(ci, (1,)).astype(jnp.int32)
    place = jnp.stack([2 * xi + yi, ci]).astype(jnp.int32)
    sums = [_chip_sum("chip_sum_" + k, g, f, c_arr) for k, g, f in zip(sharded, gds, from_sib)]
    parts = _grad_to_chips(sums)
    sharded_out = {
        k: _adamw_sharded("adamw_" + k, p, g, f, given[k][0], given['m_' + k][0], given['v_' + k][0], place)
        for k, p, g, f in zip(sharded, parts, gds, from_sib)}

    results = [{}, {}, {}, {}]
    for which in range(4):
        for k, off in zip(small, sm_offs):
            results[which][k] = _unpack(small_out[which], off, given[k].shape)
        for k in sharded:
            results[which][k] = sharded_out[k][which][None]
        results[which]['ada_w'] = ada_out[which][None]
    outs = [loss, grad_x[None]]
    for which in range(4):
        outs += [results[which][k] for k in names]
    return tuple(outs)
```

```python
import functools
import math

import jax
import jax.numpy as jnp
from jax import lax
from jax.experimental import pallas as pl
from jax.experimental.pallas import tpu as pltpu

F32, BF16 = jnp.float32, jnp.bfloat16
MESH = pl.DeviceIdType.MESH
N_DEV = 8

HEAD_DIM = 64
N_KV_HEADS = 2
ATT_BLOCK = 128
NEG_INF = -1e30
SSM_P = 16
SSM_N = 64
LANES = 128
TILE_GROUPS = LANES // SSM_P
TILE_STATES = TILE_GROUPS * SSM_N
RMS_EPS = 1e-6
ADAM_LR, ADAM_B1, ADAM_B2, ADAM_EPS, ADAM_WD, ADAM_STEP = 0.001, 0.9, 0.999, 1e-08, 0.01, 10
VMEM_LIMIT = 56 * 1024 * 1024
MATMUL_VMEM_BUDGET = 40 * 1024 * 1024


def _params(n_axes):
    return pltpu.CompilerParams(dimension_semantics=("arbitrary",) * n_axes, vmem_limit_bytes=VMEM_LIMIT)


def _pick(dim, pref, align=128):
    if dim <= align:
        return dim
    t = (min(pref, dim) // align) * align
    while t > align and dim % t:
        t -= align
    assert dim % t == 0, (dim, pref, align)
    return t


def _dev():
    return lax.axis_index("x"), lax.axis_index("y"), lax.axis_index("c")


def _tile_call(name, fn, grid, ins, in_specs, out_shapes, out_specs, acc=()):
    n_in, n_out = len(ins), len(out_shapes)
    acc_axis = len(grid) - 1

    def body(*refs):
        vals = fn(*[r[...] for r in refs[:n_in]])
        if not isinstance(vals, (tuple, list)):
            vals = (vals,)
        assert len(vals) == n_out
        for i, (r, v) in enumerate(zip(refs[n_in:], vals)):
            v = v.astype(r.dtype)
            if i in acc:
                first = pl.program_id(acc_axis) == 0

                @pl.when(first)
                def _():
                    r[...] = v

                @pl.when(jnp.logical_not(first))
                def _():
                    r[...] += v
            else:
                r[...] = v

    return pl.pallas_call(
        body, grid=grid, in_specs=in_specs, out_specs=out_specs, out_shape=out_shapes, name=name,
        compiler_params=_params(len(grid)),
    )(*ins)


def _t(tr, tc, off=0):
    return pl.BlockSpec((tr, tc), lambda j, i: (i, j + off // tc))


def _v(tc, off=0, rows=1):
    return pl.BlockSpec((rows, tc), lambda j, i: (0, j + off // tc))


def _prev8(tr, tc, off=0):
    return pl.BlockSpec((8, tc), lambda j, i: (jnp.maximum(i * (tr // 8) - 1, 0), j + off // tc))


def _next8(tr, tc, nrows, off=0):
    return pl.BlockSpec((8, tc), lambda j, i: (jnp.minimum((i + 1) * (tr // 8), nrows // 8 - 1), j + off // tc))


def _st(tr, tc):
    return pl.BlockSpec((2, tr, tc), lambda j, i: (0, i, j))


def _bc(v, rows):
    return jnp.broadcast_to(v, (rows, v.shape[-1]))


def _colsum(v):
    return jnp.sum(v, axis=0, keepdims=True)


def _matmul(name, a, b, mode, out_dtype=F32, tm=1024, tn=1024, tk=None, out_stack=None, also_bf16=False, dep=None):
    def dims(z):
        return (z.shape[-2], z.shape[-1] * (z.shape[0] if z.ndim == 3 else 1))

    ar, ac = dims(a)
    br, bc = dims(b)
    if mode == "nn":
        m, k, n = ar, ac, bc
        assert br == k
    elif mode == "nt":
        m, k, n = ar, ac, br
        assert bc == k
    else:
        m, k, n = ac, ar, bc
        assert br == k
    m_lim, k_lim, n_lim = [m], [k], [n]
    if a.ndim == 3:
        (m_lim if mode == "tn" else k_lim).append(a.shape[-1])
    if b.ndim == 3:
        (k_lim if mode == "nt" else n_lim).append(b.shape[-1])
    if out_stack:
        n_lim.append(n // out_stack)
    tm = _pick(functools.reduce(math.gcd, m_lim), tm)
    tn = _pick(functools.reduce(math.gcd, n_lim), tn)
    k_unit = functools.reduce(math.gcd, k_lim)
    if tk is None:
        sa, sb, so = a.dtype.itemsize, b.dtype.itemsize, jnp.dtype(out_dtype).itemsize + (2 if also_bf16 else 0)
        fits = [t for t in range(LANES, k_unit + 1, LANES) if k_unit % t == 0 and
                2 * t * (tm * sa + tn * sb) + tm * tn * (2 * so + (4 if t < k else 0)) <= MATMUL_VMEM_BUDGET]
        tk = max(fits) if fits else _pick(k_unit, 512)
    else:
        tk = _pick(k_unit, tk)
    nk = k // tk

    def spec(z, brows, bcols, ridx, cidx):
        if z.ndim == 3:
            per = z.shape[-1] // bcols
            return pl.BlockSpec((None, brows, bcols),
                                lambda i, j, kk: (cidx(i, j, kk) // per, ridx(i, j, kk), cidx(i, j, kk) % per))
        return pl.BlockSpec((brows, bcols), lambda i, j, kk: (ridx(i, j, kk), cidx(i, j, kk)))

    gi = lambda i, j, kk: i
    gj = lambda i, j, kk: j
    gk = lambda i, j, kk: kk
    if mode == "nn":
        a_spec, b_spec = spec(a, tm, tk, gi, gk), spec(b, tk, tn, gk, gj)
        dn = (((1,), (0,)), ((), ()))
    elif mode == "nt":
        a_spec, b_spec = spec(a, tm, tk, gi, gk), spec(b, tn, tk, gj, gk)
        dn = (((1,), (1,)), ((), ()))
    else:
        a_spec, b_spec = spec(a, tk, tm, gk, gi), spec(b, tk, tn, gk, gj)
        dn = (((0,), (0,)), ((), ()))

    n_out = 2 if also_bf16 else 1

    deps = [] if dep is None else [dep]

    def body(a_ref, b_ref, *rest):
        rest = rest[len(deps):]
        o_refs, acc = rest[:n_out], rest[n_out:]
        part = lax.dot_general(a_ref[...].astype(BF16), b_ref[...].astype(BF16), dn, preferred_element_type=F32)

        def emit(val):
            for o_ref in o_refs:
                o_ref[...] = val.astype(o_ref.dtype)

        if nk == 1:
            emit(part)
            return
        acc_ref, = acc
        kk = pl.program_id(2)

        @pl.when(kk == 0)
        def _():
            acc_ref[...] = part

        @pl.when(kk > 0)
        def _():
            acc_ref[...] += part

        @pl.when(kk == nk - 1)
        def _():
            emit(acc_ref[...])

    if out_stack:
        per = (n // out_stack) // tn
        out_spec = pl.BlockSpec((None, tm, tn), lambda i, j, kk: (j // per, i, j % per))
        shape = (out_stack, m, n // out_stack)
    else:
        out_spec = pl.BlockSpec((tm, tn), lambda i, j, kk: (i, j))
        shape = (m, n)
    dtypes = [out_dtype, BF16][:n_out]
    res = pl.pallas_call(
        body, grid=(m // tm, n // tn, nk),
        in_specs=[a_spec, b_spec] + [pl.BlockSpec(memory_space=pl.ANY)] * len(deps), out_specs=[out_spec] * n_out,
        out_shape=[jax.ShapeDtypeStruct(shape, dt) for dt in dtypes],
        scratch_shapes=[pltpu.VMEM((tm, tn), F32)] if nk > 1 else [], name=name, compiler_params=_params(3),
    )(a, b, *deps)
    return res if also_bf16 else res[0]


def _all_gather(name, arrs):
    n = len(arrs)

    def body(*refs):
        ins, outs = refs[:n], refs[n:2 * n]
        send_sems, recv_sems, local_sems = refs[2 * n:]
        x, y, c = _dev()
        me, sib = (x, y, c), (x, y, 1 - c)
        chips = [(1 - x, y), (x, 1 - y), (1 - x, 1 - y)]

        def slot(p):
            return 4 * p[0] + 2 * p[1] + p[2]

        def copy(a, k, block, to, src=None):
            dst = outs[a].at[slot(block)]
            return pltpu.make_async_remote_copy(
                src_ref=dst if src is None else src, dst_ref=dst,
                send_sem=send_sems.at[7 * a + k], recv_sem=recv_sems.at[7 * a + k],
                device_id=to, device_id_type=MESH)

        mine = [pltpu.make_async_copy(ins[a], outs[a].at[slot(me)], local_sems.at[a]) for a in range(n)]
        for cp in mine:
            cp.start()
        first = []
        for a in range(n):
            first.append(copy(a, 0, me, sib, src=ins[a]))
            first += [copy(a, 1 + j, me, (*chip, c), src=ins[a]) for j, chip in enumerate(chips)]
        for cp in first:
            cp.start()
        passed = []
        for j, chip in enumerate(chips):
            for a in range(n):
                copy(a, 1 + j, (*chip, c), me).wait_recv()
                cp = copy(a, 4 + j, (*chip, c), sib)
                cp.start()
                passed.append(cp)
        for a in range(n):
            copy(a, 0, sib, me).wait_recv()
            for j, chip in enumerate(chips):
                copy(a, 4 + j, (*chip, 1 - c), me).wait_recv()
        for cp in first + passed:
            cp.wait_send()
        for cp in mine:
            cp.wait()

    any_spec = pl.BlockSpec(memory_space=pl.ANY)
    return pl.pallas_call(
        body, in_specs=[any_spec] * n, out_specs=[any_spec] * n,
        out_shape=[jax.ShapeDtypeStruct((N_DEV,) + a.shape, a.dtype) for a in arrs],
        scratch_shapes=[pltpu.SemaphoreType.DMA((7 * n,)), pltpu.SemaphoreType.DMA((7 * n,)),
                        pltpu.SemaphoreType.DMA((n,))],
        name=name,
    )(*arrs)


def _grad_to_sibling(gds):
    n = len(gds)

    def body(*refs):
        g_refs, r_refs = refs[:n], refs[n:2 * n]
        send_sems, recv_sems = refs[2 * n:]
        x, y, c = _dev()
        cps = []
        for a in range(n):
            for k in range(4):
                cp = pltpu.make_async_remote_copy(
                    src_ref=g_refs[a].at[2 * k + (1 - c)], dst_ref=r_refs[a].at[k],
                    send_sem=send_sems.at[4 * a + k], recv_sem=recv_sems.at[4 * a + k],
                    device_id=(x, y, 1 - c), device_id_type=MESH)
                cp.start()
                cps.append(cp)
        for cp in cps:
            cp.wait()

    any_spec = pl.BlockSpec(memory_space=pl.ANY)
    return pl.pallas_call(
        body, in_specs=[any_spec] * n, out_specs=[any_spec] * n,
        out_shape=[jax.ShapeDtypeStruct((4,) + g.shape[1:], g.dtype) for g in gds],
        scratch_shapes=[pltpu.SemaphoreType.DMA((4 * n,)), pltpu.SemaphoreType.DMA((4 * n,))],
        name="grad_to_sibling",
    )(*gds)


def _chip_sum(name, gd, from_sib, c_arr):
    _, k, n = gd.shape
    tr = _pick(k, max(16, (1 << 20) // (4 * n)), 16)

    def body(c_ref, a_ref, b_ref, o_ref):
        o_ref[...] = (a_ref[...] + b_ref[...]).astype(o_ref.dtype)

    return pl.pallas_call(
        body,
        grid_spec=pltpu.PrefetchScalarGridSpec(
            num_scalar_prefetch=1, grid=(4, k // tr),
            in_specs=[pl.BlockSpec((1, tr, n), lambda q, i, cr: (2 * q + cr[0], i, 0)),
                      pl.BlockSpec((1, tr, n), lambda q, i, cr: (q, i, 0))],
            out_specs=pl.BlockSpec((1, tr, n), lambda q, i, cr: (q, i, 0))),
        out_shape=jax.ShapeDtypeStruct((4, k, n), BF16), name=name, compiler_params=_params(2),
    )(c_arr, gd, from_sib)


def _grad_to_chips(sums):
    n = len(sums)

    def body(*refs):
        s_refs, p_refs = refs[:n], refs[n:2 * n]
        send_sems, recv_sems, local_sems = refs[2 * n:]
        x, y, c = _dev()
        my_chip = 2 * x + y
        cps = []
        for a in range(n):
            local = pltpu.make_async_copy(s_refs[a].at[my_chip], p_refs[a].at[my_chip], local_sems.at[a])
            local.start()
            cps.append(local)
            for j, (px, py) in enumerate([(1 - x, y), (x, 1 - y), (1 - x, 1 - y)]):
                cp = pltpu.make_async_remote_copy(
                    src_ref=s_refs[a].at[2 * px + py], dst_ref=p_refs[a].at[my_chip],
                    send_sem=send_sems.at[3 * a + j], recv_sem=recv_sems.at[3 * a + j],
                    device_id=(px, py, c), device_id_type=MESH)
                cp.start()
                cps.append(cp)
        for cp in cps:
            cp.wait()

    any_spec = pl.BlockSpec(memory_space=pl.ANY)
    return pl.pallas_call(
        body, in_specs=[any_spec] * n, out_specs=[any_spec] * n,
        out_shape=[jax.ShapeDtypeStruct(s.shape, s.dtype) for s in sums],
        scratch_shapes=[pltpu.SemaphoreType.DMA((3 * n,)), pltpu.SemaphoreType.DMA((3 * n,)),
                        pltpu.SemaphoreType.DMA((n,))],
        name="grad_to_chips",
    )(*sums)


FLIPS = [(0, 0, 1), (0, 1, 0), (1, 0, 0), (0, 1, 1), (1, 0, 1), (1, 1, 0), (1, 1, 1)]
N_PEERS = len(FLIPS)
_HBM = pl.BlockSpec(memory_space=pltpu.HBM)
_SEM = pl.BlockSpec(memory_space=pltpu.SEMAPHORE)
_EFFECT = pltpu.SideEffectType.DATAFLOW_SIDE_EFFECTING


def _flip(x, y, c, f):
    return (1 - x if f[0] else x, 1 - y if f[1] else y, 1 - c if f[2] else c)


def _slot(p):
    return 4 * p[0] + 2 * p[1] + p[2]


def _exchange_copies(src_refs, land_refs, send_sems, recv_sems, gather):
    x, y, c = _dev()
    mine = _slot((x, y, c))
    cps = []
    for a, (src, land) in enumerate(zip(src_refs, land_refs)):
        for k, f in enumerate(FLIPS):
            peer = _flip(x, y, c, f)
            cps.append(pltpu.make_async_remote_copy(
                src_ref=src if gather else src.at[_slot(peer)], dst_ref=land.at[mine],
                send_sem=send_sems.at[N_PEERS * a + k], recv_sem=recv_sems.at[N_PEERS * a + k],
                device_id=peer, device_id_type=MESH))
    return cps


def _exchange_start(name, srcs, gather, after):
    n = len(srcs)
    lands = [lax.empty(((N_DEV,) + s.shape) if gather else s.shape, s.dtype) for s in srcs]

    def body(*refs):
        src_refs, land_refs = refs[:n], refs[n:2 * n]
        send_sems, recv_sems, local_sems = refs[2 * n + 1:2 * n + 4]
        token = refs[-1]
        if gather:
            x, y, c = _dev()
            for a in range(n):
                pltpu.make_async_copy(src_refs[a], land_refs[a].at[_slot((x, y, c))], local_sems.at[a]).start()
        for cp in _exchange_copies(src_refs, land_refs, send_sems, recv_sems, gather):
            cp.start()
        token[...] = jnp.zeros_like(token)

    hbm = lambda z: pltpu.HBM(z.shape, z.dtype)
    outs = pl.pallas_call(
        body, name=name,
        out_shape=(pltpu.SemaphoreType.DMA((N_PEERS * n,)), pltpu.SemaphoreType.DMA((N_PEERS * n,)),
                   pltpu.SemaphoreType.DMA((n,)), *[hbm(s) for s in srcs], *[hbm(z) for z in lands],
                   jax.ShapeDtypeStruct((8, LANES), F32)),
        in_specs=[_HBM] * (2 * n) + [pl.BlockSpec(memory_space=pl.ANY)],
        out_specs=(_SEM, _SEM, _SEM, *[_HBM] * (2 * n), pl.BlockSpec(memory_space=pltpu.VMEM)),
        input_output_aliases={i: 3 + i for i in range(2 * n)},
        compiler_params=pltpu.CompilerParams(has_side_effects=_EFFECT),
    )(*[pltpu.with_memory_space_constraint(z, pltpu.HBM) for z in list(srcs) + lands], after)
    return (outs[:3], outs[3:3 + n], outs[3 + n:3 + 2 * n], gather), outs[-1]


def _exchange_wait(name, handles, after):
    sems, srcs, lands, gather = handles
    n = len(srcs)

    def body(*refs):
        src_refs, land_refs = refs[:n], refs[n:2 * n]
        send_sems, recv_sems, local_sems = refs[2 * n:2 * n + 3]
        if gather:
            for a in range(n):
                pltpu.make_async_copy(src_refs[a], land_refs[a].at[0], local_sems.at[a]).wait()
        for cp in _exchange_copies(src_refs, land_refs, send_sems, recv_sems, gather):
            cp.wait_send()
            cp.wait_recv()

    hbm = lambda z: pltpu.HBM(z.shape, z.dtype)
    outs = pl.pallas_call(
        body, name=name, out_shape=tuple(hbm(z) for z in list(srcs) + list(lands)),
        in_specs=[_HBM] * (2 * n) + [_SEM] * 3 + [pl.BlockSpec(memory_space=pl.ANY)],
        out_specs=tuple([_HBM] * (2 * n)), input_output_aliases={i: i for i in range(2 * n)},
        compiler_params=pltpu.CompilerParams(has_side_effects=_EFFECT),
    )(*srcs, *lands, *sems, after)
    return list(outs[n:])


def _pack_rows(sizes, width, row_align):
    offs, r = [], 0
    for s in sizes:
        offs.append(r)
        r += -(-s // width)
    total = -(-r // row_align) * row_align
    return offs, total


def _pack(items, width, row_align, lead=()):
    nl = len(lead)
    sizes = [int(jnp.size(a)) // max(1, functools.reduce(lambda p, q: p * q, lead, 1)) for a in items]
    offs, total = _pack_rows(sizes, width, row_align)
    flat = []
    used = 0
    for a, s in zip(items, sizes):
        f = a.reshape(lead + (s,))
        pad = -(-s // width) * width - s
        if pad:
            f = jnp.pad(f, [(0, 0)] * nl + [(0, pad)])
        flat.append(f)
        used += s + pad
    tail = total * width - used
    if tail:
        flat.append(jnp.zeros(lead + (tail,), items[0].dtype))
    return jnp.concatenate(flat, axis=-1).reshape(lead + (total, width)), offs


def _unpack(packed, off, shape, lead=()):
    nl = len(lead)
    size = functools.reduce(lambda p, q: p * q, shape, 1)
    width = packed.shape[-1]
    rows = -(-size // width)
    blk = lax.slice_in_dim(packed, off, off + rows, axis=nl).reshape(lead + (rows * width,))
    return lax.slice_in_dim(blk, 0, size, axis=nl).reshape(lead + tuple(shape))


def _rms(x, g):
    return (x * lax.rsqrt(jnp.mean(x * x, axis=-1, keepdims=True) + RMS_EPS)) * g


def _norm_mod(x, g, sc, sh):
    return _rms(x, g) * (1.0 + sc) + sh


def _mix_fn(glu_a, glu_b, attn, ga, gs):
    return jax.nn.sigmoid(ga) * attn + jax.nn.sigmoid(gs) * (glu_a * jax.nn.sigmoid(glu_b))


def _s5_disc_fn(a_re, a_im, log_dt, b_re, b_im):
    dt = jnp.exp(log_dt)
    mag = jnp.exp(a_re * dt)
    lr, li = mag * jnp.cos(a_im * dt), mag * jnp.sin(a_im * dt)
    den = a_re * a_re + a_im * a_im
    zr = ((lr - 1.0) * a_re + li * a_im) / den
    zi = (li * a_re - (lr - 1.0) * a_im) / den
    return lr, li, zr[None] * b_re - zi[None] * b_im, zr[None] * b_im + zi[None] * b_re


def _adamw_fn(w, g, m, v):
    m = ADAM_B1 * m + (1.0 - ADAM_B1) * g
    v = ADAM_B2 * v + (1.0 - ADAM_B2) * jnp.square(g)
    m_hat = m / (1.0 - ADAM_B1 ** ADAM_STEP)
    v_hat = v / (1.0 - ADAM_B2 ** ADAM_STEP)
    delta = -ADAM_LR * (m_hat / (jnp.sqrt(v_hat) + ADAM_EPS) + ADAM_WD * w)
    return delta, m, v


def _adamw(name, parts, w, m, v):
    p, r, c = parts.shape
    tr = _pick(r, max(8, (1 << 21) // (4 * c * max(p, 2))), 8)

    def fn(pv, wv, mv, vv):
        g = pv[0]
        for i in range(1, p):
            g = g + pv[i]
        d, m2, v2 = _adamw_fn(wv, g, mv, vv)
        return g, d, m2, v2

    spec = pl.BlockSpec((tr, c), lambda i: (i, 0))
    return _tile_call(
        name, fn, (r // tr,), [parts, w, m, v],
        [pl.BlockSpec((p, tr, c), lambda i: (0, i, 0)), spec, spec, spec],
        [jax.ShapeDtypeStruct((r, c), F32)] * 4, [spec] * 4)


def _adamw_sharded(name, parts, gd, w, m, v, place):
    _, k, n = parts.shape
    tr = _pick(k, max(16, (1 << 19) // (4 * n)), 16)

    def body(pl_ref, p_ref, a_ref, w_ref, m_ref, v_ref, g_ref, d_ref, m2_ref, v2_ref):
        own = a_ref[0]
        g = None
        for q in range(N_DEV):
            term = jnp.where(pl_ref[0] == q, own, p_ref[q].astype(F32))
            g = term if g is None else g + term
        d, m2, v2 = _adamw_fn(w_ref[...], g, m_ref[...], v_ref[...])
        g_ref[...] = g
        d_ref[...] = d
        m2_ref[...] = m2
        v2_ref[...] = v2

    spec = pl.BlockSpec((tr, n), lambda i, pr: (i, 0))
    return pl.pallas_call(
        body,
        grid_spec=pltpu.PrefetchScalarGridSpec(
            num_scalar_prefetch=1, grid=(k // tr,),
            in_specs=[pl.BlockSpec((N_DEV, tr, n), lambda i, pr: (0, i, 0)),
                      pl.BlockSpec((1, tr, n), lambda i, pr: (pr[0], i, 0)),
                      spec, spec, spec],
            out_specs=[spec] * 4),
        out_shape=[jax.ShapeDtypeStruct((k, n), F32)] * 4, name=name, compiler_params=_params(1),
    )(place, parts, gd, w, m, v)


def _attn_mask(n, rows):
    qi = lax.broadcasted_iota(jnp.int32, (rows, 2 * ATT_BLOCK), 0) & (ATT_BLOCK - 1)
    kj = lax.broadcasted_iota(jnp.int32, (rows, 2 * ATT_BLOCK), 1)
    rel = qi + ATT_BLOCK - kj
    return (rel >= 0) & (rel < ATT_BLOCK) & ((kj >= ATT_BLOCK) | (n > 0))


def _attn_probs(q, k, sink, mask):
    s = lax.dot_general(q, k, (((1,), (1,)), ((), ())), preferred_element_type=F32) * (HEAD_DIM ** -0.5)
    s = jnp.where(mask, s, NEG_INF)
    m = jnp.maximum(jnp.max(s, axis=-1, keepdims=True), sink)
    p = jnp.exp(s - m)
    e_sink = jnp.exp(sink - m)
    den = jnp.sum(p, axis=-1, keepdims=True) + e_sink
    return p / den, e_sink / den


def _attn_specs(qpk):
    blk = ATT_BLOCK
    q_spec = pl.BlockSpec((qpk, blk, HEAD_DIM), lambda h, n: (h, n, 0))
    cur = pl.BlockSpec((1, blk, HEAD_DIM), lambda h, n: (h, n, 0))
    prev = pl.BlockSpec((1, blk, HEAD_DIM), lambda h, n: (h, jnp.maximum(n - 1, 0), 0))
    sink_spec = pl.BlockSpec((1, qpk * blk, 1), lambda h, n: (h, 0, 0))
    return q_spec, cur, prev, sink_spec


def _attn_fwd(q, k, v, sinks):
    hq, l, _ = q.shape
    qpk = hq // N_KV_HEADS
    nb = l // ATT_BLOCK
    rows = qpk * ATT_BLOCK
    q_spec, cur, prev, sink_spec = _attn_specs(qpk)

    def body(q_ref, kp_ref, kc_ref, vp_ref, vc_ref, sink_ref, o_ref):
        mask = _attn_mask(pl.program_id(1), rows)
        kk = jnp.concatenate([kp_ref[0], kc_ref[0]], axis=0).astype(BF16)
        vv = jnp.concatenate([vp_ref[0], vc_ref[0]], axis=0).astype(BF16)
        p, _ = _attn_probs(q_ref[...].reshape(rows, HEAD_DIM).astype(BF16), kk, sink_ref[0], mask)
        o = jnp.dot(p.astype(BF16), vv, preferred_element_type=F32)
        o_ref[...] = o.reshape(qpk, ATT_BLOCK, HEAD_DIM).astype(o_ref.dtype)

    return pl.pallas_call(
        body, grid=(N_KV_HEADS, nb), in_specs=[q_spec, prev, cur, prev, cur, sink_spec],
        out_specs=q_spec, out_shape=jax.ShapeDtypeStruct((hq, l, HEAD_DIM), BF16),
        name="attn_fwd", compiler_params=_params(2),
    )(q, k, k, v, v, sinks)


def _attn_bwd(q, k, v, sinks, do):
    hq, l, _ = q.shape
    qpk = hq // N_KV_HEADS
    nb = l // ATT_BLOCK
    blk = ATT_BLOCK
    rows = qpk * blk
    q_spec, cur, prev, sink_spec = _attn_specs(qpk)
    part_spec = pl.BlockSpec((1, 1, 2 * blk, HEAD_DIM), lambda h, n: (h, n, 0, 0))
    dsink_spec = pl.BlockSpec((qpk, 1, LANES), lambda h, n: (h, 0, 0))
    tn = (((0,), (0,)), ((), ()))

    def body(q_ref, do_ref, kp_ref, kc_ref, vp_ref, vc_ref, sink_ref, dq_ref, dkp_ref, dvp_ref, dsink_ref):
        n = pl.program_id(1)
        mask = _attn_mask(n, rows)
        kk = jnp.concatenate([kp_ref[0], kc_ref[0]], axis=0).astype(BF16)
        vv = jnp.concatenate([vp_ref[0], vc_ref[0]], axis=0).astype(BF16)
        qb = q_ref[...].reshape(rows, HEAD_DIM).astype(BF16)
        do32 = do_ref[...].astype(F32).reshape(rows, HEAD_DIM)
        dob = do32.astype(BF16)
        p, p_sink = _attn_probs(qb, kk, sink_ref[0], mask)
        pb = p.astype(BF16)
        o = jnp.dot(pb, vv, preferred_element_type=F32)
        delta = jnp.sum(do32 * o, axis=-1, keepdims=True)
        dp = lax.dot_general(dob, vv, (((1,), (1,)), ((), ())), preferred_element_type=F32)
        ds = (p * (dp - delta) * (HEAD_DIM ** -0.5)).astype(BF16)
        dq = jnp.dot(ds, kk, preferred_element_type=F32)
        dq_ref[...] = dq.reshape(qpk, blk, HEAD_DIM).astype(dq_ref.dtype)
        dkp_ref[0, 0] = lax.dot_general(ds, qb, tn, preferred_element_type=F32)
        dvp_ref[0, 0] = lax.dot_general(pb, dob, tn, preferred_element_type=F32)
        dsr = p_sink * delta
        for g in range(qpk):
            dsg = jnp.broadcast_to(-_colsum(dsr[g * blk:(g + 1) * blk]), (1, LANES))

            @pl.when(n == 0)
            def _():
                dsink_ref[g] = dsg

            @pl.when(n > 0)
            def _():
                dsink_ref[g] += dsg


    part_shape = jax.ShapeDtypeStruct((N_KV_HEADS, nb, 2 * blk, HEAD_DIM), F32)
    dq, dkp, dvp, dsink = pl.pallas_call(
        body, grid=(N_KV_HEADS, nb), in_specs=[q_spec, q_spec, prev, cur, prev, cur, sink_spec],
        out_specs=[q_spec, part_spec, part_spec, dsink_spec],
        out_shape=[jax.ShapeDtypeStruct((hq, l, HEAD_DIM), BF16), part_shape, part_shape,
                   jax.ShapeDtypeStruct((hq, 1, LANES), F32)],
        name="attn_bwd", compiler_params=_params(2),
    )(q, do, k, k, v, v, sinks)

    def combine(a_cur, a_nxt, b_cur, b_nxt):
        last = pl.program_id(1) == nb - 1
        keep = jnp.where(last, 0.0, 1.0)
        return (a_cur[0, 0, blk:] + keep * a_nxt[0, 0, :blk])[None], (b_cur[0, 0, blk:] + keep * b_nxt[0, 0, :blk])[None]

    nxt_spec = pl.BlockSpec((1, 1, 2 * blk, HEAD_DIM), lambda h, n: (h, jnp.minimum(n + 1, nb - 1), 0, 0))
    kv_shape = jax.ShapeDtypeStruct((N_KV_HEADS, l, HEAD_DIM), BF16)
    dk, dv = _tile_call("attn_dkv", combine, (N_KV_HEADS, nb), [dkp, dkp, dvp, dvp],
                        [part_spec, nxt_spec, part_spec, nxt_spec], [kv_shape, kv_shape], [cur, cur])
    return dq, dk, dv, dsink


def _block_diag(m):
    j, gl, a, b = m.shape
    eye = jnp.eye(gl, dtype=m.dtype)
    return (m[:, :, :, None, :] * eye[None, :, None, :, None]).reshape(j, gl * a, gl * b)


def _diag_blocks(z, a):
    j = z.shape[0]
    gl = z.shape[1] // a
    b = z.shape[2] // gl
    d = jnp.diagonal(z.reshape(j, gl, a, gl, b), axis1=1, axis2=3)
    return d.transpose(0, 3, 1, 2)


def _s5_permute(src_ref, dst_ref, t_len):
    seg = t_len // 8

    def body(k, carry):
        dst_ref[pl.ds(pl.multiple_of(8 * k, 8), 8), :] = src_ref[pl.ds(k, 8, stride=seg), :]
        return carry

    lax.fori_loop(0, seg, body, 0)


def _s5_unpermuted_rows(perm_ref, m, t_len):
    per_seg = t_len // 64
    return perm_ref[pl.ds(64 * (m % per_seg) + m // per_seg, 8, stride=8), :]


def _s5_powers(p_ref, lr, li, seg):
    hs = TILE_STATES

    def step(k, carry):
        pr, pi = carry
        p_ref[pl.ds(k, 1), 0:hs] = pr
        p_ref[pl.ds(k, 1), hs:2 * hs] = pi
        return lr * pr - li * pi, lr * pi + li * pr

    lax.fori_loop(0, seg, step, (lr, li))


def _s5_local_scan(x_ref, base, lr, li, seg, reverse):
    hs = TILE_STATES
    lr8, li8 = jnp.broadcast_to(lr, (8, hs)), jnp.broadcast_to(li, (8, hs))
    if reverse:
        li8 = -li8

    def step(i, carry):
        hr, hi = carry
        k = seg - 1 - i if reverse else i
        rows = pl.ds(pl.multiple_of(base + 8 * k, 8), 8)
        nr = lr8 * hr - li8 * hi + x_ref[rows, 0:hs]
        ni = lr8 * hi + li8 * hr + x_ref[rows, hs:2 * hs]
        x_ref[rows, 0:hs] = nr
        x_ref[rows, hs:2 * hs] = ni
        return nr, ni

    zero = jnp.zeros((8, hs), F32)
    return lax.fori_loop(0, seg, step, (zero, zero), unroll=2)


def _s5_carries(c_ref, e_ref, ends, start, pw_r, pw_i, reverse):
    hs = TILE_STATES
    e_ref[:, 0:hs] = ends[0]
    e_ref[:, hs:2 * hs] = ends[1]
    cr, ci = start
    if reverse:
        pw_i = -pw_i
    for s in (range(7, -1, -1) if reverse else range(8)):
        c_ref[s:s + 1, 0:hs] = cr
        c_ref[s:s + 1, hs:2 * hs] = ci
        er, ei = e_ref[s:s + 1, 0:hs], e_ref[s:s + 1, hs:2 * hs]
        cr, ci = er + pw_r * cr - pw_i * ci, ei + pw_r * ci + pw_i * cr
    return cr, ci


def _s5_states(u_perm_b16, bd_ref, x_ref, base, c_ref, e_ref, p_ref, lr, li, h_in, t_len):
    hs = TILE_STATES
    seg = t_len // 8
    x_ref[pl.ds(base, t_len), :] = jnp.dot(u_perm_b16, bd_ref[0], preferred_element_type=F32)
    ends = _s5_local_scan(x_ref, base, lr, li, seg, False)
    pw_r, pw_i = p_ref[seg - 1:seg, 0:hs], p_ref[seg - 1:seg, hs:2 * hs]
    h_out = _s5_carries(c_ref, e_ref, ends, h_in, pw_r, pw_i, False)
    cr, ci = c_ref[:, 0:hs], c_ref[:, hs:2 * hs]

    def fix(k, carry):
        rows = pl.ds(pl.multiple_of(base + 8 * k, 8), 8)
        pr, pi = p_ref[pl.ds(k, 1), 0:hs], p_ref[pl.ds(k, 1), hs:2 * hs]
        x_ref[rows, 0:hs] += pr * cr - pi * ci
        x_ref[rows, hs:2 * hs] += pr * ci + pi * cr
        return carry

    lax.fori_loop(0, seg, fix, 0, unroll=2)
    return h_out


def _s5_fwd(proj, u_off, bd, cbd, lam, dvec, t_len):
    l = proj.shape[0]
    nj = bd.shape[0]
    nch = l // t_len
    hs = TILE_STATES
    ub = u_off // LANES
    seg = t_len // 8
    assert t_len % 64 == 0

    def body(u_ref, bd_ref, cbd_ref, lam_ref, d_ref, y_ref, hst_ref, x_ref, h_ref, p_ref, c_ref, e_ref, up_ref, yp_ref):
        lr, li = lam_ref[0, 0:1, :], lam_ref[0, 1:2, :]

        @pl.when(pl.program_id(1) == 0)
        def _():
            h_ref[...] = jnp.zeros_like(h_ref)
            _s5_powers(p_ref, lr, li, seg)

        hst_ref[0, 0] = h_ref[...]
        _s5_permute(u_ref, up_ref, t_len)
        h_out = _s5_states(up_ref[...].astype(BF16), bd_ref, x_ref, 0, c_ref, e_ref, p_ref, lr, li,
                           (h_ref[:, 0:hs], h_ref[:, hs:2 * hs]), t_len)
        h_ref[:, 0:hs] = h_out[0]
        h_ref[:, hs:2 * hs] = h_out[1]
        yp_ref[...] = jnp.dot(x_ref[...].astype(BF16), cbd_ref[0], preferred_element_type=F32)
        dv = d_ref[0]

        def out(m, carry):
            rows = pl.ds(pl.multiple_of(8 * m, 8), 8)
            y_ref[rows, :] = _s5_unpermuted_rows(yp_ref, m, t_len) + dv * u_ref[rows, :]
            return carry

        lax.fori_loop(0, seg, out, 0)

    return pl.pallas_call(
        body, grid=(nj, nch),
        in_specs=[pl.BlockSpec((t_len, LANES), lambda j, c: (c, ub + j)),
                  pl.BlockSpec((1, LANES, 2 * hs), lambda j, c: (j, 0, 0)),
                  pl.BlockSpec((1, 2 * hs, LANES), lambda j, c: (j, 0, 0)),
                  pl.BlockSpec((1, 2, hs), lambda j, c: (j, 0, 0)),
                  pl.BlockSpec((1, 1, LANES), lambda j, c: (j, 0, 0))],
        out_specs=[pl.BlockSpec((t_len, LANES), lambda j, c: (c, j)),
                   pl.BlockSpec((1, 1, 1, 2 * hs), lambda j, c: (j, c, 0, 0))],
        out_shape=[jax.ShapeDtypeStruct((l, nj * LANES), F32),
                   jax.ShapeDtypeStruct((nj, nch, 1, 2 * hs), F32)],
        scratch_shapes=[pltpu.VMEM((t_len, 2 * hs), F32), pltpu.VMEM((1, 2 * hs), F32),
                        pltpu.VMEM((seg, 2 * hs), F32), pltpu.VMEM((8, 2 * hs), F32), pltpu.VMEM((8, 2 * hs), F32),
                        pltpu.VMEM((t_len, LANES), F32), pltpu.VMEM((t_len, LANES), F32)],
        name="s5_fwd", compiler_params=_params(2),
    )(proj, bd, cbd, lam, dvec)


def _s5_bwd(proj, u_off, dy, hst, bd, bdt, cbdt, lam, dvec, t_len):
    l = proj.shape[0]
    nj = bd.shape[0]
    nch = l // t_len
    hs = TILE_STATES
    ub = u_off // LANES
    seg = t_len // 8
    tn = (((0,), (0,)), ((), ()))
    assert t_len % 64 == 0

    def body(u_ref, dy_ref, hst_ref, bd_ref, bdt_ref, cbdt_ref, lam_ref, d_ref,
             du_ref, dbd_ref, dcbdt_ref, dlam_ref, dd_ref,
             x_ref, g_ref, gc_ref, p_ref, c_ref, e_ref, up_ref, dyp_ref, dup_ref):
        first = pl.program_id(1) == 0
        lr, li = lam_ref[0, 0:1, :], lam_ref[0, 1:2, :]

        @pl.when(first)
        def _():
            gc_ref[...] = jnp.zeros_like(gc_ref)
            _s5_powers(p_ref, lr, li, seg)

        _s5_permute(u_ref, up_ref, t_len)
        _s5_permute(dy_ref, dyp_ref, t_len)
        ub16, dyb16 = up_ref[...].astype(BF16), dyp_ref[...].astype(BF16)
        h0 = hst_ref[0, 0]
        _s5_states(ub16, bd_ref, x_ref, 8, c_ref, e_ref, p_ref, lr, li, (h0[:, 0:hs], h0[:, hs:2 * hs]), t_len)
        x_ref[0:8, :] = c_ref[...]
        g_ref[...] = jnp.dot(dyb16, cbdt_ref[0], preferred_element_type=F32)
        starts = _s5_local_scan(g_ref, 0, lr, li, seg, True)
        pw_r, pw_i = p_ref[seg - 1:seg, 0:hs], p_ref[seg - 1:seg, hs:2 * hs]
        g_out = _s5_carries(c_ref, e_ref, starts, (gc_ref[:, 0:hs], gc_ref[:, hs:2 * hs]), pw_r, pw_i, True)
        gc_ref[:, 0:hs] = g_out[0]
        gc_ref[:, hs:2 * hs] = g_out[1]
        cr, ci = c_ref[:, 0:hs], c_ref[:, hs:2 * hs]

        def fix(k, carry):
            alr, ali = carry
            rows = pl.ds(pl.multiple_of(8 * k, 8), 8)
            pr, pi = p_ref[pl.ds(seg - 1 - k, 1), 0:hs], p_ref[pl.ds(seg - 1 - k, 1), hs:2 * hs]
            gr = g_ref[rows, 0:hs] + pr * cr + pi * ci
            gi = g_ref[rows, hs:2 * hs] + pr * ci - pi * cr
            g_ref[rows, 0:hs] = gr
            g_ref[rows, hs:2 * hs] = gi
            hpr, hpi = x_ref[rows, 0:hs], x_ref[rows, hs:2 * hs]
            return alr + gr * hpr + gi * hpi, ali + gi * hpr - gr * hpi

        zero = jnp.zeros((8, hs), F32)
        alr, ali = lax.fori_loop(0, seg, fix, (zero, zero), unroll=2)
        alr, ali = _colsum(alr), _colsum(ali)
        g = g_ref[...].astype(BF16)
        h = x_ref[pl.ds(8, t_len), :].astype(BF16)
        dup_ref[...] = jnp.dot(g, bdt_ref[0], preferred_element_type=F32)
        dv = d_ref[0]

        def out(m, carry):
            rows = pl.ds(pl.multiple_of(8 * m, 8), 8)
            du_ref[rows, :] = (_s5_unpermuted_rows(dup_ref, m, t_len) + dv * dy_ref[rows, :]).astype(du_ref.dtype)
            return carry

        lax.fori_loop(0, seg, out, 0)
        sign = jnp.where(lax.broadcasted_iota(jnp.int32, (1, 2 * hs), 1) < hs, 1.0, -1.0)
        dbd = lax.dot_general(ub16, g, tn, preferred_element_type=F32)
        dcbdt = lax.dot_general(dyb16, h, tn, preferred_element_type=F32) * sign
        ddv = _colsum(dy_ref[...] * u_ref[...])

        @pl.when(first)
        def _():
            dbd_ref[0] = dbd
            dcbdt_ref[0] = dcbdt
            dlam_ref[0, 0:1, :] = alr
            dlam_ref[0, 1:2, :] = ali
            dd_ref[0] = ddv

        @pl.when(jnp.logical_not(first))
        def _():
            dbd_ref[0] += dbd
            dcbdt_ref[0] += dcbdt
            dlam_ref[0, 0:1, :] += alr
            dlam_ref[0, 1:2, :] += ali
            dd_ref[0] += ddv

    rev = lambda c: nch - 1 - c
    wide = pl.BlockSpec((1, LANES, 2 * hs), lambda j, c: (j, 0, 0))
    tall = pl.BlockSpec((1, 2 * hs, LANES), lambda j, c: (j, 0, 0))
    return pl.pallas_call(
        body, grid=(nj, nch),
        in_specs=[pl.BlockSpec((t_len, LANES), lambda j, c: (rev(c), ub + j)),
                  pl.BlockSpec((t_len, LANES), lambda j, c: (rev(c), j)),
                  pl.BlockSpec((1, 1, 1, 2 * hs), lambda j, c: (j, rev(c), 0, 0)),
                  wide, tall, wide,
                  pl.BlockSpec((1, 2, hs), lambda j, c: (j, 0, 0)),
                  pl.BlockSpec((1, 1, LANES), lambda j, c: (j, 0, 0))],
        out_specs=[pl.BlockSpec((t_len, LANES), lambda j, c: (rev(c), j)),
                   wide, wide,
                   pl.BlockSpec((1, 2, hs), lambda j, c: (j, 0, 0)),
                   pl.BlockSpec((1, 1, LANES), lambda j, c: (j, 0, 0))],
        out_shape=[jax.ShapeDtypeStruct((l, nj * LANES), BF16),
                   jax.ShapeDtypeStruct((nj, LANES, 2 * hs), F32),
                   jax.ShapeDtypeStruct((nj, LANES, 2 * hs), F32),
                   jax.ShapeDtypeStruct((nj, 2, hs), F32),
                   jax.ShapeDtypeStruct((nj, 1, LANES), F32)],
        scratch_shapes=[pltpu.VMEM((t_len + 8, 2 * hs), F32), pltpu.VMEM((t_len, 2 * hs), F32),
                        pltpu.VMEM((1, 2 * hs), F32), pltpu.VMEM((seg, 2 * hs), F32),
                        pltpu.VMEM((8, 2 * hs), F32), pltpu.VMEM((8, 2 * hs), F32),
                        pltpu.VMEM((t_len, LANES), F32), pltpu.VMEM((t_len, LANES), F32),
                        pltpu.VMEM((t_len, LANES), F32)],
        name="s5_bwd", compiler_params=_params(2),
    )(proj, dy, hst, bd, bdt, cbdt, lam, dvec)


def _full_spec(shape):
    nd = len(shape)
    return pl.BlockSpec(tuple(shape), lambda i: (0,) * nd)


def _sds(shape, dtype=F32):
    return jax.ShapeDtypeStruct(tuple(shape), dtype)


def kernel(x, c, ada_w, ada_b, norm_mix_g, w_in, attn_sinks, w_attn_proj, ssm_a_re, ssm_a_im, ssm_log_dt, ssm_b_re, ssm_b_im, ssm_c_re, ssm_c_im, ssm_d, w_ssm_glu, w_out, norm_ffn_g, w_ffn_up, ffn_conv_w, ffn_conv_b, w_ffn_down, final_g, loss_target, m_ada_w, m_ada_b, m_norm_mix_g, m_w_in, m_attn_sinks, m_w_attn_proj, m_ssm_a_re, m_ssm_a_im, m_ssm_log_dt, m_ssm_b_re, m_ssm_b_im, m_ssm_c_re, m_ssm_c_im, m_ssm_d, m_w_ssm_glu, m_w_out, m_norm_ffn_g, m_w_ffn_up, m_ffn_conv_w, m_ffn_conv_b, m_w_ffn_down, m_final_g, v_ada_w, v_ada_b, v_norm_mix_g, v_w_in, v_attn_sinks, v_w_attn_proj, v_ssm_a_re, v_ssm_a_im, v_ssm_log_dt, v_ssm_b_re, v_ssm_b_im, v_ssm_c_re, v_ssm_c_im, v_ssm_d, v_w_ssm_glu, v_w_out, v_norm_ffn_g, v_w_ffn_up, v_ffn_conv_w, v_ffn_conv_b, v_w_ffn_down, v_final_g):
    given = dict(locals())
    names = ['ada_w', 'ada_b', 'norm_mix_g', 'w_in', 'attn_sinks', 'w_attn_proj', 'ssm_a_re', 'ssm_a_im',
             'ssm_log_dt', 'ssm_b_re', 'ssm_b_im', 'ssm_c_re', 'ssm_c_im', 'ssm_d', 'w_ssm_glu', 'w_out',
             'norm_ffn_g', 'w_ffn_up', 'ffn_conv_w', 'ffn_conv_b', 'w_ffn_down', 'final_g']

    xs = x[0]
    tgt = loss_target[0]
    l, d = xs.shape
    attn_w = w_attn_proj.shape[1]
    ssm_w = w_ssm_glu.shape[1]
    hq = attn_sinks.shape[1]
    qpk = hq // N_KV_HEADS
    kv_w = N_KV_HEADS * HEAD_DIM
    n_groups = ssm_a_re.shape[1]
    dff = ffn_conv_b.shape[1]
    in_w = attn_w + 2 * kv_w + ssm_w + 2 * d
    nj = ssm_w // LANES
    off_k, off_v, off_u = attn_w, attn_w + kv_w, attn_w + 2 * kv_w
    off_ga, off_gs = off_u + ssm_w, off_u + ssm_w + d
    assert hq * HEAD_DIM == attn_w and n_groups * SSM_P == ssm_w and l % ATT_BLOCK == 0

    xi, yi, ci = _dev()
    idx = 4 * xi + 2 * yi + ci

    row_sharded = {'w_out': (d, d), 'w_ffn_down': (dff, d)}
    big = ['w_in', 'w_attn_proj', 'w_ssm_glu', 'w_out', 'w_ffn_up', 'w_ffn_down']
    spack, s_offs = _pack([c, ffn_conv_w[0]], LANES, 8)
    w16 = {k: given[k][0].astype(BF16) for k in big}
    wg_in, sg = _all_gather("gather_first", [w16['w_in'], spack])
    mixer_w = ['w_attn_proj', 'w_ssm_glu', 'w_out']
    h_mixer, tok = _exchange_start("gather_mixer_start", [w16[k] for k in mixer_w], True, wg_in)
    h_up, tok = _exchange_start("gather_ffn_up_start", [w16['w_ffn_up']], True, tok)
    h_down, tok = _exchange_start("gather_ffn_down_start", [w16['w_ffn_down']], True, tok)
    full = {'w_in': wg_in.transpose(1, 0, 2).reshape(d, in_w)}
    c_all = _unpack(sg, s_offs[0], (d,), lead=(N_DEV,))
    conv_w = _unpack(sg, s_offs[1], ffn_conv_w.shape[1:], lead=(N_DEV,)).transpose(1, 0, 2).reshape(3, dff)
    conv_b = ffn_conv_b

    mod_n = ada_w.shape[2]
    tcm = _pick(mod_n, 512)
    ada_b_mine = lax.dynamic_slice_in_dim(ada_b, idx * mod_n, mod_n, axis=1)

    def modpart_fn(cv, wv, bv):
        cond = cv * jax.nn.sigmoid(cv)
        return jnp.dot(cond.astype(BF16), wv.astype(BF16), preferred_element_type=F32) + bv, cond

    modp, cond_all = _tile_call(
        "ada_rows", modpart_fn, (mod_n // tcm,), [c_all, ada_w[0], ada_b_mine],
        [pl.BlockSpec((N_DEV, d), lambda j: (0, 0)), pl.BlockSpec((d, tcm), lambda j: (0, j)),
         pl.BlockSpec((1, tcm), lambda j: (0, j))],
        [_sds((N_DEV, mod_n)), _sds((N_DEV, d))],
        [pl.BlockSpec((N_DEV, tcm), lambda j: (0, j)), pl.BlockSpec((N_DEV, d), lambda j: (0, 0))])
    (modg,) = _all_gather("gather_ada_rows", [modp])
    mod = lax.dynamic_index_in_dim(modg, idx, axis=1, keepdims=False).reshape(1, N_DEV * mod_n)
    sh1, sc1, g1, sh2, sc2, g2 = [mod[:, i * d:(i + 1) * d] for i in range(6)]

    tr = _pick(l, 256, 8)
    trh = _pick(l, 128, 8)
    nr, nrh = l // tr, l // trh
    g_mix, g_ffn, g_fin = norm_mix_g + tok[0:1, 0:1], norm_ffn_g, final_g.reshape(1, d)

    h1 = _tile_call("norm_mod_mix", _norm_mod, (1, nr), [xs, g_mix, sc1, sh1],
                    [_t(tr, d), _v(d), _v(d), _v(d)], [_sds((l, d), BF16)], [_t(tr, d)])[0]
    proj = _matmul("proj_in", h1, full['w_in'], "nn", tn=1280)

    def heads(z, n):
        return z.reshape(l, n, HEAD_DIM).transpose(1, 0, 2)

    qh = heads(proj[:, :attn_w], hq)
    kh = heads(proj[:, off_k:off_k + kv_w], N_KV_HEADS)
    vh = heads(proj[:, off_v:off_v + kv_w], N_KV_HEADS)
    sinks3 = jnp.repeat(attn_sinks.reshape(N_KV_HEADS, qpk), ATT_BLOCK, axis=1)[..., None]
    o_h = _attn_fwd(qh, kh, vh, sinks3)
    o2 = o_h.transpose(1, 0, 2).reshape(l, attn_w)

    gn = (n_groups, SSM_N)
    pgn = (SSM_P, n_groups, SSM_N)
    a_re, a_im, log_dt = ssm_a_re[0], ssm_a_im[0], ssm_log_dt[0].reshape(n_groups, 1)
    b_re, b_im = ssm_b_re[0].transpose(2, 0, 1), ssm_b_im[0].transpose(2, 0, 1)
    disc_ins = [a_re, a_im, log_dt, b_re, b_im]
    disc_specs = [_full_spec(gn), _full_spec(gn), _full_spec((n_groups, 1)), _full_spec(pgn), _full_spec(pgn)]
    lam_r, lam_i, bb_r, bb_i = _tile_call(
        "s5_discretise", _s5_disc_fn, (1,), disc_ins, disc_specs,
        [_sds(gn), _sds(gn), _sds(pgn), _sds(pgn)],
        [_full_spec(gn), _full_spec(gn), _full_spec(pgn), _full_spec(pgn)])

    def tiles_gpn(z):
        return z.reshape(SSM_P, nj, TILE_GROUPS, SSM_N).transpose(1, 2, 0, 3)

    bd = jnp.concatenate([_block_diag(tiles_gpn(bb_r)), _block_diag(tiles_gpn(bb_i))], axis=2).astype(BF16)
    c_r = ssm_c_re[0].reshape(nj, TILE_GROUPS, SSM_P, SSM_N).transpose(0, 1, 3, 2)
    c_i = (-ssm_c_im[0]).reshape(nj, TILE_GROUPS, SSM_P, SSM_N).transpose(0, 1, 3, 2)
    cbd = jnp.concatenate([_block_diag(c_r), _block_diag(c_i)], axis=1).astype(BF16)
    bdt, cbdt = bd.transpose(0, 2, 1), cbd.transpose(0, 2, 1)
    lam = jnp.stack([lam_r.reshape(nj, TILE_STATES), lam_i.reshape(nj, TILE_STATES)], axis=1)
    dvec = ssm_d[0].reshape(nj, 1, LANES)
    t_len = _pick(l, 512, 8)
    y, hst = _s5_fwd(proj, off_u, bd, cbd, lam, dvec, t_len)

    tcs = _pick(ssm_w, 512)
    gy = _tile_call("gelu", lambda v: jax.nn.gelu(v), (ssm_w // tcs, nr), [y], [_t(tr, tcs)],
                    [_sds((l, ssm_w), BF16)], [_t(tr, tcs)])[0]
    full.update(zip(mixer_w, _exchange_wait("gather_mixer_wait", h_mixer, gy)))
    full['w_out'] = full['w_out'].reshape(row_sharded['w_out'])
    attn = _matmul("attn_proj", o2, full['w_attn_proj'], "nn")
    glu = _matmul("ssm_glu", gy, full['w_ssm_glu'], "nn")

    tcd = 256 if d % 256 == 0 and off_ga % 256 == 0 else LANES
    assert d % tcd == 0 and off_ga % tcd == 0 and off_gs % tcd == 0
    mix_in_specs = [_t(tr, tcd), _t(tr, tcd, d), _t(tr, tcd), _t(tr, tcd, off_ga), _t(tr, tcd, off_gs)]
    mixed = _tile_call("gate_mix", _mix_fn, (d // tcd, nr), [glu, glu, attn, proj, proj], mix_in_specs,
                       [_sds((l, d), BF16)], [_t(tr, tcd)])[0]
    mixout = _matmul("mix_out", mixed, full['w_out'], "nn")

    def res_norm_fn(xv, mo, g1v, gv, scv, shv):
        x2v = xv + g1v * mo
        return x2v, _norm_mod(x2v, gv, scv, shv)

    x2, h2 = _tile_call("residual_norm_mod_ffn", res_norm_fn, (1, nr), [xs, mixout, g1, g_ffn, sc2, sh2],
                        [_t(tr, d), _t(tr, d), _v(d), _v(d), _v(d), _v(d)],
                        [_sds((l, d)), _sds((l, d), BF16)], [_t(tr, d), _t(tr, d)])
    full['w_ffn_up'], = _exchange_wait("gather_ffn_up_wait", h_up, h2)
    up = _matmul("ffn_up", h2, full['w_ffn_up'], "nn", tn=1408)

    tcf = _pick(dff, 512)
    assert dff % tcf == 0
    ncf = dff // tcf

    taps = [conv_w[i:i + 1] for i in range(3)]

    def conv_gate(gp, gp_prev, w0, w1, w2, bv):
        prev = jnp.where(pl.program_id(1) == 0, 0.0, 1.0) * gp_prev
        ext = jnp.concatenate([prev, gp], axis=0)
        m1 = pltpu.roll(ext, 1, 0)[8:]
        m2 = pltpu.roll(ext, 2, 0)[8:]
        return w0 * m2 + w1 * m1 + w2 * gp + bv, m1, m2

    def convglu_fn(gp, gp_prev, val, w0, w1, w2, bv):
        gate, _, _ = conv_gate(gp, gp_prev, w0, w1, w2, bv)
        return gate * jax.nn.sigmoid(gate) * val

    act = _tile_call("conv_swiglu", convglu_fn, (ncf, nr), [up, up, up] + taps + [conv_b],
                     [_t(tr, tcf), _prev8(tr, tcf), _t(tr, tcf, dff)] + [_v(tcf)] * 4,
                     [_sds((l, dff), BF16)], [_t(tr, tcf)])[0]
    full['w_ffn_down'] = _exchange_wait("gather_ffn_down_wait", h_down, act)[0].reshape(row_sharded['w_ffn_down'])
    ffn = _matmul("ffn_down", act, full['w_ffn_down'], "nn")

    def final_fn(x2v, fv, g2v, gv, tv):
        rows = x2v.shape[0]

        def loss_of(x2a, fa, g2a, ga):
            out = _rms(x2a + g2a * fa, ga)
            err = out - tv
            return 0.5 * _colsum(jnp.mean(err * err, axis=-1, keepdims=True))

        loss, vjp = jax.vjp(loss_of, x2v, fv, _bc(g2v, rows), _bc(gv, rows))
        dx3, dffn, dg2, dgf = vjp(jnp.ones((1, 1), F32))
        return jnp.broadcast_to(loss, (1, LANES)), dx3, dffn, _colsum(dg2), _colsum(dgf)

    loss_p, dx3, dffn, dg2, dg_fin = _tile_call(
        "loss_final_norm", final_fn, (1, nrh), [x2, ffn, g2, g_fin, tgt],
        [_t(trh, d), _t(trh, d), _v(d), _v(d), _t(trh, d)],
        [_sds((1, LANES)), _sds((l, d)), _sds((l, d), BF16), _sds((1, d)), _sds((1, d))],
        [_v(LANES), _t(trh, d), _t(trh, d), _v(d), _v(d)], acc=(0, 3, 4))
    loss = lax.psum(loss_p[0, 0], ("x", "y", "c"))

    dact = _matmul("d_act", dffn, full['w_ffn_down'], "nt", tn=1408)
    gd, gd16, pending = {}, {}, []
    dw_down, dw_down16 = _matmul("dw_ffn_down", act, dffn, "tn", tm=1408, also_bf16=True)
    gd['w_ffn_down'], gd16['w_ffn_down'] = [z.reshape((N_DEV,) + w_ffn_down.shape[1:]) for z in (dw_down, dw_down16)]
    handle, tok = _exchange_start("grad_ffn_down_start", [gd16['w_ffn_down']], False, dw_down)
    pending.append((['w_ffn_down'], handle))
    conv_b_bwd = conv_b + tok[0:1, 0:1]

    def convglu_bwd_fn(gp, gp_prev, val, da, w0, w1, w2, bv):
        gate, m1, m2 = conv_gate(gp, gp_prev, w0, w1, w2, bv)
        sg = jax.nn.sigmoid(gate)
        dgate = da * val * (sg * (1.0 + gate * (1.0 - sg)))
        return (dgate, da * (gate * sg), _colsum(dgate), _colsum(dgate * m2), _colsum(dgate * m1),
                _colsum(dgate * gp))

    dgate, dval, dconv_b, dcw0, dcw1, dcw2 = _tile_call(
        "conv_swiglu_bwd", convglu_bwd_fn, (ncf, nr), [up, up, up, dact] + taps + [conv_b_bwd],
        [_t(tr, tcf), _prev8(tr, tcf), _t(tr, tcf, dff), _t(tr, tcf)] + [_v(tcf)] * 4,
        [_sds((l, dff)), _sds((l, dff), BF16)] + [_sds((1, dff))] * 4,
        [_t(tr, tcf), _t(tr, tcf)] + [_v(tcf)] * 4, acc=(2, 3, 4, 5))

    def conv_t_fn(dg, dg_next, dv, w0, w1, w2):
        rows = dg.shape[0]
        nxt = jnp.where(pl.program_id(1) == pl.num_programs(1) - 1, 0.0, 1.0) * dg_next
        ext = jnp.concatenate([dg, nxt], axis=0)
        p1 = pltpu.roll(ext, rows + 8 - 1, 0)[:rows]
        p2 = pltpu.roll(ext, rows + 8 - 2, 0)[:rows]
        return jnp.stack([w2 * dg + w1 * p1 + w0 * p2, dv.astype(F32)], axis=0)

    dup = _tile_call("conv_transpose", conv_t_fn, (ncf, nr), [dgate, dgate, dval] + taps,
                     [_t(tr, tcf), _next8(tr, tcf, l), _t(tr, tcf)] + [_v(tcf)] * 3,
                     [_sds((2, l, dff), BF16)], [_st(tr, tcf)])[0]
    dh2 = _matmul("d_h2", dup, full['w_ffn_up'], "nt")
    gd['w_ffn_up'], gd16['w_ffn_up'] = _matmul("dw_ffn_up", h2, dup, "tn", tn=1408, out_stack=N_DEV, also_bf16=True)
    handle, tok = _exchange_start("grad_ffn_up_start", [gd16['w_ffn_up']], False, gd['w_ffn_up'])
    pending.append((['w_ffn_up'], handle))
    g_ffn_bwd = g_ffn + tok[0:1, 0:1]

    def res_norm_bwd_fn(xv, mo, g1v, gv, scv, shv, dhv, dxv):
        rows = xv.shape[0]
        _, vjp = jax.vjp(res_norm_fn, xv, mo, _bc(g1v, rows), _bc(gv, rows), _bc(scv, rows), _bc(shv, rows))
        dx, dmo, dg1v, dgv, dscv, dshv = vjp((dxv, dhv))
        return dx, dmo, _colsum(dg1v), _colsum(dgv), _colsum(dscv), _colsum(dshv)

    dx2, dmixout, dg1, dg_ffn, dsc2, dsh2 = _tile_call(
        "residual_norm_mod_ffn_bwd", res_norm_bwd_fn, (1, nrh), [xs, mixout, g1, g_ffn_bwd, sc2, sh2, dh2, dx3],
        [_t(trh, d), _t(trh, d), _v(d), _v(d), _v(d), _v(d), _t(trh, d), _t(trh, d)],
        [_sds((l, d)), _sds((l, d), BF16)] + [_sds((1, d))] * 4,
        [_t(trh, d), _t(trh, d)] + [_v(d)] * 4, acc=(2, 3, 4, 5))

    dmixed = _matmul("d_mixed", dmixout, full['w_out'], "nt")
    dw_out, dw_out16 = _matmul("dw_out", mixed, dmixout, "tn", also_bf16=True)
    gd['w_out'], gd16['w_out'] = [z.reshape((N_DEV,) + w_out.shape[1:]) for z in (dw_out, dw_out16)]

    def mix_bwd_fn(ga_, gb_, at, pa, ps, dm):
        _, vjp = jax.vjp(_mix_fn, ga_, gb_, at, pa, ps)
        da, db, dat, dpa, dps = vjp(dm)
        return jnp.stack([da, db], axis=0), dat, dpa, dps

    dglu, dattn, dga, dgs = _tile_call(
        "gate_mix_bwd", mix_bwd_fn, (d // tcd, nr), [glu, glu, attn, proj, proj, dmixed],
        mix_in_specs + [_t(tr, tcd)],
        [_sds((2, l, d), BF16)] + [_sds((l, d), BF16)] * 3, [_st(tr, tcd)] + [_t(tr, tcd)] * 3)

    dgy = _matmul("d_gelu_y", dglu, full['w_ssm_glu'], "nt")
    gd['w_ssm_glu'], gd16['w_ssm_glu'] = _matmul("dw_ssm_glu", gy, dglu, "tn", out_stack=N_DEV, also_bf16=True)

    def gelu_bwd_fn(yv, dv):
        _, vjp = jax.vjp(lambda z: jax.nn.gelu(z), yv)
        return vjp(dv)[0]

    dy = _tile_call("gelu_bwd", gelu_bwd_fn, (ssm_w // tcs, nr), [y, dgy], [_t(tr, tcs), _t(tr, tcs)],
                    [_sds((l, ssm_w))], [_t(tr, tcs)])[0]
    du, dbd, dcbdt, dlam, dd_tiles = _s5_bwd(proj, off_u, dy, hst, bd, bdt, cbdt, lam, dvec, t_len)

    def gpn_of(z):
        return z.transpose(2, 0, 1, 3).reshape(pgn)

    dbb_r = gpn_of(_diag_blocks(dbd[:, :, :TILE_STATES], SSM_P))
    dbb_i = gpn_of(_diag_blocks(dbd[:, :, TILE_STATES:], SSM_P))
    dc_re = _diag_blocks(dcbdt[:, :, :TILE_STATES], SSM_P).reshape(n_groups, SSM_P, SSM_N)
    dc_im = _diag_blocks(dcbdt[:, :, TILE_STATES:], SSM_P).reshape(n_groups, SSM_P, SSM_N)
    dlam_r, dlam_i = dlam[:, 0].reshape(gn), dlam[:, 1].reshape(gn)

    def disc_bwd_fn(ar, ai, ld, br, bi, dlr, dli, dbr, dbi):
        _, vjp = jax.vjp(_s5_disc_fn, ar, ai, ld, br, bi)
        return vjp((dlr, dli, dbr, dbi))

    da_re, da_im, dlog_dt, db_re, db_im = _tile_call(
        "s5_discretise_bwd", disc_bwd_fn, (1,), disc_ins + [dlam_r, dlam_i, dbb_r, dbb_i],
        disc_specs + [_full_spec(gn), _full_spec(gn), _full_spec(pgn), _full_spec(pgn)],
        [_sds(gn), _sds(gn), _sds((n_groups, 1)), _sds(pgn), _sds(pgn)], disc_specs)

    do2 = _matmul("d_attn_heads", dattn, full['w_attn_proj'], "nt")
    gd['w_attn_proj'], gd16['w_attn_proj'] = _matmul("dw_attn_proj", o2, dattn, "tn", out_stack=N_DEV, also_bf16=True)
    handle, tok = _exchange_start("grad_mixer_start", [gd16[k] for k in mixer_w], False, gd['w_attn_proj'])
    pending.append((mixer_w, handle))
    do_h = heads(do2.astype(BF16), hq)
    dq_h, dk_h, dv_h, dsink = _attn_bwd(qh, kh, vh, sinks3 + tok[0:1, 0:1], do_h)

    def unheads(z):
        return z.transpose(1, 0, 2).reshape(l, z.shape[0] * HEAD_DIM)

    dproj = jnp.concatenate([unheads(dq_h), unheads(dk_h), unheads(dv_h), du, dga, dgs], axis=1)
    dw_in, dw_in16 = _matmul("dw_in", h1, dproj, "tn", tn=1280, also_bf16=True)
    dcw = jnp.concatenate([dcw0, dcw1, dcw2], axis=0)
    shard_in, shard_cw = w_in.shape[1:], ffn_conv_w.shape[1:]
    gd['w_in'], gd16['w_in'] = [z.reshape(shard_in[0], N_DEV, shard_in[1]).transpose(1, 0, 2) for z in (dw_in, dw_in16)]
    gd['ffn_conv_w'] = dcw.reshape(shard_cw[0], N_DEV, shard_cw[1]).transpose(1, 0, 2)
    gd16['ffn_conv_w'] = gd['ffn_conv_w'].astype(BF16)
    handle, tok = _exchange_start("grad_in_start", [gd16['w_in'], gd16['ffn_conv_w']], False, dw_in)
    pending.append((['w_in', 'ffn_conv_w'], handle))
    dh1 = _matmul("d_h1", dproj, full['w_in'], "nt", dep=tok)

    def norm_bwd_fn(xv, gv, scv, shv, dhv, dxv):
        rows = xv.shape[0]
        _, vjp = jax.vjp(_norm_mod, xv, _bc(gv, rows), _bc(scv, rows), _bc(shv, rows))
        dx, dgv, dscv, dshv = vjp(dhv)
        return dx + dxv, _colsum(dgv), _colsum(dscv), _colsum(dshv)

    grad_x, dg_mix, dsc1, dsh1 = _tile_call(
        "norm_mod_mix_bwd", norm_bwd_fn, (1, nrh), [xs, g_mix, sc1, sh1, dh1, dx2],
        [_t(trh, d), _v(d), _v(d), _v(d), _t(trh, d), _t(trh, d)],
        [_sds((l, d))] + [_sds((1, d))] * 3, [_t(trh, d)] + [_v(d)] * 3, acc=(1, 2, 3))

    dmod = jnp.concatenate([dsh1, dsc1, dg1, dsh2, dsc2, dg2], axis=1)
    small = ['ada_b', 'norm_mix_g', 'attn_sinks', 'ssm_a_re', 'ssm_a_im', 'ssm_log_dt', 'ssm_b_re', 'ssm_b_im',
             'ssm_c_re', 'ssm_c_im', 'ssm_d', 'norm_ffn_g', 'ffn_conv_b', 'final_g']
    small_grads = {
        'ada_b': dmod, 'norm_mix_g': dg_mix, 'attn_sinks': dsink[:, 0, 0], 'ssm_a_re': da_re, 'ssm_a_im': da_im,
        'ssm_log_dt': dlog_dt, 'ssm_b_re': db_re.transpose(1, 2, 0), 'ssm_b_im': db_im.transpose(1, 2, 0),
        'ssm_c_re': dc_re, 'ssm_c_im': dc_im, 'ssm_d': dd_tiles, 'norm_ffn_g': dg_ffn, 'ffn_conv_b': dconv_b,
        'final_g': dg_fin}
    gs_pack, sm_offs = _pack([small_grads[k] for k in small], LANES, 8)
    (gs_all,) = _all_gather("gather_small_grads", [gs_pack])
    ws_pack, _ = _pack([given[k] for k in small], LANES, 8)
    ms_pack, _ = _pack([given['m_' + k] for k in small], LANES, 8)
    vs_pack, _ = _pack([given['v_' + k] for k in small], LANES, 8)
    small_out = _adamw("adamw_replicated", gs_all, ws_pack, ms_pack, vs_pack)

    dmod_all = _unpack(gs_all, sm_offs[0], (N_DEV * mod_n,), lead=(N_DEV,))
    dmod_mine = lax.dynamic_slice_in_dim(dmod_all, idx * mod_n, mod_n, axis=1)
    kpad = LANES - N_DEV
    cond_t = jnp.pad(cond_all.T, ((0, 0), (0, kpad)))
    dmod_pad = jnp.pad(dmod_mine, ((0, kpad), (0, 0)))
    g_ada_w = _matmul("dw_ada", cond_t, dmod_pad, "nn")
    ada_out = _adamw("adamw_ada_w", g_ada_w[None], ada_w[0], m_ada_w[0], v_ada_w[0])

    sharded = big + ['ffn_conv_w']
    place = jnp.reshape(idx, (1,)).astype(jnp.int32)
    sharded_out = {}
    for group, handle in pending:
        for k, parts in zip(group, _exchange_wait("grad_" + group[0] + "_wait", handle, ada_out[0])):
            sharded_out[k] = _adamw_sharded("adamw_" + k, parts, gd[k], given[k][0], given['m_' + k][0],
                                            given['v_' + k][0], place)

    results = [{}, {}, {}, {}]
    for which in range(4):
        for k, off in zip(small, sm_offs):
            results[which][k] = _unpack(small_out[which], off, given[k].shape)
        for k in sharded:
            results[which][k] = sharded_out[k][which][None]
        results[which]['ada_w'] = ada_out[which][None]
    outs = [loss, grad_x[None]]
    for which in range(4):
        outs += [results[which][k] for k in names]
    return tuple(outs)
```

```python
import functools
import math

import jax
import jax.numpy as jnp
from jax import lax
from jax.experimental import pallas as pl
from jax.experimental.pallas import tpu as pltpu

F32, BF16 = jnp.float32, jnp.bfloat16
MESH = pl.DeviceIdType.MESH
N_DEV = 8

HEAD_DIM = 64
N_KV_HEADS = 2
ATT_BLOCK = 128
NEG_INF = -1e30
SSM_P = 16
SSM_N = 64
LANES = 128
TILE_GROUPS = LANES // SSM_P
TILE_STATES = TILE_GROUPS * SSM_N
RMS_EPS = 1e-6
ADAM_LR, ADAM_B1, ADAM_B2, ADAM_EPS, ADAM_WD, ADAM_STEP = 0.001, 0.9, 0.999, 1e-08, 0.01, 10
VMEM_LIMIT = 56 * 1024 * 1024
MATMUL_VMEM_BUDGET = 40 * 1024 * 1024


def _params(n_axes):
    return pltpu.CompilerParams(dimension_semantics=("arbitrary",) * n_axes, vmem_limit_bytes=VMEM_LIMIT)


def _pick(dim, pref, align=128):
    if dim <= align:
        return dim
    t = (min(pref, dim) // align) * align
    while t > align and dim % t:
        t -= align
    assert dim % t == 0, (dim, pref, align)
    return t


def _dev():
    return lax.axis_index("x"), lax.axis_index("y"), lax.axis_index("c")


def _tile_call(name, fn, grid, ins, in_specs, out_shapes, out_specs, acc=()):
    n_in, n_out = len(ins), len(out_shapes)
    acc_axis = len(grid) - 1

    def body(*refs):
        vals = fn(*[r[...] for r in refs[:n_in]])
        if not isinstance(vals, (tuple, list)):
            vals = (vals,)
        assert len(vals) == n_out
        for i, (r, v) in enumerate(zip(refs[n_in:], vals)):
            v = v.astype(r.dtype)
            if i in acc:
                first = pl.program_id(acc_axis) == 0

                @pl.when(first)
                def _():
                    r[...] = v

                @pl.when(jnp.logical_not(first))
                def _():
                    r[...] += v
            else:
                r[...] = v

    return pl.pallas_call(
        body, grid=grid, in_specs=in_specs, out_specs=out_specs, out_shape=out_shapes, name=name,
        compiler_params=_params(len(grid)),
    )(*ins)


def _t(tr, tc, off=0):
    return pl.BlockSpec((tr, tc), lambda j, i: (i, j + off // tc))


def _tt(tr, tc):
    return pl.BlockSpec((tc, tr), lambda j, i: (j, i))


def _v(tc, off=0, rows=1):
    return pl.BlockSpec((rows, tc), lambda j, i: (0, j + off // tc))


def _prev8(tr, tc, off=0):
    return pl.BlockSpec((8, tc), lambda j, i: (jnp.maximum(i * (tr // 8) - 1, 0), j + off // tc))


def _next8(tr, tc, nrows, off=0):
    return pl.BlockSpec((8, tc), lambda j, i: (jnp.minimum((i + 1) * (tr // 8), nrows // 8 - 1), j + off // tc))


def _st(tr, tc):
    return pl.BlockSpec((2, tr, tc), lambda j, i: (0, i, j))


def _bc(v, rows):
    return jnp.broadcast_to(v, (rows, v.shape[-1]))


def _colsum(v):
    return jnp.sum(v, axis=0, keepdims=True)


def _matmul(name, a, b, mode, out_dtype=F32, tm=1024, tn=1024, tk=None, out_stack=None, also_bf16=False, dep=None):
    def dims(z):
        return (z.shape[-2], z.shape[-1] * (z.shape[0] if z.ndim == 3 else 1))

    ar, ac = dims(a)
    br, bc = dims(b)
    if mode == "nn":
        m, k, n = ar, ac, bc
        assert br == k
    elif mode == "nt":
        m, k, n = ar, ac, br
        assert bc == k
    else:
        m, k, n = ac, ar, bc
        assert br == k
    m_lim, k_lim, n_lim = [m], [k], [n]
    if a.ndim == 3:
        (m_lim if mode == "tn" else k_lim).append(a.shape[-1])
    if b.ndim == 3:
        (k_lim if mode == "nt" else n_lim).append(b.shape[-1])
    if out_stack:
        n_lim.append(n // out_stack)
    tm = _pick(functools.reduce(math.gcd, m_lim), tm)
    tn = _pick(functools.reduce(math.gcd, n_lim), tn)
    k_unit = functools.reduce(math.gcd, k_lim)
    if tk is None:
        sa, sb, so = a.dtype.itemsize, b.dtype.itemsize, jnp.dtype(out_dtype).itemsize + (2 if also_bf16 else 0)
        fits = [t for t in range(LANES, k_unit + 1, LANES) if k_unit % t == 0 and
                2 * t * (tm * sa + tn * sb) + tm * tn * (2 * so + (4 if t < k else 0)) <= MATMUL_VMEM_BUDGET]
        tk = max(fits) if fits else _pick(k_unit, 512)
    else:
        tk = _pick(k_unit, tk)
    nk = k // tk

    def spec(z, brows, bcols, ridx, cidx):
        if z.ndim == 3:
            per = z.shape[-1] // bcols
            return pl.BlockSpec((None, brows, bcols),
                                lambda i, j, kk: (cidx(i, j, kk) // per, ridx(i, j, kk), cidx(i, j, kk) % per))
        return pl.BlockSpec((brows, bcols), lambda i, j, kk: (ridx(i, j, kk), cidx(i, j, kk)))

    gi = lambda i, j, kk: i
    gj = lambda i, j, kk: j
    gk = lambda i, j, kk: kk
    if mode == "nn":
        a_spec, b_spec = spec(a, tm, tk, gi, gk), spec(b, tk, tn, gk, gj)
        dn = (((1,), (0,)), ((), ()))
    elif mode == "nt":
        a_spec, b_spec = spec(a, tm, tk, gi, gk), spec(b, tn, tk, gj, gk)
        dn = (((1,), (1,)), ((), ()))
    else:
        a_spec, b_spec = spec(a, tk, tm, gk, gi), spec(b, tk, tn, gk, gj)
        dn = (((0,), (0,)), ((), ()))

    n_out = 2 if also_bf16 else 1

    deps = [] if dep is None else [dep]

    def body(a_ref, b_ref, *rest):
        rest = rest[len(deps):]
        o_refs, acc = rest[:n_out], rest[n_out:]
        part = lax.dot_general(a_ref[...].astype(BF16), b_ref[...].astype(BF16), dn, preferred_element_type=F32)

        def emit(val):
            for o_ref in o_refs:
                o_ref[...] = val.astype(o_ref.dtype)

        if nk == 1:
            emit(part)
            return
        acc_ref, = acc
        kk = pl.program_id(2)

        @pl.when(kk == 0)
        def _():
            acc_ref[...] = part

        @pl.when(kk > 0)
        def _():
            acc_ref[...] += part

        @pl.when(kk == nk - 1)
        def _():
            emit(acc_ref[...])

    if out_stack:
        per = (n // out_stack) // tn
        out_spec = pl.BlockSpec((None, tm, tn), lambda i, j, kk: (j // per, i, j % per))
        shape = (out_stack, m, n // out_stack)
    else:
        out_spec = pl.BlockSpec((tm, tn), lambda i, j, kk: (i, j))
        shape = (m, n)
    dtypes = [out_dtype, BF16][:n_out]
    res = pl.pallas_call(
        body, grid=(m // tm, n // tn, nk),
        in_specs=[a_spec, b_spec] + [pl.BlockSpec(memory_space=pl.ANY)] * len(deps), out_specs=[out_spec] * n_out,
        out_shape=[jax.ShapeDtypeStruct(shape, dt) for dt in dtypes],
        scratch_shapes=[pltpu.VMEM((tm, tn), F32)] if nk > 1 else [], name=name, compiler_params=_params(3),
    )(a, b, *deps)
    return res if also_bf16 else res[0]


def _all_gather(name, arrs):
    n = len(arrs)

    def body(*refs):
        ins, outs = refs[:n], refs[n:2 * n]
        send_sems, recv_sems, local_sems = refs[2 * n:]
        x, y, c = _dev()
        me, sib = (x, y, c), (x, y, 1 - c)
        chips = [(1 - x, y), (x, 1 - y), (1 - x, 1 - y)]

        def slot(p):
            return 4 * p[0] + 2 * p[1] + p[2]

        def copy(a, k, block, to, src=None):
            dst = outs[a].at[slot(block)]
            return pltpu.make_async_remote_copy(
                src_ref=dst if src is None else src, dst_ref=dst,
                send_sem=send_sems.at[7 * a + k], recv_sem=recv_sems.at[7 * a + k],
                device_id=to, device_id_type=MESH)

        mine = [pltpu.make_async_copy(ins[a], outs[a].at[slot(me)], local_sems.at[a]) for a in range(n)]
        for cp in mine:
            cp.start()
        first = []
        for a in range(n):
            first.append(copy(a, 0, me, sib, src=ins[a]))
            first += [copy(a, 1 + j, me, (*chip, c), src=ins[a]) for j, chip in enumerate(chips)]
        for cp in first:
            cp.start()
        passed = []
        for j, chip in enumerate(chips):
            for a in range(n):
                copy(a, 1 + j, (*chip, c), me).wait_recv()
                cp = copy(a, 4 + j, (*chip, c), sib)
                cp.start()
                passed.append(cp)
        for a in range(n):
            copy(a, 0, sib, me).wait_recv()
            for j, chip in enumerate(chips):
                copy(a, 4 + j, (*chip, 1 - c), me).wait_recv()
        for cp in first + passed:
            cp.wait_send()
        for cp in mine:
            cp.wait()

    any_spec = pl.BlockSpec(memory_space=pl.ANY)
    return pl.pallas_call(
        body, in_specs=[any_spec] * n, out_specs=[any_spec] * n,
        out_shape=[jax.ShapeDtypeStruct((N_DEV,) + a.shape, a.dtype) for a in arrs],
        scratch_shapes=[pltpu.SemaphoreType.DMA((7 * n,)), pltpu.SemaphoreType.DMA((7 * n,)),
                        pltpu.SemaphoreType.DMA((n,))],
        name=name,
    )(*arrs)


def _grad_to_sibling(gds):
    n = len(gds)

    def body(*refs):
        g_refs, r_refs = refs[:n], refs[n:2 * n]
        send_sems, recv_sems = refs[2 * n:]
        x, y, c = _dev()
        cps = []
        for a in range(n):
            for k in range(4):
                cp = pltpu.make_async_remote_copy(
                    src_ref=g_refs[a].at[2 * k + (1 - c)], dst_ref=r_refs[a].at[k],
                    send_sem=send_sems.at[4 * a + k], recv_sem=recv_sems.at[4 * a + k],
                    device_id=(x, y, 1 - c), device_id_type=MESH)
                cp.start()
                cps.append(cp)
        for cp in cps:
            cp.wait()

    any_spec = pl.BlockSpec(memory_space=pl.ANY)
    return pl.pallas_call(
        body, in_specs=[any_spec] * n, out_specs=[any_spec] * n,
        out_shape=[jax.ShapeDtypeStruct((4,) + g.shape[1:], g.dtype) for g in gds],
        scratch_shapes=[pltpu.SemaphoreType.DMA((4 * n,)), pltpu.SemaphoreType.DMA((4 * n,))],
        name="grad_to_sibling",
    )(*gds)


def _chip_sum(name, gd, from_sib, c_arr):
    _, k, n = gd.shape
    tr = _pick(k, max(16, (1 << 20) // (4 * n)), 16)

    def body(c_ref, a_ref, b_ref, o_ref):
        o_ref[...] = (a_ref[...] + b_ref[...]).astype(o_ref.dtype)

    return pl.pallas_call(
        body,
        grid_spec=pltpu.PrefetchScalarGridSpec(
            num_scalar_prefetch=1, grid=(4, k // tr),
            in_specs=[pl.BlockSpec((1, tr, n), lambda q, i, cr: (2 * q + cr[0], i, 0)),
                      pl.BlockSpec((1, tr, n), lambda q, i, cr: (q, i, 0))],
            out_specs=pl.BlockSpec((1, tr, n), lambda q, i, cr: (q, i, 0))),
        out_shape=jax.ShapeDtypeStruct((4, k, n), BF16), name=name, compiler_params=_params(2),
    )(c_arr, gd, from_sib)


def _grad_to_chips(sums):
    n = len(sums)

    def body(*refs):
        s_refs, p_refs = refs[:n], refs[n:2 * n]
        send_sems, recv_sems, local_sems = refs[2 * n:]
        x, y, c = _dev()
        my_chip = 2 * x + y
        cps = []
        for a in range(n):
            local = pltpu.make_async_copy(s_refs[a].at[my_chip], p_refs[a].at[my_chip], local_sems.at[a])
            local.start()
            cps.append(local)
            for j, (px, py) in enumerate([(1 - x, y), (x, 1 - y), (1 - x, 1 - y)]):
                cp = pltpu.make_async_remote_copy(
                    src_ref=s_refs[a].at[2 * px + py], dst_ref=p_refs[a].at[my_chip],
                    send_sem=send_sems.at[3 * a + j], recv_sem=recv_sems.at[3 * a + j],
                    device_id=(px, py, c), device_id_type=MESH)
                cp.start()
                cps.append(cp)
        for cp in cps:
            cp.wait()

    any_spec = pl.BlockSpec(memory_space=pl.ANY)
    return pl.pallas_call(
        body, in_specs=[any_spec] * n, out_specs=[any_spec] * n,
        out_shape=[jax.ShapeDtypeStruct(s.shape, s.dtype) for s in sums],
        scratch_shapes=[pltpu.SemaphoreType.DMA((3 * n,)), pltpu.SemaphoreType.DMA((3 * n,)),
                        pltpu.SemaphoreType.DMA((n,))],
        name="grad_to_chips",
    )(*sums)


FLIPS = [(0, 0, 1), (0, 1, 0), (1, 0, 0), (0, 1, 1), (1, 0, 1), (1, 1, 0), (1, 1, 1)]
N_PEERS = len(FLIPS)
_HBM = pl.BlockSpec(memory_space=pltpu.HBM)
_SEM = pl.BlockSpec(memory_space=pltpu.SEMAPHORE)
_EFFECT = pltpu.SideEffectType.DATAFLOW_SIDE_EFFECTING


def _flip(x, y, c, f):
    return (1 - x if f[0] else x, 1 - y if f[1] else y, 1 - c if f[2] else c)


def _slot(p):
    return 4 * p[0] + 2 * p[1] + p[2]


def _exchange_copies(src_refs, land_refs, send_sems, recv_sems, gather):
    x, y, c = _dev()
    mine = _slot((x, y, c))
    cps = []
    for a, (src, land) in enumerate(zip(src_refs, land_refs)):
        for k, f in enumerate(FLIPS):
            peer = _flip(x, y, c, f)
            cps.append(pltpu.make_async_remote_copy(
                src_ref=src if gather else src.at[_slot(peer)], dst_ref=land.at[mine],
                send_sem=send_sems.at[N_PEERS * a + k], recv_sem=recv_sems.at[N_PEERS * a + k],
                device_id=peer, device_id_type=MESH))
    return cps


def _exchange_start(name, srcs, gather, after):
    n = len(srcs)
    lands = [lax.empty(((N_DEV,) + s.shape) if gather else s.shape, s.dtype) for s in srcs]

    def body(*refs):
        src_refs, land_refs = refs[:n], refs[n:2 * n]
        send_sems, recv_sems, local_sems = refs[2 * n + 1:2 * n + 4]
        token = refs[-1]
        if gather:
            x, y, c = _dev()
            for a in range(n):
                pltpu.make_async_copy(src_refs[a], land_refs[a].at[_slot((x, y, c))], local_sems.at[a]).start()
        for cp in _exchange_copies(src_refs, land_refs, send_sems, recv_sems, gather):
            cp.start()
        token[...] = jnp.zeros_like(token)

    hbm = lambda z: pltpu.HBM(z.shape, z.dtype)
    outs = pl.pallas_call(
        body, name=name,
        out_shape=(pltpu.SemaphoreType.DMA((N_PEERS * n,)), pltpu.SemaphoreType.DMA((N_PEERS * n,)),
                   pltpu.SemaphoreType.DMA((n,)), *[hbm(s) for s in srcs], *[hbm(z) for z in lands],
                   jax.ShapeDtypeStruct((8, LANES), F32)),
        in_specs=[_HBM] * (2 * n) + [pl.BlockSpec(memory_space=pl.ANY)],
        out_specs=(_SEM, _SEM, _SEM, *[_HBM] * (2 * n), pl.BlockSpec(memory_space=pltpu.VMEM)),
        input_output_aliases={i: 3 + i for i in range(2 * n)},
        compiler_params=pltpu.CompilerParams(has_side_effects=_EFFECT),
    )(*[pltpu.with_memory_space_constraint(z, pltpu.HBM) for z in list(srcs) + lands], after)
    return (outs[:3], outs[3:3 + n], outs[3 + n:3 + 2 * n], gather), outs[-1]


def _exchange_wait(name, handles, after):
    sems, srcs, lands, gather = handles
    n = len(srcs)

    def body(*refs):
        src_refs, land_refs = refs[:n], refs[n:2 * n]
        send_sems, recv_sems, local_sems = refs[2 * n:2 * n + 3]
        if gather:
            for a in range(n):
                pltpu.make_async_copy(src_refs[a], land_refs[a].at[0], local_sems.at[a]).wait()
        for cp in _exchange_copies(src_refs, land_refs, send_sems, recv_sems, gather):
            cp.wait_send()
            cp.wait_recv()

    hbm = lambda z: pltpu.HBM(z.shape, z.dtype)
    outs = pl.pallas_call(
        body, name=name, out_shape=tuple(hbm(z) for z in list(srcs) + list(lands)),
        in_specs=[_HBM] * (2 * n) + [_SEM] * 3 + [pl.BlockSpec(memory_space=pl.ANY)],
        out_specs=tuple([_HBM] * (2 * n)), input_output_aliases={i: i for i in range(2 * n)},
        compiler_params=pltpu.CompilerParams(has_side_effects=_EFFECT),
    )(*srcs, *lands, *sems, after)
    return list(outs[n:])


def _pack_rows(sizes, width, row_align):
    offs, r = [], 0
    for s in sizes:
        offs.append(r)
        r += -(-s // width)
    total = -(-r // row_align) * row_align
    return offs, total


def _pack(items, width, row_align, lead=()):
    nl = len(lead)
    sizes = [int(jnp.size(a)) // max(1, functools.reduce(lambda p, q: p * q, lead, 1)) for a in items]
    offs, total = _pack_rows(sizes, width, row_align)
    flat = []
    used = 0
    for a, s in zip(items, sizes):
        f = a.reshape(lead + (s,))
        pad = -(-s // width) * width - s
        if pad:
            f = jnp.pad(f, [(0, 0)] * nl + [(0, pad)])
        flat.append(f)
        used += s + pad
    tail = total * width - used
    if tail:
        flat.append(jnp.zeros(lead + (tail,), items[0].dtype))
    return jnp.concatenate(flat, axis=-1).reshape(lead + (total, width)), offs


def _unpack(packed, off, shape, lead=()):
    nl = len(lead)
    size = functools.reduce(lambda p, q: p * q, shape, 1)
    width = packed.shape[-1]
    rows = -(-size // width)
    blk = lax.slice_in_dim(packed, off, off + rows, axis=nl).reshape(lead + (rows * width,))
    return lax.slice_in_dim(blk, 0, size, axis=nl).reshape(lead + tuple(shape))


def _rms(x, g):
    return (x * lax.rsqrt(jnp.mean(x * x, axis=-1, keepdims=True) + RMS_EPS)) * g


def _norm_mod(x, g, sc, sh):
    return _rms(x, g) * (1.0 + sc) + sh


def _mix_fn(glu_a, glu_b, attn, ga, gs):
    return jax.nn.sigmoid(ga) * attn + jax.nn.sigmoid(gs) * (glu_a * jax.nn.sigmoid(glu_b))


def _s5_disc_fn(a_re, a_im, log_dt, b_re, b_im):
    dt = jnp.exp(log_dt)
    mag = jnp.exp(a_re * dt)
    lr, li = mag * jnp.cos(a_im * dt), mag * jnp.sin(a_im * dt)
    den = a_re * a_re + a_im * a_im
    zr = ((lr - 1.0) * a_re + li * a_im) / den
    zi = (li * a_re - (lr - 1.0) * a_im) / den
    return lr, li, zr[None] * b_re - zi[None] * b_im, zr[None] * b_im + zi[None] * b_re


def _adamw_fn(w, g, m, v):
    m = ADAM_B1 * m + (1.0 - ADAM_B1) * g
    v = ADAM_B2 * v + (1.0 - ADAM_B2) * jnp.square(g)
    m_hat = m / (1.0 - ADAM_B1 ** ADAM_STEP)
    v_hat = v / (1.0 - ADAM_B2 ** ADAM_STEP)
    delta = -ADAM_LR * (m_hat / (jnp.sqrt(v_hat) + ADAM_EPS) + ADAM_WD * w)
    return delta, m, v


def _adamw(name, parts, w, m, v):
    p, r, c = parts.shape
    tr = _pick(r, max(8, (1 << 21) // (4 * c * max(p, 2))), 8)

    def fn(pv, wv, mv, vv):
        g = pv[0]
        for i in range(1, p):
            g = g + pv[i]
        d, m2, v2 = _adamw_fn(wv, g, mv, vv)
        return g, d, m2, v2

    spec = pl.BlockSpec((tr, c), lambda i: (i, 0))
    return _tile_call(
        name, fn, (r // tr,), [parts, w, m, v],
        [pl.BlockSpec((p, tr, c), lambda i: (0, i, 0)), spec, spec, spec],
        [jax.ShapeDtypeStruct((r, c), F32)] * 4, [spec] * 4)


def _adamw_sharded(name, parts, gd, w, m, v, place):
    _, k, n = parts.shape
    tr = _pick(k, max(16, (1 << 19) // (4 * n)), 16)

    def body(pl_ref, p_ref, a_ref, w_ref, m_ref, v_ref, g_ref, d_ref, m2_ref, v2_ref):
        own = a_ref[0]
        g = None
        for q in range(N_DEV):
            term = jnp.where(pl_ref[0] == q, own, p_ref[q].astype(F32))
            g = term if g is None else g + term
        d, m2, v2 = _adamw_fn(w_ref[...], g, m_ref[...], v_ref[...])
        g_ref[...] = g
        d_ref[...] = d
        m2_ref[...] = m2
        v2_ref[...] = v2

    spec = pl.BlockSpec((tr, n), lambda i, pr: (i, 0))
    return pl.pallas_call(
        body,
        grid_spec=pltpu.PrefetchScalarGridSpec(
            num_scalar_prefetch=1, grid=(k // tr,),
            in_specs=[pl.BlockSpec((N_DEV, tr, n), lambda i, pr: (0, i, 0)),
                      pl.BlockSpec((1, tr, n), lambda i, pr: (pr[0], i, 0)),
                      spec, spec, spec],
            out_specs=[spec] * 4),
        out_shape=[jax.ShapeDtypeStruct((k, n), F32)] * 4, name=name, compiler_params=_params(1),
    )(place, parts, gd, w, m, v)


def _attn_mask(n, rows):
    qi = lax.broadcasted_iota(jnp.int32, (rows, 2 * ATT_BLOCK), 0) & (ATT_BLOCK - 1)
    kj = lax.broadcasted_iota(jnp.int32, (rows, 2 * ATT_BLOCK), 1)
    rel = qi + ATT_BLOCK - kj
    return (rel >= 0) & (rel < ATT_BLOCK) & ((kj >= ATT_BLOCK) | (n > 0))


def _attn_probs(q, k, sink, mask):
    s = lax.dot_general(q, k, (((1,), (1,)), ((), ())), preferred_element_type=F32) * (HEAD_DIM ** -0.5)
    s = jnp.where(mask, s, NEG_INF)
    m = jnp.maximum(jnp.max(s, axis=-1, keepdims=True), sink)
    p = jnp.exp(s - m)
    e_sink = jnp.exp(sink - m)
    den = jnp.sum(p, axis=-1, keepdims=True) + e_sink
    return p / den, e_sink / den


def _attn_specs(qpk):
    blk = ATT_BLOCK
    q_spec = pl.BlockSpec((qpk, blk, HEAD_DIM), lambda h, n: (h, n, 0))
    cur = pl.BlockSpec((1, blk, HEAD_DIM), lambda h, n: (h, n, 0))
    prev = pl.BlockSpec((1, blk, HEAD_DIM), lambda h, n: (h, jnp.maximum(n - 1, 0), 0))
    sink_spec = pl.BlockSpec((1, qpk * blk, 1), lambda h, n: (h, 0, 0))
    return q_spec, cur, prev, sink_spec


def _attn_fwd(q, k, v, sinks):
    hq, l, _ = q.shape
    qpk = hq // N_KV_HEADS
    nb = l // ATT_BLOCK
    rows = qpk * ATT_BLOCK
    q_spec, cur, prev, sink_spec = _attn_specs(qpk)

    def body(q_ref, kp_ref, kc_ref, vp_ref, vc_ref, sink_ref, o_ref):
        mask = _attn_mask(pl.program_id(1), rows)
        kk = jnp.concatenate([kp_ref[0], kc_ref[0]], axis=0).astype(BF16)
        vv = jnp.concatenate([vp_ref[0], vc_ref[0]], axis=0).astype(BF16)
        p, _ = _attn_probs(q_ref[...].reshape(rows, HEAD_DIM).astype(BF16), kk, sink_ref[0], mask)
        o = jnp.dot(p.astype(BF16), vv, preferred_element_type=F32)
        o_ref[...] = o.reshape(qpk, ATT_BLOCK, HEAD_DIM).astype(o_ref.dtype)

    return pl.pallas_call(
        body, grid=(N_KV_HEADS, nb), in_specs=[q_spec, prev, cur, prev, cur, sink_spec],
        out_specs=q_spec, out_shape=jax.ShapeDtypeStruct((hq, l, HEAD_DIM), BF16),
        name="attn_fwd", compiler_params=_params(2),
    )(q, k, k, v, v, sinks)


def _attn_bwd(q, k, v, sinks, do):
    hq, l, _ = q.shape
    qpk = hq // N_KV_HEADS
    nb = l // ATT_BLOCK
    blk = ATT_BLOCK
    rows = qpk * blk
    q_spec, cur, prev, sink_spec = _attn_specs(qpk)
    part_spec = pl.BlockSpec((1, 1, 2 * blk, HEAD_DIM), lambda h, n: (h, n, 0, 0))
    dsink_spec = pl.BlockSpec((qpk, 1, LANES), lambda h, n: (h, 0, 0))
    tn = (((0,), (0,)), ((), ()))

    def body(q_ref, do_ref, kp_ref, kc_ref, vp_ref, vc_ref, sink_ref, dq_ref, dkp_ref, dvp_ref, dsink_ref):
        n = pl.program_id(1)
        mask = _attn_mask(n, rows)
        kk = jnp.concatenate([kp_ref[0], kc_ref[0]], axis=0).astype(BF16)
        vv = jnp.concatenate([vp_ref[0], vc_ref[0]], axis=0).astype(BF16)
        qb = q_ref[...].reshape(rows, HEAD_DIM).astype(BF16)
        do32 = do_ref[...].astype(F32).reshape(rows, HEAD_DIM)
        dob = do32.astype(BF16)
        p, p_sink = _attn_probs(qb, kk, sink_ref[0], mask)
        pb = p.astype(BF16)
        o = jnp.dot(pb, vv, preferred_element_type=F32)
        delta = jnp.sum(do32 * o, axis=-1, keepdims=True)
        dp = lax.dot_general(dob, vv, (((1,), (1,)), ((), ())), preferred_element_type=F32)
        ds = (p * (dp - delta) * (HEAD_DIM ** -0.5)).astype(BF16)
        dq = jnp.dot(ds, kk, preferred_element_type=F32)
        dq_ref[...] = dq.reshape(qpk, blk, HEAD_DIM).astype(dq_ref.dtype)
        dkp_ref[0, 0] = lax.dot_general(ds, qb, tn, preferred_element_type=F32)
        dvp_ref[0, 0] = lax.dot_general(pb, dob, tn, preferred_element_type=F32)
        dsr = p_sink * delta
        for g in range(qpk):
            dsg = jnp.broadcast_to(-_colsum(dsr[g * blk:(g + 1) * blk]), (1, LANES))

            @pl.when(n == 0)
            def _():
                dsink_ref[g] = dsg

            @pl.when(n > 0)
            def _():
                dsink_ref[g] += dsg


    part_shape = jax.ShapeDtypeStruct((N_KV_HEADS, nb, 2 * blk, HEAD_DIM), F32)
    dq, dkp, dvp, dsink = pl.pallas_call(
        body, grid=(N_KV_HEADS, nb), in_specs=[q_spec, q_spec, prev, cur, prev, cur, sink_spec],
        out_specs=[q_spec, part_spec, part_spec, dsink_spec],
        out_shape=[jax.ShapeDtypeStruct((hq, l, HEAD_DIM), BF16), part_shape, part_shape,
                   jax.ShapeDtypeStruct((hq, 1, LANES), F32)],
        name="attn_bwd", compiler_params=_params(2),
    )(q, do, k, k, v, v, sinks)

    def combine(a_cur, a_nxt, b_cur, b_nxt):
        last = pl.program_id(1) == nb - 1
        keep = jnp.where(last, 0.0, 1.0)
        return (a_cur[0, 0, blk:] + keep * a_nxt[0, 0, :blk])[None], (b_cur[0, 0, blk:] + keep * b_nxt[0, 0, :blk])[None]

    nxt_spec = pl.BlockSpec((1, 1, 2 * blk, HEAD_DIM), lambda h, n: (h, jnp.minimum(n + 1, nb - 1), 0, 0))
    kv_shape = jax.ShapeDtypeStruct((N_KV_HEADS, l, HEAD_DIM), BF16)
    dk, dv = _tile_call("attn_dkv", combine, (N_KV_HEADS, nb), [dkp, dkp, dvp, dvp],
                        [part_spec, nxt_spec, part_spec, nxt_spec], [kv_shape, kv_shape], [cur, cur])
    return dq, dk, dv, dsink


def _block_diag(m):
    j, gl, a, b = m.shape
    eye = jnp.eye(gl, dtype=m.dtype)
    return (m[:, :, :, None, :] * eye[None, :, None, :, None]).reshape(j, gl * a, gl * b)


def _diag_blocks(z, a):
    j = z.shape[0]
    gl = z.shape[1] // a
    b = z.shape[2] // gl
    d = jnp.diagonal(z.reshape(j, gl, a, gl, b), axis1=1, axis2=3)
    return d.transpose(0, 3, 1, 2)


def _s5_permute(src_ref, dst_ref, t_len):
    seg = t_len // 8
    for k in range(seg):
        dst_ref[8 * k:8 * k + 8, :] = src_ref[pl.ds(k, 8, stride=seg), :]


def _s5_unpermute(perm_ref, t_len, emit):
    per_seg = t_len // 64
    for m in range(t_len // 8):
        emit(8 * m, perm_ref[pl.ds(64 * (m % per_seg) + m // per_seg, 8, stride=8), :])


def _s5_powers(p_ref, lr, li, seg):
    hs = TILE_STATES

    def step(k, carry):
        pr, pi = carry
        p_ref[pl.ds(k, 1), 0:hs] = pr
        p_ref[pl.ds(k, 1), hs:2 * hs] = pi
        return lr * pr - li * pi, lr * pi + li * pr

    lax.fori_loop(0, seg, step, (lr, li))


def _s5_local_scan(x_ref, base, lr, li, seg, reverse):
    hs = TILE_STATES
    lr8, li8 = jnp.broadcast_to(lr, (8, hs)), jnp.broadcast_to(li, (8, hs))
    if reverse:
        li8 = -li8

    def step(i, carry):
        hr, hi = carry
        k = seg - 1 - i if reverse else i
        rows = pl.ds(pl.multiple_of(base + 8 * k, 8), 8)
        nr = lr8 * hr - li8 * hi + x_ref[rows, 0:hs]
        ni = lr8 * hi + li8 * hr + x_ref[rows, hs:2 * hs]
        x_ref[rows, 0:hs] = nr
        x_ref[rows, hs:2 * hs] = ni
        return nr, ni

    zero = jnp.zeros((8, hs), F32)
    return lax.fori_loop(0, seg, step, (zero, zero), unroll=2)


def _s5_carries(c_ref, e_ref, ends, start, pw_r, pw_i, reverse):
    hs = TILE_STATES
    e_ref[:, 0:hs] = ends[0]
    e_ref[:, hs:2 * hs] = ends[1]
    cr, ci = start
    if reverse:
        pw_i = -pw_i
    for s in (range(7, -1, -1) if reverse else range(8)):
        c_ref[s:s + 1, 0:hs] = cr
        c_ref[s:s + 1, hs:2 * hs] = ci
        er, ei = e_ref[s:s + 1, 0:hs], e_ref[s:s + 1, hs:2 * hs]
        cr, ci = er + pw_r * cr - pw_i * ci, ei + pw_r * ci + pw_i * cr
    return cr, ci


def _s5_states(u_perm_b16, bd_ref, x_ref, base, c_ref, e_ref, p_ref, lr, li, h_in, t_len):
    hs = TILE_STATES
    seg = t_len // 8
    x_ref[pl.ds(base, t_len), :] = jnp.dot(u_perm_b16, bd_ref[0], preferred_element_type=F32)
    ends = _s5_local_scan(x_ref, base, lr, li, seg, False)
    pw_r, pw_i = p_ref[seg - 1:seg, 0:hs], p_ref[seg - 1:seg, hs:2 * hs]
    h_out = _s5_carries(c_ref, e_ref, ends, h_in, pw_r, pw_i, False)
    cr, ci = c_ref[:, 0:hs], c_ref[:, hs:2 * hs]

    def fix(k, carry):
        rows = pl.ds(pl.multiple_of(base + 8 * k, 8), 8)
        pr, pi = p_ref[pl.ds(k, 1), 0:hs], p_ref[pl.ds(k, 1), hs:2 * hs]
        x_ref[rows, 0:hs] += pr * cr - pi * ci
        x_ref[rows, hs:2 * hs] += pr * ci + pi * cr
        return carry

    lax.fori_loop(0, seg, fix, 0, unroll=2)
    return h_out


def _s5_fwd(proj, u_off, bd, cbd, lam, dvec, t_len):
    l = proj.shape[0]
    nj = bd.shape[0]
    nch = l // t_len
    hs = TILE_STATES
    ub = u_off // LANES
    seg = t_len // 8
    assert t_len % 64 == 0

    def body(u_ref, bd_ref, cbd_ref, lam_ref, d_ref, y_ref, hst_ref, x_ref, h_ref, p_ref, c_ref, e_ref, up_ref, yp_ref):
        lr, li = lam_ref[0, 0:1, :], lam_ref[0, 1:2, :]

        @pl.when(pl.program_id(1) == 0)
        def _():
            h_ref[...] = jnp.zeros_like(h_ref)
            _s5_powers(p_ref, lr, li, seg)

        hst_ref[0, 0] = h_ref[...]
        _s5_permute(u_ref, up_ref, t_len)
        h_out = _s5_states(up_ref[...].astype(BF16), bd_ref, x_ref, 0, c_ref, e_ref, p_ref, lr, li,
                           (h_ref[:, 0:hs], h_ref[:, hs:2 * hs]), t_len)
        h_ref[:, 0:hs] = h_out[0]
        h_ref[:, hs:2 * hs] = h_out[1]
        yp_ref[...] = jnp.dot(x_ref[...].astype(BF16), cbd_ref[0], preferred_element_type=F32)
        dv = d_ref[0]

        def out(r0, rows):
            y_ref[r0:r0 + 8, :] = rows + dv * u_ref[r0:r0 + 8, :]

        _s5_unpermute(yp_ref, t_len, out)

    return pl.pallas_call(
        body, grid=(nj, nch),
        in_specs=[pl.BlockSpec((t_len, LANES), lambda j, c: (c, ub + j)),
                  pl.BlockSpec((1, LANES, 2 * hs), lambda j, c: (j, 0, 0)),
                  pl.BlockSpec((1, 2 * hs, LANES), lambda j, c: (j, 0, 0)),
                  pl.BlockSpec((1, 2, hs), lambda j, c: (j, 0, 0)),
                  pl.BlockSpec((1, 1, LANES), lambda j, c: (j, 0, 0))],
        out_specs=[pl.BlockSpec((t_len, LANES), lambda j, c: (c, j)),
                   pl.BlockSpec((1, 1, 1, 2 * hs), lambda j, c: (j, c, 0, 0))],
        out_shape=[jax.ShapeDtypeStruct((l, nj * LANES), F32),
                   jax.ShapeDtypeStruct((nj, nch, 1, 2 * hs), F32)],
        scratch_shapes=[pltpu.VMEM((t_len, 2 * hs), F32), pltpu.VMEM((1, 2 * hs), F32),
                        pltpu.VMEM((seg, 2 * hs), F32), pltpu.VMEM((8, 2 * hs), F32), pltpu.VMEM((8, 2 * hs), F32),
                        pltpu.VMEM((t_len, LANES), F32), pltpu.VMEM((t_len, LANES), F32)],
        name="s5_fwd", compiler_params=_params(2),
    )(proj, bd, cbd, lam, dvec)


def _s5_bwd(proj, u_off, dy, hst, bd, bdt, cbdt, lam, dvec, t_len):
    l = proj.shape[0]
    nj = bd.shape[0]
    nch = l // t_len
    hs = TILE_STATES
    ub = u_off // LANES
    seg = t_len // 8
    tn = (((0,), (0,)), ((), ()))
    assert t_len % 64 == 0

    def body(u_ref, dy_ref, hst_ref, bd_ref, bdt_ref, cbdt_ref, lam_ref, d_ref,
             du_ref, dbd_ref, dcbdt_ref, dlam_ref, dd_ref,
             x_ref, g_ref, gc_ref, p_ref, c_ref, e_ref, up_ref, dyp_ref, dup_ref):
        first = pl.program_id(1) == 0
        lr, li = lam_ref[0, 0:1, :], lam_ref[0, 1:2, :]

        @pl.when(first)
        def _():
            gc_ref[...] = jnp.zeros_like(gc_ref)
            _s5_powers(p_ref, lr, li, seg)

        _s5_permute(u_ref, up_ref, t_len)
        _s5_permute(dy_ref, dyp_ref, t_len)
        ub16, dyb16 = up_ref[...].astype(BF16), dyp_ref[...].astype(BF16)
        h0 = hst_ref[0, 0]
        _s5_states(ub16, bd_ref, x_ref, 8, c_ref, e_ref, p_ref, lr, li, (h0[:, 0:hs], h0[:, hs:2 * hs]), t_len)
        x_ref[0:8, :] = c_ref[...]
        g_ref[...] = jnp.dot(dyb16, cbdt_ref[0], preferred_element_type=F32)
        starts = _s5_local_scan(g_ref, 0, lr, li, seg, True)
        pw_r, pw_i = p_ref[seg - 1:seg, 0:hs], p_ref[seg - 1:seg, hs:2 * hs]
        g_out = _s5_carries(c_ref, e_ref, starts, (gc_ref[:, 0:hs], gc_ref[:, hs:2 * hs]), pw_r, pw_i, True)
        gc_ref[:, 0:hs] = g_out[0]
        gc_ref[:, hs:2 * hs] = g_out[1]
        cr, ci = c_ref[:, 0:hs], c_ref[:, hs:2 * hs]

        def fix(k, carry):
            alr, ali = carry
            rows = pl.ds(pl.multiple_of(8 * k, 8), 8)
            pr, pi = p_ref[pl.ds(seg - 1 - k, 1), 0:hs], p_ref[pl.ds(seg - 1 - k, 1), hs:2 * hs]
            gr = g_ref[rows, 0:hs] + pr * cr + pi * ci
            gi = g_ref[rows, hs:2 * hs] + pr * ci - pi * cr
            g_ref[rows, 0:hs] = gr
            g_ref[rows, hs:2 * hs] = gi
            hpr, hpi = x_ref[rows, 0:hs], x_ref[rows, hs:2 * hs]
            return alr + gr * hpr + gi * hpi, ali + gi * hpr - gr * hpi

        zero = jnp.zeros((8, hs), F32)
        alr, ali = lax.fori_loop(0, seg, fix, (zero, zero), unroll=2)
        alr, ali = _colsum(alr), _colsum(ali)
        g = g_ref[...].astype(BF16)
        h = x_ref[pl.ds(8, t_len), :].astype(BF16)
        dup_ref[...] = jnp.dot(g, bdt_ref[0], preferred_element_type=F32)
        dv = d_ref[0]

        def out(r0, rows):
            du_ref[r0:r0 + 8, :] = (rows + dv * dy_ref[r0:r0 + 8, :]).astype(du_ref.dtype)

        _s5_unpermute(dup_ref, t_len, out)
        sign = jnp.where(lax.broadcasted_iota(jnp.int32, (1, 2 * hs), 1) < hs, 1.0, -1.0)
        dbd = lax.dot_general(ub16, g, tn, preferred_element_type=F32)
        dcbdt = lax.dot_general(dyb16, h, tn, preferred_element_type=F32) * sign
        ddv = _colsum(dy_ref[...] * u_ref[...])

        @pl.when(first)
        def _():
            dbd_ref[0] = dbd
            dcbdt_ref[0] = dcbdt
            dlam_ref[0, 0:1, :] = alr
            dlam_ref[0, 1:2, :] = ali
            dd_ref[0] = ddv

        @pl.when(jnp.logical_not(first))
        def _():
            dbd_ref[0] += dbd
            dcbdt_ref[0] += dcbdt
            dlam_ref[0, 0:1, :] += alr
            dlam_ref[0, 1:2, :] += ali
            dd_ref[0] += ddv

    rev = lambda c: nch - 1 - c
    wide = pl.BlockSpec((1, LANES, 2 * hs), lambda j, c: (j, 0, 0))
    tall = pl.BlockSpec((1, 2 * hs, LANES), lambda j, c: (j, 0, 0))
    return pl.pallas_call(
        body, grid=(nj, nch),
        in_specs=[pl.BlockSpec((t_len, LANES), lambda j, c: (rev(c), ub + j)),
                  pl.BlockSpec((t_len, LANES), lambda j, c: (rev(c), j)),
                  pl.BlockSpec((1, 1, 1, 2 * hs), lambda j, c: (j, rev(c), 0, 0)),
                  wide, tall, wide,
                  pl.BlockSpec((1, 2, hs), lambda j, c: (j, 0, 0)),
                  pl.BlockSpec((1, 1, LANES), lambda j, c: (j, 0, 0))],
        out_specs=[pl.BlockSpec((t_len, LANES), lambda j, c: (rev(c), j)),
                   wide, wide,
                   pl.BlockSpec((1, 2, hs), lambda j, c: (j, 0, 0)),
                   pl.BlockSpec((1, 1, LANES), lambda j, c: (j, 0, 0))],
        out_shape=[jax.ShapeDtypeStruct((l, nj * LANES), BF16),
                   jax.ShapeDtypeStruct((nj, LANES, 2 * hs), F32),
                   jax.ShapeDtypeStruct((nj, LANES, 2 * hs), F32),
                   jax.ShapeDtypeStruct((nj, 2, hs), F32),
                   jax.ShapeDtypeStruct((nj, 1, LANES), F32)],
        scratch_shapes=[pltpu.VMEM((t_len + 8, 2 * hs), F32), pltpu.VMEM((t_len, 2 * hs), F32),
                        pltpu.VMEM((1, 2 * hs), F32), pltpu.VMEM((seg, 2 * hs), F32),
                        pltpu.VMEM((8, 2 * hs), F32), pltpu.VMEM((8, 2 * hs), F32),
                        pltpu.VMEM((t_len, LANES), F32), pltpu.VMEM((t_len, LANES), F32),
                        pltpu.VMEM((t_len, LANES), F32)],
        name="s5_bwd", compiler_params=_params(2),
    )(proj, dy, hst, bd, bdt, cbdt, lam, dvec)


def _full_spec(shape):
    nd = len(shape)
    return pl.BlockSpec(tuple(shape), lambda i: (0,) * nd)


def _sds(shape, dtype=F32):
    return jax.ShapeDtypeStruct(tuple(shape), dtype)


def kernel(x, c, ada_w, ada_b, norm_mix_g, w_in, attn_sinks, w_attn_proj, ssm_a_re, ssm_a_im, ssm_log_dt, ssm_b_re, ssm_b_im, ssm_c_re, ssm_c_im, ssm_d, w_ssm_glu, w_out, norm_ffn_g, w_ffn_up, ffn_conv_w, ffn_conv_b, w_ffn_down, final_g, loss_target, m_ada_w, m_ada_b, m_norm_mix_g, m_w_in, m_attn_sinks, m_w_attn_proj, m_ssm_a_re, m_ssm_a_im, m_ssm_log_dt, m_ssm_b_re, m_ssm_b_im, m_ssm_c_re, m_ssm_c_im, m_ssm_d, m_w_ssm_glu, m_w_out, m_norm_ffn_g, m_w_ffn_up, m_ffn_conv_w, m_ffn_conv_b, m_w_ffn_down, m_final_g, v_ada_w, v_ada_b, v_norm_mix_g, v_w_in, v_attn_sinks, v_w_attn_proj, v_ssm_a_re, v_ssm_a_im, v_ssm_log_dt, v_ssm_b_re, v_ssm_b_im, v_ssm_c_re, v_ssm_c_im, v_ssm_d, v_w_ssm_glu, v_w_out, v_norm_ffn_g, v_w_ffn_up, v_ffn_conv_w, v_ffn_conv_b, v_w_ffn_down, v_final_g):
    given = dict(locals())
    names = ['ada_w', 'ada_b', 'norm_mix_g', 'w_in', 'attn_sinks', 'w_attn_proj', 'ssm_a_re', 'ssm_a_im',
             'ssm_log_dt', 'ssm_b_re', 'ssm_b_im', 'ssm_c_re', 'ssm_c_im', 'ssm_d', 'w_ssm_glu', 'w_out',
             'norm_ffn_g', 'w_ffn_up', 'ffn_conv_w', 'ffn_conv_b', 'w_ffn_down', 'final_g']

    xs = x[0]
    tgt = loss_target[0]
    l, d = xs.shape
    attn_w = w_attn_proj.shape[1]
    ssm_w = w_ssm_glu.shape[1]
    hq = attn_sinks.shape[1]
    qpk = hq // N_KV_HEADS
    kv_w = N_KV_HEADS * HEAD_DIM
    n_groups = ssm_a_re.shape[1]
    dff = ffn_conv_b.shape[1]
    in_w = attn_w + 2 * kv_w + ssm_w + 2 * d
    nj = ssm_w // LANES
    off_k, off_v, off_u = attn_w, attn_w + kv_w, attn_w + 2 * kv_w
    off_ga, off_gs = off_u + ssm_w, off_u + ssm_w + d
    assert hq * HEAD_DIM == attn_w and n_groups * SSM_P == ssm_w and l % ATT_BLOCK == 0

    xi, yi, ci = _dev()
    idx = 4 * xi + 2 * yi + ci

    row_sharded = {'w_out': (d, d), 'w_ffn_down': (dff, d)}
    big = ['w_in', 'w_attn_proj', 'w_ssm_glu', 'w_out', 'w_ffn_up', 'w_ffn_down']
    spack, s_offs = _pack([c, ffn_conv_w[0]], LANES, 8)
    w16 = {k: given[k][0].astype(BF16) for k in big}
    wg_in, sg = _all_gather("gather_first", [w16['w_in'], spack])
    mixer_w = ['w_attn_proj', 'w_ssm_glu', 'w_out']
    h_mixer, tok = _exchange_start("gather_mixer_start", [w16[k] for k in mixer_w], True, wg_in)
    h_up, tok = _exchange_start("gather_ffn_up_start", [w16['w_ffn_up']], True, tok)
    h_down, tok = _exchange_start("gather_ffn_down_start", [w16['w_ffn_down']], True, tok)
    full = {'w_in': wg_in.transpose(1, 0, 2).reshape(d, in_w)}
    c_all = _unpack(sg, s_offs[0], (d,), lead=(N_DEV,))
    conv_w = _unpack(sg, s_offs[1], ffn_conv_w.shape[1:], lead=(N_DEV,)).transpose(1, 0, 2).reshape(3, dff)
    conv_b = ffn_conv_b

    mod_n = ada_w.shape[2]
    tcm = _pick(mod_n, 512)
    ada_b_mine = lax.dynamic_slice_in_dim(ada_b, idx * mod_n, mod_n, axis=1)

    def modpart_fn(cv, wv, bv):
        cond = cv * jax.nn.sigmoid(cv)
        return jnp.dot(cond.astype(BF16), wv.astype(BF16), preferred_element_type=F32) + bv, cond

    modp, cond_all = _tile_call(
        "ada_rows", modpart_fn, (mod_n // tcm,), [c_all, ada_w[0], ada_b_mine],
        [pl.BlockSpec((N_DEV, d), lambda j: (0, 0)), pl.BlockSpec((d, tcm), lambda j: (0, j)),
         pl.BlockSpec((1, tcm), lambda j: (0, j))],
        [_sds((N_DEV, mod_n)), _sds((N_DEV, d))],
        [pl.BlockSpec((N_DEV, tcm), lambda j: (0, j)), pl.BlockSpec((N_DEV, d), lambda j: (0, 0))])
    (modg,) = _all_gather("gather_ada_rows", [modp])
    mod = lax.dynamic_index_in_dim(modg, idx, axis=1, keepdims=False).reshape(1, N_DEV * mod_n)
    sh1, sc1, g1, sh2, sc2, g2 = [mod[:, i * d:(i + 1) * d] for i in range(6)]

    tr = _pick(l, 256, 8)
    trh = _pick(l, 128, 8)
    nr, nrh = l // tr, l // trh
    g_mix, g_ffn, g_fin = norm_mix_g + tok[0:1, 0:1], norm_ffn_g, final_g.reshape(1, d)

    def with_t(fn):
        def wrapped(*vals):
            out = fn(*vals)
            out = out if isinstance(out, tuple) else (out,)
            return out + (out[-1].T,)
        return wrapped

    h1, h1_t = _tile_call("norm_mod_mix", with_t(_norm_mod), (1, nr), [xs, g_mix, sc1, sh1],
                          [_t(tr, d), _v(d), _v(d), _v(d)], [_sds((l, d), BF16), _sds((d, l), BF16)],
                          [_t(tr, d), _tt(tr, d)])
    proj = _matmul("proj_in", h1, full['w_in'], "nn", tn=1280)

    def heads(z, n):
        return z.reshape(l, n, HEAD_DIM).transpose(1, 0, 2)

    qh = heads(proj[:, :attn_w], hq)
    kh = heads(proj[:, off_k:off_k + kv_w], N_KV_HEADS)
    vh = heads(proj[:, off_v:off_v + kv_w], N_KV_HEADS)
    sinks3 = jnp.repeat(attn_sinks.reshape(N_KV_HEADS, qpk), ATT_BLOCK, axis=1)[..., None]
    o_h = _attn_fwd(qh, kh, vh, sinks3)
    o2 = o_h.transpose(1, 0, 2).reshape(l, attn_w)

    gn = (n_groups, SSM_N)
    pgn = (SSM_P, n_groups, SSM_N)
    a_re, a_im, log_dt = ssm_a_re[0], ssm_a_im[0], ssm_log_dt[0].reshape(n_groups, 1)
    b_re, b_im = ssm_b_re[0].transpose(2, 0, 1), ssm_b_im[0].transpose(2, 0, 1)
    disc_ins = [a_re, a_im, log_dt, b_re, b_im]
    disc_specs = [_full_spec(gn), _full_spec(gn), _full_spec((n_groups, 1)), _full_spec(pgn), _full_spec(pgn)]
    lam_r, lam_i, bb_r, bb_i = _tile_call(
        "s5_discretise", _s5_disc_fn, (1,), disc_ins, disc_specs,
        [_sds(gn), _sds(gn), _sds(pgn), _sds(pgn)],
        [_full_spec(gn), _full_spec(gn), _full_spec(pgn), _full_spec(pgn)])

    def tiles_gpn(z):
        return z.reshape(SSM_P, nj, TILE_GROUPS, SSM_N).transpose(1, 2, 0, 3)

    bd = jnp.concatenate([_block_diag(tiles_gpn(bb_r)), _block_diag(tiles_gpn(bb_i))], axis=2).astype(BF16)
    c_r = ssm_c_re[0].reshape(nj, TILE_GROUPS, SSM_P, SSM_N).transpose(0, 1, 3, 2)
    c_i = (-ssm_c_im[0]).reshape(nj, TILE_GROUPS, SSM_P, SSM_N).transpose(0, 1, 3, 2)
    cbd = jnp.concatenate([_block_diag(c_r), _block_diag(c_i)], axis=1).astype(BF16)
    bdt, cbdt = bd.transpose(0, 2, 1), cbd.transpose(0, 2, 1)
    lam = jnp.stack([lam_r.reshape(nj, TILE_STATES), lam_i.reshape(nj, TILE_STATES)], axis=1)
    dvec = ssm_d[0].reshape(nj, 1, LANES)
    t_len = _pick(l, 512, 8)
    y, hst = _s5_fwd(proj, off_u, bd, cbd, lam, dvec, t_len)

    tcs, trg = _pick(ssm_w, 1024), _pick(l, 512, 8)
    gy = _tile_call("gelu", lambda v: jax.nn.gelu(v), (ssm_w // tcs, l // trg), [y], [_t(trg, tcs)],
                    [_sds((l, ssm_w), BF16)], [_t(trg, tcs)])[0]
    full.update(zip(mixer_w, _exchange_wait("gather_mixer_wait", h_mixer, gy)))
    full['w_out'] = full['w_out'].reshape(row_sharded['w_out'])
    attn = _matmul("attn_proj", o2, full['w_attn_proj'], "nn")
    glu = _matmul("ssm_glu", gy, full['w_ssm_glu'], "nn")

    tcd = 256 if d % 256 == 0 and off_ga % 256 == 0 else LANES
    assert d % tcd == 0 and off_ga % tcd == 0 and off_gs % tcd == 0
    trm = _pick(l, 1024, 8)
    mix_in_specs = [_t(trm, tcd), _t(trm, tcd, d), _t(trm, tcd), _t(trm, tcd, off_ga), _t(trm, tcd, off_gs)]
    mixed = _tile_call("gate_mix", _mix_fn, (d // tcd, l // trm), [glu, glu, attn, proj, proj], mix_in_specs,
                       [_sds((l, d), BF16)], [_t(trm, tcd)])[0]
    mixout = _matmul("mix_out", mixed, full['w_out'], "nn")

    def res_norm_fn(xv, mo, g1v, gv, scv, shv):
        x2v = xv + g1v * mo
        return x2v, _norm_mod(x2v, gv, scv, shv)

    x2, h2, h2_t = _tile_call("residual_norm_mod_ffn", with_t(res_norm_fn), (1, nr), [xs, mixout, g1, g_ffn, sc2, sh2],
                              [_t(tr, d), _t(tr, d), _v(d), _v(d), _v(d), _v(d)],
                              [_sds((l, d)), _sds((l, d), BF16), _sds((d, l), BF16)],
                              [_t(tr, d), _t(tr, d), _tt(tr, d)])
    full['w_ffn_up'], = _exchange_wait("gather_ffn_up_wait", h_up, h2)
    up = _matmul("ffn_up", h2, full['w_ffn_up'], "nn", tn=1408)

    tcf, trc = _pick(dff, 1408), _pick(l, 512, 8)
    assert dff % tcf == 0
    ncf = dff // tcf

    taps = [conv_w[i:i + 1] for i in range(3)]

    def conv_gate(gp, gp_prev, w0, w1, w2, bv):
        prev = jnp.where(pl.program_id(1) == 0, 0.0, 1.0) * gp_prev
        ext = jnp.concatenate([prev, gp], axis=0)
        m1 = pltpu.roll(ext, 1, 0)[8:]
        m2 = pltpu.roll(ext, 2, 0)[8:]
        return w0 * m2 + w1 * m1 + w2 * gp + bv, m1, m2

    def convglu_fn(gp, gp_prev, val, w0, w1, w2, bv):
        gate, _, _ = conv_gate(gp, gp_prev, w0, w1, w2, bv)
        return gate * jax.nn.sigmoid(gate) * val

    act, act_t = _tile_call("conv_swiglu", with_t(convglu_fn), (ncf, l // trc), [up, up, up] + taps + [conv_b],
                            [_t(trc, tcf), _prev8(trc, tcf), _t(trc, tcf, dff)] + [_v(tcf)] * 4,
                            [_sds((l, dff), BF16), _sds((dff, l), BF16)], [_t(trc, tcf), _tt(trc, tcf)])
    full['w_ffn_down'] = _exchange_wait("gather_ffn_down_wait", h_down, act)[0].reshape(row_sharded['w_ffn_down'])
    ffn = _matmul("ffn_down", act, full['w_ffn_down'], "nn")

    def final_fn(x2v, fv, g2v, gv, tv):
        rows = x2v.shape[0]

        def loss_of(x2a, fa, g2a, ga):
            out = _rms(x2a + g2a * fa, ga)
            err = out - tv
            return 0.5 * _colsum(jnp.mean(err * err, axis=-1, keepdims=True))

        loss, vjp = jax.vjp(loss_of, x2v, fv, _bc(g2v, rows), _bc(gv, rows))
        dx3, dffn, dg2, dgf = vjp(jnp.ones((1, 1), F32))
        return jnp.broadcast_to(loss, (1, LANES)), dx3, dffn, _colsum(dg2), _colsum(dgf)

    loss_p, dx3, dffn, dg2, dg_fin = _tile_call(
        "loss_final_norm", final_fn, (1, nrh), [x2, ffn, g2, g_fin, tgt],
        [_t(trh, d), _t(trh, d), _v(d), _v(d), _t(trh, d)],
        [_sds((1, LANES)), _sds((l, d)), _sds((l, d), BF16), _sds((1, d)), _sds((1, d))],
        [_v(LANES), _t(trh, d), _t(trh, d), _v(d), _v(d)], acc=(0, 3, 4))
    loss = lax.psum(loss_p[0, 0], ("x", "y", "c"))

    dact = _matmul("d_act", dffn, full['w_ffn_down'], "nt", tn=1408)
    gd, gd16, pending = {}, {}, []
    dw_down, dw_down16 = _matmul("dw_ffn_down", act_t, dffn, "nn", tm=512, also_bf16=True)
    gd['w_ffn_down'], gd16['w_ffn_down'] = [z.reshape((N_DEV,) + w_ffn_down.shape[1:]) for z in (dw_down, dw_down16)]
    handle, tok = _exchange_start("grad_ffn_down_start", [gd16['w_ffn_down']], False, dw_down)
    pending.append((['w_ffn_down'], handle))
    conv_b_bwd = conv_b + tok[0:1, 0:1]

    def convglu_bwd_fn(gp, gp_prev, val, da, w0, w1, w2, bv):
        gate, m1, m2 = conv_gate(gp, gp_prev, w0, w1, w2, bv)
        sg = jax.nn.sigmoid(gate)
        dgate = da * val * (sg * (1.0 + gate * (1.0 - sg)))
        return (dgate, da * (gate * sg), _colsum(dgate), _colsum(dgate * m2), _colsum(dgate * m1),
                _colsum(dgate * gp))

    dgate, dval, dconv_b, dcw0, dcw1, dcw2 = _tile_call(
        "conv_swiglu_bwd", convglu_bwd_fn, (ncf, nr), [up, up, up, dact] + taps + [conv_b_bwd],
        [_t(tr, tcf), _prev8(tr, tcf), _t(tr, tcf, dff), _t(tr, tcf)] + [_v(tcf)] * 4,
        [_sds((l, dff)), _sds((l, dff), BF16)] + [_sds((1, dff))] * 4,
        [_t(tr, tcf), _t(tr, tcf)] + [_v(tcf)] * 4, acc=(2, 3, 4, 5))

    def conv_t_fn(dg, dg_next, dv, w0, w1, w2):
        rows = dg.shape[0]
        nxt = jnp.where(pl.program_id(1) == pl.num_programs(1) - 1, 0.0, 1.0) * dg_next
        ext = jnp.concatenate([dg, nxt], axis=0)
        p1 = pltpu.roll(ext, rows + 8 - 1, 0)[:rows]
        p2 = pltpu.roll(ext, rows + 8 - 2, 0)[:rows]
        return jnp.stack([w2 * dg + w1 * p1 + w0 * p2, dv.astype(F32)], axis=0)

    dup = _tile_call("conv_transpose", conv_t_fn, (ncf, nr), [dgate, dgate, dval] + taps,
                     [_t(tr, tcf), _next8(tr, tcf, l), _t(tr, tcf)] + [_v(tcf)] * 3,
                     [_sds((2, l, dff), BF16)], [_st(tr, tcf)])[0]
    dh2 = _matmul("d_h2", dup, full['w_ffn_up'], "nt")
    gd['w_ffn_up'], gd16['w_ffn_up'] = _matmul("dw_ffn_up", h2_t, dup, "nn", tm=512, tn=1408, out_stack=N_DEV, also_bf16=True)
    handle, tok = _exchange_start("grad_ffn_up_start", [gd16['w_ffn_up']], False, gd['w_ffn_up'])
    pending.append((['w_ffn_up'], handle))
    g_ffn_bwd = g_ffn + tok[0:1, 0:1]

    def res_norm_bwd_fn(xv, mo, g1v, gv, scv, shv, dhv, dxv):
        rows = xv.shape[0]
        _, vjp = jax.vjp(res_norm_fn, xv, mo, _bc(g1v, rows), _bc(gv, rows), _bc(scv, rows), _bc(shv, rows))
        dx, dmo, dg1v, dgv, dscv, dshv = vjp((dxv, dhv))
        return dx, dmo, _colsum(dg1v), _colsum(dgv), _colsum(dscv), _colsum(dshv)

    dx2, dmixout, dg1, dg_ffn, dsc2, dsh2 = _tile_call(
        "residual_norm_mod_ffn_bwd", res_norm_bwd_fn, (1, nrh), [xs, mixout, g1, g_ffn_bwd, sc2, sh2, dh2, dx3],
        [_t(trh, d), _t(trh, d), _v(d), _v(d), _v(d), _v(d), _t(trh, d), _t(trh, d)],
        [_sds((l, d)), _sds((l, d), BF16)] + [_sds((1, d))] * 4,
        [_t(trh, d), _t(trh, d)] + [_v(d)] * 4, acc=(2, 3, 4, 5))

    dmixed = _matmul("d_mixed", dmixout, full['w_out'], "nt")
    dw_out, dw_out16 = _matmul("dw_out", mixed, dmixout, "tn", also_bf16=True)
    gd['w_out'], gd16['w_out'] = [z.reshape((N_DEV,) + w_out.shape[1:]) for z in (dw_out, dw_out16)]

    def mix_bwd_fn(ga_, gb_, at, pa, ps, dm):
        _, vjp = jax.vjp(_mix_fn, ga_, gb_, at, pa, ps)
        da, db, dat, dpa, dps = vjp(dm)
        return jnp.stack([da, db], axis=0), dat, dpa, dps

    dglu, dattn, dga, dgs = _tile_call(
        "gate_mix_bwd", mix_bwd_fn, (d // tcd, l // trm), [glu, glu, attn, proj, proj, dmixed],
        mix_in_specs + [_t(trm, tcd)],
        [_sds((2, l, d), BF16)] + [_sds((l, d), BF16)] * 3, [_st(trm, tcd)] + [_t(trm, tcd)] * 3)

    dgy = _matmul("d_gelu_y", dglu, full['w_ssm_glu'], "nt")
    gd['w_ssm_glu'], gd16['w_ssm_glu'] = _matmul("dw_ssm_glu", gy, dglu, "tn", out_stack=N_DEV, also_bf16=True)

    def gelu_bwd_fn(yv, dv):
        _, vjp = jax.vjp(lambda z: jax.nn.gelu(z), yv)
        return vjp(dv)[0]

    dy = _tile_call("gelu_bwd", gelu_bwd_fn, (ssm_w // tcs, l // trg), [y, dgy], [_t(trg, tcs), _t(trg, tcs)],
                    [_sds((l, ssm_w))], [_t(trg, tcs)])[0]
    du, dbd, dcbdt, dlam, dd_tiles = _s5_bwd(proj, off_u, dy, hst, bd, bdt, cbdt, lam, dvec, t_len)

    def gpn_of(z):
        return z.transpose(2, 0, 1, 3).reshape(pgn)

    dbb_r = gpn_of(_diag_blocks(dbd[:, :, :TILE_STATES], SSM_P))
    dbb_i = gpn_of(_diag_blocks(dbd[:, :, TILE_STATES:], SSM_P))
    dc_re = _diag_blocks(dcbdt[:, :, :TILE_STATES], SSM_P).reshape(n_groups, SSM_P, SSM_N)
    dc_im = _diag_blocks(dcbdt[:, :, TILE_STATES:], SSM_P).reshape(n_groups, SSM_P, SSM_N)
    dlam_r, dlam_i = dlam[:, 0].reshape(gn), dlam[:, 1].reshape(gn)

    def disc_bwd_fn(ar, ai, ld, br, bi, dlr, dli, dbr, dbi):
        _, vjp = jax.vjp(_s5_disc_fn, ar, ai, ld, br, bi)
        return vjp((dlr, dli, dbr, dbi))

    da_re, da_im, dlog_dt, db_re, db_im = _tile_call(
        "s5_discretise_bwd", disc_bwd_fn, (1,), disc_ins + [dlam_r, dlam_i, dbb_r, dbb_i],
        disc_specs + [_full_spec(gn), _full_spec(gn), _full_spec(pgn), _full_spec(pgn)],
        [_sds(gn), _sds(gn), _sds((n_groups, 1)), _sds(pgn), _sds(pgn)], disc_specs)

    do2 = _matmul("d_attn_heads", dattn, full['w_attn_proj'], "nt")
    gd['w_attn_proj'], gd16['w_attn_proj'] = _matmul("dw_attn_proj", o2, dattn, "tn", out_stack=N_DEV, also_bf16=True)
    handle, tok = _exchange_start("grad_mixer_start", [gd16[k] for k in mixer_w], False, gd['w_attn_proj'])
    pending.append((mixer_w, handle))
    do_h = heads(do2.astype(BF16), hq)
    dq_h, dk_h, dv_h, dsink = _attn_bwd(qh, kh, vh, sinks3 + tok[0:1, 0:1], do_h)

    def unheads(z):
        return z.transpose(1, 0, 2).reshape(l, z.shape[0] * HEAD_DIM)

    dproj = jnp.concatenate([unheads(dq_h), unheads(dk_h), unheads(dv_h), du, dga, dgs], axis=1)
    dw_in, dw_in16 = _matmul("dw_in", h1_t, dproj, "nn", tm=512, tn=1280, also_bf16=True)
    dcw = jnp.concatenate([dcw0, dcw1, dcw2], axis=0)
    shard_in, shard_cw = w_in.shape[1:], ffn_conv_w.shape[1:]
    gd['w_in'], gd16['w_in'] = [z.reshape(shard_in[0], N_DEV, shard_in[1]).transpose(1, 0, 2) for z in (dw_in, dw_in16)]
    gd['ffn_conv_w'] = dcw.reshape(shard_cw[0], N_DEV, shard_cw[1]).transpose(1, 0, 2)
    gd16['ffn_conv_w'] = gd['ffn_conv_w'].astype(BF16)
    handle, tok = _exchange_start("grad_in_start", [gd16['w_in'], gd16['ffn_conv_w']], False, dw_in)
    pending.append((['w_in', 'ffn_conv_w'], handle))
    dh1 = _matmul("d_h1", dproj, full['w_in'], "nt", dep=tok)

    def norm_bwd_fn(xv, gv, scv, shv, dhv, dxv):
        rows = xv.shape[0]
        _, vjp = jax.vjp(_norm_mod, xv, _bc(gv, rows), _bc(scv, rows), _bc(shv, rows))
        dx, dgv, dscv, dshv = vjp(dhv)
        return dx + dxv, _colsum(dgv), _colsum(dscv), _colsum(dshv)

    grad_x, dg_mix, dsc1, dsh1 = _tile_call(
        "norm_mod_mix_bwd", norm_bwd_fn, (1, nrh), [xs, g_mix, sc1, sh1, dh1, dx2],
        [_t(trh, d), _v(d), _v(d), _v(d), _t(trh, d), _t(trh, d)],
        [_sds((l, d))] + [_sds((1, d))] * 3, [_t(trh, d)] + [_v(d)] * 3, acc=(1, 2, 3))

    dmod = jnp.concatenate([dsh1, dsc1, dg1, dsh2, dsc2, dg2], axis=1)
    small = ['ada_b', 'norm_mix_g', 'attn_sinks', 'ssm_a_re', 'ssm_a_im', 'ssm_log_dt', 'ssm_b_re', 'ssm_b_im',
             'ssm_c_re', 'ssm_c_im', 'ssm_d', 'norm_ffn_g', 'ffn_conv_b', 'final_g']
    small_grads = {
        'ada_b': dmod, 'norm_mix_g': dg_mix, 'attn_sinks': dsink[:, 0, 0], 'ssm_a_re': da_re, 'ssm_a_im': da_im,
        'ssm_log_dt': dlog_dt, 'ssm_b_re': db_re.transpose(1, 2, 0), 'ssm_b_im': db_im.transpose(1, 2, 0),
        'ssm_c_re': dc_re, 'ssm_c_im': dc_im, 'ssm_d': dd_tiles, 'norm_ffn_g': dg_ffn, 'ffn_conv_b': dconv_b,
        'final_g': dg_fin}
    gs_pack, sm_offs = _pack([small_grads[k] for k in small], LANES, 8)
    (gs_all,) = _all_gather("gather_small_grads", [gs_pack])
    ws_pack, _ = _pack([given[k] for k in small], LANES, 8)
    ms_pack, _ = _pack([given['m_' + k] for k in small], LANES, 8)
    vs_pack, _ = _pack([given['v_' + k] for k in small], LANES, 8)
    small_out = _adamw("adamw_replicated", gs_all, ws_pack, ms_pack, vs_pack)

    dmod_all = _unpack(gs_all, sm_offs[0], (N_DEV * mod_n,), lead=(N_DEV,))
    dmod_mine = lax.dynamic_slice_in_dim(dmod_all, idx * mod_n, mod_n, axis=1)
    kpad = LANES - N_DEV
    cond_t = jnp.pad(cond_all.T, ((0, 0), (0, kpad)))
    dmod_pad = jnp.pad(dmod_mine, ((0, kpad), (0, 0)))
    g_ada_w = _matmul("dw_ada", cond_t, dmod_pad, "nn")
    ada_out = _adamw("adamw_ada_w", g_ada_w[None], ada_w[0], m_ada_w[0], v_ada_w[0])

    sharded = big + ['ffn_conv_w']
    place = jnp.reshape(idx, (1,)).astype(jnp.int32)
    sharded_out = {}
    for group, handle in pending:
        for k, parts in zip(group, _exchange_wait("grad_" + group[0] + "_wait", handle, ada_out[0])):
            sharded_out[k] = _adamw_sharded("adamw_" + k, parts, gd[k], given[k][0], given['m_' + k][0],
                                            given['v_' + k][0], place)

    results = [{}, {}, {}, {}]
    for which in range(4):
        for k, off in zip(small, sm_offs):
            results[which][k] = _unpack(small_out[which], off, given[k].shape)
        for k in sharded:
            results[which][k] = sharded_out[k][which][None]
        results[which]['ada_w'] = ada_out[which][None]
    outs = [loss, grad_x[None]]
    for which in range(4):
        outs += [results[which][k] for k in names]
    return tuple(outs)
```

```python
import functools
import math

import jax
import jax.numpy as jnp
from jax import lax
from jax.experimental import pallas as pl
from jax.experimental.pallas import tpu as pltpu

F32, BF16 = jnp.float32, jnp.bfloat16
MESH = pl.DeviceIdType.MESH
N_DEV = 8

HEAD_DIM = 64
N_KV_HEADS = 2
ATT_BLOCK = 128
NEG_INF = -1e30
SSM_P = 16
SSM_N = 64
LANES = 128
TILE_GROUPS = LANES // SSM_P
TILE_STATES = TILE_GROUPS * SSM_N
RMS_EPS = 1e-6
ADAM_LR, ADAM_B1, ADAM_B2, ADAM_EPS, ADAM_WD, ADAM_STEP = 0.001, 0.9, 0.999, 1e-08, 0.01, 10
VMEM_LIMIT = 56 * 1024 * 1024
MATMUL_VMEM_BUDGET = 44 * 1024 * 1024


def _params(n_axes):
    return pltpu.CompilerParams(dimension_semantics=("arbitrary",) * n_axes, vmem_limit_bytes=VMEM_LIMIT)


def _pick(dim, pref, align=128):
    if dim <= align:
        return dim
    t = (min(pref, dim) // align) * align
    while t > align and dim % t:
        t -= align
    assert dim % t == 0, (dim, pref, align)
    return t


def _dev():
    return lax.axis_index("x"), lax.axis_index("y"), lax.axis_index("c")


def _tile_call(name, fn, grid, ins, in_specs, out_shapes, out_specs, acc=()):
    n_in, n_out = len(ins), len(out_shapes)
    acc_axis = len(grid) - 1

    def body(*refs):
        vals = fn(*[r[...] for r in refs[:n_in]])
        if not isinstance(vals, (tuple, list)):
            vals = (vals,)
        assert len(vals) == n_out
        for i, (r, v) in enumerate(zip(refs[n_in:], vals)):
            v = v.astype(r.dtype)
            if i in acc:
                first = pl.program_id(acc_axis) == 0

                @pl.when(first)
                def _():
                    r[...] = v

                @pl.when(jnp.logical_not(first))
                def _():
                    r[...] += v
            else:
                r[...] = v

    return pl.pallas_call(
        body, grid=grid, in_specs=in_specs, out_specs=out_specs, out_shape=out_shapes, name=name,
        compiler_params=_params(len(grid)),
    )(*ins)


def _t(tr, tc, off=0):
    return pl.BlockSpec((tr, tc), lambda j, i: (i, j + off // tc))


def _tt(tr, tc):
    return pl.BlockSpec((tc, tr), lambda j, i: (j, i))


def _v(tc, off=0, rows=1):
    return pl.BlockSpec((rows, tc), lambda j, i: (0, j + off // tc))


def _prev8(tr, tc, off=0):
    return pl.BlockSpec((8, tc), lambda j, i: (jnp.maximum(i * (tr // 8) - 1, 0), j + off // tc))


def _next8(tr, tc, nrows, off=0):
    return pl.BlockSpec((8, tc), lambda j, i: (jnp.minimum((i + 1) * (tr // 8), nrows // 8 - 1), j + off // tc))


def _st(tr, tc):
    return pl.BlockSpec((2, tr, tc), lambda j, i: (0, i, j))


def _bc(v, rows):
    return jnp.broadcast_to(v, (rows, v.shape[-1]))


def _colsum(v):
    return jnp.sum(v, axis=0, keepdims=True)


def _matmul(name, a, b, mode, out_dtype=F32, tm=1024, tn=1024, tk=None, out_stack=None, also_bf16=False, dep=None):
    def dims(z):
        return (z.shape[-2], z.shape[-1] * (z.shape[0] if z.ndim == 3 else 1))

    ar, ac = dims(a)
    br, bc = dims(b)
    if mode == "nn":
        m, k, n = ar, ac, bc
        assert br == k
    elif mode == "nt":
        m, k, n = ar, ac, br
        assert bc == k
    else:
        m, k, n = ac, ar, bc
        assert br == k
    m_lim, k_lim, n_lim = [m], [k], [n]
    if a.ndim == 3:
        (m_lim if mode == "tn" else k_lim).append(a.shape[-1])
    if b.ndim == 3:
        (k_lim if mode == "nt" else n_lim).append(b.shape[-1])
    if out_stack:
        n_lim.append(n // out_stack)
    tm = _pick(functools.reduce(math.gcd, m_lim), tm)
    tn = _pick(functools.reduce(math.gcd, n_lim), tn)
    k_unit = functools.reduce(math.gcd, k_lim)
    if tk is None:
        sa, sb, so = a.dtype.itemsize, b.dtype.itemsize, jnp.dtype(out_dtype).itemsize + (2 if also_bf16 else 0)
        fits = [t for t in range(LANES, k_unit + 1, LANES) if k_unit % t == 0 and
                2 * t * (tm * sa + tn * sb) + tm * tn * (2 * so + (4 if t < k else 0)) <= MATMUL_VMEM_BUDGET]
        tk = max(fits) if fits else _pick(k_unit, 512)
    else:
        tk = _pick(k_unit, tk)
    nk = k // tk

    def spec(z, brows, bcols, ridx, cidx):
        if z.ndim == 3:
            per = z.shape[-1] // bcols
            return pl.BlockSpec((None, brows, bcols),
                                lambda i, j, kk: (cidx(i, j, kk) // per, ridx(i, j, kk), cidx(i, j, kk) % per))
        return pl.BlockSpec((brows, bcols), lambda i, j, kk: (ridx(i, j, kk), cidx(i, j, kk)))

    gi = lambda i, j, kk: i
    gj = lambda i, j, kk: j
    gk = lambda i, j, kk: kk
    if mode == "nn":
        a_spec, b_spec = spec(a, tm, tk, gi, gk), spec(b, tk, tn, gk, gj)
        dn = (((1,), (0,)), ((), ()))
    elif mode == "nt":
        a_spec, b_spec = spec(a, tm, tk, gi, gk), spec(b, tn, tk, gj, gk)
        dn = (((1,), (1,)), ((), ()))
    else:
        a_spec, b_spec = spec(a, tk, tm, gk, gi), spec(b, tk, tn, gk, gj)
        dn = (((0,), (0,)), ((), ()))

    n_out = 2 if also_bf16 else 1

    deps = [] if dep is None else [dep]

    def body(a_ref, b_ref, *rest):
        rest = rest[len(deps):]
        o_refs, acc = rest[:n_out], rest[n_out:]
        part = lax.dot_general(a_ref[...].astype(BF16), b_ref[...].astype(BF16), dn, preferred_element_type=F32)

        def emit(val):
            for o_ref in o_refs:
                o_ref[...] = val.astype(o_ref.dtype)

        if nk == 1:
            emit(part)
            return
        acc_ref, = acc
        kk = pl.program_id(2)

        @pl.when(kk == 0)
        def _():
            acc_ref[...] = part

        @pl.when(kk > 0)
        def _():
            acc_ref[...] += part

        @pl.when(kk == nk - 1)
        def _():
            emit(acc_ref[...])

    if out_stack:
        per = (n // out_stack) // tn
        out_spec = pl.BlockSpec((None, tm, tn), lambda i, j, kk: (j // per, i, j % per))
        shape = (out_stack, m, n // out_stack)
    else:
        out_spec = pl.BlockSpec((tm, tn), lambda i, j, kk: (i, j))
        shape = (m, n)
    dtypes = [out_dtype, BF16][:n_out]
    res = pl.pallas_call(
        body, grid=(m // tm, n // tn, nk),
        in_specs=[a_spec, b_spec] + [pl.BlockSpec(memory_space=pl.ANY)] * len(deps), out_specs=[out_spec] * n_out,
        out_shape=[jax.ShapeDtypeStruct(shape, dt) for dt in dtypes],
        scratch_shapes=[pltpu.VMEM((tm, tn), F32)] if nk > 1 else [], name=name, compiler_params=_params(3),
    )(a, b, *deps)
    return res if also_bf16 else res[0]


def _all_gather(name, arrs):
    n = len(arrs)

    def body(*refs):
        ins, outs = refs[:n], refs[n:2 * n]
        send_sems, recv_sems, local_sems = refs[2 * n:]
        x, y, c = _dev()
        me, sib = (x, y, c), (x, y, 1 - c)
        chips = [(1 - x, y), (x, 1 - y), (1 - x, 1 - y)]

        def slot(p):
            return 4 * p[0] + 2 * p[1] + p[2]

        def copy(a, k, block, to, src=None):
            dst = outs[a].at[slot(block)]
            return pltpu.make_async_remote_copy(
                src_ref=dst if src is None else src, dst_ref=dst,
                send_sem=send_sems.at[7 * a + k], recv_sem=recv_sems.at[7 * a + k],
                device_id=to, device_id_type=MESH)

        mine = [pltpu.make_async_copy(ins[a], outs[a].at[slot(me)], local_sems.at[a]) for a in range(n)]
        for cp in mine:
            cp.start()
        first = []
        for a in range(n):
            first.append(copy(a, 0, me, sib, src=ins[a]))
            first += [copy(a, 1 + j, me, (*chip, c), src=ins[a]) for j, chip in enumerate(chips)]
        for cp in first:
            cp.start()
        passed = []
        for j, chip in enumerate(chips):
            for a in range(n):
                copy(a, 1 + j, (*chip, c), me).wait_recv()
                cp = copy(a, 4 + j, (*chip, c), sib)
                cp.start()
                passed.append(cp)
        for a in range(n):
            copy(a, 0, sib, me).wait_recv()
            for j, chip in enumerate(chips):
                copy(a, 4 + j, (*chip, 1 - c), me).wait_recv()
        for cp in first + passed:
            cp.wait_send()
        for cp in mine:
            cp.wait()

    any_spec = pl.BlockSpec(memory_space=pl.ANY)
    return pl.pallas_call(
        body, in_specs=[any_spec] * n, out_specs=[any_spec] * n,
        out_shape=[jax.ShapeDtypeStruct((N_DEV,) + a.shape, a.dtype) for a in arrs],
        scratch_shapes=[pltpu.SemaphoreType.DMA((7 * n,)), pltpu.SemaphoreType.DMA((7 * n,)),
                        pltpu.SemaphoreType.DMA((n,))],
        name=name,
    )(*arrs)


def _grad_to_sibling(gds):
    n = len(gds)

    def body(*refs):
        g_refs, r_refs = refs[:n], refs[n:2 * n]
        send_sems, recv_sems = refs[2 * n:]
        x, y, c = _dev()
        cps = []
        for a in range(n):
            for k in range(4):
                cp = pltpu.make_async_remote_copy(
                    src_ref=g_refs[a].at[2 * k + (1 - c)], dst_ref=r_refs[a].at[k],
                    send_sem=send_sems.at[4 * a + k], recv_sem=recv_sems.at[4 * a + k],
                    device_id=(x, y, 1 - c), device_id_type=MESH)
                cp.start()
                cps.append(cp)
        for cp in cps:
            cp.wait()

    any_spec = pl.BlockSpec(memory_space=pl.ANY)
    return pl.pallas_call(
        body, in_specs=[any_spec] * n, out_specs=[any_spec] * n,
        out_shape=[jax.ShapeDtypeStruct((4,) + g.shape[1:], g.dtype) for g in gds],
        scratch_shapes=[pltpu.SemaphoreType.DMA((4 * n,)), pltpu.SemaphoreType.DMA((4 * n,))],
        name="grad_to_sibling",
    )(*gds)


def _chip_sum(name, gd, from_sib, c_arr):
    _, k, n = gd.shape
    tr = _pick(k, max(16, (1 << 20) // (4 * n)), 16)

    def body(c_ref, a_ref, b_ref, o_ref):
        o_ref[...] = (a_ref[...] + b_ref[...]).astype(o_ref.dtype)

    return pl.pallas_call(
        body,
        grid_spec=pltpu.PrefetchScalarGridSpec(
            num_scalar_prefetch=1, grid=(4, k // tr),
            in_specs=[pl.BlockSpec((1, tr, n), lambda q, i, cr: (2 * q + cr[0], i, 0)),
                      pl.BlockSpec((1, tr, n), lambda q, i, cr: (q, i, 0))],
            out_specs=pl.BlockSpec((1, tr, n), lambda q, i, cr: (q, i, 0))),
        out_shape=jax.ShapeDtypeStruct((4, k, n), BF16), name=name, compiler_params=_params(2),
    )(c_arr, gd, from_sib)


def _grad_to_chips(sums):
    n = len(sums)

    def body(*refs):
        s_refs, p_refs = refs[:n], refs[n:2 * n]
        send_sems, recv_sems, local_sems = refs[2 * n:]
        x, y, c = _dev()
        my_chip = 2 * x + y
        cps = []
        for a in range(n):
            local = pltpu.make_async_copy(s_refs[a].at[my_chip], p_refs[a].at[my_chip], local_sems.at[a])
            local.start()
            cps.append(local)
            for j, (px, py) in enumerate([(1 - x, y), (x, 1 - y), (1 - x, 1 - y)]):
                cp = pltpu.make_async_remote_copy(
                    src_ref=s_refs[a].at[2 * px + py], dst_ref=p_refs[a].at[my_chip],
                    send_sem=send_sems.at[3 * a + j], recv_sem=recv_sems.at[3 * a + j],
                    device_id=(px, py, c), device_id_type=MESH)
                cp.start()
                cps.append(cp)
        for cp in cps:
            cp.wait()

    any_spec = pl.BlockSpec(memory_space=pl.ANY)
    return pl.pallas_call(
        body, in_specs=[any_spec] * n, out_specs=[any_spec] * n,
        out_shape=[jax.ShapeDtypeStruct(s.shape, s.dtype) for s in sums],
        scratch_shapes=[pltpu.SemaphoreType.DMA((3 * n,)), pltpu.SemaphoreType.DMA((3 * n,)),
                        pltpu.SemaphoreType.DMA((n,))],
        name="grad_to_chips",
    )(*sums)


FLIPS = [(0, 0, 1), (0, 1, 0), (1, 0, 0), (0, 1, 1), (1, 0, 1), (1, 1, 0), (1, 1, 1)]
N_PEERS = len(FLIPS)
_HBM = pl.BlockSpec(memory_space=pltpu.HBM)
_SEM = pl.BlockSpec(memory_space=pltpu.SEMAPHORE)
_EFFECT = pltpu.SideEffectType.DATAFLOW_SIDE_EFFECTING


def _flip(x, y, c, f):
    return (1 - x if f[0] else x, 1 - y if f[1] else y, 1 - c if f[2] else c)


def _slot(p):
    return 4 * p[0] + 2 * p[1] + p[2]


def _exchange_copies(src_refs, land_refs, send_sems, recv_sems, gather):
    x, y, c = _dev()
    mine = _slot((x, y, c))
    cps = []
    for a, (src, land) in enumerate(zip(src_refs, land_refs)):
        for k, f in enumerate(FLIPS):
            peer = _flip(x, y, c, f)
            cps.append(pltpu.make_async_remote_copy(
                src_ref=src if gather else src.at[_slot(peer)], dst_ref=land.at[mine],
                send_sem=send_sems.at[N_PEERS * a + k], recv_sem=recv_sems.at[N_PEERS * a + k],
                device_id=peer, device_id_type=MESH))
    return cps


def _exchange_start(name, srcs, gather, after):
    n = len(srcs)
    lands = [lax.empty(((N_DEV,) + s.shape) if gather else s.shape, s.dtype) for s in srcs]

    def body(*refs):
        src_refs, land_refs = refs[:n], refs[n:2 * n]
        send_sems, recv_sems, local_sems = refs[2 * n + 1:2 * n + 4]
        token = refs[-1]
        if gather:
            x, y, c = _dev()
            for a in range(n):
                pltpu.make_async_copy(src_refs[a], land_refs[a].at[_slot((x, y, c))], local_sems.at[a]).start()
        for cp in _exchange_copies(src_refs, land_refs, send_sems, recv_sems, gather):
            cp.start()
        token[...] = jnp.zeros_like(token)

    hbm = lambda z: pltpu.HBM(z.shape, z.dtype)
    outs = pl.pallas_call(
        body, name=name,
        out_shape=(pltpu.SemaphoreType.DMA((N_PEERS * n,)), pltpu.SemaphoreType.DMA((N_PEERS * n,)),
                   pltpu.SemaphoreType.DMA((n,)), *[hbm(s) for s in srcs], *[hbm(z) for z in lands],
                   jax.ShapeDtypeStruct((8, LANES), F32)),
        in_specs=[_HBM] * (2 * n) + [pl.BlockSpec(memory_space=pl.ANY)],
        out_specs=(_SEM, _SEM, _SEM, *[_HBM] * (2 * n), pl.BlockSpec(memory_space=pltpu.VMEM)),
        input_output_aliases={i: 3 + i for i in range(2 * n)},
        compiler_params=pltpu.CompilerParams(has_side_effects=_EFFECT),
    )(*[pltpu.with_memory_space_constraint(z, pltpu.HBM) for z in list(srcs) + lands], after)
    return (outs[:3], outs[3:3 + n], outs[3 + n:3 + 2 * n], gather), outs[-1]


def _exchange_wait(name, handles, after):
    sems, srcs, lands, gather = handles
    n = len(srcs)

    def body(*refs):
        src_refs, land_refs = refs[:n], refs[n:2 * n]
        send_sems, recv_sems, local_sems = refs[2 * n:2 * n + 3]
        if gather:
            for a in range(n):
                pltpu.make_async_copy(src_refs[a], land_refs[a].at[0], local_sems.at[a]).wait()
        for cp in _exchange_copies(src_refs, land_refs, send_sems, recv_sems, gather):
            cp.wait_send()
            cp.wait_recv()

    hbm = lambda z: pltpu.HBM(z.shape, z.dtype)
    outs = pl.pallas_call(
        body, name=name, out_shape=tuple(hbm(z) for z in list(srcs) + list(lands)),
        in_specs=[_HBM] * (2 * n) + [_SEM] * 3 + [pl.BlockSpec(memory_space=pl.ANY)],
        out_specs=tuple([_HBM] * (2 * n)), input_output_aliases={i: i for i in range(2 * n)},
        compiler_params=pltpu.CompilerParams(has_side_effects=_EFFECT),
    )(*srcs, *lands, *sems, after)
    return list(outs[n:])


def _pack_rows(sizes, width, row_align):
    offs, r = [], 0
    for s in sizes:
        offs.append(r)
        r += -(-s // width)
    total = -(-r // row_align) * row_align
    return offs, total


def _pack(items, width, row_align, lead=()):
    nl = len(lead)
    sizes = [int(jnp.size(a)) // max(1, functools.reduce(lambda p, q: p * q, lead, 1)) for a in items]
    offs, total = _pack_rows(sizes, width, row_align)
    flat = []
    used = 0
    for a, s in zip(items, sizes):
        f = a.reshape(lead + (s,))
        pad = -(-s // width) * width - s
        if pad:
            f = jnp.pad(f, [(0, 0)] * nl + [(0, pad)])
        flat.append(f)
        used += s + pad
    tail = total * width - used
    if tail:
        flat.append(jnp.zeros(lead + (tail,), items[0].dtype))
    return jnp.concatenate(flat, axis=-1).reshape(lead + (total, width)), offs


def _unpack(packed, off, shape, lead=()):
    nl = len(lead)
    size = functools.reduce(lambda p, q: p * q, shape, 1)
    width = packed.shape[-1]
    rows = -(-size // width)
    blk = lax.slice_in_dim(packed, off, off + rows, axis=nl).reshape(lead + (rows * width,))
    return lax.slice_in_dim(blk, 0, size, axis=nl).reshape(lead + tuple(shape))


def _rms(x, g):
    return (x * lax.rsqrt(jnp.mean(x * x, axis=-1, keepdims=True) + RMS_EPS)) * g


def _norm_mod(x, g, sc, sh):
    return _rms(x, g) * (1.0 + sc) + sh


def _mix_fn(glu_a, glu_b, attn, ga, gs):
    return jax.nn.sigmoid(ga) * attn + jax.nn.sigmoid(gs) * (glu_a * jax.nn.sigmoid(glu_b))


def _s5_disc_fn(a_re, a_im, log_dt, b_re, b_im):
    dt = jnp.exp(log_dt)
    mag = jnp.exp(a_re * dt)
    lr, li = mag * jnp.cos(a_im * dt), mag * jnp.sin(a_im * dt)
    den = a_re * a_re + a_im * a_im
    zr = ((lr - 1.0) * a_re + li * a_im) / den
    zi = (li * a_re - (lr - 1.0) * a_im) / den
    return lr, li, zr[None] * b_re - zi[None] * b_im, zr[None] * b_im + zi[None] * b_re


def _adamw_fn(w, g, m, v):
    m = ADAM_B1 * m + (1.0 - ADAM_B1) * g
    v = ADAM_B2 * v + (1.0 - ADAM_B2) * jnp.square(g)
    m_hat = m / (1.0 - ADAM_B1 ** ADAM_STEP)
    v_hat = v / (1.0 - ADAM_B2 ** ADAM_STEP)
    delta = -ADAM_LR * (m_hat / (jnp.sqrt(v_hat) + ADAM_EPS) + ADAM_WD * w)
    return delta, m, v


def _adamw(name, parts, w, m, v):
    p, r, c = parts.shape
    tr = _pick(r, max(8, (1 << 21) // (4 * c * max(p, 2))), 8)

    def fn(pv, wv, mv, vv):
        g = pv[0]
        for i in range(1, p):
            g = g + pv[i]
        d, m2, v2 = _adamw_fn(wv, g, mv, vv)
        return g, d, m2, v2

    spec = pl.BlockSpec((tr, c), lambda i: (i, 0))
    return _tile_call(
        name, fn, (r // tr,), [parts, w, m, v],
        [pl.BlockSpec((p, tr, c), lambda i: (0, i, 0)), spec, spec, spec],
        [jax.ShapeDtypeStruct((r, c), F32)] * 4, [spec] * 4)


def _adamw_sharded(name, parts, own_src, w, m, v, place):
    _, k, n = parts.shape
    tr = _pick(k, max(16, (1 << 19) // (4 * n)), 16)

    def body(pl_ref, p_ref, a_ref, w_ref, m_ref, v_ref, g_ref, d_ref, m2_ref, v2_ref):
        own = a_ref[0]
        g = None
        for q in range(N_DEV):
            term = jnp.where(pl_ref[0] == q, own, p_ref[q].astype(F32))
            g = term if g is None else g + term
        d, m2, v2 = _adamw_fn(w_ref[...], g, m_ref[...], v_ref[...])
        g_ref[...] = g
        d_ref[...] = d
        m2_ref[...] = m2
        v2_ref[...] = v2

    spec = pl.BlockSpec((tr, n), lambda i, pr: (i, 0))
    return pl.pallas_call(
        body,
        grid_spec=pltpu.PrefetchScalarGridSpec(
            num_scalar_prefetch=1, grid=(k // tr,),
            in_specs=[pl.BlockSpec((N_DEV, tr, n), lambda i, pr: (0, i, 0)),
                      pl.BlockSpec((1, tr, n), lambda i, pr: (pr[1], i, 0)),
                      spec, spec, spec],
            out_specs=[spec] * 4),
        out_shape=[jax.ShapeDtypeStruct((k, n), F32)] * 4, name=name, compiler_params=_params(1),
    )(place, parts, own_src, w, m, v)


def _attn_mask(n, rows):
    qi = lax.broadcasted_iota(jnp.int32, (rows, 2 * ATT_BLOCK), 0) & (ATT_BLOCK - 1)
    kj = lax.broadcasted_iota(jnp.int32, (rows, 2 * ATT_BLOCK), 1)
    rel = qi + ATT_BLOCK - kj
    return (rel >= 0) & (rel < ATT_BLOCK) & ((kj >= ATT_BLOCK) | (n > 0))


def _attn_probs(q, k, sink, mask):
    s = lax.dot_general(q, k, (((1,), (1,)), ((), ())), preferred_element_type=F32) * (HEAD_DIM ** -0.5)
    s = jnp.where(mask, s, NEG_INF)
    m = jnp.maximum(jnp.max(s, axis=-1, keepdims=True), sink)
    p = jnp.exp(s - m)
    e_sink = jnp.exp(sink - m)
    inv = 1.0 / (jnp.sum(p, axis=-1, keepdims=True) + e_sink)
    return p * inv, e_sink * inv


def _attn_specs(qpk):
    blk = ATT_BLOCK
    q_spec = pl.BlockSpec((qpk, blk, HEAD_DIM), lambda h, n: (h, n, 0))
    cur = pl.BlockSpec((1, blk, HEAD_DIM), lambda h, n: (h, n, 0))
    prev = pl.BlockSpec((1, blk, HEAD_DIM), lambda h, n: (h, jnp.maximum(n - 1, 0), 0))
    sink_spec = pl.BlockSpec((1, qpk * blk, 1), lambda h, n: (h, 0, 0))
    return q_spec, cur, prev, sink_spec


def _attn_fwd(q, k, v, sinks):
    hq, l, _ = q.shape
    qpk = hq // N_KV_HEADS
    nb = l // ATT_BLOCK
    rows = qpk * ATT_BLOCK
    q_spec, cur, prev, sink_spec = _attn_specs(qpk)

    def body(q_ref, kp_ref, kc_ref, vp_ref, vc_ref, sink_ref, o_ref):
        mask = _attn_mask(pl.program_id(1), rows)
        kk = jnp.concatenate([kp_ref[0], kc_ref[0]], axis=0).astype(BF16)
        vv = jnp.concatenate([vp_ref[0], vc_ref[0]], axis=0).astype(BF16)
        p, _ = _attn_probs(q_ref[...].reshape(rows, HEAD_DIM).astype(BF16), kk, sink_ref[0], mask)
        o = jnp.dot(p.astype(BF16), vv, preferred_element_type=F32)
        o_ref[...] = o.reshape(qpk, ATT_BLOCK, HEAD_DIM).astype(o_ref.dtype)

    return pl.pallas_call(
        body, grid=(N_KV_HEADS, nb), in_specs=[q_spec, prev, cur, prev, cur, sink_spec],
        out_specs=q_spec, out_shape=jax.ShapeDtypeStruct((hq, l, HEAD_DIM), BF16),
        name="attn_fwd", compiler_params=_params(2),
    )(q, k, k, v, v, sinks)


def _attn_bwd(q, k, v, sinks, do):
    hq, l, _ = q.shape
    qpk = hq // N_KV_HEADS
    nb = l // ATT_BLOCK
    blk = ATT_BLOCK
    rows = qpk * blk
    q_spec, cur, prev, sink_spec = _attn_specs(qpk)
    part_spec = pl.BlockSpec((1, 1, 2 * blk, HEAD_DIM), lambda h, n: (h, n, 0, 0))
    dsink_spec = pl.BlockSpec((qpk, 1, LANES), lambda h, n: (h, 0, 0))
    tn = (((0,), (0,)), ((), ()))

    def body(q_ref, do_ref, kp_ref, kc_ref, vp_ref, vc_ref, sink_ref, dq_ref, dkp_ref, dvp_ref, dsink_ref):
        n = pl.program_id(1)
        mask = _attn_mask(n, rows)
        kk = jnp.concatenate([kp_ref[0], kc_ref[0]], axis=0).astype(BF16)
        vv = jnp.concatenate([vp_ref[0], vc_ref[0]], axis=0).astype(BF16)
        qb = q_ref[...].reshape(rows, HEAD_DIM).astype(BF16)
        do32 = do_ref[...].astype(F32).reshape(rows, HEAD_DIM)
        dob = do32.astype(BF16)
        p, p_sink = _attn_probs(qb, kk, sink_ref[0], mask)
        pb = p.astype(BF16)
        o = jnp.dot(pb, vv, preferred_element_type=F32)
        delta = jnp.sum(do32 * o, axis=-1, keepdims=True)
        dp = lax.dot_general(dob, vv, (((1,), (1,)), ((), ())), preferred_element_type=F32)
        ds = (p * (dp - delta) * (HEAD_DIM ** -0.5)).astype(BF16)
        dq = jnp.dot(ds, kk, preferred_element_type=F32)
        dq_ref[...] = dq.reshape(qpk, blk, HEAD_DIM).astype(dq_ref.dtype)
        dkp_ref[0, 0] = lax.dot_general(ds, qb, tn, preferred_element_type=F32)
        dvp_ref[0, 0] = lax.dot_general(pb, dob, tn, preferred_element_type=F32)
        dsr = p_sink * delta
        for g in range(qpk):
            dsg = jnp.broadcast_to(-_colsum(dsr[g * blk:(g + 1) * blk]), (1, LANES))

            @pl.when(n == 0)
            def _():
                dsink_ref[g] = dsg

            @pl.when(n > 0)
            def _():
                dsink_ref[g] += dsg


    part_shape = jax.ShapeDtypeStruct((N_KV_HEADS, nb, 2 * blk, HEAD_DIM), F32)
    dq, dkp, dvp, dsink = pl.pallas_call(
        body, grid=(N_KV_HEADS, nb), in_specs=[q_spec, q_spec, prev, cur, prev, cur, sink_spec],
        out_specs=[q_spec, part_spec, part_spec, dsink_spec],
        out_shape=[jax.ShapeDtypeStruct((hq, l, HEAD_DIM), BF16), part_shape, part_shape,
                   jax.ShapeDtypeStruct((hq, 1, LANES), F32)],
        name="attn_bwd", compiler_params=_params(2),
    )(q, do, k, k, v, v, sinks)

    def combine(a_cur, a_nxt, b_cur, b_nxt):
        last = pl.program_id(1) == nb - 1
        keep = jnp.where(last, 0.0, 1.0)
        return (a_cur[0, 0, blk:] + keep * a_nxt[0, 0, :blk])[None], (b_cur[0, 0, blk:] + keep * b_nxt[0, 0, :blk])[None]

    nxt_spec = pl.BlockSpec((1, 1, 2 * blk, HEAD_DIM), lambda h, n: (h, jnp.minimum(n + 1, nb - 1), 0, 0))
    kv_shape = jax.ShapeDtypeStruct((N_KV_HEADS, l, HEAD_DIM), BF16)
    dk, dv = _tile_call("attn_dkv", combine, (N_KV_HEADS, nb), [dkp, dkp, dvp, dvp],
                        [part_spec, nxt_spec, part_spec, nxt_spec], [kv_shape, kv_shape], [cur, cur])
    return dq, dk, dv, dsink


def _block_diag(m):
    j, gl, a, b = m.shape
    eye = jnp.eye(gl, dtype=m.dtype)
    return (m[:, :, :, None, :] * eye[None, :, None, :, None]).reshape(j, gl * a, gl * b)


def _diag_blocks(z, a):
    j = z.shape[0]
    gl = z.shape[1] // a
    b = z.shape[2] // gl
    d = jnp.diagonal(z.reshape(j, gl, a, gl, b), axis1=1, axis2=3)
    return d.transpose(0, 3, 1, 2)


def _s5_permute(src_ref, dst_ref, t_len):
    seg = t_len // 8
    for k in range(seg):
        dst_ref[8 * k:8 * k + 8, :] = src_ref[pl.ds(k, 8, stride=seg), :]


def _s5_unpermute(perm_ref, t_len, emit):
    per_seg = t_len // 64
    for m in range(t_len // 8):
        emit(8 * m, perm_ref[pl.ds(64 * (m % per_seg) + m // per_seg, 8, stride=8), :])


def _s5_powers(p_ref, lr, li, seg):
    hs = TILE_STATES

    def step(k, carry):
        pr, pi = carry
        p_ref[pl.ds(k, 1), 0:hs] = pr
        p_ref[pl.ds(k, 1), hs:2 * hs] = pi
        return lr * pr - li * pi, lr * pi + li * pr

    lax.fori_loop(0, seg, step, (lr, li))


def _s5_local_scan(x_ref, base, lr, li, seg, reverse):
    hs = TILE_STATES
    lr8, li8 = jnp.broadcast_to(lr, (8, hs)), jnp.broadcast_to(li, (8, hs))
    if reverse:
        li8 = -li8

    def step(i, carry):
        hr, hi = carry
        k = seg - 1 - i if reverse else i
        rows = pl.ds(pl.multiple_of(base + 8 * k, 8), 8)
        nr = lr8 * hr - li8 * hi + x_ref[rows, 0:hs]
        ni = lr8 * hi + li8 * hr + x_ref[rows, hs:2 * hs]
        x_ref[rows, 0:hs] = nr
        x_ref[rows, hs:2 * hs] = ni
        return nr, ni

    zero = jnp.zeros((8, hs), F32)
    return lax.fori_loop(0, seg, step, (zero, zero), unroll=2)


def _s5_carries(c_ref, e_ref, ends, start, pw_r, pw_i, reverse):
    hs = TILE_STATES
    e_ref[:, 0:hs] = ends[0]
    e_ref[:, hs:2 * hs] = ends[1]
    cr, ci = start
    if reverse:
        pw_i = -pw_i
    for s in (range(7, -1, -1) if reverse else range(8)):
        c_ref[s:s + 1, 0:hs] = cr
        c_ref[s:s + 1, hs:2 * hs] = ci
        er, ei = e_ref[s:s + 1, 0:hs], e_ref[s:s + 1, hs:2 * hs]
        cr, ci = er + pw_r * cr - pw_i * ci, ei + pw_r * ci + pw_i * cr
    return cr, ci


def _s5_states(u_perm_b16, bd_ref, x_ref, base, c_ref, e_ref, p_ref, lr, li, h_in, t_len):
    hs = TILE_STATES
    seg = t_len // 8
    x_ref[pl.ds(base, t_len), :] = jnp.dot(u_perm_b16, bd_ref[0], preferred_element_type=F32)
    ends = _s5_local_scan(x_ref, base, lr, li, seg, False)
    pw_r, pw_i = p_ref[seg - 1:seg, 0:hs], p_ref[seg - 1:seg, hs:2 * hs]
    h_out = _s5_carries(c_ref, e_ref, ends, h_in, pw_r, pw_i, False)
    cr, ci = c_ref[:, 0:hs], c_ref[:, hs:2 * hs]

    def fix(k, carry):
        rows = pl.ds(pl.multiple_of(base + 8 * k, 8), 8)
        pr, pi = p_ref[pl.ds(k, 1), 0:hs], p_ref[pl.ds(k, 1), hs:2 * hs]
        x_ref[rows, 0:hs] += pr * cr - pi * ci
        x_ref[rows, hs:2 * hs] += pr * ci + pi * cr
        return carry

    lax.fori_loop(0, seg, fix, 0, unroll=2)
    return h_out


def _s5_fwd(proj, u_off, bd, cbd, lam, dvec, t_len):
    l = proj.shape[0]
    nj = bd.shape[0]
    nch = l // t_len
    hs = TILE_STATES
    ub = u_off // LANES
    seg = t_len // 8
    assert t_len % 64 == 0

    def body(u_ref, bd_ref, cbd_ref, lam_ref, d_ref, y_ref, hst_ref, x_ref, h_ref, p_ref, c_ref, e_ref, up_ref, yp_ref):
        lr, li = lam_ref[0, 0:1, :], lam_ref[0, 1:2, :]

        @pl.when(pl.program_id(1) == 0)
        def _():
            h_ref[...] = jnp.zeros_like(h_ref)
            _s5_powers(p_ref, lr, li, seg)

        hst_ref[0, 0] = h_ref[...]
        _s5_permute(u_ref, up_ref, t_len)
        h_out = _s5_states(up_ref[...].astype(BF16), bd_ref, x_ref, 0, c_ref, e_ref, p_ref, lr, li,
                           (h_ref[:, 0:hs], h_ref[:, hs:2 * hs]), t_len)
        h_ref[:, 0:hs] = h_out[0]
        h_ref[:, hs:2 * hs] = h_out[1]
        yp_ref[...] = jnp.dot(x_ref[...].astype(BF16), cbd_ref[0], preferred_element_type=F32)
        dv = d_ref[0]

        def out(r0, rows):
            y_ref[r0:r0 + 8, :] = rows + dv * u_ref[r0:r0 + 8, :]

        _s5_unpermute(yp_ref, t_len, out)

    return pl.pallas_call(
        body, grid=(nj, nch),
        in_specs=[pl.BlockSpec((t_len, LANES), lambda j, c: (c, ub + j)),
                  pl.BlockSpec((1, LANES, 2 * hs), lambda j, c: (j, 0, 0)),
                  pl.BlockSpec((1, 2 * hs, LANES), lambda j, c: (j, 0, 0)),
                  pl.BlockSpec((1, 2, hs), lambda j, c: (j, 0, 0)),
                  pl.BlockSpec((1, 1, LANES), lambda j, c: (j, 0, 0))],
        out_specs=[pl.BlockSpec((t_len, LANES), lambda j, c: (c, j)),
                   pl.BlockSpec((1, 1, 1, 2 * hs), lambda j, c: (j, c, 0, 0))],
        out_shape=[jax.ShapeDtypeStruct((l, nj * LANES), F32),
                   jax.ShapeDtypeStruct((nj, nch, 1, 2 * hs), F32)],
        scratch_shapes=[pltpu.VMEM((t_len, 2 * hs), F32), pltpu.VMEM((1, 2 * hs), F32),
                        pltpu.VMEM((seg, 2 * hs), F32), pltpu.VMEM((8, 2 * hs), F32), pltpu.VMEM((8, 2 * hs), F32),
                        pltpu.VMEM((t_len, LANES), F32), pltpu.VMEM((t_len, LANES), F32)],
        name="s5_fwd", compiler_params=_params(2),
    )(proj, bd, cbd, lam, dvec)


def _s5_bwd(proj, u_off, dy, hst, bd, bdt, cbdt, lam, dvec, t_len):
    l = proj.shape[0]
    nj = bd.shape[0]
    nch = l // t_len
    hs = TILE_STATES
    ub = u_off // LANES
    seg = t_len // 8
    tn = (((0,), (0,)), ((), ()))
    assert t_len % 64 == 0

    def body(u_ref, dy_ref, hst_ref, bd_ref, bdt_ref, cbdt_ref, lam_ref, d_ref,
             du_ref, dbd_ref, dcbdt_ref, dlam_ref, dd_ref,
             x_ref, g_ref, gc_ref, p_ref, c_ref, e_ref, up_ref, dyp_ref, dup_ref):
        first = pl.program_id(1) == 0
        lr, li = lam_ref[0, 0:1, :], lam_ref[0, 1:2, :]

        @pl.when(first)
        def _():
            gc_ref[...] = jnp.zeros_like(gc_ref)
            _s5_powers(p_ref, lr, li, seg)

        _s5_permute(u_ref, up_ref, t_len)
        _s5_permute(dy_ref, dyp_ref, t_len)
        ub16, dyb16 = up_ref[...].astype(BF16), dyp_ref[...].astype(BF16)
        h0 = hst_ref[0, 0]
        _s5_states(ub16, bd_ref, x_ref, 8, c_ref, e_ref, p_ref, lr, li, (h0[:, 0:hs], h0[:, hs:2 * hs]), t_len)
        x_ref[0:8, :] = c_ref[...]
        g_ref[...] = jnp.dot(dyb16, cbdt_ref[0], preferred_element_type=F32)
        starts = _s5_local_scan(g_ref, 0, lr, li, seg, True)
        pw_r, pw_i = p_ref[seg - 1:seg, 0:hs], p_ref[seg - 1:seg, hs:2 * hs]
        g_out = _s5_carries(c_ref, e_ref, starts, (gc_ref[:, 0:hs], gc_ref[:, hs:2 * hs]), pw_r, pw_i, True)
        gc_ref[:, 0:hs] = g_out[0]
        gc_ref[:, hs:2 * hs] = g_out[1]
        cr, ci = c_ref[:, 0:hs], c_ref[:, hs:2 * hs]

        def fix(k, carry):
            alr, ali = carry
            rows = pl.ds(pl.multiple_of(8 * k, 8), 8)
            pr, pi = p_ref[pl.ds(seg - 1 - k, 1), 0:hs], p_ref[pl.ds(seg - 1 - k, 1), hs:2 * hs]
            gr = g_ref[rows, 0:hs] + pr * cr + pi * ci
            gi = g_ref[rows, hs:2 * hs] + pr * ci - pi * cr
            g_ref[rows, 0:hs] = gr
            g_ref[rows, hs:2 * hs] = gi
            hpr, hpi = x_ref[rows, 0:hs], x_ref[rows, hs:2 * hs]
            return alr + gr * hpr + gi * hpi, ali + gi * hpr - gr * hpi

        zero = jnp.zeros((8, hs), F32)
        alr, ali = lax.fori_loop(0, seg, fix, (zero, zero), unroll=2)
        alr, ali = _colsum(alr), _colsum(ali)
        g = g_ref[...].astype(BF16)
        h = x_ref[pl.ds(8, t_len), :].astype(BF16)
        dup_ref[...] = jnp.dot(g, bdt_ref[0], preferred_element_type=F32)
        dv = d_ref[0]

        def out(r0, rows):
            du_ref[r0:r0 + 8, :] = (rows + dv * dy_ref[r0:r0 + 8, :]).astype(du_ref.dtype)

        _s5_unpermute(dup_ref, t_len, out)
        sign = jnp.where(lax.broadcasted_iota(jnp.int32, (1, 2 * hs), 1) < hs, 1.0, -1.0)
        dbd = lax.dot_general(ub16, g, tn, preferred_element_type=F32)
        dcbdt = lax.dot_general(dyb16, h, tn, preferred_element_type=F32) * sign
        ddv = _colsum(dy_ref[...] * u_ref[...])

        @pl.when(first)
        def _():
            dbd_ref[0] = dbd
            dcbdt_ref[0] = dcbdt
            dlam_ref[0, 0:1, :] = alr
            dlam_ref[0, 1:2, :] = ali
            dd_ref[0] = ddv

        @pl.when(jnp.logical_not(first))
        def _():
            dbd_ref[0] += dbd
            dcbdt_ref[0] += dcbdt
            dlam_ref[0, 0:1, :] += alr
            dlam_ref[0, 1:2, :] += ali
            dd_ref[0] += ddv

    rev = lambda c: nch - 1 - c
    wide = pl.BlockSpec((1, LANES, 2 * hs), lambda j, c: (j, 0, 0))
    tall = pl.BlockSpec((1, 2 * hs, LANES), lambda j, c: (j, 0, 0))
    return pl.pallas_call(
        body, grid=(nj, nch),
        in_specs=[pl.BlockSpec((t_len, LANES), lambda j, c: (rev(c), ub + j)),
                  pl.BlockSpec((t_len, LANES), lambda j, c: (rev(c), j)),
                  pl.BlockSpec((1, 1, 1, 2 * hs), lambda j, c: (j, rev(c), 0, 0)),
                  wide, tall, wide,
                  pl.BlockSpec((1, 2, hs), lambda j, c: (j, 0, 0)),
                  pl.BlockSpec((1, 1, LANES), lambda j, c: (j, 0, 0))],
        out_specs=[pl.BlockSpec((t_len, LANES), lambda j, c: (rev(c), j)),
                   wide, wide,
                   pl.BlockSpec((1, 2, hs), lambda j, c: (j, 0, 0)),
                   pl.BlockSpec((1, 1, LANES), lambda j, c: (j, 0, 0))],
        out_shape=[jax.ShapeDtypeStruct((l, nj * LANES), BF16),
                   jax.ShapeDtypeStruct((nj, LANES, 2 * hs), F32),
                   jax.ShapeDtypeStruct((nj, LANES, 2 * hs), F32),
                   jax.ShapeDtypeStruct((nj, 2, hs), F32),
                   jax.ShapeDtypeStruct((nj, 1, LANES), F32)],
        scratch_shapes=[pltpu.VMEM((t_len + 8, 2 * hs), F32), pltpu.VMEM((t_len, 2 * hs), F32),
                        pltpu.VMEM((1, 2 * hs), F32), pltpu.VMEM((seg, 2 * hs), F32),
                        pltpu.VMEM((8, 2 * hs), F32), pltpu.VMEM((8, 2 * hs), F32),
                        pltpu.VMEM((t_len, LANES), F32), pltpu.VMEM((t_len, LANES), F32),
                        pltpu.VMEM((t_len, LANES), F32)],
        name="s5_bwd", compiler_params=_params(2),
    )(proj, dy, hst, bd, bdt, cbdt, lam, dvec)


def _full_spec(shape):
    nd = len(shape)
    return pl.BlockSpec(tuple(shape), lambda i: (0,) * nd)


def _sds(shape, dtype=F32):
    return jax.ShapeDtypeStruct(tuple(shape), dtype)


def kernel(x, c, ada_w, ada_b, norm_mix_g, w_in, attn_sinks, w_attn_proj, ssm_a_re, ssm_a_im, ssm_log_dt, ssm_b_re, ssm_b_im, ssm_c_re, ssm_c_im, ssm_d, w_ssm_glu, w_out, norm_ffn_g, w_ffn_up, ffn_conv_w, ffn_conv_b, w_ffn_down, final_g, loss_target, m_ada_w, m_ada_b, m_norm_mix_g, m_w_in, m_attn_sinks, m_w_attn_proj, m_ssm_a_re, m_ssm_a_im, m_ssm_log_dt, m_ssm_b_re, m_ssm_b_im, m_ssm_c_re, m_ssm_c_im, m_ssm_d, m_w_ssm_glu, m_w_out, m_norm_ffn_g, m_w_ffn_up, m_ffn_conv_w, m_ffn_conv_b, m_w_ffn_down, m_final_g, v_ada_w, v_ada_b, v_norm_mix_g, v_w_in, v_attn_sinks, v_w_attn_proj, v_ssm_a_re, v_ssm_a_im, v_ssm_log_dt, v_ssm_b_re, v_ssm_b_im, v_ssm_c_re, v_ssm_c_im, v_ssm_d, v_w_ssm_glu, v_w_out, v_norm_ffn_g, v_w_ffn_up, v_ffn_conv_w, v_ffn_conv_b, v_w_ffn_down, v_final_g):
    given = dict(locals())
    names = ['ada_w', 'ada_b', 'norm_mix_g', 'w_in', 'attn_sinks', 'w_attn_proj', 'ssm_a_re', 'ssm_a_im',
             'ssm_log_dt', 'ssm_b_re', 'ssm_b_im', 'ssm_c_re', 'ssm_c_im', 'ssm_d', 'w_ssm_glu', 'w_out',
             'norm_ffn_g', 'w_ffn_up', 'ffn_conv_w', 'ffn_conv_b', 'w_ffn_down', 'final_g']

    xs = x[0]
    tgt = loss_target[0]
    l, d = xs.shape
    attn_w = w_attn_proj.shape[1]
    ssm_w = w_ssm_glu.shape[1]
    hq = attn_sinks.shape[1]
    qpk = hq // N_KV_HEADS
    kv_w = N_KV_HEADS * HEAD_DIM
    n_groups = ssm_a_re.shape[1]
    dff = ffn_conv_b.shape[1]
    in_w = attn_w + 2 * kv_w + ssm_w + 2 * d
    nj = ssm_w // LANES
    off_k, off_v, off_u = attn_w, attn_w + kv_w, attn_w + 2 * kv_w
    off_ga, off_gs = off_u + ssm_w, off_u + ssm_w + d
    assert hq * HEAD_DIM == attn_w and n_groups * SSM_P == ssm_w and l % ATT_BLOCK == 0

    xi, yi, ci = _dev()
    idx = 4 * xi + 2 * yi + ci

    row_sharded = {'w_out': (d, d), 'w_ffn_down': (dff, d)}
    big = ['w_in', 'w_attn_proj', 'w_ssm_glu', 'w_out', 'w_ffn_up', 'w_ffn_down']
    spack, s_offs = _pack([c, ffn_conv_w[0]], LANES, 8)
    w16 = {k: given[k][0].astype(BF16) for k in big}
    wg_in, sg = _all_gather("gather_first", [w16['w_in'], spack])
    mixer_w = ['w_attn_proj', 'w_ssm_glu', 'w_out']
    h_mixer, tok = _exchange_start("gather_mixer_start", [w16[k] for k in mixer_w], True, wg_in)
    h_up, tok = _exchange_start("gather_ffn_up_start", [w16['w_ffn_up']], True, tok)
    h_down, tok = _exchange_start("gather_ffn_down_start", [w16['w_ffn_down']], True, tok)
    full = {'w_in': wg_in.transpose(1, 0, 2).reshape(d, in_w)}
    c_all = _unpack(sg, s_offs[0], (d,), lead=(N_DEV,))
    conv_w = _unpack(sg, s_offs[1], ffn_conv_w.shape[1:], lead=(N_DEV,)).transpose(1, 0, 2).reshape(3, dff)
    conv_b = ffn_conv_b

    mod_n = ada_w.shape[2]
    tcm = _pick(mod_n, 512)
    ada_b_mine = lax.dynamic_slice_in_dim(ada_b, idx * mod_n, mod_n, axis=1)

    def modpart_fn(cv, wv, bv):
        cond = cv * jax.nn.sigmoid(cv)
        return jnp.dot(cond.astype(BF16), wv.astype(BF16), preferred_element_type=F32) + bv, cond

    modp, cond_all = _tile_call(
        "ada_rows", modpart_fn, (mod_n // tcm,), [c_all, ada_w[0], ada_b_mine],
        [pl.BlockSpec((N_DEV, d), lambda j: (0, 0)), pl.BlockSpec((d, tcm), lambda j: (0, j)),
         pl.BlockSpec((1, tcm), lambda j: (0, j))],
        [_sds((N_DEV, mod_n)), _sds((N_DEV, d))],
        [pl.BlockSpec((N_DEV, tcm), lambda j: (0, j)), pl.BlockSpec((N_DEV, d), lambda j: (0, 0))])
    (modg,) = _all_gather("gather_ada_rows", [modp])
    mod = lax.dynamic_index_in_dim(modg, idx, axis=1, keepdims=False).reshape(1, N_DEV * mod_n)
    sh1, sc1, g1, sh2, sc2, g2 = [mod[:, i * d:(i + 1) * d] for i in range(6)]

    tr = _pick(l, 256, 8)
    trh = _pick(l, 128, 8)
    nr, nrh = l // tr, l // trh
    g_mix, g_ffn, g_fin = norm_mix_g + tok[0:1, 0:1], norm_ffn_g, final_g.reshape(1, d)

    def with_t(fn):
        def wrapped(*vals):
            out = fn(*vals)
            out = out if isinstance(out, tuple) else (out,)
            return out + (out[-1].T,)
        return wrapped

    h1, h1_t = _tile_call("norm_mod_mix", with_t(_norm_mod), (1, nr), [xs, g_mix, sc1, sh1],
                          [_t(tr, d), _v(d), _v(d), _v(d)], [_sds((l, d), BF16), _sds((d, l), BF16)],
                          [_t(tr, d), _tt(tr, d)])
    proj = _matmul("proj_in", h1, full['w_in'], "nn", tn=1280)

    def heads(z, n):
        return z.reshape(l, n, HEAD_DIM).transpose(1, 0, 2)

    qh = heads(proj[:, :attn_w], hq)
    kh = heads(proj[:, off_k:off_k + kv_w], N_KV_HEADS)
    vh = heads(proj[:, off_v:off_v + kv_w], N_KV_HEADS)
    sinks3 = jnp.repeat(attn_sinks.reshape(N_KV_HEADS, qpk), ATT_BLOCK, axis=1)[..., None]
    o_h = _attn_fwd(qh, kh, vh, sinks3)
    o2 = o_h.transpose(1, 0, 2).reshape(l, attn_w)

    gn = (n_groups, SSM_N)
    pgn = (SSM_P, n_groups, SSM_N)
    a_re, a_im, log_dt = ssm_a_re[0], ssm_a_im[0], ssm_log_dt[0].reshape(n_groups, 1)
    b_re, b_im = ssm_b_re[0].transpose(2, 0, 1), ssm_b_im[0].transpose(2, 0, 1)
    disc_ins = [a_re, a_im, log_dt, b_re, b_im]
    disc_specs = [_full_spec(gn), _full_spec(gn), _full_spec((n_groups, 1)), _full_spec(pgn), _full_spec(pgn)]
    lam_r, lam_i, bb_r, bb_i = _tile_call(
        "s5_discretise", _s5_disc_fn, (1,), disc_ins, disc_specs,
        [_sds(gn), _sds(gn), _sds(pgn), _sds(pgn)],
        [_full_spec(gn), _full_spec(gn), _full_spec(pgn), _full_spec(pgn)])

    def tiles_gpn(z):
        return z.reshape(SSM_P, nj, TILE_GROUPS, SSM_N).transpose(1, 2, 0, 3)

    bd = jnp.concatenate([_block_diag(tiles_gpn(bb_r)), _block_diag(tiles_gpn(bb_i))], axis=2).astype(BF16)
    c_r = ssm_c_re[0].reshape(nj, TILE_GROUPS, SSM_P, SSM_N).transpose(0, 1, 3, 2)
    c_i = (-ssm_c_im[0]).reshape(nj, TILE_GROUPS, SSM_P, SSM_N).transpose(0, 1, 3, 2)
    cbd = jnp.concatenate([_block_diag(c_r), _block_diag(c_i)], axis=1).astype(BF16)
    bdt, cbdt = bd.transpose(0, 2, 1), cbd.transpose(0, 2, 1)
    lam = jnp.stack([lam_r.reshape(nj, TILE_STATES), lam_i.reshape(nj, TILE_STATES)], axis=1)
    dvec = ssm_d[0].reshape(nj, 1, LANES)
    t_len = _pick(l, 512, 8)
    y, hst = _s5_fwd(proj, off_u, bd, cbd, lam, dvec, t_len)

    tcs, trg = _pick(ssm_w, 1024), _pick(l, 512, 8)
    gy = _tile_call("gelu", lambda v: jax.nn.gelu(v), (ssm_w // tcs, l // trg), [y], [_t(trg, tcs)],
                    [_sds((l, ssm_w), BF16)], [_t(trg, tcs)])[0]
    full.update(zip(mixer_w, _exchange_wait("gather_mixer_wait", h_mixer, gy)))
    full['w_out'] = full['w_out'].reshape(row_sharded['w_out'])
    attn = _matmul("attn_proj", o2, full['w_attn_proj'], "nn")
    glu = _matmul("ssm_glu", gy, full['w_ssm_glu'], "nn")

    tcd = 256 if d % 256 == 0 and off_ga % 256 == 0 else LANES
    assert d % tcd == 0 and off_ga % tcd == 0 and off_gs % tcd == 0
    trm = _pick(l, 1024, 8)
    mix_in_specs = [_t(trm, tcd), _t(trm, tcd, d), _t(trm, tcd), _t(trm, tcd, off_ga), _t(trm, tcd, off_gs)]
    mixed = _tile_call("gate_mix", _mix_fn, (d // tcd, l // trm), [glu, glu, attn, proj, proj], mix_in_specs,
                       [_sds((l, d), BF16)], [_t(trm, tcd)])[0]
    mixout = _matmul("mix_out", mixed, full['w_out'], "nn")

    def res_norm_fn(xv, mo, g1v, gv, scv, shv):
        x2v = xv + g1v * mo
        return x2v, _norm_mod(x2v, gv, scv, shv)

    x2, h2, h2_t = _tile_call("residual_norm_mod_ffn", with_t(res_norm_fn), (1, nr), [xs, mixout, g1, g_ffn, sc2, sh2],
                              [_t(tr, d), _t(tr, d), _v(d), _v(d), _v(d), _v(d)],
                              [_sds((l, d)), _sds((l, d), BF16), _sds((d, l), BF16)],
                              [_t(tr, d), _t(tr, d), _tt(tr, d)])
    full['w_ffn_up'], = _exchange_wait("gather_ffn_up_wait", h_up, h2)
    up = _matmul("ffn_up", h2, full['w_ffn_up'], "nn", tn=1408)

    tcf, trc = _pick(dff, 1408), _pick(l, 512, 8)
    assert dff % tcf == 0
    ncf = dff // tcf

    taps = [conv_w[i:i + 1] for i in range(3)]

    def conv_gate(gp, gp_prev, w0, w1, w2, bv):
        prev = jnp.where(pl.program_id(1) == 0, 0.0, 1.0) * gp_prev
        ext = jnp.concatenate([prev, gp], axis=0)
        m1 = pltpu.roll(ext, 1, 0)[8:]
        m2 = pltpu.roll(ext, 2, 0)[8:]
        return w0 * m2 + w1 * m1 + w2 * gp + bv, m1, m2

    def convglu_fn(gp, gp_prev, val, w0, w1, w2, bv):
        gate, _, _ = conv_gate(gp, gp_prev, w0, w1, w2, bv)
        return gate * jax.nn.sigmoid(gate) * val

    act, act_t = _tile_call("conv_swiglu", with_t(convglu_fn), (ncf, l // trc), [up, up, up] + taps + [conv_b],
                            [_t(trc, tcf), _prev8(trc, tcf), _t(trc, tcf, dff)] + [_v(tcf)] * 4,
                            [_sds((l, dff), BF16), _sds((dff, l), BF16)], [_t(trc, tcf), _tt(trc, tcf)])
    full['w_ffn_down'] = _exchange_wait("gather_ffn_down_wait", h_down, act)[0].reshape(row_sharded['w_ffn_down'])
    ffn = _matmul("ffn_down", act, full['w_ffn_down'], "nn", tm=512)

    def final_fn(x2v, fv, g2v, gv, tv):
        rows = x2v.shape[0]

        def loss_of(x2a, fa, g2a, ga):
            out = _rms(x2a + g2a * fa, ga)
            err = out - tv
            return 0.5 * _colsum(jnp.mean(err * err, axis=-1, keepdims=True))

        loss, vjp = jax.vjp(loss_of, x2v, fv, _bc(g2v, rows), _bc(gv, rows))
        dx3, dffn, dg2, dgf = vjp(jnp.ones((1, 1), F32))
        return jnp.broadcast_to(loss, (1, LANES)), dx3, dffn, _colsum(dg2), _colsum(dgf)

    loss_p, dx3, dffn, dg2, dg_fin = _tile_call(
        "loss_final_norm", final_fn, (1, nrh), [x2, ffn, g2, g_fin, tgt],
        [_t(trh, d), _t(trh, d), _v(d), _v(d), _t(trh, d)],
        [_sds((1, LANES)), _sds((l, d)), _sds((l, d), BF16), _sds((1, d)), _sds((1, d))],
        [_v(LANES), _t(trh, d), _t(trh, d), _v(d), _v(d)], acc=(0, 3, 4))
    loss = lax.psum(loss_p[0, 0], ("x", "y", "c"))

    dact = _matmul("d_act", dffn, full['w_ffn_down'], "nt", tn=1408)
    gd, gd16, pending = {}, {}, []
    dw_down, dw_down16 = _matmul("dw_ffn_down", act_t, dffn, "nn", tm=512, also_bf16=True)
    gd['w_ffn_down'], gd16['w_ffn_down'] = [z.reshape((N_DEV,) + w_ffn_down.shape[1:]) for z in (dw_down, dw_down16)]
    handle, tok = _exchange_start("grad_ffn_down_start", [gd16['w_ffn_down']], False, dw_down)
    pending.append((['w_ffn_down'], handle))
    conv_b_bwd = conv_b + tok[0:1, 0:1]

    def convglu_bwd_fn(gp, gp_prev, gp_next, val, val_next, da, da_next, w0, w1, w2, bv):
        rows = gp.shape[0]
        i = pl.program_id(1)
        prev = jnp.where(i == 0, 0.0, 1.0) * gp_prev
        more = jnp.where(i == pl.num_programs(1) - 1, 0.0, 1.0)
        ext = jnp.concatenate([prev, gp, gp_next], axis=0)
        cur = ext[8:]
        m1 = pltpu.roll(ext, 1, 0)[8:]
        m2 = pltpu.roll(ext, 2, 0)[8:]
        gate = w0 * m2 + w1 * m1 + w2 * cur + bv
        sg = jax.nn.sigmoid(gate)
        val_e = jnp.concatenate([val, val_next], axis=0)
        da_e = jnp.concatenate([da, more * da_next], axis=0)
        dgate = da_e * val_e * (sg * (1.0 + gate * (1.0 - sg)))
        p1 = pltpu.roll(dgate, rows + 8 - 1, 0)[:rows]
        p2 = pltpu.roll(dgate, rows + 8 - 2, 0)[:rows]
        dg = dgate[:rows]
        dgp = w2 * dg + w1 * p1 + w0 * p2
        dval = da * (gate[:rows] * sg[:rows])
        return (jnp.stack([dgp, dval], axis=0), _colsum(dg), _colsum(dg * m2[:rows]), _colsum(dg * m1[:rows]),
                _colsum(dg * gp))

    dup, dconv_b, dcw0, dcw1, dcw2 = _tile_call(
        "conv_swiglu_bwd", convglu_bwd_fn, (ncf, nr), [up, up, up, up, up, dact, dact] + taps + [conv_b_bwd],
        [_t(tr, tcf), _prev8(tr, tcf), _next8(tr, tcf, l), _t(tr, tcf, dff), _next8(tr, tcf, l, dff),
         _t(tr, tcf), _next8(tr, tcf, l)] + [_v(tcf)] * 4,
        [_sds((2, l, dff), BF16)] + [_sds((1, dff))] * 4, [_st(tr, tcf)] + [_v(tcf)] * 4, acc=(1, 2, 3, 4))
    dh2 = _matmul("d_h2", dup, full['w_ffn_up'], "nt")
    gd['w_ffn_up'], gd16['w_ffn_up'] = _matmul("dw_ffn_up", h2_t, dup, "nn", tm=512, tn=1408, out_stack=N_DEV, also_bf16=True)
    handle, tok = _exchange_start("grad_ffn_up_start", [gd16['w_ffn_up']], False, gd['w_ffn_up'])
    pending.append((['w_ffn_up'], handle))
    g_ffn_bwd = g_ffn + tok[0:1, 0:1]

    def res_norm_bwd_fn(xv, mo, g1v, gv, scv, shv, dhv, dxv):
        rows = xv.shape[0]
        _, vjp = jax.vjp(res_norm_fn, xv, mo, _bc(g1v, rows), _bc(gv, rows), _bc(scv, rows), _bc(shv, rows))
        dx, dmo, dg1v, dgv, dscv, dshv = vjp((dxv, dhv))
        return dx, dmo, _colsum(dg1v), _colsum(dgv), _colsum(dscv), _colsum(dshv)

    dx2, dmixout, dg1, dg_ffn, dsc2, dsh2 = _tile_call(
        "residual_norm_mod_ffn_bwd", res_norm_bwd_fn, (1, nrh), [xs, mixout, g1, g_ffn_bwd, sc2, sh2, dh2, dx3],
        [_t(trh, d), _t(trh, d), _v(d), _v(d), _v(d), _v(d), _t(trh, d), _t(trh, d)],
        [_sds((l, d)), _sds((l, d), BF16)] + [_sds((1, d))] * 4,
        [_t(trh, d), _t(trh, d)] + [_v(d)] * 4, acc=(2, 3, 4, 5))

    dmixed = _matmul("d_mixed", dmixout, full['w_out'], "nt")
    dw_out, dw_out16 = _matmul("dw_out", mixed, dmixout, "tn", also_bf16=True)
    gd['w_out'], gd16['w_out'] = [z.reshape((N_DEV,) + w_out.shape[1:]) for z in (dw_out, dw_out16)]

    def mix_bwd_fn(ga_, gb_, at, pa, ps, dm):
        _, vjp = jax.vjp(_mix_fn, ga_, gb_, at, pa, ps)
        da, db, dat, dpa, dps = vjp(dm)
        return jnp.stack([da, db], axis=0), dat, dpa, dps

    dglu, dattn, dga, dgs = _tile_call(
        "gate_mix_bwd", mix_bwd_fn, (d // tcd, l // trm), [glu, glu, attn, proj, proj, dmixed],
        mix_in_specs + [_t(trm, tcd)],
        [_sds((2, l, d), BF16)] + [_sds((l, d), BF16)] * 3, [_st(trm, tcd)] + [_t(trm, tcd)] * 3)

    dgy = _matmul("d_gelu_y", dglu, full['w_ssm_glu'], "nt")
    gd['w_ssm_glu'], gd16['w_ssm_glu'] = _matmul("dw_ssm_glu", gy, dglu, "tn", out_stack=N_DEV, also_bf16=True)

    def gelu_bwd_fn(yv, dv):
        _, vjp = jax.vjp(lambda z: jax.nn.gelu(z), yv)
        return vjp(dv)[0]

    dy = _tile_call("gelu_bwd", gelu_bwd_fn, (ssm_w // tcs, l // trg), [y, dgy], [_t(trg, tcs), _t(trg, tcs)],
                    [_sds((l, ssm_w))], [_t(trg, tcs)])[0]
    du, dbd, dcbdt, dlam, dd_tiles = _s5_bwd(proj, off_u, dy, hst, bd, bdt, cbdt, lam, dvec, t_len)

    def gpn_of(z):
        return z.transpose(2, 0, 1, 3).reshape(pgn)

    dbb_r = gpn_of(_diag_blocks(dbd[:, :, :TILE_STATES], SSM_P))
    dbb_i = gpn_of(_diag_blocks(dbd[:, :, TILE_STATES:], SSM_P))
    dc_re = _diag_blocks(dcbdt[:, :, :TILE_STATES], SSM_P).reshape(n_groups, SSM_P, SSM_N)
    dc_im = _diag_blocks(dcbdt[:, :, TILE_STATES:], SSM_P).reshape(n_groups, SSM_P, SSM_N)
    dlam_r, dlam_i = dlam[:, 0].reshape(gn), dlam[:, 1].reshape(gn)

    def disc_bwd_fn(ar, ai, ld, br, bi, dlr, dli, dbr, dbi):
        _, vjp = jax.vjp(_s5_disc_fn, ar, ai, ld, br, bi)
        return vjp((dlr, dli, dbr, dbi))

    da_re, da_im, dlog_dt, db_re, db_im = _tile_call(
        "s5_discretise_bwd", disc_bwd_fn, (1,), disc_ins + [dlam_r, dlam_i, dbb_r, dbb_i],
        disc_specs + [_full_spec(gn), _full_spec(gn), _full_spec(pgn), _full_spec(pgn)],
        [_sds(gn), _sds(gn), _sds((n_groups, 1)), _sds(pgn), _sds(pgn)], disc_specs)

    do2 = _matmul("d_attn_heads", dattn, full['w_attn_proj'], "nt")
    gd['w_attn_proj'], gd16['w_attn_proj'] = _matmul("dw_attn_proj", o2, dattn, "tn", out_stack=N_DEV, also_bf16=True)
    handle, tok = _exchange_start("grad_mixer_start", [gd16[k] for k in mixer_w], False, gd['w_attn_proj'])
    pending.append((mixer_w, handle))
    do_h = heads(do2.astype(BF16), hq)
    dq_h, dk_h, dv_h, dsink = _attn_bwd(qh, kh, vh, sinks3 + tok[0:1, 0:1], do_h)

    def unheads(z):
        return z.transpose(1, 0, 2).reshape(l, z.shape[0] * HEAD_DIM)

    dproj = jnp.concatenate([unheads(dq_h), unheads(dk_h), unheads(dv_h), du, dga, dgs], axis=1)
    dw_in, dw_in16 = _matmul("dw_in", h1_t, dproj, "nn", tm=512, tn=1280, also_bf16=True)
    dcw = jnp.concatenate([dcw0, dcw1, dcw2], axis=0)
    shard_in, shard_cw = w_in.shape[1:], ffn_conv_w.shape[1:]
    gd16['w_in'] = dw_in16.reshape(shard_in[0], N_DEV, shard_in[1]).transpose(1, 0, 2)
    own_in = lax.dynamic_slice_in_dim(dw_in, idx * shard_in[1], shard_in[1], axis=1)[None]
    gd['ffn_conv_w'] = dcw.reshape(shard_cw[0], N_DEV, shard_cw[1]).transpose(1, 0, 2)
    gd16['ffn_conv_w'] = gd['ffn_conv_w'].astype(BF16)
    handle, tok = _exchange_start("grad_in_start", [gd16['w_in'], gd16['ffn_conv_w']], False, dw_in)
    pending.append((['w_in', 'ffn_conv_w'], handle))
    dh1 = _matmul("d_h1", dproj, full['w_in'], "nt", tm=512, dep=tok)

    def norm_bwd_fn(xv, gv, scv, shv, dhv, dxv):
        rows = xv.shape[0]
        _, vjp = jax.vjp(_norm_mod, xv, _bc(gv, rows), _bc(scv, rows), _bc(shv, rows))
        dx, dgv, dscv, dshv = vjp(dhv)
        return dx + dxv, _colsum(dgv), _colsum(dscv), _colsum(dshv)

    grad_x, dg_mix, dsc1, dsh1 = _tile_call(
        "norm_mod_mix_bwd", norm_bwd_fn, (1, nrh), [xs, g_mix, sc1, sh1, dh1, dx2],
        [_t(trh, d), _v(d), _v(d), _v(d), _t(trh, d), _t(trh, d)],
        [_sds((l, d))] + [_sds((1, d))] * 3, [_t(trh, d)] + [_v(d)] * 3, acc=(1, 2, 3))

    dmod = jnp.concatenate([dsh1, dsc1, dg1, dsh2, dsc2, dg2], axis=1)
    small = ['ada_b', 'norm_mix_g', 'attn_sinks', 'ssm_a_re', 'ssm_a_im', 'ssm_log_dt', 'ssm_b_re', 'ssm_b_im',
             'ssm_c_re', 'ssm_c_im', 'ssm_d', 'norm_ffn_g', 'ffn_conv_b', 'final_g']
    small_grads = {
        'ada_b': dmod, 'norm_mix_g': dg_mix, 'attn_sinks': dsink[:, 0, 0], 'ssm_a_re': da_re, 'ssm_a_im': da_im,
        'ssm_log_dt': dlog_dt, 'ssm_b_re': db_re.transpose(1, 2, 0), 'ssm_b_im': db_im.transpose(1, 2, 0),
        'ssm_c_re': dc_re, 'ssm_c_im': dc_im, 'ssm_d': dd_tiles, 'norm_ffn_g': dg_ffn, 'ffn_conv_b': dconv_b,
        'final_g': dg_fin}
    gs_pack, sm_offs = _pack([small_grads[k] for k in small], LANES, 8)
    (gs_all,) = _all_gather("gather_small_grads", [gs_pack])
    ws_pack, _ = _pack([given[k] for k in small], LANES, 8)
    ms_pack, _ = _pack([given['m_' + k] for k in small], LANES, 8)
    vs_pack, _ = _pack([given['v_' + k] for k in small], LANES, 8)
    small_out = _adamw("adamw_replicated", gs_all, ws_pack, ms_pack, vs_pack)

    dmod_all = _unpack(gs_all, sm_offs[0], (N_DEV * mod_n,), lead=(N_DEV,))
    dmod_mine = lax.dynamic_slice_in_dim(dmod_all, idx * mod_n, mod_n, axis=1)
    kpad = LANES - N_DEV
    cond_t = jnp.pad(cond_all.T, ((0, 0), (0, kpad)))
    dmod_pad = jnp.pad(dmod_mine, ((0, kpad), (0, 0)))
    g_ada_w = _matmul("dw_ada", cond_t, dmod_pad, "nn")
    ada_out = _adamw("adamw_ada_w", g_ada_w[None], ada_w[0], m_ada_w[0], v_ada_w[0])

    sharded = big + ['ffn_conv_w']
    sharded_out = {}
    for group, handle in pending:
        for k, parts in zip(group, _exchange_wait("grad_" + group[0] + "_wait", handle, ada_out[0])):
            own_src, own_at = (own_in, 0 * idx) if k == 'w_in' else (gd[k], idx)
            sharded_out[k] = _adamw_sharded("adamw_" + k, parts, own_src, given[k][0], given['m_' + k][0],
                                            given['v_' + k][0], jnp.stack([idx, own_at]).astype(jnp.int32))

    results = [{}, {}, {}, {}]
    for which in range(4):
        for k, off in zip(small, sm_offs):
            results[which][k] = _unpack(small_out[which], off, given[k].shape)
        for k in sharded:
            results[which][k] = sharded_out[k][which][None]
        results[which]['ada_w'] = ada_out[which][None]
    outs = [loss, grad_x[None]]
    for which in range(4):
        outs += [results[which][k] for k in names]
    return tuple(outs)
```

```python
import functools
import math

import jax
import jax.numpy as jnp
from jax import lax
from jax.experimental import pallas as pl
from jax.experimental.pallas import tpu as pltpu

F32, BF16 = jnp.float32, jnp.bfloat16
MESH = pl.DeviceIdType.MESH
N_DEV = 8

HEAD_DIM = 64
N_KV_HEADS = 2
ATT_BLOCK = 128
NEG_INF = -1e30
SSM_P = 16
SSM_N = 64
LANES = 128
TILE_GROUPS = LANES // SSM_P
TILE_STATES = TILE_GROUPS * SSM_N
RMS_EPS = 1e-6
ADAM_LR, ADAM_B1, ADAM_B2, ADAM_EPS, ADAM_WD, ADAM_STEP = 0.001, 0.9, 0.999, 1e-08, 0.01, 10
VMEM_LIMIT = 56 * 1024 * 1024
MATMUL_VMEM_BUDGET = 44 * 1024 * 1024


def _params(n_axes):
    return pltpu.CompilerParams(dimension_semantics=("arbitrary",) * n_axes, vmem_limit_bytes=VMEM_LIMIT)


def _pick(dim, pref, align=128):
    if dim <= align:
        return dim
    t = (min(pref, dim) // align) * align
    while t > align and dim % t:
        t -= align
    assert dim % t == 0, (dim, pref, align)
    return t


def _dev():
    return lax.axis_index("x"), lax.axis_index("y"), lax.axis_index("c")


def _tile_call(name, fn, grid, ins, in_specs, out_shapes, out_specs, acc=()):
    n_in, n_out = len(ins), len(out_shapes)
    acc_axis = len(grid) - 1

    def body(*refs):
        vals = fn(*[r[...] for r in refs[:n_in]])
        if not isinstance(vals, (tuple, list)):
            vals = (vals,)
        assert len(vals) == n_out
        for i, (r, v) in enumerate(zip(refs[n_in:], vals)):
            v = v.astype(r.dtype)
            if i in acc:
                first = pl.program_id(acc_axis) == 0

                @pl.when(first)
                def _():
                    r[...] = v

                @pl.when(jnp.logical_not(first))
                def _():
                    r[...] += v
            else:
                r[...] = v

    return pl.pallas_call(
        body, grid=grid, in_specs=in_specs, out_specs=out_specs, out_shape=out_shapes, name=name,
        compiler_params=_params(len(grid)),
    )(*ins)


def _t(tr, tc, off=0):
    return pl.BlockSpec((tr, tc), lambda j, i: (i, j + off // tc))


def _tt(tr, tc):
    return pl.BlockSpec((tc, tr), lambda j, i: (j, i))


def _v(tc, off=0, rows=1):
    return pl.BlockSpec((rows, tc), lambda j, i: (0, j + off // tc))


HALO = 16


def _prev_rows(tr, tc, off=0):
    return pl.BlockSpec((HALO, tc), lambda j, i: (jnp.maximum(i * (tr // HALO) - 1, 0), j + off // tc))


def _next_rows(tr, tc, nrows, off=0):
    return pl.BlockSpec((HALO, tc),
                        lambda j, i: (jnp.minimum((i + 1) * (tr // HALO), nrows // HALO - 1), j + off // tc))


def _st(tr, tc):
    return pl.BlockSpec((2, tr, tc), lambda j, i: (0, i, j))


def _bc(v, rows):
    return jnp.broadcast_to(v, (rows, v.shape[-1]))


def _colsum(v):
    return jnp.sum(v, axis=0, keepdims=True)


def _matmul(name, a, b, mode, out_dtype=F32, tm=1024, tn=1024, tk=None, out_stack=None, also_bf16=False, dep=None):
    def dims(z):
        return (z.shape[-2], z.shape[-1] * (z.shape[0] if z.ndim == 3 else 1))

    ar, ac = dims(a)
    br, bc = dims(b)
    if mode == "nn":
        m, k, n = ar, ac, bc
        assert br == k
    elif mode == "nt":
        m, k, n = ar, ac, br
        assert bc == k
    else:
        m, k, n = ac, ar, bc
        assert br == k
    m_lim, k_lim, n_lim = [m], [k], [n]
    if a.ndim == 3:
        (m_lim if mode == "tn" else k_lim).append(a.shape[-1])
    if b.ndim == 3:
        (k_lim if mode == "nt" else n_lim).append(b.shape[-1])
    if out_stack:
        n_lim.append(n // out_stack)
    tm = _pick(functools.reduce(math.gcd, m_lim), tm)
    tn = _pick(functools.reduce(math.gcd, n_lim), tn)
    k_unit = functools.reduce(math.gcd, k_lim)
    if tk is None:
        sa, sb, so = a.dtype.itemsize, b.dtype.itemsize, jnp.dtype(out_dtype).itemsize + (2 if also_bf16 else 0)
        fits = [t for t in range(LANES, k_unit + 1, LANES) if k_unit % t == 0 and
                2 * t * (tm * sa + tn * sb) + tm * tn * (2 * so + (4 if t < k else 0)) <= MATMUL_VMEM_BUDGET]
        tk = max(fits) if fits else _pick(k_unit, 512)
    else:
        tk = _pick(k_unit, tk)
    nk = k // tk

    def spec(z, brows, bcols, ridx, cidx):
        if z.ndim == 3:
            per = z.shape[-1] // bcols
            return pl.BlockSpec((None, brows, bcols),
                                lambda i, j, kk: (cidx(i, j, kk) // per, ridx(i, j, kk), cidx(i, j, kk) % per))
        return pl.BlockSpec((brows, bcols), lambda i, j, kk: (ridx(i, j, kk), cidx(i, j, kk)))

    gi = lambda i, j, kk: i
    gj = lambda i, j, kk: j
    gk = lambda i, j, kk: kk
    if mode == "nn":
        a_spec, b_spec = spec(a, tm, tk, gi, gk), spec(b, tk, tn, gk, gj)
        dn = (((1,), (0,)), ((), ()))
    elif mode == "nt":
        a_spec, b_spec = spec(a, tm, tk, gi, gk), spec(b, tn, tk, gj, gk)
        dn = (((1,), (1,)), ((), ()))
    else:
        a_spec, b_spec = spec(a, tk, tm, gk, gi), spec(b, tk, tn, gk, gj)
        dn = (((0,), (0,)), ((), ()))

    n_out = 2 if also_bf16 else 1

    deps = [] if dep is None else [dep]

    def body(a_ref, b_ref, *rest):
        rest = rest[len(deps):]
        o_refs, acc = rest[:n_out], rest[n_out:]
        part = lax.dot_general(a_ref[...].astype(BF16), b_ref[...].astype(BF16), dn, preferred_element_type=F32)

        def emit(val):
            for o_ref in o_refs:
                o_ref[...] = val.astype(o_ref.dtype)

        if nk == 1:
            emit(part)
            return
        acc_ref, = acc
        kk = pl.program_id(2)

        @pl.when(kk == 0)
        def _():
            acc_ref[...] = part

        @pl.when(kk > 0)
        def _():
            acc_ref[...] += part

        @pl.when(kk == nk - 1)
        def _():
            emit(acc_ref[...])

    if out_stack:
        per = (n // out_stack) // tn
        out_spec = pl.BlockSpec((None, tm, tn), lambda i, j, kk: (j // per, i, j % per))
        shape = (out_stack, m, n // out_stack)
    else:
        out_spec = pl.BlockSpec((tm, tn), lambda i, j, kk: (i, j))
        shape = (m, n)
    dtypes = [out_dtype, BF16][:n_out]
    res = pl.pallas_call(
        body, grid=(m // tm, n // tn, nk),
        in_specs=[a_spec, b_spec] + [pl.BlockSpec(memory_space=pl.ANY)] * len(deps), out_specs=[out_spec] * n_out,
        out_shape=[jax.ShapeDtypeStruct(shape, dt) for dt in dtypes],
        scratch_shapes=[pltpu.VMEM((tm, tn), F32)] if nk > 1 else [], name=name, compiler_params=_params(3),
    )(a, b, *deps)
    return res if also_bf16 else res[0]


def _all_gather(name, arrs, dep=None):
    n = len(arrs)
    deps = [] if dep is None else [dep]

    def body(*refs):
        ins, outs = refs[:n], refs[n + len(deps):2 * n + len(deps)]
        send_sems, recv_sems, local_sems = refs[2 * n + len(deps):]
        x, y, c = _dev()
        me, sib = (x, y, c), (x, y, 1 - c)
        chips = [(1 - x, y), (x, 1 - y), (1 - x, 1 - y)]

        def slot(p):
            return 4 * p[0] + 2 * p[1] + p[2]

        def copy(a, k, block, to, src=None):
            dst = outs[a].at[slot(block)]
            return pltpu.make_async_remote_copy(
                src_ref=dst if src is None else src, dst_ref=dst,
                send_sem=send_sems.at[7 * a + k], recv_sem=recv_sems.at[7 * a + k],
                device_id=to, device_id_type=MESH)

        mine = [pltpu.make_async_copy(ins[a], outs[a].at[slot(me)], local_sems.at[a]) for a in range(n)]
        for cp in mine:
            cp.start()
        first = []
        for a in range(n):
            first.append(copy(a, 0, me, sib, src=ins[a]))
            first += [copy(a, 1 + j, me, (*chip, c), src=ins[a]) for j, chip in enumerate(chips)]
        for cp in first:
            cp.start()
        passed = []
        for j, chip in enumerate(chips):
            for a in range(n):
                copy(a, 1 + j, (*chip, c), me).wait_recv()
                cp = copy(a, 4 + j, (*chip, c), sib)
                cp.start()
                passed.append(cp)
        for a in range(n):
            copy(a, 0, sib, me).wait_recv()
            for j, chip in enumerate(chips):
                copy(a, 4 + j, (*chip, 1 - c), me).wait_recv()
        for cp in first + passed:
            cp.wait_send()
        for cp in mine:
            cp.wait()

    any_spec = pl.BlockSpec(memory_space=pl.ANY)
    return pl.pallas_call(
        body, in_specs=[any_spec] * (n + len(deps)), out_specs=[any_spec] * n,
        out_shape=[jax.ShapeDtypeStruct((N_DEV,) + a.shape, a.dtype) for a in arrs],
        scratch_shapes=[pltpu.SemaphoreType.DMA((7 * n,)), pltpu.SemaphoreType.DMA((7 * n,)),
                        pltpu.SemaphoreType.DMA((n,))],
        name=name,
    )(*arrs, *deps)


def _grad_to_sibling(gds):
    n = len(gds)

    def body(*refs):
        g_refs, r_refs = refs[:n], refs[n:2 * n]
        send_sems, recv_sems = refs[2 * n:]
        x, y, c = _dev()
        cps = []
        for a in range(n):
            for k in range(4):
                cp = pltpu.make_async_remote_copy(
                    src_ref=g_refs[a].at[2 * k + (1 - c)], dst_ref=r_refs[a].at[k],
                    send_sem=send_sems.at[4 * a + k], recv_sem=recv_sems.at[4 * a + k],
                    device_id=(x, y, 1 - c), device_id_type=MESH)
                cp.start()
                cps.append(cp)
        for cp in cps:
            cp.wait()

    any_spec = pl.BlockSpec(memory_space=pl.ANY)
    return pl.pallas_call(
        body, in_specs=[any_spec] * n, out_specs=[any_spec] * n,
        out_shape=[jax.ShapeDtypeStruct((4,) + g.shape[1:], g.dtype) for g in gds],
        scratch_shapes=[pltpu.SemaphoreType.DMA((4 * n,)), pltpu.SemaphoreType.DMA((4 * n,))],
        name="grad_to_sibling",
    )(*gds)


def _chip_sum(name, gd, from_sib, c_arr):
    _, k, n = gd.shape
    tr = _pick(k, max(16, (1 << 20) // (4 * n)), 16)

    def body(c_ref, a_ref, b_ref, o_ref):
        o_ref[...] = (a_ref[...] + b_ref[...]).astype(o_ref.dtype)

    return pl.pallas_call(
        body,
        grid_spec=pltpu.PrefetchScalarGridSpec(
            num_scalar_prefetch=1, grid=(4, k // tr),
            in_specs=[pl.BlockSpec((1, tr, n), lambda q, i, cr: (2 * q + cr[0], i, 0)),
                      pl.BlockSpec((1, tr, n), lambda q, i, cr: (q, i, 0))],
            out_specs=pl.BlockSpec((1, tr, n), lambda q, i, cr: (q, i, 0))),
        out_shape=jax.ShapeDtypeStruct((4, k, n), BF16), name=name, compiler_params=_params(2),
    )(c_arr, gd, from_sib)


def _grad_to_chips(sums):
    n = len(sums)

    def body(*refs):
        s_refs, p_refs = refs[:n], refs[n:2 * n]
        send_sems, recv_sems, local_sems = refs[2 * n:]
        x, y, c = _dev()
        my_chip = 2 * x + y
        cps = []
        for a in range(n):
            local = pltpu.make_async_copy(s_refs[a].at[my_chip], p_refs[a].at[my_chip], local_sems.at[a])
            local.start()
            cps.append(local)
            for j, (px, py) in enumerate([(1 - x, y), (x, 1 - y), (1 - x, 1 - y)]):
                cp = pltpu.make_async_remote_copy(
                    src_ref=s_refs[a].at[2 * px + py], dst_ref=p_refs[a].at[my_chip],
                    send_sem=send_sems.at[3 * a + j], recv_sem=recv_sems.at[3 * a + j],
                    device_id=(px, py, c), device_id_type=MESH)
                cp.start()
                cps.append(cp)
        for cp in cps:
            cp.wait()

    any_spec = pl.BlockSpec(memory_space=pl.ANY)
    return pl.pallas_call(
        body, in_specs=[any_spec] * n, out_specs=[any_spec] * n,
        out_shape=[jax.ShapeDtypeStruct(s.shape, s.dtype) for s in sums],
        scratch_shapes=[pltpu.SemaphoreType.DMA((3 * n,)), pltpu.SemaphoreType.DMA((3 * n,)),
                        pltpu.SemaphoreType.DMA((n,))],
        name="grad_to_chips",
    )(*sums)


FLIPS = [(0, 0, 1), (0, 1, 0), (1, 0, 0), (0, 1, 1), (1, 0, 1), (1, 1, 0), (1, 1, 1)]
N_PEERS = len(FLIPS)
_HBM = pl.BlockSpec(memory_space=pltpu.HBM)
_SEM = pl.BlockSpec(memory_space=pltpu.SEMAPHORE)
_EFFECT = pltpu.SideEffectType.DATAFLOW_SIDE_EFFECTING


def _flip(x, y, c, f):
    return (1 - x if f[0] else x, 1 - y if f[1] else y, 1 - c if f[2] else c)


def _slot(p):
    return 4 * p[0] + 2 * p[1] + p[2]


def _exchange_copies(src_refs, land_refs, send_sems, recv_sems, gather):
    x, y, c = _dev()
    mine = _slot((x, y, c))
    cps = []
    for a, (src, land) in enumerate(zip(src_refs, land_refs)):
        for k, f in enumerate(FLIPS):
            peer = _flip(x, y, c, f)
            cps.append(pltpu.make_async_remote_copy(
                src_ref=src if gather else src.at[_slot(peer)], dst_ref=land.at[mine],
                send_sem=send_sems.at[N_PEERS * a + k], recv_sem=recv_sems.at[N_PEERS * a + k],
                device_id=peer, device_id_type=MESH))
    return cps


def _exchange_start(name, srcs, gather, after):
    n = len(srcs)
    lands = [lax.empty(((N_DEV,) + s.shape) if gather else s.shape, s.dtype) for s in srcs]

    def body(*refs):
        src_refs, land_refs = refs[:n], refs[n:2 * n]
        send_sems, recv_sems, local_sems = refs[2 * n + 1:2 * n + 4]
        token = refs[-1]
        if gather:
            x, y, c = _dev()
            for a in range(n):
                pltpu.make_async_copy(src_refs[a], land_refs[a].at[_slot((x, y, c))], local_sems.at[a]).start()
        for cp in _exchange_copies(src_refs, land_refs, send_sems, recv_sems, gather):
            cp.start()
        token[...] = jnp.zeros_like(token)

    hbm = lambda z: pltpu.HBM(z.shape, z.dtype)
    outs = pl.pallas_call(
        body, name=name,
        out_shape=(pltpu.SemaphoreType.DMA((N_PEERS * n,)), pltpu.SemaphoreType.DMA((N_PEERS * n,)),
                   pltpu.SemaphoreType.DMA((n,)), *[hbm(s) for s in srcs], *[hbm(z) for z in lands],
                   jax.ShapeDtypeStruct((8, LANES), F32)),
        in_specs=[_HBM] * (2 * n) + [pl.BlockSpec(memory_space=pl.ANY)],
        out_specs=(_SEM, _SEM, _SEM, *[_HBM] * (2 * n), pl.BlockSpec(memory_space=pltpu.VMEM)),
        input_output_aliases={i: 3 + i for i in range(2 * n)},
        compiler_params=pltpu.CompilerParams(has_side_effects=_EFFECT),
    )(*[pltpu.with_memory_space_constraint(z, pltpu.HBM) for z in list(srcs) + lands], after)
    return (outs[:3], outs[3:3 + n], outs[3 + n:3 + 2 * n], gather), outs[-1]


def _exchange_wait(name, handles, after):
    sems, srcs, lands, gather = handles
    n = len(srcs)

    def body(*refs):
        src_refs, land_refs = refs[:n], refs[n:2 * n]
        send_sems, recv_sems, local_sems = refs[2 * n:2 * n + 3]
        if gather:
            for a in range(n):
                pltpu.make_async_copy(src_refs[a], land_refs[a].at[0], local_sems.at[a]).wait()
        for cp in _exchange_copies(src_refs, land_refs, send_sems, recv_sems, gather):
            cp.wait_send()
            cp.wait_recv()

    hbm = lambda z: pltpu.HBM(z.shape, z.dtype)
    outs = pl.pallas_call(
        body, name=name, out_shape=tuple(hbm(z) for z in list(srcs) + list(lands)),
        in_specs=[_HBM] * (2 * n) + [_SEM] * 3 + [pl.BlockSpec(memory_space=pl.ANY)],
        out_specs=tuple([_HBM] * (2 * n)), input_output_aliases={i: i for i in range(2 * n)},
        compiler_params=pltpu.CompilerParams(has_side_effects=_EFFECT),
    )(*srcs, *lands, *sems, after)
    return list(outs[n:])


def _pack_rows(sizes, width, row_align):
    offs, r = [], 0
    for s in sizes:
        offs.append(r)
        r += -(-s // width)
    total = -(-r // row_align) * row_align
    return offs, total


def _pack(items, width, row_align, lead=()):
    nl = len(lead)
    sizes = [int(jnp.size(a)) // max(1, functools.reduce(lambda p, q: p * q, lead, 1)) for a in items]
    offs, total = _pack_rows(sizes, width, row_align)
    flat = []
    used = 0
    for a, s in zip(items, sizes):
        f = a.reshape(lead + (s,))
        pad = -(-s // width) * width - s
        if pad:
            f = jnp.pad(f, [(0, 0)] * nl + [(0, pad)])
        flat.append(f)
        used += s + pad
    tail = total * width - used
    if tail:
        flat.append(jnp.zeros(lead + (tail,), items[0].dtype))
    return jnp.concatenate(flat, axis=-1).reshape(lead + (total, width)), offs


def _unpack(packed, off, shape, lead=()):
    nl = len(lead)
    size = functools.reduce(lambda p, q: p * q, shape, 1)
    width = packed.shape[-1]
    rows = -(-size // width)
    blk = lax.slice_in_dim(packed, off, off + rows, axis=nl).reshape(lead + (rows * width,))
    return lax.slice_in_dim(blk, 0, size, axis=nl).reshape(lead + tuple(shape))


def _rms(x, g):
    return (x * lax.rsqrt(jnp.mean(x * x, axis=-1, keepdims=True) + RMS_EPS)) * g


def _norm_mod(x, g, sc, sh):
    return _rms(x, g) * (1.0 + sc) + sh


def _mix_fn(glu_a, glu_b, attn, ga, gs):
    return jax.nn.sigmoid(ga) * attn + jax.nn.sigmoid(gs) * (glu_a * jax.nn.sigmoid(glu_b))


def _s5_disc_fn(a_re, a_im, log_dt, b_re, b_im):
    dt = jnp.exp(log_dt)
    mag = jnp.exp(a_re * dt)
    lr, li = mag * jnp.cos(a_im * dt), mag * jnp.sin(a_im * dt)
    den = a_re * a_re + a_im * a_im
    zr = ((lr - 1.0) * a_re + li * a_im) / den
    zi = (li * a_re - (lr - 1.0) * a_im) / den
    return lr, li, zr[None] * b_re - zi[None] * b_im, zr[None] * b_im + zi[None] * b_re


def _adamw_fn(w, g, m, v):
    m = ADAM_B1 * m + (1.0 - ADAM_B1) * g
    v = ADAM_B2 * v + (1.0 - ADAM_B2) * jnp.square(g)
    m_hat = m / (1.0 - ADAM_B1 ** ADAM_STEP)
    v_hat = v / (1.0 - ADAM_B2 ** ADAM_STEP)
    delta = -ADAM_LR * (m_hat / (jnp.sqrt(v_hat) + ADAM_EPS) + ADAM_WD * w)
    return delta, m, v


def _adamw(name, parts, w, m, v):
    p, r, c = parts.shape
    tr = _pick(r, max(8, (1 << 21) // (4 * c * max(p, 2))), 8)

    def fn(pv, wv, mv, vv):
        g = pv[0]
        for i in range(1, p):
            g = g + pv[i]
        d, m2, v2 = _adamw_fn(wv, g, mv, vv)
        return g, d, m2, v2

    spec = pl.BlockSpec((tr, c), lambda i: (i, 0))
    return _tile_call(
        name, fn, (r // tr,), [parts, w, m, v],
        [pl.BlockSpec((p, tr, c), lambda i: (0, i, 0)), spec, spec, spec],
        [jax.ShapeDtypeStruct((r, c), F32)] * 4, [spec] * 4)


def _adamw_sharded(name, parts, own_src, w, m, v, place):
    _, k, n = parts.shape
    tr = _pick(k, max(16, (1 << 19) // (4 * n)), 16)

    def body(pl_ref, p_ref, a_ref, w_ref, m_ref, v_ref, g_ref, d_ref, m2_ref, v2_ref):
        own = a_ref[0]
        g = None
        for q in range(N_DEV):
            term = jnp.where(pl_ref[0] == q, own, p_ref[q].astype(F32))
            g = term if g is None else g + term
        d, m2, v2 = _adamw_fn(w_ref[...], g, m_ref[...], v_ref[...])
        g_ref[...] = g
        d_ref[...] = d
        m2_ref[...] = m2
        v2_ref[...] = v2

    spec = pl.BlockSpec((tr, n), lambda i, pr: (i, 0))
    return pl.pallas_call(
        body,
        grid_spec=pltpu.PrefetchScalarGridSpec(
            num_scalar_prefetch=1, grid=(k // tr,),
            in_specs=[pl.BlockSpec((N_DEV, tr, n), lambda i, pr: (0, i, 0)),
                      pl.BlockSpec((1, tr, n), lambda i, pr: (pr[1], i, 0)),
                      spec, spec, spec],
            out_specs=[spec] * 4),
        out_shape=[jax.ShapeDtypeStruct((k, n), F32)] * 4, name=name, compiler_params=_params(1),
    )(place, parts, own_src, w, m, v)


def _attn_mask(n, rows):
    qi = lax.broadcasted_iota(jnp.int32, (rows, 2 * ATT_BLOCK), 0) & (ATT_BLOCK - 1)
    kj = lax.broadcasted_iota(jnp.int32, (rows, 2 * ATT_BLOCK), 1)
    rel = qi + ATT_BLOCK - kj
    return (rel >= 0) & (rel < ATT_BLOCK) & ((kj >= ATT_BLOCK) | (n > 0))


def _attn_probs(q, k, sink, mask):
    s = lax.dot_general(q, k, (((1,), (1,)), ((), ())), preferred_element_type=F32) * (HEAD_DIM ** -0.5)
    s = jnp.where(mask, s, NEG_INF)
    m = jnp.maximum(jnp.max(s, axis=-1, keepdims=True), sink)
    p = jnp.exp(s - m)
    e_sink = jnp.exp(sink - m)
    inv = 1.0 / (jnp.sum(p, axis=-1, keepdims=True) + e_sink)
    return p * inv, e_sink * inv


def _attn_specs(qpk):
    blk = ATT_BLOCK
    q_spec = pl.BlockSpec((qpk, blk, HEAD_DIM), lambda h, n: (h, n, 0))
    cur = pl.BlockSpec((1, blk, HEAD_DIM), lambda h, n: (h, n, 0))
    prev = pl.BlockSpec((1, blk, HEAD_DIM), lambda h, n: (h, jnp.maximum(n - 1, 0), 0))
    sink_spec = pl.BlockSpec((1, qpk * blk, 1), lambda h, n: (h, 0, 0))
    return q_spec, cur, prev, sink_spec


def _attn_fwd(q, k, v, sinks):
    hq, l, _ = q.shape
    qpk = hq // N_KV_HEADS
    nb = l // ATT_BLOCK
    rows = qpk * ATT_BLOCK
    q_spec, cur, prev, sink_spec = _attn_specs(qpk)

    def body(q_ref, kp_ref, kc_ref, vp_ref, vc_ref, sink_ref, o_ref):
        mask = _attn_mask(pl.program_id(1), rows)
        kk = jnp.concatenate([kp_ref[0], kc_ref[0]], axis=0).astype(BF16)
        vv = jnp.concatenate([vp_ref[0], vc_ref[0]], axis=0).astype(BF16)
        p, _ = _attn_probs(q_ref[...].reshape(rows, HEAD_DIM).astype(BF16), kk, sink_ref[0], mask)
        o = jnp.dot(p.astype(BF16), vv, preferred_element_type=F32)
        o_ref[...] = o.reshape(qpk, ATT_BLOCK, HEAD_DIM).astype(o_ref.dtype)

    return pl.pallas_call(
        body, grid=(N_KV_HEADS, nb), in_specs=[q_spec, prev, cur, prev, cur, sink_spec],
        out_specs=q_spec, out_shape=jax.ShapeDtypeStruct((hq, l, HEAD_DIM), BF16),
        name="attn_fwd", compiler_params=_params(2),
    )(q, k, k, v, v, sinks)


def _attn_bwd(q, k, v, sinks, do):
    hq, l, _ = q.shape
    qpk = hq // N_KV_HEADS
    nb = l // ATT_BLOCK
    blk = ATT_BLOCK
    rows = qpk * blk
    q_spec, cur, prev, sink_spec = _attn_specs(qpk)
    part_spec = pl.BlockSpec((1, 1, 2 * blk, HEAD_DIM), lambda h, n: (h, n, 0, 0))
    dsink_spec = pl.BlockSpec((qpk, 1, LANES), lambda h, n: (h, 0, 0))
    tn = (((0,), (0,)), ((), ()))

    def body(q_ref, do_ref, kp_ref, kc_ref, vp_ref, vc_ref, sink_ref, dq_ref, dkp_ref, dvp_ref, dsink_ref):
        n = pl.program_id(1)
        mask = _attn_mask(n, rows)
        kk = jnp.concatenate([kp_ref[0], kc_ref[0]], axis=0).astype(BF16)
        vv = jnp.concatenate([vp_ref[0], vc_ref[0]], axis=0).astype(BF16)
        qb = q_ref[...].reshape(rows, HEAD_DIM).astype(BF16)
        do32 = do_ref[...].astype(F32).reshape(rows, HEAD_DIM)
        dob = do32.astype(BF16)
        p, p_sink = _attn_probs(qb, kk, sink_ref[0], mask)
        pb = p.astype(BF16)
        o = jnp.dot(pb, vv, preferred_element_type=F32)
        delta = jnp.sum(do32 * o, axis=-1, keepdims=True)
        dp = lax.dot_general(dob, vv, (((1,), (1,)), ((), ())), preferred_element_type=F32)
        ds = (p * (dp - delta) * (HEAD_DIM ** -0.5)).astype(BF16)
        dq = jnp.dot(ds, kk, preferred_element_type=F32)
        dq_ref[...] = dq.reshape(qpk, blk, HEAD_DIM).astype(dq_ref.dtype)
        dkp_ref[0, 0] = lax.dot_general(ds, qb, tn, preferred_element_type=F32)
        dvp_ref[0, 0] = lax.dot_general(pb, dob, tn, preferred_element_type=F32)
        dsr = p_sink * delta
        for g in range(qpk):
            dsg = jnp.broadcast_to(-_colsum(dsr[g * blk:(g + 1) * blk]), (1, LANES))

            @pl.when(n == 0)
            def _():
                dsink_ref[g] = dsg

            @pl.when(n > 0)
            def _():
                dsink_ref[g] += dsg


    part_shape = jax.ShapeDtypeStruct((N_KV_HEADS, nb, 2 * blk, HEAD_DIM), F32)
    dq, dkp, dvp, dsink = pl.pallas_call(
        body, grid=(N_KV_HEADS, nb), in_specs=[q_spec, q_spec, prev, cur, prev, cur, sink_spec],
        out_specs=[q_spec, part_spec, part_spec, dsink_spec],
        out_shape=[jax.ShapeDtypeStruct((hq, l, HEAD_DIM), BF16), part_shape, part_shape,
                   jax.ShapeDtypeStruct((hq, 1, LANES), F32)],
        name="attn_bwd", compiler_params=_params(2),
    )(q, do, k, k, v, v, sinks)

    def combine(a_cur, a_nxt, b_cur, b_nxt):
        last = pl.program_id(1) == nb - 1
        keep = jnp.where(last, 0.0, 1.0)
        return (a_cur[0, 0, blk:] + keep * a_nxt[0, 0, :blk])[None], (b_cur[0, 0, blk:] + keep * b_nxt[0, 0, :blk])[None]

    nxt_spec = pl.BlockSpec((1, 1, 2 * blk, HEAD_DIM), lambda h, n: (h, jnp.minimum(n + 1, nb - 1), 0, 0))
    kv_shape = jax.ShapeDtypeStruct((N_KV_HEADS, l, HEAD_DIM), BF16)
    dk, dv = _tile_call("attn_dkv", combine, (N_KV_HEADS, nb), [dkp, dkp, dvp, dvp],
                        [part_spec, nxt_spec, part_spec, nxt_spec], [kv_shape, kv_shape], [cur, cur])
    return dq, dk, dv, dsink


def _block_diag(m):
    j, gl, a, b = m.shape
    eye = jnp.eye(gl, dtype=m.dtype)
    return (m[:, :, :, None, :] * eye[None, :, None, :, None]).reshape(j, gl * a, gl * b)


def _diag_blocks(z, a):
    j = z.shape[0]
    gl = z.shape[1] // a
    b = z.shape[2] // gl
    d = jnp.diagonal(z.reshape(j, gl, a, gl, b), axis1=1, axis2=3)
    return d.transpose(0, 3, 1, 2)


def _s5_permute(src_ref, dst_ref, t_len):
    seg = t_len // 8
    for k in range(seg):
        dst_ref[8 * k:8 * k + 8, :] = src_ref[pl.ds(k, 8, stride=seg), :]


def _s5_unpermute(perm_ref, t_len, emit):
    per_seg = t_len // 64
    for m in range(t_len // 8):
        emit(8 * m, perm_ref[pl.ds(64 * (m % per_seg) + m // per_seg, 8, stride=8), :])


def _s5_powers(p_ref, lr, li, seg):
    hs = TILE_STATES

    def step(k, carry):
        pr, pi = carry
        p_ref[pl.ds(k, 1), 0:hs] = pr
        p_ref[pl.ds(k, 1), hs:2 * hs] = pi
        return lr * pr - li * pi, lr * pi + li * pr

    lax.fori_loop(0, seg, step, (lr, li))


def _s5_local_scan(x_ref, base, lr, li, seg, reverse):
    hs = TILE_STATES
    lr8, li8 = jnp.broadcast_to(lr, (8, hs)), jnp.broadcast_to(li, (8, hs))
    if reverse:
        li8 = -li8

    def step(i, carry):
        hr, hi = carry
        k = seg - 1 - i if reverse else i
        rows = pl.ds(pl.multiple_of(base + 8 * k, 8), 8)
        nr = lr8 * hr - li8 * hi + x_ref[rows, 0:hs]
        ni = lr8 * hi + li8 * hr + x_ref[rows, hs:2 * hs]
        x_ref[rows, 0:hs] = nr
        x_ref[rows, hs:2 * hs] = ni
        return nr, ni

    zero = jnp.zeros((8, hs), F32)
    return lax.fori_loop(0, seg, step, (zero, zero), unroll=2)


def _s5_carries(c_ref, e_ref, ends, start, pw_r, pw_i, reverse):
    hs = TILE_STATES
    e_ref[:, 0:hs] = ends[0]
    e_ref[:, hs:2 * hs] = ends[1]
    cr, ci = start
    if reverse:
        pw_i = -pw_i
    for s in (range(7, -1, -1) if reverse else range(8)):
        c_ref[s:s + 1, 0:hs] = cr
        c_ref[s:s + 1, hs:2 * hs] = ci
        er, ei = e_ref[s:s + 1, 0:hs], e_ref[s:s + 1, hs:2 * hs]
        cr, ci = er + pw_r * cr - pw_i * ci, ei + pw_r * ci + pw_i * cr
    return cr, ci


def _s5_states(u_perm_b16, bd_ref, x_ref, base, c_ref, e_ref, p_ref, lr, li, h_in, t_len):
    hs = TILE_STATES
    seg = t_len // 8
    x_ref[pl.ds(base, t_len), :] = jnp.dot(u_perm_b16, bd_ref[0], preferred_element_type=F32)
    ends = _s5_local_scan(x_ref, base, lr, li, seg, False)
    pw_r, pw_i = p_ref[seg - 1:seg, 0:hs], p_ref[seg - 1:seg, hs:2 * hs]
    h_out = _s5_carries(c_ref, e_ref, ends, h_in, pw_r, pw_i, False)
    cr, ci = c_ref[:, 0:hs], c_ref[:, hs:2 * hs]

    def fix(k, carry):
        rows = pl.ds(pl.multiple_of(base + 8 * k, 8), 8)
        pr, pi = p_ref[pl.ds(k, 1), 0:hs], p_ref[pl.ds(k, 1), hs:2 * hs]
        x_ref[rows, 0:hs] += pr * cr - pi * ci
        x_ref[rows, hs:2 * hs] += pr * ci + pi * cr
        return carry

    lax.fori_loop(0, seg, fix, 0, unroll=2)
    return h_out


def _s5_fwd(proj, u_off, bd, cbd, lam, dvec, t_len):
    l = proj.shape[0]
    nj = bd.shape[0]
    nch = l // t_len
    hs = TILE_STATES
    ub = u_off // LANES
    seg = t_len // 8
    assert t_len % 64 == 0

    def body(u_ref, bd_ref, cbd_ref, lam_ref, d_ref, y_ref, hst_ref, x_ref, h_ref, p_ref, c_ref, e_ref, up_ref, yp_ref):
        lr, li = lam_ref[0, 0:1, :], lam_ref[0, 1:2, :]

        @pl.when(pl.program_id(1) == 0)
        def _():
            h_ref[...] = jnp.zeros_like(h_ref)
            _s5_powers(p_ref, lr, li, seg)

        hst_ref[0, 0] = h_ref[...]
        _s5_permute(u_ref, up_ref, t_len)
        h_out = _s5_states(up_ref[...].astype(BF16), bd_ref, x_ref, 0, c_ref, e_ref, p_ref, lr, li,
                           (h_ref[:, 0:hs], h_ref[:, hs:2 * hs]), t_len)
        h_ref[:, 0:hs] = h_out[0]
        h_ref[:, hs:2 * hs] = h_out[1]
        yp_ref[...] = jnp.dot(x_ref[...].astype(BF16), cbd_ref[0], preferred_element_type=F32)
        dv = d_ref[0]

        def out(r0, rows):
            y_ref[r0:r0 + 8, :] = rows + dv * u_ref[r0:r0 + 8, :]

        _s5_unpermute(yp_ref, t_len, out)

    return pl.pallas_call(
        body, grid=(nj, nch),
        in_specs=[pl.BlockSpec((t_len, LANES), lambda j, c: (c, ub + j)),
                  pl.BlockSpec((1, LANES, 2 * hs), lambda j, c: (j, 0, 0)),
                  pl.BlockSpec((1, 2 * hs, LANES), lambda j, c: (j, 0, 0)),
                  pl.BlockSpec((1, 2, hs), lambda j, c: (j, 0, 0)),
                  pl.BlockSpec((1, 1, LANES), lambda j, c: (j, 0, 0))],
        out_specs=[pl.BlockSpec((t_len, LANES), lambda j, c: (c, j)),
                   pl.BlockSpec((1, 1, 1, 2 * hs), lambda j, c: (j, c, 0, 0))],
        out_shape=[jax.ShapeDtypeStruct((l, nj * LANES), F32),
                   jax.ShapeDtypeStruct((nj, nch, 1, 2 * hs), F32)],
        scratch_shapes=[pltpu.VMEM((t_len, 2 * hs), F32), pltpu.VMEM((1, 2 * hs), F32),
                        pltpu.VMEM((seg, 2 * hs), F32), pltpu.VMEM((8, 2 * hs), F32), pltpu.VMEM((8, 2 * hs), F32),
                        pltpu.VMEM((t_len, LANES), F32), pltpu.VMEM((t_len, LANES), F32)],
        name="s5_fwd", compiler_params=_params(2),
    )(proj, bd, cbd, lam, dvec)


def _s5_bwd(proj, u_off, dy, hst, bd, bdt, cbdt, lam, dvec, t_len):
    l = proj.shape[0]
    nj = bd.shape[0]
    nch = l // t_len
    hs = TILE_STATES
    ub = u_off // LANES
    seg = t_len // 8
    tn = (((0,), (0,)), ((), ()))
    assert t_len % 64 == 0

    def body(u_ref, dy_ref, hst_ref, bd_ref, bdt_ref, cbdt_ref, lam_ref, d_ref,
             du_ref, dbd_ref, dcbdt_ref, dlam_ref, dd_ref,
             x_ref, g_ref, gc_ref, p_ref, c_ref, e_ref, up_ref, dyp_ref, dup_ref):
        first = pl.program_id(1) == 0
        lr, li = lam_ref[0, 0:1, :], lam_ref[0, 1:2, :]

        @pl.when(first)
        def _():
            gc_ref[...] = jnp.zeros_like(gc_ref)
            _s5_powers(p_ref, lr, li, seg)

        _s5_permute(u_ref, up_ref, t_len)
        _s5_permute(dy_ref, dyp_ref, t_len)
        ub16, dyb16 = up_ref[...].astype(BF16), dyp_ref[...].astype(BF16)
        h0 = hst_ref[0, 0]
        _s5_states(ub16, bd_ref, x_ref, 8, c_ref, e_ref, p_ref, lr, li, (h0[:, 0:hs], h0[:, hs:2 * hs]), t_len)
        x_ref[0:8, :] = c_ref[...]
        g_ref[...] = jnp.dot(dyb16, cbdt_ref[0], preferred_element_type=F32)
        starts = _s5_local_scan(g_ref, 0, lr, li, seg, True)
        pw_r, pw_i = p_ref[seg - 1:seg, 0:hs], p_ref[seg - 1:seg, hs:2 * hs]
        g_out = _s5_carries(c_ref, e_ref, starts, (gc_ref[:, 0:hs], gc_ref[:, hs:2 * hs]), pw_r, pw_i, True)
        gc_ref[:, 0:hs] = g_out[0]
        gc_ref[:, hs:2 * hs] = g_out[1]
        cr, ci = c_ref[:, 0:hs], c_ref[:, hs:2 * hs]

        def fix(k, carry):
            alr, ali = carry
            rows = pl.ds(pl.multiple_of(8 * k, 8), 8)
            pr, pi = p_ref[pl.ds(seg - 1 - k, 1), 0:hs], p_ref[pl.ds(seg - 1 - k, 1), hs:2 * hs]
            gr = g_ref[rows, 0:hs] + pr * cr + pi * ci
            gi = g_ref[rows, hs:2 * hs] + pr * ci - pi * cr
            g_ref[rows, 0:hs] = gr
            g_ref[rows, hs:2 * hs] = gi
            hpr, hpi = x_ref[rows, 0:hs], x_ref[rows, hs:2 * hs]
            return alr + gr * hpr + gi * hpi, ali + gi * hpr - gr * hpi

        zero = jnp.zeros((8, hs), F32)
        alr, ali = lax.fori_loop(0, seg, fix, (zero, zero), unroll=2)
        alr, ali = _colsum(alr), _colsum(ali)
        g = g_ref[...].astype(BF16)
        h = x_ref[pl.ds(8, t_len), :].astype(BF16)
        dup_ref[...] = jnp.dot(g, bdt_ref[0], preferred_element_type=F32)
        dv = d_ref[0]

        def out(r0, rows):
            du_ref[r0:r0 + 8, :] = (rows + dv * dy_ref[r0:r0 + 8, :]).astype(du_ref.dtype)

        _s5_unpermute(dup_ref, t_len, out)
        sign = jnp.where(lax.broadcasted_iota(jnp.int32, (1, 2 * hs), 1) < hs, 1.0, -1.0)
        dbd = lax.dot_general(ub16, g, tn, preferred_element_type=F32)
        dcbdt = lax.dot_general(dyb16, h, tn, preferred_element_type=F32) * sign
        ddv = _colsum(dy_ref[...] * u_ref[...])

        @pl.when(first)
        def _():
            dbd_ref[0] = dbd
            dcbdt_ref[0] = dcbdt
            dlam_ref[0, 0:1, :] = alr
            dlam_ref[0, 1:2, :] = ali
            dd_ref[0] = ddv

        @pl.when(jnp.logical_not(first))
        def _():
            dbd_ref[0] += dbd
            dcbdt_ref[0] += dcbdt
            dlam_ref[0, 0:1, :] += alr
            dlam_ref[0, 1:2, :] += ali
            dd_ref[0] += ddv

    rev = lambda c: nch - 1 - c
    wide = pl.BlockSpec((1, LANES, 2 * hs), lambda j, c: (j, 0, 0))
    tall = pl.BlockSpec((1, 2 * hs, LANES), lambda j, c: (j, 0, 0))
    return pl.pallas_call(
        body, grid=(nj, nch),
        in_specs=[pl.BlockSpec((t_len, LANES), lambda j, c: (rev(c), ub + j)),
                  pl.BlockSpec((t_len, LANES), lambda j, c: (rev(c), j)),
                  pl.BlockSpec((1, 1, 1, 2 * hs), lambda j, c: (j, rev(c), 0, 0)),
                  wide, tall, wide,
                  pl.BlockSpec((1, 2, hs), lambda j, c: (j, 0, 0)),
                  pl.BlockSpec((1, 1, LANES), lambda j, c: (j, 0, 0))],
        out_specs=[pl.BlockSpec((t_len, LANES), lambda j, c: (rev(c), j)),
                   wide, wide,
                   pl.BlockSpec((1, 2, hs), lambda j, c: (j, 0, 0)),
                   pl.BlockSpec((1, 1, LANES), lambda j, c: (j, 0, 0))],
        out_shape=[jax.ShapeDtypeStruct((l, nj * LANES), BF16),
                   jax.ShapeDtypeStruct((nj, LANES, 2 * hs), F32),
                   jax.ShapeDtypeStruct((nj, LANES, 2 * hs), F32),
                   jax.ShapeDtypeStruct((nj, 2, hs), F32),
                   jax.ShapeDtypeStruct((nj, 1, LANES), F32)],
        scratch_shapes=[pltpu.VMEM((t_len + 8, 2 * hs), F32), pltpu.VMEM((t_len, 2 * hs), F32),
                        pltpu.VMEM((1, 2 * hs), F32), pltpu.VMEM((seg, 2 * hs), F32),
                        pltpu.VMEM((8, 2 * hs), F32), pltpu.VMEM((8, 2 * hs), F32),
                        pltpu.VMEM((t_len, LANES), F32), pltpu.VMEM((t_len, LANES), F32),
                        pltpu.VMEM((t_len, LANES), F32)],
        name="s5_bwd", compiler_params=_params(2),
    )(proj, dy, hst, bd, bdt, cbdt, lam, dvec)


def _full_spec(shape):
    nd = len(shape)
    return pl.BlockSpec(tuple(shape), lambda i: (0,) * nd)


def _sds(shape, dtype=F32):
    return jax.ShapeDtypeStruct(tuple(shape), dtype)


def kernel(x, c, ada_w, ada_b, norm_mix_g, w_in, attn_sinks, w_attn_proj, ssm_a_re, ssm_a_im, ssm_log_dt, ssm_b_re, ssm_b_im, ssm_c_re, ssm_c_im, ssm_d, w_ssm_glu, w_out, norm_ffn_g, w_ffn_up, ffn_conv_w, ffn_conv_b, w_ffn_down, final_g, loss_target, m_ada_w, m_ada_b, m_norm_mix_g, m_w_in, m_attn_sinks, m_w_attn_proj, m_ssm_a_re, m_ssm_a_im, m_ssm_log_dt, m_ssm_b_re, m_ssm_b_im, m_ssm_c_re, m_ssm_c_im, m_ssm_d, m_w_ssm_glu, m_w_out, m_norm_ffn_g, m_w_ffn_up, m_ffn_conv_w, m_ffn_conv_b, m_w_ffn_down, m_final_g, v_ada_w, v_ada_b, v_norm_mix_g, v_w_in, v_attn_sinks, v_w_attn_proj, v_ssm_a_re, v_ssm_a_im, v_ssm_log_dt, v_ssm_b_re, v_ssm_b_im, v_ssm_c_re, v_ssm_c_im, v_ssm_d, v_w_ssm_glu, v_w_out, v_norm_ffn_g, v_w_ffn_up, v_ffn_conv_w, v_ffn_conv_b, v_w_ffn_down, v_final_g):
    given = dict(locals())
    names = ['ada_w', 'ada_b', 'norm_mix_g', 'w_in', 'attn_sinks', 'w_attn_proj', 'ssm_a_re', 'ssm_a_im',
             'ssm_log_dt', 'ssm_b_re', 'ssm_b_im', 'ssm_c_re', 'ssm_c_im', 'ssm_d', 'w_ssm_glu', 'w_out',
             'norm_ffn_g', 'w_ffn_up', 'ffn_conv_w', 'ffn_conv_b', 'w_ffn_down', 'final_g']

    xs = x[0]
    tgt = loss_target[0]
    l, d = xs.shape
    attn_w = w_attn_proj.shape[1]
    ssm_w = w_ssm_glu.shape[1]
    hq = attn_sinks.shape[1]
    qpk = hq // N_KV_HEADS
    kv_w = N_KV_HEADS * HEAD_DIM
    n_groups = ssm_a_re.shape[1]
    dff = ffn_conv_b.shape[1]
    in_w = attn_w + 2 * kv_w + ssm_w + 2 * d
    nj = ssm_w // LANES
    off_k, off_v, off_u = attn_w, attn_w + kv_w, attn_w + 2 * kv_w
    off_ga, off_gs = off_u + ssm_w, off_u + ssm_w + d
    assert hq * HEAD_DIM == attn_w and n_groups * SSM_P == ssm_w and l % ATT_BLOCK == 0

    xi, yi, ci = _dev()
    idx = 4 * xi + 2 * yi + ci

    row_sharded = {'w_out': (d, d), 'w_ffn_down': (dff, d)}
    big = ['w_in', 'w_attn_proj', 'w_ssm_glu', 'w_out', 'w_ffn_up', 'w_ffn_down']
    spack, s_offs = _pack([c, ffn_conv_w[0]], LANES, 8)
    w16 = {k: given[k][0].astype(BF16) for k in big}
    wg_in, sg = _all_gather("gather_first", [w16['w_in'], spack])
    mixer_w = ['w_attn_proj', 'w_ssm_glu', 'w_out']
    h_mixer, tok = _exchange_start("gather_mixer_start", [w16[k] for k in mixer_w], True, wg_in)
    h_up, tok = _exchange_start("gather_ffn_up_start", [w16['w_ffn_up']], True, tok)
    h_down, tok = _exchange_start("gather_ffn_down_start", [w16['w_ffn_down']], True, tok)
    full = {'w_in': wg_in.transpose(1, 0, 2).reshape(d, in_w)}
    c_all = _unpack(sg, s_offs[0], (d,), lead=(N_DEV,))
    conv_w = _unpack(sg, s_offs[1], ffn_conv_w.shape[1:], lead=(N_DEV,)).transpose(1, 0, 2).reshape(3, dff)
    conv_b = ffn_conv_b

    mod_n = ada_w.shape[2]
    tcm = _pick(mod_n, 512)
    ada_b_mine = lax.dynamic_slice_in_dim(ada_b, idx * mod_n, mod_n, axis=1)

    def modpart_fn(cv, wv, bv):
        cond = cv * jax.nn.sigmoid(cv)
        return jnp.dot(cond.astype(BF16), wv.astype(BF16), preferred_element_type=F32) + bv, cond

    modp, cond_all = _tile_call(
        "ada_rows", modpart_fn, (mod_n // tcm,), [c_all, ada_w[0], ada_b_mine],
        [pl.BlockSpec((N_DEV, d), lambda j: (0, 0)), pl.BlockSpec((d, tcm), lambda j: (0, j)),
         pl.BlockSpec((1, tcm), lambda j: (0, j))],
        [_sds((N_DEV, mod_n)), _sds((N_DEV, d))],
        [pl.BlockSpec((N_DEV, tcm), lambda j: (0, j)), pl.BlockSpec((N_DEV, d), lambda j: (0, 0))])
    (modg,) = _all_gather("gather_ada_rows", [modp])
    mod = lax.dynamic_index_in_dim(modg, idx, axis=1, keepdims=False).reshape(1, N_DEV * mod_n)
    sh1, sc1, g1, sh2, sc2, g2 = [mod[:, i * d:(i + 1) * d] for i in range(6)]

    tr = _pick(l, 256, 8)
    trh = _pick(l, 128, 8)
    nr, nrh = l // tr, l // trh
    g_mix, g_ffn, g_fin = norm_mix_g + tok[0:1, 0:1], norm_ffn_g, final_g.reshape(1, d)

    def with_t(fn):
        def wrapped(*vals):
            out = fn(*vals)
            out = out if isinstance(out, tuple) else (out,)
            return out + (out[-1].T,)
        return wrapped

    h1, h1_t = _tile_call("norm_mod_mix", with_t(_norm_mod), (1, nr), [xs, g_mix, sc1, sh1],
                          [_t(tr, d), _v(d), _v(d), _v(d)], [_sds((l, d), BF16), _sds((d, l), BF16)],
                          [_t(tr, d), _tt(tr, d)])
    proj = _matmul("proj_in", h1, full['w_in'], "nn", tn=1280)

    def heads(z, n):
        return z.reshape(l, n, HEAD_DIM).transpose(1, 0, 2)

    qh = heads(proj[:, :attn_w], hq)
    kh = heads(proj[:, off_k:off_k + kv_w], N_KV_HEADS)
    vh = heads(proj[:, off_v:off_v + kv_w], N_KV_HEADS)
    sinks3 = jnp.repeat(attn_sinks.reshape(N_KV_HEADS, qpk), ATT_BLOCK, axis=1)[..., None]
    o_h = _attn_fwd(qh, kh, vh, sinks3)
    o2 = o_h.transpose(1, 0, 2).reshape(l, attn_w)

    gn = (n_groups, SSM_N)
    pgn = (SSM_P, n_groups, SSM_N)
    a_re, a_im, log_dt = ssm_a_re[0], ssm_a_im[0], ssm_log_dt[0].reshape(n_groups, 1)
    b_re, b_im = ssm_b_re[0].transpose(2, 0, 1), ssm_b_im[0].transpose(2, 0, 1)
    disc_ins = [a_re, a_im, log_dt, b_re, b_im]
    disc_specs = [_full_spec(gn), _full_spec(gn), _full_spec((n_groups, 1)), _full_spec(pgn), _full_spec(pgn)]
    lam_r, lam_i, bb_r, bb_i = _tile_call(
        "s5_discretise", _s5_disc_fn, (1,), disc_ins, disc_specs,
        [_sds(gn), _sds(gn), _sds(pgn), _sds(pgn)],
        [_full_spec(gn), _full_spec(gn), _full_spec(pgn), _full_spec(pgn)])

    def tiles_gpn(z):
        return z.reshape(SSM_P, nj, TILE_GROUPS, SSM_N).transpose(1, 2, 0, 3)

    bd = jnp.concatenate([_block_diag(tiles_gpn(bb_r)), _block_diag(tiles_gpn(bb_i))], axis=2).astype(BF16)
    c_r = ssm_c_re[0].reshape(nj, TILE_GROUPS, SSM_P, SSM_N).transpose(0, 1, 3, 2)
    c_i = (-ssm_c_im[0]).reshape(nj, TILE_GROUPS, SSM_P, SSM_N).transpose(0, 1, 3, 2)
    cbd = jnp.concatenate([_block_diag(c_r), _block_diag(c_i)], axis=1).astype(BF16)
    bdt, cbdt = bd.transpose(0, 2, 1), cbd.transpose(0, 2, 1)
    lam = jnp.stack([lam_r.reshape(nj, TILE_STATES), lam_i.reshape(nj, TILE_STATES)], axis=1)
    dvec = ssm_d[0].reshape(nj, 1, LANES)
    t_len = _pick(l, 512, 8)
    y, hst = _s5_fwd(proj, off_u, bd, cbd, lam, dvec, t_len)

    tcs, trg = _pick(ssm_w, 1024), _pick(l, 512, 8)
    gy = _tile_call("gelu", lambda v: jax.nn.gelu(v), (ssm_w // tcs, l // trg), [y], [_t(trg, tcs)],
                    [_sds((l, ssm_w), BF16)], [_t(trg, tcs)])[0]
    full.update(zip(mixer_w, _exchange_wait("gather_mixer_wait", h_mixer, gy)))
    full['w_out'] = full['w_out'].reshape(row_sharded['w_out'])
    attn = _matmul("attn_proj", o2, full['w_attn_proj'], "nn")
    glu = _matmul("ssm_glu", gy, full['w_ssm_glu'], "nn")

    tcd = 256 if d % 256 == 0 and off_ga % 256 == 0 else LANES
    assert d % tcd == 0 and off_ga % tcd == 0 and off_gs % tcd == 0
    trm = _pick(l, 1024, 8)
    mix_in_specs = [_t(trm, tcd), _t(trm, tcd, d), _t(trm, tcd), _t(trm, tcd, off_ga), _t(trm, tcd, off_gs)]
    mixed = _tile_call("gate_mix", _mix_fn, (d // tcd, l // trm), [glu, glu, attn, proj, proj], mix_in_specs,
                       [_sds((l, d), BF16)], [_t(trm, tcd)])[0]
    mixout = _matmul("mix_out", mixed, full['w_out'], "nn")

    def res_norm_fn(xv, mo, g1v, gv, scv, shv):
        x2v = xv + g1v * mo
        return x2v, _norm_mod(x2v, gv, scv, shv)

    x2, h2, h2_t = _tile_call("residual_norm_mod_ffn", with_t(res_norm_fn), (1, nr), [xs, mixout, g1, g_ffn, sc2, sh2],
                              [_t(tr, d), _t(tr, d), _v(d), _v(d), _v(d), _v(d)],
                              [_sds((l, d)), _sds((l, d), BF16), _sds((d, l), BF16)],
                              [_t(tr, d), _t(tr, d), _tt(tr, d)])
    full['w_ffn_up'], = _exchange_wait("gather_ffn_up_wait", h_up, h2)
    up = _matmul("ffn_up", h2, full['w_ffn_up'], "nn", out_dtype=BF16, tn=1408)

    tcf, trc = _pick(dff, 1408), _pick(l, 512, 8)
    assert dff % tcf == 0
    ncf = dff // tcf

    taps = [conv_w[i:i + 1] for i in range(3)]

    def conv_gate(gp, gp_prev, w0, w1, w2, bv):
        gp = gp.astype(F32)
        prev = jnp.where(pl.program_id(1) == 0, 0.0, 1.0) * gp_prev.astype(F32)
        ext = jnp.concatenate([prev, gp], axis=0)
        m1 = pltpu.roll(ext, 1, 0)[HALO:]
        m2 = pltpu.roll(ext, 2, 0)[HALO:]
        return w0 * m2 + w1 * m1 + w2 * gp + bv, m1, m2

    def convglu_fn(gp, gp_prev, val, w0, w1, w2, bv):
        gate, _, _ = conv_gate(gp, gp_prev, w0, w1, w2, bv)
        return gate * jax.nn.sigmoid(gate) * val.astype(F32)

    act, act_t = _tile_call("conv_swiglu", with_t(convglu_fn), (ncf, l // trc), [up, up, up] + taps + [conv_b],
                            [_t(trc, tcf), _prev_rows(trc, tcf), _t(trc, tcf, dff)] + [_v(tcf)] * 4,
                            [_sds((l, dff), BF16), _sds((dff, l), BF16)], [_t(trc, tcf), _tt(trc, tcf)])
    full['w_ffn_down'] = _exchange_wait("gather_ffn_down_wait", h_down, act)[0].reshape(row_sharded['w_ffn_down'])
    ffn = _matmul("ffn_down", act, full['w_ffn_down'], "nn", tm=512)

    def final_fn(x2v, fv, g2v, gv, tv):
        rows = x2v.shape[0]

        def loss_of(x2a, fa, g2a, ga):
            out = _rms(x2a + g2a * fa, ga)
            err = out - tv
            return 0.5 * _colsum(jnp.mean(err * err, axis=-1, keepdims=True))

        loss, vjp = jax.vjp(loss_of, x2v, fv, _bc(g2v, rows), _bc(gv, rows))
        dx3, dffn, dg2, dgf = vjp(jnp.ones((1, 1), F32))
        return jnp.broadcast_to(loss, (1, LANES)), dx3, dffn, _colsum(dg2), _colsum(dgf)

    loss_p, dx3, dffn, dg2, dg_fin = _tile_call(
        "loss_final_norm", final_fn, (1, nrh), [x2, ffn, g2, g_fin, tgt],
        [_t(trh, d), _t(trh, d), _v(d), _v(d), _t(trh, d)],
        [_sds((1, LANES)), _sds((l, d)), _sds((l, d), BF16), _sds((1, d)), _sds((1, d))],
        [_v(LANES), _t(trh, d), _t(trh, d), _v(d), _v(d)], acc=(0, 3, 4))
    loss = lax.psum(loss_p[0, 0], ("x", "y", "c"))

    dact = _matmul("d_act", dffn, full['w_ffn_down'], "nt", out_dtype=BF16, tn=1408)
    gd, gd16, pending = {}, {}, []
    dw_down, dw_down16 = _matmul("dw_ffn_down", act_t, dffn, "nn", tm=512, also_bf16=True)
    gd['w_ffn_down'], gd16['w_ffn_down'] = [z.reshape((N_DEV,) + w_ffn_down.shape[1:]) for z in (dw_down, dw_down16)]
    handle, tok = _exchange_start("grad_ffn_down_start", [gd16['w_ffn_down']], False, dw_down)
    pending.append((['w_ffn_down'], handle))
    conv_b_bwd = conv_b + tok[0:1, 0:1]

    def convglu_bwd_fn(gp, gp_prev, gp_next, val, val_next, da, da_next, w0, w1, w2, bv):
        rows = gp.shape[0]
        i = pl.program_id(1)
        gp, val, da = gp.astype(F32), val.astype(F32), da.astype(F32)
        prev = jnp.where(i == 0, 0.0, 1.0) * gp_prev.astype(F32)
        more = jnp.where(i == pl.num_programs(1) - 1, 0.0, 1.0)
        ext = jnp.concatenate([prev, gp, gp_next.astype(F32)], axis=0)
        cur = ext[HALO:]
        m1 = pltpu.roll(ext, 1, 0)[HALO:]
        m2 = pltpu.roll(ext, 2, 0)[HALO:]
        gate = w0 * m2 + w1 * m1 + w2 * cur + bv
        sg = jax.nn.sigmoid(gate)
        val_e = jnp.concatenate([val, val_next.astype(F32)], axis=0)
        da_e = jnp.concatenate([da, more * da_next.astype(F32)], axis=0)
        dgate = da_e * val_e * (sg * (1.0 + gate * (1.0 - sg)))
        p1 = pltpu.roll(dgate, rows + HALO - 1, 0)[:rows]
        p2 = pltpu.roll(dgate, rows + HALO - 2, 0)[:rows]
        dg = dgate[:rows]
        dgp = w2 * dg + w1 * p1 + w0 * p2
        dval = da * (gate[:rows] * sg[:rows])
        return (jnp.stack([dgp, dval], axis=0), _colsum(dg), _colsum(dg * m2[:rows]), _colsum(dg * m1[:rows]),
                _colsum(dg * gp))

    dup, dconv_b, dcw0, dcw1, dcw2 = _tile_call(
        "conv_swiglu_bwd", convglu_bwd_fn, (ncf, nr), [up, up, up, up, up, dact, dact] + taps + [conv_b_bwd],
        [_t(tr, tcf), _prev_rows(tr, tcf), _next_rows(tr, tcf, l), _t(tr, tcf, dff), _next_rows(tr, tcf, l, dff),
         _t(tr, tcf), _next_rows(tr, tcf, l)] + [_v(tcf)] * 4,
        [_sds((2, l, dff), BF16)] + [_sds((1, dff))] * 4, [_st(tr, tcf)] + [_v(tcf)] * 4, acc=(1, 2, 3, 4))
    dh2 = _matmul("d_h2", dup, full['w_ffn_up'], "nt")
    gd['w_ffn_up'], gd16['w_ffn_up'] = _matmul("dw_ffn_up", h2_t, dup, "nn", tm=512, tn=1408, out_stack=N_DEV, also_bf16=True)
    handle, tok = _exchange_start("grad_ffn_up_start", [gd16['w_ffn_up']], False, gd['w_ffn_up'])
    pending.append((['w_ffn_up'], handle))
    g_ffn_bwd = g_ffn + tok[0:1, 0:1]

    def res_norm_bwd_fn(xv, mo, g1v, gv, scv, shv, dhv, dxv):
        rows = xv.shape[0]
        _, vjp = jax.vjp(res_norm_fn, xv, mo, _bc(g1v, rows), _bc(gv, rows), _bc(scv, rows), _bc(shv, rows))
        dx, dmo, dg1v, dgv, dscv, dshv = vjp((dxv, dhv))
        return dx, dmo, _colsum(dg1v), _colsum(dgv), _colsum(dscv), _colsum(dshv)

    dx2, dmixout, dg1, dg_ffn, dsc2, dsh2 = _tile_call(
        "residual_norm_mod_ffn_bwd", res_norm_bwd_fn, (1, nrh), [xs, mixout, g1, g_ffn_bwd, sc2, sh2, dh2, dx3],
        [_t(trh, d), _t(trh, d), _v(d), _v(d), _v(d), _v(d), _t(trh, d), _t(trh, d)],
        [_sds((l, d)), _sds((l, d), BF16)] + [_sds((1, d))] * 4,
        [_t(trh, d), _t(trh, d)] + [_v(d)] * 4, acc=(2, 3, 4, 5))

    dmixed = _matmul("d_mixed", dmixout, full['w_out'], "nt")
    dw_out, dw_out16 = _matmul("dw_out", mixed, dmixout, "tn", also_bf16=True)
    gd['w_out'], gd16['w_out'] = [z.reshape((N_DEV,) + w_out.shape[1:]) for z in (dw_out, dw_out16)]

    def mix_bwd_fn(ga_, gb_, at, pa, ps, dm):
        _, vjp = jax.vjp(_mix_fn, ga_, gb_, at, pa, ps)
        da, db, dat, dpa, dps = vjp(dm)
        return jnp.stack([da, db], axis=0), dat, dpa, dps

    dglu, dattn, dga, dgs = _tile_call(
        "gate_mix_bwd", mix_bwd_fn, (d // tcd, l // trm), [glu, glu, attn, proj, proj, dmixed],
        mix_in_specs + [_t(trm, tcd)],
        [_sds((2, l, d), BF16)] + [_sds((l, d), BF16)] * 3, [_st(trm, tcd)] + [_t(trm, tcd)] * 3)

    dgy = _matmul("d_gelu_y", dglu, full['w_ssm_glu'], "nt")
    gd['w_ssm_glu'], gd16['w_ssm_glu'] = _matmul("dw_ssm_glu", gy, dglu, "tn", out_stack=N_DEV, also_bf16=True)

    def gelu_bwd_fn(yv, dv):
        _, vjp = jax.vjp(lambda z: jax.nn.gelu(z), yv)
        return vjp(dv)[0]

    dy = _tile_call("gelu_bwd", gelu_bwd_fn, (ssm_w // tcs, l // trg), [y, dgy], [_t(trg, tcs), _t(trg, tcs)],
                    [_sds((l, ssm_w))], [_t(trg, tcs)])[0]
    du, dbd, dcbdt, dlam, dd_tiles = _s5_bwd(proj, off_u, dy, hst, bd, bdt, cbdt, lam, dvec, t_len)

    def gpn_of(z):
        return z.transpose(2, 0, 1, 3).reshape(pgn)

    dbb_r = gpn_of(_diag_blocks(dbd[:, :, :TILE_STATES], SSM_P))
    dbb_i = gpn_of(_diag_blocks(dbd[:, :, TILE_STATES:], SSM_P))
    dc_re = _diag_blocks(dcbdt[:, :, :TILE_STATES], SSM_P).reshape(n_groups, SSM_P, SSM_N)
    dc_im = _diag_blocks(dcbdt[:, :, TILE_STATES:], SSM_P).reshape(n_groups, SSM_P, SSM_N)
    dlam_r, dlam_i = dlam[:, 0].reshape(gn), dlam[:, 1].reshape(gn)

    def disc_bwd_fn(ar, ai, ld, br, bi, dlr, dli, dbr, dbi):
        _, vjp = jax.vjp(_s5_disc_fn, ar, ai, ld, br, bi)
        return vjp((dlr, dli, dbr, dbi))

    da_re, da_im, dlog_dt, db_re, db_im = _tile_call(
        "s5_discretise_bwd", disc_bwd_fn, (1,), disc_ins + [dlam_r, dlam_i, dbb_r, dbb_i],
        disc_specs + [_full_spec(gn), _full_spec(gn), _full_spec(pgn), _full_spec(pgn)],
        [_sds(gn), _sds(gn), _sds((n_groups, 1)), _sds(pgn), _sds(pgn)], disc_specs)

    do2 = _matmul("d_attn_heads", dattn, full['w_attn_proj'], "nt")
    gd['w_attn_proj'], gd16['w_attn_proj'] = _matmul("dw_attn_proj", o2, dattn, "tn", out_stack=N_DEV, also_bf16=True)
    handle, tok = _exchange_start("grad_mixer_start", [gd16[k] for k in mixer_w], False, gd['w_attn_proj'])
    pending.append((mixer_w, handle))
    do_h = heads(do2.astype(BF16), hq)
    dq_h, dk_h, dv_h, dsink = _attn_bwd(qh, kh, vh, sinks3 + tok[0:1, 0:1], do_h)

    def unheads(z):
        return z.transpose(1, 0, 2).reshape(l, z.shape[0] * HEAD_DIM)

    dproj = jnp.concatenate([unheads(dq_h), unheads(dk_h), unheads(dv_h), du, dga, dgs], axis=1)
    dw_in, dw_in16 = _matmul("dw_in", h1_t, dproj, "nn", tm=512, tn=1280, also_bf16=True)
    dcw = jnp.concatenate([dcw0, dcw1, dcw2], axis=0)
    shard_in, shard_cw = w_in.shape[1:], ffn_conv_w.shape[1:]
    gd16['w_in'] = dw_in16.reshape(shard_in[0], N_DEV, shard_in[1]).transpose(1, 0, 2)
    own_in = lax.dynamic_slice_in_dim(dw_in, idx * shard_in[1], shard_in[1], axis=1)[None]
    gd['ffn_conv_w'] = dcw.reshape(shard_cw[0], N_DEV, shard_cw[1]).transpose(1, 0, 2)
    gd16['ffn_conv_w'] = gd['ffn_conv_w'].astype(BF16)
    handle, tok = _exchange_start("grad_in_start", [gd16['w_in'], gd16['ffn_conv_w']], False, dw_in)
    pending.append((['w_in', 'ffn_conv_w'], handle))
    dh1 = _matmul("d_h1", dproj, full['w_in'], "nt", tm=512, dep=tok)

    def norm_bwd_fn(xv, gv, scv, shv, dhv, dxv):
        rows = xv.shape[0]
        _, vjp = jax.vjp(_norm_mod, xv, _bc(gv, rows), _bc(scv, rows), _bc(shv, rows))
        dx, dgv, dscv, dshv = vjp(dhv)
        return dx + dxv, _colsum(dgv), _colsum(dscv), _colsum(dshv)

    grad_x, dg_mix, dsc1, dsh1 = _tile_call(
        "norm_mod_mix_bwd", norm_bwd_fn, (1, nrh), [xs, g_mix, sc1, sh1, dh1, dx2],
        [_t(trh, d), _v(d), _v(d), _v(d), _t(trh, d), _t(trh, d)],
        [_sds((l, d))] + [_sds((1, d))] * 3, [_t(trh, d)] + [_v(d)] * 3, acc=(1, 2, 3))

    dmod = jnp.concatenate([dsh1, dsc1, dg1, dsh2, dsc2, dg2], axis=1)
    small = ['ada_b', 'norm_mix_g', 'attn_sinks', 'ssm_a_re', 'ssm_a_im', 'ssm_log_dt', 'ssm_b_re', 'ssm_b_im',
             'ssm_c_re', 'ssm_c_im', 'ssm_d', 'norm_ffn_g', 'ffn_conv_b', 'final_g']
    small_grads = {
        'ada_b': dmod, 'norm_mix_g': dg_mix, 'attn_sinks': dsink[:, 0, 0], 'ssm_a_re': da_re, 'ssm_a_im': da_im,
        'ssm_log_dt': dlog_dt, 'ssm_b_re': db_re.transpose(1, 2, 0), 'ssm_b_im': db_im.transpose(1, 2, 0),
        'ssm_c_re': dc_re, 'ssm_c_im': dc_im, 'ssm_d': dd_tiles, 'norm_ffn_g': dg_ffn, 'ffn_conv_b': dconv_b,
        'final_g': dg_fin}
    gs_pack, sm_offs = _pack([small_grads[k] for k in small], LANES, 8)

    sharded = big + ['ffn_conv_w']
    sharded_out = {}

    def finish(group, handle, after):
        for k, parts in zip(group, _exchange_wait("grad_" + group[0] + "_wait", handle, after)):
            own_src, own_at = (own_in, 0 * idx) if k == 'w_in' else (gd[k], idx)
            sharded_out[k] = _adamw_sharded("adamw_" + k, parts, own_src, given[k][0], given['m_' + k][0],
                                            given['v_' + k][0], jnp.stack([idx, own_at]).astype(jnp.int32))

    for group, handle in pending[:-1]:
        finish(group, handle, grad_x)
    (gs_all,) = _all_gather("gather_small_grads", [gs_pack], dep=sharded_out[pending[-2][0][-1]][1])
    ws_pack, _ = _pack([given[k] for k in small], LANES, 8)
    ms_pack, _ = _pack([given['m_' + k] for k in small], LANES, 8)
    vs_pack, _ = _pack([given['v_' + k] for k in small], LANES, 8)
    small_out = _adamw("adamw_replicated", gs_all, ws_pack, ms_pack, vs_pack)

    dmod_all = _unpack(gs_all, sm_offs[0], (N_DEV * mod_n,), lead=(N_DEV,))
    dmod_mine = lax.dynamic_slice_in_dim(dmod_all, idx * mod_n, mod_n, axis=1)
    kpad = LANES - N_DEV
    cond_t = jnp.pad(cond_all.T, ((0, 0), (0, kpad)))
    dmod_pad = jnp.pad(dmod_mine, ((0, kpad), (0, 0)))
    g_ada_w = _matmul("dw_ada", cond_t, dmod_pad, "nn")
    ada_out = _adamw("adamw_ada_w", g_ada_w[None], ada_w[0], m_ada_w[0], v_ada_w[0])

    finish(*pending[-1], ada_out[0])

    results = [{}, {}, {}, {}]
    for which in range(4):
        for k, off in zip(small, sm_offs):
            results[which][k] = _unpack(small_out[which], off, given[k].shape)
        for k in sharded:
            results[which][k] = sharded_out[k][which][None]
        results[which]['ada_w'] = ada_out[which][None]
    outs = [loss, grad_x[None]]
    for which in range(4):
        outs += [results[which][k] for k in names]
    return tuple(outs)
```

```python
import functools
import math

import jax
import jax.numpy as jnp
from jax import lax
from jax.experimental import pallas as pl
from jax.experimental.pallas import tpu as pltpu

F32, BF16 = jnp.float32, jnp.bfloat16
MESH = pl.DeviceIdType.MESH
N_DEV = 8

HEAD_DIM = 64
N_KV_HEADS = 2
ATT_BLOCK = 128
NEG_INF = -1e30
SSM_P = 16
SSM_N = 64
LANES = 128
TILE_GROUPS = LANES // SSM_P
TILE_STATES = TILE_GROUPS * SSM_N
RMS_EPS = 1e-6
ADAM_LR, ADAM_B1, ADAM_B2, ADAM_EPS, ADAM_WD, ADAM_STEP = 0.001, 0.9, 0.999, 1e-08, 0.01, 10
VMEM_LIMIT = 56 * 1024 * 1024
MATMUL_VMEM_BUDGET = 44 * 1024 * 1024


def _params(n_axes):
    return pltpu.CompilerParams(dimension_semantics=("arbitrary",) * n_axes, vmem_limit_bytes=VMEM_LIMIT)


def _pick(dim, pref, align=128):
    if dim <= align:
        return dim
    t = (min(pref, dim) // align) * align
    while t > align and dim % t:
        t -= align
    assert dim % t == 0, (dim, pref, align)
    return t


def _dev():
    return lax.axis_index("x"), lax.axis_index("y"), lax.axis_index("c")


def _tile_call(name, fn, grid, ins, in_specs, out_shapes, out_specs, acc=()):
    n_in, n_out = len(ins), len(out_shapes)
    acc_axis = len(grid) - 1

    def body(*refs):
        vals = fn(*[r[...] for r in refs[:n_in]])
        if not isinstance(vals, (tuple, list)):
            vals = (vals,)
        assert len(vals) == n_out
        for i, (r, v) in enumerate(zip(refs[n_in:], vals)):
            v = v.astype(r.dtype)
            if i in acc:
                first = pl.program_id(acc_axis) == 0

                @pl.when(first)
                def _():
                    r[...] = v

                @pl.when(jnp.logical_not(first))
                def _():
                    r[...] += v
            else:
                r[...] = v

    return pl.pallas_call(
        body, grid=grid, in_specs=in_specs, out_specs=out_specs, out_shape=out_shapes, name=name,
        compiler_params=_params(len(grid)),
    )(*ins)


def _t(tr, tc, off=0):
    return pl.BlockSpec((tr, tc), lambda j, i: (i, j + off // tc))


def _tt(tr, tc):
    return pl.BlockSpec((tc, tr), lambda j, i: (j, i))


def _v(tc, off=0, rows=1):
    return pl.BlockSpec((rows, tc), lambda j, i: (0, j + off // tc))


HALO = 16


def _prev_rows(tr, tc, off=0):
    return pl.BlockSpec((HALO, tc), lambda j, i: (jnp.maximum(i * (tr // HALO) - 1, 0), j + off // tc))


def _next_rows(tr, tc, nrows, off=0):
    return pl.BlockSpec((HALO, tc),
                        lambda j, i: (jnp.minimum((i + 1) * (tr // HALO), nrows // HALO - 1), j + off // tc))


def _st(tr, tc):
    return pl.BlockSpec((2, tr, tc), lambda j, i: (0, i, j))


def _bc(v, rows):
    return jnp.broadcast_to(v, (rows, v.shape[-1]))


def _colsum(v):
    return jnp.sum(v, axis=0, keepdims=True)


def _matmul(name, a, b, mode, out_dtype=F32, tm=1024, tn=1024, tk=None, out_stack=None, also_bf16=False, dep=None):
    def dims(z):
        return (z.shape[-2], z.shape[-1] * (z.shape[0] if z.ndim == 3 else 1))

    ar, ac = dims(a)
    br, bc = dims(b)
    if mode == "nn":
        m, k, n = ar, ac, bc
        assert br == k
    elif mode == "nt":
        m, k, n = ar, ac, br
        assert bc == k
    else:
        m, k, n = ac, ar, bc
        assert br == k
    m_lim, k_lim, n_lim = [m], [k], [n]
    if a.ndim == 3:
        (m_lim if mode == "tn" else k_lim).append(a.shape[-1])
    if b.ndim == 3:
        (k_lim if mode == "nt" else n_lim).append(b.shape[-1])
    if out_stack:
        n_lim.append(n // out_stack)
    tm = _pick(functools.reduce(math.gcd, m_lim), tm)
    tn = _pick(functools.reduce(math.gcd, n_lim), tn)
    k_unit = functools.reduce(math.gcd, k_lim)
    if tk is None:
        sa, sb, so = a.dtype.itemsize, b.dtype.itemsize, jnp.dtype(out_dtype).itemsize + (2 if also_bf16 else 0)
        fits = [t for t in range(LANES, k_unit + 1, LANES) if k_unit % t == 0 and
                2 * t * (tm * sa + tn * sb) + tm * tn * (2 * so + (4 if t < k else 0)) <= MATMUL_VMEM_BUDGET]
        tk = max(fits) if fits else _pick(k_unit, 512)
    else:
        tk = _pick(k_unit, tk)
    nk = k // tk

    def spec(z, brows, bcols, ridx, cidx):
        if z.ndim == 3:
            per = z.shape[-1] // bcols
            return pl.BlockSpec((None, brows, bcols),
                                lambda i, j, kk: (cidx(i, j, kk) // per, ridx(i, j, kk), cidx(i, j, kk) % per))
        return pl.BlockSpec((brows, bcols), lambda i, j, kk: (ridx(i, j, kk), cidx(i, j, kk)))

    gi = lambda i, j, kk: i
    gj = lambda i, j, kk: j
    gk = lambda i, j, kk: kk
    if mode == "nn":
        a_spec, b_spec = spec(a, tm, tk, gi, gk), spec(b, tk, tn, gk, gj)
        dn = (((1,), (0,)), ((), ()))
    elif mode == "nt":
        a_spec, b_spec = spec(a, tm, tk, gi, gk), spec(b, tn, tk, gj, gk)
        dn = (((1,), (1,)), ((), ()))
    else:
        a_spec, b_spec = spec(a, tk, tm, gk, gi), spec(b, tk, tn, gk, gj)
        dn = (((0,), (0,)), ((), ()))

    n_out = 2 if also_bf16 else 1

    deps = [] if dep is None else [dep]

    def body(a_ref, b_ref, *rest):
        rest = rest[len(deps):]
        o_refs, acc = rest[:n_out], rest[n_out:]
        part = lax.dot_general(a_ref[...].astype(BF16), b_ref[...].astype(BF16), dn, preferred_element_type=F32)

        def emit(val):
            for o_ref in o_refs:
                o_ref[...] = val.astype(o_ref.dtype)

        if nk == 1:
            emit(part)
            return
        acc_ref, = acc
        kk = pl.program_id(2)

        @pl.when(kk == 0)
        def _():
            acc_ref[...] = part

        @pl.when(kk > 0)
        def _():
            acc_ref[...] += part

        @pl.when(kk == nk - 1)
        def _():
            emit(acc_ref[...])

    if out_stack:
        per = (n // out_stack) // tn
        out_spec = pl.BlockSpec((None, tm, tn), lambda i, j, kk: (j // per, i, j % per))
        shape = (out_stack, m, n // out_stack)
    else:
        out_spec = pl.BlockSpec((tm, tn), lambda i, j, kk: (i, j))
        shape = (m, n)
    dtypes = [out_dtype, BF16][:n_out]
    res = pl.pallas_call(
        body, grid=(m // tm, n // tn, nk),
        in_specs=[a_spec, b_spec] + [pl.BlockSpec(memory_space=pl.ANY)] * len(deps), out_specs=[out_spec] * n_out,
        out_shape=[jax.ShapeDtypeStruct(shape, dt) for dt in dtypes],
        scratch_shapes=[pltpu.VMEM((tm, tn), F32)] if nk > 1 else [], name=name, compiler_params=_params(3),
    )(a, b, *deps)
    return res if also_bf16 else res[0]


def _all_gather(name, arrs, dep=None):
    n = len(arrs)
    deps = [] if dep is None else [dep]

    def body(*refs):
        ins, outs = refs[:n], refs[n + len(deps):2 * n + len(deps)]
        send_sems, recv_sems, local_sems = refs[2 * n + len(deps):]
        x, y, c = _dev()
        me, sib = (x, y, c), (x, y, 1 - c)
        chips = [(1 - x, y), (x, 1 - y), (1 - x, 1 - y)]

        def slot(p):
            return 4 * p[0] + 2 * p[1] + p[2]

        def copy(a, k, block, to, src=None):
            dst = outs[a].at[slot(block)]
            return pltpu.make_async_remote_copy(
                src_ref=dst if src is None else src, dst_ref=dst,
                send_sem=send_sems.at[7 * a + k], recv_sem=recv_sems.at[7 * a + k],
                device_id=to, device_id_type=MESH)

        mine = [pltpu.make_async_copy(ins[a], outs[a].at[slot(me)], local_sems.at[a]) for a in range(n)]
        for cp in mine:
            cp.start()
        first = []
        for a in range(n):
            first.append(copy(a, 0, me, sib, src=ins[a]))
            first += [copy(a, 1 + j, me, (*chip, c), src=ins[a]) for j, chip in enumerate(chips)]
        for cp in first:
            cp.start()
        passed = []
        for j, chip in enumerate(chips):
            for a in range(n):
                copy(a, 1 + j, (*chip, c), me).wait_recv()
                cp = copy(a, 4 + j, (*chip, c), sib)
                cp.start()
                passed.append(cp)
        for a in range(n):
            copy(a, 0, sib, me).wait_recv()
            for j, chip in enumerate(chips):
                copy(a, 4 + j, (*chip, 1 - c), me).wait_recv()
        for cp in first + passed:
            cp.wait_send()
        for cp in mine:
            cp.wait()

    any_spec = pl.BlockSpec(memory_space=pl.ANY)
    return pl.pallas_call(
        body, in_specs=[any_spec] * (n + len(deps)), out_specs=[any_spec] * n,
        out_shape=[jax.ShapeDtypeStruct((N_DEV,) + a.shape, a.dtype) for a in arrs],
        scratch_shapes=[pltpu.SemaphoreType.DMA((7 * n,)), pltpu.SemaphoreType.DMA((7 * n,)),
                        pltpu.SemaphoreType.DMA((n,))],
        name=name,
    )(*arrs, *deps)


def _grad_to_sibling(gds):
    n = len(gds)

    def body(*refs):
        g_refs, r_refs = refs[:n], refs[n:2 * n]
        send_sems, recv_sems = refs[2 * n:]
        x, y, c = _dev()
        cps = []
        for a in range(n):
            for k in range(4):
                cp = pltpu.make_async_remote_copy(
                    src_ref=g_refs[a].at[2 * k + (1 - c)], dst_ref=r_refs[a].at[k],
                    send_sem=send_sems.at[4 * a + k], recv_sem=recv_sems.at[4 * a + k],
                    device_id=(x, y, 1 - c), device_id_type=MESH)
                cp.start()
                cps.append(cp)
        for cp in cps:
            cp.wait()

    any_spec = pl.BlockSpec(memory_space=pl.ANY)
    return pl.pallas_call(
        body, in_specs=[any_spec] * n, out_specs=[any_spec] * n,
        out_shape=[jax.ShapeDtypeStruct((4,) + g.shape[1:], g.dtype) for g in gds],
        scratch_shapes=[pltpu.SemaphoreType.DMA((4 * n,)), pltpu.SemaphoreType.DMA((4 * n,))],
        name="grad_to_sibling",
    )(*gds)


def _chip_sum(name, gd, from_sib, c_arr):
    _, k, n = gd.shape
    tr = _pick(k, max(16, (1 << 20) // (4 * n)), 16)

    def body(c_ref, a_ref, b_ref, o_ref):
        o_ref[...] = (a_ref[...] + b_ref[...]).astype(o_ref.dtype)

    return pl.pallas_call(
        body,
        grid_spec=pltpu.PrefetchScalarGridSpec(
            num_scalar_prefetch=1, grid=(4, k // tr),
            in_specs=[pl.BlockSpec((1, tr, n), lambda q, i, cr: (2 * q + cr[0], i, 0)),
                      pl.BlockSpec((1, tr, n), lambda q, i, cr: (q, i, 0))],
            out_specs=pl.BlockSpec((1, tr, n), lambda q, i, cr: (q, i, 0))),
        out_shape=jax.ShapeDtypeStruct((4, k, n), BF16), name=name, compiler_params=_params(2),
    )(c_arr, gd, from_sib)


def _grad_to_chips(sums):
    n = len(sums)

    def body(*refs):
        s_refs, p_refs = refs[:n], refs[n:2 * n]
        send_sems, recv_sems, local_sems = refs[2 * n:]
        x, y, c = _dev()
        my_chip = 2 * x + y
        cps = []
        for a in range(n):
            local = pltpu.make_async_copy(s_refs[a].at[my_chip], p_refs[a].at[my_chip], local_sems.at[a])
            local.start()
            cps.append(local)
            for j, (px, py) in enumerate([(1 - x, y), (x, 1 - y), (1 - x, 1 - y)]):
                cp = pltpu.make_async_remote_copy(
                    src_ref=s_refs[a].at[2 * px + py], dst_ref=p_refs[a].at[my_chip],
                    send_sem=send_sems.at[3 * a + j], recv_sem=recv_sems.at[3 * a + j],
                    device_id=(px, py, c), device_id_type=MESH)
                cp.start()
                cps.append(cp)
        for cp in cps:
            cp.wait()

    any_spec = pl.BlockSpec(memory_space=pl.ANY)
    return pl.pallas_call(
        body, in_specs=[any_spec] * n, out_specs=[any_spec] * n,
        out_shape=[jax.ShapeDtypeStruct(s.shape, s.dtype) for s in sums],
        scratch_shapes=[pltpu.SemaphoreType.DMA((3 * n,)), pltpu.SemaphoreType.DMA((3 * n,)),
                        pltpu.SemaphoreType.DMA((n,))],
        name="grad_to_chips",
    )(*sums)


FLIPS = [(0, 0, 1), (0, 1, 0), (1, 0, 0), (0, 1, 1), (1, 0, 1), (1, 1, 0), (1, 1, 1)]
N_PEERS = len(FLIPS)
_HBM = pl.BlockSpec(memory_space=pltpu.HBM)
_SEM = pl.BlockSpec(memory_space=pltpu.SEMAPHORE)
_EFFECT = pltpu.SideEffectType.DATAFLOW_SIDE_EFFECTING


def _flip(x, y, c, f):
    return (1 - x if f[0] else x, 1 - y if f[1] else y, 1 - c if f[2] else c)


def _slot(p):
    return 4 * p[0] + 2 * p[1] + p[2]


def _exchange_copies(src_refs, land_refs, send_sems, recv_sems, gather):
    x, y, c = _dev()
    mine = _slot((x, y, c))
    cps = []
    for a, (src, land) in enumerate(zip(src_refs, land_refs)):
        for k, f in enumerate(FLIPS):
            peer = _flip(x, y, c, f)
            cps.append(pltpu.make_async_remote_copy(
                src_ref=src if gather else src.at[_slot(peer)], dst_ref=land.at[mine],
                send_sem=send_sems.at[N_PEERS * a + k], recv_sem=recv_sems.at[N_PEERS * a + k],
                device_id=peer, device_id_type=MESH))
    return cps


def _exchange_start(name, srcs, gather, after):
    n = len(srcs)
    lands = [lax.empty(((N_DEV,) + s.shape) if gather else s.shape, s.dtype) for s in srcs]

    def body(*refs):
        src_refs, land_refs = refs[:n], refs[n:2 * n]
        send_sems, recv_sems, local_sems = refs[2 * n + 1:2 * n + 4]
        token = refs[-1]
        if gather:
            x, y, c = _dev()
            for a in range(n):
                pltpu.make_async_copy(src_refs[a], land_refs[a].at[_slot((x, y, c))], local_sems.at[a]).start()
        for cp in _exchange_copies(src_refs, land_refs, send_sems, recv_sems, gather):
            cp.start()
        token[...] = jnp.zeros_like(token)

    hbm = lambda z: pltpu.HBM(z.shape, z.dtype)
    outs = pl.pallas_call(
        body, name=name,
        out_shape=(pltpu.SemaphoreType.DMA((N_PEERS * n,)), pltpu.SemaphoreType.DMA((N_PEERS * n,)),
                   pltpu.SemaphoreType.DMA((n,)), *[hbm(s) for s in srcs], *[hbm(z) for z in lands],
                   jax.ShapeDtypeStruct((8, LANES), F32)),
        in_specs=[_HBM] * (2 * n) + [pl.BlockSpec(memory_space=pl.ANY)],
        out_specs=(_SEM, _SEM, _SEM, *[_HBM] * (2 * n), pl.BlockSpec(memory_space=pltpu.VMEM)),
        input_output_aliases={i: 3 + i for i in range(2 * n)},
        compiler_params=pltpu.CompilerParams(has_side_effects=_EFFECT),
    )(*[pltpu.with_memory_space_constraint(z, pltpu.HBM) for z in list(srcs) + lands], after)
    return (outs[:3], outs[3:3 + n], outs[3 + n:3 + 2 * n], gather), outs[-1]


def _exchange_wait(name, handles, after):
    sems, srcs, lands, gather = handles
    n = len(srcs)

    def body(*refs):
        src_refs, land_refs = refs[:n], refs[n:2 * n]
        send_sems, recv_sems, local_sems = refs[2 * n:2 * n + 3]
        if gather:
            for a in range(n):
                pltpu.make_async_copy(src_refs[a], land_refs[a].at[0], local_sems.at[a]).wait()
        for cp in _exchange_copies(src_refs, land_refs, send_sems, recv_sems, gather):
            cp.wait_send()
            cp.wait_recv()

    hbm = lambda z: pltpu.HBM(z.shape, z.dtype)
    outs = pl.pallas_call(
        body, name=name, out_shape=tuple(hbm(z) for z in list(srcs) + list(lands)),
        in_specs=[_HBM] * (2 * n) + [_SEM] * 3 + [pl.BlockSpec(memory_space=pl.ANY)],
        out_specs=tuple([_HBM] * (2 * n)), input_output_aliases={i: i for i in range(2 * n)},
        compiler_params=pltpu.CompilerParams(has_side_effects=_EFFECT),
    )(*srcs, *lands, *sems, after)
    return list(outs[n:])


def _item_rows(size, width, row_align):
    return -(-size // (width * row_align)) * row_align


def _pack_rows(sizes, width, row_align):
    offs, r = [], 0
    for s in sizes:
        offs.append(r)
        r += _item_rows(s, width, row_align)
    return offs, r


def _pack(items, width, row_align, lead=()):
    nl = len(lead)
    sizes = [int(jnp.size(a)) // max(1, functools.reduce(lambda p, q: p * q, lead, 1)) for a in items]
    offs, total = _pack_rows(sizes, width, row_align)
    blocks = []
    for a, s in zip(items, sizes):
        f = a.reshape(lead + (s,))
        rows = _item_rows(s, width, row_align)
        if rows * width != s:
            f = jnp.pad(f, [(0, 0)] * nl + [(0, rows * width - s)])
        blocks.append(f.reshape(lead + (rows, width)))
    return jnp.concatenate(blocks, axis=-2), offs


def _unpack(packed, off, shape, lead=()):
    nl = len(lead)
    size = functools.reduce(lambda p, q: p * q, shape, 1)
    width = packed.shape[-1]
    rows = -(-size // width)
    blk = lax.slice_in_dim(packed, off, off + rows, axis=nl).reshape(lead + (rows * width,))
    return lax.slice_in_dim(blk, 0, size, axis=nl).reshape(lead + tuple(shape))


def _rms(x, g):
    return (x * lax.rsqrt(jnp.mean(x * x, axis=-1, keepdims=True) + RMS_EPS)) * g


def _norm_mod(x, g, sc, sh):
    return _rms(x, g) * (1.0 + sc) + sh


def _mix_fn(glu_a, glu_b, attn, ga, gs):
    return jax.nn.sigmoid(ga) * attn + jax.nn.sigmoid(gs) * (glu_a * jax.nn.sigmoid(glu_b))


def _s5_disc_fn(a_re, a_im, log_dt, b_re, b_im):
    dt = jnp.exp(log_dt)
    mag = jnp.exp(a_re * dt)
    lr, li = mag * jnp.cos(a_im * dt), mag * jnp.sin(a_im * dt)
    den = a_re * a_re + a_im * a_im
    zr = ((lr - 1.0) * a_re + li * a_im) / den
    zi = (li * a_re - (lr - 1.0) * a_im) / den
    return lr, li, zr[None] * b_re - zi[None] * b_im, zr[None] * b_im + zi[None] * b_re


def _adamw_fn(w, g, m, v):
    m = ADAM_B1 * m + (1.0 - ADAM_B1) * g
    v = ADAM_B2 * v + (1.0 - ADAM_B2) * jnp.square(g)
    m_hat = m / (1.0 - ADAM_B1 ** ADAM_STEP)
    v_hat = v / (1.0 - ADAM_B2 ** ADAM_STEP)
    delta = -ADAM_LR * (m_hat / (jnp.sqrt(v_hat) + ADAM_EPS) + ADAM_WD * w)
    return delta, m, v


def _adamw(name, parts, w, m, v):
    p, r, c = parts.shape
    tr = _pick(r, max(8, (1 << 21) // (4 * c * max(p, 2))), 8)

    def fn(pv, wv, mv, vv):
        g = pv[0]
        for i in range(1, p):
            g = g + pv[i]
        d, m2, v2 = _adamw_fn(wv, g, mv, vv)
        return g, d, m2, v2

    spec = pl.BlockSpec((tr, c), lambda i: (i, 0))
    return _tile_call(
        name, fn, (r // tr,), [parts, w, m, v],
        [pl.BlockSpec((p, tr, c), lambda i: (0, i, 0)), spec, spec, spec],
        [jax.ShapeDtypeStruct((r, c), F32)] * 4, [spec] * 4)


def _adamw_sharded(name, parts, own_src, w, m, v, place):
    _, k, n = parts.shape
    tr = _pick(k, max(16, (1 << 19) // (4 * n)), 16)

    def body(pl_ref, p_ref, a_ref, w_ref, m_ref, v_ref, g_ref, d_ref, m2_ref, v2_ref):
        own = a_ref[0]
        g = None
        for q in range(N_DEV):
            term = jnp.where(pl_ref[0] == q, own, p_ref[q].astype(F32))
            g = term if g is None else g + term
        d, m2, v2 = _adamw_fn(w_ref[...], g, m_ref[...], v_ref[...])
        g_ref[...] = g
        d_ref[...] = d
        m2_ref[...] = m2
        v2_ref[...] = v2

    spec = pl.BlockSpec((tr, n), lambda i, pr: (i, 0))
    return pl.pallas_call(
        body,
        grid_spec=pltpu.PrefetchScalarGridSpec(
            num_scalar_prefetch=1, grid=(k // tr,),
            in_specs=[pl.BlockSpec((N_DEV, tr, n), lambda i, pr: (0, i, 0)),
                      pl.BlockSpec((1, tr, n), lambda i, pr: (pr[1], i, 0)),
                      spec, spec, spec],
            out_specs=[spec] * 4),
        out_shape=[jax.ShapeDtypeStruct((k, n), F32)] * 4, name=name, compiler_params=_params(1),
    )(place, parts, own_src, w, m, v)


def _attn_mask(n, rows):
    qi = lax.broadcasted_iota(jnp.int32, (rows, 2 * ATT_BLOCK), 0) & (ATT_BLOCK - 1)
    kj = lax.broadcasted_iota(jnp.int32, (rows, 2 * ATT_BLOCK), 1)
    rel = qi + ATT_BLOCK - kj
    return (rel >= 0) & (rel < ATT_BLOCK) & ((kj >= ATT_BLOCK) | (n > 0))


def _attn_probs(q, k, sink, mask):
    s = lax.dot_general(q, k, (((1,), (1,)), ((), ())), preferred_element_type=F32) * (HEAD_DIM ** -0.5)
    s = jnp.where(mask, s, NEG_INF)
    m = jnp.maximum(jnp.max(s, axis=-1, keepdims=True), sink)
    p = jnp.exp(s - m)
    e_sink = jnp.exp(sink - m)
    inv = 1.0 / (jnp.sum(p, axis=-1, keepdims=True) + e_sink)
    return p * inv, e_sink * inv


def _attn_specs(qpk):
    blk = ATT_BLOCK
    q_spec = pl.BlockSpec((qpk, blk, HEAD_DIM), lambda h, n: (h, n, 0))
    cur = pl.BlockSpec((1, blk, HEAD_DIM), lambda h, n: (h, n, 0))
    prev = pl.BlockSpec((1, blk, HEAD_DIM), lambda h, n: (h, jnp.maximum(n - 1, 0), 0))
    sink_spec = pl.BlockSpec((1, qpk * blk, 1), lambda h, n: (h, 0, 0))
    return q_spec, cur, prev, sink_spec


def _attn_fwd(q, k, v, sinks):
    hq, l, _ = q.shape
    qpk = hq // N_KV_HEADS
    nb = l // ATT_BLOCK
    rows = qpk * ATT_BLOCK
    q_spec, cur, prev, sink_spec = _attn_specs(qpk)

    def body(q_ref, kp_ref, kc_ref, vp_ref, vc_ref, sink_ref, o_ref):
        mask = _attn_mask(pl.program_id(1), rows)
        kk = jnp.concatenate([kp_ref[0], kc_ref[0]], axis=0).astype(BF16)
        vv = jnp.concatenate([vp_ref[0], vc_ref[0]], axis=0).astype(BF16)
        p, _ = _attn_probs(q_ref[...].reshape(rows, HEAD_DIM).astype(BF16), kk, sink_ref[0], mask)
        o = jnp.dot(p.astype(BF16), vv, preferred_element_type=F32)
        o_ref[...] = o.reshape(qpk, ATT_BLOCK, HEAD_DIM).astype(o_ref.dtype)

    return pl.pallas_call(
        body, grid=(N_KV_HEADS, nb), in_specs=[q_spec, prev, cur, prev, cur, sink_spec],
        out_specs=q_spec, out_shape=jax.ShapeDtypeStruct((hq, l, HEAD_DIM), BF16),
        name="attn_fwd", compiler_params=_params(2),
    )(q, k, k, v, v, sinks)


def _attn_bwd(q, k, v, sinks, do):
    hq, l, _ = q.shape
    qpk = hq // N_KV_HEADS
    nb = l // ATT_BLOCK
    blk = ATT_BLOCK
    rows = qpk * blk
    q_spec, cur, prev, sink_spec = _attn_specs(qpk)
    part_spec = pl.BlockSpec((1, 1, 2 * blk, HEAD_DIM), lambda h, n: (h, n, 0, 0))
    dsink_spec = pl.BlockSpec((qpk, 1, LANES), lambda h, n: (h, 0, 0))
    tn = (((0,), (0,)), ((), ()))

    def body(q_ref, do_ref, kp_ref, kc_ref, vp_ref, vc_ref, sink_ref, dq_ref, dkp_ref, dvp_ref, dsink_ref):
        n = pl.program_id(1)
        mask = _attn_mask(n, rows)
        kk = jnp.concatenate([kp_ref[0], kc_ref[0]], axis=0).astype(BF16)
        vv = jnp.concatenate([vp_ref[0], vc_ref[0]], axis=0).astype(BF16)
        qb = q_ref[...].reshape(rows, HEAD_DIM).astype(BF16)
        do32 = do_ref[...].astype(F32).reshape(rows, HEAD_DIM)
        dob = do32.astype(BF16)
        p, p_sink = _attn_probs(qb, kk, sink_ref[0], mask)
        pb = p.astype(BF16)
        o = jnp.dot(pb, vv, preferred_element_type=F32)
        delta = jnp.sum(do32 * o, axis=-1, keepdims=True)
        dp = lax.dot_general(dob, vv, (((1,), (1,)), ((), ())), preferred_element_type=F32)
        ds = (p * (dp - delta) * (HEAD_DIM ** -0.5)).astype(BF16)
        dq = jnp.dot(ds, kk, preferred_element_type=F32)
        dq_ref[...] = dq.reshape(qpk, blk, HEAD_DIM).astype(dq_ref.dtype)
        dkp_ref[0, 0] = lax.dot_general(ds, qb, tn, preferred_element_type=F32)
        dvp_ref[0, 0] = lax.dot_general(pb, dob, tn, preferred_element_type=F32)
        dsr = p_sink * delta
        for g in range(qpk):
            dsg = jnp.broadcast_to(-_colsum(dsr[g * blk:(g + 1) * blk]), (1, LANES))

            @pl.when(n == 0)
            def _():
                dsink_ref[g] = dsg

            @pl.when(n > 0)
            def _():
                dsink_ref[g] += dsg


    part_shape = jax.ShapeDtypeStruct((N_KV_HEADS, nb, 2 * blk, HEAD_DIM), F32)
    dq, dkp, dvp, dsink = pl.pallas_call(
        body, grid=(N_KV_HEADS, nb), in_specs=[q_spec, q_spec, prev, cur, prev, cur, sink_spec],
        out_specs=[q_spec, part_spec, part_spec, dsink_spec],
        out_shape=[jax.ShapeDtypeStruct((hq, l, HEAD_DIM), BF16), part_shape, part_shape,
                   jax.ShapeDtypeStruct((hq, 1, LANES), F32)],
        name="attn_bwd", compiler_params=_params(2),
    )(q, do, k, k, v, v, sinks)

    def combine(a_cur, a_nxt, b_cur, b_nxt):
        last = pl.program_id(1) == nb - 1
        keep = jnp.where(last, 0.0, 1.0)
        return (a_cur[0, 0, blk:] + keep * a_nxt[0, 0, :blk])[None], (b_cur[0, 0, blk:] + keep * b_nxt[0, 0, :blk])[None]

    nxt_spec = pl.BlockSpec((1, 1, 2 * blk, HEAD_DIM), lambda h, n: (h, jnp.minimum(n + 1, nb - 1), 0, 0))
    kv_shape = jax.ShapeDtypeStruct((N_KV_HEADS, l, HEAD_DIM), BF16)
    dk, dv = _tile_call("attn_dkv", combine, (N_KV_HEADS, nb), [dkp, dkp, dvp, dvp],
                        [part_spec, nxt_spec, part_spec, nxt_spec], [kv_shape, kv_shape], [cur, cur])
    return dq, dk, dv, dsink


def _block_diag(m):
    j, gl, a, b = m.shape
    eye = jnp.eye(gl, dtype=m.dtype)
    return (m[:, :, :, None, :] * eye[None, :, None, :, None]).reshape(j, gl * a, gl * b)


def _diag_blocks(z, a):
    j = z.shape[0]
    gl = z.shape[1] // a
    b = z.shape[2] // gl
    d = jnp.diagonal(z.reshape(j, gl, a, gl, b), axis1=1, axis2=3)
    return d.transpose(0, 3, 1, 2)


def _s5_permute(src_ref, dst_ref, t_len):
    seg = t_len // 8
    for k in range(seg):
        dst_ref[8 * k:8 * k + 8, :] = src_ref[pl.ds(k, 8, stride=seg), :]


def _s5_unpermute(perm_ref, t_len, emit):
    per_seg = t_len // 64
    for m in range(t_len // 8):
        emit(8 * m, perm_ref[pl.ds(64 * (m % per_seg) + m // per_seg, 8, stride=8), :])


def _s5_powers(p_ref, lr, li, seg):
    hs = TILE_STATES

    def step(k, carry):
        pr, pi = carry
        p_ref[pl.ds(k, 1), 0:hs] = pr
        p_ref[pl.ds(k, 1), hs:2 * hs] = pi
        return lr * pr - li * pi, lr * pi + li * pr

    lax.fori_loop(0, seg, step, (lr, li))


def _s5_local_scan(x_ref, base, lr, li, seg, reverse):
    hs = TILE_STATES
    lr8, li8 = jnp.broadcast_to(lr, (8, hs)), jnp.broadcast_to(li, (8, hs))
    if reverse:
        li8 = -li8

    def step(i, carry):
        hr, hi = carry
        k = seg - 1 - i if reverse else i
        rows = pl.ds(pl.multiple_of(base + 8 * k, 8), 8)
        nr = lr8 * hr - li8 * hi + x_ref[rows, 0:hs]
        ni = lr8 * hi + li8 * hr + x_ref[rows, hs:2 * hs]
        x_ref[rows, 0:hs] = nr
        x_ref[rows, hs:2 * hs] = ni
        return nr, ni

    zero = jnp.zeros((8, hs), F32)
    return lax.fori_loop(0, seg, step, (zero, zero), unroll=2)


def _s5_carries(c_ref, e_ref, ends, start, pw_r, pw_i, reverse):
    hs = TILE_STATES
    e_ref[:, 0:hs] = ends[0]
    e_ref[:, hs:2 * hs] = ends[1]
    cr, ci = start
    if reverse:
        pw_i = -pw_i
    for s in (range(7, -1, -1) if reverse else range(8)):
        c_ref[s:s + 1, 0:hs] = cr
        c_ref[s:s + 1, hs:2 * hs] = ci
        er, ei = e_ref[s:s + 1, 0:hs], e_ref[s:s + 1, hs:2 * hs]
        cr, ci = er + pw_r * cr - pw_i * ci, ei + pw_r * ci + pw_i * cr
    return cr, ci


def _s5_states(u_perm_b16, bd_ref, x_ref, base, c_ref, e_ref, p_ref, lr, li, h_in, t_len):
    hs = TILE_STATES
    seg = t_len // 8
    x_ref[pl.ds(base, t_len), :] = jnp.dot(u_perm_b16, bd_ref[0], preferred_element_type=F32)
    ends = _s5_local_scan(x_ref, base, lr, li, seg, False)
    pw_r, pw_i = p_ref[seg - 1:seg, 0:hs], p_ref[seg - 1:seg, hs:2 * hs]
    h_out = _s5_carries(c_ref, e_ref, ends, h_in, pw_r, pw_i, False)
    cr, ci = c_ref[:, 0:hs], c_ref[:, hs:2 * hs]

    def fix(k, carry):
        rows = pl.ds(pl.multiple_of(base + 8 * k, 8), 8)
        pr, pi = p_ref[pl.ds(k, 1), 0:hs], p_ref[pl.ds(k, 1), hs:2 * hs]
        x_ref[rows, 0:hs] += pr * cr - pi * ci
        x_ref[rows, hs:2 * hs] += pr * ci + pi * cr
        return carry

    lax.fori_loop(0, seg, fix, 0, unroll=2)
    return h_out


def _s5_fwd(proj, u_off, bd, cbd, lam, dvec, t_len):
    l = proj.shape[0]
    nj = bd.shape[0]
    nch = l // t_len
    hs = TILE_STATES
    ub = u_off // LANES
    seg = t_len // 8
    assert t_len % 64 == 0

    def body(u_ref, bd_ref, cbd_ref, lam_ref, d_ref, y_ref, hst_ref, x_ref, h_ref, p_ref, c_ref, e_ref, up_ref, yp_ref):
        lr, li = lam_ref[0, 0:1, :], lam_ref[0, 1:2, :]

        @pl.when(pl.program_id(1) == 0)
        def _():
            h_ref[...] = jnp.zeros_like(h_ref)
            _s5_powers(p_ref, lr, li, seg)

        hst_ref[0, 0] = h_ref[...]
        _s5_permute(u_ref, up_ref, t_len)
        h_out = _s5_states(up_ref[...].astype(BF16), bd_ref, x_ref, 0, c_ref, e_ref, p_ref, lr, li,
                           (h_ref[:, 0:hs], h_ref[:, hs:2 * hs]), t_len)
        h_ref[:, 0:hs] = h_out[0]
        h_ref[:, hs:2 * hs] = h_out[1]
        yp_ref[...] = jnp.dot(x_ref[...].astype(BF16), cbd_ref[0], preferred_element_type=F32)
        dv = d_ref[0]

        def out(r0, rows):
            y_ref[r0:r0 + 8, :] = rows + dv * u_ref[r0:r0 + 8, :]

        _s5_unpermute(yp_ref, t_len, out)

    return pl.pallas_call(
        body, grid=(nj, nch),
        in_specs=[pl.BlockSpec((t_len, LANES), lambda j, c: (c, ub + j)),
                  pl.BlockSpec((1, LANES, 2 * hs), lambda j, c: (j, 0, 0)),
                  pl.BlockSpec((1, 2 * hs, LANES), lambda j, c: (j, 0, 0)),
                  pl.BlockSpec((1, 2, hs), lambda j, c: (j, 0, 0)),
                  pl.BlockSpec((1, 1, LANES), lambda j, c: (j, 0, 0))],
        out_specs=[pl.BlockSpec((t_len, LANES), lambda j, c: (c, j)),
                   pl.BlockSpec((1, 1, 1, 2 * hs), lambda j, c: (j, c, 0, 0))],
        out_shape=[jax.ShapeDtypeStruct((l, nj * LANES), F32),
                   jax.ShapeDtypeStruct((nj, nch, 1, 2 * hs), F32)],
        scratch_shapes=[pltpu.VMEM((t_len, 2 * hs), F32), pltpu.VMEM((1, 2 * hs), F32),
                        pltpu.VMEM((seg, 2 * hs), F32), pltpu.VMEM((8, 2 * hs), F32), pltpu.VMEM((8, 2 * hs), F32),
                        pltpu.VMEM((t_len, LANES), F32), pltpu.VMEM((t_len, LANES), F32)],
        name="s5_fwd", compiler_params=_params(2),
    )(proj, bd, cbd, lam, dvec)


def _s5_bwd(proj, u_off, dy, hst, bd, bdt, cbdt, lam, dvec, t_len):
    l = proj.shape[0]
    nj = bd.shape[0]
    nch = l // t_len
    hs = TILE_STATES
    ub = u_off // LANES
    seg = t_len // 8
    tn = (((0,), (0,)), ((), ()))
    assert t_len % 64 == 0

    def body(u_ref, dy_ref, hst_ref, bd_ref, bdt_ref, cbdt_ref, lam_ref, d_ref,
             du_ref, dbd_ref, dcbdt_ref, dlam_ref, dd_ref,
             x_ref, g_ref, gc_ref, p_ref, c_ref, e_ref, up_ref, dyp_ref, dup_ref):
        first = pl.program_id(1) == 0
        lr, li = lam_ref[0, 0:1, :], lam_ref[0, 1:2, :]

        @pl.when(first)
        def _():
            gc_ref[...] = jnp.zeros_like(gc_ref)
            _s5_powers(p_ref, lr, li, seg)

        _s5_permute(u_ref, up_ref, t_len)
        _s5_permute(dy_ref, dyp_ref, t_len)
        ub16, dyb16 = up_ref[...].astype(BF16), dyp_ref[...].astype(BF16)
        h0 = hst_ref[0, 0]
        _s5_states(ub16, bd_ref, x_ref, 8, c_ref, e_ref, p_ref, lr, li, (h0[:, 0:hs], h0[:, hs:2 * hs]), t_len)
        x_ref[0:8, :] = c_ref[...]
        g_ref[...] = jnp.dot(dyb16, cbdt_ref[0], preferred_element_type=F32)
        starts = _s5_local_scan(g_ref, 0, lr, li, seg, True)
        pw_r, pw_i = p_ref[seg - 1:seg, 0:hs], p_ref[seg - 1:seg, hs:2 * hs]
        g_out = _s5_carries(c_ref, e_ref, starts, (gc_ref[:, 0:hs], gc_ref[:, hs:2 * hs]), pw_r, pw_i, True)
        gc_ref[:, 0:hs] = g_out[0]
        gc_ref[:, hs:2 * hs] = g_out[1]
        cr, ci = c_ref[:, 0:hs], c_ref[:, hs:2 * hs]

        def fix(k, carry):
            alr, ali = carry
            rows = pl.ds(pl.multiple_of(8 * k, 8), 8)
            pr, pi = p_ref[pl.ds(seg - 1 - k, 1), 0:hs], p_ref[pl.ds(seg - 1 - k, 1), hs:2 * hs]
            gr = g_ref[rows, 0:hs] + pr * cr + pi * ci
            gi = g_ref[rows, hs:2 * hs] + pr * ci - pi * cr
            g_ref[rows, 0:hs] = gr
            g_ref[rows, hs:2 * hs] = gi
            hpr, hpi = x_ref[rows, 0:hs], x_ref[rows, hs:2 * hs]
            return alr + gr * hpr + gi * hpi, ali + gi * hpr - gr * hpi

        zero = jnp.zeros((8, hs), F32)
        alr, ali = lax.fori_loop(0, seg, fix, (zero, zero), unroll=2)
        alr, ali = _colsum(alr), _colsum(ali)
        g = g_ref[...].astype(BF16)
        h = x_ref[pl.ds(8, t_len), :].astype(BF16)
        dup_ref[...] = jnp.dot(g, bdt_ref[0], preferred_element_type=F32)
        dv = d_ref[0]

        def out(r0, rows):
            du_ref[r0:r0 + 8, :] = (rows + dv * dy_ref[r0:r0 + 8, :]).astype(du_ref.dtype)

        _s5_unpermute(dup_ref, t_len, out)
        sign = jnp.where(lax.broadcasted_iota(jnp.int32, (1, 2 * hs), 1) < hs, 1.0, -1.0)
        dbd = lax.dot_general(ub16, g, tn, preferred_element_type=F32)
        dcbdt = lax.dot_general(dyb16, h, tn, preferred_element_type=F32) * sign
        ddv = _colsum(dy_ref[...] * u_ref[...])

        @pl.when(first)
        def _():
            dbd_ref[0] = dbd
            dcbdt_ref[0] = dcbdt
            dlam_ref[0, 0:1, :] = alr
            dlam_ref[0, 1:2, :] = ali
            dd_ref[0] = ddv

        @pl.when(jnp.logical_not(first))
        def _():
            dbd_ref[0] += dbd
            dcbdt_ref[0] += dcbdt
            dlam_ref[0, 0:1, :] += alr
            dlam_ref[0, 1:2, :] += ali
            dd_ref[0] += ddv

    rev = lambda c: nch - 1 - c
    wide = pl.BlockSpec((1, LANES, 2 * hs), lambda j, c: (j, 0, 0))
    tall = pl.BlockSpec((1, 2 * hs, LANES), lambda j, c: (j, 0, 0))
    return pl.pallas_call(
        body, grid=(nj, nch),
        in_specs=[pl.BlockSpec((t_len, LANES), lambda j, c: (rev(c), ub + j)),
                  pl.BlockSpec((t_len, LANES), lambda j, c: (rev(c), j)),
                  pl.BlockSpec((1, 1, 1, 2 * hs), lambda j, c: (j, rev(c), 0, 0)),
                  wide, tall, wide,
                  pl.BlockSpec((1, 2, hs), lambda j, c: (j, 0, 0)),
                  pl.BlockSpec((1, 1, LANES), lambda j, c: (j, 0, 0))],
        out_specs=[pl.BlockSpec((t_len, LANES), lambda j, c: (rev(c), j)),
                   wide, wide,
                   pl.BlockSpec((1, 2, hs), lambda j, c: (j, 0, 0)),
                   pl.BlockSpec((1, 1, LANES), lambda j, c: (j, 0, 0))],
        out_shape=[jax.ShapeDtypeStruct((l, nj * LANES), BF16),
                   jax.ShapeDtypeStruct((nj, LANES, 2 * hs), F32),
                   jax.ShapeDtypeStruct((nj, LANES, 2 * hs), F32),
                   jax.ShapeDtypeStruct((nj, 2, hs), F32),
                   jax.ShapeDtypeStruct((nj, 1, LANES), F32)],
        scratch_shapes=[pltpu.VMEM((t_len + 8, 2 * hs), F32), pltpu.VMEM((t_len, 2 * hs), F32),
                        pltpu.VMEM((1, 2 * hs), F32), pltpu.VMEM((seg, 2 * hs), F32),
                        pltpu.VMEM((8, 2 * hs), F32), pltpu.VMEM((8, 2 * hs), F32),
                        pltpu.VMEM((t_len, LANES), F32), pltpu.VMEM((t_len, LANES), F32),
                        pltpu.VMEM((t_len, LANES), F32)],
        name="s5_bwd", compiler_params=_params(2),
    )(proj, dy, hst, bd, bdt, cbdt, lam, dvec)


def _full_spec(shape):
    nd = len(shape)
    return pl.BlockSpec(tuple(shape), lambda i: (0,) * nd)


def _sds(shape, dtype=F32):
    return jax.ShapeDtypeStruct(tuple(shape), dtype)


def kernel(x, c, ada_w, ada_b, norm_mix_g, w_in, attn_sinks, w_attn_proj, ssm_a_re, ssm_a_im, ssm_log_dt, ssm_b_re, ssm_b_im, ssm_c_re, ssm_c_im, ssm_d, w_ssm_glu, w_out, norm_ffn_g, w_ffn_up, ffn_conv_w, ffn_conv_b, w_ffn_down, final_g, loss_target, m_ada_w, m_ada_b, m_norm_mix_g, m_w_in, m_attn_sinks, m_w_attn_proj, m_ssm_a_re, m_ssm_a_im, m_ssm_log_dt, m_ssm_b_re, m_ssm_b_im, m_ssm_c_re, m_ssm_c_im, m_ssm_d, m_w_ssm_glu, m_w_out, m_norm_ffn_g, m_w_ffn_up, m_ffn_conv_w, m_ffn_conv_b, m_w_ffn_down, m_final_g, v_ada_w, v_ada_b, v_norm_mix_g, v_w_in, v_attn_sinks, v_w_attn_proj, v_ssm_a_re, v_ssm_a_im, v_ssm_log_dt, v_ssm_b_re, v_ssm_b_im, v_ssm_c_re, v_ssm_c_im, v_ssm_d, v_w_ssm_glu, v_w_out, v_norm_ffn_g, v_w_ffn_up, v_ffn_conv_w, v_ffn_conv_b, v_w_ffn_down, v_final_g):
    given = dict(locals())
    names = ['ada_w', 'ada_b', 'norm_mix_g', 'w_in', 'attn_sinks', 'w_attn_proj', 'ssm_a_re', 'ssm_a_im',
             'ssm_log_dt', 'ssm_b_re', 'ssm_b_im', 'ssm_c_re', 'ssm_c_im', 'ssm_d', 'w_ssm_glu', 'w_out',
             'norm_ffn_g', 'w_ffn_up', 'ffn_conv_w', 'ffn_conv_b', 'w_ffn_down', 'final_g']

    xs = x[0]
    tgt = loss_target[0]
    l, d = xs.shape
    attn_w = w_attn_proj.shape[1]
    ssm_w = w_ssm_glu.shape[1]
    hq = attn_sinks.shape[1]
    qpk = hq // N_KV_HEADS
    kv_w = N_KV_HEADS * HEAD_DIM
    n_groups = ssm_a_re.shape[1]
    dff = ffn_conv_b.shape[1]
    in_w = attn_w + 2 * kv_w + ssm_w + 2 * d
    nj = ssm_w // LANES
    off_k, off_v, off_u = attn_w, attn_w + kv_w, attn_w + 2 * kv_w
    off_ga, off_gs = off_u + ssm_w, off_u + ssm_w + d
    assert hq * HEAD_DIM == attn_w and n_groups * SSM_P == ssm_w and l % ATT_BLOCK == 0

    xi, yi, ci = _dev()
    idx = 4 * xi + 2 * yi + ci

    row_sharded = {'w_out': (d, d), 'w_ffn_down': (dff, d)}
    big = ['w_in', 'w_attn_proj', 'w_ssm_glu', 'w_out', 'w_ffn_up', 'w_ffn_down']
    spack, s_offs = _pack([c, ffn_conv_w[0]], LANES, 8)
    w16 = {k: given[k][0].astype(BF16) for k in big}
    wg_in, sg = _all_gather("gather_first", [w16['w_in'], spack])
    mixer_w = ['w_attn_proj', 'w_ssm_glu', 'w_out']
    h_mixer, tok = _exchange_start("gather_mixer_start", [w16[k] for k in mixer_w], True, wg_in)
    h_up, tok = _exchange_start("gather_ffn_up_start", [w16['w_ffn_up']], True, tok)
    h_down, tok = _exchange_start("gather_ffn_down_start", [w16['w_ffn_down']], True, tok)
    full = {'w_in': wg_in.transpose(1, 0, 2).reshape(d, in_w)}
    c_all = _unpack(sg, s_offs[0], (d,), lead=(N_DEV,))
    conv_w = _unpack(sg, s_offs[1], ffn_conv_w.shape[1:], lead=(N_DEV,)).transpose(1, 0, 2).reshape(3, dff)
    conv_b = ffn_conv_b

    mod_n = ada_w.shape[2]
    tcm = _pick(mod_n, 512)
    ada_b_mine = lax.dynamic_slice_in_dim(ada_b, idx * mod_n, mod_n, axis=1)

    def modpart_fn(cv, wv, bv):
        cond = cv * jax.nn.sigmoid(cv)
        return jnp.dot(cond.astype(BF16), wv.astype(BF16), preferred_element_type=F32) + bv, cond

    modp, cond_all = _tile_call(
        "ada_rows", modpart_fn, (mod_n // tcm,), [c_all, ada_w[0], ada_b_mine],
        [pl.BlockSpec((N_DEV, d), lambda j: (0, 0)), pl.BlockSpec((d, tcm), lambda j: (0, j)),
         pl.BlockSpec((1, tcm), lambda j: (0, j))],
        [_sds((N_DEV, mod_n)), _sds((N_DEV, d))],
        [pl.BlockSpec((N_DEV, tcm), lambda j: (0, j)), pl.BlockSpec((N_DEV, d), lambda j: (0, 0))])
    (modg,) = _all_gather("gather_ada_rows", [modp])
    mod = lax.dynamic_index_in_dim(modg, idx, axis=1, keepdims=False).reshape(1, N_DEV * mod_n)
    sh1, sc1, g1, sh2, sc2, g2 = [mod[:, i * d:(i + 1) * d] for i in range(6)]

    tr = _pick(l, 256, 8)
    trh = _pick(l, 128, 8)
    nr, nrh = l // tr, l // trh
    g_mix, g_ffn, g_fin = norm_mix_g + tok[0:1, 0:1], norm_ffn_g, final_g.reshape(1, d)

    def with_t(fn):
        def wrapped(*vals):
            out = fn(*vals)
            out = out if isinstance(out, tuple) else (out,)
            return out + (out[-1].T,)
        return wrapped

    h1, h1_t = _tile_call("norm_mod_mix", with_t(_norm_mod), (1, nr), [xs, g_mix, sc1, sh1],
                          [_t(tr, d), _v(d), _v(d), _v(d)], [_sds((l, d), BF16), _sds((d, l), BF16)],
                          [_t(tr, d), _tt(tr, d)])
    proj = _matmul("proj_in", h1, full['w_in'], "nn", tn=1280)

    def heads(z, n):
        return z.reshape(l, n, HEAD_DIM).transpose(1, 0, 2)

    qh = heads(proj[:, :attn_w], hq)
    kh = heads(proj[:, off_k:off_k + kv_w], N_KV_HEADS)
    vh = heads(proj[:, off_v:off_v + kv_w], N_KV_HEADS)
    sinks3 = jnp.repeat(attn_sinks.reshape(N_KV_HEADS, qpk), ATT_BLOCK, axis=1)[..., None]
    o_h = _attn_fwd(qh, kh, vh, sinks3)
    o2 = o_h.transpose(1, 0, 2).reshape(l, attn_w)

    gn = (n_groups, SSM_N)
    pgn = (SSM_P, n_groups, SSM_N)
    a_re, a_im, log_dt = ssm_a_re[0], ssm_a_im[0], ssm_log_dt[0].reshape(n_groups, 1)
    b_re, b_im = ssm_b_re[0].transpose(2, 0, 1), ssm_b_im[0].transpose(2, 0, 1)
    disc_ins = [a_re, a_im, log_dt, b_re, b_im]
    disc_specs = [_full_spec(gn), _full_spec(gn), _full_spec((n_groups, 1)), _full_spec(pgn), _full_spec(pgn)]
    lam_r, lam_i, bb_r, bb_i = _tile_call(
        "s5_discretise", _s5_disc_fn, (1,), disc_ins, disc_specs,
        [_sds(gn), _sds(gn), _sds(pgn), _sds(pgn)],
        [_full_spec(gn), _full_spec(gn), _full_spec(pgn), _full_spec(pgn)])

    def tiles_gpn(z):
        return z.reshape(SSM_P, nj, TILE_GROUPS, SSM_N).transpose(1, 2, 0, 3)

    bd = jnp.concatenate([_block_diag(tiles_gpn(bb_r)), _block_diag(tiles_gpn(bb_i))], axis=2).astype(BF16)
    c_r = ssm_c_re[0].reshape(nj, TILE_GROUPS, SSM_P, SSM_N).transpose(0, 1, 3, 2)
    c_i = (-ssm_c_im[0]).reshape(nj, TILE_GROUPS, SSM_P, SSM_N).transpose(0, 1, 3, 2)
    cbd = jnp.concatenate([_block_diag(c_r), _block_diag(c_i)], axis=1).astype(BF16)
    bdt, cbdt = bd.transpose(0, 2, 1), cbd.transpose(0, 2, 1)
    lam = jnp.stack([lam_r.reshape(nj, TILE_STATES), lam_i.reshape(nj, TILE_STATES)], axis=1)
    dvec = ssm_d[0].reshape(nj, 1, LANES)
    t_len = _pick(l, 512, 8)
    y, hst = _s5_fwd(proj, off_u, bd, cbd, lam, dvec, t_len)

    tcs, trg = _pick(ssm_w, 1024), _pick(l, 512, 8)
    gy = _tile_call("gelu", lambda v: jax.nn.gelu(v), (ssm_w // tcs, l // trg), [y], [_t(trg, tcs)],
                    [_sds((l, ssm_w), BF16)], [_t(trg, tcs)])[0]
    full.update(zip(mixer_w, _exchange_wait("gather_mixer_wait", h_mixer, gy)))
    full['w_out'] = full['w_out'].reshape(row_sharded['w_out'])
    attn = _matmul("attn_proj", o2, full['w_attn_proj'], "nn")
    glu = _matmul("ssm_glu", gy, full['w_ssm_glu'], "nn")

    tcd = 256 if d % 256 == 0 and off_ga % 256 == 0 else LANES
    assert d % tcd == 0 and off_ga % tcd == 0 and off_gs % tcd == 0
    trm = _pick(l, 1024, 8)
    mix_in_specs = [_t(trm, tcd), _t(trm, tcd, d), _t(trm, tcd), _t(trm, tcd, off_ga), _t(trm, tcd, off_gs)]
    mixed = _tile_call("gate_mix", _mix_fn, (d // tcd, l // trm), [glu, glu, attn, proj, proj], mix_in_specs,
                       [_sds((l, d), BF16)], [_t(trm, tcd)])[0]
    mixout = _matmul("mix_out", mixed, full['w_out'], "nn")

    def res_norm_fn(xv, mo, g1v, gv, scv, shv):
        x2v = xv + g1v * mo
        return x2v, _norm_mod(x2v, gv, scv, shv)

    x2, h2, h2_t = _tile_call("residual_norm_mod_ffn", with_t(res_norm_fn), (1, nr), [xs, mixout, g1, g_ffn, sc2, sh2],
                              [_t(tr, d), _t(tr, d), _v(d), _v(d), _v(d), _v(d)],
                              [_sds((l, d)), _sds((l, d), BF16), _sds((d, l), BF16)],
                              [_t(tr, d), _t(tr, d), _tt(tr, d)])
    full['w_ffn_up'], = _exchange_wait("gather_ffn_up_wait", h_up, h2)
    up = _matmul("ffn_up", h2, full['w_ffn_up'], "nn", out_dtype=BF16, tn=1408)

    tcf, trc = _pick(dff, 1408), _pick(l, 512, 8)
    assert dff % tcf == 0
    ncf = dff // tcf

    taps = [conv_w[i:i + 1] for i in range(3)]

    def conv_gate(gp, gp_prev, w0, w1, w2, bv):
        gp = gp.astype(F32)
        prev = jnp.where(pl.program_id(1) == 0, 0.0, 1.0) * gp_prev.astype(F32)
        ext = jnp.concatenate([prev, gp], axis=0)
        m1 = pltpu.roll(ext, 1, 0)[HALO:]
        m2 = pltpu.roll(ext, 2, 0)[HALO:]
        return w0 * m2 + w1 * m1 + w2 * gp + bv, m1, m2

    def convglu_fn(gp, gp_prev, val, w0, w1, w2, bv):
        gate, _, _ = conv_gate(gp, gp_prev, w0, w1, w2, bv)
        return gate * jax.nn.sigmoid(gate) * val.astype(F32)

    act, act_t = _tile_call("conv_swiglu", with_t(convglu_fn), (ncf, l // trc), [up, up, up] + taps + [conv_b],
                            [_t(trc, tcf), _prev_rows(trc, tcf), _t(trc, tcf, dff)] + [_v(tcf)] * 4,
                            [_sds((l, dff), BF16), _sds((dff, l), BF16)], [_t(trc, tcf), _tt(trc, tcf)])
    full['w_ffn_down'] = _exchange_wait("gather_ffn_down_wait", h_down, act)[0].reshape(row_sharded['w_ffn_down'])
    ffn = _matmul("ffn_down", act, full['w_ffn_down'], "nn", tm=512)

    def final_fn(x2v, fv, g2v, gv, tv):
        rows = x2v.shape[0]

        def loss_of(x2a, fa, g2a, ga):
            out = _rms(x2a + g2a * fa, ga)
            err = out - tv
            return 0.5 * _colsum(jnp.mean(err * err, axis=-1, keepdims=True))

        loss, vjp = jax.vjp(loss_of, x2v, fv, _bc(g2v, rows), _bc(gv, rows))
        dx3, dffn, dg2, dgf = vjp(jnp.ones((1, 1), F32))
        return jnp.broadcast_to(loss, (1, LANES)), dx3, dffn, _colsum(dg2), _colsum(dgf)

    loss_p, dx3, dffn, dg2, dg_fin = _tile_call(
        "loss_final_norm", final_fn, (1, nrh), [x2, ffn, g2, g_fin, tgt],
        [_t(trh, d), _t(trh, d), _v(d), _v(d), _t(trh, d)],
        [_sds((1, LANES)), _sds((l, d)), _sds((l, d), BF16), _sds((1, d)), _sds((1, d))],
        [_v(LANES), _t(trh, d), _t(trh, d), _v(d), _v(d)], acc=(0, 3, 4))
    loss = lax.psum(loss_p[0, 0], ("x", "y", "c"))

    dact = _matmul("d_act", dffn, full['w_ffn_down'], "nt", out_dtype=BF16, tn=1408, dep=loss.reshape(1, 1))
    gd, gd16, pending = {}, {}, []
    dw_down, dw_down16 = _matmul("dw_ffn_down", act_t, dffn, "nn", tm=512, also_bf16=True)
    gd['w_ffn_down'], gd16['w_ffn_down'] = [z.reshape((N_DEV,) + w_ffn_down.shape[1:]) for z in (dw_down, dw_down16)]
    handle, tok = _exchange_start("grad_ffn_down_start", [gd16['w_ffn_down']], False, dw_down)
    pending.append((['w_ffn_down'], handle))
    conv_b_bwd = conv_b + tok[0:1, 0:1]

    def convglu_bwd_fn(gp, gp_prev, gp_next, val, val_next, da, da_next, w0, w1, w2, bv):
        rows = gp.shape[0]
        i = pl.program_id(1)
        gp, val, da = gp.astype(F32), val.astype(F32), da.astype(F32)
        prev = jnp.where(i == 0, 0.0, 1.0) * gp_prev.astype(F32)
        more = jnp.where(i == pl.num_programs(1) - 1, 0.0, 1.0)
        ext = jnp.concatenate([prev, gp, gp_next.astype(F32)], axis=0)
        cur = ext[HALO:]
        m1 = pltpu.roll(ext, 1, 0)[HALO:]
        m2 = pltpu.roll(ext, 2, 0)[HALO:]
        gate = w0 * m2 + w1 * m1 + w2 * cur + bv
        sg = jax.nn.sigmoid(gate)
        val_e = jnp.concatenate([val, val_next.astype(F32)], axis=0)
        da_e = jnp.concatenate([da, more * da_next.astype(F32)], axis=0)
        dgate = da_e * val_e * (sg * (1.0 + gate * (1.0 - sg)))
        p1 = pltpu.roll(dgate, rows + HALO - 1, 0)[:rows]
        p2 = pltpu.roll(dgate, rows + HALO - 2, 0)[:rows]
        dg = dgate[:rows]
        dgp = w2 * dg + w1 * p1 + w0 * p2
        dval = da * (gate[:rows] * sg[:rows])
        return (jnp.stack([dgp, dval], axis=0), _colsum(dg), _colsum(dg * m2[:rows]), _colsum(dg * m1[:rows]),
                _colsum(dg * gp))

    dup, dconv_b, dcw0, dcw1, dcw2 = _tile_call(
        "conv_swiglu_bwd", convglu_bwd_fn, (ncf, nr), [up, up, up, up, up, dact, dact] + taps + [conv_b_bwd],
        [_t(tr, tcf), _prev_rows(tr, tcf), _next_rows(tr, tcf, l), _t(tr, tcf, dff), _next_rows(tr, tcf, l, dff),
         _t(tr, tcf), _next_rows(tr, tcf, l)] + [_v(tcf)] * 4,
        [_sds((2, l, dff), BF16)] + [_sds((1, dff))] * 4, [_st(tr, tcf)] + [_v(tcf)] * 4, acc=(1, 2, 3, 4))
    dh2 = _matmul("d_h2", dup, full['w_ffn_up'], "nt")
    gd['w_ffn_up'], gd16['w_ffn_up'] = _matmul("dw_ffn_up", h2_t, dup, "nn", tm=512, tn=1408, out_stack=N_DEV, also_bf16=True)
    handle, tok = _exchange_start("grad_ffn_up_start", [gd16['w_ffn_up']], False, gd['w_ffn_up'])
    pending.append((['w_ffn_up'], handle))
    g_ffn_bwd = g_ffn + tok[0:1, 0:1]

    def res_norm_bwd_fn(xv, mo, g1v, gv, scv, shv, dhv, dxv):
        rows = xv.shape[0]
        _, vjp = jax.vjp(res_norm_fn, xv, mo, _bc(g1v, rows), _bc(gv, rows), _bc(scv, rows), _bc(shv, rows))
        dx, dmo, dg1v, dgv, dscv, dshv = vjp((dxv, dhv))
        return dx, dmo, _colsum(dg1v), _colsum(dgv), _colsum(dscv), _colsum(dshv)

    dx2, dmixout, dg1, dg_ffn, dsc2, dsh2 = _tile_call(
        "residual_norm_mod_ffn_bwd", res_norm_bwd_fn, (1, nrh), [xs, mixout, g1, g_ffn_bwd, sc2, sh2, dh2, dx3],
        [_t(trh, d), _t(trh, d), _v(d), _v(d), _v(d), _v(d), _t(trh, d), _t(trh, d)],
        [_sds((l, d)), _sds((l, d), BF16)] + [_sds((1, d))] * 4,
        [_t(trh, d), _t(trh, d)] + [_v(d)] * 4, acc=(2, 3, 4, 5))

    dmixed = _matmul("d_mixed", dmixout, full['w_out'], "nt")
    dw_out, dw_out16 = _matmul("dw_out", mixed, dmixout, "tn", also_bf16=True)
    gd['w_out'], gd16['w_out'] = [z.reshape((N_DEV,) + w_out.shape[1:]) for z in (dw_out, dw_out16)]

    def mix_bwd_fn(ga_, gb_, at, pa, ps, dm):
        _, vjp = jax.vjp(_mix_fn, ga_, gb_, at, pa, ps)
        da, db, dat, dpa, dps = vjp(dm)
        return jnp.stack([da, db], axis=0), dat, dpa, dps

    dglu, dattn, dga, dgs = _tile_call(
        "gate_mix_bwd", mix_bwd_fn, (d // tcd, l // trm), [glu, glu, attn, proj, proj, dmixed],
        mix_in_specs + [_t(trm, tcd)],
        [_sds((2, l, d), BF16)] + [_sds((l, d), BF16)] * 3, [_st(trm, tcd)] + [_t(trm, tcd)] * 3)

    dgy = _matmul("d_gelu_y", dglu, full['w_ssm_glu'], "nt")
    gd['w_ssm_glu'], gd16['w_ssm_glu'] = _matmul("dw_ssm_glu", gy, dglu, "tn", out_stack=N_DEV, also_bf16=True)

    def gelu_bwd_fn(yv, dv):
        _, vjp = jax.vjp(lambda z: jax.nn.gelu(z), yv)
        return vjp(dv)[0]

    dy = _tile_call("gelu_bwd", gelu_bwd_fn, (ssm_w // tcs, l // trg), [y, dgy], [_t(trg, tcs), _t(trg, tcs)],
                    [_sds((l, ssm_w))], [_t(trg, tcs)])[0]
    du, dbd, dcbdt, dlam, dd_tiles = _s5_bwd(proj, off_u, dy, hst, bd, bdt, cbdt, lam, dvec, t_len)

    def gpn_of(z):
        return z.transpose(2, 0, 1, 3).reshape(pgn)

    dbb_r = gpn_of(_diag_blocks(dbd[:, :, :TILE_STATES], SSM_P))
    dbb_i = gpn_of(_diag_blocks(dbd[:, :, TILE_STATES:], SSM_P))
    dc_re = _diag_blocks(dcbdt[:, :, :TILE_STATES], SSM_P).reshape(n_groups, SSM_P, SSM_N)
    dc_im = _diag_blocks(dcbdt[:, :, TILE_STATES:], SSM_P).reshape(n_groups, SSM_P, SSM_N)
    dlam_r, dlam_i = dlam[:, 0].reshape(gn), dlam[:, 1].reshape(gn)

    def disc_bwd_fn(ar, ai, ld, br, bi, dlr, dli, dbr, dbi):
        _, vjp = jax.vjp(_s5_disc_fn, ar, ai, ld, br, bi)
        return vjp((dlr, dli, dbr, dbi))

    da_re, da_im, dlog_dt, db_re, db_im = _tile_call(
        "s5_discretise_bwd", disc_bwd_fn, (1,), disc_ins + [dlam_r, dlam_i, dbb_r, dbb_i],
        disc_specs + [_full_spec(gn), _full_spec(gn), _full_spec(pgn), _full_spec(pgn)],
        [_sds(gn), _sds(gn), _sds((n_groups, 1)), _sds(pgn), _sds(pgn)], disc_specs)

    do2 = _matmul("d_attn_heads", dattn, full['w_attn_proj'], "nt")
    gd['w_attn_proj'], gd16['w_attn_proj'] = _matmul("dw_attn_proj", o2, dattn, "tn", out_stack=N_DEV, also_bf16=True)
    handle, tok = _exchange_start("grad_mixer_start", [gd16[k] for k in mixer_w], False, gd['w_attn_proj'])
    pending.append((mixer_w, handle))
    do_h = heads(do2.astype(BF16), hq)
    dq_h, dk_h, dv_h, dsink = _attn_bwd(qh, kh, vh, sinks3 + tok[0:1, 0:1], do_h)

    def unheads(z):
        return z.transpose(1, 0, 2).reshape(l, z.shape[0] * HEAD_DIM)

    dproj = jnp.concatenate([unheads(dq_h), unheads(dk_h), unheads(dv_h), du, dga, dgs], axis=1)
    dw_in, dw_in16 = _matmul("dw_in", h1_t, dproj, "nn", tm=512, tn=1280, also_bf16=True)
    dcw = jnp.concatenate([dcw0, dcw1, dcw2], axis=0)
    shard_in, shard_cw = w_in.shape[1:], ffn_conv_w.shape[1:]
    gd16['w_in'] = dw_in16.reshape(shard_in[0], N_DEV, shard_in[1]).transpose(1, 0, 2)
    own_in = lax.dynamic_slice_in_dim(dw_in, idx * shard_in[1], shard_in[1], axis=1)[None]
    gd['ffn_conv_w'] = dcw.reshape(shard_cw[0], N_DEV, shard_cw[1]).transpose(1, 0, 2)
    gd16['ffn_conv_w'] = gd['ffn_conv_w'].astype(BF16)
    handle, tok = _exchange_start("grad_in_start", [gd16['w_in'], gd16['ffn_conv_w']], False, dw_in)
    pending.append((['w_in', 'ffn_conv_w'], handle))
    dh1 = _matmul("d_h1", dproj, full['w_in'], "nt", tm=512, dep=tok)

    def norm_bwd_fn(xv, gv, scv, shv, dhv, dxv):
        rows = xv.shape[0]
        _, vjp = jax.vjp(_norm_mod, xv, _bc(gv, rows), _bc(scv, rows), _bc(shv, rows))
        dx, dgv, dscv, dshv = vjp(dhv)
        return dx + dxv, _colsum(dgv), _colsum(dscv), _colsum(dshv)

    grad_x, dg_mix, dsc1, dsh1 = _tile_call(
        "norm_mod_mix_bwd", norm_bwd_fn, (1, nrh), [xs, g_mix, sc1, sh1, dh1, dx2],
        [_t(trh, d), _v(d), _v(d), _v(d), _t(trh, d), _t(trh, d)],
        [_sds((l, d))] + [_sds((1, d))] * 3, [_t(trh, d)] + [_v(d)] * 3, acc=(1, 2, 3))

    dmod = jnp.concatenate([dsh1, dsc1, dg1, dsh2, dsc2, dg2], axis=1)
    small = ['ada_b', 'norm_mix_g', 'attn_sinks', 'ssm_a_re', 'ssm_a_im', 'ssm_log_dt', 'ssm_b_re', 'ssm_b_im',
             'ssm_c_re', 'ssm_c_im', 'ssm_d', 'norm_ffn_g', 'ffn_conv_b', 'final_g']
    small_grads = {
        'ada_b': dmod, 'norm_mix_g': dg_mix, 'attn_sinks': dsink[:, 0, 0], 'ssm_a_re': da_re, 'ssm_a_im': da_im,
        'ssm_log_dt': dlog_dt, 'ssm_b_re': db_re.transpose(1, 2, 0), 'ssm_b_im': db_im.transpose(1, 2, 0),
        'ssm_c_re': dc_re, 'ssm_c_im': dc_im, 'ssm_d': dd_tiles, 'norm_ffn_g': dg_ffn, 'ffn_conv_b': dconv_b,
        'final_g': dg_fin}
    gs_pack, sm_offs = _pack([small_grads[k] for k in small], LANES, 8)

    sharded = big + ['ffn_conv_w']
    sharded_out = {}

    def finish(group, handle, after):
        for k, parts in zip(group, _exchange_wait("grad_" + group[0] + "_wait", handle, after)):
            own_src, own_at = (own_in, 0 * idx) if k == 'w_in' else (gd[k], idx)
            sharded_out[k] = _adamw_sharded("adamw_" + k, parts, own_src, given[k][0], given['m_' + k][0],
                                            given['v_' + k][0], jnp.stack([idx, own_at]).astype(jnp.int32))

    for group, handle in pending[:-1]:
        finish(group, handle, grad_x)
    (gs_all,) = _all_gather("gather_small_grads", [gs_pack], dep=sharded_out[pending[-2][0][-1]][1])
    ws_pack, _ = _pack([given[k] for k in small], LANES, 8)
    ms_pack, _ = _pack([given['m_' + k] for k in small], LANES, 8)
    vs_pack, _ = _pack([given['v_' + k] for k in small], LANES, 8)
    small_out = _adamw("adamw_replicated", gs_all, ws_pack, ms_pack, vs_pack)

    dmod_all = _unpack(gs_all, sm_offs[0], (N_DEV * mod_n,), lead=(N_DEV,))
    dmod_mine = lax.dynamic_slice_in_dim(dmod_all, idx * mod_n, mod_n, axis=1)
    kpad = LANES - N_DEV
    cond_t = jnp.pad(cond_all.T, ((0, 0), (0, kpad)))
    dmod_pad = jnp.pad(dmod_mine, ((0, kpad), (0, 0)))
    g_ada_w = _matmul("dw_ada", cond_t, dmod_pad, "nn")
    ada_out = _adamw("adamw_ada_w", g_ada_w[None], ada_w[0], m_ada_w[0], v_ada_w[0])

    finish(*pending[-1], ada_out[0])

    results = [{}, {}, {}, {}]
    for which in range(4):
        for k, off in zip(small, sm_offs):
            results[which][k] = _unpack(small_out[which], off, given[k].shape)
        for k in sharded:
            results[which][k] = sharded_out[k][which][None]
        results[which]['ada_w'] = ada_out[which][None]
    outs = [loss, grad_x[None]]
    for which in range(4):
        outs += [results[which][k] for k in names]
    return tuple(outs)
```

```python
import functools
import math

import jax
import jax.numpy as jnp
from jax import lax
from jax.experimental import pallas as pl
from jax.experimental.pallas import tpu as pltpu

F32, BF16 = jnp.float32, jnp.bfloat16
MESH = pl.DeviceIdType.MESH
N_DEV = 8

HEAD_DIM = 64
N_KV_HEADS = 2
ATT_BLOCK = 128
NEG_INF = -1e30
SSM_P = 16
SSM_N = 64
LANES = 128
TILE_GROUPS = LANES // SSM_P
TILE_STATES = TILE_GROUPS * SSM_N
RMS_EPS = 1e-6
ADAM_LR, ADAM_B1, ADAM_B2, ADAM_EPS, ADAM_WD, ADAM_STEP = 0.001, 0.9, 0.999, 1e-08, 0.01, 10
VMEM_LIMIT = 56 * 1024 * 1024
MATMUL_VMEM_BUDGET = 44 * 1024 * 1024


def _params(n_axes):
    return pltpu.CompilerParams(dimension_semantics=("arbitrary",) * n_axes, vmem_limit_bytes=VMEM_LIMIT)


def _pick(dim, pref, align=128):
    if dim <= align:
        return dim
    t = (min(pref, dim) // align) * align
    while t > align and dim % t:
        t -= align
    assert dim % t == 0, (dim, pref, align)
    return t


def _dev():
    return lax.axis_index("x"), lax.axis_index("y"), lax.axis_index("c")


def _tile_call(name, fn, grid, ins, in_specs, out_shapes, out_specs, acc=()):
    n_in, n_out = len(ins), len(out_shapes)
    acc_axis = len(grid) - 1

    def body(*refs):
        vals = fn(*[r[...] for r in refs[:n_in]])
        if not isinstance(vals, (tuple, list)):
            vals = (vals,)
        assert len(vals) == n_out
        for i, (r, v) in enumerate(zip(refs[n_in:], vals)):
            v = v.astype(r.dtype)
            if i in acc:
                first = pl.program_id(acc_axis) == 0

                @pl.when(first)
                def _():
                    r[...] = v

                @pl.when(jnp.logical_not(first))
                def _():
                    r[...] += v
            else:
                r[...] = v

    return pl.pallas_call(
        body, grid=grid, in_specs=in_specs, out_specs=out_specs, out_shape=out_shapes, name=name,
        compiler_params=_params(len(grid)),
    )(*ins)


def _t(tr, tc, off=0):
    return pl.BlockSpec((tr, tc), lambda j, i: (i, j + off // tc))


def _tt(tr, tc):
    return pl.BlockSpec((tc, tr), lambda j, i: (j, i))


def _v(tc, off=0, rows=1):
    return pl.BlockSpec((rows, tc), lambda j, i: (0, j + off // tc))


HALO = 16


def _prev_rows(tr, tc, off=0):
    return pl.BlockSpec((HALO, tc), lambda j, i: (jnp.maximum(i * (tr // HALO) - 1, 0), j + off // tc))


def _next_rows(tr, tc, nrows, off=0):
    return pl.BlockSpec((HALO, tc),
                        lambda j, i: (jnp.minimum((i + 1) * (tr // HALO), nrows // HALO - 1), j + off // tc))


def _st(tr, tc):
    return pl.BlockSpec((2, tr, tc), lambda j, i: (0, i, j))


def _bc(v, rows):
    return jnp.broadcast_to(v, (rows, v.shape[-1]))


def _colsum(v):
    return jnp.sum(v, axis=0, keepdims=True)


def _matmul(name, a, b, mode, out_dtype=F32, tm=1024, tn=1024, tk=None, out_stack=None, also_bf16=False, dep=None):
    def dims(z):
        return (z.shape[-2], z.shape[-1] * (z.shape[0] if z.ndim == 3 else 1))

    ar, ac = dims(a)
    br, bc = dims(b)
    if mode == "nn":
        m, k, n = ar, ac, bc
        assert br == k
    elif mode == "nt":
        m, k, n = ar, ac, br
        assert bc == k
    else:
        m, k, n = ac, ar, bc
        assert br == k
    m_lim, k_lim, n_lim = [m], [k], [n]
    if a.ndim == 3:
        (m_lim if mode == "tn" else k_lim).append(a.shape[-1])
    if b.ndim == 3:
        (k_lim if mode == "nt" else n_lim).append(b.shape[-1])
    if out_stack:
        n_lim.append(n // out_stack)
    tm = _pick(functools.reduce(math.gcd, m_lim), tm)
    tn = _pick(functools.reduce(math.gcd, n_lim), tn)
    k_unit = functools.reduce(math.gcd, k_lim)
    if tk is None:
        sa, sb, so = a.dtype.itemsize, b.dtype.itemsize, jnp.dtype(out_dtype).itemsize + (2 if also_bf16 else 0)
        fits = [t for t in range(LANES, k_unit + 1, LANES) if k_unit % t == 0 and
                2 * t * (tm * sa + tn * sb) + tm * tn * (2 * so + (4 if t < k else 0)) <= MATMUL_VMEM_BUDGET]
        tk = max(fits) if fits else _pick(k_unit, 512)
    else:
        tk = _pick(k_unit, tk)
    nk = k // tk

    def spec(z, brows, bcols, ridx, cidx):
        if z.ndim == 3:
            per = z.shape[-1] // bcols
            return pl.BlockSpec((None, brows, bcols),
                                lambda i, j, kk: (cidx(i, j, kk) // per, ridx(i, j, kk), cidx(i, j, kk) % per))
        return pl.BlockSpec((brows, bcols), lambda i, j, kk: (ridx(i, j, kk), cidx(i, j, kk)))

    gi = lambda i, j, kk: i
    gj = lambda i, j, kk: j
    gk = lambda i, j, kk: kk
    if mode == "nn":
        a_spec, b_spec = spec(a, tm, tk, gi, gk), spec(b, tk, tn, gk, gj)
        dn = (((1,), (0,)), ((), ()))
    elif mode == "nt":
        a_spec, b_spec = spec(a, tm, tk, gi, gk), spec(b, tn, tk, gj, gk)
        dn = (((1,), (1,)), ((), ()))
    else:
        a_spec, b_spec = spec(a, tk, tm, gk, gi), spec(b, tk, tn, gk, gj)
        dn = (((0,), (0,)), ((), ()))

    n_out = 2 if also_bf16 else 1

    deps = [] if dep is None else [dep]

    def body(a_ref, b_ref, *rest):
        rest = rest[len(deps):]
        o_refs, acc = rest[:n_out], rest[n_out:]
        part = lax.dot_general(a_ref[...].astype(BF16), b_ref[...].astype(BF16), dn, preferred_element_type=F32)

        def emit(val):
            for o_ref in o_refs:
                o_ref[...] = val.astype(o_ref.dtype)

        if nk == 1:
            emit(part)
            return
        acc_ref, = acc
        kk = pl.program_id(2)

        @pl.when(kk == 0)
        def _():
            acc_ref[...] = part

        @pl.when(kk > 0)
        def _():
            acc_ref[...] += part

        @pl.when(kk == nk - 1)
        def _():
            emit(acc_ref[...])

    if out_stack:
        per = (n // out_stack) // tn
        out_spec = pl.BlockSpec((None, tm, tn), lambda i, j, kk: (j // per, i, j % per))
        shape = (out_stack, m, n // out_stack)
    else:
        out_spec = pl.BlockSpec((tm, tn), lambda i, j, kk: (i, j))
        shape = (m, n)
    dtypes = [out_dtype, BF16][:n_out]
    res = pl.pallas_call(
        body, grid=(m // tm, n // tn, nk),
        in_specs=[a_spec, b_spec] + [pl.BlockSpec(memory_space=pl.ANY)] * len(deps), out_specs=[out_spec] * n_out,
        out_shape=[jax.ShapeDtypeStruct(shape, dt) for dt in dtypes],
        scratch_shapes=[pltpu.VMEM((tm, tn), F32)] if nk > 1 else [], name=name, compiler_params=_params(3),
    )(a, b, *deps)
    return res if also_bf16 else res[0]


def _all_gather(name, arrs, dep=None):
    n = len(arrs)
    deps = [] if dep is None else [dep]

    def body(*refs):
        ins, outs = refs[:n], refs[n + len(deps):2 * n + len(deps)]
        send_sems, recv_sems, local_sems = refs[2 * n + len(deps):]
        x, y, c = _dev()
        me, sib = (x, y, c), (x, y, 1 - c)
        chips = [(1 - x, y), (x, 1 - y), (1 - x, 1 - y)]

        def slot(p):
            return 4 * p[0] + 2 * p[1] + p[2]

        def copy(a, k, block, to, src=None):
            dst = outs[a].at[slot(block)]
            return pltpu.make_async_remote_copy(
                src_ref=dst if src is None else src, dst_ref=dst,
                send_sem=send_sems.at[7 * a + k], recv_sem=recv_sems.at[7 * a + k],
                device_id=to, device_id_type=MESH)

        mine = [pltpu.make_async_copy(ins[a], outs[a].at[slot(me)], local_sems.at[a]) for a in range(n)]
        for cp in mine:
            cp.start()
        first = []
        for a in range(n):
            first.append(copy(a, 0, me, sib, src=ins[a]))
            first += [copy(a, 1 + j, me, (*chip, c), src=ins[a]) for j, chip in enumerate(chips)]
        for cp in first:
            cp.start()
        passed = []
        for j, chip in enumerate(chips):
            for a in range(n):
                copy(a, 1 + j, (*chip, c), me).wait_recv()
                cp = copy(a, 4 + j, (*chip, c), sib)
                cp.start()
                passed.append(cp)
        for a in range(n):
            copy(a, 0, sib, me).wait_recv()
            for j, chip in enumerate(chips):
                copy(a, 4 + j, (*chip, 1 - c), me).wait_recv()
        for cp in first + passed:
            cp.wait_send()
        for cp in mine:
            cp.wait()

    any_spec = pl.BlockSpec(memory_space=pl.ANY)
    return pl.pallas_call(
        body, in_specs=[any_spec] * (n + len(deps)), out_specs=[any_spec] * n,
        out_shape=[jax.ShapeDtypeStruct((N_DEV,) + a.shape, a.dtype) for a in arrs],
        scratch_shapes=[pltpu.SemaphoreType.DMA((7 * n,)), pltpu.SemaphoreType.DMA((7 * n,)),
                        pltpu.SemaphoreType.DMA((n,))],
        name=name,
    )(*arrs, *deps)


def _grad_to_sibling(gds):
    n = len(gds)

    def body(*refs):
        g_refs, r_refs = refs[:n], refs[n:2 * n]
        send_sems, recv_sems = refs[2 * n:]
        x, y, c = _dev()
        cps = []
        for a in range(n):
            for k in range(4):
                cp = pltpu.make_async_remote_copy(
                    src_ref=g_refs[a].at[2 * k + (1 - c)], dst_ref=r_refs[a].at[k],
                    send_sem=send_sems.at[4 * a + k], recv_sem=recv_sems.at[4 * a + k],
                    device_id=(x, y, 1 - c), device_id_type=MESH)
                cp.start()
                cps.append(cp)
        for cp in cps:
            cp.wait()

    any_spec = pl.BlockSpec(memory_space=pl.ANY)
    return pl.pallas_call(
        body, in_specs=[any_spec] * n, out_specs=[any_spec] * n,
        out_shape=[jax.ShapeDtypeStruct((4,) + g.shape[1:], g.dtype) for g in gds],
        scratch_shapes=[pltpu.SemaphoreType.DMA((4 * n,)), pltpu.SemaphoreType.DMA((4 * n,))],
        name="grad_to_sibling",
    )(*gds)


def _chip_sum(name, gd, from_sib, c_arr):
    _, k, n = gd.shape
    tr = _pick(k, max(16, (1 << 20) // (4 * n)), 16)

    def body(c_ref, a_ref, b_ref, o_ref):
        o_ref[...] = (a_ref[...] + b_ref[...]).astype(o_ref.dtype)

    return pl.pallas_call(
        body,
        grid_spec=pltpu.PrefetchScalarGridSpec(
            num_scalar_prefetch=1, grid=(4, k // tr),
            in_specs=[pl.BlockSpec((1, tr, n), lambda q, i, cr: (2 * q + cr[0], i, 0)),
                      pl.BlockSpec((1, tr, n), lambda q, i, cr: (q, i, 0))],
            out_specs=pl.BlockSpec((1, tr, n), lambda q, i, cr: (q, i, 0))),
        out_shape=jax.ShapeDtypeStruct((4, k, n), BF16), name=name, compiler_params=_params(2),
    )(c_arr, gd, from_sib)


def _grad_to_chips(sums):
    n = len(sums)

    def body(*refs):
        s_refs, p_refs = refs[:n], refs[n:2 * n]
        send_sems, recv_sems, local_sems = refs[2 * n:]
        x, y, c = _dev()
        my_chip = 2 * x + y
        cps = []
        for a in range(n):
            local = pltpu.make_async_copy(s_refs[a].at[my_chip], p_refs[a].at[my_chip], local_sems.at[a])
            local.start()
            cps.append(local)
            for j, (px, py) in enumerate([(1 - x, y), (x, 1 - y), (1 - x, 1 - y)]):
                cp = pltpu.make_async_remote_copy(
                    src_ref=s_refs[a].at[2 * px + py], dst_ref=p_refs[a].at[my_chip],
                    send_sem=send_sems.at[3 * a + j], recv_sem=recv_sems.at[3 * a + j],
                    device_id=(px, py, c), device_id_type=MESH)
                cp.start()
                cps.append(cp)
        for cp in cps:
            cp.wait()

    any_spec = pl.BlockSpec(memory_space=pl.ANY)
    return pl.pallas_call(
        body, in_specs=[any_spec] * n, out_specs=[any_spec] * n,
        out_shape=[jax.ShapeDtypeStruct(s.shape, s.dtype) for s in sums],
        scratch_shapes=[pltpu.SemaphoreType.DMA((3 * n,)), pltpu.SemaphoreType.DMA((3 * n,)),
                        pltpu.SemaphoreType.DMA((n,))],
        name="grad_to_chips",
    )(*sums)


FLIPS = [(0, 0, 1), (0, 1, 0), (1, 0, 0), (0, 1, 1), (1, 0, 1), (1, 1, 0), (1, 1, 1)]
N_PEERS = len(FLIPS)
_HBM = pl.BlockSpec(memory_space=pltpu.HBM)
_SEM = pl.BlockSpec(memory_space=pltpu.SEMAPHORE)
_EFFECT = pltpu.SideEffectType.DATAFLOW_SIDE_EFFECTING


def _flip(x, y, c, f):
    return (1 - x if f[0] else x, 1 - y if f[1] else y, 1 - c if f[2] else c)


def _slot(p):
    return 4 * p[0] + 2 * p[1] + p[2]


def _exchange_copies(src_refs, land_refs, send_sems, recv_sems, gather):
    x, y, c = _dev()
    mine = _slot((x, y, c))
    cps = []
    for a, (src, land) in enumerate(zip(src_refs, land_refs)):
        for k, f in enumerate(FLIPS):
            peer = _flip(x, y, c, f)
            cps.append(pltpu.make_async_remote_copy(
                src_ref=src if gather else src.at[_slot(peer)], dst_ref=land.at[mine],
                send_sem=send_sems.at[N_PEERS * a + k], recv_sem=recv_sems.at[N_PEERS * a + k],
                device_id=peer, device_id_type=MESH))
    return cps


def _exchange_start(name, srcs, gather, after):
    n = len(srcs)
    lands = [lax.empty(((N_DEV,) + s.shape) if gather else s.shape, s.dtype) for s in srcs]

    def body(*refs):
        src_refs, land_refs = refs[:n], refs[n:2 * n]
        send_sems, recv_sems, local_sems = refs[2 * n + 1:2 * n + 4]
        token = refs[-1]
        if gather:
            x, y, c = _dev()
            for a in range(n):
                pltpu.make_async_copy(src_refs[a], land_refs[a].at[_slot((x, y, c))], local_sems.at[a]).start()
        for cp in _exchange_copies(src_refs, land_refs, send_sems, recv_sems, gather):
            cp.start()
        token[...] = jnp.zeros_like(token)

    hbm = lambda z: pltpu.HBM(z.shape, z.dtype)
    outs = pl.pallas_call(
        body, name=name,
        out_shape=(pltpu.SemaphoreType.DMA((N_PEERS * n,)), pltpu.SemaphoreType.DMA((N_PEERS * n,)),
                   pltpu.SemaphoreType.DMA((n,)), *[hbm(s) for s in srcs], *[hbm(z) for z in lands],
                   jax.ShapeDtypeStruct((8, LANES), F32)),
        in_specs=[_HBM] * (2 * n) + [pl.BlockSpec(memory_space=pl.ANY)],
        out_specs=(_SEM, _SEM, _SEM, *[_HBM] * (2 * n), pl.BlockSpec(memory_space=pltpu.VMEM)),
        input_output_aliases={i: 3 + i for i in range(2 * n)},
        compiler_params=pltpu.CompilerParams(has_side_effects=_EFFECT),
    )(*[pltpu.with_memory_space_constraint(z, pltpu.HBM) for z in list(srcs) + lands], after)
    return (outs[:3], outs[3:3 + n], outs[3 + n:3 + 2 * n], gather), outs[-1]


def _exchange_wait(name, handles, after):
    sems, srcs, lands, gather = handles
    n = len(srcs)

    def body(*refs):
        src_refs, land_refs = refs[:n], refs[n:2 * n]
        send_sems, recv_sems, local_sems = refs[2 * n:2 * n + 3]
        if gather:
            for a in range(n):
                pltpu.make_async_copy(src_refs[a], land_refs[a].at[0], local_sems.at[a]).wait()
        for cp in _exchange_copies(src_refs, land_refs, send_sems, recv_sems, gather):
            cp.wait_send()
            cp.wait_recv()

    hbm = lambda z: pltpu.HBM(z.shape, z.dtype)
    outs = pl.pallas_call(
        body, name=name, out_shape=tuple(hbm(z) for z in list(srcs) + list(lands)),
        in_specs=[_HBM] * (2 * n) + [_SEM] * 3 + [pl.BlockSpec(memory_space=pl.ANY)],
        out_specs=tuple([_HBM] * (2 * n)), input_output_aliases={i: i for i in range(2 * n)},
        compiler_params=pltpu.CompilerParams(has_side_effects=_EFFECT),
    )(*srcs, *lands, *sems, after)
    return list(outs[n:])


def _pack_rows(sizes, width, row_align):
    offs, r = [], 0
    for s in sizes:
        offs.append(r)
        r += -(-s // width)
    total = -(-r // row_align) * row_align
    return offs, total


def _pack(items, width, row_align, lead=()):
    nl = len(lead)
    sizes = [int(jnp.size(a)) // max(1, functools.reduce(lambda p, q: p * q, lead, 1)) for a in items]
    offs, total = _pack_rows(sizes, width, row_align)
    flat = []
    used = 0
    for a, s in zip(items, sizes):
        f = a.reshape(lead + (s,))
        pad = -(-s // width) * width - s
        if pad:
            f = jnp.pad(f, [(0, 0)] * nl + [(0, pad)])
        flat.append(f)
        used += s + pad
    tail = total * width - used
    if tail:
        flat.append(jnp.zeros(lead + (tail,), items[0].dtype))
    return jnp.concatenate(flat, axis=-1).reshape(lead + (total, width)), offs


def _unpack(packed, off, shape, lead=()):
    nl = len(lead)
    size = functools.reduce(lambda p, q: p * q, shape, 1)
    width = packed.shape[-1]
    rows = -(-size // width)
    blk = lax.slice_in_dim(packed, off, off + rows, axis=nl).reshape(lead + (rows * width,))
    return lax.slice_in_dim(blk, 0, size, axis=nl).reshape(lead + tuple(shape))


def _rms(x, g):
    return (x * lax.rsqrt(jnp.mean(x * x, axis=-1, keepdims=True) + RMS_EPS)) * g


def _norm_mod(x, g, sc, sh):
    return _rms(x, g) * (1.0 + sc) + sh


def _mix_fn(glu_a, glu_b, attn, ga, gs):
    return jax.nn.sigmoid(ga) * attn + jax.nn.sigmoid(gs) * (glu_a * jax.nn.sigmoid(glu_b))


def _s5_disc_fn(a_re, a_im, log_dt, b_re, b_im):
    dt = jnp.exp(log_dt)
    mag = jnp.exp(a_re * dt)
    lr, li = mag * jnp.cos(a_im * dt), mag * jnp.sin(a_im * dt)
    den = a_re * a_re + a_im * a_im
    zr = ((lr - 1.0) * a_re + li * a_im) / den
    zi = (li * a_re - (lr - 1.0) * a_im) / den
    return lr, li, zr[None] * b_re - zi[None] * b_im, zr[None] * b_im + zi[None] * b_re


def _adamw_fn(w, g, m, v):
    m = ADAM_B1 * m + (1.0 - ADAM_B1) * g
    v = ADAM_B2 * v + (1.0 - ADAM_B2) * jnp.square(g)
    m_hat = m / (1.0 - ADAM_B1 ** ADAM_STEP)
    v_hat = v / (1.0 - ADAM_B2 ** ADAM_STEP)
    delta = -ADAM_LR * (m_hat / (jnp.sqrt(v_hat) + ADAM_EPS) + ADAM_WD * w)
    return delta, m, v


def _adamw(name, parts, w, m, v):
    p, r, c = parts.shape
    tr = _pick(r, max(8, (1 << 21) // (4 * c * max(p, 2))), 8)

    def fn(pv, wv, mv, vv):
        g = pv[0]
        for i in range(1, p):
            g = g + pv[i]
        d, m2, v2 = _adamw_fn(wv, g, mv, vv)
        return g, d, m2, v2

    spec = pl.BlockSpec((tr, c), lambda i: (i, 0))
    return _tile_call(
        name, fn, (r // tr,), [parts, w, m, v],
        [pl.BlockSpec((p, tr, c), lambda i: (0, i, 0)), spec, spec, spec],
        [jax.ShapeDtypeStruct((r, c), F32)] * 4, [spec] * 4)


def _adamw_sharded(name, parts, own_src, w, m, v, place):
    _, k, n = parts.shape
    tr = _pick(k, max(16, (1 << 19) // (4 * n)), 16)

    def body(pl_ref, p_ref, a_ref, w_ref, m_ref, v_ref, g_ref, d_ref, m2_ref, v2_ref):
        own = a_ref[0]
        g = None
        for q in range(N_DEV):
            term = jnp.where(pl_ref[0] == q, own, p_ref[q].astype(F32))
            g = term if g is None else g + term
        d, m2, v2 = _adamw_fn(w_ref[...], g, m_ref[...], v_ref[...])
        g_ref[...] = g
        d_ref[...] = d
        m2_ref[...] = m2
        v2_ref[...] = v2

    spec = pl.BlockSpec((tr, n), lambda i, pr: (i, 0))
    return pl.pallas_call(
        body,
        grid_spec=pltpu.PrefetchScalarGridSpec(
            num_scalar_prefetch=1, grid=(k // tr,),
            in_specs=[pl.BlockSpec((N_DEV, tr, n), lambda i, pr: (0, i, 0)),
                      pl.BlockSpec((1, tr, n), lambda i, pr: (pr[1], i, 0)),
                      spec, spec, spec],
            out_specs=[spec] * 4),
        out_shape=[jax.ShapeDtypeStruct((k, n), F32)] * 4, name=name, compiler_params=_params(1),
    )(place, parts, own_src, w, m, v)


def _attn_mask(n, rows):
    qi = lax.broadcasted_iota(jnp.int32, (rows, 2 * ATT_BLOCK), 0) & (ATT_BLOCK - 1)
    kj = lax.broadcasted_iota(jnp.int32, (rows, 2 * ATT_BLOCK), 1)
    rel = qi + ATT_BLOCK - kj
    return (rel >= 0) & (rel < ATT_BLOCK) & ((kj >= ATT_BLOCK) | (n > 0))


def _attn_probs(q, k, sink, mask):
    s = lax.dot_general(q, k, (((1,), (1,)), ((), ())), preferred_element_type=F32) * (HEAD_DIM ** -0.5)
    s = jnp.where(mask, s, NEG_INF)
    m = jnp.maximum(jnp.max(s, axis=-1, keepdims=True), sink)
    p = jnp.exp(s - m)
    e_sink = jnp.exp(sink - m)
    inv = 1.0 / (jnp.sum(p, axis=-1, keepdims=True) + e_sink)
    return p * inv, e_sink * inv


def _attn_specs(qpk):
    blk = ATT_BLOCK
    q_spec = pl.BlockSpec((qpk, blk, HEAD_DIM), lambda h, n: (h, n, 0))
    cur = pl.BlockSpec((1, blk, HEAD_DIM), lambda h, n: (h, n, 0))
    prev = pl.BlockSpec((1, blk, HEAD_DIM), lambda h, n: (h, jnp.maximum(n - 1, 0), 0))
    sink_spec = pl.BlockSpec((1, qpk * blk, 1), lambda h, n: (h, 0, 0))
    return q_spec, cur, prev, sink_spec


def _attn_fwd(q, k, v, sinks):
    hq, l, _ = q.shape
    qpk = hq // N_KV_HEADS
    nb = l // ATT_BLOCK
    rows = qpk * ATT_BLOCK
    q_spec, cur, prev, sink_spec = _attn_specs(qpk)

    def body(q_ref, kp_ref, kc_ref, vp_ref, vc_ref, sink_ref, o_ref):
        mask = _attn_mask(pl.program_id(1), rows)
        kk = jnp.concatenate([kp_ref[0], kc_ref[0]], axis=0).astype(BF16)
        vv = jnp.concatenate([vp_ref[0], vc_ref[0]], axis=0).astype(BF16)
        p, _ = _attn_probs(q_ref[...].reshape(rows, HEAD_DIM).astype(BF16), kk, sink_ref[0], mask)
        o = jnp.dot(p.astype(BF16), vv, preferred_element_type=F32)
        o_ref[...] = o.reshape(qpk, ATT_BLOCK, HEAD_DIM).astype(o_ref.dtype)

    return pl.pallas_call(
        body, grid=(N_KV_HEADS, nb), in_specs=[q_spec, prev, cur, prev, cur, sink_spec],
        out_specs=q_spec, out_shape=jax.ShapeDtypeStruct((hq, l, HEAD_DIM), BF16),
        name="attn_fwd", compiler_params=_params(2),
    )(q, k, k, v, v, sinks)


def _attn_bwd(q, k, v, sinks, do):
    hq, l, _ = q.shape
    qpk = hq // N_KV_HEADS
    nb = l // ATT_BLOCK
    blk = ATT_BLOCK
    rows = qpk * blk
    q_spec, cur, prev, sink_spec = _attn_specs(qpk)
    part_spec = pl.BlockSpec((1, 1, 2 * blk, HEAD_DIM), lambda h, n: (h, n, 0, 0))
    dsink_spec = pl.BlockSpec((qpk, 1, LANES), lambda h, n: (h, 0, 0))
    tn = (((0,), (0,)), ((), ()))

    def body(q_ref, do_ref, kp_ref, kc_ref, vp_ref, vc_ref, sink_ref, dq_ref, dkp_ref, dvp_ref, dsink_ref):
        n = pl.program_id(1)
        mask = _attn_mask(n, rows)
        kk = jnp.concatenate([kp_ref[0], kc_ref[0]], axis=0).astype(BF16)
        vv = jnp.concatenate([vp_ref[0], vc_ref[0]], axis=0).astype(BF16)
        qb = q_ref[...].reshape(rows, HEAD_DIM).astype(BF16)
        do32 = do_ref[...].astype(F32).reshape(rows, HEAD_DIM)
        dob = do32.astype(BF16)
        p, p_sink = _attn_probs(qb, kk, sink_ref[0], mask)
        pb = p.astype(BF16)
        o = jnp.dot(pb, vv, preferred_element_type=F32)
        delta = jnp.sum(do32 * o, axis=-1, keepdims=True)
        dp = lax.dot_general(dob, vv, (((1,), (1,)), ((), ())), preferred_element_type=F32)
        ds = (p * (dp - delta) * (HEAD_DIM ** -0.5)).astype(BF16)
        dq = jnp.dot(ds, kk, preferred_element_type=F32)
        dq_ref[...] = dq.reshape(qpk, blk, HEAD_DIM).astype(dq_ref.dtype)
        dkp_ref[0, 0] = lax.dot_general(ds, qb, tn, preferred_element_type=F32)
        dvp_ref[0, 0] = lax.dot_general(pb, dob, tn, preferred_element_type=F32)
        dsr = p_sink * delta
        for g in range(qpk):
            dsg = jnp.broadcast_to(-_colsum(dsr[g * blk:(g + 1) * blk]), (1, LANES))

            @pl.when(n == 0)
            def _():
                dsink_ref[g] = dsg

            @pl.when(n > 0)
            def _():
                dsink_ref[g] += dsg


    part_shape = jax.ShapeDtypeStruct((N_KV_HEADS, nb, 2 * blk, HEAD_DIM), F32)
    dq, dkp, dvp, dsink = pl.pallas_call(
        body, grid=(N_KV_HEADS, nb), in_specs=[q_spec, q_spec, prev, cur, prev, cur, sink_spec],
        out_specs=[q_spec, part_spec, part_spec, dsink_spec],
        out_shape=[jax.ShapeDtypeStruct((hq, l, HEAD_DIM), BF16), part_shape, part_shape,
                   jax.ShapeDtypeStruct((hq, 1, LANES), F32)],
        name="attn_bwd", compiler_params=_params(2),
    )(q, do, k, k, v, v, sinks)

    def combine(a_cur, a_nxt, b_cur, b_nxt):
        last = pl.program_id(1) == nb - 1
        keep = jnp.where(last, 0.0, 1.0)
        return (a_cur[0, 0, blk:] + keep * a_nxt[0, 0, :blk])[None], (b_cur[0, 0, blk:] + keep * b_nxt[0, 0, :blk])[None]

    nxt_spec = pl.BlockSpec((1, 1, 2 * blk, HEAD_DIM), lambda h, n: (h, jnp.minimum(n + 1, nb - 1), 0, 0))
    kv_shape = jax.ShapeDtypeStruct((N_KV_HEADS, l, HEAD_DIM), BF16)
    dk, dv = _tile_call("attn_dkv", combine, (N_KV_HEADS, nb), [dkp, dkp, dvp, dvp],
                        [part_spec, nxt_spec, part_spec, nxt_spec], [kv_shape, kv_shape], [cur, cur])
    return dq, dk, dv, dsink


def _block_diag(m):
    j, gl, a, b = m.shape
    eye = jnp.eye(gl, dtype=m.dtype)
    return (m[:, :, :, None, :] * eye[None, :, None, :, None]).reshape(j, gl * a, gl * b)


def _diag_blocks(z, a):
    j = z.shape[0]
    gl = z.shape[1] // a
    b = z.shape[2] // gl
    d = jnp.diagonal(z.reshape(j, gl, a, gl, b), axis1=1, axis2=3)
    return d.transpose(0, 3, 1, 2)


def _s5_permute(src_ref, dst_ref, t_len):
    seg = t_len // 8
    for k in range(seg):
        dst_ref[8 * k:8 * k + 8, :] = src_ref[pl.ds(k, 8, stride=seg), :]


def _s5_unpermute(perm_ref, t_len, emit):
    per_seg = t_len // 64
    for m in range(t_len // 8):
        emit(8 * m, perm_ref[pl.ds(64 * (m % per_seg) + m // per_seg, 8, stride=8), :])


def _s5_powers(p_ref, lr, li, seg):
    hs = TILE_STATES

    def step(k, carry):
        pr, pi = carry
        p_ref[pl.ds(k, 1), 0:hs] = pr
        p_ref[pl.ds(k, 1), hs:2 * hs] = pi
        return lr * pr - li * pi, lr * pi + li * pr

    lax.fori_loop(0, seg, step, (lr, li))


def _s5_local_scan(x_ref, base, lr, li, seg, reverse):
    hs = TILE_STATES
    lr8, li8 = jnp.broadcast_to(lr, (8, hs)), jnp.broadcast_to(li, (8, hs))
    if reverse:
        li8 = -li8

    def step(i, carry):
        hr, hi = carry
        k = seg - 1 - i if reverse else i
        rows = pl.ds(pl.multiple_of(base + 8 * k, 8), 8)
        nr = lr8 * hr - li8 * hi + x_ref[rows, 0:hs]
        ni = lr8 * hi + li8 * hr + x_ref[rows, hs:2 * hs]
        x_ref[rows, 0:hs] = nr
        x_ref[rows, hs:2 * hs] = ni
        return nr, ni

    zero = jnp.zeros((8, hs), F32)
    return lax.fori_loop(0, seg, step, (zero, zero), unroll=2)


def _s5_carries(c_ref, e_ref, ends, start, pw_r, pw_i, reverse):
    hs = TILE_STATES
    e_ref[:, 0:hs] = ends[0]
    e_ref[:, hs:2 * hs] = ends[1]
    cr, ci = start
    if reverse:
        pw_i = -pw_i
    for s in (range(7, -1, -1) if reverse else range(8)):
        c_ref[s:s + 1, 0:hs] = cr
        c_ref[s:s + 1, hs:2 * hs] = ci
        er, ei = e_ref[s:s + 1, 0:hs], e_ref[s:s + 1, hs:2 * hs]
        cr, ci = er + pw_r * cr - pw_i * ci, ei + pw_r * ci + pw_i * cr
    return cr, ci


def _s5_states(u_perm_b16, bd_ref, x_ref, base, c_ref, e_ref, p_ref, lr, li, h_in, t_len):
    hs = TILE_STATES
    seg = t_len // 8
    x_ref[pl.ds(base, t_len), :] = jnp.dot(u_perm_b16, bd_ref[0], preferred_element_type=F32)
    ends = _s5_local_scan(x_ref, base, lr, li, seg, False)
    pw_r, pw_i = p_ref[seg - 1:seg, 0:hs], p_ref[seg - 1:seg, hs:2 * hs]
    h_out = _s5_carries(c_ref, e_ref, ends, h_in, pw_r, pw_i, False)
    cr, ci = c_ref[:, 0:hs], c_ref[:, hs:2 * hs]

    def fix(k, carry):
        rows = pl.ds(pl.multiple_of(base + 8 * k, 8), 8)
        pr, pi = p_ref[pl.ds(k, 1), 0:hs], p_ref[pl.ds(k, 1), hs:2 * hs]
        x_ref[rows, 0:hs] += pr * cr - pi * ci
        x_ref[rows, hs:2 * hs] += pr * ci + pi * cr
        return carry

    lax.fori_loop(0, seg, fix, 0, unroll=2)
    return h_out


def _s5_fwd(proj, u_off, bd, cbd, lam, dvec, t_len):
    l = proj.shape[0]
    nj = bd.shape[0]
    nch = l // t_len
    hs = TILE_STATES
    ub = u_off // LANES
    seg = t_len // 8
    assert t_len % 64 == 0

    def body(u_ref, bd_ref, cbd_ref, lam_ref, d_ref, y_ref, hst_ref, x_ref, h_ref, p_ref, c_ref, e_ref, up_ref, yp_ref):
        lr, li = lam_ref[0, 0:1, :], lam_ref[0, 1:2, :]

        @pl.when(pl.program_id(1) == 0)
        def _():
            h_ref[...] = jnp.zeros_like(h_ref)
            _s5_powers(p_ref, lr, li, seg)

        hst_ref[0, 0] = h_ref[...]
        _s5_permute(u_ref, up_ref, t_len)
        h_out = _s5_states(up_ref[...].astype(BF16), bd_ref, x_ref, 0, c_ref, e_ref, p_ref, lr, li,
                           (h_ref[:, 0:hs], h_ref[:, hs:2 * hs]), t_len)
        h_ref[:, 0:hs] = h_out[0]
        h_ref[:, hs:2 * hs] = h_out[1]
        yp_ref[...] = jnp.dot(x_ref[...].astype(BF16), cbd_ref[0], preferred_element_type=F32)
        dv = d_ref[0]

        def out(r0, rows):
            y_ref[r0:r0 + 8, :] = rows + dv * u_ref[r0:r0 + 8, :]

        _s5_unpermute(yp_ref, t_len, out)

    return pl.pallas_call(
        body, grid=(nj, nch),
        in_specs=[pl.BlockSpec((t_len, LANES), lambda j, c: (c, ub + j)),
                  pl.BlockSpec((1, LANES, 2 * hs), lambda j, c: (j, 0, 0)),
                  pl.BlockSpec((1, 2 * hs, LANES), lambda j, c: (j, 0, 0)),
                  pl.BlockSpec((1, 2, hs), lambda j, c: (j, 0, 0)),
                  pl.BlockSpec((1, 1, LANES), lambda j, c: (j, 0, 0))],
        out_specs=[pl.BlockSpec((t_len, LANES), lambda j, c: (c, j)),
                   pl.BlockSpec((1, 1, 1, 2 * hs), lambda j, c: (j, c, 0, 0))],
        out_shape=[jax.ShapeDtypeStruct((l, nj * LANES), F32),
                   jax.ShapeDtypeStruct((nj, nch, 1, 2 * hs), F32)],
        scratch_shapes=[pltpu.VMEM((t_len, 2 * hs), F32), pltpu.VMEM((1, 2 * hs), F32),
                        pltpu.VMEM((seg, 2 * hs), F32), pltpu.VMEM((8, 2 * hs), F32), pltpu.VMEM((8, 2 * hs), F32),
                        pltpu.VMEM((t_len, LANES), F32), pltpu.VMEM((t_len, LANES), F32)],
        name="s5_fwd", compiler_params=_params(2),
    )(proj, bd, cbd, lam, dvec)


def _s5_bwd(proj, u_off, dy, hst, bd, bdt, cbdt, lam, dvec, t_len):
    l = proj.shape[0]
    nj = bd.shape[0]
    nch = l // t_len
    hs = TILE_STATES
    ub = u_off // LANES
    seg = t_len // 8
    tn = (((0,), (0,)), ((), ()))
    assert t_len % 64 == 0

    def body(u_ref, dy_ref, hst_ref, bd_ref, bdt_ref, cbdt_ref, lam_ref, d_ref,
             du_ref, dbd_ref, dcbdt_ref, dlam_ref, dd_ref,
             x_ref, g_ref, gc_ref, p_ref, c_ref, e_ref, up_ref, dyp_ref, dup_ref):
        first = pl.program_id(1) == 0
        lr, li = lam_ref[0, 0:1, :], lam_ref[0, 1:2, :]

        @pl.when(first)
        def _():
            gc_ref[...] = jnp.zeros_like(gc_ref)
            _s5_powers(p_ref, lr, li, seg)

        _s5_permute(u_ref, up_ref, t_len)
        _s5_permute(dy_ref, dyp_ref, t_len)
        ub16, dyb16 = up_ref[...].astype(BF16), dyp_ref[...].astype(BF16)
        h0 = hst_ref[0, 0]
        _s5_states(ub16, bd_ref, x_ref, 8, c_ref, e_ref, p_ref, lr, li, (h0[:, 0:hs], h0[:, hs:2 * hs]), t_len)
        x_ref[0:8, :] = c_ref[...]
        g_ref[...] = jnp.dot(dyb16, cbdt_ref[0], preferred_element_type=F32)
        starts = _s5_local_scan(g_ref, 0, lr, li, seg, True)
        pw_r, pw_i = p_ref[seg - 1:seg, 0:hs], p_ref[seg - 1:seg, hs:2 * hs]
        g_out = _s5_carries(c_ref, e_ref, starts, (gc_ref[:, 0:hs], gc_ref[:, hs:2 * hs]), pw_r, pw_i, True)
        gc_ref[:, 0:hs] = g_out[0]
        gc_ref[:, hs:2 * hs] = g_out[1]
        cr, ci = c_ref[:, 0:hs], c_ref[:, hs:2 * hs]

        def fix(k, carry):
            alr, ali = carry
            rows = pl.ds(pl.multiple_of(8 * k, 8), 8)
            pr, pi = p_ref[pl.ds(seg - 1 - k, 1), 0:hs], p_ref[pl.ds(seg - 1 - k, 1), hs:2 * hs]
            gr = g_ref[rows, 0:hs] + pr * cr + pi * ci
            gi = g_ref[rows, hs:2 * hs] + pr * ci - pi * cr
            g_ref[rows, 0:hs] = gr
            g_ref[rows, hs:2 * hs] = gi
            hpr, hpi = x_ref[rows, 0:hs], x_ref[rows, hs:2 * hs]
            return alr + gr * hpr + gi * hpi, ali + gi * hpr - gr * hpi

        zero = jnp.zeros((8, hs), F32)
        alr, ali = lax.fori_loop(0, seg, fix, (zero, zero), unroll=2)
        alr, ali = _colsum(alr), _colsum(ali)
        g = g_ref[...].astype(BF16)
        h = x_ref[pl.ds(8, t_len), :].astype(BF16)
        dup_ref[...] = jnp.dot(g, bdt_ref[0], preferred_element_type=F32)
        dv = d_ref[0]

        def out(r0, rows):
            du_ref[r0:r0 + 8, :] = (rows + dv * dy_ref[r0:r0 + 8, :]).astype(du_ref.dtype)

        _s5_unpermute(dup_ref, t_len, out)
        sign = jnp.where(lax.broadcasted_iota(jnp.int32, (1, 2 * hs), 1) < hs, 1.0, -1.0)
        dbd = lax.dot_general(ub16, g, tn, preferred_element_type=F32)
        dcbdt = lax.dot_general(dyb16, h, tn, preferred_element_type=F32) * sign
        ddv = _colsum(dy_ref[...] * u_ref[...])

        @pl.when(first)
        def _():
            dbd_ref[0] = dbd
            dcbdt_ref[0] = dcbdt
            dlam_ref[0, 0:1, :] = alr
            dlam_ref[0, 1:2, :] = ali
            dd_ref[0] = ddv

        @pl.when(jnp.logical_not(first))
        def _():
            dbd_ref[0] += dbd
            dcbdt_ref[0] += dcbdt
            dlam_ref[0, 0:1, :] += alr
            dlam_ref[0, 1:2, :] += ali
            dd_ref[0] += ddv

    rev = lambda c: nch - 1 - c
    wide = pl.BlockSpec((1, LANES, 2 * hs), lambda j, c: (j, 0, 0))
    tall = pl.BlockSpec((1, 2 * hs, LANES), lambda j, c: (j, 0, 0))
    return pl.pallas_call(
        body, grid=(nj, nch),
        in_specs=[pl.BlockSpec((t_len, LANES), lambda j, c: (rev(c), ub + j)),
                  pl.BlockSpec((t_len, LANES), lambda j, c: (rev(c), j)),
                  pl.BlockSpec((1, 1, 1, 2 * hs), lambda j, c: (j, rev(c), 0, 0)),
                  wide, tall, wide,
                  pl.BlockSpec((1, 2, hs), lambda j, c: (j, 0, 0)),
                  pl.BlockSpec((1, 1, LANES), lambda j, c: (j, 0, 0))],
        out_specs=[pl.BlockSpec((t_len, LANES), lambda j, c: (rev(c), j)),
                   wide, wide,
                   pl.BlockSpec((1, 2, hs), lambda j, c: (j, 0, 0)),
                   pl.BlockSpec((1, 1, LANES), lambda j, c: (j, 0, 0))],
        out_shape=[jax.ShapeDtypeStruct((l, nj * LANES), BF16),
                   jax.ShapeDtypeStruct((nj, LANES, 2 * hs), F32),
                   jax.ShapeDtypeStruct((nj, LANES, 2 * hs), F32),
                   jax.ShapeDtypeStruct((nj, 2, hs), F32),
                   jax.ShapeDtypeStruct((nj, 1, LANES), F32)],
        scratch_shapes=[pltpu.VMEM((t_len + 8, 2 * hs), F32), pltpu.VMEM((t_len, 2 * hs), F32),
                        pltpu.VMEM((1, 2 * hs), F32), pltpu.VMEM((seg, 2 * hs), F32),
                        pltpu.VMEM((8, 2 * hs), F32), pltpu.VMEM((8, 2 * hs), F32),
                        pltpu.VMEM((t_len, LANES), F32), pltpu.VMEM((t_len, LANES), F32),
                        pltpu.VMEM((t_len, LANES), F32)],
        name="s5_bwd", compiler_params=_params(2),
    )(proj, dy, hst, bd, bdt, cbdt, lam, dvec)


def _full_spec(shape):
    nd = len(shape)
    return pl.BlockSpec(tuple(shape), lambda i: (0,) * nd)


def _sds(shape, dtype=F32):
    return jax.ShapeDtypeStruct(tuple(shape), dtype)


def kernel(x, c, ada_w, ada_b, norm_mix_g, w_in, attn_sinks, w_attn_proj, ssm_a_re, ssm_a_im, ssm_log_dt, ssm_b_re, ssm_b_im, ssm_c_re, ssm_c_im, ssm_d, w_ssm_glu, w_out, norm_ffn_g, w_ffn_up, ffn_conv_w, ffn_conv_b, w_ffn_down, final_g, loss_target, m_ada_w, m_ada_b, m_norm_mix_g, m_w_in, m_attn_sinks, m_w_attn_proj, m_ssm_a_re, m_ssm_a_im, m_ssm_log_dt, m_ssm_b_re, m_ssm_b_im, m_ssm_c_re, m_ssm_c_im, m_ssm_d, m_w_ssm_glu, m_w_out, m_norm_ffn_g, m_w_ffn_up, m_ffn_conv_w, m_ffn_conv_b, m_w_ffn_down, m_final_g, v_ada_w, v_ada_b, v_norm_mix_g, v_w_in, v_attn_sinks, v_w_attn_proj, v_ssm_a_re, v_ssm_a_im, v_ssm_log_dt, v_ssm_b_re, v_ssm_b_im, v_ssm_c_re, v_ssm_c_im, v_ssm_d, v_w_ssm_glu, v_w_out, v_norm_ffn_g, v_w_ffn_up, v_ffn_conv_w, v_ffn_conv_b, v_w_ffn_down, v_final_g):
    given = dict(locals())
    names = ['ada_w', 'ada_b', 'norm_mix_g', 'w_in', 'attn_sinks', 'w_attn_proj', 'ssm_a_re', 'ssm_a_im',
             'ssm_log_dt', 'ssm_b_re', 'ssm_b_im', 'ssm_c_re', 'ssm_c_im', 'ssm_d', 'w_ssm_glu', 'w_out',
             'norm_ffn_g', 'w_ffn_up', 'ffn_conv_w', 'ffn_conv_b', 'w_ffn_down', 'final_g']

    xs = x[0]
    tgt = loss_target[0]
    l, d = xs.shape
    attn_w = w_attn_proj.shape[1]
    ssm_w = w_ssm_glu.shape[1]
    hq = attn_sinks.shape[1]
    qpk = hq // N_KV_HEADS
    kv_w = N_KV_HEADS * HEAD_DIM
    n_groups = ssm_a_re.shape[1]
    dff = ffn_conv_b.shape[1]
    in_w = attn_w + 2 * kv_w + ssm_w + 2 * d
    nj = ssm_w // LANES
    off_k, off_v, off_u = attn_w, attn_w + kv_w, attn_w + 2 * kv_w
    off_ga, off_gs = off_u + ssm_w, off_u + ssm_w + d
    assert hq * HEAD_DIM == attn_w and n_groups * SSM_P == ssm_w and l % ATT_BLOCK == 0

    xi, yi, ci = _dev()
    idx = 4 * xi + 2 * yi + ci

    row_sharded = {'w_out': (d, d), 'w_ffn_down': (dff, d)}
    big = ['w_in', 'w_attn_proj', 'w_ssm_glu', 'w_out', 'w_ffn_up', 'w_ffn_down']
    spack, s_offs = _pack([c, ffn_conv_w[0]], LANES, 8)
    w16 = {k: given[k][0].astype(BF16) for k in big}
    wg_in, sg = _all_gather("gather_first", [w16['w_in'], spack])
    mixer_w = ['w_attn_proj', 'w_ssm_glu', 'w_out']
    h_mixer, tok = _exchange_start("gather_mixer_start", [w16[k] for k in mixer_w], True, wg_in)
    h_up, tok = _exchange_start("gather_ffn_up_start", [w16['w_ffn_up']], True, tok)
    h_down, tok = _exchange_start("gather_ffn_down_start", [w16['w_ffn_down']], True, tok)
    full = {'w_in': wg_in.transpose(1, 0, 2).reshape(d, in_w)}
    c_all = _unpack(sg, s_offs[0], (d,), lead=(N_DEV,))
    conv_w = _unpack(sg, s_offs[1], ffn_conv_w.shape[1:], lead=(N_DEV,)).transpose(1, 0, 2).reshape(3, dff)
    conv_b = ffn_conv_b

    mod_n = ada_w.shape[2]
    tcm = _pick(mod_n, 512)
    ada_b_mine = lax.dynamic_slice_in_dim(ada_b, idx * mod_n, mod_n, axis=1)

    def modpart_fn(cv, wv, bv):
        cond = cv * jax.nn.sigmoid(cv)
        return jnp.dot(cond.astype(BF16), wv.astype(BF16), preferred_element_type=F32) + bv, cond

    modp, cond_all = _tile_call(
        "ada_rows", modpart_fn, (mod_n // tcm,), [c_all, ada_w[0], ada_b_mine],
        [pl.BlockSpec((N_DEV, d), lambda j: (0, 0)), pl.BlockSpec((d, tcm), lambda j: (0, j)),
         pl.BlockSpec((1, tcm), lambda j: (0, j))],
        [_sds((N_DEV, mod_n)), _sds((N_DEV, d))],
        [pl.BlockSpec((N_DEV, tcm), lambda j: (0, j)), pl.BlockSpec((N_DEV, d), lambda j: (0, 0))])
    (modg,) = _all_gather("gather_ada_rows", [modp])
    mod = lax.dynamic_index_in_dim(modg, idx, axis=1, keepdims=False).reshape(1, N_DEV * mod_n)
    sh1, sc1, g1, sh2, sc2, g2 = [mod[:, i * d:(i + 1) * d] for i in range(6)]

    tr = _pick(l, 256, 8)
    trh = _pick(l, 128, 8)
    nr, nrh = l // tr, l // trh
    g_mix, g_ffn, g_fin = norm_mix_g + tok[0:1, 0:1], norm_ffn_g, final_g.reshape(1, d)

    def with_t(fn):
        def wrapped(*vals):
            out = fn(*vals)
            out = out if isinstance(out, tuple) else (out,)
            return out + (out[-1].T,)
        return wrapped

    h1, h1_t = _tile_call("norm_mod_mix", with_t(_norm_mod), (1, nr), [xs, g_mix, sc1, sh1],
                          [_t(tr, d), _v(d), _v(d), _v(d)], [_sds((l, d), BF16), _sds((d, l), BF16)],
                          [_t(tr, d), _tt(tr, d)])
    proj = _matmul("proj_in", h1, full['w_in'], "nn", tn=1280)

    def heads(z, n):
        return z.reshape(l, n, HEAD_DIM).transpose(1, 0, 2)

    qh = heads(proj[:, :attn_w], hq)
    kh = heads(proj[:, off_k:off_k + kv_w], N_KV_HEADS)
    vh = heads(proj[:, off_v:off_v + kv_w], N_KV_HEADS)
    sinks3 = jnp.repeat(attn_sinks.reshape(N_KV_HEADS, qpk), ATT_BLOCK, axis=1)[..., None]
    o_h = _attn_fwd(qh, kh, vh, sinks3)
    o2 = o_h.transpose(1, 0, 2).reshape(l, attn_w)

    gn = (n_groups, SSM_N)
    pgn = (SSM_P, n_groups, SSM_N)
    a_re, a_im, log_dt = ssm_a_re[0], ssm_a_im[0], ssm_log_dt[0].reshape(n_groups, 1)
    b_re, b_im = ssm_b_re[0].transpose(2, 0, 1), ssm_b_im[0].transpose(2, 0, 1)
    disc_ins = [a_re, a_im, log_dt, b_re, b_im]
    disc_specs = [_full_spec(gn), _full_spec(gn), _full_spec((n_groups, 1)), _full_spec(pgn), _full_spec(pgn)]
    lam_r, lam_i, bb_r, bb_i = _tile_call(
        "s5_discretise", _s5_disc_fn, (1,), disc_ins, disc_specs,
        [_sds(gn), _sds(gn), _sds(pgn), _sds(pgn)],
        [_full_spec(gn), _full_spec(gn), _full_spec(pgn), _full_spec(pgn)])

    def tiles_gpn(z):
        return z.reshape(SSM_P, nj, TILE_GROUPS, SSM_N).transpose(1, 2, 0, 3)

    bd = jnp.concatenate([_block_diag(tiles_gpn(bb_r)), _block_diag(tiles_gpn(bb_i))], axis=2).astype(BF16)
    c_r = ssm_c_re[0].reshape(nj, TILE_GROUPS, SSM_P, SSM_N).transpose(0, 1, 3, 2)
    c_i = (-ssm_c_im[0]).reshape(nj, TILE_GROUPS, SSM_P, SSM_N).transpose(0, 1, 3, 2)
    cbd = jnp.concatenate([_block_diag(c_r), _block_diag(c_i)], axis=1).astype(BF16)
    bdt, cbdt = bd.transpose(0, 2, 1), cbd.transpose(0, 2, 1)
    lam = jnp.stack([lam_r.reshape(nj, TILE_STATES), lam_i.reshape(nj, TILE_STATES)], axis=1)
    dvec = ssm_d[0].reshape(nj, 1, LANES)
    t_len = _pick(l, 512, 8)
    y, hst = _s5_fwd(proj, off_u, bd, cbd, lam, dvec, t_len)

    tcs, trg = _pick(ssm_w, 1024), _pick(l, 512, 8)
    gy = _tile_call("gelu", lambda v: jax.nn.gelu(v), (ssm_w // tcs, l // trg), [y], [_t(trg, tcs)],
                    [_sds((l, ssm_w), BF16)], [_t(trg, tcs)])[0]
    full.update(zip(mixer_w, _exchange_wait("gather_mixer_wait", h_mixer, gy)))
    full['w_out'] = full['w_out'].reshape(row_sharded['w_out'])
    attn = _matmul("attn_proj", o2, full['w_attn_proj'], "nn")
    glu = _matmul("ssm_glu", gy, full['w_ssm_glu'], "nn")

    tcd = 256 if d % 256 == 0 and off_ga % 256 == 0 else LANES
    assert d % tcd == 0 and off_ga % tcd == 0 and off_gs % tcd == 0
    trm = _pick(l, 1024, 8)
    mix_in_specs = [_t(trm, tcd), _t(trm, tcd, d), _t(trm, tcd), _t(trm, tcd, off_ga), _t(trm, tcd, off_gs)]
    mixed = _tile_call("gate_mix", _mix_fn, (d // tcd, l // trm), [glu, glu, attn, proj, proj], mix_in_specs,
                       [_sds((l, d), BF16)], [_t(trm, tcd)])[0]
    mixout = _matmul("mix_out", mixed, full['w_out'], "nn")

    def res_norm_fn(xv, mo, g1v, gv, scv, shv):
        x2v = xv + g1v * mo
        return x2v, _norm_mod(x2v, gv, scv, shv)

    x2, h2, h2_t = _tile_call("residual_norm_mod_ffn", with_t(res_norm_fn), (1, nr), [xs, mixout, g1, g_ffn, sc2, sh2],
                              [_t(tr, d), _t(tr, d), _v(d), _v(d), _v(d), _v(d)],
                              [_sds((l, d)), _sds((l, d), BF16), _sds((d, l), BF16)],
                              [_t(tr, d), _t(tr, d), _tt(tr, d)])
    full['w_ffn_up'], = _exchange_wait("gather_ffn_up_wait", h_up, h2)
    up = _matmul("ffn_up", h2, full['w_ffn_up'], "nn", out_dtype=BF16, tn=1408)

    tcf, trc = _pick(dff, 1408), _pick(l, 512, 8)
    assert dff % tcf == 0
    ncf = dff // tcf

    taps = [conv_w[i:i + 1] for i in range(3)]

    def conv_gate(gp, gp_prev, w0, w1, w2, bv):
        gp = gp.astype(F32)
        prev = jnp.where(pl.program_id(1) == 0, 0.0, 1.0) * gp_prev.astype(F32)
        ext = jnp.concatenate([prev, gp], axis=0)
        m1 = pltpu.roll(ext, 1, 0)[HALO:]
        m2 = pltpu.roll(ext, 2, 0)[HALO:]
        return w0 * m2 + w1 * m1 + w2 * gp + bv, m1, m2

    def convglu_fn(gp, gp_prev, val, w0, w1, w2, bv):
        gate, _, _ = conv_gate(gp, gp_prev, w0, w1, w2, bv)
        return gate * jax.nn.sigmoid(gate) * val.astype(F32)

    act, act_t = _tile_call("conv_swiglu", with_t(convglu_fn), (ncf, l // trc), [up, up, up] + taps + [conv_b],
                            [_t(trc, tcf), _prev_rows(trc, tcf), _t(trc, tcf, dff)] + [_v(tcf)] * 4,
                            [_sds((l, dff), BF16), _sds((dff, l), BF16)], [_t(trc, tcf), _tt(trc, tcf)])
    full['w_ffn_down'] = _exchange_wait("gather_ffn_down_wait", h_down, act)[0].reshape(row_sharded['w_ffn_down'])
    ffn = _matmul("ffn_down", act, full['w_ffn_down'], "nn", tm=512)

    def final_fn(x2v, fv, g2v, gv, tv):
        rows = x2v.shape[0]

        def loss_of(x2a, fa, g2a, ga):
            out = _rms(x2a + g2a * fa, ga)
            err = out - tv
            return 0.5 * _colsum(jnp.mean(err * err, axis=-1, keepdims=True))

        loss, vjp = jax.vjp(loss_of, x2v, fv, _bc(g2v, rows), _bc(gv, rows))
        dx3, dffn, dg2, dgf = vjp(jnp.ones((1, 1), F32))
        return jnp.broadcast_to(loss, (1, LANES)), dx3, dffn, _colsum(dg2), _colsum(dgf)

    loss_p, dx3, dffn, dg2, dg_fin = _tile_call(
        "loss_final_norm", final_fn, (1, nrh), [x2, ffn, g2, g_fin, tgt],
        [_t(trh, d), _t(trh, d), _v(d), _v(d), _t(trh, d)],
        [_sds((1, LANES)), _sds((l, d)), _sds((l, d), BF16), _sds((1, d)), _sds((1, d))],
        [_v(LANES), _t(trh, d), _t(trh, d), _v(d), _v(d)], acc=(0, 3, 4))
    loss = lax.psum(loss_p[0, 0], ("x", "y", "c"))

    dact = _matmul("d_act", dffn, full['w_ffn_down'], "nt", out_dtype=BF16, tn=1408, dep=loss.reshape(1, 1))
    gd, gd16, pending = {}, {}, []
    dw_down, dw_down16 = _matmul("dw_ffn_down", act_t, dffn, "nn", tm=512, also_bf16=True)
    gd['w_ffn_down'], gd16['w_ffn_down'] = [z.reshape((N_DEV,) + w_ffn_down.shape[1:]) for z in (dw_down, dw_down16)]
    handle, tok = _exchange_start("grad_ffn_down_start", [gd16['w_ffn_down']], False, loss.reshape(1, 1))
    pending.append((['w_ffn_down'], handle))
    conv_b_bwd = conv_b + tok[0:1, 0:1]

    def convglu_bwd_fn(gp, gp_prev, gp_next, val, val_next, da, da_next, w0, w1, w2, bv):
        rows = gp.shape[0]
        i = pl.program_id(1)
        gp, val, da = gp.astype(F32), val.astype(F32), da.astype(F32)
        prev = jnp.where(i == 0, 0.0, 1.0) * gp_prev.astype(F32)
        more = jnp.where(i == pl.num_programs(1) - 1, 0.0, 1.0)
        ext = jnp.concatenate([prev, gp, gp_next.astype(F32)], axis=0)
        cur = ext[HALO:]
        m1 = pltpu.roll(ext, 1, 0)[HALO:]
        m2 = pltpu.roll(ext, 2, 0)[HALO:]
        gate = w0 * m2 + w1 * m1 + w2 * cur + bv
        sg = jax.nn.sigmoid(gate)
        val_e = jnp.concatenate([val, val_next.astype(F32)], axis=0)
        da_e = jnp.concatenate([da, more * da_next.astype(F32)], axis=0)
        dgate = da_e * val_e * (sg * (1.0 + gate * (1.0 - sg)))
        p1 = pltpu.roll(dgate, rows + HALO - 1, 0)[:rows]
        p2 = pltpu.roll(dgate, rows + HALO - 2, 0)[:rows]
        dg = dgate[:rows]
        dgp = w2 * dg + w1 * p1 + w0 * p2
        dval = da * (gate[:rows] * sg[:rows])
        return (jnp.stack([dgp, dval], axis=0), _colsum(dg), _colsum(dg * m2[:rows]), _colsum(dg * m1[:rows]),
                _colsum(dg * gp))

    dup, dconv_b, dcw0, dcw1, dcw2 = _tile_call(
        "conv_swiglu_bwd", convglu_bwd_fn, (ncf, nr), [up, up, up, up, up, dact, dact] + taps + [conv_b_bwd],
        [_t(tr, tcf), _prev_rows(tr, tcf), _next_rows(tr, tcf, l), _t(tr, tcf, dff), _next_rows(tr, tcf, l, dff),
         _t(tr, tcf), _next_rows(tr, tcf, l)] + [_v(tcf)] * 4,
        [_sds((2, l, dff), BF16)] + [_sds((1, dff))] * 4, [_st(tr, tcf)] + [_v(tcf)] * 4, acc=(1, 2, 3, 4))
    dh2 = _matmul("d_h2", dup, full['w_ffn_up'], "nt")
    gd['w_ffn_up'], gd16['w_ffn_up'] = _matmul("dw_ffn_up", h2_t, dup, "nn", tm=512, tn=1408, out_stack=N_DEV, also_bf16=True)
    handle, tok = _exchange_start("grad_ffn_up_start", [gd16['w_ffn_up']], False, gd['w_ffn_up'])
    pending.append((['w_ffn_up'], handle))
    g_ffn_bwd = g_ffn + tok[0:1, 0:1]

    def res_norm_bwd_fn(xv, mo, g1v, gv, scv, shv, dhv, dxv):
        rows = xv.shape[0]
        _, vjp = jax.vjp(res_norm_fn, xv, mo, _bc(g1v, rows), _bc(gv, rows), _bc(scv, rows), _bc(shv, rows))
        dx, dmo, dg1v, dgv, dscv, dshv = vjp((dxv, dhv))
        return dx, dmo, _colsum(dg1v), _colsum(dgv), _colsum(dscv), _colsum(dshv)

    dx2, dmixout, dg1, dg_ffn, dsc2, dsh2 = _tile_call(
        "residual_norm_mod_ffn_bwd", res_norm_bwd_fn, (1, nrh), [xs, mixout, g1, g_ffn_bwd, sc2, sh2, dh2, dx3],
        [_t(trh, d), _t(trh, d), _v(d), _v(d), _v(d), _v(d), _t(trh, d), _t(trh, d)],
        [_sds((l, d)), _sds((l, d), BF16)] + [_sds((1, d))] * 4,
        [_t(trh, d), _t(trh, d)] + [_v(d)] * 4, acc=(2, 3, 4, 5))

    dmixed = _matmul("d_mixed", dmixout, full['w_out'], "nt")
    dw_out, dw_out16 = _matmul("dw_out", mixed, dmixout, "tn", also_bf16=True)
    gd['w_out'], gd16['w_out'] = [z.reshape((N_DEV,) + w_out.shape[1:]) for z in (dw_out, dw_out16)]

    def mix_bwd_fn(ga_, gb_, at, pa, ps, dm):
        _, vjp = jax.vjp(_mix_fn, ga_, gb_, at, pa, ps)
        da, db, dat, dpa, dps = vjp(dm)
        return jnp.stack([da, db], axis=0), dat, dpa, dps

    dglu, dattn, dga, dgs = _tile_call(
        "gate_mix_bwd", mix_bwd_fn, (d // tcd, l // trm), [glu, glu, attn, proj, proj, dmixed],
        mix_in_specs + [_t(trm, tcd)],
        [_sds((2, l, d), BF16)] + [_sds((l, d), BF16)] * 3, [_st(trm, tcd)] + [_t(trm, tcd)] * 3)

    dgy = _matmul("d_gelu_y", dglu, full['w_ssm_glu'], "nt")
    gd['w_ssm_glu'], gd16['w_ssm_glu'] = _matmul("dw_ssm_glu", gy, dglu, "tn", out_stack=N_DEV, also_bf16=True)

    def gelu_bwd_fn(yv, dv):
        _, vjp = jax.vjp(lambda z: jax.nn.gelu(z), yv)
        return vjp(dv)[0]

    dy = _tile_call("gelu_bwd", gelu_bwd_fn, (ssm_w // tcs, l // trg), [y, dgy], [_t(trg, tcs), _t(trg, tcs)],
                    [_sds((l, ssm_w))], [_t(trg, tcs)])[0]
    du, dbd, dcbdt, dlam, dd_tiles = _s5_bwd(proj, off_u, dy, hst, bd, bdt, cbdt, lam, dvec, t_len)

    def gpn_of(z):
        return z.transpose(2, 0, 1, 3).reshape(pgn)

    dbb_r = gpn_of(_diag_blocks(dbd[:, :, :TILE_STATES], SSM_P))
    dbb_i = gpn_of(_diag_blocks(dbd[:, :, TILE_STATES:], SSM_P))
    dc_re = _diag_blocks(dcbdt[:, :, :TILE_STATES], SSM_P).reshape(n_groups, SSM_P, SSM_N)
    dc_im = _diag_blocks(dcbdt[:, :, TILE_STATES:], SSM_P).reshape(n_groups, SSM_P, SSM_N)
    dlam_r, dlam_i = dlam[:, 0].reshape(gn), dlam[:, 1].reshape(gn)

    def disc_bwd_fn(ar, ai, ld, br, bi, dlr, dli, dbr, dbi):
        _, vjp = jax.vjp(_s5_disc_fn, ar, ai, ld, br, bi)
        return vjp((dlr, dli, dbr, dbi))

    da_re, da_im, dlog_dt, db_re, db_im = _tile_call(
        "s5_discretise_bwd", disc_bwd_fn, (1,), disc_ins + [dlam_r, dlam_i, dbb_r, dbb_i],
        disc_specs + [_full_spec(gn), _full_spec(gn), _full_spec(pgn), _full_spec(pgn)],
        [_sds(gn), _sds(gn), _sds((n_groups, 1)), _sds(pgn), _sds(pgn)], disc_specs)

    do2 = _matmul("d_attn_heads", dattn, full['w_attn_proj'], "nt")
    gd['w_attn_proj'], gd16['w_attn_proj'] = _matmul("dw_attn_proj", o2, dattn, "tn", out_stack=N_DEV, also_bf16=True)
    handle, tok = _exchange_start("grad_mixer_start", [gd16[k] for k in mixer_w], False, gd['w_attn_proj'])
    pending.append((mixer_w, handle))
    do_h = heads(do2.astype(BF16), hq)
    dq_h, dk_h, dv_h, dsink = _attn_bwd(qh, kh, vh, sinks3 + tok[0:1, 0:1], do_h)

    def unheads(z):
        return z.transpose(1, 0, 2).reshape(l, z.shape[0] * HEAD_DIM)

    dproj = jnp.concatenate([unheads(dq_h), unheads(dk_h), unheads(dv_h), du, dga, dgs], axis=1)
    dh1 = _matmul("d_h1", dproj, full['w_in'], "nt", tm=512)

    def norm_bwd_fn(xv, gv, scv, shv, dhv, dxv):
        rows = xv.shape[0]
        _, vjp = jax.vjp(_norm_mod, xv, _bc(gv, rows), _bc(scv, rows), _bc(shv, rows))
        dx, dgv, dscv, dshv = vjp(dhv)
        return dx + dxv, _colsum(dgv), _colsum(dscv), _colsum(dshv)

    grad_x, dg_mix, dsc1, dsh1 = _tile_call(
        "norm_mod_mix_bwd", norm_bwd_fn, (1, nrh), [xs, g_mix, sc1, sh1, dh1, dx2],
        [_t(trh, d), _v(d), _v(d), _v(d), _t(trh, d), _t(trh, d)],
        [_sds((l, d))] + [_sds((1, d))] * 3, [_t(trh, d)] + [_v(d)] * 3, acc=(1, 2, 3))

    dmod = jnp.concatenate([dsh1, dsc1, dg1, dsh2, dsc2, dg2], axis=1)
    small = ['ada_b', 'norm_mix_g', 'attn_sinks', 'ssm_a_re', 'ssm_a_im', 'ssm_log_dt', 'ssm_b_re', 'ssm_b_im',
             'ssm_c_re', 'ssm_c_im', 'ssm_d', 'norm_ffn_g', 'ffn_conv_b', 'final_g']
    small_grads = {
        'ada_b': dmod, 'norm_mix_g': dg_mix, 'attn_sinks': dsink[:, 0, 0], 'ssm_a_re': da_re, 'ssm_a_im': da_im,
        'ssm_log_dt': dlog_dt, 'ssm_b_re': db_re.transpose(1, 2, 0), 'ssm_b_im': db_im.transpose(1, 2, 0),
        'ssm_c_re': dc_re, 'ssm_c_im': dc_im, 'ssm_d': dd_tiles, 'norm_ffn_g': dg_ffn, 'ffn_conv_b': dconv_b,
        'final_g': dg_fin}
    gs_pack, sm_offs = _pack([small_grads[k] for k in small], LANES, 8)
    h_small, tok = _exchange_start("gather_small_grads_start", [gs_pack], True, grad_x)

    dw_in, dw_in16 = _matmul("dw_in", h1_t, dproj, "nn", tm=512, tn=1280, also_bf16=True, dep=tok)
    dcw = jnp.concatenate([dcw0, dcw1, dcw2], axis=0)
    shard_in, shard_cw = w_in.shape[1:], ffn_conv_w.shape[1:]
    gd16['w_in'] = dw_in16.reshape(shard_in[0], N_DEV, shard_in[1]).transpose(1, 0, 2)
    own_in = lax.dynamic_slice_in_dim(dw_in, idx * shard_in[1], shard_in[1], axis=1)[None]
    gd['ffn_conv_w'] = dcw.reshape(shard_cw[0], N_DEV, shard_cw[1]).transpose(1, 0, 2)
    gd16['ffn_conv_w'] = gd['ffn_conv_w'].astype(BF16)
    handle, tok = _exchange_start("grad_in_start", [gd16['w_in'], gd16['ffn_conv_w']], False, dw_in)
    pending.append((['w_in', 'ffn_conv_w'], handle))
    gs_all, = _exchange_wait("gather_small_grads_wait", h_small, tok)
    ws_pack, _ = _pack([given[k] for k in small], LANES, 8)
    ms_pack, _ = _pack([given['m_' + k] for k in small], LANES, 8)
    vs_pack, _ = _pack([given['v_' + k] for k in small], LANES, 8)
    small_out = _adamw("adamw_replicated", gs_all, ws_pack, ms_pack, vs_pack)

    dmod_all = _unpack(gs_all, sm_offs[0], (N_DEV * mod_n,), lead=(N_DEV,))
    dmod_mine = lax.dynamic_slice_in_dim(dmod_all, idx * mod_n, mod_n, axis=1)
    kpad = LANES - N_DEV
    cond_t = jnp.pad(cond_all.T, ((0, 0), (0, kpad)))
    dmod_pad = jnp.pad(dmod_mine, ((0, kpad), (0, 0)))
    g_ada_w = _matmul("dw_ada", cond_t, dmod_pad, "nn")
    ada_out = _adamw("adamw_ada_w", g_ada_w[None], ada_w[0], m_ada_w[0], v_ada_w[0])

    sharded = big + ['ffn_conv_w']
    sharded_out = {}

    def finish(group, handle, after):
        for k, parts in zip(group, _exchange_wait("grad_" + group[0] + "_wait", handle, after)):
            own_src, own_at = (own_in, 0 * idx) if k == 'w_in' else (gd[k], idx)
            sharded_out[k] = _adamw_sharded("adamw_" + k, parts, own_src, given[k][0], given['m_' + k][0],
                                            given['v_' + k][0], jnp.stack([idx, own_at]).astype(jnp.int32))

    for group, handle in pending[:-1]:
        finish(group, handle, ada_out[0])
    done = functools.reduce(lambda p, q: p + q, [sharded_out[k][1][0:1, 0:1] for g_, _ in pending[:-1] for k in g_])
    finish(*pending[-1], done)

    results = [{}, {}, {}, {}]
    for which in range(4):
        for k, off in zip(small, sm_offs):
            results[which][k] = _unpack(small_out[which], off, given[k].shape)
        for k in sharded:
            results[which][k] = sharded_out[k][which][None]
        results[which]['ada_w'] = ada_out[which][None]
    outs = [loss, grad_x[None]]
    for which in range(4):
        outs += [results[which][k] for k in names]
    return tuple(outs)
```

```python
import functools
import math

import jax
import jax.numpy as jnp
from jax import lax
from jax.experimental import pallas as pl
from jax.experimental.pallas import tpu as pltpu

F32, BF16 = jnp.float32, jnp.bfloat16
MESH = pl.DeviceIdType.MESH
N_DEV = 8

HEAD_DIM = 64
N_KV_HEADS = 2
ATT_BLOCK = 128
NEG_INF = -1e30
SSM_P = 16
SSM_N = 64
LANES = 128
TILE_GROUPS = LANES // SSM_P
TILE_STATES = TILE_GROUPS * SSM_N
RMS_EPS = 1e-6
ADAM_LR, ADAM_B1, ADAM_B2, ADAM_EPS, ADAM_WD, ADAM_STEP = 0.001, 0.9, 0.999, 1e-08, 0.01, 10
VMEM_LIMIT = 56 * 1024 * 1024
MATMUL_VMEM_BUDGET = 44 * 1024 * 1024


def _params(n_axes):
    return pltpu.CompilerParams(dimension_semantics=("arbitrary",) * n_axes, vmem_limit_bytes=VMEM_LIMIT)


def _pick(dim, pref, align=128):
    if dim <= align:
        return dim
    t = (min(pref, dim) // align) * align
    while t > align and dim % t:
        t -= align
    assert dim % t == 0, (dim, pref, align)
    return t


def _dev():
    return lax.axis_index("x"), lax.axis_index("y"), lax.axis_index("c")


def _tile_call(name, fn, grid, ins, in_specs, out_shapes, out_specs, acc=()):
    n_in, n_out = len(ins), len(out_shapes)
    acc_axis = len(grid) - 1

    def body(*refs):
        vals = fn(*[r[...] for r in refs[:n_in]])
        if not isinstance(vals, (tuple, list)):
            vals = (vals,)
        assert len(vals) == n_out
        for i, (r, v) in enumerate(zip(refs[n_in:], vals)):
            v = v.astype(r.dtype)
            if i in acc:
                first = pl.program_id(acc_axis) == 0

                @pl.when(first)
                def _():
                    r[...] = v

                @pl.when(jnp.logical_not(first))
                def _():
                    r[...] += v
            else:
                r[...] = v

    return pl.pallas_call(
        body, grid=grid, in_specs=in_specs, out_specs=out_specs, out_shape=out_shapes, name=name,
        compiler_params=_params(len(grid)),
    )(*ins)


def _t(tr, tc, off=0):
    return pl.BlockSpec((tr, tc), lambda j, i: (i, j + off // tc))


def _tt(tr, tc):
    return pl.BlockSpec((tc, tr), lambda j, i: (j, i))


def _v(tc, off=0, rows=1):
    return pl.BlockSpec((rows, tc), lambda j, i: (0, j + off // tc))


HALO = 16


def _prev_rows(tr, tc, off=0):
    return pl.BlockSpec((HALO, tc), lambda j, i: (jnp.maximum(i * (tr // HALO) - 1, 0), j + off // tc))


def _next_rows(tr, tc, nrows, off=0):
    return pl.BlockSpec((HALO, tc),
                        lambda j, i: (jnp.minimum((i + 1) * (tr // HALO), nrows // HALO - 1), j + off // tc))


def _st(tr, tc):
    return pl.BlockSpec((2, tr, tc), lambda j, i: (0, i, j))


def _bc(v, rows):
    return jnp.broadcast_to(v, (rows, v.shape[-1]))


def _colsum(v):
    return jnp.sum(v, axis=0, keepdims=True)


def _matmul(name, a, b, mode, out_dtype=F32, tm=1024, tn=1024, tk=None, out_stack=None, also_bf16=False, dep=None,
            fold=1):
    def dims(z):
        return (z.shape[-2], z.shape[-1] * (z.shape[0] if z.ndim == 3 else 1))

    ar, ac = dims(a)
    br, bc = dims(b)
    if mode == "nn":
        m, k, n = ar, ac, bc
        assert br == k
    elif mode == "nt":
        m, k, n = ar, ac, br
        assert bc == k
    else:
        m, k, n = ac, ar, bc
        assert br == k
    m_lim, k_lim, n_lim = [m], [k], [n]
    if a.ndim == 3:
        (m_lim if mode == "tn" else k_lim).append(a.shape[-1])
    if b.ndim == 3:
        (k_lim if mode == "nt" else n_lim).append(b.shape[-1])
    if out_stack:
        n_lim.append(n // out_stack)
    tm = _pick(functools.reduce(math.gcd, m_lim), tm)
    tn = _pick(functools.reduce(math.gcd, n_lim), tn)
    k_unit = functools.reduce(math.gcd, k_lim)
    if tk is None:
        sa, sb, so = a.dtype.itemsize, b.dtype.itemsize, jnp.dtype(out_dtype).itemsize + (2 if also_bf16 else 0)
        fits = [t for t in range(LANES, k_unit + 1, LANES) if k_unit % t == 0 and
                2 * t * (tm * sa + tn * sb) + tm * tn * (2 * so + (4 if t < k else 0)) <= MATMUL_VMEM_BUDGET]
        tk = max(fits) if fits else _pick(k_unit, 512)
    else:
        tk = _pick(k_unit, tk)
    assert (k // tk) % fold == 0
    nk = k // (tk * fold)

    def spec(z, brows, bcols, ridx, cidx):
        if z.ndim == 3:
            per = z.shape[-1] // bcols
            return pl.BlockSpec((None, brows, bcols),
                                lambda i, j, kk: (cidx(i, j, kk) // per, ridx(i, j, kk), cidx(i, j, kk) % per))
        return pl.BlockSpec((brows, bcols), lambda i, j, kk: (ridx(i, j, kk), cidx(i, j, kk)))

    gi = lambda i, j, kk: i
    gj = lambda i, j, kk: j
    a_specs, b_specs = [], []
    for f in range(fold):
        gk = lambda i, j, kk, f=f: fold * kk + f
        if mode == "nn":
            a_specs.append(spec(a, tm, tk, gi, gk))
            b_specs.append(spec(b, tk, tn, gk, gj))
            dn = (((1,), (0,)), ((), ()))
        elif mode == "nt":
            a_specs.append(spec(a, tm, tk, gi, gk))
            b_specs.append(spec(b, tn, tk, gj, gk))
            dn = (((1,), (1,)), ((), ()))
        else:
            a_specs.append(spec(a, tk, tm, gk, gi))
            b_specs.append(spec(b, tk, tn, gk, gj))
            dn = (((0,), (0,)), ((), ()))

    n_out = 2 if also_bf16 else 1

    deps = [] if dep is None else [dep]

    def body(*refs):
        a_refs, b_refs = refs[:fold], refs[fold:2 * fold]
        rest = refs[2 * fold + len(deps):]
        o_refs, acc = rest[:n_out], rest[n_out:]
        part = None
        for a_ref, b_ref in zip(a_refs, b_refs):
            one = lax.dot_general(a_ref[...].astype(BF16), b_ref[...].astype(BF16), dn, preferred_element_type=F32)
            part = one if part is None else part + one

        def emit(val):
            for o_ref in o_refs:
                o_ref[...] = val.astype(o_ref.dtype)

        if nk == 1:
            emit(part)
            return
        acc_ref, = acc
        kk = pl.program_id(2)

        @pl.when(kk == 0)
        def _():
            acc_ref[...] = part

        @pl.when(kk > 0)
        def _():
            acc_ref[...] += part

        @pl.when(kk == nk - 1)
        def _():
            emit(acc_ref[...])

    if out_stack:
        per = (n // out_stack) // tn
        out_spec = pl.BlockSpec((None, tm, tn), lambda i, j, kk: (j // per, i, j % per))
        shape = (out_stack, m, n // out_stack)
    else:
        out_spec = pl.BlockSpec((tm, tn), lambda i, j, kk: (i, j))
        shape = (m, n)
    dtypes = [out_dtype, BF16][:n_out]
    res = pl.pallas_call(
        body, grid=(m // tm, n // tn, nk),
        in_specs=a_specs + b_specs + [pl.BlockSpec(memory_space=pl.ANY)] * len(deps), out_specs=[out_spec] * n_out,
        out_shape=[jax.ShapeDtypeStruct(shape, dt) for dt in dtypes],
        scratch_shapes=[pltpu.VMEM((tm, tn), F32)] if nk > 1 else [], name=name, compiler_params=_params(3),
    )(*[a] * fold, *[b] * fold, *deps)
    return res if also_bf16 else res[0]


def _all_gather(name, arrs, dep=None):
    n = len(arrs)
    deps = [] if dep is None else [dep]

    def body(*refs):
        ins, outs = refs[:n], refs[n + len(deps):2 * n + len(deps)]
        send_sems, recv_sems, local_sems = refs[2 * n + len(deps):]
        x, y, c = _dev()
        me, sib = (x, y, c), (x, y, 1 - c)
        chips = [(1 - x, y), (x, 1 - y), (1 - x, 1 - y)]

        def slot(p):
            return 4 * p[0] + 2 * p[1] + p[2]

        def copy(a, k, block, to, src=None):
            dst = outs[a].at[slot(block)]
            return pltpu.make_async_remote_copy(
                src_ref=dst if src is None else src, dst_ref=dst,
                send_sem=send_sems.at[7 * a + k], recv_sem=recv_sems.at[7 * a + k],
                device_id=to, device_id_type=MESH)

        mine = [pltpu.make_async_copy(ins[a], outs[a].at[slot(me)], local_sems.at[a]) for a in range(n)]
        for cp in mine:
            cp.start()
        first = []
        for a in range(n):
            first.append(copy(a, 0, me, sib, src=ins[a]))
            first += [copy(a, 1 + j, me, (*chip, c), src=ins[a]) for j, chip in enumerate(chips)]
        for cp in first:
            cp.start()
        passed = []
        for j, chip in enumerate(chips):
            for a in range(n):
                copy(a, 1 + j, (*chip, c), me).wait_recv()
                cp = copy(a, 4 + j, (*chip, c), sib)
                cp.start()
                passed.append(cp)
        for a in range(n):
            copy(a, 0, sib, me).wait_recv()
            for j, chip in enumerate(chips):
                copy(a, 4 + j, (*chip, 1 - c), me).wait_recv()
        for cp in first + passed:
            cp.wait_send()
        for cp in mine:
            cp.wait()

    any_spec = pl.BlockSpec(memory_space=pl.ANY)
    return pl.pallas_call(
        body, in_specs=[any_spec] * (n + len(deps)), out_specs=[any_spec] * n,
        out_shape=[jax.ShapeDtypeStruct((N_DEV,) + a.shape, a.dtype) for a in arrs],
        scratch_shapes=[pltpu.SemaphoreType.DMA((7 * n,)), pltpu.SemaphoreType.DMA((7 * n,)),
                        pltpu.SemaphoreType.DMA((n,))],
        name=name,
    )(*arrs, *deps)


def _grad_to_sibling(gds):
    n = len(gds)

    def body(*refs):
        g_refs, r_refs = refs[:n], refs[n:2 * n]
        send_sems, recv_sems = refs[2 * n:]
        x, y, c = _dev()
        cps = []
        for a in range(n):
            for k in range(4):
                cp = pltpu.make_async_remote_copy(
                    src_ref=g_refs[a].at[2 * k + (1 - c)], dst_ref=r_refs[a].at[k],
                    send_sem=send_sems.at[4 * a + k], recv_sem=recv_sems.at[4 * a + k],
                    device_id=(x, y, 1 - c), device_id_type=MESH)
                cp.start()
                cps.append(cp)
        for cp in cps:
            cp.wait()

    any_spec = pl.BlockSpec(memory_space=pl.ANY)
    return pl.pallas_call(
        body, in_specs=[any_spec] * n, out_specs=[any_spec] * n,
        out_shape=[jax.ShapeDtypeStruct((4,) + g.shape[1:], g.dtype) for g in gds],
        scratch_shapes=[pltpu.SemaphoreType.DMA((4 * n,)), pltpu.SemaphoreType.DMA((4 * n,))],
        name="grad_to_sibling",
    )(*gds)


def _chip_sum(name, gd, from_sib, c_arr):
    _, k, n = gd.shape
    tr = _pick(k, max(16, (1 << 20) // (4 * n)), 16)

    def body(c_ref, a_ref, b_ref, o_ref):
        o_ref[...] = (a_ref[...] + b_ref[...]).astype(o_ref.dtype)

    return pl.pallas_call(
        body,
        grid_spec=pltpu.PrefetchScalarGridSpec(
            num_scalar_prefetch=1, grid=(4, k // tr),
            in_specs=[pl.BlockSpec((1, tr, n), lambda q, i, cr: (2 * q + cr[0], i, 0)),
                      pl.BlockSpec((1, tr, n), lambda q, i, cr: (q, i, 0))],
            out_specs=pl.BlockSpec((1, tr, n), lambda q, i, cr: (q, i, 0))),
        out_shape=jax.ShapeDtypeStruct((4, k, n), BF16), name=name, compiler_params=_params(2),
    )(c_arr, gd, from_sib)


def _grad_to_chips(sums):
    n = len(sums)

    def body(*refs):
        s_refs, p_refs = refs[:n], refs[n:2 * n]
        send_sems, recv_sems, local_sems = refs[2 * n:]
        x, y, c = _dev()
        my_chip = 2 * x + y
        cps = []
        for a in range(n):
            local = pltpu.make_async_copy(s_refs[a].at[my_chip], p_refs[a].at[my_chip], local_sems.at[a])
            local.start()
            cps.append(local)
            for j, (px, py) in enumerate([(1 - x, y), (x, 1 - y), (1 - x, 1 - y)]):
                cp = pltpu.make_async_remote_copy(
                    src_ref=s_refs[a].at[2 * px + py], dst_ref=p_refs[a].at[my_chip],
                    send_sem=send_sems.at[3 * a + j], recv_sem=recv_sems.at[3 * a + j],
                    device_id=(px, py, c), device_id_type=MESH)
                cp.start()
                cps.append(cp)
        for cp in cps:
            cp.wait()

    any_spec = pl.BlockSpec(memory_space=pl.ANY)
    return pl.pallas_call(
        body, in_specs=[any_spec] * n, out_specs=[any_spec] * n,
        out_shape=[jax.ShapeDtypeStruct(s.shape, s.dtype) for s in sums],
        scratch_shapes=[pltpu.SemaphoreType.DMA((3 * n,)), pltpu.SemaphoreType.DMA((3 * n,)),
                        pltpu.SemaphoreType.DMA((n,))],
        name="grad_to_chips",
    )(*sums)


FLIPS = [(0, 0, 1), (0, 1, 0), (1, 0, 0), (0, 1, 1), (1, 0, 1), (1, 1, 0), (1, 1, 1)]
N_PEERS = len(FLIPS)
_HBM = pl.BlockSpec(memory_space=pltpu.HBM)
_SEM = pl.BlockSpec(memory_space=pltpu.SEMAPHORE)
_EFFECT = pltpu.SideEffectType.DATAFLOW_SIDE_EFFECTING


def _flip(x, y, c, f):
    return (1 - x if f[0] else x, 1 - y if f[1] else y, 1 - c if f[2] else c)


def _slot(p):
    return 4 * p[0] + 2 * p[1] + p[2]


def _exchange_copies(src_refs, land_refs, send_sems, recv_sems, gather):
    x, y, c = _dev()
    mine = _slot((x, y, c))
    cps = []
    for a, (src, land) in enumerate(zip(src_refs, land_refs)):
        for k, f in enumerate(FLIPS):
            peer = _flip(x, y, c, f)
            cps.append(pltpu.make_async_remote_copy(
                src_ref=src if gather else src.at[_slot(peer)], dst_ref=land.at[mine],
                send_sem=send_sems.at[N_PEERS * a + k], recv_sem=recv_sems.at[N_PEERS * a + k],
                device_id=peer, device_id_type=MESH))
    return cps


def _exchange_start(name, srcs, gather, after):
    n = len(srcs)
    lands = [lax.empty(((N_DEV,) + s.shape) if gather else s.shape, s.dtype) for s in srcs]

    def body(*refs):
        src_refs, land_refs = refs[:n], refs[n:2 * n]
        send_sems, recv_sems, local_sems = refs[2 * n + 1:2 * n + 4]
        token = refs[-1]
        if gather:
            x, y, c = _dev()
            for a in range(n):
                pltpu.make_async_copy(src_refs[a], land_refs[a].at[_slot((x, y, c))], local_sems.at[a]).start()
        for cp in _exchange_copies(src_refs, land_refs, send_sems, recv_sems, gather):
            cp.start()
        token[...] = jnp.zeros_like(token)

    hbm = lambda z: pltpu.HBM(z.shape, z.dtype)
    outs = pl.pallas_call(
        body, name=name,
        out_shape=(pltpu.SemaphoreType.DMA((N_PEERS * n,)), pltpu.SemaphoreType.DMA((N_PEERS * n,)),
                   pltpu.SemaphoreType.DMA((n,)), *[hbm(s) for s in srcs], *[hbm(z) for z in lands],
                   jax.ShapeDtypeStruct((8, LANES), F32)),
        in_specs=[_HBM] * (2 * n) + [pl.BlockSpec(memory_space=pl.ANY)],
        out_specs=(_SEM, _SEM, _SEM, *[_HBM] * (2 * n), pl.BlockSpec(memory_space=pltpu.VMEM)),
        input_output_aliases={i: 3 + i for i in range(2 * n)},
        compiler_params=pltpu.CompilerParams(has_side_effects=_EFFECT),
    )(*[pltpu.with_memory_space_constraint(z, pltpu.HBM) for z in list(srcs) + lands], after)
    return (outs[:3], outs[3:3 + n], outs[3 + n:3 + 2 * n], gather), outs[-1]


def _exchange_wait(name, handles, after):
    sems, srcs, lands, gather = handles
    n = len(srcs)

    def body(*refs):
        src_refs, land_refs = refs[:n], refs[n:2 * n]
        send_sems, recv_sems, local_sems = refs[2 * n:2 * n + 3]
        if gather:
            for a in range(n):
                pltpu.make_async_copy(src_refs[a], land_refs[a].at[0], local_sems.at[a]).wait()
        for cp in _exchange_copies(src_refs, land_refs, send_sems, recv_sems, gather):
            cp.wait_send()
            cp.wait_recv()

    hbm = lambda z: pltpu.HBM(z.shape, z.dtype)
    outs = pl.pallas_call(
        body, name=name, out_shape=tuple(hbm(z) for z in list(srcs) + list(lands)),
        in_specs=[_HBM] * (2 * n) + [_SEM] * 3 + [pl.BlockSpec(memory_space=pl.ANY)],
        out_specs=tuple([_HBM] * (2 * n)), input_output_aliases={i: i for i in range(2 * n)},
        compiler_params=pltpu.CompilerParams(has_side_effects=_EFFECT),
    )(*srcs, *lands, *sems, after)
    return list(outs[n:])


def _pack_rows(sizes, width, row_align):
    offs, r = [], 0
    for s in sizes:
        offs.append(r)
        r += -(-s // width)
    total = -(-r // row_align) * row_align
    return offs, total


def _pack(items, width, row_align, lead=()):
    nl = len(lead)
    sizes = [int(jnp.size(a)) // max(1, functools.reduce(lambda p, q: p * q, lead, 1)) for a in items]
    offs, total = _pack_rows(sizes, width, row_align)
    flat = []
    used = 0
    for a, s in zip(items, sizes):
        f = a.reshape(lead + (s,))
        pad = -(-s // width) * width - s
        if pad:
            f = jnp.pad(f, [(0, 0)] * nl + [(0, pad)])
        flat.append(f)
        used += s + pad
    tail = total * width - used
    if tail:
        flat.append(jnp.zeros(lead + (tail,), items[0].dtype))
    return jnp.concatenate(flat, axis=-1).reshape(lead + (total, width)), offs


def _unpack(packed, off, shape, lead=()):
    nl = len(lead)
    size = functools.reduce(lambda p, q: p * q, shape, 1)
    width = packed.shape[-1]
    rows = -(-size // width)
    blk = lax.slice_in_dim(packed, off, off + rows, axis=nl).reshape(lead + (rows * width,))
    return lax.slice_in_dim(blk, 0, size, axis=nl).reshape(lead + tuple(shape))


def _rms(x, g):
    return (x * lax.rsqrt(jnp.mean(x * x, axis=-1, keepdims=True) + RMS_EPS)) * g


def _norm_mod(x, g, sc, sh):
    return _rms(x, g) * (1.0 + sc) + sh


def _mix_fn(glu_a, glu_b, attn, ga, gs):
    return jax.nn.sigmoid(ga) * attn + jax.nn.sigmoid(gs) * (glu_a * jax.nn.sigmoid(glu_b))


def _s5_disc_fn(a_re, a_im, log_dt, b_re, b_im):
    dt = jnp.exp(log_dt)
    mag = jnp.exp(a_re * dt)
    lr, li = mag * jnp.cos(a_im * dt), mag * jnp.sin(a_im * dt)
    den = a_re * a_re + a_im * a_im
    zr = ((lr - 1.0) * a_re + li * a_im) / den
    zi = (li * a_re - (lr - 1.0) * a_im) / den
    return lr, li, zr[None] * b_re - zi[None] * b_im, zr[None] * b_im + zi[None] * b_re


def _adamw_fn(w, g, m, v):
    m = ADAM_B1 * m + (1.0 - ADAM_B1) * g
    v = ADAM_B2 * v + (1.0 - ADAM_B2) * jnp.square(g)
    m_hat = m / (1.0 - ADAM_B1 ** ADAM_STEP)
    v_hat = v / (1.0 - ADAM_B2 ** ADAM_STEP)
    delta = -ADAM_LR * (m_hat / (jnp.sqrt(v_hat) + ADAM_EPS) + ADAM_WD * w)
    return delta, m, v


def _adamw(name, parts, w, m, v):
    p, r, c = parts.shape
    tr = _pick(r, max(8, (1 << 21) // (4 * c * max(p, 2))), 8)

    def fn(pv, wv, mv, vv):
        g = pv[0]
        for i in range(1, p):
            g = g + pv[i]
        d, m2, v2 = _adamw_fn(wv, g, mv, vv)
        return g, d, m2, v2

    spec = pl.BlockSpec((tr, c), lambda i: (i, 0))
    return _tile_call(
        name, fn, (r // tr,), [parts, w, m, v],
        [pl.BlockSpec((p, tr, c), lambda i: (0, i, 0)), spec, spec, spec],
        [jax.ShapeDtypeStruct((r, c), F32)] * 4, [spec] * 4)


def _adamw_sharded(name, parts, own_src, w, m, v, place):
    _, k, n = parts.shape
    tr = _pick(k, max(16, (1 << 19) // (4 * n)), 16)

    def body(pl_ref, p_ref, a_ref, w_ref, m_ref, v_ref, g_ref, d_ref, m2_ref, v2_ref):
        own = a_ref[0]
        g = None
        for q in range(N_DEV):
            term = jnp.where(pl_ref[0] == q, own, p_ref[q].astype(F32))
            g = term if g is None else g + term
        d, m2, v2 = _adamw_fn(w_ref[...], g, m_ref[...], v_ref[...])
        g_ref[...] = g
        d_ref[...] = d
        m2_ref[...] = m2
        v2_ref[...] = v2

    spec = pl.BlockSpec((tr, n), lambda i, pr: (i, 0))
    return pl.pallas_call(
        body,
        grid_spec=pltpu.PrefetchScalarGridSpec(
            num_scalar_prefetch=1, grid=(k // tr,),
            in_specs=[pl.BlockSpec((N_DEV, tr, n), lambda i, pr: (0, i, 0)),
                      pl.BlockSpec((1, tr, n), lambda i, pr: (pr[1], i, 0)),
                      spec, spec, spec],
            out_specs=[spec] * 4),
        out_shape=[jax.ShapeDtypeStruct((k, n), F32)] * 4, name=name, compiler_params=_params(1),
    )(place, parts, own_src, w, m, v)


def _attn_mask(n, rows):
    qi = lax.broadcasted_iota(jnp.int32, (rows, 2 * ATT_BLOCK), 0) & (ATT_BLOCK - 1)
    kj = lax.broadcasted_iota(jnp.int32, (rows, 2 * ATT_BLOCK), 1)
    rel = qi + ATT_BLOCK - kj
    return (rel >= 0) & (rel < ATT_BLOCK) & ((kj >= ATT_BLOCK) | (n > 0))


def _attn_probs(q, k, sink, mask):
    s = lax.dot_general(q, k, (((1,), (1,)), ((), ())), preferred_element_type=F32) * (HEAD_DIM ** -0.5)
    s = jnp.where(mask, s, NEG_INF)
    m = jnp.maximum(jnp.max(s, axis=-1, keepdims=True), sink)
    p = jnp.exp(s - m)
    e_sink = jnp.exp(sink - m)
    inv = 1.0 / (jnp.sum(p, axis=-1, keepdims=True) + e_sink)
    return p * inv, e_sink * inv


def _attn_specs(qpk):
    blk = ATT_BLOCK
    q_spec = pl.BlockSpec((qpk, blk, HEAD_DIM), lambda h, n: (h, n, 0))
    cur = pl.BlockSpec((1, blk, HEAD_DIM), lambda h, n: (h, n, 0))
    prev = pl.BlockSpec((1, blk, HEAD_DIM), lambda h, n: (h, jnp.maximum(n - 1, 0), 0))
    sink_spec = pl.BlockSpec((1, qpk * blk, 1), lambda h, n: (h, 0, 0))
    return q_spec, cur, prev, sink_spec


def _attn_fwd(q, k, v, sinks):
    hq, l, _ = q.shape
    qpk = hq // N_KV_HEADS
    nb = l // ATT_BLOCK
    rows = qpk * ATT_BLOCK
    q_spec, cur, prev, sink_spec = _attn_specs(qpk)

    def body(q_ref, kp_ref, kc_ref, vp_ref, vc_ref, sink_ref, o_ref):
        mask = _attn_mask(pl.program_id(1), rows)
        kk = jnp.concatenate([kp_ref[0], kc_ref[0]], axis=0).astype(BF16)
        vv = jnp.concatenate([vp_ref[0], vc_ref[0]], axis=0).astype(BF16)
        p, _ = _attn_probs(q_ref[...].reshape(rows, HEAD_DIM).astype(BF16), kk, sink_ref[0], mask)
        o = jnp.dot(p.astype(BF16), vv, preferred_element_type=F32)
        o_ref[...] = o.reshape(qpk, ATT_BLOCK, HEAD_DIM).astype(o_ref.dtype)

    return pl.pallas_call(
        body, grid=(N_KV_HEADS, nb), in_specs=[q_spec, prev, cur, prev, cur, sink_spec],
        out_specs=q_spec, out_shape=jax.ShapeDtypeStruct((hq, l, HEAD_DIM), BF16),
        name="attn_fwd", compiler_params=_params(2),
    )(q, k, k, v, v, sinks)


def _attn_bwd(q, k, v, sinks, do):
    hq, l, _ = q.shape
    qpk = hq // N_KV_HEADS
    nb = l // ATT_BLOCK
    blk = ATT_BLOCK
    rows = qpk * blk
    q_spec, cur, prev, sink_spec = _attn_specs(qpk)
    part_spec = pl.BlockSpec((1, 1, 2 * blk, HEAD_DIM), lambda h, n: (h, n, 0, 0))
    dsink_spec = pl.BlockSpec((qpk, 1, LANES), lambda h, n: (h, 0, 0))
    tn = (((0,), (0,)), ((), ()))

    def body(q_ref, do_ref, kp_ref, kc_ref, vp_ref, vc_ref, sink_ref, dq_ref, dkp_ref, dvp_ref, dsink_ref):
        n = pl.program_id(1)
        mask = _attn_mask(n, rows)
        kk = jnp.concatenate([kp_ref[0], kc_ref[0]], axis=0).astype(BF16)
        vv = jnp.concatenate([vp_ref[0], vc_ref[0]], axis=0).astype(BF16)
        qb = q_ref[...].reshape(rows, HEAD_DIM).astype(BF16)
        do32 = do_ref[...].astype(F32).reshape(rows, HEAD_DIM)
        dob = do32.astype(BF16)
        p, p_sink = _attn_probs(qb, kk, sink_ref[0], mask)
        pb = p.astype(BF16)
        o = jnp.dot(pb, vv, preferred_element_type=F32)
        delta = jnp.sum(do32 * o, axis=-1, keepdims=True)
        dp = lax.dot_general(dob, vv, (((1,), (1,)), ((), ())), preferred_element_type=F32)
        ds = (p * (dp - delta) * (HEAD_DIM ** -0.5)).astype(BF16)
        dq = jnp.dot(ds, kk, preferred_element_type=F32)
        dq_ref[...] = dq.reshape(qpk, blk, HEAD_DIM).astype(dq_ref.dtype)
        dkp_ref[0, 0] = lax.dot_general(ds, qb, tn, preferred_element_type=F32)
        dvp_ref[0, 0] = lax.dot_general(pb, dob, tn, preferred_element_type=F32)
        dsr = p_sink * delta
        for g in range(qpk):
            dsg = jnp.broadcast_to(-_colsum(dsr[g * blk:(g + 1) * blk]), (1, LANES))

            @pl.when(n == 0)
            def _():
                dsink_ref[g] = dsg

            @pl.when(n > 0)
            def _():
                dsink_ref[g] += dsg


    part_shape = jax.ShapeDtypeStruct((N_KV_HEADS, nb, 2 * blk, HEAD_DIM), F32)
    dq, dkp, dvp, dsink = pl.pallas_call(
        body, grid=(N_KV_HEADS, nb), in_specs=[q_spec, q_spec, prev, cur, prev, cur, sink_spec],
        out_specs=[q_spec, part_spec, part_spec, dsink_spec],
        out_shape=[jax.ShapeDtypeStruct((hq, l, HEAD_DIM), BF16), part_shape, part_shape,
                   jax.ShapeDtypeStruct((hq, 1, LANES), F32)],
        name="attn_bwd", compiler_params=_params(2),
    )(q, do, k, k, v, v, sinks)

    def combine(a_cur, a_nxt, b_cur, b_nxt):
        last = pl.program_id(1) == nb - 1
        keep = jnp.where(last, 0.0, 1.0)
        return (a_cur[0, 0, blk:] + keep * a_nxt[0, 0, :blk])[None], (b_cur[0, 0, blk:] + keep * b_nxt[0, 0, :blk])[None]

    nxt_spec = pl.BlockSpec((1, 1, 2 * blk, HEAD_DIM), lambda h, n: (h, jnp.minimum(n + 1, nb - 1), 0, 0))
    kv_shape = jax.ShapeDtypeStruct((N_KV_HEADS, l, HEAD_DIM), BF16)
    dk, dv = _tile_call("attn_dkv", combine, (N_KV_HEADS, nb), [dkp, dkp, dvp, dvp],
                        [part_spec, nxt_spec, part_spec, nxt_spec], [kv_shape, kv_shape], [cur, cur])
    return dq, dk, dv, dsink


def _block_diag(m):
    j, gl, a, b = m.shape
    eye = jnp.eye(gl, dtype=m.dtype)
    return (m[:, :, :, None, :] * eye[None, :, None, :, None]).reshape(j, gl * a, gl * b)


def _diag_blocks(z, a):
    j = z.shape[0]
    gl = z.shape[1] // a
    b = z.shape[2] // gl
    d = jnp.diagonal(z.reshape(j, gl, a, gl, b), axis1=1, axis2=3)
    return d.transpose(0, 3, 1, 2)


def _s5_permute(src_ref, dst_ref, t_len):
    seg = t_len // 8
    for k in range(seg):
        dst_ref[8 * k:8 * k + 8, :] = src_ref[pl.ds(k, 8, stride=seg), :]


def _s5_unpermute(perm_ref, t_len, emit):
    per_seg = t_len // 64
    for m in range(t_len // 8):
        emit(8 * m, perm_ref[pl.ds(64 * (m % per_seg) + m // per_seg, 8, stride=8), :])


def _s5_powers(p_ref, lr, li, seg):
    hs = TILE_STATES

    def step(k, carry):
        pr, pi = carry
        p_ref[pl.ds(k, 1), 0:hs] = pr
        p_ref[pl.ds(k, 1), hs:2 * hs] = pi
        return lr * pr - li * pi, lr * pi + li * pr

    lax.fori_loop(0, seg, step, (lr, li))


def _s5_local_scan(x_ref, base, lr, li, seg, reverse):
    hs = TILE_STATES
    lr8, li8 = jnp.broadcast_to(lr, (8, hs)), jnp.broadcast_to(li, (8, hs))
    if reverse:
        li8 = -li8

    def step(i, carry):
        hr, hi = carry
        k = seg - 1 - i if reverse else i
        rows = pl.ds(pl.multiple_of(base + 8 * k, 8), 8)
        nr = lr8 * hr - li8 * hi + x_ref[rows, 0:hs]
        ni = lr8 * hi + li8 * hr + x_ref[rows, hs:2 * hs]
        x_ref[rows, 0:hs] = nr
        x_ref[rows, hs:2 * hs] = ni
        return nr, ni

    zero = jnp.zeros((8, hs), F32)
    return lax.fori_loop(0, seg, step, (zero, zero), unroll=2)


def _s5_carries(c_ref, e_ref, ends, start, pw_r, pw_i, reverse):
    hs = TILE_STATES
    e_ref[:, 0:hs] = ends[0]
    e_ref[:, hs:2 * hs] = ends[1]
    cr, ci = start
    if reverse:
        pw_i = -pw_i
    for s in (range(7, -1, -1) if reverse else range(8)):
        c_ref[s:s + 1, 0:hs] = cr
        c_ref[s:s + 1, hs:2 * hs] = ci
        er, ei = e_ref[s:s + 1, 0:hs], e_ref[s:s + 1, hs:2 * hs]
        cr, ci = er + pw_r * cr - pw_i * ci, ei + pw_r * ci + pw_i * cr
    return cr, ci


def _s5_states(u_perm_b16, bd_ref, x_ref, base, c_ref, e_ref, p_ref, lr, li, h_in, t_len):
    hs = TILE_STATES
    seg = t_len // 8
    x_ref[pl.ds(base, t_len), :] = jnp.dot(u_perm_b16, bd_ref[0], preferred_element_type=F32)
    ends = _s5_local_scan(x_ref, base, lr, li, seg, False)
    pw_r, pw_i = p_ref[seg - 1:seg, 0:hs], p_ref[seg - 1:seg, hs:2 * hs]
    h_out = _s5_carries(c_ref, e_ref, ends, h_in, pw_r, pw_i, False)
    cr, ci = c_ref[:, 0:hs], c_ref[:, hs:2 * hs]

    def fix(k, carry):
        rows = pl.ds(pl.multiple_of(base + 8 * k, 8), 8)
        pr, pi = p_ref[pl.ds(k, 1), 0:hs], p_ref[pl.ds(k, 1), hs:2 * hs]
        x_ref[rows, 0:hs] += pr * cr - pi * ci
        x_ref[rows, hs:2 * hs] += pr * ci + pi * cr
        return carry

    lax.fori_loop(0, seg, fix, 0, unroll=2)
    return h_out


def _s5_fwd(proj, u_off, bd, cbd, lam, dvec, t_len):
    l = proj.shape[0]
    nj = bd.shape[0]
    nch = l // t_len
    hs = TILE_STATES
    ub = u_off // LANES
    seg = t_len // 8
    assert t_len % 64 == 0

    def body(u_ref, bd_ref, cbd_ref, lam_ref, d_ref, y_ref, hst_ref, x_ref, h_ref, p_ref, c_ref, e_ref, up_ref, yp_ref):
        lr, li = lam_ref[0, 0:1, :], lam_ref[0, 1:2, :]

        @pl.when(pl.program_id(1) == 0)
        def _():
            h_ref[...] = jnp.zeros_like(h_ref)
            _s5_powers(p_ref, lr, li, seg)

        hst_ref[0, 0] = h_ref[...]
        _s5_permute(u_ref, up_ref, t_len)
        h_out = _s5_states(up_ref[...].astype(BF16), bd_ref, x_ref, 0, c_ref, e_ref, p_ref, lr, li,
                           (h_ref[:, 0:hs], h_ref[:, hs:2 * hs]), t_len)
        h_ref[:, 0:hs] = h_out[0]
        h_ref[:, hs:2 * hs] = h_out[1]
        yp_ref[...] = jnp.dot(x_ref[...].astype(BF16), cbd_ref[0], preferred_element_type=F32)
        dv = d_ref[0]

        def out(r0, rows):
            y_ref[r0:r0 + 8, :] = rows + dv * u_ref[r0:r0 + 8, :]

        _s5_unpermute(yp_ref, t_len, out)

    return pl.pallas_call(
        body, grid=(nj, nch),
        in_specs=[pl.BlockSpec((t_len, LANES), lambda j, c: (c, ub + j)),
                  pl.BlockSpec((1, LANES, 2 * hs), lambda j, c: (j, 0, 0)),
                  pl.BlockSpec((1, 2 * hs, LANES), lambda j, c: (j, 0, 0)),
                  pl.BlockSpec((1, 2, hs), lambda j, c: (j, 0, 0)),
                  pl.BlockSpec((1, 1, LANES), lambda j, c: (j, 0, 0))],
        out_specs=[pl.BlockSpec((t_len, LANES), lambda j, c: (c, j)),
                   pl.BlockSpec((1, 1, 1, 2 * hs), lambda j, c: (j, c, 0, 0))],
        out_shape=[jax.ShapeDtypeStruct((l, nj * LANES), F32),
                   jax.ShapeDtypeStruct((nj, nch, 1, 2 * hs), F32)],
        scratch_shapes=[pltpu.VMEM((t_len, 2 * hs), F32), pltpu.VMEM((1, 2 * hs), F32),
                        pltpu.VMEM((seg, 2 * hs), F32), pltpu.VMEM((8, 2 * hs), F32), pltpu.VMEM((8, 2 * hs), F32),
                        pltpu.VMEM((t_len, LANES), F32), pltpu.VMEM((t_len, LANES), F32)],
        name="s5_fwd", compiler_params=_params(2),
    )(proj, bd, cbd, lam, dvec)


def _s5_bwd(proj, u_off, dy, hst, bd, bdt, cbdt, lam, dvec, t_len):
    l = proj.shape[0]
    nj = bd.shape[0]
    nch = l // t_len
    hs = TILE_STATES
    ub = u_off // LANES
    seg = t_len // 8
    tn = (((0,), (0,)), ((), ()))
    assert t_len % 64 == 0

    def body(u_ref, dy_ref, hst_ref, bd_ref, bdt_ref, cbdt_ref, lam_ref, d_ref,
             du_ref, dbd_ref, dcbdt_ref, dlam_ref, dd_ref,
             x_ref, g_ref, gc_ref, p_ref, c_ref, e_ref, up_ref, dyp_ref, dup_ref):
        first = pl.program_id(1) == 0
        lr, li = lam_ref[0, 0:1, :], lam_ref[0, 1:2, :]

        @pl.when(first)
        def _():
            gc_ref[...] = jnp.zeros_like(gc_ref)
            _s5_powers(p_ref, lr, li, seg)

        _s5_permute(u_ref, up_ref, t_len)
        _s5_permute(dy_ref, dyp_ref, t_len)
        ub16, dyb16 = up_ref[...].astype(BF16), dyp_ref[...].astype(BF16)
        h0 = hst_ref[0, 0]
        _s5_states(ub16, bd_ref, x_ref, 8, c_ref, e_ref, p_ref, lr, li, (h0[:, 0:hs], h0[:, hs:2 * hs]), t_len)
        x_ref[0:8, :] = c_ref[...]
        g_ref[...] = jnp.dot(dyb16, cbdt_ref[0], preferred_element_type=F32)
        starts = _s5_local_scan(g_ref, 0, lr, li, seg, True)
        pw_r, pw_i = p_ref[seg - 1:seg, 0:hs], p_ref[seg - 1:seg, hs:2 * hs]
        g_out = _s5_carries(c_ref, e_ref, starts, (gc_ref[:, 0:hs], gc_ref[:, hs:2 * hs]), pw_r, pw_i, True)
        gc_ref[:, 0:hs] = g_out[0]
        gc_ref[:, hs:2 * hs] = g_out[1]
        cr, ci = c_ref[:, 0:hs], c_ref[:, hs:2 * hs]

        def fix(k, carry):
            alr, ali = carry
            rows = pl.ds(pl.multiple_of(8 * k, 8), 8)
            pr, pi = p_ref[pl.ds(seg - 1 - k, 1), 0:hs], p_ref[pl.ds(seg - 1 - k, 1), hs:2 * hs]
            gr = g_ref[rows, 0:hs] + pr * cr + pi * ci
            gi = g_ref[rows, hs:2 * hs] + pr * ci - pi * cr
            g_ref[rows, 0:hs] = gr
            g_ref[rows, hs:2 * hs] = gi
            hpr, hpi = x_ref[rows, 0:hs], x_ref[rows, hs:2 * hs]
            return alr + gr * hpr + gi * hpi, ali + gi * hpr - gr * hpi

        zero = jnp.zeros((8, hs), F32)
        alr, ali = lax.fori_loop(0, seg, fix, (zero, zero), unroll=2)
        alr, ali = _colsum(alr), _colsum(ali)
        g = g_ref[...].astype(BF16)
        h = x_ref[pl.ds(8, t_len), :].astype(BF16)
        dup_ref[...] = jnp.dot(g, bdt_ref[0], preferred_element_type=F32)
        dv = d_ref[0]

        def out(r0, rows):
            du_ref[r0:r0 + 8, :] = (rows + dv * dy_ref[r0:r0 + 8, :]).astype(du_ref.dtype)

        _s5_unpermute(dup_ref, t_len, out)
        sign = jnp.where(lax.broadcasted_iota(jnp.int32, (1, 2 * hs), 1) < hs, 1.0, -1.0)
        dbd = lax.dot_general(ub16, g, tn, preferred_element_type=F32)
        dcbdt = lax.dot_general(dyb16, h, tn, preferred_element_type=F32) * sign
        ddv = _colsum(dy_ref[...] * u_ref[...])

        @pl.when(first)
        def _():
            dbd_ref[0] = dbd
            dcbdt_ref[0] = dcbdt
            dlam_ref[0, 0:1, :] = alr
            dlam_ref[0, 1:2, :] = ali
            dd_ref[0] = ddv

        @pl.when(jnp.logical_not(first))
        def _():
            dbd_ref[0] += dbd
            dcbdt_ref[0] += dcbdt
            dlam_ref[0, 0:1, :] += alr
            dlam_ref[0, 1:2, :] += ali
            dd_ref[0] += ddv

    rev = lambda c: nch - 1 - c
    wide = pl.BlockSpec((1, LANES, 2 * hs), lambda j, c: (j, 0, 0))
    tall = pl.BlockSpec((1, 2 * hs, LANES), lambda j, c: (j, 0, 0))
    return pl.pallas_call(
        body, grid=(nj, nch),
        in_specs=[pl.BlockSpec((t_len, LANES), lambda j, c: (rev(c), ub + j)),
                  pl.BlockSpec((t_len, LANES), lambda j, c: (rev(c), j)),
                  pl.BlockSpec((1, 1, 1, 2 * hs), lambda j, c: (j, rev(c), 0, 0)),
                  wide, tall, wide,
                  pl.BlockSpec((1, 2, hs), lambda j, c: (j, 0, 0)),
                  pl.BlockSpec((1, 1, LANES), lambda j, c: (j, 0, 0))],
        out_specs=[pl.BlockSpec((t_len, LANES), lambda j, c: (rev(c), j)),
                   wide, wide,
                   pl.BlockSpec((1, 2, hs), lambda j, c: (j, 0, 0)),
                   pl.BlockSpec((1, 1, LANES), lambda j, c: (j, 0, 0))],
        out_shape=[jax.ShapeDtypeStruct((l, nj * LANES), BF16),
                   jax.ShapeDtypeStruct((nj, LANES, 2 * hs), F32),
                   jax.ShapeDtypeStruct((nj, LANES, 2 * hs), F32),
                   jax.ShapeDtypeStruct((nj, 2, hs), F32),
                   jax.ShapeDtypeStruct((nj, 1, LANES), F32)],
        scratch_shapes=[pltpu.VMEM((t_len + 8, 2 * hs), F32), pltpu.VMEM((t_len, 2 * hs), F32),
                        pltpu.VMEM((1, 2 * hs), F32), pltpu.VMEM((seg, 2 * hs), F32),
                        pltpu.VMEM((8, 2 * hs), F32), pltpu.VMEM((8, 2 * hs), F32),
                        pltpu.VMEM((t_len, LANES), F32), pltpu.VMEM((t_len, LANES), F32),
                        pltpu.VMEM((t_len, LANES), F32)],
        name="s5_bwd", compiler_params=_params(2),
    )(proj, dy, hst, bd, bdt, cbdt, lam, dvec)


def _full_spec(shape):
    nd = len(shape)
    return pl.BlockSpec(tuple(shape), lambda i: (0,) * nd)


def _sds(shape, dtype=F32):
    return jax.ShapeDtypeStruct(tuple(shape), dtype)


def kernel(x, c, ada_w, ada_b, norm_mix_g, w_in, attn_sinks, w_attn_proj, ssm_a_re, ssm_a_im, ssm_log_dt, ssm_b_re, ssm_b_im, ssm_c_re, ssm_c_im, ssm_d, w_ssm_glu, w_out, norm_ffn_g, w_ffn_up, ffn_conv_w, ffn_conv_b, w_ffn_down, final_g, loss_target, m_ada_w, m_ada_b, m_norm_mix_g, m_w_in, m_attn_sinks, m_w_attn_proj, m_ssm_a_re, m_ssm_a_im, m_ssm_log_dt, m_ssm_b_re, m_ssm_b_im, m_ssm_c_re, m_ssm_c_im, m_ssm_d, m_w_ssm_glu, m_w_out, m_norm_ffn_g, m_w_ffn_up, m_ffn_conv_w, m_ffn_conv_b, m_w_ffn_down, m_final_g, v_ada_w, v_ada_b, v_norm_mix_g, v_w_in, v_attn_sinks, v_w_attn_proj, v_ssm_a_re, v_ssm_a_im, v_ssm_log_dt, v_ssm_b_re, v_ssm_b_im, v_ssm_c_re, v_ssm_c_im, v_ssm_d, v_w_ssm_glu, v_w_out, v_norm_ffn_g, v_w_ffn_up, v_ffn_conv_w, v_ffn_conv_b, v_w_ffn_down, v_final_g):
    given = dict(locals())
    names = ['ada_w', 'ada_b', 'norm_mix_g', 'w_in', 'attn_sinks', 'w_attn_proj', 'ssm_a_re', 'ssm_a_im',
             'ssm_log_dt', 'ssm_b_re', 'ssm_b_im', 'ssm_c_re', 'ssm_c_im', 'ssm_d', 'w_ssm_glu', 'w_out',
             'norm_ffn_g', 'w_ffn_up', 'ffn_conv_w', 'ffn_conv_b', 'w_ffn_down', 'final_g']

    xs = x[0]
    tgt = loss_target[0]
    l, d = xs.shape
    attn_w = w_attn_proj.shape[1]
    ssm_w = w_ssm_glu.shape[1]
    hq = attn_sinks.shape[1]
    qpk = hq // N_KV_HEADS
    kv_w = N_KV_HEADS * HEAD_DIM
    n_groups = ssm_a_re.shape[1]
    dff = ffn_conv_b.shape[1]
    in_w = attn_w + 2 * kv_w + ssm_w + 2 * d
    nj = ssm_w // LANES
    off_k, off_v, off_u = attn_w, attn_w + kv_w, attn_w + 2 * kv_w
    off_ga, off_gs = off_u + ssm_w, off_u + ssm_w + d
    assert hq * HEAD_DIM == attn_w and n_groups * SSM_P == ssm_w and l % ATT_BLOCK == 0

    xi, yi, ci = _dev()
    idx = 4 * xi + 2 * yi + ci

    row_sharded = {'w_out': (d, d), 'w_ffn_down': (dff, d)}
    big = ['w_in', 'w_attn_proj', 'w_ssm_glu', 'w_out', 'w_ffn_up', 'w_ffn_down']
    spack, s_offs = _pack([c, ffn_conv_w[0]], LANES, 8)
    w16 = {k: given[k][0].astype(BF16) for k in big}
    wg_in, sg = _all_gather("gather_first", [w16['w_in'], spack])
    mixer_w = ['w_attn_proj', 'w_ssm_glu', 'w_out']
    h_mixer, tok = _exchange_start("gather_mixer_start", [w16[k] for k in mixer_w], True, wg_in)
    h_up, tok = _exchange_start("gather_ffn_up_start", [w16['w_ffn_up']], True, tok)
    h_down, tok = _exchange_start("gather_ffn_down_start", [w16['w_ffn_down']], True, tok)
    full = {'w_in': wg_in.transpose(1, 0, 2).reshape(d, in_w)}
    c_all = _unpack(sg, s_offs[0], (d,), lead=(N_DEV,))
    conv_w = _unpack(sg, s_offs[1], ffn_conv_w.shape[1:], lead=(N_DEV,)).transpose(1, 0, 2).reshape(3, dff)
    conv_b = ffn_conv_b

    mod_n = ada_w.shape[2]
    tcm = _pick(mod_n, 512)
    ada_b_mine = lax.dynamic_slice_in_dim(ada_b, idx * mod_n, mod_n, axis=1)

    def modpart_fn(cv, wv, bv):
        cond = cv * jax.nn.sigmoid(cv)
        return jnp.dot(cond.astype(BF16), wv.astype(BF16), preferred_element_type=F32) + bv, cond

    modp, cond_all = _tile_call(
        "ada_rows", modpart_fn, (mod_n // tcm,), [c_all, ada_w[0], ada_b_mine],
        [pl.BlockSpec((N_DEV, d), lambda j: (0, 0)), pl.BlockSpec((d, tcm), lambda j: (0, j)),
         pl.BlockSpec((1, tcm), lambda j: (0, j))],
        [_sds((N_DEV, mod_n)), _sds((N_DEV, d))],
        [pl.BlockSpec((N_DEV, tcm), lambda j: (0, j)), pl.BlockSpec((N_DEV, d), lambda j: (0, 0))])
    (modg,) = _all_gather("gather_ada_rows", [modp])
    mod = lax.dynamic_index_in_dim(modg, idx, axis=1, keepdims=False).reshape(1, N_DEV * mod_n)
    sh1, sc1, g1, sh2, sc2, g2 = [mod[:, i * d:(i + 1) * d] for i in range(6)]

    tr = _pick(l, 256, 8)
    trh = _pick(l, 128, 8)
    nr, nrh = l // tr, l // trh
    g_mix, g_ffn, g_fin = norm_mix_g + tok[0:1, 0:1], norm_ffn_g, final_g.reshape(1, d)

    def with_t(fn):
        def wrapped(*vals):
            out = fn(*vals)
            out = out if isinstance(out, tuple) else (out,)
            return out + (out[-1].T,)
        return wrapped

    h1, h1_t = _tile_call("norm_mod_mix", with_t(_norm_mod), (1, nr), [xs, g_mix, sc1, sh1],
                          [_t(tr, d), _v(d), _v(d), _v(d)], [_sds((l, d), BF16), _sds((d, l), BF16)],
                          [_t(tr, d), _tt(tr, d)])
    proj = _matmul("proj_in", h1, full['w_in'], "nn", tn=1280)

    def heads(z, n):
        return z.reshape(l, n, HEAD_DIM).transpose(1, 0, 2)

    qh = heads(proj[:, :attn_w], hq)
    kh = heads(proj[:, off_k:off_k + kv_w], N_KV_HEADS)
    vh = heads(proj[:, off_v:off_v + kv_w], N_KV_HEADS)
    sinks3 = jnp.repeat(attn_sinks.reshape(N_KV_HEADS, qpk), ATT_BLOCK, axis=1)[..., None]
    o_h = _attn_fwd(qh, kh, vh, sinks3)
    o2 = o_h.transpose(1, 0, 2).reshape(l, attn_w)

    gn = (n_groups, SSM_N)
    pgn = (SSM_P, n_groups, SSM_N)
    a_re, a_im, log_dt = ssm_a_re[0], ssm_a_im[0], ssm_log_dt[0].reshape(n_groups, 1)
    b_re, b_im = ssm_b_re[0].transpose(2, 0, 1), ssm_b_im[0].transpose(2, 0, 1)
    disc_ins = [a_re, a_im, log_dt, b_re, b_im]
    disc_specs = [_full_spec(gn), _full_spec(gn), _full_spec((n_groups, 1)), _full_spec(pgn), _full_spec(pgn)]
    lam_r, lam_i, bb_r, bb_i = _tile_call(
        "s5_discretise", _s5_disc_fn, (1,), disc_ins, disc_specs,
        [_sds(gn), _sds(gn), _sds(pgn), _sds(pgn)],
        [_full_spec(gn), _full_spec(gn), _full_spec(pgn), _full_spec(pgn)])

    def tiles_gpn(z):
        return z.reshape(SSM_P, nj, TILE_GROUPS, SSM_N).transpose(1, 2, 0, 3)

    bd = jnp.concatenate([_block_diag(tiles_gpn(bb_r)), _block_diag(tiles_gpn(bb_i))], axis=2).astype(BF16)
    c_r = ssm_c_re[0].reshape(nj, TILE_GROUPS, SSM_P, SSM_N).transpose(0, 1, 3, 2)
    c_i = (-ssm_c_im[0]).reshape(nj, TILE_GROUPS, SSM_P, SSM_N).transpose(0, 1, 3, 2)
    cbd = jnp.concatenate([_block_diag(c_r), _block_diag(c_i)], axis=1).astype(BF16)
    bdt, cbdt = bd.transpose(0, 2, 1), cbd.transpose(0, 2, 1)
    lam = jnp.stack([lam_r.reshape(nj, TILE_STATES), lam_i.reshape(nj, TILE_STATES)], axis=1)
    dvec = ssm_d[0].reshape(nj, 1, LANES)
    t_len = _pick(l, 512, 8)
    y, hst = _s5_fwd(proj, off_u, bd, cbd, lam, dvec, t_len)

    tcs, trg = _pick(ssm_w, 1024), _pick(l, 512, 8)
    gy = _tile_call("gelu", lambda v: jax.nn.gelu(v), (ssm_w // tcs, l // trg), [y], [_t(trg, tcs)],
                    [_sds((l, ssm_w), BF16)], [_t(trg, tcs)])[0]
    full.update(zip(mixer_w, _exchange_wait("gather_mixer_wait", h_mixer, gy)))
    full['w_out'] = full['w_out'].reshape(row_sharded['w_out'])
    full['w_attn_proj'] = full['w_attn_proj'].transpose(1, 0, 2).reshape(attn_w, d)
    full['w_ssm_glu'] = full['w_ssm_glu'].transpose(1, 0, 2).reshape(ssm_w, 2 * d)
    attn = _matmul("attn_proj", o2, full['w_attn_proj'], "nn")
    glu = _matmul("ssm_glu", gy, full['w_ssm_glu'], "nn")

    tcd = 256 if d % 256 == 0 and off_ga % 256 == 0 else LANES
    assert d % tcd == 0 and off_ga % tcd == 0 and off_gs % tcd == 0
    trm = _pick(l, 1024, 8)
    mix_in_specs = [_t(trm, tcd), _t(trm, tcd, d), _t(trm, tcd), _t(trm, tcd, off_ga), _t(trm, tcd, off_gs)]
    mixed = _tile_call("gate_mix", _mix_fn, (d // tcd, l // trm), [glu, glu, attn, proj, proj], mix_in_specs,
                       [_sds((l, d), BF16)], [_t(trm, tcd)])[0]
    mixout = _matmul("mix_out", mixed, full['w_out'], "nn")

    def res_norm_fn(xv, mo, g1v, gv, scv, shv):
        x2v = xv + g1v * mo
        return x2v, _norm_mod(x2v, gv, scv, shv)

    x2, h2, h2_t = _tile_call("residual_norm_mod_ffn", with_t(res_norm_fn), (1, nr), [xs, mixout, g1, g_ffn, sc2, sh2],
                              [_t(tr, d), _t(tr, d), _v(d), _v(d), _v(d), _v(d)],
                              [_sds((l, d)), _sds((l, d), BF16), _sds((d, l), BF16)],
                              [_t(tr, d), _t(tr, d), _tt(tr, d)])
    full['w_ffn_up'], = _exchange_wait("gather_ffn_up_wait", h_up, h2)
    up = _matmul("ffn_up", h2, full['w_ffn_up'], "nn", out_dtype=BF16, tn=1408)

    tcf, trc = _pick(dff, 1408), _pick(l, 512, 8)
    assert dff % tcf == 0
    ncf = dff // tcf

    taps = [conv_w[i:i + 1] for i in range(3)]

    def conv_gate(gp, gp_prev, w0, w1, w2, bv):
        gp = gp.astype(F32)
        prev = jnp.where(pl.program_id(1) == 0, 0.0, 1.0) * gp_prev.astype(F32)
        ext = jnp.concatenate([prev, gp], axis=0)
        m1 = pltpu.roll(ext, 1, 0)[HALO:]
        m2 = pltpu.roll(ext, 2, 0)[HALO:]
        return w0 * m2 + w1 * m1 + w2 * gp + bv, m1, m2

    def convglu_fn(gp, gp_prev, val, w0, w1, w2, bv):
        gate, _, _ = conv_gate(gp, gp_prev, w0, w1, w2, bv)
        return gate * jax.nn.sigmoid(gate) * val.astype(F32)

    act, act_t = _tile_call("conv_swiglu", with_t(convglu_fn), (ncf, l // trc), [up, up, up] + taps + [conv_b],
                            [_t(trc, tcf), _prev_rows(trc, tcf), _t(trc, tcf, dff)] + [_v(tcf)] * 4,
                            [_sds((l, dff), BF16), _sds((dff, l), BF16)], [_t(trc, tcf), _tt(trc, tcf)])
    full['w_ffn_down'] = _exchange_wait("gather_ffn_down_wait", h_down, act)[0].reshape(row_sharded['w_ffn_down'])
    ffn = _matmul("ffn_down", act, full['w_ffn_down'], "nn", tm=512)

    def final_fn(x2v, fv, g2v, gv, tv):
        rows = x2v.shape[0]

        def loss_of(x2a, fa, g2a, ga):
            out = _rms(x2a + g2a * fa, ga)
            err = out - tv
            return 0.5 * _colsum(jnp.mean(err * err, axis=-1, keepdims=True))

        loss, vjp = jax.vjp(loss_of, x2v, fv, _bc(g2v, rows), _bc(gv, rows))
        dx3, dffn, dg2, dgf = vjp(jnp.ones((1, 1), F32))
        return jnp.broadcast_to(loss, (1, LANES)), dx3, dffn, _colsum(dg2), _colsum(dgf)

    loss_p, dx3, dffn, dg2, dg_fin = _tile_call(
        "loss_final_norm", final_fn, (1, nrh), [x2, ffn, g2, g_fin, tgt],
        [_t(trh, d), _t(trh, d), _v(d), _v(d), _t(trh, d)],
        [_sds((1, LANES)), _sds((l, d)), _sds((l, d), BF16), _sds((1, d)), _sds((1, d))],
        [_v(LANES), _t(trh, d), _t(trh, d), _v(d), _v(d)], acc=(0, 3, 4))
    loss = lax.psum(loss_p[0, 0], ("x", "y", "c"))

    dact = _matmul("d_act", dffn, full['w_ffn_down'], "nt", out_dtype=BF16, tn=1408, dep=loss.reshape(1, 1))
    gd, gd16, pending = {}, {}, []
    dw_down, dw_down16 = _matmul("dw_ffn_down", act_t, dffn, "nn", tm=512, also_bf16=True)
    gd['w_ffn_down'], gd16['w_ffn_down'] = [z.reshape((N_DEV,) + w_ffn_down.shape[1:]) for z in (dw_down, dw_down16)]
    handle, tok = _exchange_start("grad_ffn_down_start", [gd16['w_ffn_down']], False, loss.reshape(1, 1))
    pending.append((['w_ffn_down'], handle))
    conv_b_bwd = conv_b + tok[0:1, 0:1]

    def convglu_bwd_fn(gp, gp_prev, gp_next, val, val_next, da, da_next, w0, w1, w2, bv):
        rows = gp.shape[0]
        i = pl.program_id(1)
        gp, val, da = gp.astype(F32), val.astype(F32), da.astype(F32)
        prev = jnp.where(i == 0, 0.0, 1.0) * gp_prev.astype(F32)
        more = jnp.where(i == pl.num_programs(1) - 1, 0.0, 1.0)
        ext = jnp.concatenate([prev, gp, gp_next.astype(F32)], axis=0)
        cur = ext[HALO:]
        m1 = pltpu.roll(ext, 1, 0)[HALO:]
        m2 = pltpu.roll(ext, 2, 0)[HALO:]
        gate = w0 * m2 + w1 * m1 + w2 * cur + bv
        sg = jax.nn.sigmoid(gate)
        val_e = jnp.concatenate([val, val_next.astype(F32)], axis=0)
        da_e = jnp.concatenate([da, more * da_next.astype(F32)], axis=0)
        dgate = da_e * val_e * (sg * (1.0 + gate * (1.0 - sg)))
        p1 = pltpu.roll(dgate, rows + HALO - 1, 0)[:rows]
        p2 = pltpu.roll(dgate, rows + HALO - 2, 0)[:rows]
        dg = dgate[:rows]
        dgp = w2 * dg + w1 * p1 + w0 * p2
        dval = da * (gate[:rows] * sg[:rows])
        return (jnp.stack([dgp, dval], axis=0), _colsum(dg), _colsum(dg * m2[:rows]), _colsum(dg * m1[:rows]),
                _colsum(dg * gp))

    dup, dconv_b, dcw0, dcw1, dcw2 = _tile_call(
        "conv_swiglu_bwd", convglu_bwd_fn, (ncf, nr), [up, up, up, up, up, dact, dact] + taps + [conv_b_bwd],
        [_t(tr, tcf), _prev_rows(tr, tcf), _next_rows(tr, tcf, l), _t(tr, tcf, dff), _next_rows(tr, tcf, l, dff),
         _t(tr, tcf), _next_rows(tr, tcf, l)] + [_v(tcf)] * 4,
        [_sds((2, l, dff), BF16)] + [_sds((1, dff))] * 4, [_st(tr, tcf)] + [_v(tcf)] * 4, acc=(1, 2, 3, 4))
    dh2 = _matmul("d_h2", dup, full['w_ffn_up'], "nt", fold=2)
    gd['w_ffn_up'], gd16['w_ffn_up'] = _matmul("dw_ffn_up", h2_t, dup, "nn", tm=512, tn=1408, out_stack=N_DEV, also_bf16=True)
    handle, tok = _exchange_start("grad_ffn_up_start", [gd16['w_ffn_up']], False, gd['w_ffn_up'])
    pending.append((['w_ffn_up'], handle))
    g_ffn_bwd = g_ffn + tok[0:1, 0:1]

    def res_norm_bwd_fn(xv, mo, g1v, gv, scv, shv, dhv, dxv):
        rows = xv.shape[0]
        _, vjp = jax.vjp(res_norm_fn, xv, mo, _bc(g1v, rows), _bc(gv, rows), _bc(scv, rows), _bc(shv, rows))
        dx, dmo, dg1v, dgv, dscv, dshv = vjp((dxv, dhv))
        return dx, dmo, _colsum(dg1v), _colsum(dgv), _colsum(dscv), _colsum(dshv)

    dx2, dmixout, dg1, dg_ffn, dsc2, dsh2 = _tile_call(
        "residual_norm_mod_ffn_bwd", res_norm_bwd_fn, (1, nrh), [xs, mixout, g1, g_ffn_bwd, sc2, sh2, dh2, dx3],
        [_t(trh, d), _t(trh, d), _v(d), _v(d), _v(d), _v(d), _t(trh, d), _t(trh, d)],
        [_sds((l, d)), _sds((l, d), BF16)] + [_sds((1, d))] * 4,
        [_t(trh, d), _t(trh, d)] + [_v(d)] * 4, acc=(2, 3, 4, 5))

    dmixed = _matmul("d_mixed", dmixout, full['w_out'], "nt")
    dw_out, dw_out16 = _matmul("dw_out", mixed, dmixout, "tn", also_bf16=True)
    gd['w_out'], gd16['w_out'] = [z.reshape((N_DEV,) + w_out.shape[1:]) for z in (dw_out, dw_out16)]

    def mix_bwd_fn(ga_, gb_, at, pa, ps, dm):
        _, vjp = jax.vjp(_mix_fn, ga_, gb_, at, pa, ps)
        da, db, dat, dpa, dps = vjp(dm)
        return jnp.stack([da, db], axis=0), dat, dpa, dps

    dglu, dattn, dga, dgs = _tile_call(
        "gate_mix_bwd", mix_bwd_fn, (d // tcd, l // trm), [glu, glu, attn, proj, proj, dmixed],
        mix_in_specs + [_t(trm, tcd)],
        [_sds((2, l, d), BF16)] + [_sds((l, d), BF16)] * 3, [_st(trm, tcd)] + [_t(trm, tcd)] * 3)

    dgy = _matmul("d_gelu_y", dglu, full['w_ssm_glu'], "nt")
    gd['w_ssm_glu'], gd16['w_ssm_glu'] = _matmul("dw_ssm_glu", gy, dglu, "tn", out_stack=N_DEV, also_bf16=True)

    def gelu_bwd_fn(yv, dv):
        _, vjp = jax.vjp(lambda z: jax.nn.gelu(z), yv)
        return vjp(dv)[0]

    dy = _tile_call("gelu_bwd", gelu_bwd_fn, (ssm_w // tcs, l // trg), [y, dgy], [_t(trg, tcs), _t(trg, tcs)],
                    [_sds((l, ssm_w))], [_t(trg, tcs)])[0]
    du, dbd, dcbdt, dlam, dd_tiles = _s5_bwd(proj, off_u, dy, hst, bd, bdt, cbdt, lam, dvec, t_len)

    def gpn_of(z):
        return z.transpose(2, 0, 1, 3).reshape(pgn)

    dbb_r = gpn_of(_diag_blocks(dbd[:, :, :TILE_STATES], SSM_P))
    dbb_i = gpn_of(_diag_blocks(dbd[:, :, TILE_STATES:], SSM_P))
    dc_re = _diag_blocks(dcbdt[:, :, :TILE_STATES], SSM_P).reshape(n_groups, SSM_P, SSM_N)
    dc_im = _diag_blocks(dcbdt[:, :, TILE_STATES:], SSM_P).reshape(n_groups, SSM_P, SSM_N)
    dlam_r, dlam_i = dlam[:, 0].reshape(gn), dlam[:, 1].reshape(gn)

    def disc_bwd_fn(ar, ai, ld, br, bi, dlr, dli, dbr, dbi):
        _, vjp = jax.vjp(_s5_disc_fn, ar, ai, ld, br, bi)
        return vjp((dlr, dli, dbr, dbi))

    da_re, da_im, dlog_dt, db_re, db_im = _tile_call(
        "s5_discretise_bwd", disc_bwd_fn, (1,), disc_ins + [dlam_r, dlam_i, dbb_r, dbb_i],
        disc_specs + [_full_spec(gn), _full_spec(gn), _full_spec(pgn), _full_spec(pgn)],
        [_sds(gn), _sds(gn), _sds((n_groups, 1)), _sds(pgn), _sds(pgn)], disc_specs)

    do2 = _matmul("d_attn_heads", dattn, full['w_attn_proj'], "nt")
    gd['w_attn_proj'], gd16['w_attn_proj'] = _matmul("dw_attn_proj", o2, dattn, "tn", out_stack=N_DEV, also_bf16=True)
    handle, tok = _exchange_start("grad_mixer_start", [gd16[k] for k in mixer_w], False, gd['w_attn_proj'])
    pending.append((mixer_w, handle))
    do_h = heads(do2.astype(BF16), hq)
    dq_h, dk_h, dv_h, dsink = _attn_bwd(qh, kh, vh, sinks3 + tok[0:1, 0:1], do_h)

    def unheads(z):
        return z.transpose(1, 0, 2).reshape(l, z.shape[0] * HEAD_DIM)

    dproj = jnp.concatenate([unheads(dq_h), unheads(dk_h), unheads(dv_h), du, dga, dgs], axis=1)
    dh1 = _matmul("d_h1", dproj, full['w_in'], "nt", tm=512)

    def norm_bwd_fn(xv, gv, scv, shv, dhv, dxv):
        rows = xv.shape[0]
        _, vjp = jax.vjp(_norm_mod, xv, _bc(gv, rows), _bc(scv, rows), _bc(shv, rows))
        dx, dgv, dscv, dshv = vjp(dhv)
        return dx + dxv, _colsum(dgv), _colsum(dscv), _colsum(dshv)

    grad_x, dg_mix, dsc1, dsh1 = _tile_call(
        "norm_mod_mix_bwd", norm_bwd_fn, (1, nrh), [xs, g_mix, sc1, sh1, dh1, dx2],
        [_t(trh, d), _v(d), _v(d), _v(d), _t(trh, d), _t(trh, d)],
        [_sds((l, d))] + [_sds((1, d))] * 3, [_t(trh, d)] + [_v(d)] * 3, acc=(1, 2, 3))

    dmod = jnp.concatenate([dsh1, dsc1, dg1, dsh2, dsc2, dg2], axis=1)
    small = ['ada_b', 'norm_mix_g', 'attn_sinks', 'ssm_a_re', 'ssm_a_im', 'ssm_log_dt', 'ssm_b_re', 'ssm_b_im',
             'ssm_c_re', 'ssm_c_im', 'ssm_d', 'norm_ffn_g', 'ffn_conv_b', 'final_g']
    small_grads = {
        'ada_b': dmod, 'norm_mix_g': dg_mix, 'attn_sinks': dsink[:, 0, 0], 'ssm_a_re': da_re, 'ssm_a_im': da_im,
        'ssm_log_dt': dlog_dt, 'ssm_b_re': db_re.transpose(1, 2, 0), 'ssm_b_im': db_im.transpose(1, 2, 0),
        'ssm_c_re': dc_re, 'ssm_c_im': dc_im, 'ssm_d': dd_tiles, 'norm_ffn_g': dg_ffn, 'ffn_conv_b': dconv_b,
        'final_g': dg_fin}
    gs_pack, sm_offs = _pack([small_grads[k] for k in small], LANES, 8)
    h_small, tok = _exchange_start("gather_small_grads_start", [gs_pack], True, grad_x)

    dw_in, dw_in16 = _matmul("dw_in", h1_t, dproj, "nn", tm=512, tn=1280, also_bf16=True, dep=tok)
    dcw = jnp.concatenate([dcw0, dcw1, dcw2], axis=0)
    shard_in, shard_cw = w_in.shape[1:], ffn_conv_w.shape[1:]
    gd16['w_in'] = dw_in16.reshape(shard_in[0], N_DEV, shard_in[1]).transpose(1, 0, 2)
    own_in = lax.dynamic_slice_in_dim(dw_in, idx * shard_in[1], shard_in[1], axis=1)[None]
    gd['ffn_conv_w'] = dcw.reshape(shard_cw[0], N_DEV, shard_cw[1]).transpose(1, 0, 2)
    gd16['ffn_conv_w'] = gd['ffn_conv_w'].astype(BF16)
    handle, tok = _exchange_start("grad_in_start", [gd16['w_in'], gd16['ffn_conv_w']], False, dw_in)
    pending.append((['w_in', 'ffn_conv_w'], handle))
    gs_all, = _exchange_wait("gather_small_grads_wait", h_small, tok)
    ws_pack, _ = _pack([given[k] for k in small], LANES, 8)
    ms_pack, _ = _pack([given['m_' + k] for k in small], LANES, 8)
    vs_pack, _ = _pack([given['v_' + k] for k in small], LANES, 8)
    small_out = _adamw("adamw_replicated", gs_all, ws_pack, ms_pack, vs_pack)

    dmod_all = _unpack(gs_all, sm_offs[0], (N_DEV * mod_n,), lead=(N_DEV,))
    dmod_mine = lax.dynamic_slice_in_dim(dmod_all, idx * mod_n, mod_n, axis=1)
    kpad = LANES - N_DEV
    cond_t = jnp.pad(cond_all.T, ((0, 0), (0, kpad)))
    dmod_pad = jnp.pad(dmod_mine, ((0, kpad), (0, 0)))
    g_ada_w = _matmul("dw_ada", cond_t, dmod_pad, "nn")
    ada_out = _adamw("adamw_ada_w", g_ada_w[None], ada_w[0], m_ada_w[0], v_ada_w[0])

    sharded = big + ['ffn_conv_w']
    sharded_out = {}

    def finish(group, handle, after):
        for k, parts in zip(group, _exchange_wait("grad_" + group[0] + "_wait", handle, after)):
            own_src, own_at = (own_in, 0 * idx) if k == 'w_in' else (gd[k], idx)
            sharded_out[k] = _adamw_sharded("adamw_" + k, parts, own_src, given[k][0], given['m_' + k][0],
                                            given['v_' + k][0], jnp.stack([idx, own_at]).astype(jnp.int32))

    for group, handle in pending[:-1]:
        finish(group, handle, ada_out[0])
    done = functools.reduce(lambda p, q: p + q, [sharded_out[k][1][0:1, 0:1] for g_, _ in pending[:-1] for k in g_])
    finish(*pending[-1], done)

    results = [{}, {}, {}, {}]
    for which in range(4):
        for k, off in zip(small, sm_offs):
            results[which][k] = _unpack(small_out[which], off, given[k].shape)
        for k in sharded:
            results[which][k] = sharded_out[k][which][None]
        results[which]['ada_w'] = ada_out[which][None]
    outs = [loss, grad_x[None]]
    for which in range(4):
        outs += [results[which][k] for k in names]
    return tuple(outs)
```

```python
import functools
import math

import jax
import jax.numpy as jnp
from jax import lax
from jax.experimental import pallas as pl
from jax.experimental.pallas import tpu as pltpu

F32, BF16 = jnp.float32, jnp.bfloat16
MESH = pl.DeviceIdType.MESH
N_DEV = 8

HEAD_DIM = 64
N_KV_HEADS = 2
ATT_BLOCK = 128
NEG_INF = -1e30
SSM_P = 16
SSM_N = 64
LANES = 128
TILE_GROUPS = LANES // SSM_P
TILE_STATES = TILE_GROUPS * SSM_N
RMS_EPS = 1e-6
ADAM_LR, ADAM_B1, ADAM_B2, ADAM_EPS, ADAM_WD, ADAM_STEP = 0.001, 0.9, 0.999, 1e-08, 0.01, 10
VMEM_LIMIT = 56 * 1024 * 1024
MATMUL_VMEM_BUDGET = 44 * 1024 * 1024


def _params(n_axes):
    return pltpu.CompilerParams(dimension_semantics=("arbitrary",) * n_axes, vmem_limit_bytes=VMEM_LIMIT)


def _pick(dim, pref, align=128):
    if dim <= align:
        return dim
    t = (min(pref, dim) // align) * align
    while t > align and dim % t:
        t -= align
    assert dim % t == 0, (dim, pref, align)
    return t


def _dev():
    return lax.axis_index("x"), lax.axis_index("y"), lax.axis_index("c")


def _tile_call(name, fn, grid, ins, in_specs, out_shapes, out_specs, acc=()):
    n_in, n_out = len(ins), len(out_shapes)
    acc_axis = len(grid) - 1

    def body(*refs):
        vals = fn(*[r[...] for r in refs[:n_in]])
        if not isinstance(vals, (tuple, list)):
            vals = (vals,)
        assert len(vals) == n_out
        for i, (r, v) in enumerate(zip(refs[n_in:], vals)):
            v = v.astype(r.dtype)
            if i in acc:
                first = pl.program_id(acc_axis) == 0

                @pl.when(first)
                def _():
                    r[...] = v

                @pl.when(jnp.logical_not(first))
                def _():
                    r[...] += v
            else:
                r[...] = v

    return pl.pallas_call(
        body, grid=grid, in_specs=in_specs, out_specs=out_specs, out_shape=out_shapes, name=name,
        compiler_params=_params(len(grid)),
    )(*ins)


def _t(tr, tc, off=0):
    return pl.BlockSpec((tr, tc), lambda j, i: (i, j + off // tc))


def _tt(tr, tc):
    return pl.BlockSpec((tc, tr), lambda j, i: (j, i))


def _v(tc, off=0, rows=1):
    return pl.BlockSpec((rows, tc), lambda j, i: (0, j + off // tc))


HALO = 16


def _prev_rows(tr, tc, off=0):
    return pl.BlockSpec((HALO, tc), lambda j, i: (jnp.maximum(i * (tr // HALO) - 1, 0), j + off // tc))


def _next_rows(tr, tc, nrows, off=0):
    return pl.BlockSpec((HALO, tc),
                        lambda j, i: (jnp.minimum((i + 1) * (tr // HALO), nrows // HALO - 1), j + off // tc))


def _st(tr, tc):
    return pl.BlockSpec((2, tr, tc), lambda j, i: (0, i, j))


def _bc(v, rows):
    return jnp.broadcast_to(v, (rows, v.shape[-1]))


def _colsum(v):
    return jnp.sum(v, axis=0, keepdims=True)


def _matmul(name, a, b, mode, out_dtype=F32, tm=1024, tn=1024, tk=None, out_stack=None, also_bf16=False, dep=None,
            fold=1):
    def dims(z):
        return (z.shape[-2], z.shape[-1] * (z.shape[0] if z.ndim == 3 else 1))

    ar, ac = dims(a)
    br, bc = dims(b)
    if mode == "nn":
        m, k, n = ar, ac, bc
        assert br == k
    elif mode == "nt":
        m, k, n = ar, ac, br
        assert bc == k
    else:
        m, k, n = ac, ar, bc
        assert br == k
    m_lim, k_lim, n_lim = [m], [k], [n]
    if a.ndim == 3:
        (m_lim if mode == "tn" else k_lim).append(a.shape[-1])
    if b.ndim == 3:
        (k_lim if mode == "nt" else n_lim).append(b.shape[-1])
    if out_stack:
        n_lim.append(n // out_stack)
    tm = _pick(functools.reduce(math.gcd, m_lim), tm)
    tn = _pick(functools.reduce(math.gcd, n_lim), tn)
    k_unit = functools.reduce(math.gcd, k_lim)
    if tk is None:
        sa, sb, so = a.dtype.itemsize, b.dtype.itemsize, jnp.dtype(out_dtype).itemsize + (2 if also_bf16 else 0)
        fits = [t for t in range(LANES, k_unit + 1, LANES) if k_unit % t == 0 and
                2 * t * (tm * sa + tn * sb) + tm * tn * (2 * so + (4 if t < k else 0)) <= MATMUL_VMEM_BUDGET]
        tk = max(fits) if fits else _pick(k_unit, 512)
    else:
        tk = _pick(k_unit, tk)
    assert (k // tk) % fold == 0
    nk = k // (tk * fold)

    def spec(z, brows, bcols, ridx, cidx):
        if z.ndim == 3:
            per = z.shape[-1] // bcols
            return pl.BlockSpec((None, brows, bcols),
                                lambda i, j, kk: (cidx(i, j, kk) // per, ridx(i, j, kk), cidx(i, j, kk) % per))
        return pl.BlockSpec((brows, bcols), lambda i, j, kk: (ridx(i, j, kk), cidx(i, j, kk)))

    gi = lambda i, j, kk: i
    gj = lambda i, j, kk: j
    a_specs, b_specs = [], []
    for f in range(fold):
        gk = lambda i, j, kk, f=f: fold * kk + f
        if mode == "nn":
            a_specs.append(spec(a, tm, tk, gi, gk))
            b_specs.append(spec(b, tk, tn, gk, gj))
            dn = (((1,), (0,)), ((), ()))
        elif mode == "nt":
            a_specs.append(spec(a, tm, tk, gi, gk))
            b_specs.append(spec(b, tn, tk, gj, gk))
            dn = (((1,), (1,)), ((), ()))
        else:
            a_specs.append(spec(a, tk, tm, gk, gi))
            b_specs.append(spec(b, tk, tn, gk, gj))
            dn = (((0,), (0,)), ((), ()))

    n_out = 2 if also_bf16 else 1

    deps = [] if dep is None else [dep]

    def body(*refs):
        a_refs, b_refs = refs[:fold], refs[fold:2 * fold]
        rest = refs[2 * fold + len(deps):]
        o_refs, acc = rest[:n_out], rest[n_out:]
        part = None
        for a_ref, b_ref in zip(a_refs, b_refs):
            one = lax.dot_general(a_ref[...].astype(BF16), b_ref[...].astype(BF16), dn, preferred_element_type=F32)
            part = one if part is None else part + one

        def emit(val):
            for o_ref in o_refs:
                o_ref[...] = val.astype(o_ref.dtype)

        if nk == 1:
            emit(part)
            return
        acc_ref, = acc
        kk = pl.program_id(2)

        @pl.when(kk == 0)
        def _():
            acc_ref[...] = part

        @pl.when(kk > 0)
        def _():
            acc_ref[...] += part

        @pl.when(kk == nk - 1)
        def _():
            emit(acc_ref[...])

    if out_stack:
        per = (n // out_stack) // tn
        out_spec = pl.BlockSpec((None, tm, tn), lambda i, j, kk: (j // per, i, j % per))
        shape = (out_stack, m, n // out_stack)
    else:
        out_spec = pl.BlockSpec((tm, tn), lambda i, j, kk: (i, j))
        shape = (m, n)
    dtypes = [out_dtype, BF16][:n_out]
    res = pl.pallas_call(
        body, grid=(m // tm, n // tn, nk),
        in_specs=a_specs + b_specs + [pl.BlockSpec(memory_space=pl.ANY)] * len(deps), out_specs=[out_spec] * n_out,
        out_shape=[jax.ShapeDtypeStruct(shape, dt) for dt in dtypes],
        scratch_shapes=[pltpu.VMEM((tm, tn), F32)] if nk > 1 else [], name=name, compiler_params=_params(3),
    )(*[a] * fold, *[b] * fold, *deps)
    return res if also_bf16 else res[0]


def _all_gather(name, arrs, dep=None):
    n = len(arrs)
    deps = [] if dep is None else [dep]

    def body(*refs):
        ins, outs = refs[:n], refs[n + len(deps):2 * n + len(deps)]
        send_sems, recv_sems, local_sems = refs[2 * n + len(deps):]
        x, y, c = _dev()
        me, sib = (x, y, c), (x, y, 1 - c)
        chips = [(1 - x, y), (x, 1 - y), (1 - x, 1 - y)]

        def slot(p):
            return 4 * p[0] + 2 * p[1] + p[2]

        def copy(a, k, block, to, src=None):
            dst = outs[a].at[slot(block)]
            return pltpu.make_async_remote_copy(
                src_ref=dst if src is None else src, dst_ref=dst,
                send_sem=send_sems.at[7 * a + k], recv_sem=recv_sems.at[7 * a + k],
                device_id=to, device_id_type=MESH)

        mine = [pltpu.make_async_copy(ins[a], outs[a].at[slot(me)], local_sems.at[a]) for a in range(n)]
        for cp in mine:
            cp.start()
        first = []
        for a in range(n):
            first.append(copy(a, 0, me, sib, src=ins[a]))
            first += [copy(a, 1 + j, me, (*chip, c), src=ins[a]) for j, chip in enumerate(chips)]
        for cp in first:
            cp.start()
        passed = []
        for j, chip in enumerate(chips):
            for a in range(n):
                copy(a, 1 + j, (*chip, c), me).wait_recv()
                cp = copy(a, 4 + j, (*chip, c), sib)
                cp.start()
                passed.append(cp)
        for a in range(n):
            copy(a, 0, sib, me).wait_recv()
            for j, chip in enumerate(chips):
                copy(a, 4 + j, (*chip, 1 - c), me).wait_recv()
        for cp in first + passed:
            cp.wait_send()
        for cp in mine:
            cp.wait()

    any_spec = pl.BlockSpec(memory_space=pl.ANY)
    return pl.pallas_call(
        body, in_specs=[any_spec] * (n + len(deps)), out_specs=[any_spec] * n,
        out_shape=[jax.ShapeDtypeStruct((N_DEV,) + a.shape, a.dtype) for a in arrs],
        scratch_shapes=[pltpu.SemaphoreType.DMA((7 * n,)), pltpu.SemaphoreType.DMA((7 * n,)),
                        pltpu.SemaphoreType.DMA((n,))],
        name=name,
    )(*arrs, *deps)


def _grad_to_sibling(gds):
    n = len(gds)

    def body(*refs):
        g_refs, r_refs = refs[:n], refs[n:2 * n]
        send_sems, recv_sems = refs[2 * n:]
        x, y, c = _dev()
        cps = []
        for a in range(n):
            for k in range(4):
                cp = pltpu.make_async_remote_copy(
                    src_ref=g_refs[a].at[2 * k + (1 - c)], dst_ref=r_refs[a].at[k],
                    send_sem=send_sems.at[4 * a + k], recv_sem=recv_sems.at[4 * a + k],
                    device_id=(x, y, 1 - c), device_id_type=MESH)
                cp.start()
                cps.append(cp)
        for cp in cps:
            cp.wait()

    any_spec = pl.BlockSpec(memory_space=pl.ANY)
    return pl.pallas_call(
        body, in_specs=[any_spec] * n, out_specs=[any_spec] * n,
        out_shape=[jax.ShapeDtypeStruct((4,) + g.shape[1:], g.dtype) for g in gds],
        scratch_shapes=[pltpu.SemaphoreType.DMA((4 * n,)), pltpu.SemaphoreType.DMA((4 * n,))],
        name="grad_to_sibling",
    )(*gds)


def _chip_sum(name, gd, from_sib, c_arr):
    _, k, n = gd.shape
    tr = _pick(k, max(16, (1 << 20) // (4 * n)), 16)

    def body(c_ref, a_ref, b_ref, o_ref):
        o_ref[...] = (a_ref[...] + b_ref[...]).astype(o_ref.dtype)

    return pl.pallas_call(
        body,
        grid_spec=pltpu.PrefetchScalarGridSpec(
            num_scalar_prefetch=1, grid=(4, k // tr),
            in_specs=[pl.BlockSpec((1, tr, n), lambda q, i, cr: (2 * q + cr[0], i, 0)),
                      pl.BlockSpec((1, tr, n), lambda q, i, cr: (q, i, 0))],
            out_specs=pl.BlockSpec((1, tr, n), lambda q, i, cr: (q, i, 0))),
        out_shape=jax.ShapeDtypeStruct((4, k, n), BF16), name=name, compiler_params=_params(2),
    )(c_arr, gd, from_sib)


def _grad_to_chips(sums):
    n = len(sums)

    def body(*refs):
        s_refs, p_refs = refs[:n], refs[n:2 * n]
        send_sems, recv_sems, local_sems = refs[2 * n:]
        x, y, c = _dev()
        my_chip = 2 * x + y
        cps = []
        for a in range(n):
            local = pltpu.make_async_copy(s_refs[a].at[my_chip], p_refs[a].at[my_chip], local_sems.at[a])
            local.start()
            cps.append(local)
            for j, (px, py) in enumerate([(1 - x, y), (x, 1 - y), (1 - x, 1 - y)]):
                cp = pltpu.make_async_remote_copy(
                    src_ref=s_refs[a].at[2 * px + py], dst_ref=p_refs[a].at[my_chip],
                    send_sem=send_sems.at[3 * a + j], recv_sem=recv_sems.at[3 * a + j],
                    device_id=(px, py, c), device_id_type=MESH)
                cp.start()
                cps.append(cp)
        for cp in cps:
            cp.wait()

    any_spec = pl.BlockSpec(memory_space=pl.ANY)
    return pl.pallas_call(
        body, in_specs=[any_spec] * n, out_specs=[any_spec] * n,
        out_shape=[jax.ShapeDtypeStruct(s.shape, s.dtype) for s in sums],
        scratch_shapes=[pltpu.SemaphoreType.DMA((3 * n,)), pltpu.SemaphoreType.DMA((3 * n,)),
                        pltpu.SemaphoreType.DMA((n,))],
        name="grad_to_chips",
    )(*sums)


FLIPS = [(0, 0, 1), (0, 1, 0), (1, 0, 0), (0, 1, 1), (1, 0, 1), (1, 1, 0), (1, 1, 1)]
N_PEERS = len(FLIPS)
_HBM = pl.BlockSpec(memory_space=pltpu.HBM)
_SEM = pl.BlockSpec(memory_space=pltpu.SEMAPHORE)
_EFFECT = pltpu.SideEffectType.DATAFLOW_SIDE_EFFECTING


def _flip(x, y, c, f):
    return (1 - x if f[0] else x, 1 - y if f[1] else y, 1 - c if f[2] else c)


def _slot(p):
    return 4 * p[0] + 2 * p[1] + p[2]


def _exchange_copies(src_refs, land_refs, send_sems, recv_sems, gather):
    x, y, c = _dev()
    mine = _slot((x, y, c))
    cps = []
    for a, (src, land) in enumerate(zip(src_refs, land_refs)):
        for k, f in enumerate(FLIPS):
            peer = _flip(x, y, c, f)
            cps.append(pltpu.make_async_remote_copy(
                src_ref=src if gather else src.at[_slot(peer)], dst_ref=land.at[mine],
                send_sem=send_sems.at[N_PEERS * a + k], recv_sem=recv_sems.at[N_PEERS * a + k],
                device_id=peer, device_id_type=MESH))
    return cps


def _exchange_start(name, srcs, gather, after):
    n = len(srcs)
    lands = [lax.empty(((N_DEV,) + s.shape) if gather else s.shape, s.dtype) for s in srcs]

    def body(*refs):
        src_refs, land_refs = refs[:n], refs[n:2 * n]
        send_sems, recv_sems, local_sems = refs[2 * n + 1:2 * n + 4]
        token = refs[-1]
        if gather:
            x, y, c = _dev()
            for a in range(n):
                pltpu.make_async_copy(src_refs[a], land_refs[a].at[_slot((x, y, c))], local_sems.at[a]).start()
        for cp in _exchange_copies(src_refs, land_refs, send_sems, recv_sems, gather):
            cp.start()
        token[...] = jnp.zeros_like(token)

    hbm = lambda z: pltpu.HBM(z.shape, z.dtype)
    outs = pl.pallas_call(
        body, name=name,
        out_shape=(pltpu.SemaphoreType.DMA((N_PEERS * n,)), pltpu.SemaphoreType.DMA((N_PEERS * n,)),
                   pltpu.SemaphoreType.DMA((n,)), *[hbm(s) for s in srcs], *[hbm(z) for z in lands],
                   jax.ShapeDtypeStruct((8, LANES), F32)),
        in_specs=[_HBM] * (2 * n) + [pl.BlockSpec(memory_space=pl.ANY)],
        out_specs=(_SEM, _SEM, _SEM, *[_HBM] * (2 * n), pl.BlockSpec(memory_space=pltpu.VMEM)),
        input_output_aliases={i: 3 + i for i in range(2 * n)},
        compiler_params=pltpu.CompilerParams(has_side_effects=_EFFECT),
    )(*[pltpu.with_memory_space_constraint(z, pltpu.HBM) for z in list(srcs) + lands], after)
    return (outs[:3], outs[3:3 + n], outs[3 + n:3 + 2 * n], gather), outs[-1]


def _exchange_wait(name, handles, after):
    sems, srcs, lands, gather = handles
    n = len(srcs)

    def body(*refs):
        src_refs, land_refs = refs[:n], refs[n:2 * n]
        send_sems, recv_sems, local_sems = refs[2 * n:2 * n + 3]
        if gather:
            for a in range(n):
                pltpu.make_async_copy(src_refs[a], land_refs[a].at[0], local_sems.at[a]).wait()
        for cp in _exchange_copies(src_refs, land_refs, send_sems, recv_sems, gather):
            cp.wait_send()
            cp.wait_recv()

    hbm = lambda z: pltpu.HBM(z.shape, z.dtype)
    outs = pl.pallas_call(
        body, name=name, out_shape=tuple(hbm(z) for z in list(srcs) + list(lands)),
        in_specs=[_HBM] * (2 * n) + [_SEM] * 3 + [pl.BlockSpec(memory_space=pl.ANY)],
        out_specs=tuple([_HBM] * (2 * n)), input_output_aliases={i: i for i in range(2 * n)},
        compiler_params=pltpu.CompilerParams(has_side_effects=_EFFECT),
    )(*srcs, *lands, *sems, after)
    return list(outs[n:])


def _pack_rows(sizes, width, row_align):
    offs, r = [], 0
    for s in sizes:
        offs.append(r)
        r += -(-s // width)
    total = -(-r // row_align) * row_align
    return offs, total


def _pack(items, width, row_align, lead=()):
    nl = len(lead)
    sizes = [int(jnp.size(a)) // max(1, functools.reduce(lambda p, q: p * q, lead, 1)) for a in items]
    offs, total = _pack_rows(sizes, width, row_align)
    flat = []
    used = 0
    for a, s in zip(items, sizes):
        f = a.reshape(lead + (s,))
        pad = -(-s // width) * width - s
        if pad:
            f = jnp.pad(f, [(0, 0)] * nl + [(0, pad)])
        flat.append(f)
        used += s + pad
    tail = total * width - used
    if tail:
        flat.append(jnp.zeros(lead + (tail,), items[0].dtype))
    return jnp.concatenate(flat, axis=-1).reshape(lead + (total, width)), offs


def _unpack(packed, off, shape, lead=()):
    nl = len(lead)
    size = functools.reduce(lambda p, q: p * q, shape, 1)
    width = packed.shape[-1]
    rows = -(-size // width)
    blk = lax.slice_in_dim(packed, off, off + rows, axis=nl).reshape(lead + (rows * width,))
    return lax.slice_in_dim(blk, 0, size, axis=nl).reshape(lead + tuple(shape))


def _rms(x, g):
    return (x * lax.rsqrt(jnp.mean(x * x, axis=-1, keepdims=True) + RMS_EPS)) * g


def _norm_mod(x, g, sc, sh):
    return _rms(x, g) * (1.0 + sc) + sh


def _mix_fn(glu_a, glu_b, attn, ga, gs):
    return jax.nn.sigmoid(ga) * attn + jax.nn.sigmoid(gs) * (glu_a * jax.nn.sigmoid(glu_b))


def _s5_disc_fn(a_re, a_im, log_dt, b_re, b_im):
    dt = jnp.exp(log_dt)
    mag = jnp.exp(a_re * dt)
    lr, li = mag * jnp.cos(a_im * dt), mag * jnp.sin(a_im * dt)
    den = a_re * a_re + a_im * a_im
    zr = ((lr - 1.0) * a_re + li * a_im) / den
    zi = (li * a_re - (lr - 1.0) * a_im) / den
    return lr, li, zr[None] * b_re - zi[None] * b_im, zr[None] * b_im + zi[None] * b_re


def _adamw_fn(w, g, m, v):
    m = ADAM_B1 * m + (1.0 - ADAM_B1) * g
    v = ADAM_B2 * v + (1.0 - ADAM_B2) * jnp.square(g)
    m_hat = m / (1.0 - ADAM_B1 ** ADAM_STEP)
    v_hat = v / (1.0 - ADAM_B2 ** ADAM_STEP)
    delta = -ADAM_LR * (m_hat / (jnp.sqrt(v_hat) + ADAM_EPS) + ADAM_WD * w)
    return delta, m, v


def _adamw(name, parts, w, m, v):
    p, r, c = parts.shape
    tr = _pick(r, max(8, (1 << 21) // (4 * c * max(p, 2))), 8)

    def fn(pv, wv, mv, vv):
        g = pv[0]
        for i in range(1, p):
            g = g + pv[i]
        d, m2, v2 = _adamw_fn(wv, g, mv, vv)
        return g, d, m2, v2

    spec = pl.BlockSpec((tr, c), lambda i: (i, 0))
    return _tile_call(
        name, fn, (r // tr,), [parts, w, m, v],
        [pl.BlockSpec((p, tr, c), lambda i: (0, i, 0)), spec, spec, spec],
        [jax.ShapeDtypeStruct((r, c), F32)] * 4, [spec] * 4)


def _adamw_sharded(name, parts, own_src, w, m, v, place):
    _, k, n = parts.shape
    tr = _pick(k, max(16, (1 << 19) // (4 * n)), 16)

    def body(pl_ref, p_ref, a_ref, w_ref, m_ref, v_ref, g_ref, d_ref, m2_ref, v2_ref):
        own = a_ref[0]
        g = None
        for q in range(N_DEV):
            term = jnp.where(pl_ref[0] == q, own, p_ref[q].astype(F32))
            g = term if g is None else g + term
        d, m2, v2 = _adamw_fn(w_ref[...], g, m_ref[...], v_ref[...])
        g_ref[...] = g
        d_ref[...] = d
        m2_ref[...] = m2
        v2_ref[...] = v2

    spec = pl.BlockSpec((tr, n), lambda i, pr: (i, 0))
    return pl.pallas_call(
        body,
        grid_spec=pltpu.PrefetchScalarGridSpec(
            num_scalar_prefetch=1, grid=(k // tr,),
            in_specs=[pl.BlockSpec((N_DEV, tr, n), lambda i, pr: (0, i, 0)),
                      pl.BlockSpec((1, tr, n), lambda i, pr: (pr[1], i, 0)),
                      spec, spec, spec],
            out_specs=[spec] * 4),
        out_shape=[jax.ShapeDtypeStruct((k, n), F32)] * 4, name=name, compiler_params=_params(1),
    )(place, parts, own_src, w, m, v)


def _attn_mask(n, rows):
    qi = lax.broadcasted_iota(jnp.int32, (rows, 2 * ATT_BLOCK), 0) & (ATT_BLOCK - 1)
    kj = lax.broadcasted_iota(jnp.int32, (rows, 2 * ATT_BLOCK), 1)
    rel = qi + ATT_BLOCK - kj
    return (rel >= 0) & (rel < ATT_BLOCK) & ((kj >= ATT_BLOCK) | (n > 0))


def _attn_probs(q, k, sink, mask):
    s = lax.dot_general(q, k, (((1,), (1,)), ((), ())), preferred_element_type=F32) * (HEAD_DIM ** -0.5)
    s = jnp.where(mask, s, NEG_INF)
    m = jnp.maximum(jnp.max(s, axis=-1, keepdims=True), sink)
    p = jnp.exp(s - m)
    e_sink = jnp.exp(sink - m)
    inv = 1.0 / (jnp.sum(p, axis=-1, keepdims=True) + e_sink)
    return p * inv, e_sink * inv


def _attn_specs(qpk):
    blk = ATT_BLOCK
    q_spec = pl.BlockSpec((qpk, blk, HEAD_DIM), lambda h, n: (h, n, 0))
    cur = pl.BlockSpec((1, blk, HEAD_DIM), lambda h, n: (h, n, 0))
    prev = pl.BlockSpec((1, blk, HEAD_DIM), lambda h, n: (h, jnp.maximum(n - 1, 0), 0))
    sink_spec = pl.BlockSpec((1, qpk * blk, 1), lambda h, n: (h, 0, 0))
    return q_spec, cur, prev, sink_spec


def _attn_fwd(q, k, v, sinks):
    hq, l, _ = q.shape
    qpk = hq // N_KV_HEADS
    nb = l // ATT_BLOCK
    rows = qpk * ATT_BLOCK
    q_spec, cur, prev, sink_spec = _attn_specs(qpk)

    def body(q_ref, kp_ref, kc_ref, vp_ref, vc_ref, sink_ref, o_ref):
        mask = _attn_mask(pl.program_id(1), rows)
        kk = jnp.concatenate([kp_ref[0], kc_ref[0]], axis=0).astype(BF16)
        vv = jnp.concatenate([vp_ref[0], vc_ref[0]], axis=0).astype(BF16)
        p, _ = _attn_probs(q_ref[...].reshape(rows, HEAD_DIM).astype(BF16), kk, sink_ref[0], mask)
        o = jnp.dot(p.astype(BF16), vv, preferred_element_type=F32)
        o_ref[...] = o.reshape(qpk, ATT_BLOCK, HEAD_DIM).astype(o_ref.dtype)

    return pl.pallas_call(
        body, grid=(N_KV_HEADS, nb), in_specs=[q_spec, prev, cur, prev, cur, sink_spec],
        out_specs=q_spec, out_shape=jax.ShapeDtypeStruct((hq, l, HEAD_DIM), BF16),
        name="attn_fwd", compiler_params=_params(2),
    )(q, k, k, v, v, sinks)


def _attn_bwd(q, k, v, sinks, do):
    hq, l, _ = q.shape
    qpk = hq // N_KV_HEADS
    nb = l // ATT_BLOCK
    blk = ATT_BLOCK
    rows = qpk * blk
    q_spec, cur, prev, sink_spec = _attn_specs(qpk)
    part_spec = pl.BlockSpec((1, 1, 2 * blk, HEAD_DIM), lambda h, n: (h, n, 0, 0))
    dsink_spec = pl.BlockSpec((qpk, 1, LANES), lambda h, n: (h, 0, 0))
    tn = (((0,), (0,)), ((), ()))

    def body(q_ref, do_ref, kp_ref, kc_ref, vp_ref, vc_ref, sink_ref, dq_ref, dkp_ref, dvp_ref, dsink_ref):
        n = pl.program_id(1)
        mask = _attn_mask(n, rows)
        kk = jnp.concatenate([kp_ref[0], kc_ref[0]], axis=0).astype(BF16)
        vv = jnp.concatenate([vp_ref[0], vc_ref[0]], axis=0).astype(BF16)
        qb = q_ref[...].reshape(rows, HEAD_DIM).astype(BF16)
        do32 = do_ref[...].astype(F32).reshape(rows, HEAD_DIM)
        dob = do32.astype(BF16)
        p, p_sink = _attn_probs(qb, kk, sink_ref[0], mask)
        pb = p.astype(BF16)
        o = jnp.dot(pb, vv, preferred_element_type=F32)
        delta = jnp.sum(do32 * o, axis=-1, keepdims=True)
        dp = lax.dot_general(dob, vv, (((1,), (1,)), ((), ())), preferred_element_type=F32)
        ds = (p * (dp - delta) * (HEAD_DIM ** -0.5)).astype(BF16)
        dq = jnp.dot(ds, kk, preferred_element_type=F32)
        dq_ref[...] = dq.reshape(qpk, blk, HEAD_DIM).astype(dq_ref.dtype)
        dkp_ref[0, 0] = lax.dot_general(ds, qb, tn, preferred_element_type=F32)
        dvp_ref[0, 0] = lax.dot_general(pb, dob, tn, preferred_element_type=F32)
        dsr = p_sink * delta
        for g in range(qpk):
            dsg = jnp.broadcast_to(-_colsum(dsr[g * blk:(g + 1) * blk]), (1, LANES))

            @pl.when(n == 0)
            def _():
                dsink_ref[g] = dsg

            @pl.when(n > 0)
            def _():
                dsink_ref[g] += dsg


    part_shape = jax.ShapeDtypeStruct((N_KV_HEADS, nb, 2 * blk, HEAD_DIM), F32)
    dq, dkp, dvp, dsink = pl.pallas_call(
        body, grid=(N_KV_HEADS, nb), in_specs=[q_spec, q_spec, prev, cur, prev, cur, sink_spec],
        out_specs=[q_spec, part_spec, part_spec, dsink_spec],
        out_shape=[jax.ShapeDtypeStruct((hq, l, HEAD_DIM), BF16), part_shape, part_shape,
                   jax.ShapeDtypeStruct((hq, 1, LANES), F32)],
        name="attn_bwd", compiler_params=_params(2),
    )(q, do, k, k, v, v, sinks)

    def combine(a_cur, a_nxt, b_cur, b_nxt):
        last = pl.program_id(1) == nb - 1
        keep = jnp.where(last, 0.0, 1.0)
        return (a_cur[0, 0, blk:] + keep * a_nxt[0, 0, :blk])[None], (b_cur[0, 0, blk:] + keep * b_nxt[0, 0, :blk])[None]

    nxt_spec = pl.BlockSpec((1, 1, 2 * blk, HEAD_DIM), lambda h, n: (h, jnp.minimum(n + 1, nb - 1), 0, 0))
    kv_shape = jax.ShapeDtypeStruct((N_KV_HEADS, l, HEAD_DIM), BF16)
    dk, dv = _tile_call("attn_dkv", combine, (N_KV_HEADS, nb), [dkp, dkp, dvp, dvp],
                        [part_spec, nxt_spec, part_spec, nxt_spec], [kv_shape, kv_shape], [cur, cur])
    return dq, dk, dv, dsink


def _block_diag(m):
    j, gl, a, b = m.shape
    eye = jnp.eye(gl, dtype=m.dtype)
    return (m[:, :, :, None, :] * eye[None, :, None, :, None]).reshape(j, gl * a, gl * b)


def _diag_blocks(z, a):
    j = z.shape[0]
    gl = z.shape[1] // a
    b = z.shape[2] // gl
    d = jnp.diagonal(z.reshape(j, gl, a, gl, b), axis1=1, axis2=3)
    return d.transpose(0, 3, 1, 2)


def _s5_permute(src_ref, dst_ref, t_len):
    seg = t_len // 8
    for k in range(seg):
        dst_ref[8 * k:8 * k + 8, :] = src_ref[pl.ds(k, 8, stride=seg), :]


def _s5_unpermute(perm_ref, t_len, emit):
    per_seg = t_len // 64
    for m in range(t_len // 8):
        emit(8 * m, perm_ref[pl.ds(64 * (m % per_seg) + m // per_seg, 8, stride=8), :])


def _s5_powers(p_ref, lr, li, seg):
    hs = TILE_STATES

    def step(k, carry):
        pr, pi = carry
        p_ref[pl.ds(k, 1), 0:hs] = pr
        p_ref[pl.ds(k, 1), hs:2 * hs] = pi
        return lr * pr - li * pi, lr * pi + li * pr

    lax.fori_loop(0, seg, step, (lr, li))


def _s5_local_scan(x_ref, base, lr, li, seg, reverse):
    hs = TILE_STATES
    lr8, li8 = jnp.broadcast_to(lr, (8, hs)), jnp.broadcast_to(li, (8, hs))
    if reverse:
        li8 = -li8

    def step(i, carry):
        hr, hi = carry
        k = seg - 1 - i if reverse else i
        rows = pl.ds(pl.multiple_of(base + 8 * k, 8), 8)
        nr = lr8 * hr - li8 * hi + x_ref[rows, 0:hs]
        ni = lr8 * hi + li8 * hr + x_ref[rows, hs:2 * hs]
        x_ref[rows, 0:hs] = nr
        x_ref[rows, hs:2 * hs] = ni
        return nr, ni

    zero = jnp.zeros((8, hs), F32)
    return lax.fori_loop(0, seg, step, (zero, zero), unroll=2)


def _s5_carries(c_ref, e_ref, ends, start, pw_r, pw_i, reverse):
    hs = TILE_STATES
    e_ref[:, 0:hs] = ends[0]
    e_ref[:, hs:2 * hs] = ends[1]
    cr, ci = start
    if reverse:
        pw_i = -pw_i
    for s in (range(7, -1, -1) if reverse else range(8)):
        c_ref[s:s + 1, 0:hs] = cr
        c_ref[s:s + 1, hs:2 * hs] = ci
        er, ei = e_ref[s:s + 1, 0:hs], e_ref[s:s + 1, hs:2 * hs]
        cr, ci = er + pw_r * cr - pw_i * ci, ei + pw_r * ci + pw_i * cr
    return cr, ci


def _s5_states(u_perm_b16, bd_ref, x_ref, base, c_ref, e_ref, p_ref, lr, li, h_in, t_len):
    hs = TILE_STATES
    seg = t_len // 8
    x_ref[pl.ds(base, t_len), :] = jnp.dot(u_perm_b16, bd_ref[0], preferred_element_type=F32)
    ends = _s5_local_scan(x_ref, base, lr, li, seg, False)
    pw_r, pw_i = p_ref[seg - 1:seg, 0:hs], p_ref[seg - 1:seg, hs:2 * hs]
    h_out = _s5_carries(c_ref, e_ref, ends, h_in, pw_r, pw_i, False)
    cr, ci = c_ref[:, 0:hs], c_ref[:, hs:2 * hs]

    def fix(k, carry):
        rows = pl.ds(pl.multiple_of(base + 8 * k, 8), 8)
        pr, pi = p_ref[pl.ds(k, 1), 0:hs], p_ref[pl.ds(k, 1), hs:2 * hs]
        x_ref[rows, 0:hs] += pr * cr - pi * ci
        x_ref[rows, hs:2 * hs] += pr * ci + pi * cr
        return carry

    lax.fori_loop(0, seg, fix, 0, unroll=2)
    return h_out


def _s5_fwd(proj, u_off, bd, cbd, lam, dvec, t_len):
    l = proj.shape[0]
    nj = bd.shape[0]
    nch = l // t_len
    hs = TILE_STATES
    ub = u_off // LANES
    seg = t_len // 8
    assert t_len % 64 == 0

    def body(u_ref, bd_ref, cbd_ref, lam_ref, d_ref, y_ref, hst_ref, x_ref, h_ref, p_ref, c_ref, e_ref, up_ref, yp_ref):
        lr, li = lam_ref[0, 0:1, :], lam_ref[0, 1:2, :]

        @pl.when(pl.program_id(1) == 0)
        def _():
            h_ref[...] = jnp.zeros_like(h_ref)
            _s5_powers(p_ref, lr, li, seg)

        hst_ref[0, 0] = h_ref[...]
        _s5_permute(u_ref, up_ref, t_len)
        h_out = _s5_states(up_ref[...].astype(BF16), bd_ref, x_ref, 0, c_ref, e_ref, p_ref, lr, li,
                           (h_ref[:, 0:hs], h_ref[:, hs:2 * hs]), t_len)
        h_ref[:, 0:hs] = h_out[0]
        h_ref[:, hs:2 * hs] = h_out[1]
        yp_ref[...] = jnp.dot(x_ref[...].astype(BF16), cbd_ref[0], preferred_element_type=F32)
        dv = d_ref[0]

        def out(r0, rows):
            y_ref[r0:r0 + 8, :] = rows + dv * u_ref[r0:r0 + 8, :]

        _s5_unpermute(yp_ref, t_len, out)

    return pl.pallas_call(
        body, grid=(nj, nch),
        in_specs=[pl.BlockSpec((t_len, LANES), lambda j, c: (c, ub + j)),
                  pl.BlockSpec((1, LANES, 2 * hs), lambda j, c: (j, 0, 0)),
                  pl.BlockSpec((1, 2 * hs, LANES), lambda j, c: (j, 0, 0)),
                  pl.BlockSpec((1, 2, hs), lambda j, c: (j, 0, 0)),
                  pl.BlockSpec((1, 1, LANES), lambda j, c: (j, 0, 0))],
        out_specs=[pl.BlockSpec((t_len, LANES), lambda j, c: (c, j)),
                   pl.BlockSpec((1, 1, 1, 2 * hs), lambda j, c: (j, c, 0, 0))],
        out_shape=[jax.ShapeDtypeStruct((l, nj * LANES), F32),
                   jax.ShapeDtypeStruct((nj, nch, 1, 2 * hs), F32)],
        scratch_shapes=[pltpu.VMEM((t_len, 2 * hs), F32), pltpu.VMEM((1, 2 * hs), F32),
                        pltpu.VMEM((seg, 2 * hs), F32), pltpu.VMEM((8, 2 * hs), F32), pltpu.VMEM((8, 2 * hs), F32),
                        pltpu.VMEM((t_len, LANES), F32), pltpu.VMEM((t_len, LANES), F32)],
        name="s5_fwd", compiler_params=_params(2),
    )(proj, bd, cbd, lam, dvec)


def _s5_bwd(proj, u_off, dy, hst, bd, bdt, cbdt, lam, dvec, t_len):
    l = proj.shape[0]
    nj = bd.shape[0]
    nch = l // t_len
    hs = TILE_STATES
    ub = u_off // LANES
    seg = t_len // 8
    tn = (((0,), (0,)), ((), ()))
    assert t_len % 64 == 0

    def body(u_ref, dy_ref, hst_ref, bd_ref, bdt_ref, cbdt_ref, lam_ref, d_ref,
             du_ref, dbd_ref, dcbdt_ref, dlam_ref, dd_ref,
             x_ref, g_ref, gc_ref, p_ref, c_ref, e_ref, up_ref, dyp_ref, dup_ref):
        first = pl.program_id(1) == 0
        lr, li = lam_ref[0, 0:1, :], lam_ref[0, 1:2, :]

        @pl.when(first)
        def _():
            gc_ref[...] = jnp.zeros_like(gc_ref)
            _s5_powers(p_ref, lr, li, seg)

        _s5_permute(u_ref, up_ref, t_len)
        _s5_permute(dy_ref, dyp_ref, t_len)
        ub16, dyb16 = up_ref[...].astype(BF16), dyp_ref[...].astype(BF16)
        h0 = hst_ref[0, 0]
        _s5_states(ub16, bd_ref, x_ref, 8, c_ref, e_ref, p_ref, lr, li, (h0[:, 0:hs], h0[:, hs:2 * hs]), t_len)
        x_ref[0:8, :] = c_ref[...]
        g_ref[...] = jnp.dot(dyb16, cbdt_ref[0], preferred_element_type=F32)
        starts = _s5_local_scan(g_ref, 0, lr, li, seg, True)
        pw_r, pw_i = p_ref[seg - 1:seg, 0:hs], p_ref[seg - 1:seg, hs:2 * hs]
        g_out = _s5_carries(c_ref, e_ref, starts, (gc_ref[:, 0:hs], gc_ref[:, hs:2 * hs]), pw_r, pw_i, True)
        gc_ref[:, 0:hs] = g_out[0]
        gc_ref[:, hs:2 * hs] = g_out[1]
        cr, ci = c_ref[:, 0:hs], c_ref[:, hs:2 * hs]

        def fix(k, carry):
            alr, ali = carry
            rows = pl.ds(pl.multiple_of(8 * k, 8), 8)
            pr, pi = p_ref[pl.ds(seg - 1 - k, 1), 0:hs], p_ref[pl.ds(seg - 1 - k, 1), hs:2 * hs]
            gr = g_ref[rows, 0:hs] + pr * cr + pi * ci
            gi = g_ref[rows, hs:2 * hs] + pr * ci - pi * cr
            g_ref[rows, 0:hs] = gr
            g_ref[rows, hs:2 * hs] = gi
            hpr, hpi = x_ref[rows, 0:hs], x_ref[rows, hs:2 * hs]
            return alr + gr * hpr + gi * hpi, ali + gi * hpr - gr * hpi

        zero = jnp.zeros((8, hs), F32)
        alr, ali = lax.fori_loop(0, seg, fix, (zero, zero), unroll=2)
        alr, ali = _colsum(alr), _colsum(ali)
        g = g_ref[...].astype(BF16)
        h = x_ref[pl.ds(8, t_len), :].astype(BF16)
        dup_ref[...] = jnp.dot(g, bdt_ref[0], preferred_element_type=F32)
        dv = d_ref[0]

        def out(r0, rows):
            du_ref[r0:r0 + 8, :] = (rows + dv * dy_ref[r0:r0 + 8, :]).astype(du_ref.dtype)

        _s5_unpermute(dup_ref, t_len, out)
        sign = jnp.where(lax.broadcasted_iota(jnp.int32, (1, 2 * hs), 1) < hs, 1.0, -1.0)
        dbd = lax.dot_general(ub16, g, tn, preferred_element_type=F32)
        dcbdt = lax.dot_general(dyb16, h, tn, preferred_element_type=F32) * sign
        ddv = _colsum(dy_ref[...] * u_ref[...])

        @pl.when(first)
        def _():
            dbd_ref[0] = dbd
            dcbdt_ref[0] = dcbdt
            dlam_ref[0, 0:1, :] = alr
            dlam_ref[0, 1:2, :] = ali
            dd_ref[0] = ddv

        @pl.when(jnp.logical_not(first))
        def _():
            dbd_ref[0] += dbd
            dcbdt_ref[0] += dcbdt
            dlam_ref[0, 0:1, :] += alr
            dlam_ref[0, 1:2, :] += ali
            dd_ref[0] += ddv

    rev = lambda c: nch - 1 - c
    wide = pl.BlockSpec((1, LANES, 2 * hs), lambda j, c: (j, 0, 0))
    tall = pl.BlockSpec((1, 2 * hs, LANES), lambda j, c: (j, 0, 0))
    return pl.pallas_call(
        body, grid=(nj, nch),
        in_specs=[pl.BlockSpec((t_len, LANES), lambda j, c: (rev(c), ub + j)),
                  pl.BlockSpec((t_len, LANES), lambda j, c: (rev(c), j)),
                  pl.BlockSpec((1, 1, 1, 2 * hs), lambda j, c: (j, rev(c), 0, 0)),
                  wide, tall, wide,
                  pl.BlockSpec((1, 2, hs), lambda j, c: (j, 0, 0)),
                  pl.BlockSpec((1, 1, LANES), lambda j, c: (j, 0, 0))],
        out_specs=[pl.BlockSpec((t_len, LANES), lambda j, c: (rev(c), j)),
                   wide, wide,
                   pl.BlockSpec((1, 2, hs), lambda j, c: (j, 0, 0)),
                   pl.BlockSpec((1, 1, LANES), lambda j, c: (j, 0, 0))],
        out_shape=[jax.ShapeDtypeStruct((l, nj * LANES), BF16),
                   jax.ShapeDtypeStruct((nj, LANES, 2 * hs), F32),
                   jax.ShapeDtypeStruct((nj, LANES, 2 * hs), F32),
                   jax.ShapeDtypeStruct((nj, 2, hs), F32),
                   jax.ShapeDtypeStruct((nj, 1, LANES), F32)],
        scratch_shapes=[pltpu.VMEM((t_len + 8, 2 * hs), F32), pltpu.VMEM((t_len, 2 * hs), F32),
                        pltpu.VMEM((1, 2 * hs), F32), pltpu.VMEM((seg, 2 * hs), F32),
                        pltpu.VMEM((8, 2 * hs), F32), pltpu.VMEM((8, 2 * hs), F32),
                        pltpu.VMEM((t_len, LANES), F32), pltpu.VMEM((t_len, LANES), F32),
                        pltpu.VMEM((t_len, LANES), F32)],
        name="s5_bwd", compiler_params=_params(2),
    )(proj, dy, hst, bd, bdt, cbdt, lam, dvec)


def _full_spec(shape):
    nd = len(shape)
    return pl.BlockSpec(tuple(shape), lambda i: (0,) * nd)


def _sds(shape, dtype=F32):
    return jax.ShapeDtypeStruct(tuple(shape), dtype)


def kernel(x, c, ada_w, ada_b, norm_mix_g, w_in, attn_sinks, w_attn_proj, ssm_a_re, ssm_a_im, ssm_log_dt, ssm_b_re, ssm_b_im, ssm_c_re, ssm_c_im, ssm_d, w_ssm_glu, w_out, norm_ffn_g, w_ffn_up, ffn_conv_w, ffn_conv_b, w_ffn_down, final_g, loss_target, m_ada_w, m_ada_b, m_norm_mix_g, m_w_in, m_attn_sinks, m_w_attn_proj, m_ssm_a_re, m_ssm_a_im, m_ssm_log_dt, m_ssm_b_re, m_ssm_b_im, m_ssm_c_re, m_ssm_c_im, m_ssm_d, m_w_ssm_glu, m_w_out, m_norm_ffn_g, m_w_ffn_up, m_ffn_conv_w, m_ffn_conv_b, m_w_ffn_down, m_final_g, v_ada_w, v_ada_b, v_norm_mix_g, v_w_in, v_attn_sinks, v_w_attn_proj, v_ssm_a_re, v_ssm_a_im, v_ssm_log_dt, v_ssm_b_re, v_ssm_b_im, v_ssm_c_re, v_ssm_c_im, v_ssm_d, v_w_ssm_glu, v_w_out, v_norm_ffn_g, v_w_ffn_up, v_ffn_conv_w, v_ffn_conv_b, v_w_ffn_down, v_final_g):
    given = dict(locals())
    names = ['ada_w', 'ada_b', 'norm_mix_g', 'w_in', 'attn_sinks', 'w_attn_proj', 'ssm_a_re', 'ssm_a_im',
             'ssm_log_dt', 'ssm_b_re', 'ssm_b_im', 'ssm_c_re', 'ssm_c_im', 'ssm_d', 'w_ssm_glu', 'w_out',
             'norm_ffn_g', 'w_ffn_up', 'ffn_conv_w', 'ffn_conv_b', 'w_ffn_down', 'final_g']

    xs = x[0]
    tgt = loss_target[0]
    l, d = xs.shape
    attn_w = w_attn_proj.shape[1]
    ssm_w = w_ssm_glu.shape[1]
    hq = attn_sinks.shape[1]
    qpk = hq // N_KV_HEADS
    kv_w = N_KV_HEADS * HEAD_DIM
    n_groups = ssm_a_re.shape[1]
    dff = ffn_conv_b.shape[1]
    in_w = attn_w + 2 * kv_w + ssm_w + 2 * d
    nj = ssm_w // LANES
    off_k, off_v, off_u = attn_w, attn_w + kv_w, attn_w + 2 * kv_w
    off_ga, off_gs = off_u + ssm_w, off_u + ssm_w + d
    assert hq * HEAD_DIM == attn_w and n_groups * SSM_P == ssm_w and l % ATT_BLOCK == 0

    xi, yi, ci = _dev()
    idx = 4 * xi + 2 * yi + ci

    row_sharded = {'w_out': (d, d), 'w_ffn_down': (dff, d)}
    big = ['w_in', 'w_attn_proj', 'w_ssm_glu', 'w_out', 'w_ffn_up', 'w_ffn_down']
    spack, s_offs = _pack([c, ffn_conv_w[0]], LANES, 8)
    w16 = {k: given[k][0].astype(BF16) for k in big}
    wg_in, sg = _all_gather("gather_first", [w16['w_in'], spack])
    mixer_w = ['w_attn_proj', 'w_ssm_glu', 'w_out']
    h_mixer, tok = _exchange_start("gather_mixer_start", [w16[k] for k in mixer_w], True, wg_in)
    h_up, tok = _exchange_start("gather_ffn_up_start", [w16['w_ffn_up']], True, tok)
    h_down, tok = _exchange_start("gather_ffn_down_start", [w16['w_ffn_down']], True, tok)
    full = {'w_in': wg_in.transpose(1, 0, 2).reshape(d, in_w)}
    c_all = _unpack(sg, s_offs[0], (d,), lead=(N_DEV,))
    conv_w = _unpack(sg, s_offs[1], ffn_conv_w.shape[1:], lead=(N_DEV,)).transpose(1, 0, 2).reshape(3, dff)
    conv_b = ffn_conv_b

    mod_n = ada_w.shape[2]
    tcm = _pick(mod_n, 512)
    ada_b_mine = lax.dynamic_slice_in_dim(ada_b, idx * mod_n, mod_n, axis=1)

    def modpart_fn(cv, wv, bv):
        cond = cv * jax.nn.sigmoid(cv)
        return jnp.dot(cond.astype(BF16), wv.astype(BF16), preferred_element_type=F32) + bv, cond

    modp, cond_all = _tile_call(
        "ada_rows", modpart_fn, (mod_n // tcm,), [c_all, ada_w[0], ada_b_mine],
        [pl.BlockSpec((N_DEV, d), lambda j: (0, 0)), pl.BlockSpec((d, tcm), lambda j: (0, j)),
         pl.BlockSpec((1, tcm), lambda j: (0, j))],
        [_sds((N_DEV, mod_n)), _sds((N_DEV, d))],
        [pl.BlockSpec((N_DEV, tcm), lambda j: (0, j)), pl.BlockSpec((N_DEV, d), lambda j: (0, 0))])
    (modg,) = _all_gather("gather_ada_rows", [modp])
    mod = lax.dynamic_index_in_dim(modg, idx, axis=1, keepdims=False).reshape(1, N_DEV * mod_n)
    sh1, sc1, g1, sh2, sc2, g2 = [mod[:, i * d:(i + 1) * d] for i in range(6)]

    tr = _pick(l, 256, 8)
    trh = _pick(l, 128, 8)
    nr, nrh = l // tr, l // trh
    g_mix, g_ffn, g_fin = norm_mix_g + tok[0:1, 0:1], norm_ffn_g, final_g.reshape(1, d)

    def with_t(fn):
        def wrapped(*vals):
            out = fn(*vals)
            out = out if isinstance(out, tuple) else (out,)
            return out + (out[-1].T,)
        return wrapped

    h1, h1_t = _tile_call("norm_mod_mix", with_t(_norm_mod), (1, nr), [xs, g_mix, sc1, sh1],
                          [_t(tr, d), _v(d), _v(d), _v(d)], [_sds((l, d), BF16), _sds((d, l), BF16)],
                          [_t(tr, d), _tt(tr, d)])
    proj = _matmul("proj_in", h1, full['w_in'], "nn", tn=1280)

    def heads(z, n):
        return z.reshape(l, n, HEAD_DIM).transpose(1, 0, 2)

    qh = heads(proj[:, :attn_w], hq)
    kh = heads(proj[:, off_k:off_k + kv_w], N_KV_HEADS)
    vh = heads(proj[:, off_v:off_v + kv_w], N_KV_HEADS)
    sinks3 = jnp.repeat(attn_sinks.reshape(N_KV_HEADS, qpk), ATT_BLOCK, axis=1)[..., None]
    o_h = _attn_fwd(qh, kh, vh, sinks3)
    o2 = o_h.transpose(1, 0, 2).reshape(l, attn_w)

    gn = (n_groups, SSM_N)
    pgn = (SSM_P, n_groups, SSM_N)
    a_re, a_im, log_dt = ssm_a_re[0], ssm_a_im[0], ssm_log_dt[0].reshape(n_groups, 1)
    b_re, b_im = ssm_b_re[0].transpose(2, 0, 1), ssm_b_im[0].transpose(2, 0, 1)
    disc_ins = [a_re, a_im, log_dt, b_re, b_im]
    disc_specs = [_full_spec(gn), _full_spec(gn), _full_spec((n_groups, 1)), _full_spec(pgn), _full_spec(pgn)]
    lam_r, lam_i, bb_r, bb_i = _tile_call(
        "s5_discretise", _s5_disc_fn, (1,), disc_ins, disc_specs,
        [_sds(gn), _sds(gn), _sds(pgn), _sds(pgn)],
        [_full_spec(gn), _full_spec(gn), _full_spec(pgn), _full_spec(pgn)])

    def tiles_gpn(z):
        return z.reshape(SSM_P, nj, TILE_GROUPS, SSM_N).transpose(1, 2, 0, 3)

    bd = jnp.concatenate([_block_diag(tiles_gpn(bb_r)), _block_diag(tiles_gpn(bb_i))], axis=2).astype(BF16)
    c_r = ssm_c_re[0].reshape(nj, TILE_GROUPS, SSM_P, SSM_N).transpose(0, 1, 3, 2)
    c_i = (-ssm_c_im[0]).reshape(nj, TILE_GROUPS, SSM_P, SSM_N).transpose(0, 1, 3, 2)
    cbd = jnp.concatenate([_block_diag(c_r), _block_diag(c_i)], axis=1).astype(BF16)
    bdt, cbdt = bd.transpose(0, 2, 1), cbd.transpose(0, 2, 1)
    lam = jnp.stack([lam_r.reshape(nj, TILE_STATES), lam_i.reshape(nj, TILE_STATES)], axis=1)
    dvec = ssm_d[0].reshape(nj, 1, LANES)
    t_len = _pick(l, 512, 8)
    y, hst = _s5_fwd(proj, off_u, bd, cbd, lam, dvec, t_len)

    tcs, trg = _pick(ssm_w, 1024), _pick(l, 512, 8)
    gy = _tile_call("gelu", lambda v: jax.nn.gelu(v), (ssm_w // tcs, l // trg), [y], [_t(trg, tcs)],
                    [_sds((l, ssm_w), BF16)], [_t(trg, tcs)])[0]
    full.update(zip(mixer_w, _exchange_wait("gather_mixer_wait", h_mixer, gy)))
    full['w_out'] = full['w_out'].reshape(row_sharded['w_out'])
    full['w_attn_proj'] = full['w_attn_proj'].transpose(1, 0, 2).reshape(attn_w, d)
    full['w_ssm_glu'] = full['w_ssm_glu'].transpose(1, 0, 2).reshape(ssm_w, 2 * d)
    attn = _matmul("attn_proj", o2, full['w_attn_proj'], "nn")
    glu = _matmul("ssm_glu", gy, full['w_ssm_glu'], "nn")

    tcd = 256 if d % 256 == 0 and off_ga % 256 == 0 else LANES
    assert d % tcd == 0 and off_ga % tcd == 0 and off_gs % tcd == 0
    trm = _pick(l, 1024, 8)
    mix_in_specs = [_t(trm, tcd), _t(trm, tcd, d), _t(trm, tcd), _t(trm, tcd, off_ga), _t(trm, tcd, off_gs)]
    mixed = _tile_call("gate_mix", _mix_fn, (d // tcd, l // trm), [glu, glu, attn, proj, proj], mix_in_specs,
                       [_sds((l, d), BF16)], [_t(trm, tcd)])[0]
    mixout = _matmul("mix_out", mixed, full['w_out'], "nn")

    def res_norm_fn(xv, mo, g1v, gv, scv, shv):
        x2v = xv + g1v * mo
        return x2v, _norm_mod(x2v, gv, scv, shv)

    x2, h2, h2_t = _tile_call("residual_norm_mod_ffn", with_t(res_norm_fn), (1, nr), [xs, mixout, g1, g_ffn, sc2, sh2],
                              [_t(tr, d), _t(tr, d), _v(d), _v(d), _v(d), _v(d)],
                              [_sds((l, d)), _sds((l, d), BF16), _sds((d, l), BF16)],
                              [_t(tr, d), _t(tr, d), _tt(tr, d)])
    full['w_ffn_up'], = _exchange_wait("gather_ffn_up_wait", h_up, h2)
    up = _matmul("ffn_up", h2, full['w_ffn_up'], "nn", out_dtype=BF16, tn=1408)

    tcf, trc = _pick(dff, 1408), _pick(l, 512, 8)
    assert dff % tcf == 0
    ncf = dff // tcf

    taps = [conv_w[i:i + 1] for i in range(3)]

    def conv_gate(gp, gp_prev, w0, w1, w2, bv):
        gp = gp.astype(F32)
        prev = jnp.where(pl.program_id(1) == 0, 0.0, 1.0) * gp_prev.astype(F32)
        ext = jnp.concatenate([prev, gp], axis=0)
        m1 = pltpu.roll(ext, 1, 0)[HALO:]
        m2 = pltpu.roll(ext, 2, 0)[HALO:]
        return w0 * m2 + w1 * m1 + w2 * gp + bv, m1, m2

    def convglu_fn(gp, gp_prev, val, w0, w1, w2, bv):
        gate, _, _ = conv_gate(gp, gp_prev, w0, w1, w2, bv)
        return gate * jax.nn.sigmoid(gate) * val.astype(F32)

    act, act_t = _tile_call("conv_swiglu", with_t(convglu_fn), (ncf, l // trc), [up, up, up] + taps + [conv_b],
                            [_t(trc, tcf), _prev_rows(trc, tcf), _t(trc, tcf, dff)] + [_v(tcf)] * 4,
                            [_sds((l, dff), BF16), _sds((dff, l), BF16)], [_t(trc, tcf), _tt(trc, tcf)])
    full['w_ffn_down'] = _exchange_wait("gather_ffn_down_wait", h_down, act)[0].reshape(row_sharded['w_ffn_down'])
    ffn = _matmul("ffn_down", act, full['w_ffn_down'], "nn", tm=512)

    def final_fn(x2v, fv, g2v, gv, tv):
        rows = x2v.shape[0]

        def loss_of(x2a, fa, g2a, ga):
            out = _rms(x2a + g2a * fa, ga)
            err = out - tv
            return 0.5 * _colsum(jnp.mean(err * err, axis=-1, keepdims=True))

        loss, vjp = jax.vjp(loss_of, x2v, fv, _bc(g2v, rows), _bc(gv, rows))
        dx3, dffn, dg2, dgf = vjp(jnp.ones((1, 1), F32))
        return jnp.broadcast_to(loss, (1, LANES)), dx3, dffn, _colsum(dg2), _colsum(dgf)

    loss_p, dx3, dffn, dg2, dg_fin = _tile_call(
        "loss_final_norm", final_fn, (1, nrh), [x2, ffn, g2, g_fin, tgt],
        [_t(trh, d), _t(trh, d), _v(d), _v(d), _t(trh, d)],
        [_sds((1, LANES)), _sds((l, d)), _sds((l, d), BF16), _sds((1, d)), _sds((1, d))],
        [_v(LANES), _t(trh, d), _t(trh, d), _v(d), _v(d)], acc=(0, 3, 4))
    loss = lax.psum(loss_p[0, 0], ("x", "y", "c"))

    dact = _matmul("d_act", dffn, full['w_ffn_down'], "nt", out_dtype=BF16, tn=1408, dep=loss.reshape(1, 1))
    gd, gd16, pending = {}, {}, []
    dw_down, dw_down16 = _matmul("dw_ffn_down", act_t, dffn, "nn", tm=512, also_bf16=True)
    gd['w_ffn_down'], gd16['w_ffn_down'] = [z.reshape((N_DEV,) + w_ffn_down.shape[1:]) for z in (dw_down, dw_down16)]
    handle, tok = _exchange_start("grad_ffn_down_start", [gd16['w_ffn_down']], False, loss.reshape(1, 1))
    pending.append((['w_ffn_down'], handle))
    conv_b_bwd = conv_b + tok[0:1, 0:1]

    def convglu_bwd_fn(gp, gp_prev, gp_next, val, val_next, da, da_next, w0, w1, w2, bv):
        rows = gp.shape[0]
        i = pl.program_id(1)
        gp, val, da = gp.astype(F32), val.astype(F32), da.astype(F32)
        prev = jnp.where(i == 0, 0.0, 1.0) * gp_prev.astype(F32)
        more = jnp.where(i == pl.num_programs(1) - 1, 0.0, 1.0)
        ext = jnp.concatenate([prev, gp, gp_next.astype(F32)], axis=0)
        cur = ext[HALO:]
        m1 = pltpu.roll(ext, 1, 0)[HALO:]
        m2 = pltpu.roll(ext, 2, 0)[HALO:]
        gate = w0 * m2 + w1 * m1 + w2 * cur + bv
        sg = jax.nn.sigmoid(gate)
        val_e = jnp.concatenate([val, val_next.astype(F32)], axis=0)
        da_e = jnp.concatenate([da, more * da_next.astype(F32)], axis=0)
        dgate = da_e * val_e * (sg * (1.0 + gate * (1.0 - sg)))
        p1 = pltpu.roll(dgate, rows + HALO - 1, 0)[:rows]
        p2 = pltpu.roll(dgate, rows + HALO - 2, 0)[:rows]
        dg = dgate[:rows]
        dgp = w2 * dg + w1 * p1 + w0 * p2
        dval = da * (gate[:rows] * sg[:rows])
        return (jnp.stack([dgp, dval], axis=0), _colsum(dg), _colsum(dg * m2[:rows]), _colsum(dg * m1[:rows]),
                _colsum(dg * gp))

    dup, dconv_b, dcw0, dcw1, dcw2 = _tile_call(
        "conv_swiglu_bwd", convglu_bwd_fn, (ncf, nr), [up, up, up, up, up, dact, dact] + taps + [conv_b_bwd],
        [_t(tr, tcf), _prev_rows(tr, tcf), _next_rows(tr, tcf, l), _t(tr, tcf, dff), _next_rows(tr, tcf, l, dff),
         _t(tr, tcf), _next_rows(tr, tcf, l)] + [_v(tcf)] * 4,
        [_sds((2, l, dff), BF16)] + [_sds((1, dff))] * 4, [_st(tr, tcf)] + [_v(tcf)] * 4, acc=(1, 2, 3, 4))
    dh2 = _matmul("d_h2", dup, full['w_ffn_up'], "nt", fold=2)
    gd['w_ffn_up'], gd16['w_ffn_up'] = _matmul("dw_ffn_up", h2_t, dup, "nn", tm=512, tn=1408, out_stack=N_DEV, also_bf16=True)
    handle, tok = _exchange_start("grad_ffn_up_start", [gd16['w_ffn_up']], False, gd['w_ffn_up'])
    pending.append((['w_ffn_up'], handle))
    g_ffn_bwd = g_ffn + tok[0:1, 0:1]

    def res_norm_bwd_fn(xv, mo, g1v, gv, scv, shv, dhv, dxv):
        rows = xv.shape[0]
        _, vjp = jax.vjp(res_norm_fn, xv, mo, _bc(g1v, rows), _bc(gv, rows), _bc(scv, rows), _bc(shv, rows))
        dx, dmo, dg1v, dgv, dscv, dshv = vjp((dxv, dhv))
        return dx, dmo, _colsum(dg1v), _colsum(dgv), _colsum(dscv), _colsum(dshv)

    dx2, dmixout, dg1, dg_ffn, dsc2, dsh2 = _tile_call(
        "residual_norm_mod_ffn_bwd", res_norm_bwd_fn, (1, nrh), [xs, mixout, g1, g_ffn_bwd, sc2, sh2, dh2, dx3],
        [_t(trh, d), _t(trh, d), _v(d), _v(d), _v(d), _v(d), _t(trh, d), _t(trh, d)],
        [_sds((l, d)), _sds((l, d), BF16)] + [_sds((1, d))] * 4,
        [_t(trh, d), _t(trh, d)] + [_v(d)] * 4, acc=(2, 3, 4, 5))

    dmixed = _matmul("d_mixed", dmixout, full['w_out'], "nt")
    dw_out, dw_out16 = _matmul("dw_out", mixed, dmixout, "tn", also_bf16=True)
    gd['w_out'], gd16['w_out'] = [z.reshape((N_DEV,) + w_out.shape[1:]) for z in (dw_out, dw_out16)]

    def mix_bwd_fn(ga_, gb_, at, pa, ps, dm):
        _, vjp = jax.vjp(_mix_fn, ga_, gb_, at, pa, ps)
        da, db, dat, dpa, dps = vjp(dm)
        return jnp.stack([da, db], axis=0), dat, dpa, dps

    dglu, dattn, dga, dgs = _tile_call(
        "gate_mix_bwd", mix_bwd_fn, (d // tcd, l // trm), [glu, glu, attn, proj, proj, dmixed],
        mix_in_specs + [_t(trm, tcd)],
        [_sds((2, l, d), BF16)] + [_sds((l, d), BF16)] * 3, [_st(trm, tcd)] + [_t(trm, tcd)] * 3)

    dgy = _matmul("d_gelu_y", dglu, full['w_ssm_glu'], "nt")
    gd['w_ssm_glu'], gd16['w_ssm_glu'] = _matmul("dw_ssm_glu", gy, dglu, "tn", out_stack=N_DEV, also_bf16=True)

    def gelu_bwd_fn(yv, dv):
        _, vjp = jax.vjp(lambda z: jax.nn.gelu(z), yv)
        return vjp(dv)[0]

    dy = _tile_call("gelu_bwd", gelu_bwd_fn, (ssm_w // tcs, l // trg), [y, dgy], [_t(trg, tcs), _t(trg, tcs)],
                    [_sds((l, ssm_w))], [_t(trg, tcs)])[0]
    du, dbd, dcbdt, dlam, dd_tiles = _s5_bwd(proj, off_u, dy, hst, bd, bdt, cbdt, lam, dvec, t_len)

    def gpn_of(z):
        return z.transpose(2, 0, 1, 3).reshape(pgn)

    dbb_r = gpn_of(_diag_blocks(dbd[:, :, :TILE_STATES], SSM_P))
    dbb_i = gpn_of(_diag_blocks(dbd[:, :, TILE_STATES:], SSM_P))
    dc_re = _diag_blocks(dcbdt[:, :, :TILE_STATES], SSM_P).reshape(n_groups, SSM_P, SSM_N)
    dc_im = _diag_blocks(dcbdt[:, :, TILE_STATES:], SSM_P).reshape(n_groups, SSM_P, SSM_N)
    dlam_r, dlam_i = dlam[:, 0].reshape(gn), dlam[:, 1].reshape(gn)

    def disc_bwd_fn(ar, ai, ld, br, bi, dlr, dli, dbr, dbi):
        _, vjp = jax.vjp(_s5_disc_fn, ar, ai, ld, br, bi)
        return vjp((dlr, dli, dbr, dbi))

    da_re, da_im, dlog_dt, db_re, db_im = _tile_call(
        "s5_discretise_bwd", disc_bwd_fn, (1,), disc_ins + [dlam_r, dlam_i, dbb_r, dbb_i],
        disc_specs + [_full_spec(gn), _full_spec(gn), _full_spec(pgn), _full_spec(pgn)],
        [_sds(gn), _sds(gn), _sds((n_groups, 1)), _sds(pgn), _sds(pgn)], disc_specs)

    do2 = _matmul("d_attn_heads", dattn, full['w_attn_proj'], "nt")
    gd['w_attn_proj'], gd16['w_attn_proj'] = _matmul("dw_attn_proj", o2, dattn, "tn", out_stack=N_DEV, also_bf16=True)
    handle, tok = _exchange_start("grad_mixer_start", [gd16[k] for k in mixer_w], False, gd['w_attn_proj'])
    pending.append((mixer_w, handle))
    do_h = heads(do2.astype(BF16), hq)
    dq_h, dk_h, dv_h, dsink = _attn_bwd(qh, kh, vh, sinks3 + tok[0:1, 0:1], do_h)

    def unheads(z):
        return z.transpose(1, 0, 2).reshape(l, z.shape[0] * HEAD_DIM)

    early = ['attn_sinks', 'ssm_a_re', 'ssm_a_im', 'ssm_log_dt', 'ssm_b_re', 'ssm_b_im', 'ssm_c_re', 'ssm_c_im',
             'ssm_d', 'norm_ffn_g', 'ffn_conv_b', 'final_g']
    early_grads = {
        'attn_sinks': dsink[:, 0, 0], 'ssm_a_re': da_re, 'ssm_a_im': da_im, 'ssm_log_dt': dlog_dt,
        'ssm_b_re': db_re.transpose(1, 2, 0), 'ssm_b_im': db_im.transpose(1, 2, 0), 'ssm_c_re': dc_re,
        'ssm_c_im': dc_im, 'ssm_d': dd_tiles, 'norm_ffn_g': dg_ffn, 'ffn_conv_b': dconv_b, 'final_g': dg_fin}
    ge_pack, e_offs = _pack([jnp.concatenate([dg1, dsh2, dsc2, dg2], axis=1)] + [early_grads[k] for k in early],
                            LANES, 8)
    h_early, tok = _exchange_start("gather_small_early_start", [ge_pack], True, dsink)

    dproj = jnp.concatenate([unheads(dq_h), unheads(dk_h), unheads(dv_h), du, dga, dgs], axis=1)
    dw_in, dw_in16 = _matmul("dw_in", h1_t, dproj, "nn", tm=512, tn=1280, also_bf16=True, dep=tok)
    dcw = jnp.concatenate([dcw0, dcw1, dcw2], axis=0)
    shard_in, shard_cw = w_in.shape[1:], ffn_conv_w.shape[1:]
    gd16['w_in'] = dw_in16.reshape(shard_in[0], N_DEV, shard_in[1]).transpose(1, 0, 2)
    own_in = lax.dynamic_slice_in_dim(dw_in, idx * shard_in[1], shard_in[1], axis=1)[None]
    gd['ffn_conv_w'] = dcw.reshape(shard_cw[0], N_DEV, shard_cw[1]).transpose(1, 0, 2)
    gd16['ffn_conv_w'] = gd['ffn_conv_w'].astype(BF16)
    handle, tok = _exchange_start("grad_in_start", [gd16['w_in'], gd16['ffn_conv_w']], False, dw_in)
    pending.append((['w_in', 'ffn_conv_w'], handle))
    dh1 = _matmul("d_h1", dproj, full['w_in'], "nt", tm=512, dep=tok)

    def norm_bwd_fn(xv, gv, scv, shv, dhv, dxv):
        rows = xv.shape[0]
        _, vjp = jax.vjp(_norm_mod, xv, _bc(gv, rows), _bc(scv, rows), _bc(shv, rows))
        dx, dgv, dscv, dshv = vjp(dhv)
        return dx + dxv, _colsum(dgv), _colsum(dscv), _colsum(dshv)

    grad_x, dg_mix, dsc1, dsh1 = _tile_call(
        "norm_mod_mix_bwd", norm_bwd_fn, (1, nrh), [xs, g_mix, sc1, sh1, dh1, dx2],
        [_t(trh, d), _v(d), _v(d), _v(d), _t(trh, d), _t(trh, d)],
        [_sds((l, d))] + [_sds((1, d))] * 3, [_t(trh, d)] + [_v(d)] * 3, acc=(1, 2, 3))

    gl_pack, l_offs = _pack([jnp.concatenate([dsh1, dsc1], axis=1), dg_mix], LANES, 8)
    h_late, tok = _exchange_start("gather_small_late_start", [gl_pack], True, grad_x)

    sharded = big + ['ffn_conv_w']
    sharded_out = {}

    def finish(group, handle, after):
        for k, parts in zip(group, _exchange_wait("grad_" + group[0] + "_wait", handle, after)):
            own_src, own_at = (own_in, 0 * idx) if k == 'w_in' else (gd[k], idx)
            sharded_out[k] = _adamw_sharded("adamw_" + k, parts, own_src, given[k][0], given['m_' + k][0],
                                            given['v_' + k][0], jnp.stack([idx, own_at]).astype(jnp.int32))

    for group, handle in pending[:-1]:
        finish(group, handle, tok)
    done = functools.reduce(lambda p, q: p + q, [sharded_out[k][1][0:1, 0:1] for g_, _ in pending[:-1] for k in g_])
    finish(*pending[-1], done)

    ge_all, = _exchange_wait("gather_small_early_wait", h_early, done)
    gl_all, = _exchange_wait("gather_small_late_wait", h_late, sharded_out['w_in'][1])
    gs_all = jnp.concatenate([ge_all, gl_all], axis=1)
    rows_e = ge_pack.shape[0]

    def small_pack(prefix):
        ab = given[prefix + 'ada_b']
        p_early, _ = _pack([ab[:, 2 * d:]] + [given[prefix + k] for k in early], LANES, 8)
        p_late, _ = _pack([ab[:, :2 * d], given[prefix + 'norm_mix_g']], LANES, 8)
        return jnp.concatenate([p_early, p_late], axis=0)

    small_out = _adamw("adamw_replicated", gs_all, small_pack(''), small_pack('m_'), small_pack('v_'))

    dmod_all = jnp.concatenate([_unpack(gl_all, l_offs[0], (2 * d,), lead=(N_DEV,)),
                                _unpack(ge_all, e_offs[0], (4 * d,), lead=(N_DEV,))], axis=1)
    dmod_mine = lax.dynamic_slice_in_dim(dmod_all, idx * mod_n, mod_n, axis=1)
    kpad = LANES - N_DEV
    cond_t = jnp.pad(cond_all.T, ((0, 0), (0, kpad)))
    dmod_pad = jnp.pad(dmod_mine, ((0, kpad), (0, 0)))
    g_ada_w = _matmul("dw_ada", cond_t, dmod_pad, "nn")
    ada_out = _adamw("adamw_ada_w", g_ada_w[None], ada_w[0], m_ada_w[0], v_ada_w[0])

    results = [{}, {}, {}, {}]
    for which in range(4):
        out = small_out[which]
        results[which]['ada_b'] = jnp.concatenate([_unpack(out, rows_e + l_offs[0], (1, 2 * d)),
                                                   _unpack(out, e_offs[0], (1, 4 * d))], axis=1)
        results[which]['norm_mix_g'] = _unpack(out, rows_e + l_offs[1], norm_mix_g.shape)
        for k, off in zip(early, e_offs[1:]):
            results[which][k] = _unpack(out, off, given[k].shape)
        for k in sharded:
            results[which][k] = sharded_out[k][which][None]
        results[which]['ada_w'] = ada_out[which][None]
    outs = [loss, grad_x[None]]
    for which in range(4):
        outs += [results[which][k] for k in names]
    return tuple(outs)
```

```python
import functools
import math

import jax
import jax.numpy as jnp
from jax import lax
from jax.experimental import pallas as pl
from jax.experimental.pallas import tpu as pltpu

F32, BF16 = jnp.float32, jnp.bfloat16
MESH = pl.DeviceIdType.MESH
N_DEV = 8

HEAD_DIM = 64
N_KV_HEADS = 2
ATT_BLOCK = 128
NEG_INF = -1e30
SSM_P = 16
SSM_N = 64
LANES = 128
TILE_GROUPS = LANES // SSM_P
TILE_STATES = TILE_GROUPS * SSM_N
RMS_EPS = 1e-6
ADAM_LR, ADAM_B1, ADAM_B2, ADAM_EPS, ADAM_WD, ADAM_STEP = 0.001, 0.9, 0.999, 1e-08, 0.01, 10
VMEM_LIMIT = 56 * 1024 * 1024
MATMUL_VMEM_BUDGET = 44 * 1024 * 1024


def _params(n_axes):
    return pltpu.CompilerParams(dimension_semantics=("arbitrary",) * n_axes, vmem_limit_bytes=VMEM_LIMIT)


def _pick(dim, pref, align=128):
    if dim <= align:
        return dim
    t = (min(pref, dim) // align) * align
    while t > align and dim % t:
        t -= align
    assert dim % t == 0, (dim, pref, align)
    return t


def _dev():
    return lax.axis_index("x"), lax.axis_index("y"), lax.axis_index("c")


def _tile_call(name, fn, grid, ins, in_specs, out_shapes, out_specs, acc=()):
    n_in, n_out = len(ins), len(out_shapes)
    acc_axis = len(grid) - 1

    def body(*refs):
        vals = fn(*[r[...] for r in refs[:n_in]])
        if not isinstance(vals, (tuple, list)):
            vals = (vals,)
        assert len(vals) == n_out
        for i, (r, v) in enumerate(zip(refs[n_in:], vals)):
            v = v.astype(r.dtype)
            if i in acc:
                first = pl.program_id(acc_axis) == 0

                @pl.when(first)
                def _():
                    r[...] = v

                @pl.when(jnp.logical_not(first))
                def _():
                    r[...] += v
            else:
                r[...] = v

    return pl.pallas_call(
        body, grid=grid, in_specs=in_specs, out_specs=out_specs, out_shape=out_shapes, name=name,
        compiler_params=_params(len(grid)),
    )(*ins)


def _t(tr, tc, off=0):
    return pl.BlockSpec((tr, tc), lambda j, i: (i, j + off // tc))


def _tt(tr, tc):
    return pl.BlockSpec((tc, tr), lambda j, i: (j, i))


def _v(tc, off=0, rows=1):
    return pl.BlockSpec((rows, tc), lambda j, i: (0, j + off // tc))


HALO = 16


def _prev_rows(tr, tc, off=0):
    return pl.BlockSpec((HALO, tc), lambda j, i: (jnp.maximum(i * (tr // HALO) - 1, 0), j + off // tc))


def _next_rows(tr, tc, nrows, off=0):
    return pl.BlockSpec((HALO, tc),
                        lambda j, i: (jnp.minimum((i + 1) * (tr // HALO), nrows // HALO - 1), j + off // tc))


def _st(tr, tc):
    return pl.BlockSpec((2, tr, tc), lambda j, i: (0, i, j))


def _bc(v, rows):
    return jnp.broadcast_to(v, (rows, v.shape[-1]))


def _colsum(v):
    return jnp.sum(v, axis=0, keepdims=True)


def _matmul(name, a, b, mode, out_dtype=F32, tm=1024, tn=1024, tk=None, out_stack=None, also_bf16=False, dep=None,
            fold=1, epilogue=None):
    def dims(z):
        return (z.shape[-2], z.shape[-1] * (z.shape[0] if z.ndim == 3 else 1))

    ar, ac = dims(a)
    br, bc = dims(b)
    if mode == "nn":
        m, k, n = ar, ac, bc
        assert br == k
    elif mode == "nt":
        m, k, n = ar, ac, br
        assert bc == k
    else:
        m, k, n = ac, ar, bc
        assert br == k
    m_lim, k_lim, n_lim = [m], [k], [n]
    if a.ndim == 3:
        (m_lim if mode == "tn" else k_lim).append(a.shape[-1])
    if b.ndim == 3:
        (k_lim if mode == "nt" else n_lim).append(b.shape[-1])
    if out_stack:
        n_lim.append(n // out_stack)
    tm = _pick(functools.reduce(math.gcd, m_lim), tm)
    tn = _pick(functools.reduce(math.gcd, n_lim), tn)
    k_unit = functools.reduce(math.gcd, k_lim)
    if tk is None:
        sa, sb, so = a.dtype.itemsize, b.dtype.itemsize, jnp.dtype(out_dtype).itemsize + (2 if also_bf16 else 0)
        fits = [t for t in range(LANES, k_unit + 1, LANES) if k_unit % t == 0 and
                2 * t * (tm * sa + tn * sb) + tm * tn * (2 * so + (4 if t < k else 0)) <= MATMUL_VMEM_BUDGET]
        tk = max(fits) if fits else _pick(k_unit, 512)
    else:
        tk = _pick(k_unit, tk)
    assert (k // tk) % fold == 0
    nk = k // (tk * fold)

    def spec(z, brows, bcols, ridx, cidx):
        if z.ndim == 3:
            per = z.shape[-1] // bcols
            return pl.BlockSpec((None, brows, bcols),
                                lambda i, j, kk: (cidx(i, j, kk) // per, ridx(i, j, kk), cidx(i, j, kk) % per))
        return pl.BlockSpec((brows, bcols), lambda i, j, kk: (ridx(i, j, kk), cidx(i, j, kk)))

    gi = lambda i, j, kk: i
    gj = lambda i, j, kk: j
    a_specs, b_specs = [], []
    for f in range(fold):
        gk = lambda i, j, kk, f=f: fold * kk + f
        if mode == "nn":
            a_specs.append(spec(a, tm, tk, gi, gk))
            b_specs.append(spec(b, tk, tn, gk, gj))
            dn = (((1,), (0,)), ((), ()))
        elif mode == "nt":
            a_specs.append(spec(a, tm, tk, gi, gk))
            b_specs.append(spec(b, tn, tk, gj, gk))
            dn = (((1,), (1,)), ((), ()))
        else:
            a_specs.append(spec(a, tk, tm, gk, gi))
            b_specs.append(spec(b, tk, tn, gk, gj))
            dn = (((0,), (0,)), ((), ()))

    epi_fn, epi_ins, epi_outs = epilogue if epilogue else (None, [], [])
    n_out = len(epi_outs) if epilogue else (2 if also_bf16 else 1)

    deps = [] if dep is None else [dep]

    def body(*refs):
        a_refs, b_refs = refs[:fold], refs[fold:2 * fold]
        e_refs = refs[2 * fold:2 * fold + len(epi_ins)]
        rest = refs[2 * fold + len(epi_ins) + len(deps):]
        o_refs, acc = rest[:n_out], rest[n_out:]
        part = None
        for a_ref, b_ref in zip(a_refs, b_refs):
            one = lax.dot_general(a_ref[...].astype(BF16), b_ref[...].astype(BF16), dn, preferred_element_type=F32)
            part = one if part is None else part + one

        def emit(val):
            vals = epi_fn(val, *[r[...] for r in e_refs]) if epilogue else [val] * n_out
            for o_ref, v in zip(o_refs, vals):
                o_ref[...] = v.astype(o_ref.dtype)

        if nk == 1:
            emit(part)
            return
        acc_ref, = acc
        kk = pl.program_id(2)

        @pl.when(kk == 0)
        def _():
            acc_ref[...] = part

        @pl.when(kk > 0)
        def _():
            acc_ref[...] += part

        @pl.when(kk == nk - 1)
        def _():
            emit(acc_ref[...])

    if out_stack:
        per = (n // out_stack) // tn
        out_spec = pl.BlockSpec((None, tm, tn), lambda i, j, kk: (j // per, i, j % per))
        shape = (out_stack, m, n // out_stack)
    else:
        out_spec = pl.BlockSpec((tm, tn), lambda i, j, kk: (i, j))
        shape = (m, n)
    if epilogue:
        assert not out_stack and not also_bf16
        out_specs = [pl.BlockSpec((tn, tm), lambda i, j, kk: (j, i)) if t else out_spec for _, t in epi_outs]
        out_shapes = [jax.ShapeDtypeStruct((n, m) if t else (m, n), dt) for dt, t in epi_outs]
    else:
        out_specs = [out_spec] * n_out
        out_shapes = [jax.ShapeDtypeStruct(shape, dt) for dt in [out_dtype, BF16][:n_out]]
    e_specs = [pl.BlockSpec((1, tn), lambda i, j, kk: (0, j)) if z.shape[0] == 1 else
               pl.BlockSpec((tm, tn), lambda i, j, kk: (i, j)) for z in epi_ins]
    res = pl.pallas_call(
        body, grid=(m // tm, n // tn, nk),
        in_specs=a_specs + b_specs + e_specs + [pl.BlockSpec(memory_space=pl.ANY)] * len(deps),
        out_specs=out_specs, out_shape=out_shapes,
        scratch_shapes=[pltpu.VMEM((tm, tn), F32)] if nk > 1 else [], name=name, compiler_params=_params(3),
    )(*[a] * fold, *[b] * fold, *epi_ins, *deps)
    return res if (also_bf16 or epilogue) else res[0]


def _all_gather(name, arrs, dep=None):
    n = len(arrs)
    deps = [] if dep is None else [dep]

    def body(*refs):
        ins, outs = refs[:n], refs[n + len(deps):2 * n + len(deps)]
        send_sems, recv_sems, local_sems = refs[2 * n + len(deps):]
        x, y, c = _dev()
        me, sib = (x, y, c), (x, y, 1 - c)
        chips = [(1 - x, y), (x, 1 - y), (1 - x, 1 - y)]

        def slot(p):
            return 4 * p[0] + 2 * p[1] + p[2]

        def copy(a, k, block, to, src=None):
            dst = outs[a].at[slot(block)]
            return pltpu.make_async_remote_copy(
                src_ref=dst if src is None else src, dst_ref=dst,
                send_sem=send_sems.at[7 * a + k], recv_sem=recv_sems.at[7 * a + k],
                device_id=to, device_id_type=MESH)

        mine = [pltpu.make_async_copy(ins[a], outs[a].at[slot(me)], local_sems.at[a]) for a in range(n)]
        for cp in mine:
            cp.start()
        first = []
        for a in range(n):
            first.append(copy(a, 0, me, sib, src=ins[a]))
            first += [copy(a, 1 + j, me, (*chip, c), src=ins[a]) for j, chip in enumerate(chips)]
        for cp in first:
            cp.start()
        passed = []
        for j, chip in enumerate(chips):
            for a in range(n):
                copy(a, 1 + j, (*chip, c), me).wait_recv()
                cp = copy(a, 4 + j, (*chip, c), sib)
                cp.start()
                passed.append(cp)
        for a in range(n):
            copy(a, 0, sib, me).wait_recv()
            for j, chip in enumerate(chips):
                copy(a, 4 + j, (*chip, 1 - c), me).wait_recv()
        for cp in first + passed:
            cp.wait_send()
        for cp in mine:
            cp.wait()

    any_spec = pl.BlockSpec(memory_space=pl.ANY)
    return pl.pallas_call(
        body, in_specs=[any_spec] * (n + len(deps)), out_specs=[any_spec] * n,
        out_shape=[jax.ShapeDtypeStruct((N_DEV,) + a.shape, a.dtype) for a in arrs],
        scratch_shapes=[pltpu.SemaphoreType.DMA((7 * n,)), pltpu.SemaphoreType.DMA((7 * n,)),
                        pltpu.SemaphoreType.DMA((n,))],
        name=name,
    )(*arrs, *deps)


def _grad_to_sibling(gds):
    n = len(gds)

    def body(*refs):
        g_refs, r_refs = refs[:n], refs[n:2 * n]
        send_sems, recv_sems = refs[2 * n:]
        x, y, c = _dev()
        cps = []
        for a in range(n):
            for k in range(4):
                cp = pltpu.make_async_remote_copy(
                    src_ref=g_refs[a].at[2 * k + (1 - c)], dst_ref=r_refs[a].at[k],
                    send_sem=send_sems.at[4 * a + k], recv_sem=recv_sems.at[4 * a + k],
                    device_id=(x, y, 1 - c), device_id_type=MESH)
                cp.start()
                cps.append(cp)
        for cp in cps:
            cp.wait()

    any_spec = pl.BlockSpec(memory_space=pl.ANY)
    return pl.pallas_call(
        body, in_specs=[any_spec] * n, out_specs=[any_spec] * n,
        out_shape=[jax.ShapeDtypeStruct((4,) + g.shape[1:], g.dtype) for g in gds],
        scratch_shapes=[pltpu.SemaphoreType.DMA((4 * n,)), pltpu.SemaphoreType.DMA((4 * n,))],
        name="grad_to_sibling",
    )(*gds)


def _chip_sum(name, gd, from_sib, c_arr):
    _, k, n = gd.shape
    tr = _pick(k, max(16, (1 << 20) // (4 * n)), 16)

    def body(c_ref, a_ref, b_ref, o_ref):
        o_ref[...] = (a_ref[...] + b_ref[...]).astype(o_ref.dtype)

    return pl.pallas_call(
        body,
        grid_spec=pltpu.PrefetchScalarGridSpec(
            num_scalar_prefetch=1, grid=(4, k // tr),
            in_specs=[pl.BlockSpec((1, tr, n), lambda q, i, cr: (2 * q + cr[0], i, 0)),
                      pl.BlockSpec((1, tr, n), lambda q, i, cr: (q, i, 0))],
            out_specs=pl.BlockSpec((1, tr, n), lambda q, i, cr: (q, i, 0))),
        out_shape=jax.ShapeDtypeStruct((4, k, n), BF16), name=name, compiler_params=_params(2),
    )(c_arr, gd, from_sib)


def _grad_to_chips(sums):
    n = len(sums)

    def body(*refs):
        s_refs, p_refs = refs[:n], refs[n:2 * n]
        send_sems, recv_sems, local_sems = refs[2 * n:]
        x, y, c = _dev()
        my_chip = 2 * x + y
        cps = []
        for a in range(n):
            local = pltpu.make_async_copy(s_refs[a].at[my_chip], p_refs[a].at[my_chip], local_sems.at[a])
            local.start()
            cps.append(local)
            for j, (px, py) in enumerate([(1 - x, y), (x, 1 - y), (1 - x, 1 - y)]):
                cp = pltpu.make_async_remote_copy(
                    src_ref=s_refs[a].at[2 * px + py], dst_ref=p_refs[a].at[my_chip],
                    send_sem=send_sems.at[3 * a + j], recv_sem=recv_sems.at[3 * a + j],
                    device_id=(px, py, c), device_id_type=MESH)
                cp.start()
                cps.append(cp)
        for cp in cps:
            cp.wait()

    any_spec = pl.BlockSpec(memory_space=pl.ANY)
    return pl.pallas_call(
        body, in_specs=[any_spec] * n, out_specs=[any_spec] * n,
        out_shape=[jax.ShapeDtypeStruct(s.shape, s.dtype) for s in sums],
        scratch_shapes=[pltpu.SemaphoreType.DMA((3 * n,)), pltpu.SemaphoreType.DMA((3 * n,)),
                        pltpu.SemaphoreType.DMA((n,))],
        name="grad_to_chips",
    )(*sums)


FLIPS = [(0, 0, 1), (0, 1, 0), (1, 0, 0), (0, 1, 1), (1, 0, 1), (1, 1, 0), (1, 1, 1)]
N_PEERS = len(FLIPS)
_HBM = pl.BlockSpec(memory_space=pltpu.HBM)
_SEM = pl.BlockSpec(memory_space=pltpu.SEMAPHORE)
_EFFECT = pltpu.SideEffectType.DATAFLOW_SIDE_EFFECTING


def _flip(x, y, c, f):
    return (1 - x if f[0] else x, 1 - y if f[1] else y, 1 - c if f[2] else c)


def _slot(p):
    return 4 * p[0] + 2 * p[1] + p[2]


def _exchange_copies(src_refs, land_refs, send_sems, recv_sems, gather):
    x, y, c = _dev()
    mine = _slot((x, y, c))
    cps = []
    for a, (src, land) in enumerate(zip(src_refs, land_refs)):
        for k, f in enumerate(FLIPS):
            peer = _flip(x, y, c, f)
            cps.append(pltpu.make_async_remote_copy(
                src_ref=src if gather else src.at[_slot(peer)], dst_ref=land.at[mine],
                send_sem=send_sems.at[N_PEERS * a + k], recv_sem=recv_sems.at[N_PEERS * a + k],
                device_id=peer, device_id_type=MESH))
    return cps


def _exchange_start(name, srcs, gather, after):
    n = len(srcs)
    lands = [lax.empty(((N_DEV,) + s.shape) if gather else s.shape, s.dtype) for s in srcs]

    def body(*refs):
        src_refs, land_refs = refs[:n], refs[n:2 * n]
        send_sems, recv_sems, local_sems = refs[2 * n + 1:2 * n + 4]
        token = refs[-1]
        if gather:
            x, y, c = _dev()
            for a in range(n):
                pltpu.make_async_copy(src_refs[a], land_refs[a].at[_slot((x, y, c))], local_sems.at[a]).start()
        for cp in _exchange_copies(src_refs, land_refs, send_sems, recv_sems, gather):
            cp.start()
        token[...] = jnp.zeros_like(token)

    hbm = lambda z: pltpu.HBM(z.shape, z.dtype)
    outs = pl.pallas_call(
        body, name=name,
        out_shape=(pltpu.SemaphoreType.DMA((N_PEERS * n,)), pltpu.SemaphoreType.DMA((N_PEERS * n,)),
                   pltpu.SemaphoreType.DMA((n,)), *[hbm(s) for s in srcs], *[hbm(z) for z in lands],
                   jax.ShapeDtypeStruct((8, LANES), F32)),
        in_specs=[_HBM] * (2 * n) + [pl.BlockSpec(memory_space=pl.ANY)],
        out_specs=(_SEM, _SEM, _SEM, *[_HBM] * (2 * n), pl.BlockSpec(memory_space=pltpu.VMEM)),
        input_output_aliases={i: 3 + i for i in range(2 * n)},
        compiler_params=pltpu.CompilerParams(has_side_effects=_EFFECT),
    )(*[pltpu.with_memory_space_constraint(z, pltpu.HBM) for z in list(srcs) + lands], after)
    return (outs[:3], outs[3:3 + n], outs[3 + n:3 + 2 * n], gather), outs[-1]


def _exchange_wait(name, handles, after):
    sems, srcs, lands, gather = handles
    n = len(srcs)

    def body(*refs):
        src_refs, land_refs = refs[:n], refs[n:2 * n]
        send_sems, recv_sems, local_sems = refs[2 * n:2 * n + 3]
        if gather:
            for a in range(n):
                pltpu.make_async_copy(src_refs[a], land_refs[a].at[0], local_sems.at[a]).wait()
        for cp in _exchange_copies(src_refs, land_refs, send_sems, recv_sems, gather):
            cp.wait_send()
            cp.wait_recv()

    hbm = lambda z: pltpu.HBM(z.shape, z.dtype)
    outs = pl.pallas_call(
        body, name=name, out_shape=tuple(hbm(z) for z in list(srcs) + list(lands)),
        in_specs=[_HBM] * (2 * n) + [_SEM] * 3 + [pl.BlockSpec(memory_space=pl.ANY)],
        out_specs=tuple([_HBM] * (2 * n)), input_output_aliases={i: i for i in range(2 * n)},
        compiler_params=pltpu.CompilerParams(has_side_effects=_EFFECT),
    )(*srcs, *lands, *sems, after)
    return list(outs[n:])


def _pack_rows(sizes, width, row_align):
    offs, r = [], 0
    for s in sizes:
        offs.append(r)
        r += -(-s // width)
    total = -(-r // row_align) * row_align
    return offs, total


def _pack(items, width, row_align, lead=()):
    nl = len(lead)
    sizes = [int(jnp.size(a)) // max(1, functools.reduce(lambda p, q: p * q, lead, 1)) for a in items]
    offs, total = _pack_rows(sizes, width, row_align)
    flat = []
    used = 0
    for a, s in zip(items, sizes):
        f = a.reshape(lead + (s,))
        pad = -(-s // width) * width - s
        if pad:
            f = jnp.pad(f, [(0, 0)] * nl + [(0, pad)])
        flat.append(f)
        used += s + pad
    tail = total * width - used
    if tail:
        flat.append(jnp.zeros(lead + (tail,), items[0].dtype))
    return jnp.concatenate(flat, axis=-1).reshape(lead + (total, width)), offs


def _unpack(packed, off, shape, lead=()):
    nl = len(lead)
    size = functools.reduce(lambda p, q: p * q, shape, 1)
    width = packed.shape[-1]
    rows = -(-size // width)
    blk = lax.slice_in_dim(packed, off, off + rows, axis=nl).reshape(lead + (rows * width,))
    return lax.slice_in_dim(blk, 0, size, axis=nl).reshape(lead + tuple(shape))


def _rms(x, g):
    return (x * lax.rsqrt(jnp.mean(x * x, axis=-1, keepdims=True) + RMS_EPS)) * g


def _norm_mod(x, g, sc, sh):
    return _rms(x, g) * (1.0 + sc) + sh


def _mix_fn(glu_a, glu_b, attn, ga, gs):
    return jax.nn.sigmoid(ga) * attn + jax.nn.sigmoid(gs) * (glu_a * jax.nn.sigmoid(glu_b))


def _s5_disc_fn(a_re, a_im, log_dt, b_re, b_im):
    dt = jnp.exp(log_dt)
    mag = jnp.exp(a_re * dt)
    lr, li = mag * jnp.cos(a_im * dt), mag * jnp.sin(a_im * dt)
    den = a_re * a_re + a_im * a_im
    zr = ((lr - 1.0) * a_re + li * a_im) / den
    zi = (li * a_re - (lr - 1.0) * a_im) / den
    return lr, li, zr[None] * b_re - zi[None] * b_im, zr[None] * b_im + zi[None] * b_re


def _adamw_fn(w, g, m, v):
    m = ADAM_B1 * m + (1.0 - ADAM_B1) * g
    v = ADAM_B2 * v + (1.0 - ADAM_B2) * jnp.square(g)
    m_hat = m / (1.0 - ADAM_B1 ** ADAM_STEP)
    v_hat = v / (1.0 - ADAM_B2 ** ADAM_STEP)
    delta = -ADAM_LR * (m_hat / (jnp.sqrt(v_hat) + ADAM_EPS) + ADAM_WD * w)
    return delta, m, v


def _adamw(name, parts, w, m, v):
    p, r, c = parts.shape
    tr = _pick(r, max(8, (1 << 21) // (4 * c * max(p, 2))), 8)

    def fn(pv, wv, mv, vv):
        g = pv[0]
        for i in range(1, p):
            g = g + pv[i]
        d, m2, v2 = _adamw_fn(wv, g, mv, vv)
        return g, d, m2, v2

    spec = pl.BlockSpec((tr, c), lambda i: (i, 0))
    return _tile_call(
        name, fn, (r // tr,), [parts, w, m, v],
        [pl.BlockSpec((p, tr, c), lambda i: (0, i, 0)), spec, spec, spec],
        [jax.ShapeDtypeStruct((r, c), F32)] * 4, [spec] * 4)


def _adamw_sharded(name, parts, own_src, w, m, v, place):
    _, k, n = parts.shape
    tr = _pick(k, max(16, (1 << 19) // (4 * n)), 16)

    def body(pl_ref, p_ref, a_ref, w_ref, m_ref, v_ref, g_ref, d_ref, m2_ref, v2_ref):
        own = a_ref[0]
        g = None
        for q in range(N_DEV):
            term = jnp.where(pl_ref[0] == q, own, p_ref[q].astype(F32))
            g = term if g is None else g + term
        d, m2, v2 = _adamw_fn(w_ref[...], g, m_ref[...], v_ref[...])
        g_ref[...] = g
        d_ref[...] = d
        m2_ref[...] = m2
        v2_ref[...] = v2

    spec = pl.BlockSpec((tr, n), lambda i, pr: (i, 0))
    return pl.pallas_call(
        body,
        grid_spec=pltpu.PrefetchScalarGridSpec(
            num_scalar_prefetch=1, grid=(k // tr,),
            in_specs=[pl.BlockSpec((N_DEV, tr, n), lambda i, pr: (0, i, 0)),
                      pl.BlockSpec((1, tr, n), lambda i, pr: (pr[1], i, 0)),
                      spec, spec, spec],
            out_specs=[spec] * 4),
        out_shape=[jax.ShapeDtypeStruct((k, n), F32)] * 4, name=name, compiler_params=_params(1),
    )(place, parts, own_src, w, m, v)


def _attn_mask(n, rows):
    qi = lax.broadcasted_iota(jnp.int32, (rows, 2 * ATT_BLOCK), 0) & (ATT_BLOCK - 1)
    kj = lax.broadcasted_iota(jnp.int32, (rows, 2 * ATT_BLOCK), 1)
    rel = qi + ATT_BLOCK - kj
    return (rel >= 0) & (rel < ATT_BLOCK) & ((kj >= ATT_BLOCK) | (n > 0))


def _attn_probs(q, k, sink, mask):
    s = lax.dot_general(q, k, (((1,), (1,)), ((), ())), preferred_element_type=F32) * (HEAD_DIM ** -0.5)
    s = jnp.where(mask, s, NEG_INF)
    m = jnp.maximum(jnp.max(s, axis=-1, keepdims=True), sink)
    p = jnp.exp(s - m)
    e_sink = jnp.exp(sink - m)
    inv = 1.0 / (jnp.sum(p, axis=-1, keepdims=True) + e_sink)
    return p * inv, e_sink * inv


def _attn_specs(qpk):
    blk = ATT_BLOCK
    q_spec = pl.BlockSpec((qpk, blk, HEAD_DIM), lambda h, n: (h, n, 0))
    cur = pl.BlockSpec((1, blk, HEAD_DIM), lambda h, n: (h, n, 0))
    prev = pl.BlockSpec((1, blk, HEAD_DIM), lambda h, n: (h, jnp.maximum(n - 1, 0), 0))
    sink_spec = pl.BlockSpec((1, qpk * blk, 1), lambda h, n: (h, 0, 0))
    return q_spec, cur, prev, sink_spec


def _attn_fwd(q, k, v, sinks):
    hq, l, _ = q.shape
    qpk = hq // N_KV_HEADS
    nb = l // ATT_BLOCK
    rows = qpk * ATT_BLOCK
    q_spec, cur, prev, sink_spec = _attn_specs(qpk)

    def body(q_ref, kp_ref, kc_ref, vp_ref, vc_ref, sink_ref, o_ref):
        mask = _attn_mask(pl.program_id(1), rows)
        kk = jnp.concatenate([kp_ref[0], kc_ref[0]], axis=0).astype(BF16)
        vv = jnp.concatenate([vp_ref[0], vc_ref[0]], axis=0).astype(BF16)
        p, _ = _attn_probs(q_ref[...].reshape(rows, HEAD_DIM).astype(BF16), kk, sink_ref[0], mask)
        o = jnp.dot(p.astype(BF16), vv, preferred_element_type=F32)
        o_ref[...] = o.reshape(qpk, ATT_BLOCK, HEAD_DIM).astype(o_ref.dtype)

    return pl.pallas_call(
        body, grid=(N_KV_HEADS, nb), in_specs=[q_spec, prev, cur, prev, cur, sink_spec],
        out_specs=q_spec, out_shape=jax.ShapeDtypeStruct((hq, l, HEAD_DIM), BF16),
        name="attn_fwd", compiler_params=_params(2),
    )(q, k, k, v, v, sinks)


def _attn_bwd(q, k, v, sinks, do):
    hq, l, _ = q.shape
    qpk = hq // N_KV_HEADS
    nb = l // ATT_BLOCK
    blk = ATT_BLOCK
    rows = qpk * blk
    q_spec, cur, prev, sink_spec = _attn_specs(qpk)
    part_spec = pl.BlockSpec((1, 1, 2 * blk, HEAD_DIM), lambda h, n: (h, n, 0, 0))
    dsink_spec = pl.BlockSpec((qpk, 1, LANES), lambda h, n: (h, 0, 0))
    tn = (((0,), (0,)), ((), ()))

    def body(q_ref, do_ref, kp_ref, kc_ref, vp_ref, vc_ref, sink_ref, dq_ref, dkp_ref, dvp_ref, dsink_ref):
        n = pl.program_id(1)
        mask = _attn_mask(n, rows)
        kk = jnp.concatenate([kp_ref[0], kc_ref[0]], axis=0).astype(BF16)
        vv = jnp.concatenate([vp_ref[0], vc_ref[0]], axis=0).astype(BF16)
        qb = q_ref[...].reshape(rows, HEAD_DIM).astype(BF16)
        do32 = do_ref[...].astype(F32).reshape(rows, HEAD_DIM)
        dob = do32.astype(BF16)
        p, p_sink = _attn_probs(qb, kk, sink_ref[0], mask)
        pb = p.astype(BF16)
        o = jnp.dot(pb, vv, preferred_element_type=F32)
        delta = jnp.sum(do32 * o, axis=-1, keepdims=True)
        dp = lax.dot_general(dob, vv, (((1,), (1,)), ((), ())), preferred_element_type=F32)
        ds = (p * (dp - delta) * (HEAD_DIM ** -0.5)).astype(BF16)
        dq = jnp.dot(ds, kk, preferred_element_type=F32)
        dq_ref[...] = dq.reshape(qpk, blk, HEAD_DIM).astype(dq_ref.dtype)
        dkp_ref[0, 0] = lax.dot_general(ds, qb, tn, preferred_element_type=F32)
        dvp_ref[0, 0] = lax.dot_general(pb, dob, tn, preferred_element_type=F32)
        dsr = p_sink * delta
        for g in range(qpk):
            dsg = jnp.broadcast_to(-_colsum(dsr[g * blk:(g + 1) * blk]), (1, LANES))

            @pl.when(n == 0)
            def _():
                dsink_ref[g] = dsg

            @pl.when(n > 0)
            def _():
                dsink_ref[g] += dsg


    part_shape = jax.ShapeDtypeStruct((N_KV_HEADS, nb, 2 * blk, HEAD_DIM), F32)
    dq, dkp, dvp, dsink = pl.pallas_call(
        body, grid=(N_KV_HEADS, nb), in_specs=[q_spec, q_spec, prev, cur, prev, cur, sink_spec],
        out_specs=[q_spec, part_spec, part_spec, dsink_spec],
        out_shape=[jax.ShapeDtypeStruct((hq, l, HEAD_DIM), BF16), part_shape, part_shape,
                   jax.ShapeDtypeStruct((hq, 1, LANES), F32)],
        name="attn_bwd", compiler_params=_params(2),
    )(q, do, k, k, v, v, sinks)

    def combine(a_cur, a_nxt, b_cur, b_nxt):
        last = pl.program_id(1) == nb - 1
        keep = jnp.where(last, 0.0, 1.0)
        return (a_cur[0, 0, blk:] + keep * a_nxt[0, 0, :blk])[None], (b_cur[0, 0, blk:] + keep * b_nxt[0, 0, :blk])[None]

    nxt_spec = pl.BlockSpec((1, 1, 2 * blk, HEAD_DIM), lambda h, n: (h, jnp.minimum(n + 1, nb - 1), 0, 0))
    kv_shape = jax.ShapeDtypeStruct((N_KV_HEADS, l, HEAD_DIM), BF16)
    dk, dv = _tile_call("attn_dkv", combine, (N_KV_HEADS, nb), [dkp, dkp, dvp, dvp],
                        [part_spec, nxt_spec, part_spec, nxt_spec], [kv_shape, kv_shape], [cur, cur])
    return dq, dk, dv, dsink


def _block_diag(m):
    j, gl, a, b = m.shape
    eye = jnp.eye(gl, dtype=m.dtype)
    return (m[:, :, :, None, :] * eye[None, :, None, :, None]).reshape(j, gl * a, gl * b)


def _diag_blocks(z, a):
    j = z.shape[0]
    gl = z.shape[1] // a
    b = z.shape[2] // gl
    d = jnp.diagonal(z.reshape(j, gl, a, gl, b), axis1=1, axis2=3)
    return d.transpose(0, 3, 1, 2)


def _s5_permute(src_ref, dst_ref, t_len):
    seg = t_len // 8
    for k in range(seg):
        dst_ref[8 * k:8 * k + 8, :] = src_ref[pl.ds(k, 8, stride=seg), :]


def _s5_unpermute(perm_ref, t_len, emit):
    per_seg = t_len // 64
    for m in range(t_len // 8):
        emit(8 * m, perm_ref[pl.ds(64 * (m % per_seg) + m // per_seg, 8, stride=8), :])


def _s5_powers(p_ref, lr, li, seg):
    hs = TILE_STATES

    def step(k, carry):
        pr, pi = carry
        p_ref[pl.ds(k, 1), 0:hs] = pr
        p_ref[pl.ds(k, 1), hs:2 * hs] = pi
        return lr * pr - li * pi, lr * pi + li * pr

    lax.fori_loop(0, seg, step, (lr, li))


def _s5_local_scan(x_ref, base, lr, li, seg, reverse):
    hs = TILE_STATES
    lr8, li8 = jnp.broadcast_to(lr, (8, hs)), jnp.broadcast_to(li, (8, hs))
    if reverse:
        li8 = -li8

    def step(i, carry):
        hr, hi = carry
        k = seg - 1 - i if reverse else i
        rows = pl.ds(pl.multiple_of(base + 8 * k, 8), 8)
        nr = lr8 * hr - li8 * hi + x_ref[rows, 0:hs]
        ni = lr8 * hi + li8 * hr + x_ref[rows, hs:2 * hs]
        x_ref[rows, 0:hs] = nr
        x_ref[rows, hs:2 * hs] = ni
        return nr, ni

    zero = jnp.zeros((8, hs), F32)
    return lax.fori_loop(0, seg, step, (zero, zero), unroll=2)


def _s5_carries(c_ref, e_ref, ends, start, pw_r, pw_i, reverse):
    hs = TILE_STATES
    e_ref[:, 0:hs] = ends[0]
    e_ref[:, hs:2 * hs] = ends[1]
    cr, ci = start
    if reverse:
        pw_i = -pw_i
    for s in (range(7, -1, -1) if reverse else range(8)):
        c_ref[s:s + 1, 0:hs] = cr
        c_ref[s:s + 1, hs:2 * hs] = ci
        er, ei = e_ref[s:s + 1, 0:hs], e_ref[s:s + 1, hs:2 * hs]
        cr, ci = er + pw_r * cr - pw_i * ci, ei + pw_r * ci + pw_i * cr
    return cr, ci


def _s5_states(u_perm_b16, bd_ref, x_ref, base, c_ref, e_ref, p_ref, lr, li, h_in, t_len):
    hs = TILE_STATES
    seg = t_len // 8
    x_ref[pl.ds(base, t_len), :] = jnp.dot(u_perm_b16, bd_ref[0], preferred_element_type=F32)
    ends = _s5_local_scan(x_ref, base, lr, li, seg, False)
    pw_r, pw_i = p_ref[seg - 1:seg, 0:hs], p_ref[seg - 1:seg, hs:2 * hs]
    h_out = _s5_carries(c_ref, e_ref, ends, h_in, pw_r, pw_i, False)
    cr, ci = c_ref[:, 0:hs], c_ref[:, hs:2 * hs]

    def fix(k, carry):
        rows = pl.ds(pl.multiple_of(base + 8 * k, 8), 8)
        pr, pi = p_ref[pl.ds(k, 1), 0:hs], p_ref[pl.ds(k, 1), hs:2 * hs]
        x_ref[rows, 0:hs] += pr * cr - pi * ci
        x_ref[rows, hs:2 * hs] += pr * ci + pi * cr
        return carry

    lax.fori_loop(0, seg, fix, 0, unroll=2)
    return h_out


def _s5_fwd(proj, u_off, bd, cbd, lam, dvec, t_len):
    l = proj.shape[0]
    nj = bd.shape[0]
    nch = l // t_len
    hs = TILE_STATES
    ub = u_off // LANES
    seg = t_len // 8
    assert t_len % 64 == 0

    def body(u_ref, bd_ref, cbd_ref, lam_ref, d_ref, y_ref, hst_ref, x_ref, h_ref, p_ref, c_ref, e_ref, up_ref, yp_ref):
        lr, li = lam_ref[0, 0:1, :], lam_ref[0, 1:2, :]

        @pl.when(pl.program_id(1) == 0)
        def _():
            h_ref[...] = jnp.zeros_like(h_ref)
            _s5_powers(p_ref, lr, li, seg)

        hst_ref[0, 0] = h_ref[...]
        _s5_permute(u_ref, up_ref, t_len)
        h_out = _s5_states(up_ref[...].astype(BF16), bd_ref, x_ref, 0, c_ref, e_ref, p_ref, lr, li,
                           (h_ref[:, 0:hs], h_ref[:, hs:2 * hs]), t_len)
        h_ref[:, 0:hs] = h_out[0]
        h_ref[:, hs:2 * hs] = h_out[1]
        yp_ref[...] = jnp.dot(x_ref[...].astype(BF16), cbd_ref[0], preferred_element_type=F32)
        dv = d_ref[0]

        def out(r0, rows):
            y_ref[r0:r0 + 8, :] = rows + dv * u_ref[r0:r0 + 8, :]

        _s5_unpermute(yp_ref, t_len, out)

    return pl.pallas_call(
        body, grid=(nj, nch),
        in_specs=[pl.BlockSpec((t_len, LANES), lambda j, c: (c, ub + j)),
                  pl.BlockSpec((1, LANES, 2 * hs), lambda j, c: (j, 0, 0)),
                  pl.BlockSpec((1, 2 * hs, LANES), lambda j, c: (j, 0, 0)),
                  pl.BlockSpec((1, 2, hs), lambda j, c: (j, 0, 0)),
                  pl.BlockSpec((1, 1, LANES), lambda j, c: (j, 0, 0))],
        out_specs=[pl.BlockSpec((t_len, LANES), lambda j, c: (c, j)),
                   pl.BlockSpec((1, 1, 1, 2 * hs), lambda j, c: (j, c, 0, 0))],
        out_shape=[jax.ShapeDtypeStruct((l, nj * LANES), F32),
                   jax.ShapeDtypeStruct((nj, nch, 1, 2 * hs), F32)],
        scratch_shapes=[pltpu.VMEM((t_len, 2 * hs), F32), pltpu.VMEM((1, 2 * hs), F32),
                        pltpu.VMEM((seg, 2 * hs), F32), pltpu.VMEM((8, 2 * hs), F32), pltpu.VMEM((8, 2 * hs), F32),
                        pltpu.VMEM((t_len, LANES), F32), pltpu.VMEM((t_len, LANES), F32)],
        name="s5_fwd", compiler_params=_params(2),
    )(proj, bd, cbd, lam, dvec)


def _s5_bwd(proj, u_off, dy, hst, bd, bdt, cbdt, lam, dvec, t_len):
    l = proj.shape[0]
    nj = bd.shape[0]
    nch = l // t_len
    hs = TILE_STATES
    ub = u_off // LANES
    seg = t_len // 8
    tn = (((0,), (0,)), ((), ()))
    assert t_len % 64 == 0

    def body(u_ref, dy_ref, hst_ref, bd_ref, bdt_ref, cbdt_ref, lam_ref, d_ref,
             du_ref, dbd_ref, dcbdt_ref, dlam_ref, dd_ref,
             x_ref, g_ref, gc_ref, p_ref, c_ref, e_ref, up_ref, dyp_ref, dup_ref):
        first = pl.program_id(1) == 0
        lr, li = lam_ref[0, 0:1, :], lam_ref[0, 1:2, :]

        @pl.when(first)
        def _():
            gc_ref[...] = jnp.zeros_like(gc_ref)
            _s5_powers(p_ref, lr, li, seg)

        _s5_permute(u_ref, up_ref, t_len)
        _s5_permute(dy_ref, dyp_ref, t_len)
        ub16, dyb16 = up_ref[...].astype(BF16), dyp_ref[...].astype(BF16)
        h0 = hst_ref[0, 0]
        _s5_states(ub16, bd_ref, x_ref, 8, c_ref, e_ref, p_ref, lr, li, (h0[:, 0:hs], h0[:, hs:2 * hs]), t_len)
        x_ref[0:8, :] = c_ref[...]
        g_ref[...] = jnp.dot(dyb16, cbdt_ref[0], preferred_element_type=F32)
        starts = _s5_local_scan(g_ref, 0, lr, li, seg, True)
        pw_r, pw_i = p_ref[seg - 1:seg, 0:hs], p_ref[seg - 1:seg, hs:2 * hs]
        g_out = _s5_carries(c_ref, e_ref, starts, (gc_ref[:, 0:hs], gc_ref[:, hs:2 * hs]), pw_r, pw_i, True)
        gc_ref[:, 0:hs] = g_out[0]
        gc_ref[:, hs:2 * hs] = g_out[1]
        cr, ci = c_ref[:, 0:hs], c_ref[:, hs:2 * hs]

        def fix(k, carry):
            alr, ali = carry
            rows = pl.ds(pl.multiple_of(8 * k, 8), 8)
            pr, pi = p_ref[pl.ds(seg - 1 - k, 1), 0:hs], p_ref[pl.ds(seg - 1 - k, 1), hs:2 * hs]
            gr = g_ref[rows, 0:hs] + pr * cr + pi * ci
            gi = g_ref[rows, hs:2 * hs] + pr * ci - pi * cr
            g_ref[rows, 0:hs] = gr
            g_ref[rows, hs:2 * hs] = gi
            hpr, hpi = x_ref[rows, 0:hs], x_ref[rows, hs:2 * hs]
            return alr + gr * hpr + gi * hpi, ali + gi * hpr - gr * hpi

        zero = jnp.zeros((8, hs), F32)
        alr, ali = lax.fori_loop(0, seg, fix, (zero, zero), unroll=2)
        alr, ali = _colsum(alr), _colsum(ali)
        g = g_ref[...].astype(BF16)
        h = x_ref[pl.ds(8, t_len), :].astype(BF16)
        dup_ref[...] = jnp.dot(g, bdt_ref[0], preferred_element_type=F32)
        dv = d_ref[0]

        def out(r0, rows):
            du_ref[r0:r0 + 8, :] = (rows + dv * dy_ref[r0:r0 + 8, :]).astype(du_ref.dtype)

        _s5_unpermute(dup_ref, t_len, out)
        sign = jnp.where(lax.broadcasted_iota(jnp.int32, (1, 2 * hs), 1) < hs, 1.0, -1.0)
        dbd = lax.dot_general(ub16, g, tn, preferred_element_type=F32)
        dcbdt = lax.dot_general(dyb16, h, tn, preferred_element_type=F32) * sign
        ddv = _colsum(dy_ref[...] * u_ref[...])

        @pl.when(first)
        def _():
            dbd_ref[0] = dbd
            dcbdt_ref[0] = dcbdt
            dlam_ref[0, 0:1, :] = alr
            dlam_ref[0, 1:2, :] = ali
            dd_ref[0] = ddv

        @pl.when(jnp.logical_not(first))
        def _():
            dbd_ref[0] += dbd
            dcbdt_ref[0] += dcbdt
            dlam_ref[0, 0:1, :] += alr
            dlam_ref[0, 1:2, :] += ali
            dd_ref[0] += ddv

    rev = lambda c: nch - 1 - c
    wide = pl.BlockSpec((1, LANES, 2 * hs), lambda j, c: (j, 0, 0))
    tall = pl.BlockSpec((1, 2 * hs, LANES), lambda j, c: (j, 0, 0))
    return pl.pallas_call(
        body, grid=(nj, nch),
        in_specs=[pl.BlockSpec((t_len, LANES), lambda j, c: (rev(c), ub + j)),
                  pl.BlockSpec((t_len, LANES), lambda j, c: (rev(c), j)),
                  pl.BlockSpec((1, 1, 1, 2 * hs), lambda j, c: (j, rev(c), 0, 0)),
                  wide, tall, wide,
                  pl.BlockSpec((1, 2, hs), lambda j, c: (j, 0, 0)),
                  pl.BlockSpec((1, 1, LANES), lambda j, c: (j, 0, 0))],
        out_specs=[pl.BlockSpec((t_len, LANES), lambda j, c: (rev(c), j)),
                   wide, wide,
                   pl.BlockSpec((1, 2, hs), lambda j, c: (j, 0, 0)),
                   pl.BlockSpec((1, 1, LANES), lambda j, c: (j, 0, 0))],
        out_shape=[jax.ShapeDtypeStruct((l, nj * LANES), BF16),
                   jax.ShapeDtypeStruct((nj, LANES, 2 * hs), F32),
                   jax.ShapeDtypeStruct((nj, LANES, 2 * hs), F32),
                   jax.ShapeDtypeStruct((nj, 2, hs), F32),
                   jax.ShapeDtypeStruct((nj, 1, LANES), F32)],
        scratch_shapes=[pltpu.VMEM((t_len + 8, 2 * hs), F32), pltpu.VMEM((t_len, 2 * hs), F32),
                        pltpu.VMEM((1, 2 * hs), F32), pltpu.VMEM((seg, 2 * hs), F32),
                        pltpu.VMEM((8, 2 * hs), F32), pltpu.VMEM((8, 2 * hs), F32),
                        pltpu.VMEM((t_len, LANES), F32), pltpu.VMEM((t_len, LANES), F32),
                        pltpu.VMEM((t_len, LANES), F32)],
        name="s5_bwd", compiler_params=_params(2),
    )(proj, dy, hst, bd, bdt, cbdt, lam, dvec)


def _full_spec(shape):
    nd = len(shape)
    return pl.BlockSpec(tuple(shape), lambda i: (0,) * nd)


def _sds(shape, dtype=F32):
    return jax.ShapeDtypeStruct(tuple(shape), dtype)


def kernel(x, c, ada_w, ada_b, norm_mix_g, w_in, attn_sinks, w_attn_proj, ssm_a_re, ssm_a_im, ssm_log_dt, ssm_b_re, ssm_b_im, ssm_c_re, ssm_c_im, ssm_d, w_ssm_glu, w_out, norm_ffn_g, w_ffn_up, ffn_conv_w, ffn_conv_b, w_ffn_down, final_g, loss_target, m_ada_w, m_ada_b, m_norm_mix_g, m_w_in, m_attn_sinks, m_w_attn_proj, m_ssm_a_re, m_ssm_a_im, m_ssm_log_dt, m_ssm_b_re, m_ssm_b_im, m_ssm_c_re, m_ssm_c_im, m_ssm_d, m_w_ssm_glu, m_w_out, m_norm_ffn_g, m_w_ffn_up, m_ffn_conv_w, m_ffn_conv_b, m_w_ffn_down, m_final_g, v_ada_w, v_ada_b, v_norm_mix_g, v_w_in, v_attn_sinks, v_w_attn_proj, v_ssm_a_re, v_ssm_a_im, v_ssm_log_dt, v_ssm_b_re, v_ssm_b_im, v_ssm_c_re, v_ssm_c_im, v_ssm_d, v_w_ssm_glu, v_w_out, v_norm_ffn_g, v_w_ffn_up, v_ffn_conv_w, v_ffn_conv_b, v_w_ffn_down, v_final_g):
    given = dict(locals())
    names = ['ada_w', 'ada_b', 'norm_mix_g', 'w_in', 'attn_sinks', 'w_attn_proj', 'ssm_a_re', 'ssm_a_im',
             'ssm_log_dt', 'ssm_b_re', 'ssm_b_im', 'ssm_c_re', 'ssm_c_im', 'ssm_d', 'w_ssm_glu', 'w_out',
             'norm_ffn_g', 'w_ffn_up', 'ffn_conv_w', 'ffn_conv_b', 'w_ffn_down', 'final_g']

    xs = x[0]
    tgt = loss_target[0]
    l, d = xs.shape
    attn_w = w_attn_proj.shape[1]
    ssm_w = w_ssm_glu.shape[1]
    hq = attn_sinks.shape[1]
    qpk = hq // N_KV_HEADS
    kv_w = N_KV_HEADS * HEAD_DIM
    n_groups = ssm_a_re.shape[1]
    dff = ffn_conv_b.shape[1]
    in_w = attn_w + 2 * kv_w + ssm_w + 2 * d
    nj = ssm_w // LANES
    off_k, off_v, off_u = attn_w, attn_w + kv_w, attn_w + 2 * kv_w
    off_ga, off_gs = off_u + ssm_w, off_u + ssm_w + d
    assert hq * HEAD_DIM == attn_w and n_groups * SSM_P == ssm_w and l % ATT_BLOCK == 0

    xi, yi, ci = _dev()
    idx = 4 * xi + 2 * yi + ci

    row_sharded = {'w_out': (d, d), 'w_ffn_down': (dff, d)}
    big = ['w_in', 'w_attn_proj', 'w_ssm_glu', 'w_out', 'w_ffn_up', 'w_ffn_down']
    spack, s_offs = _pack([c, ffn_conv_w[0]], LANES, 8)
    w16 = {k: given[k][0].astype(BF16) for k in big}
    wg_in, sg = _all_gather("gather_first", [w16['w_in'], spack])
    mixer_w = ['w_attn_proj', 'w_ssm_glu', 'w_out']
    h_mixer, tok = _exchange_start("gather_mixer_start", [w16[k] for k in mixer_w], True, wg_in)
    h_up, tok = _exchange_start("gather_ffn_up_start", [w16['w_ffn_up']], True, tok)
    h_down, tok = _exchange_start("gather_ffn_down_start", [w16['w_ffn_down']], True, tok)
    full = {'w_in': wg_in.transpose(1, 0, 2).reshape(d, in_w)}
    c_all = _unpack(sg, s_offs[0], (d,), lead=(N_DEV,))
    conv_w = _unpack(sg, s_offs[1], ffn_conv_w.shape[1:], lead=(N_DEV,)).transpose(1, 0, 2).reshape(3, dff)
    conv_b = ffn_conv_b

    mod_n = ada_w.shape[2]
    tcm = _pick(mod_n, 512)
    ada_b_mine = lax.dynamic_slice_in_dim(ada_b, idx * mod_n, mod_n, axis=1)

    def modpart_fn(cv, wv, bv):
        cond = cv * jax.nn.sigmoid(cv)
        return jnp.dot(cond.astype(BF16), wv.astype(BF16), preferred_element_type=F32) + bv, cond

    modp, cond_all = _tile_call(
        "ada_rows", modpart_fn, (mod_n // tcm,), [c_all, ada_w[0], ada_b_mine],
        [pl.BlockSpec((N_DEV, d), lambda j: (0, 0)), pl.BlockSpec((d, tcm), lambda j: (0, j)),
         pl.BlockSpec((1, tcm), lambda j: (0, j))],
        [_sds((N_DEV, mod_n)), _sds((N_DEV, d))],
        [pl.BlockSpec((N_DEV, tcm), lambda j: (0, j)), pl.BlockSpec((N_DEV, d), lambda j: (0, 0))])
    (modg,) = _all_gather("gather_ada_rows", [modp])
    mod = lax.dynamic_index_in_dim(modg, idx, axis=1, keepdims=False).reshape(1, N_DEV * mod_n)
    sh1, sc1, g1, sh2, sc2, g2 = [mod[:, i * d:(i + 1) * d] for i in range(6)]

    tr = _pick(l, 256, 8)
    trh = _pick(l, 128, 8)
    nr, nrh = l // tr, l // trh
    g_mix, g_ffn, g_fin = norm_mix_g + tok[0:1, 0:1], norm_ffn_g, final_g.reshape(1, d)

    def with_t(fn):
        def wrapped(*vals):
            out = fn(*vals)
            out = out if isinstance(out, tuple) else (out,)
            return out + (out[-1].T,)
        return wrapped

    h1, h1_t = _tile_call("norm_mod_mix", with_t(_norm_mod), (1, nr), [xs, g_mix, sc1, sh1],
                          [_t(tr, d), _v(d), _v(d), _v(d)], [_sds((l, d), BF16), _sds((d, l), BF16)],
                          [_t(tr, d), _tt(tr, d)])
    proj = _matmul("proj_in", h1, full['w_in'], "nn", tn=1280)

    def heads(z, n):
        return z.reshape(l, n, HEAD_DIM).transpose(1, 0, 2)

    qh = heads(proj[:, :attn_w], hq)
    kh = heads(proj[:, off_k:off_k + kv_w], N_KV_HEADS)
    vh = heads(proj[:, off_v:off_v + kv_w], N_KV_HEADS)
    sinks3 = jnp.repeat(attn_sinks.reshape(N_KV_HEADS, qpk), ATT_BLOCK, axis=1)[..., None]
    o_h = _attn_fwd(qh, kh, vh, sinks3)
    o2 = o_h.transpose(1, 0, 2).reshape(l, attn_w)

    gn = (n_groups, SSM_N)
    pgn = (SSM_P, n_groups, SSM_N)
    a_re, a_im, log_dt = ssm_a_re[0], ssm_a_im[0], ssm_log_dt[0].reshape(n_groups, 1)
    b_re, b_im = ssm_b_re[0].transpose(2, 0, 1), ssm_b_im[0].transpose(2, 0, 1)
    disc_ins = [a_re, a_im, log_dt, b_re, b_im]
    disc_specs = [_full_spec(gn), _full_spec(gn), _full_spec((n_groups, 1)), _full_spec(pgn), _full_spec(pgn)]
    lam_r, lam_i, bb_r, bb_i = _tile_call(
        "s5_discretise", _s5_disc_fn, (1,), disc_ins, disc_specs,
        [_sds(gn), _sds(gn), _sds(pgn), _sds(pgn)],
        [_full_spec(gn), _full_spec(gn), _full_spec(pgn), _full_spec(pgn)])

    def tiles_gpn(z):
        return z.reshape(SSM_P, nj, TILE_GROUPS, SSM_N).transpose(1, 2, 0, 3)

    bd = jnp.concatenate([_block_diag(tiles_gpn(bb_r)), _block_diag(tiles_gpn(bb_i))], axis=2).astype(BF16)
    c_r = ssm_c_re[0].reshape(nj, TILE_GROUPS, SSM_P, SSM_N).transpose(0, 1, 3, 2)
    c_i = (-ssm_c_im[0]).reshape(nj, TILE_GROUPS, SSM_P, SSM_N).transpose(0, 1, 3, 2)
    cbd = jnp.concatenate([_block_diag(c_r), _block_diag(c_i)], axis=1).astype(BF16)
    bdt, cbdt = bd.transpose(0, 2, 1), cbd.transpose(0, 2, 1)
    lam = jnp.stack([lam_r.reshape(nj, TILE_STATES), lam_i.reshape(nj, TILE_STATES)], axis=1)
    dvec = ssm_d[0].reshape(nj, 1, LANES)
    t_len = _pick(l, 512, 8)
    y, hst = _s5_fwd(proj, off_u, bd, cbd, lam, dvec, t_len)

    tcs, trg = _pick(ssm_w, 1024), _pick(l, 512, 8)
    gy = _tile_call("gelu", lambda v: jax.nn.gelu(v), (ssm_w // tcs, l // trg), [y], [_t(trg, tcs)],
                    [_sds((l, ssm_w), BF16)], [_t(trg, tcs)])[0]
    full.update(zip(mixer_w, _exchange_wait("gather_mixer_wait", h_mixer, gy)))
    full['w_out'] = full['w_out'].reshape(row_sharded['w_out'])
    full['w_attn_proj'] = full['w_attn_proj'].transpose(1, 0, 2).reshape(attn_w, d)
    full['w_ssm_glu'] = full['w_ssm_glu'].transpose(1, 0, 2).reshape(ssm_w, 2 * d)
    attn = _matmul("attn_proj", o2, full['w_attn_proj'], "nn")
    glu = _matmul("ssm_glu", gy, full['w_ssm_glu'], "nn")

    tcd = 256 if d % 256 == 0 and off_ga % 256 == 0 else LANES
    assert d % tcd == 0 and off_ga % tcd == 0 and off_gs % tcd == 0
    trm = _pick(l, 1024, 8)
    mix_in_specs = [_t(trm, tcd), _t(trm, tcd, d), _t(trm, tcd), _t(trm, tcd, off_ga), _t(trm, tcd, off_gs)]
    mixed = _tile_call("gate_mix", _mix_fn, (d // tcd, l // trm), [glu, glu, attn, proj, proj], mix_in_specs,
                       [_sds((l, d), BF16)], [_t(trm, tcd)])[0]
    def res_norm_fn(xv, mo, g1v, gv, scv, shv):
        x2v = xv + g1v * mo
        return x2v, _norm_mod(x2v, gv, scv, shv)

    def res_norm_epilogue(mo, xv, g1v, gv, scv, shv):
        x2v, h2v = res_norm_fn(xv, mo, g1v, gv, scv, shv)
        return mo, x2v, h2v, h2v.T

    mixout, x2, h2, h2_t = _matmul(
        "mix_out_residual_norm_mod_ffn", mixed, full['w_out'], "nn", tm=256, tn=d,
        epilogue=(res_norm_epilogue, [xs, g1, g_ffn, sc2, sh2], [(F32, False), (F32, False), (BF16, False), (BF16, True)]))
    full['w_ffn_up'], = _exchange_wait("gather_ffn_up_wait", h_up, h2)
    up = _matmul("ffn_up", h2, full['w_ffn_up'], "nn", out_dtype=BF16, tn=1408)

    tcf, trc = _pick(dff, 1408), _pick(l, 512, 8)
    assert dff % tcf == 0
    ncf = dff // tcf

    taps = [conv_w[i:i + 1] for i in range(3)]

    def conv_gate(gp, gp_prev, w0, w1, w2, bv):
        gp = gp.astype(F32)
        prev = jnp.where(pl.program_id(1) == 0, 0.0, 1.0) * gp_prev.astype(F32)
        ext = jnp.concatenate([prev, gp], axis=0)
        m1 = pltpu.roll(ext, 1, 0)[HALO:]
        m2 = pltpu.roll(ext, 2, 0)[HALO:]
        return w0 * m2 + w1 * m1 + w2 * gp + bv, m1, m2

    def convglu_fn(gp, gp_prev, val, w0, w1, w2, bv):
        gate, _, _ = conv_gate(gp, gp_prev, w0, w1, w2, bv)
        return gate * jax.nn.sigmoid(gate) * val.astype(F32)

    act, act_t = _tile_call("conv_swiglu", with_t(convglu_fn), (ncf, l // trc), [up, up, up] + taps + [conv_b],
                            [_t(trc, tcf), _prev_rows(trc, tcf), _t(trc, tcf, dff)] + [_v(tcf)] * 4,
                            [_sds((l, dff), BF16), _sds((dff, l), BF16)], [_t(trc, tcf), _tt(trc, tcf)])
    full['w_ffn_down'] = _exchange_wait("gather_ffn_down_wait", h_down, act)[0].reshape(row_sharded['w_ffn_down'])
    ffn = _matmul("ffn_down", act, full['w_ffn_down'], "nn", tm=512)

    def final_fn(x2v, fv, g2v, gv, tv):
        rows = x2v.shape[0]

        def loss_of(x2a, fa, g2a, ga):
            out = _rms(x2a + g2a * fa, ga)
            err = out - tv
            return 0.5 * _colsum(jnp.mean(err * err, axis=-1, keepdims=True))

        loss, vjp = jax.vjp(loss_of, x2v, fv, _bc(g2v, rows), _bc(gv, rows))
        dx3, dffn, dg2, dgf = vjp(jnp.ones((1, 1), F32))
        return jnp.broadcast_to(loss, (1, LANES)), dx3, dffn, _colsum(dg2), _colsum(dgf)

    loss_p, dx3, dffn, dg2, dg_fin = _tile_call(
        "loss_final_norm", final_fn, (1, nrh), [x2, ffn, g2, g_fin, tgt],
        [_t(trh, d), _t(trh, d), _v(d), _v(d), _t(trh, d)],
        [_sds((1, LANES)), _sds((l, d)), _sds((l, d), BF16), _sds((1, d)), _sds((1, d))],
        [_v(LANES), _t(trh, d), _t(trh, d), _v(d), _v(d)], acc=(0, 3, 4))
    loss = lax.psum(loss_p[0, 0], ("x", "y", "c"))

    dact = _matmul("d_act", dffn, full['w_ffn_down'], "nt", out_dtype=BF16, tn=1408, dep=loss.reshape(1, 1))
    gd, gd16, pending = {}, {}, []
    dw_down, dw_down16 = _matmul("dw_ffn_down", act_t, dffn, "nn", tm=512, also_bf16=True)
    gd['w_ffn_down'], gd16['w_ffn_down'] = [z.reshape((N_DEV,) + w_ffn_down.shape[1:]) for z in (dw_down, dw_down16)]
    handle, tok = _exchange_start("grad_ffn_down_start", [gd16['w_ffn_down']], False, loss.reshape(1, 1))
    pending.append((['w_ffn_down'], handle))
    conv_b_bwd = conv_b + tok[0:1, 0:1]

    def convglu_bwd_fn(gp, gp_prev, gp_next, val, val_next, da, da_next, w0, w1, w2, bv):
        rows = gp.shape[0]
        i = pl.program_id(1)
        gp, val, da = gp.astype(F32), val.astype(F32), da.astype(F32)
        prev = jnp.where(i == 0, 0.0, 1.0) * gp_prev.astype(F32)
        more = jnp.where(i == pl.num_programs(1) - 1, 0.0, 1.0)
        ext = jnp.concatenate([prev, gp, gp_next.astype(F32)], axis=0)
        cur = ext[HALO:]
        m1 = pltpu.roll(ext, 1, 0)[HALO:]
        m2 = pltpu.roll(ext, 2, 0)[HALO:]
        gate = w0 * m2 + w1 * m1 + w2 * cur + bv
        sg = jax.nn.sigmoid(gate)
        val_e = jnp.concatenate([val, val_next.astype(F32)], axis=0)
        da_e = jnp.concatenate([da, more * da_next.astype(F32)], axis=0)
        dgate = da_e * val_e * (sg * (1.0 + gate * (1.0 - sg)))
        p1 = pltpu.roll(dgate, rows + HALO - 1, 0)[:rows]
        p2 = pltpu.roll(dgate, rows + HALO - 2, 0)[:rows]
        dg = dgate[:rows]
        dgp = w2 * dg + w1 * p1 + w0 * p2
        dval = da * (gate[:rows] * sg[:rows])
        return (jnp.stack([dgp, dval], axis=0), _colsum(dg), _colsum(dg * m2[:rows]), _colsum(dg * m1[:rows]),
                _colsum(dg * gp))

    dup, dconv_b, dcw0, dcw1, dcw2 = _tile_call(
        "conv_swiglu_bwd", convglu_bwd_fn, (ncf, nr), [up, up, up, up, up, dact, dact] + taps + [conv_b_bwd],
        [_t(tr, tcf), _prev_rows(tr, tcf), _next_rows(tr, tcf, l), _t(tr, tcf, dff), _next_rows(tr, tcf, l, dff),
         _t(tr, tcf), _next_rows(tr, tcf, l)] + [_v(tcf)] * 4,
        [_sds((2, l, dff), BF16)] + [_sds((1, dff))] * 4, [_st(tr, tcf)] + [_v(tcf)] * 4, acc=(1, 2, 3, 4))
    dh2 = _matmul("d_h2", dup, full['w_ffn_up'], "nt", tm=512, fold=4)
    gd['w_ffn_up'], gd16['w_ffn_up'] = _matmul("dw_ffn_up", h2_t, dup, "nn", tm=512, tn=1408, out_stack=N_DEV, also_bf16=True)
    handle, tok = _exchange_start("grad_ffn_up_start", [gd16['w_ffn_up']], False, gd['w_ffn_up'])
    pending.append((['w_ffn_up'], handle))
    g_ffn_bwd = g_ffn + tok[0:1, 0:1]

    def res_norm_bwd_fn(xv, mo, g1v, gv, scv, shv, dhv, dxv):
        rows = xv.shape[0]
        _, vjp = jax.vjp(res_norm_fn, xv, mo, _bc(g1v, rows), _bc(gv, rows), _bc(scv, rows), _bc(shv, rows))
        dx, dmo, dg1v, dgv, dscv, dshv = vjp((dxv, dhv))
        return dx, dmo, _colsum(dg1v), _colsum(dgv), _colsum(dscv), _colsum(dshv)

    dx2, dmixout, dg1, dg_ffn, dsc2, dsh2 = _tile_call(
        "residual_norm_mod_ffn_bwd", res_norm_bwd_fn, (1, nrh), [xs, mixout, g1, g_ffn_bwd, sc2, sh2, dh2, dx3],
        [_t(trh, d), _t(trh, d), _v(d), _v(d), _v(d), _v(d), _t(trh, d), _t(trh, d)],
        [_sds((l, d)), _sds((l, d), BF16)] + [_sds((1, d))] * 4,
        [_t(trh, d), _t(trh, d)] + [_v(d)] * 4, acc=(2, 3, 4, 5))

    dmixed = _matmul("d_mixed", dmixout, full['w_out'], "nt")
    dw_out, dw_out16 = _matmul("dw_out", mixed, dmixout, "tn", also_bf16=True)
    gd['w_out'], gd16['w_out'] = [z.reshape((N_DEV,) + w_out.shape[1:]) for z in (dw_out, dw_out16)]

    def mix_bwd_fn(ga_, gb_, at, pa, ps, dm):
        _, vjp = jax.vjp(_mix_fn, ga_, gb_, at, pa, ps)
        da, db, dat, dpa, dps = vjp(dm)
        return jnp.stack([da, db], axis=0), dat, dpa, dps

    dglu, dattn, dga, dgs = _tile_call(
        "gate_mix_bwd", mix_bwd_fn, (d // tcd, l // trm), [glu, glu, attn, proj, proj, dmixed],
        mix_in_specs + [_t(trm, tcd)],
        [_sds((2, l, d), BF16)] + [_sds((l, d), BF16)] * 3, [_st(trm, tcd)] + [_t(trm, tcd)] * 3)

    def gelu_bwd_epilogue(dgy, yv):
        _, vjp = jax.vjp(lambda z: jax.nn.gelu(z), yv)
        return (vjp(dgy)[0],)

    dy, = _matmul("d_gelu_y_gelu_bwd", dglu, full['w_ssm_glu'], "nt", epilogue=(gelu_bwd_epilogue, [y], [(F32, False)]))
    gd['w_ssm_glu'], gd16['w_ssm_glu'] = _matmul("dw_ssm_glu", gy, dglu, "tn", out_stack=N_DEV, also_bf16=True)
    du, dbd, dcbdt, dlam, dd_tiles = _s5_bwd(proj, off_u, dy, hst, bd, bdt, cbdt, lam, dvec, t_len)

    def gpn_of(z):
        return z.transpose(2, 0, 1, 3).reshape(pgn)

    dbb_r = gpn_of(_diag_blocks(dbd[:, :, :TILE_STATES], SSM_P))
    dbb_i = gpn_of(_diag_blocks(dbd[:, :, TILE_STATES:], SSM_P))
    dc_re = _diag_blocks(dcbdt[:, :, :TILE_STATES], SSM_P).reshape(n_groups, SSM_P, SSM_N)
    dc_im = _diag_blocks(dcbdt[:, :, TILE_STATES:], SSM_P).reshape(n_groups, SSM_P, SSM_N)
    dlam_r, dlam_i = dlam[:, 0].reshape(gn), dlam[:, 1].reshape(gn)

    def disc_bwd_fn(ar, ai, ld, br, bi, dlr, dli, dbr, dbi):
        _, vjp = jax.vjp(_s5_disc_fn, ar, ai, ld, br, bi)
        return vjp((dlr, dli, dbr, dbi))

    da_re, da_im, dlog_dt, db_re, db_im = _tile_call(
        "s5_discretise_bwd", disc_bwd_fn, (1,), disc_ins + [dlam_r, dlam_i, dbb_r, dbb_i],
        disc_specs + [_full_spec(gn), _full_spec(gn), _full_spec(pgn), _full_spec(pgn)],
        [_sds(gn), _sds(gn), _sds((n_groups, 1)), _sds(pgn), _sds(pgn)], disc_specs)

    do2 = _matmul("d_attn_heads", dattn, full['w_attn_proj'], "nt")
    gd['w_attn_proj'], gd16['w_attn_proj'] = _matmul("dw_attn_proj", o2, dattn, "tn", out_stack=N_DEV, also_bf16=True)
    handle, tok = _exchange_start("grad_mixer_start", [gd16[k] for k in mixer_w], False, gd['w_attn_proj'])
    pending.append((mixer_w, handle))
    do_h = heads(do2.astype(BF16), hq)
    dq_h, dk_h, dv_h, dsink = _attn_bwd(qh, kh, vh, sinks3 + tok[0:1, 0:1], do_h)

    def unheads(z):
        return z.transpose(1, 0, 2).reshape(l, z.shape[0] * HEAD_DIM)

    early = ['attn_sinks', 'ssm_a_re', 'ssm_a_im', 'ssm_log_dt', 'ssm_b_re', 'ssm_b_im', 'ssm_c_re', 'ssm_c_im',
             'ssm_d', 'norm_ffn_g', 'ffn_conv_b', 'final_g']
    early_grads = {
        'attn_sinks': dsink[:, 0, 0], 'ssm_a_re': da_re, 'ssm_a_im': da_im, 'ssm_log_dt': dlog_dt,
        'ssm_b_re': db_re.transpose(1, 2, 0), 'ssm_b_im': db_im.transpose(1, 2, 0), 'ssm_c_re': dc_re,
        'ssm_c_im': dc_im, 'ssm_d': dd_tiles, 'norm_ffn_g': dg_ffn, 'ffn_conv_b': dconv_b, 'final_g': dg_fin}
    ge_pack, e_offs = _pack([jnp.concatenate([dg1, dsh2, dsc2, dg2], axis=1)] + [early_grads[k] for k in early],
                            LANES, 8)
    h_early, tok = _exchange_start("gather_small_early_start", [ge_pack], True, dsink)

    dproj = jnp.concatenate([unheads(dq_h), unheads(dk_h), unheads(dv_h), du, dga, dgs], axis=1)
    dw_in, dw_in16 = _matmul("dw_in", h1_t, dproj, "nn", tm=512, tn=1280, also_bf16=True, dep=tok)
    dcw = jnp.concatenate([dcw0, dcw1, dcw2], axis=0)
    shard_in, shard_cw = w_in.shape[1:], ffn_conv_w.shape[1:]
    gd16['w_in'] = dw_in16.reshape(shard_in[0], N_DEV, shard_in[1]).transpose(1, 0, 2)
    own_in = lax.dynamic_slice_in_dim(dw_in, idx * shard_in[1], shard_in[1], axis=1)[None]
    gd['ffn_conv_w'] = dcw.reshape(shard_cw[0], N_DEV, shard_cw[1]).transpose(1, 0, 2)
    gd16['ffn_conv_w'] = gd['ffn_conv_w'].astype(BF16)
    handle, tok = _exchange_start("grad_in_start", [gd16['w_in'], gd16['ffn_conv_w']], False, dw_in)
    pending.append((['w_in', 'ffn_conv_w'], handle))
    dh1 = _matmul("d_h1", dproj, full['w_in'], "nt", tm=512, dep=tok)

    def norm_bwd_fn(xv, gv, scv, shv, dhv, dxv):
        rows = xv.shape[0]
        _, vjp = jax.vjp(_norm_mod, xv, _bc(gv, rows), _bc(scv, rows), _bc(shv, rows))
        dx, dgv, dscv, dshv = vjp(dhv)
        return dx + dxv, _colsum(dgv), _colsum(dscv), _colsum(dshv)

    grad_x, dg_mix, dsc1, dsh1 = _tile_call(
        "norm_mod_mix_bwd", norm_bwd_fn, (1, nrh), [xs, g_mix, sc1, sh1, dh1, dx2],
        [_t(trh, d), _v(d), _v(d), _v(d), _t(trh, d), _t(trh, d)],
        [_sds((l, d))] + [_sds((1, d))] * 3, [_t(trh, d)] + [_v(d)] * 3, acc=(1, 2, 3))

    gl_pack, l_offs = _pack([jnp.concatenate([dsh1, dsc1], axis=1), dg_mix], LANES, 8)
    h_late, tok = _exchange_start("gather_small_late_start", [gl_pack], True, grad_x)

    sharded = big + ['ffn_conv_w']
    sharded_out = {}

    def finish(group, handle, after):
        for k, parts in zip(group, _exchange_wait("grad_" + group[0] + "_wait", handle, after)):
            own_src, own_at = (own_in, 0 * idx) if k == 'w_in' else (gd[k], idx)
            sharded_out[k] = _adamw_sharded("adamw_" + k, parts, own_src, given[k][0], given['m_' + k][0],
                                            given['v_' + k][0], jnp.stack([idx, own_at]).astype(jnp.int32))

    for group, handle in pending[:-1]:
        finish(group, handle, tok)
    done = functools.reduce(lambda p, q: p + q, [sharded_out[k][1][0:1, 0:1] for g_, _ in pending[:-1] for k in g_])
    finish(*pending[-1], done)

    ge_all, = _exchange_wait("gather_small_early_wait", h_early, done)
    gl_all, = _exchange_wait("gather_small_late_wait", h_late, sharded_out['w_in'][1])
    gs_all = jnp.concatenate([ge_all, gl_all], axis=1)
    rows_e = ge_pack.shape[0]

    def small_pack(prefix):
        ab = given[prefix + 'ada_b']
        p_early, _ = _pack([ab[:, 2 * d:]] + [given[prefix + k] for k in early], LANES, 8)
        p_late, _ = _pack([ab[:, :2 * d], given[prefix + 'norm_mix_g']], LANES, 8)
        return jnp.concatenate([p_early, p_late], axis=0)

    small_out = _adamw("adamw_replicated", gs_all, small_pack(''), small_pack('m_'), small_pack('v_'))

    dmod_all = jnp.concatenate([_unpack(gl_all, l_offs[0], (2 * d,), lead=(N_DEV,)),
                                _unpack(ge_all, e_offs[0], (4 * d,), lead=(N_DEV,))], axis=1)
    dmod_mine = lax.dynamic_slice_in_dim(dmod_all, idx * mod_n, mod_n, axis=1)
    kpad = LANES - N_DEV
    cond_t = jnp.pad(cond_all.T, ((0, 0), (0, kpad)))
    dmod_pad = jnp.pad(dmod_mine, ((0, kpad), (0, 0)))
    g_ada_w = _matmul("dw_ada", cond_t, dmod_pad, "nn")
    ada_out = _adamw("adamw_ada_w", g_ada_w[None], ada_w[0], m_ada_w[0], v_ada_w[0])

    results = [{}, {}, {}, {}]
    for which in range(4):
        out = small_out[which]
        results[which]['ada_b'] = jnp.concatenate([_unpack(out, rows_e + l_offs[0], (1, 2 * d)),
                                                   _unpack(out, e_offs[0], (1, 4 * d))], axis=1)
        results[which]['norm_mix_g'] = _unpack(out, rows_e + l_offs[1], norm_mix_g.shape)
        for k, off in zip(early, e_offs[1:]):
            results[which][k] = _unpack(out, off, given[k].shape)
        for k in sharded:
            results[which][k] = sharded_out[k][which][None]
        results[which]['ada_w'] = ada_out[which][None]
    outs = [loss, grad_x[None]]
    for which in range(4):
        outs += [results[which][k] for k in names]
    return tuple(outs)
```

```python
import functools
import math

import jax
import jax.numpy as jnp
from jax import lax
from jax.experimental import pallas as pl
from jax.experimental.pallas import tpu as pltpu

F32, BF16 = jnp.float32, jnp.bfloat16
MESH = pl.DeviceIdType.MESH
N_DEV = 8

HEAD_DIM = 64
N_KV_HEADS = 2
ATT_BLOCK = 128
NEG_INF = -1e30
SSM_P = 16
SSM_N = 64
LANES = 128
TILE_GROUPS = LANES // SSM_P
TILE_STATES = TILE_GROUPS * SSM_N
RMS_EPS = 1e-6
ADAM_LR, ADAM_B1, ADAM_B2, ADAM_EPS, ADAM_WD, ADAM_STEP = 0.001, 0.9, 0.999, 1e-08, 0.01, 10
VMEM_LIMIT = 56 * 1024 * 1024
MATMUL_VMEM_BUDGET = 44 * 1024 * 1024


def _params(n_axes):
    return pltpu.CompilerParams(dimension_semantics=("arbitrary",) * n_axes, vmem_limit_bytes=VMEM_LIMIT)


def _pick(dim, pref, align=128):
    if dim <= align:
        return dim
    t = (min(pref, dim) // align) * align
    while t > align and dim % t:
        t -= align
    assert dim % t == 0, (dim, pref, align)
    return t


def _dev():
    return lax.axis_index("x"), lax.axis_index("y"), lax.axis_index("c")


def _tile_call(name, fn, grid, ins, in_specs, out_shapes, out_specs, acc=()):
    n_in, n_out = len(ins), len(out_shapes)
    acc_axis = len(grid) - 1

    def body(*refs):
        vals = fn(*[r[...] for r in refs[:n_in]])
        if not isinstance(vals, (tuple, list)):
            vals = (vals,)
        assert len(vals) == n_out
        for i, (r, v) in enumerate(zip(refs[n_in:], vals)):
            v = v.astype(r.dtype)
            if i in acc:
                first = pl.program_id(acc_axis) == 0

                @pl.when(first)
                def _():
                    r[...] = v

                @pl.when(jnp.logical_not(first))
                def _():
                    r[...] += v
            else:
                r[...] = v

    return pl.pallas_call(
        body, grid=grid, in_specs=in_specs, out_specs=out_specs, out_shape=out_shapes, name=name,
        compiler_params=_params(len(grid)),
    )(*ins)


def _t(tr, tc, off=0):
    return pl.BlockSpec((tr, tc), lambda j, i: (i, j + off // tc))


def _tt(tr, tc):
    return pl.BlockSpec((tc, tr), lambda j, i: (j, i))


def _v(tc, off=0, rows=1):
    return pl.BlockSpec((rows, tc), lambda j, i: (0, j + off // tc))


HALO = 16


def _prev_rows(tr, tc, off=0):
    return pl.BlockSpec((HALO, tc), lambda j, i: (jnp.maximum(i * (tr // HALO) - 1, 0), j + off // tc))


def _next_rows(tr, tc, nrows, off=0):
    return pl.BlockSpec((HALO, tc),
                        lambda j, i: (jnp.minimum((i + 1) * (tr // HALO), nrows // HALO - 1), j + off // tc))


def _st(tr, tc):
    return pl.BlockSpec((2, tr, tc), lambda j, i: (0, i, j))


def _bc(v, rows):
    return jnp.broadcast_to(v, (rows, v.shape[-1]))


def _colsum(v):
    return jnp.sum(v, axis=0, keepdims=True)


def _matmul(name, a, b, mode, out_dtype=F32, tm=1024, tn=1024, tk=None, out_stack=None, also_bf16=False, dep=None,
            fold=1, epilogue=None):
    def dims(z):
        return (z.shape[-2], z.shape[-1] * (z.shape[0] if z.ndim == 3 else 1))

    ar, ac = dims(a)
    br, bc = dims(b)
    if mode == "nn":
        m, k, n = ar, ac, bc
        assert br == k
    elif mode == "nt":
        m, k, n = ar, ac, br
        assert bc == k
    else:
        m, k, n = ac, ar, bc
        assert br == k
    m_lim, k_lim, n_lim = [m], [k], [n]
    if a.ndim == 3:
        (m_lim if mode == "tn" else k_lim).append(a.shape[-1])
    if b.ndim == 3:
        (k_lim if mode == "nt" else n_lim).append(b.shape[-1])
    if out_stack:
        n_lim.append(n // out_stack)
    tm = _pick(functools.reduce(math.gcd, m_lim), tm)
    tn = _pick(functools.reduce(math.gcd, n_lim), tn)
    k_unit = functools.reduce(math.gcd, k_lim)
    if tk is None:
        sa, sb, so = a.dtype.itemsize, b.dtype.itemsize, jnp.dtype(out_dtype).itemsize + (2 if also_bf16 else 0)
        fits = [t for t in range(LANES, k_unit + 1, LANES) if k_unit % t == 0 and
                2 * t * (tm * sa + tn * sb) + tm * tn * (2 * so + (4 if t < k else 0)) <= MATMUL_VMEM_BUDGET]
        tk = max(fits) if fits else _pick(k_unit, 512)
    else:
        tk = _pick(k_unit, tk)
    assert (k // tk) % fold == 0
    nk = k // (tk * fold)

    def spec(z, brows, bcols, ridx, cidx):
        if z.ndim == 3:
            per = z.shape[-1] // bcols
            return pl.BlockSpec((None, brows, bcols),
                                lambda i, j, kk: (cidx(i, j, kk) // per, ridx(i, j, kk), cidx(i, j, kk) % per))
        return pl.BlockSpec((brows, bcols), lambda i, j, kk: (ridx(i, j, kk), cidx(i, j, kk)))

    gi = lambda i, j, kk: i
    gj = lambda i, j, kk: j
    a_specs, b_specs = [], []
    for f in range(fold):
        gk = lambda i, j, kk, f=f: fold * kk + f
        if mode == "nn":
            a_specs.append(spec(a, tm, tk, gi, gk))
            b_specs.append(spec(b, tk, tn, gk, gj))
            dn = (((1,), (0,)), ((), ()))
        elif mode == "nt":
            a_specs.append(spec(a, tm, tk, gi, gk))
            b_specs.append(spec(b, tn, tk, gj, gk))
            dn = (((1,), (1,)), ((), ()))
        else:
            a_specs.append(spec(a, tk, tm, gk, gi))
            b_specs.append(spec(b, tk, tn, gk, gj))
            dn = (((0,), (0,)), ((), ()))

    epi_fn, epi_ins, epi_outs = epilogue if epilogue else (None, [], [])
    n_out = len(epi_outs) if epilogue else (2 if also_bf16 else 1)

    deps = [] if dep is None else [dep]

    def body(*refs):
        a_refs, b_refs = refs[:fold], refs[fold:2 * fold]
        e_refs = refs[2 * fold:2 * fold + len(epi_ins)]
        rest = refs[2 * fold + len(epi_ins) + len(deps):]
        o_refs, acc = rest[:n_out], rest[n_out:]
        part = None
        for a_ref, b_ref in zip(a_refs, b_refs):
            one = lax.dot_general(a_ref[...].astype(BF16), b_ref[...].astype(BF16), dn, preferred_element_type=F32)
            part = one if part is None else part + one

        def emit(val):
            vals = epi_fn(val, *[r[...] for r in e_refs]) if epilogue else [val] * n_out
            for o_ref, v in zip(o_refs, vals):
                o_ref[...] = v.astype(o_ref.dtype)

        if nk == 1:
            emit(part)
            return
        acc_ref, = acc
        kk = pl.program_id(2)

        @pl.when(kk == 0)
        def _():
            acc_ref[...] = part

        @pl.when(kk > 0)
        def _():
            acc_ref[...] += part

        @pl.when(kk == nk - 1)
        def _():
            emit(acc_ref[...])

    if out_stack:
        per = (n // out_stack) // tn
        out_spec = pl.BlockSpec((None, tm, tn), lambda i, j, kk: (j // per, i, j % per))
        shape = (out_stack, m, n // out_stack)
    else:
        out_spec = pl.BlockSpec((tm, tn), lambda i, j, kk: (i, j))
        shape = (m, n)
    if epilogue:
        assert not out_stack and not also_bf16
        kinds = {False: (pl.BlockSpec((tm, tn), lambda i, j, kk: (i, j)), (m, n)),
                 True: (pl.BlockSpec((tn, tm), lambda i, j, kk: (j, i)), (n, m)),
                 'pair': (pl.BlockSpec((2, tm, tn), lambda i, j, kk: (0, i, j)), (2, m, n))}
        out_specs = [kinds[t][0] for _, t in epi_outs]
        out_shapes = [jax.ShapeDtypeStruct(kinds[t][1], dt) for dt, t in epi_outs]
    else:
        out_specs = [out_spec] * n_out
        out_shapes = [jax.ShapeDtypeStruct(shape, dt) for dt in [out_dtype, BF16][:n_out]]
    e_pairs = [z if isinstance(z, tuple) else (z, 0) for z in epi_ins]
    assert all(off % tn == 0 for _, off in e_pairs)
    e_specs = [pl.BlockSpec((1, tn) if z.shape[0] == 1 else (tm, tn),
                            lambda i, j, kk, ob=off // tn, row=z.shape[0] == 1: (0 if row else i, j + ob))
               for z, off in e_pairs]
    res = pl.pallas_call(
        body, grid=(m // tm, n // tn, nk),
        in_specs=a_specs + b_specs + e_specs + [pl.BlockSpec(memory_space=pl.ANY)] * len(deps),
        out_specs=out_specs, out_shape=out_shapes,
        scratch_shapes=[pltpu.VMEM((tm, tn), F32)] if nk > 1 else [], name=name, compiler_params=_params(3),
    )(*[a] * fold, *[b] * fold, *[z for z, _ in e_pairs], *deps)
    return res if (also_bf16 or epilogue) else res[0]


def _all_gather(name, arrs, dep=None):
    n = len(arrs)
    deps = [] if dep is None else [dep]

    def body(*refs):
        ins, outs = refs[:n], refs[n + len(deps):2 * n + len(deps)]
        send_sems, recv_sems, local_sems = refs[2 * n + len(deps):]
        x, y, c = _dev()
        me, sib = (x, y, c), (x, y, 1 - c)
        chips = [(1 - x, y), (x, 1 - y), (1 - x, 1 - y)]

        def slot(p):
            return 4 * p[0] + 2 * p[1] + p[2]

        def copy(a, k, block, to, src=None):
            dst = outs[a].at[slot(block)]
            return pltpu.make_async_remote_copy(
                src_ref=dst if src is None else src, dst_ref=dst,
                send_sem=send_sems.at[7 * a + k], recv_sem=recv_sems.at[7 * a + k],
                device_id=to, device_id_type=MESH)

        mine = [pltpu.make_async_copy(ins[a], outs[a].at[slot(me)], local_sems.at[a]) for a in range(n)]
        for cp in mine:
            cp.start()
        first = []
        for a in range(n):
            first.append(copy(a, 0, me, sib, src=ins[a]))
            first += [copy(a, 1 + j, me, (*chip, c), src=ins[a]) for j, chip in enumerate(chips)]
        for cp in first:
            cp.start()
        passed = []
        for j, chip in enumerate(chips):
            for a in range(n):
                copy(a, 1 + j, (*chip, c), me).wait_recv()
                cp = copy(a, 4 + j, (*chip, c), sib)
                cp.start()
                passed.append(cp)
        for a in range(n):
            copy(a, 0, sib, me).wait_recv()
            for j, chip in enumerate(chips):
                copy(a, 4 + j, (*chip, 1 - c), me).wait_recv()
        for cp in first + passed:
            cp.wait_send()
        for cp in mine:
            cp.wait()

    any_spec = pl.BlockSpec(memory_space=pl.ANY)
    return pl.pallas_call(
        body, in_specs=[any_spec] * (n + len(deps)), out_specs=[any_spec] * n,
        out_shape=[jax.ShapeDtypeStruct((N_DEV,) + a.shape, a.dtype) for a in arrs],
        scratch_shapes=[pltpu.SemaphoreType.DMA((7 * n,)), pltpu.SemaphoreType.DMA((7 * n,)),
                        pltpu.SemaphoreType.DMA((n,))],
        name=name,
    )(*arrs, *deps)


def _grad_to_sibling(gds):
    n = len(gds)

    def body(*refs):
        g_refs, r_refs = refs[:n], refs[n:2 * n]
        send_sems, recv_sems = refs[2 * n:]
        x, y, c = _dev()
        cps = []
        for a in range(n):
            for k in range(4):
                cp = pltpu.make_async_remote_copy(
                    src_ref=g_refs[a].at[2 * k + (1 - c)], dst_ref=r_refs[a].at[k],
                    send_sem=send_sems.at[4 * a + k], recv_sem=recv_sems.at[4 * a + k],
                    device_id=(x, y, 1 - c), device_id_type=MESH)
                cp.start()
                cps.append(cp)
        for cp in cps:
            cp.wait()

    any_spec = pl.BlockSpec(memory_space=pl.ANY)
    return pl.pallas_call(
        body, in_specs=[any_spec] * n, out_specs=[any_spec] * n,
        out_shape=[jax.ShapeDtypeStruct((4,) + g.shape[1:], g.dtype) for g in gds],
        scratch_shapes=[pltpu.SemaphoreType.DMA((4 * n,)), pltpu.SemaphoreType.DMA((4 * n,))],
        name="grad_to_sibling",
    )(*gds)


def _chip_sum(name, gd, from_sib, c_arr):
    _, k, n = gd.shape
    tr = _pick(k, max(16, (1 << 20) // (4 * n)), 16)

    def body(c_ref, a_ref, b_ref, o_ref):
        o_ref[...] = (a_ref[...] + b_ref[...]).astype(o_ref.dtype)

    return pl.pallas_call(
        body,
        grid_spec=pltpu.PrefetchScalarGridSpec(
            num_scalar_prefetch=1, grid=(4, k // tr),
            in_specs=[pl.BlockSpec((1, tr, n), lambda q, i, cr: (2 * q + cr[0], i, 0)),
                      pl.BlockSpec((1, tr, n), lambda q, i, cr: (q, i, 0))],
            out_specs=pl.BlockSpec((1, tr, n), lambda q, i, cr: (q, i, 0))),
        out_shape=jax.ShapeDtypeStruct((4, k, n), BF16), name=name, compiler_params=_params(2),
    )(c_arr, gd, from_sib)


def _grad_to_chips(sums):
    n = len(sums)

    def body(*refs):
        s_refs, p_refs = refs[:n], refs[n:2 * n]
        send_sems, recv_sems, local_sems = refs[2 * n:]
        x, y, c = _dev()
        my_chip = 2 * x + y
        cps = []
        for a in range(n):
            local = pltpu.make_async_copy(s_refs[a].at[my_chip], p_refs[a].at[my_chip], local_sems.at[a])
            local.start()
            cps.append(local)
            for j, (px, py) in enumerate([(1 - x, y), (x, 1 - y), (1 - x, 1 - y)]):
                cp = pltpu.make_async_remote_copy(
                    src_ref=s_refs[a].at[2 * px + py], dst_ref=p_refs[a].at[my_chip],
                    send_sem=send_sems.at[3 * a + j], recv_sem=recv_sems.at[3 * a + j],
                    device_id=(px, py, c), device_id_type=MESH)
                cp.start()
                cps.append(cp)
        for cp in cps:
            cp.wait()

    any_spec = pl.BlockSpec(memory_space=pl.ANY)
    return pl.pallas_call(
        body, in_specs=[any_spec] * n, out_specs=[any_spec] * n,
        out_shape=[jax.ShapeDtypeStruct(s.shape, s.dtype) for s in sums],
        scratch_shapes=[pltpu.SemaphoreType.DMA((3 * n,)), pltpu.SemaphoreType.DMA((3 * n,)),
                        pltpu.SemaphoreType.DMA((n,))],
        name="grad_to_chips",
    )(*sums)


FLIPS = [(0, 0, 1), (0, 1, 0), (1, 0, 0), (0, 1, 1), (1, 0, 1), (1, 1, 0), (1, 1, 1)]
N_PEERS = len(FLIPS)
_HBM = pl.BlockSpec(memory_space=pltpu.HBM)
_SEM = pl.BlockSpec(memory_space=pltpu.SEMAPHORE)
_EFFECT = pltpu.SideEffectType.DATAFLOW_SIDE_EFFECTING


def _flip(x, y, c, f):
    return (1 - x if f[0] else x, 1 - y if f[1] else y, 1 - c if f[2] else c)


def _slot(p):
    return 4 * p[0] + 2 * p[1] + p[2]


def _exchange_copies(src_refs, land_refs, send_sems, recv_sems, gather):
    x, y, c = _dev()
    mine = _slot((x, y, c))
    cps = []
    for a, (src, land) in enumerate(zip(src_refs, land_refs)):
        for k, f in enumerate(FLIPS):
            peer = _flip(x, y, c, f)
            cps.append(pltpu.make_async_remote_copy(
                src_ref=src if gather else src.at[_slot(peer)], dst_ref=land.at[mine],
                send_sem=send_sems.at[N_PEERS * a + k], recv_sem=recv_sems.at[N_PEERS * a + k],
                device_id=peer, device_id_type=MESH))
    return cps


def _exchange_start(name, srcs, gather, after):
    n = len(srcs)
    lands = [lax.empty(((N_DEV,) + s.shape) if gather else s.shape, s.dtype) for s in srcs]

    def body(*refs):
        src_refs, land_refs = refs[:n], refs[n:2 * n]
        send_sems, recv_sems, local_sems = refs[2 * n + 1:2 * n + 4]
        token = refs[-1]
        if gather:
            x, y, c = _dev()
            for a in range(n):
                pltpu.make_async_copy(src_refs[a], land_refs[a].at[_slot((x, y, c))], local_sems.at[a]).start()
        for cp in _exchange_copies(src_refs, land_refs, send_sems, recv_sems, gather):
            cp.start()
        token[...] = jnp.zeros_like(token)

    hbm = lambda z: pltpu.HBM(z.shape, z.dtype)
    outs = pl.pallas_call(
        body, name=name,
        out_shape=(pltpu.SemaphoreType.DMA((N_PEERS * n,)), pltpu.SemaphoreType.DMA((N_PEERS * n,)),
                   pltpu.SemaphoreType.DMA((n,)), *[hbm(s) for s in srcs], *[hbm(z) for z in lands],
                   jax.ShapeDtypeStruct((8, LANES), F32)),
        in_specs=[_HBM] * (2 * n) + [pl.BlockSpec(memory_space=pl.ANY)],
        out_specs=(_SEM, _SEM, _SEM, *[_HBM] * (2 * n), pl.BlockSpec(memory_space=pltpu.VMEM)),
        input_output_aliases={i: 3 + i for i in range(2 * n)},
        compiler_params=pltpu.CompilerParams(has_side_effects=_EFFECT),
    )(*[pltpu.with_memory_space_constraint(z, pltpu.HBM) for z in list(srcs) + lands], after)
    return (outs[:3], outs[3:3 + n], outs[3 + n:3 + 2 * n], gather), outs[-1]


def _exchange_wait(name, handles, after):
    sems, srcs, lands, gather = handles
    n = len(srcs)

    def body(*refs):
        src_refs, land_refs = refs[:n], refs[n:2 * n]
        send_sems, recv_sems, local_sems = refs[2 * n:2 * n + 3]
        if gather:
            for a in range(n):
                pltpu.make_async_copy(src_refs[a], land_refs[a].at[0], local_sems.at[a]).wait()
        for cp in _exchange_copies(src_refs, land_refs, send_sems, recv_sems, gather):
            cp.wait_send()
            cp.wait_recv()

    hbm = lambda z: pltpu.HBM(z.shape, z.dtype)
    outs = pl.pallas_call(
        body, name=name, out_shape=tuple(hbm(z) for z in list(srcs) + list(lands)),
        in_specs=[_HBM] * (2 * n) + [_SEM] * 3 + [pl.BlockSpec(memory_space=pl.ANY)],
        out_specs=tuple([_HBM] * (2 * n)), input_output_aliases={i: i for i in range(2 * n)},
        compiler_params=pltpu.CompilerParams(has_side_effects=_EFFECT),
    )(*srcs, *lands, *sems, after)
    return list(outs[n:])


def _pack_rows(sizes, width, row_align):
    offs, r = [], 0
    for s in sizes:
        offs.append(r)
        r += -(-s // width)
    total = -(-r // row_align) * row_align
    return offs, total


def _pack(items, width, row_align, lead=()):
    nl = len(lead)
    sizes = [int(jnp.size(a)) // max(1, functools.reduce(lambda p, q: p * q, lead, 1)) for a in items]
    offs, total = _pack_rows(sizes, width, row_align)
    flat = []
    used = 0
    for a, s in zip(items, sizes):
        f = a.reshape(lead + (s,))
        pad = -(-s // width) * width - s
        if pad:
            f = jnp.pad(f, [(0, 0)] * nl + [(0, pad)])
        flat.append(f)
        used += s + pad
    tail = total * width - used
    if tail:
        flat.append(jnp.zeros(lead + (tail,), items[0].dtype))
    return jnp.concatenate(flat, axis=-1).reshape(lead + (total, width)), offs


def _unpack(packed, off, shape, lead=()):
    nl = len(lead)
    size = functools.reduce(lambda p, q: p * q, shape, 1)
    width = packed.shape[-1]
    rows = -(-size // width)
    blk = lax.slice_in_dim(packed, off, off + rows, axis=nl).reshape(lead + (rows * width,))
    return lax.slice_in_dim(blk, 0, size, axis=nl).reshape(lead + tuple(shape))


def _rms(x, g):
    return (x * lax.rsqrt(jnp.mean(x * x, axis=-1, keepdims=True) + RMS_EPS)) * g


def _norm_mod(x, g, sc, sh):
    return _rms(x, g) * (1.0 + sc) + sh


def _mix_fn(glu_a, glu_b, attn, ga, gs):
    return jax.nn.sigmoid(ga) * attn + jax.nn.sigmoid(gs) * (glu_a * jax.nn.sigmoid(glu_b))


def _s5_disc_fn(a_re, a_im, log_dt, b_re, b_im):
    dt = jnp.exp(log_dt)
    mag = jnp.exp(a_re * dt)
    lr, li = mag * jnp.cos(a_im * dt), mag * jnp.sin(a_im * dt)
    den = a_re * a_re + a_im * a_im
    zr = ((lr - 1.0) * a_re + li * a_im) / den
    zi = (li * a_re - (lr - 1.0) * a_im) / den
    return lr, li, zr[None] * b_re - zi[None] * b_im, zr[None] * b_im + zi[None] * b_re


def _adamw_fn(w, g, m, v):
    m = ADAM_B1 * m + (1.0 - ADAM_B1) * g
    v = ADAM_B2 * v + (1.0 - ADAM_B2) * jnp.square(g)
    m_hat = m / (1.0 - ADAM_B1 ** ADAM_STEP)
    v_hat = v / (1.0 - ADAM_B2 ** ADAM_STEP)
    delta = -ADAM_LR * (m_hat / (jnp.sqrt(v_hat) + ADAM_EPS) + ADAM_WD * w)
    return delta, m, v


def _adamw(name, parts, w, m, v):
    p, r, c = parts.shape
    tr = _pick(r, max(8, (1 << 21) // (4 * c * max(p, 2))), 8)

    def fn(pv, wv, mv, vv):
        g = pv[0]
        for i in range(1, p):
            g = g + pv[i]
        d, m2, v2 = _adamw_fn(wv, g, mv, vv)
        return g, d, m2, v2

    spec = pl.BlockSpec((tr, c), lambda i: (i, 0))
    return _tile_call(
        name, fn, (r // tr,), [parts, w, m, v],
        [pl.BlockSpec((p, tr, c), lambda i: (0, i, 0)), spec, spec, spec],
        [jax.ShapeDtypeStruct((r, c), F32)] * 4, [spec] * 4)


def _adamw_sharded(name, parts, own_src, w, m, v, place):
    _, k, n = parts.shape
    tr = _pick(k, max(16, (1 << 19) // (4 * n)), 16)

    def body(pl_ref, p_ref, a_ref, w_ref, m_ref, v_ref, g_ref, d_ref, m2_ref, v2_ref):
        own = a_ref[0]
        g = None
        for q in range(N_DEV):
            term = jnp.where(pl_ref[0] == q, own, p_ref[q].astype(F32))
            g = term if g is None else g + term
        d, m2, v2 = _adamw_fn(w_ref[...], g, m_ref[...], v_ref[...])
        g_ref[...] = g
        d_ref[...] = d
        m2_ref[...] = m2
        v2_ref[...] = v2

    spec = pl.BlockSpec((tr, n), lambda i, pr: (i, 0))
    return pl.pallas_call(
        body,
        grid_spec=pltpu.PrefetchScalarGridSpec(
            num_scalar_prefetch=1, grid=(k // tr,),
            in_specs=[pl.BlockSpec((N_DEV, tr, n), lambda i, pr: (0, i, 0)),
                      pl.BlockSpec((1, tr, n), lambda i, pr: (pr[1], i, 0)),
                      spec, spec, spec],
            out_specs=[spec] * 4),
        out_shape=[jax.ShapeDtypeStruct((k, n), F32)] * 4, name=name, compiler_params=_params(1),
    )(place, parts, own_src, w, m, v)


def _attn_mask(n, rows):
    qi = lax.broadcasted_iota(jnp.int32, (rows, 2 * ATT_BLOCK), 0) & (ATT_BLOCK - 1)
    kj = lax.broadcasted_iota(jnp.int32, (rows, 2 * ATT_BLOCK), 1)
    rel = qi + ATT_BLOCK - kj
    return (rel >= 0) & (rel < ATT_BLOCK) & ((kj >= ATT_BLOCK) | (n > 0))


def _attn_probs(q, k, sink, mask):
    s = lax.dot_general(q, k, (((1,), (1,)), ((), ())), preferred_element_type=F32) * (HEAD_DIM ** -0.5)
    s = jnp.where(mask, s, NEG_INF)
    m = jnp.maximum(jnp.max(s, axis=-1, keepdims=True), sink)
    p = jnp.exp(s - m)
    e_sink = jnp.exp(sink - m)
    inv = 1.0 / (jnp.sum(p, axis=-1, keepdims=True) + e_sink)
    return p * inv, e_sink * inv


def _attn_specs(qpk):
    blk = ATT_BLOCK
    q_spec = pl.BlockSpec((qpk, blk, HEAD_DIM), lambda h, n: (h, n, 0))
    cur = pl.BlockSpec((1, blk, HEAD_DIM), lambda h, n: (h, n, 0))
    prev = pl.BlockSpec((1, blk, HEAD_DIM), lambda h, n: (h, jnp.maximum(n - 1, 0), 0))
    sink_spec = pl.BlockSpec((1, qpk * blk, 1), lambda h, n: (h, 0, 0))
    return q_spec, cur, prev, sink_spec


def _attn_fwd(q, k, v, sinks):
    hq, l, _ = q.shape
    qpk = hq // N_KV_HEADS
    nb = l // ATT_BLOCK
    rows = qpk * ATT_BLOCK
    q_spec, cur, prev, sink_spec = _attn_specs(qpk)

    def body(q_ref, kp_ref, kc_ref, vp_ref, vc_ref, sink_ref, o_ref):
        mask = _attn_mask(pl.program_id(1), rows)
        kk = jnp.concatenate([kp_ref[0], kc_ref[0]], axis=0).astype(BF16)
        vv = jnp.concatenate([vp_ref[0], vc_ref[0]], axis=0).astype(BF16)
        p, _ = _attn_probs(q_ref[...].reshape(rows, HEAD_DIM).astype(BF16), kk, sink_ref[0], mask)
        o = jnp.dot(p.astype(BF16), vv, preferred_element_type=F32)
        o_ref[...] = o.reshape(qpk, ATT_BLOCK, HEAD_DIM).astype(o_ref.dtype)

    return pl.pallas_call(
        body, grid=(N_KV_HEADS, nb), in_specs=[q_spec, prev, cur, prev, cur, sink_spec],
        out_specs=q_spec, out_shape=jax.ShapeDtypeStruct((hq, l, HEAD_DIM), BF16),
        name="attn_fwd", compiler_params=_params(2),
    )(q, k, k, v, v, sinks)


def _attn_bwd(q, k, v, sinks, do):
    hq, l, _ = q.shape
    qpk = hq // N_KV_HEADS
    nb = l // ATT_BLOCK
    blk = ATT_BLOCK
    rows = qpk * blk
    q_spec, cur, prev, sink_spec = _attn_specs(qpk)
    part_spec = pl.BlockSpec((1, 1, 2 * blk, HEAD_DIM), lambda h, n: (h, n, 0, 0))
    dsink_spec = pl.BlockSpec((qpk, 1, LANES), lambda h, n: (h, 0, 0))
    tn = (((0,), (0,)), ((), ()))

    def body(q_ref, do_ref, kp_ref, kc_ref, vp_ref, vc_ref, sink_ref, dq_ref, dkp_ref, dvp_ref, dsink_ref):
        n = pl.program_id(1)
        mask = _attn_mask(n, rows)
        kk = jnp.concatenate([kp_ref[0], kc_ref[0]], axis=0).astype(BF16)
        vv = jnp.concatenate([vp_ref[0], vc_ref[0]], axis=0).astype(BF16)
        qb = q_ref[...].reshape(rows, HEAD_DIM).astype(BF16)
        do32 = do_ref[...].astype(F32).reshape(rows, HEAD_DIM)
        dob = do32.astype(BF16)
        p, p_sink = _attn_probs(qb, kk, sink_ref[0], mask)
        pb = p.astype(BF16)
        o = jnp.dot(pb, vv, preferred_element_type=F32)
        delta = jnp.sum(do32 * o, axis=-1, keepdims=True)
        dp = lax.dot_general(dob, vv, (((1,), (1,)), ((), ())), preferred_element_type=F32)
        ds = (p * (dp - delta) * (HEAD_DIM ** -0.5)).astype(BF16)
        dq = jnp.dot(ds, kk, preferred_element_type=F32)
        dq_ref[...] = dq.reshape(qpk, blk, HEAD_DIM).astype(dq_ref.dtype)
        dkp_ref[0, 0] = lax.dot_general(ds, qb, tn, preferred_element_type=F32)
        dvp_ref[0, 0] = lax.dot_general(pb, dob, tn, preferred_element_type=F32)
        dsr = p_sink * delta
        for g in range(qpk):
            dsg = jnp.broadcast_to(-_colsum(dsr[g * blk:(g + 1) * blk]), (1, LANES))

            @pl.when(n == 0)
            def _():
                dsink_ref[g] = dsg

            @pl.when(n > 0)
            def _():
                dsink_ref[g] += dsg


    part_shape = jax.ShapeDtypeStruct((N_KV_HEADS, nb, 2 * blk, HEAD_DIM), F32)
    dq, dkp, dvp, dsink = pl.pallas_call(
        body, grid=(N_KV_HEADS, nb), in_specs=[q_spec, q_spec, prev, cur, prev, cur, sink_spec],
        out_specs=[q_spec, part_spec, part_spec, dsink_spec],
        out_shape=[jax.ShapeDtypeStruct((hq, l, HEAD_DIM), BF16), part_shape, part_shape,
                   jax.ShapeDtypeStruct((hq, 1, LANES), F32)],
        name="attn_bwd", compiler_params=_params(2),
    )(q, do, k, k, v, v, sinks)

    def combine(a_cur, a_nxt, b_cur, b_nxt):
        last = pl.program_id(1) == nb - 1
        keep = jnp.where(last, 0.0, 1.0)
        return (a_cur[0, 0, blk:] + keep * a_nxt[0, 0, :blk])[None], (b_cur[0, 0, blk:] + keep * b_nxt[0, 0, :blk])[None]

    nxt_spec = pl.BlockSpec((1, 1, 2 * blk, HEAD_DIM), lambda h, n: (h, jnp.minimum(n + 1, nb - 1), 0, 0))
    kv_shape = jax.ShapeDtypeStruct((N_KV_HEADS, l, HEAD_DIM), BF16)
    dk, dv = _tile_call("attn_dkv", combine, (N_KV_HEADS, nb), [dkp, dkp, dvp, dvp],
                        [part_spec, nxt_spec, part_spec, nxt_spec], [kv_shape, kv_shape], [cur, cur])
    return dq, dk, dv, dsink


def _block_diag(m):
    j, gl, a, b = m.shape
    eye = jnp.eye(gl, dtype=m.dtype)
    return (m[:, :, :, None, :] * eye[None, :, None, :, None]).reshape(j, gl * a, gl * b)


def _diag_blocks(z, a):
    j = z.shape[0]
    gl = z.shape[1] // a
    b = z.shape[2] // gl
    d = jnp.diagonal(z.reshape(j, gl, a, gl, b), axis1=1, axis2=3)
    return d.transpose(0, 3, 1, 2)


def _s5_permute(src_ref, dst_ref, t_len):
    seg = t_len // 8
    for k in range(seg):
        dst_ref[8 * k:8 * k + 8, :] = src_ref[pl.ds(k, 8, stride=seg), :]


def _s5_unpermute(perm_ref, t_len, emit):
    per_seg = t_len // 64
    for m in range(t_len // 8):
        emit(8 * m, perm_ref[pl.ds(64 * (m % per_seg) + m // per_seg, 8, stride=8), :])


def _s5_powers(p_ref, lr, li, seg):
    hs = TILE_STATES

    def step(k, carry):
        pr, pi = carry
        p_ref[pl.ds(k, 1), 0:hs] = pr
        p_ref[pl.ds(k, 1), hs:2 * hs] = pi
        return lr * pr - li * pi, lr * pi + li * pr

    lax.fori_loop(0, seg, step, (lr, li))


def _s5_local_scan(x_ref, base, lr, li, seg, reverse):
    hs = TILE_STATES
    lr8, li8 = jnp.broadcast_to(lr, (8, hs)), jnp.broadcast_to(li, (8, hs))
    if reverse:
        li8 = -li8

    def step(i, carry):
        hr, hi = carry
        k = seg - 1 - i if reverse else i
        rows = pl.ds(pl.multiple_of(base + 8 * k, 8), 8)
        nr = lr8 * hr - li8 * hi + x_ref[rows, 0:hs]
        ni = lr8 * hi + li8 * hr + x_ref[rows, hs:2 * hs]
        x_ref[rows, 0:hs] = nr
        x_ref[rows, hs:2 * hs] = ni
        return nr, ni

    zero = jnp.zeros((8, hs), F32)
    return lax.fori_loop(0, seg, step, (zero, zero), unroll=2)


def _s5_carries(c_ref, e_ref, ends, start, pw_r, pw_i, reverse):
    hs = TILE_STATES
    e_ref[:, 0:hs] = ends[0]
    e_ref[:, hs:2 * hs] = ends[1]
    cr, ci = start
    if reverse:
        pw_i = -pw_i
    for s in (range(7, -1, -1) if reverse else range(8)):
        c_ref[s:s + 1, 0:hs] = cr
        c_ref[s:s + 1, hs:2 * hs] = ci
        er, ei = e_ref[s:s + 1, 0:hs], e_ref[s:s + 1, hs:2 * hs]
        cr, ci = er + pw_r * cr - pw_i * ci, ei + pw_r * ci + pw_i * cr
    return cr, ci


def _s5_states(u_perm_b16, bd_ref, x_ref, base, c_ref, e_ref, p_ref, lr, li, h_in, t_len):
    hs = TILE_STATES
    seg = t_len // 8
    x_ref[pl.ds(base, t_len), :] = jnp.dot(u_perm_b16, bd_ref[0], preferred_element_type=F32)
    ends = _s5_local_scan(x_ref, base, lr, li, seg, False)
    pw_r, pw_i = p_ref[seg - 1:seg, 0:hs], p_ref[seg - 1:seg, hs:2 * hs]
    h_out = _s5_carries(c_ref, e_ref, ends, h_in, pw_r, pw_i, False)
    cr, ci = c_ref[:, 0:hs], c_ref[:, hs:2 * hs]

    def fix(k, carry):
        rows = pl.ds(pl.multiple_of(base + 8 * k, 8), 8)
        pr, pi = p_ref[pl.ds(k, 1), 0:hs], p_ref[pl.ds(k, 1), hs:2 * hs]
        x_ref[rows, 0:hs] += pr * cr - pi * ci
        x_ref[rows, hs:2 * hs] += pr * ci + pi * cr
        return carry

    lax.fori_loop(0, seg, fix, 0, unroll=2)
    return h_out


def _s5_fwd(proj, u_off, bd, cbd, lam, dvec, t_len):
    l = proj.shape[0]
    nj = bd.shape[0]
    nch = l // t_len
    hs = TILE_STATES
    ub = u_off // LANES
    seg = t_len // 8
    assert t_len % 64 == 0

    def body(u_ref, bd_ref, cbd_ref, lam_ref, d_ref, y_ref, hst_ref, x_ref, h_ref, p_ref, c_ref, e_ref, up_ref, yp_ref):
        lr, li = lam_ref[0, 0:1, :], lam_ref[0, 1:2, :]

        @pl.when(pl.program_id(1) == 0)
        def _():
            h_ref[...] = jnp.zeros_like(h_ref)
            _s5_powers(p_ref, lr, li, seg)

        hst_ref[0, 0] = h_ref[...]
        _s5_permute(u_ref, up_ref, t_len)
        h_out = _s5_states(up_ref[...].astype(BF16), bd_ref, x_ref, 0, c_ref, e_ref, p_ref, lr, li,
                           (h_ref[:, 0:hs], h_ref[:, hs:2 * hs]), t_len)
        h_ref[:, 0:hs] = h_out[0]
        h_ref[:, hs:2 * hs] = h_out[1]
        yp_ref[...] = jnp.dot(x_ref[...].astype(BF16), cbd_ref[0], preferred_element_type=F32)
        dv = d_ref[0]

        def out(r0, rows):
            y_ref[r0:r0 + 8, :] = rows + dv * u_ref[r0:r0 + 8, :]

        _s5_unpermute(yp_ref, t_len, out)

    return pl.pallas_call(
        body, grid=(nj, nch),
        in_specs=[pl.BlockSpec((t_len, LANES), lambda j, c: (c, ub + j)),
                  pl.BlockSpec((1, LANES, 2 * hs), lambda j, c: (j, 0, 0)),
                  pl.BlockSpec((1, 2 * hs, LANES), lambda j, c: (j, 0, 0)),
                  pl.BlockSpec((1, 2, hs), lambda j, c: (j, 0, 0)),
                  pl.BlockSpec((1, 1, LANES), lambda j, c: (j, 0, 0))],
        out_specs=[pl.BlockSpec((t_len, LANES), lambda j, c: (c, j)),
                   pl.BlockSpec((1, 1, 1, 2 * hs), lambda j, c: (j, c, 0, 0))],
        out_shape=[jax.ShapeDtypeStruct((l, nj * LANES), F32),
                   jax.ShapeDtypeStruct((nj, nch, 1, 2 * hs), F32)],
        scratch_shapes=[pltpu.VMEM((t_len, 2 * hs), F32), pltpu.VMEM((1, 2 * hs), F32),
                        pltpu.VMEM((seg, 2 * hs), F32), pltpu.VMEM((8, 2 * hs), F32), pltpu.VMEM((8, 2 * hs), F32),
                        pltpu.VMEM((t_len, LANES), F32), pltpu.VMEM((t_len, LANES), F32)],
        name="s5_fwd", compiler_params=_params(2),
    )(proj, bd, cbd, lam, dvec)


def _s5_bwd(proj, u_off, dy, hst, bd, bdt, cbdt, lam, dvec, t_len):
    l = proj.shape[0]
    nj = bd.shape[0]
    nch = l // t_len
    hs = TILE_STATES
    ub = u_off // LANES
    seg = t_len // 8
    tn = (((0,), (0,)), ((), ()))
    assert t_len % 64 == 0

    def body(u_ref, dy_ref, hst_ref, bd_ref, bdt_ref, cbdt_ref, lam_ref, d_ref,
             du_ref, dbd_ref, dcbdt_ref, dlam_ref, dd_ref,
             x_ref, g_ref, gc_ref, p_ref, c_ref, e_ref, up_ref, dyp_ref, dup_ref):
        first = pl.program_id(1) == 0
        lr, li = lam_ref[0, 0:1, :], lam_ref[0, 1:2, :]

        @pl.when(first)
        def _():
            gc_ref[...] = jnp.zeros_like(gc_ref)
            _s5_powers(p_ref, lr, li, seg)

        _s5_permute(u_ref, up_ref, t_len)
        _s5_permute(dy_ref, dyp_ref, t_len)
        ub16, dyb16 = up_ref[...].astype(BF16), dyp_ref[...].astype(BF16)
        h0 = hst_ref[0, 0]
        _s5_states(ub16, bd_ref, x_ref, 8, c_ref, e_ref, p_ref, lr, li, (h0[:, 0:hs], h0[:, hs:2 * hs]), t_len)
        x_ref[0:8, :] = c_ref[...]
        g_ref[...] = jnp.dot(dyb16, cbdt_ref[0], preferred_element_type=F32)
        starts = _s5_local_scan(g_ref, 0, lr, li, seg, True)
        pw_r, pw_i = p_ref[seg - 1:seg, 0:hs], p_ref[seg - 1:seg, hs:2 * hs]
        g_out = _s5_carries(c_ref, e_ref, starts, (gc_ref[:, 0:hs], gc_ref[:, hs:2 * hs]), pw_r, pw_i, True)
        gc_ref[:, 0:hs] = g_out[0]
        gc_ref[:, hs:2 * hs] = g_out[1]
        cr, ci = c_ref[:, 0:hs], c_ref[:, hs:2 * hs]

        def fix(k, carry):
            alr, ali = carry
            rows = pl.ds(pl.multiple_of(8 * k, 8), 8)
            pr, pi = p_ref[pl.ds(seg - 1 - k, 1), 0:hs], p_ref[pl.ds(seg - 1 - k, 1), hs:2 * hs]
            gr = g_ref[rows, 0:hs] + pr * cr + pi * ci
            gi = g_ref[rows, hs:2 * hs] + pr * ci - pi * cr
            g_ref[rows, 0:hs] = gr
            g_ref[rows, hs:2 * hs] = gi
            hpr, hpi = x_ref[rows, 0:hs], x_ref[rows, hs:2 * hs]
            return alr + gr * hpr + gi * hpi, ali + gi * hpr - gr * hpi

        zero = jnp.zeros((8, hs), F32)
        alr, ali = lax.fori_loop(0, seg, fix, (zero, zero), unroll=2)
        alr, ali = _colsum(alr), _colsum(ali)
        g = g_ref[...].astype(BF16)
        h = x_ref[pl.ds(8, t_len), :].astype(BF16)
        dup_ref[...] = jnp.dot(g, bdt_ref[0], preferred_element_type=F32)
        dv = d_ref[0]

        def out(r0, rows):
            du_ref[r0:r0 + 8, :] = (rows + dv * dy_ref[r0:r0 + 8, :]).astype(du_ref.dtype)

        _s5_unpermute(dup_ref, t_len, out)
        sign = jnp.where(lax.broadcasted_iota(jnp.int32, (1, 2 * hs), 1) < hs, 1.0, -1.0)
        dbd = lax.dot_general(ub16, g, tn, preferred_element_type=F32)
        dcbdt = lax.dot_general(dyb16, h, tn, preferred_element_type=F32) * sign
        ddv = _colsum(dy_ref[...] * u_ref[...])

        @pl.when(first)
        def _():
            dbd_ref[0] = dbd
            dcbdt_ref[0] = dcbdt
            dlam_ref[0, 0:1, :] = alr
            dlam_ref[0, 1:2, :] = ali
            dd_ref[0] = ddv

        @pl.when(jnp.logical_not(first))
        def _():
            dbd_ref[0] += dbd
            dcbdt_ref[0] += dcbdt
            dlam_ref[0, 0:1, :] += alr
            dlam_ref[0, 1:2, :] += ali
            dd_ref[0] += ddv

    rev = lambda c: nch - 1 - c
    wide = pl.BlockSpec((1, LANES, 2 * hs), lambda j, c: (j, 0, 0))
    tall = pl.BlockSpec((1, 2 * hs, LANES), lambda j, c: (j, 0, 0))
    return pl.pallas_call(
        body, grid=(nj, nch),
        in_specs=[pl.BlockSpec((t_len, LANES), lambda j, c: (rev(c), ub + j)),
                  pl.BlockSpec((t_len, LANES), lambda j, c: (rev(c), j)),
                  pl.BlockSpec((1, 1, 1, 2 * hs), lambda j, c: (j, rev(c), 0, 0)),
                  wide, tall, wide,
                  pl.BlockSpec((1, 2, hs), lambda j, c: (j, 0, 0)),
                  pl.BlockSpec((1, 1, LANES), lambda j, c: (j, 0, 0))],
        out_specs=[pl.BlockSpec((t_len, LANES), lambda j, c: (rev(c), j)),
                   wide, wide,
                   pl.BlockSpec((1, 2, hs), lambda j, c: (j, 0, 0)),
                   pl.BlockSpec((1, 1, LANES), lambda j, c: (j, 0, 0))],
        out_shape=[jax.ShapeDtypeStruct((l, nj * LANES), BF16),
                   jax.ShapeDtypeStruct((nj, LANES, 2 * hs), F32),
                   jax.ShapeDtypeStruct((nj, LANES, 2 * hs), F32),
                   jax.ShapeDtypeStruct((nj, 2, hs), F32),
                   jax.ShapeDtypeStruct((nj, 1, LANES), F32)],
        scratch_shapes=[pltpu.VMEM((t_len + 8, 2 * hs), F32), pltpu.VMEM((t_len, 2 * hs), F32),
                        pltpu.VMEM((1, 2 * hs), F32), pltpu.VMEM((seg, 2 * hs), F32),
                        pltpu.VMEM((8, 2 * hs), F32), pltpu.VMEM((8, 2 * hs), F32),
                        pltpu.VMEM((t_len, LANES), F32), pltpu.VMEM((t_len, LANES), F32),
                        pltpu.VMEM((t_len, LANES), F32)],
        name="s5_bwd", compiler_params=_params(2),
    )(proj, dy, hst, bd, bdt, cbdt, lam, dvec)


def _full_spec(shape):
    nd = len(shape)
    return pl.BlockSpec(tuple(shape), lambda i: (0,) * nd)


def _sds(shape, dtype=F32):
    return jax.ShapeDtypeStruct(tuple(shape), dtype)


def kernel(x, c, ada_w, ada_b, norm_mix_g, w_in, attn_sinks, w_attn_proj, ssm_a_re, ssm_a_im, ssm_log_dt, ssm_b_re, ssm_b_im, ssm_c_re, ssm_c_im, ssm_d, w_ssm_glu, w_out, norm_ffn_g, w_ffn_up, ffn_conv_w, ffn_conv_b, w_ffn_down, final_g, loss_target, m_ada_w, m_ada_b, m_norm_mix_g, m_w_in, m_attn_sinks, m_w_attn_proj, m_ssm_a_re, m_ssm_a_im, m_ssm_log_dt, m_ssm_b_re, m_ssm_b_im, m_ssm_c_re, m_ssm_c_im, m_ssm_d, m_w_ssm_glu, m_w_out, m_norm_ffn_g, m_w_ffn_up, m_ffn_conv_w, m_ffn_conv_b, m_w_ffn_down, m_final_g, v_ada_w, v_ada_b, v_norm_mix_g, v_w_in, v_attn_sinks, v_w_attn_proj, v_ssm_a_re, v_ssm_a_im, v_ssm_log_dt, v_ssm_b_re, v_ssm_b_im, v_ssm_c_re, v_ssm_c_im, v_ssm_d, v_w_ssm_glu, v_w_out, v_norm_ffn_g, v_w_ffn_up, v_ffn_conv_w, v_ffn_conv_b, v_w_ffn_down, v_final_g):
    given = dict(locals())
    names = ['ada_w', 'ada_b', 'norm_mix_g', 'w_in', 'attn_sinks', 'w_attn_proj', 'ssm_a_re', 'ssm_a_im',
             'ssm_log_dt', 'ssm_b_re', 'ssm_b_im', 'ssm_c_re', 'ssm_c_im', 'ssm_d', 'w_ssm_glu', 'w_out',
             'norm_ffn_g', 'w_ffn_up', 'ffn_conv_w', 'ffn_conv_b', 'w_ffn_down', 'final_g']

    xs = x[0]
    tgt = loss_target[0]
    l, d = xs.shape
    attn_w = w_attn_proj.shape[1]
    ssm_w = w_ssm_glu.shape[1]
    hq = attn_sinks.shape[1]
    qpk = hq // N_KV_HEADS
    kv_w = N_KV_HEADS * HEAD_DIM
    n_groups = ssm_a_re.shape[1]
    dff = ffn_conv_b.shape[1]
    in_w = attn_w + 2 * kv_w + ssm_w + 2 * d
    nj = ssm_w // LANES
    off_k, off_v, off_u = attn_w, attn_w + kv_w, attn_w + 2 * kv_w
    off_ga, off_gs = off_u + ssm_w, off_u + ssm_w + d
    assert hq * HEAD_DIM == attn_w and n_groups * SSM_P == ssm_w and l % ATT_BLOCK == 0

    xi, yi, ci = _dev()
    idx = 4 * xi + 2 * yi + ci

    row_sharded = {'w_out': (d, d), 'w_ffn_down': (dff, d)}
    big = ['w_in', 'w_attn_proj', 'w_ssm_glu', 'w_out', 'w_ffn_up', 'w_ffn_down']
    spack, s_offs = _pack([c, ffn_conv_w[0]], LANES, 8)
    w16 = {k: given[k][0].astype(BF16) for k in big}
    wg_in, sg = _all_gather("gather_first", [w16['w_in'], spack])
    mixer_w = ['w_attn_proj', 'w_ssm_glu', 'w_out']
    h_mixer, tok = _exchange_start("gather_mixer_start", [w16[k] for k in mixer_w], True, wg_in)
    h_up, tok = _exchange_start("gather_ffn_up_start", [w16['w_ffn_up']], True, tok)
    h_down, tok = _exchange_start("gather_ffn_down_start", [w16['w_ffn_down']], True, tok)
    full = {'w_in': wg_in.transpose(1, 0, 2).reshape(d, in_w)}
    c_all = _unpack(sg, s_offs[0], (d,), lead=(N_DEV,))
    conv_w = _unpack(sg, s_offs[1], ffn_conv_w.shape[1:], lead=(N_DEV,)).transpose(1, 0, 2).reshape(3, dff)
    conv_b = ffn_conv_b

    mod_n = ada_w.shape[2]
    tcm = _pick(mod_n, 512)
    ada_b_mine = lax.dynamic_slice_in_dim(ada_b, idx * mod_n, mod_n, axis=1)

    def modpart_fn(cv, wv, bv):
        cond = cv * jax.nn.sigmoid(cv)
        return jnp.dot(cond.astype(BF16), wv.astype(BF16), preferred_element_type=F32) + bv, cond

    modp, cond_all = _tile_call(
        "ada_rows", modpart_fn, (mod_n // tcm,), [c_all, ada_w[0], ada_b_mine],
        [pl.BlockSpec((N_DEV, d), lambda j: (0, 0)), pl.BlockSpec((d, tcm), lambda j: (0, j)),
         pl.BlockSpec((1, tcm), lambda j: (0, j))],
        [_sds((N_DEV, mod_n)), _sds((N_DEV, d))],
        [pl.BlockSpec((N_DEV, tcm), lambda j: (0, j)), pl.BlockSpec((N_DEV, d), lambda j: (0, 0))])
    (modg,) = _all_gather("gather_ada_rows", [modp])
    mod = lax.dynamic_index_in_dim(modg, idx, axis=1, keepdims=False).reshape(1, N_DEV * mod_n)
    sh1, sc1, g1, sh2, sc2, g2 = [mod[:, i * d:(i + 1) * d] for i in range(6)]

    tr = _pick(l, 256, 8)
    trh = _pick(l, 128, 8)
    nr, nrh = l // tr, l // trh
    g_mix, g_ffn, g_fin = norm_mix_g + tok[0:1, 0:1], norm_ffn_g, final_g.reshape(1, d)

    def with_t(fn):
        def wrapped(*vals):
            out = fn(*vals)
            out = out if isinstance(out, tuple) else (out,)
            return out + (out[-1].T,)
        return wrapped

    h1, h1_t = _tile_call("norm_mod_mix", with_t(_norm_mod), (1, nr), [xs, g_mix, sc1, sh1],
                          [_t(tr, d), _v(d), _v(d), _v(d)], [_sds((l, d), BF16), _sds((d, l), BF16)],
                          [_t(tr, d), _tt(tr, d)])
    proj = _matmul("proj_in", h1, full['w_in'], "nn", tn=1280)

    def heads(z, n):
        return z.reshape(l, n, HEAD_DIM).transpose(1, 0, 2)

    qh = heads(proj[:, :attn_w], hq)
    kh = heads(proj[:, off_k:off_k + kv_w], N_KV_HEADS)
    vh = heads(proj[:, off_v:off_v + kv_w], N_KV_HEADS)
    sinks3 = jnp.repeat(attn_sinks.reshape(N_KV_HEADS, qpk), ATT_BLOCK, axis=1)[..., None]
    o_h = _attn_fwd(qh, kh, vh, sinks3)
    o2 = o_h.transpose(1, 0, 2).reshape(l, attn_w)

    gn = (n_groups, SSM_N)
    pgn = (SSM_P, n_groups, SSM_N)
    a_re, a_im, log_dt = ssm_a_re[0], ssm_a_im[0], ssm_log_dt[0].reshape(n_groups, 1)
    b_re, b_im = ssm_b_re[0].transpose(2, 0, 1), ssm_b_im[0].transpose(2, 0, 1)
    disc_ins = [a_re, a_im, log_dt, b_re, b_im]
    disc_specs = [_full_spec(gn), _full_spec(gn), _full_spec((n_groups, 1)), _full_spec(pgn), _full_spec(pgn)]
    lam_r, lam_i, bb_r, bb_i = _tile_call(
        "s5_discretise", _s5_disc_fn, (1,), disc_ins, disc_specs,
        [_sds(gn), _sds(gn), _sds(pgn), _sds(pgn)],
        [_full_spec(gn), _full_spec(gn), _full_spec(pgn), _full_spec(pgn)])

    def tiles_gpn(z):
        return z.reshape(SSM_P, nj, TILE_GROUPS, SSM_N).transpose(1, 2, 0, 3)

    bd = jnp.concatenate([_block_diag(tiles_gpn(bb_r)), _block_diag(tiles_gpn(bb_i))], axis=2).astype(BF16)
    c_r = ssm_c_re[0].reshape(nj, TILE_GROUPS, SSM_P, SSM_N).transpose(0, 1, 3, 2)
    c_i = (-ssm_c_im[0]).reshape(nj, TILE_GROUPS, SSM_P, SSM_N).transpose(0, 1, 3, 2)
    cbd = jnp.concatenate([_block_diag(c_r), _block_diag(c_i)], axis=1).astype(BF16)
    bdt, cbdt = bd.transpose(0, 2, 1), cbd.transpose(0, 2, 1)
    lam = jnp.stack([lam_r.reshape(nj, TILE_STATES), lam_i.reshape(nj, TILE_STATES)], axis=1)
    dvec = ssm_d[0].reshape(nj, 1, LANES)
    t_len = _pick(l, 512, 8)
    y, hst = _s5_fwd(proj, off_u, bd, cbd, lam, dvec, t_len)

    tcs, trg = _pick(ssm_w, 1024), _pick(l, 512, 8)
    gy = _tile_call("gelu", lambda v: jax.nn.gelu(v), (ssm_w // tcs, l // trg), [y], [_t(trg, tcs)],
                    [_sds((l, ssm_w), BF16)], [_t(trg, tcs)])[0]
    full.update(zip(mixer_w, _exchange_wait("gather_mixer_wait", h_mixer, gy)))
    full['w_out'] = full['w_out'].reshape(row_sharded['w_out'])
    full['w_attn_proj'] = full['w_attn_proj'].transpose(1, 0, 2).reshape(attn_w, d)
    full['w_ssm_glu'] = full['w_ssm_glu'].transpose(1, 0, 2).reshape(ssm_w, 2 * d)
    glu = _matmul("ssm_glu", gy, full['w_ssm_glu'], "nn")

    tcd = 256 if d % 256 == 0 and off_ga % 256 == 0 else LANES
    assert d % tcd == 0 and off_ga % tcd == 0 and off_gs % tcd == 0
    gate_ins = [(glu, 0), (glu, d), (proj, off_ga), (proj, off_gs)]

    def mix_epilogue(at, ga_, gb_, pa, ps):
        return at, _mix_fn(ga_, gb_, at, pa, ps)

    attn, mixed = _matmul("attn_proj_gate_mix", o2, full['w_attn_proj'], "nn", tn=tcd,
                          epilogue=(mix_epilogue, gate_ins, [(F32, False), (BF16, False)]))
    def res_norm_fn(xv, mo, g1v, gv, scv, shv):
        x2v = xv + g1v * mo
        return x2v, _norm_mod(x2v, gv, scv, shv)

    def res_norm_epilogue(mo, xv, g1v, gv, scv, shv):
        x2v, h2v = res_norm_fn(xv, mo, g1v, gv, scv, shv)
        return mo, x2v, h2v, h2v.T

    mixout, x2, h2, h2_t = _matmul(
        "mix_out_residual_norm_mod_ffn", mixed, full['w_out'], "nn", tm=256, tn=d,
        epilogue=(res_norm_epilogue, [xs, g1, g_ffn, sc2, sh2], [(F32, False), (F32, False), (BF16, False), (BF16, True)]))
    full['w_ffn_up'], = _exchange_wait("gather_ffn_up_wait", h_up, h2)
    up = _matmul("ffn_up", h2, full['w_ffn_up'], "nn", out_dtype=BF16, tn=1408)

    tcf, trc = _pick(dff, 1408), _pick(l, 512, 8)
    assert dff % tcf == 0
    ncf = dff // tcf

    taps = [conv_w[i:i + 1] for i in range(3)]

    def conv_gate(gp, gp_prev, w0, w1, w2, bv):
        gp = gp.astype(F32)
        prev = jnp.where(pl.program_id(1) == 0, 0.0, 1.0) * gp_prev.astype(F32)
        ext = jnp.concatenate([prev, gp], axis=0)
        m1 = pltpu.roll(ext, 1, 0)[HALO:]
        m2 = pltpu.roll(ext, 2, 0)[HALO:]
        return w0 * m2 + w1 * m1 + w2 * gp + bv, m1, m2

    def convglu_fn(gp, gp_prev, val, w0, w1, w2, bv):
        gate, _, _ = conv_gate(gp, gp_prev, w0, w1, w2, bv)
        return gate * jax.nn.sigmoid(gate) * val.astype(F32)

    act, act_t = _tile_call("conv_swiglu", with_t(convglu_fn), (ncf, l // trc), [up, up, up] + taps + [conv_b],
                            [_t(trc, tcf), _prev_rows(trc, tcf), _t(trc, tcf, dff)] + [_v(tcf)] * 4,
                            [_sds((l, dff), BF16), _sds((dff, l), BF16)], [_t(trc, tcf), _tt(trc, tcf)])
    full['w_ffn_down'] = _exchange_wait("gather_ffn_down_wait", h_down, act)[0].reshape(row_sharded['w_ffn_down'])
    ffn = _matmul("ffn_down", act, full['w_ffn_down'], "nn", tm=512)

    def final_fn(x2v, fv, g2v, gv, tv):
        rows = x2v.shape[0]

        def loss_of(x2a, fa, g2a, ga):
            out = _rms(x2a + g2a * fa, ga)
            err = out - tv
            return 0.5 * _colsum(jnp.mean(err * err, axis=-1, keepdims=True))

        loss, vjp = jax.vjp(loss_of, x2v, fv, _bc(g2v, rows), _bc(gv, rows))
        dx3, dffn, dg2, dgf = vjp(jnp.ones((1, 1), F32))
        return jnp.broadcast_to(loss, (1, LANES)), dx3, dffn, _colsum(dg2), _colsum(dgf)

    loss_p, dx3, dffn, dg2, dg_fin = _tile_call(
        "loss_final_norm", final_fn, (1, nrh), [x2, ffn, g2, g_fin, tgt],
        [_t(trh, d), _t(trh, d), _v(d), _v(d), _t(trh, d)],
        [_sds((1, LANES)), _sds((l, d)), _sds((l, d), BF16), _sds((1, d)), _sds((1, d))],
        [_v(LANES), _t(trh, d), _t(trh, d), _v(d), _v(d)], acc=(0, 3, 4))
    loss = lax.psum(loss_p[0, 0], ("x", "y", "c"))

    dact = _matmul("d_act", dffn, full['w_ffn_down'], "nt", out_dtype=BF16, tn=1408, dep=loss.reshape(1, 1))
    gd, gd16, pending = {}, {}, []
    dw_down, dw_down16 = _matmul("dw_ffn_down", act_t, dffn, "nn", tm=512, also_bf16=True)
    gd['w_ffn_down'], gd16['w_ffn_down'] = [z.reshape((N_DEV,) + w_ffn_down.shape[1:]) for z in (dw_down, dw_down16)]
    handle, tok = _exchange_start("grad_ffn_down_start", [gd16['w_ffn_down']], False, loss.reshape(1, 1))
    pending.append((['w_ffn_down'], handle))
    conv_b_bwd = conv_b + tok[0:1, 0:1]

    def convglu_bwd_fn(gp, gp_prev, gp_next, val, val_next, da, da_next, w0, w1, w2, bv):
        rows = gp.shape[0]
        i = pl.program_id(1)
        gp, val, da = gp.astype(F32), val.astype(F32), da.astype(F32)
        prev = jnp.where(i == 0, 0.0, 1.0) * gp_prev.astype(F32)
        more = jnp.where(i == pl.num_programs(1) - 1, 0.0, 1.0)
        ext = jnp.concatenate([prev, gp, gp_next.astype(F32)], axis=0)
        cur = ext[HALO:]
        m1 = pltpu.roll(ext, 1, 0)[HALO:]
        m2 = pltpu.roll(ext, 2, 0)[HALO:]
        gate = w0 * m2 + w1 * m1 + w2 * cur + bv
        sg = jax.nn.sigmoid(gate)
        val_e = jnp.concatenate([val, val_next.astype(F32)], axis=0)
        da_e = jnp.concatenate([da, more * da_next.astype(F32)], axis=0)
        dgate = da_e * val_e * (sg * (1.0 + gate * (1.0 - sg)))
        p1 = pltpu.roll(dgate, rows + HALO - 1, 0)[:rows]
        p2 = pltpu.roll(dgate, rows + HALO - 2, 0)[:rows]
        dg = dgate[:rows]
        dgp = w2 * dg + w1 * p1 + w0 * p2
        dval = da * (gate[:rows] * sg[:rows])
        return (jnp.stack([dgp, dval], axis=0), _colsum(dg), _colsum(dg * m2[:rows]), _colsum(dg * m1[:rows]),
                _colsum(dg * gp))

    dup, dconv_b, dcw0, dcw1, dcw2 = _tile_call(
        "conv_swiglu_bwd", convglu_bwd_fn, (ncf, nr), [up, up, up, up, up, dact, dact] + taps + [conv_b_bwd],
        [_t(tr, tcf), _prev_rows(tr, tcf), _next_rows(tr, tcf, l), _t(tr, tcf, dff), _next_rows(tr, tcf, l, dff),
         _t(tr, tcf), _next_rows(tr, tcf, l)] + [_v(tcf)] * 4,
        [_sds((2, l, dff), BF16)] + [_sds((1, dff))] * 4, [_st(tr, tcf)] + [_v(tcf)] * 4, acc=(1, 2, 3, 4))
    dh2 = _matmul("d_h2", dup, full['w_ffn_up'], "nt", tm=512, fold=4)
    gd['w_ffn_up'], gd16['w_ffn_up'] = _matmul("dw_ffn_up", h2_t, dup, "nn", tm=512, tn=1408, out_stack=N_DEV, also_bf16=True)
    handle, tok = _exchange_start("grad_ffn_up_start", [gd16['w_ffn_up']], False, gd['w_ffn_up'])
    pending.append((['w_ffn_up'], handle))
    g_ffn_bwd = g_ffn + tok[0:1, 0:1]

    def res_norm_bwd_fn(xv, mo, g1v, gv, scv, shv, dhv, dxv):
        rows = xv.shape[0]
        _, vjp = jax.vjp(res_norm_fn, xv, mo, _bc(g1v, rows), _bc(gv, rows), _bc(scv, rows), _bc(shv, rows))
        dx, dmo, dg1v, dgv, dscv, dshv = vjp((dxv, dhv))
        return dx, dmo, _colsum(dg1v), _colsum(dgv), _colsum(dscv), _colsum(dshv)

    dx2, dmixout, dg1, dg_ffn, dsc2, dsh2 = _tile_call(
        "residual_norm_mod_ffn_bwd", res_norm_bwd_fn, (1, nrh), [xs, mixout, g1, g_ffn_bwd, sc2, sh2, dh2, dx3],
        [_t(trh, d), _t(trh, d), _v(d), _v(d), _v(d), _v(d), _t(trh, d), _t(trh, d)],
        [_sds((l, d)), _sds((l, d), BF16)] + [_sds((1, d))] * 4,
        [_t(trh, d), _t(trh, d)] + [_v(d)] * 4, acc=(2, 3, 4, 5))

    def mix_bwd_epilogue(dm, ga_, gb_, pa, ps, at):
        _, vjp = jax.vjp(_mix_fn, ga_, gb_, at, pa, ps)
        da, db, dat, dpa, dps = vjp(dm)
        return jnp.stack([da, db], axis=0), dat, dpa, dps

    dglu, dattn, dga, dgs = _matmul(
        "d_mixed_gate_mix_bwd", dmixout, full['w_out'], "nt", tn=tcd,
        epilogue=(mix_bwd_epilogue, gate_ins + [attn], [(BF16, 'pair')] + [(BF16, False)] * 3))
    dw_out, dw_out16 = _matmul("dw_out", mixed, dmixout, "tn", also_bf16=True)
    gd['w_out'], gd16['w_out'] = [z.reshape((N_DEV,) + w_out.shape[1:]) for z in (dw_out, dw_out16)]

    def gelu_bwd_epilogue(dgy, yv):
        _, vjp = jax.vjp(lambda z: jax.nn.gelu(z), yv)
        return (vjp(dgy)[0],)

    dy, = _matmul("d_gelu_y_gelu_bwd", dglu, full['w_ssm_glu'], "nt", epilogue=(gelu_bwd_epilogue, [y], [(F32, False)]))
    gd['w_ssm_glu'], gd16['w_ssm_glu'] = _matmul("dw_ssm_glu", gy, dglu, "tn", out_stack=N_DEV, also_bf16=True)
    du, dbd, dcbdt, dlam, dd_tiles = _s5_bwd(proj, off_u, dy, hst, bd, bdt, cbdt, lam, dvec, t_len)

    def gpn_of(z):
        return z.transpose(2, 0, 1, 3).reshape(pgn)

    dbb_r = gpn_of(_diag_blocks(dbd[:, :, :TILE_STATES], SSM_P))
    dbb_i = gpn_of(_diag_blocks(dbd[:, :, TILE_STATES:], SSM_P))
    dc_re = _diag_blocks(dcbdt[:, :, :TILE_STATES], SSM_P).reshape(n_groups, SSM_P, SSM_N)
    dc_im = _diag_blocks(dcbdt[:, :, TILE_STATES:], SSM_P).reshape(n_groups, SSM_P, SSM_N)
    dlam_r, dlam_i = dlam[:, 0].reshape(gn), dlam[:, 1].reshape(gn)

    def disc_bwd_fn(ar, ai, ld, br, bi, dlr, dli, dbr, dbi):
        _, vjp = jax.vjp(_s5_disc_fn, ar, ai, ld, br, bi)
        return vjp((dlr, dli, dbr, dbi))

    da_re, da_im, dlog_dt, db_re, db_im = _tile_call(
        "s5_discretise_bwd", disc_bwd_fn, (1,), disc_ins + [dlam_r, dlam_i, dbb_r, dbb_i],
        disc_specs + [_full_spec(gn), _full_spec(gn), _full_spec(pgn), _full_spec(pgn)],
        [_sds(gn), _sds(gn), _sds((n_groups, 1)), _sds(pgn), _sds(pgn)], disc_specs)

    do2 = _matmul("d_attn_heads", dattn, full['w_attn_proj'], "nt")
    gd['w_attn_proj'], gd16['w_attn_proj'] = _matmul("dw_attn_proj", o2, dattn, "tn", out_stack=N_DEV, also_bf16=True)
    handle, tok = _exchange_start("grad_mixer_start", [gd16[k] for k in mixer_w], False, gd['w_attn_proj'])
    pending.append((mixer_w, handle))
    do_h = heads(do2.astype(BF16), hq)
    dq_h, dk_h, dv_h, dsink = _attn_bwd(qh, kh, vh, sinks3 + tok[0:1, 0:1], do_h)

    def unheads(z):
        return z.transpose(1, 0, 2).reshape(l, z.shape[0] * HEAD_DIM)

    early = ['attn_sinks', 'ssm_a_re', 'ssm_a_im', 'ssm_log_dt', 'ssm_b_re', 'ssm_b_im', 'ssm_c_re', 'ssm_c_im',
             'ssm_d', 'norm_ffn_g', 'ffn_conv_b', 'final_g']
    early_grads = {
        'attn_sinks': dsink[:, 0, 0], 'ssm_a_re': da_re, 'ssm_a_im': da_im, 'ssm_log_dt': dlog_dt,
        'ssm_b_re': db_re.transpose(1, 2, 0), 'ssm_b_im': db_im.transpose(1, 2, 0), 'ssm_c_re': dc_re,
        'ssm_c_im': dc_im, 'ssm_d': dd_tiles, 'norm_ffn_g': dg_ffn, 'ffn_conv_b': dconv_b, 'final_g': dg_fin}
    ge_pack, e_offs = _pack([jnp.concatenate([dg1, dsh2, dsc2, dg2], axis=1)] + [early_grads[k] for k in early],
                            LANES, 8)
    h_early, tok = _exchange_start("gather_small_early_start", [ge_pack], True, dsink)

    dproj = jnp.concatenate([unheads(dq_h), unheads(dk_h), unheads(dv_h), du, dga, dgs], axis=1)
    dw_in, dw_in16 = _matmul("dw_in", h1_t, dproj, "nn", tm=512, tn=1280, also_bf16=True, dep=tok)
    dcw = jnp.concatenate([dcw0, dcw1, dcw2], axis=0)
    shard_in, shard_cw = w_in.shape[1:], ffn_conv_w.shape[1:]
    gd16['w_in'] = dw_in16.reshape(shard_in[0], N_DEV, shard_in[1]).transpose(1, 0, 2)
    own_in = lax.dynamic_slice_in_dim(dw_in, idx * shard_in[1], shard_in[1], axis=1)[None]
    gd['ffn_conv_w'] = dcw.reshape(shard_cw[0], N_DEV, shard_cw[1]).transpose(1, 0, 2)
    gd16['ffn_conv_w'] = gd['ffn_conv_w'].astype(BF16)
    handle, tok = _exchange_start("grad_in_start", [gd16['w_in'], gd16['ffn_conv_w']], False, dw_in)
    pending.append((['w_in', 'ffn_conv_w'], handle))
    dh1 = _matmul("d_h1", dproj, full['w_in'], "nt", tm=512, dep=tok)

    def norm_bwd_fn(xv, gv, scv, shv, dhv, dxv):
        rows = xv.shape[0]
        _, vjp = jax.vjp(_norm_mod, xv, _bc(gv, rows), _bc(scv, rows), _bc(shv, rows))
        dx, dgv, dscv, dshv = vjp(dhv)
        return dx + dxv, _colsum(dgv), _colsum(dscv), _colsum(dshv)

    grad_x, dg_mix, dsc1, dsh1 = _tile_call(
        "norm_mod_mix_bwd", norm_bwd_fn, (1, nrh), [xs, g_mix, sc1, sh1, dh1, dx2],
        [_t(trh, d), _v(d), _v(d), _v(d), _t(trh, d), _t(trh, d)],
        [_sds((l, d))] + [_sds((1, d))] * 3, [_t(trh, d)] + [_v(d)] * 3, acc=(1, 2, 3))

    gl_pack, l_offs = _pack([jnp.concatenate([dsh1, dsc1], axis=1), dg_mix], LANES, 8)
    h_late, tok = _exchange_start("gather_small_late_start", [gl_pack], True, grad_x)

    sharded = big + ['ffn_conv_w']
    sharded_out = {}

    def finish(group, handle, after):
        for k, parts in zip(group, _exchange_wait("grad_" + group[0] + "_wait", handle, after)):
            own_src, own_at = (own_in, 0 * idx) if k == 'w_in' else (gd[k], idx)
            sharded_out[k] = _adamw_sharded("adamw_" + k, parts, own_src, given[k][0], given['m_' + k][0],
                                            given['v_' + k][0], jnp.stack([idx, own_at]).astype(jnp.int32))

    for group, handle in pending[:-1]:
        finish(group, handle, tok)
    done = functools.reduce(lambda p, q: p + q, [sharded_out[k][1][0:1, 0:1] for g_, _ in pending[:-1] for k in g_])
    finish(*pending[-1], done)

    ge_all, = _exchange_wait("gather_small_early_wait", h_early, done)
    gl_all, = _exchange_wait("gather_small_late_wait", h_late, sharded_out['w_in'][1])
    gs_all = jnp.concatenate([ge_all, gl_all], axis=1)
    rows_e = ge_pack.shape[0]

    def small_pack(prefix):
        ab = given[prefix + 'ada_b']
        p_early, _ = _pack([ab[:, 2 * d:]] + [given[prefix + k] for k in early], LANES, 8)
        p_late, _ = _pack([ab[:, :2 * d], given[prefix + 'norm_mix_g']], LANES, 8)
        return jnp.concatenate([p_early, p_late], axis=0)

    small_out = _adamw("adamw_replicated", gs_all, small_pack(''), small_pack('m_'), small_pack('v_'))

    dmod_all = jnp.concatenate([_unpack(gl_all, l_offs[0], (2 * d,), lead=(N_DEV,)),
                                _unpack(ge_all, e_offs[0], (4 * d,), lead=(N_DEV,))], axis=1)
    dmod_mine = lax.dynamic_slice_in_dim(dmod_all, idx * mod_n, mod_n, axis=1)
    kpad = LANES - N_DEV
    cond_t = jnp.pad(cond_all.T, ((0, 0), (0, kpad)))
    dmod_pad = jnp.pad(dmod_mine, ((0, kpad), (0, 0)))
    g_ada_w = _matmul("dw_ada", cond_t, dmod_pad, "nn")
    ada_out = _adamw("adamw_ada_w", g_ada_w[None], ada_w[0], m_ada_w[0], v_ada_w[0])

    results = [{}, {}, {}, {}]
    for which in range(4):
        out = small_out[which]
        results[which]['ada_b'] = jnp.concatenate([_unpack(out, rows_e + l_offs[0], (1, 2 * d)),
                                                   _unpack(out, e_offs[0], (1, 4 * d))], axis=1)
        results[which]['norm_mix_g'] = _unpack(out, rows_e + l_offs[1], norm_mix_g.shape)
        for k, off in zip(early, e_offs[1:]):
            results[which][k] = _unpack(out, off, given[k].shape)
        for k in sharded:
            results[which][k] = sharded_out[k][which][None]
        results[which]['ada_w'] = ada_out[which][None]
    outs = [loss, grad_x[None]]
    for which in range(4):
        outs += [results[which][k] for k in names]
    return tuple(outs)
```

```python
import functools
import math

import jax
import jax.numpy as jnp
from jax import lax
from jax.experimental import pallas as pl
from jax.experimental.pallas import tpu as pltpu

F32, BF16 = jnp.float32, jnp.bfloat16
MESH = pl.DeviceIdType.MESH
N_DEV = 8

HEAD_DIM = 64
N_KV_HEADS = 2
ATT_BLOCK = 128
NEG_INF = -1e30
SSM_P = 16
SSM_N = 64
LANES = 128
TILE_GROUPS = LANES // SSM_P
TILE_STATES = TILE_GROUPS * SSM_N
RMS_EPS = 1e-6
ADAM_LR, ADAM_B1, ADAM_B2, ADAM_EPS, ADAM_WD, ADAM_STEP = 0.001, 0.9, 0.999, 1e-08, 0.01, 10
VMEM_LIMIT = 56 * 1024 * 1024
MATMUL_VMEM_BUDGET = 44 * 1024 * 1024


def _params(n_axes):
    return pltpu.CompilerParams(dimension_semantics=("arbitrary",) * n_axes, vmem_limit_bytes=VMEM_LIMIT)


def _pick(dim, pref, align=128):
    if dim <= align:
        return dim
    t = (min(pref, dim) // align) * align
    while t > align and dim % t:
        t -= align
    assert dim % t == 0, (dim, pref, align)
    return t


def _dev():
    return lax.axis_index("x"), lax.axis_index("y"), lax.axis_index("c")


def _tile_call(name, fn, grid, ins, in_specs, out_shapes, out_specs, acc=()):
    n_in, n_out = len(ins), len(out_shapes)
    acc_axis = len(grid) - 1

    def body(*refs):
        vals = fn(*[r[...] for r in refs[:n_in]])
        if not isinstance(vals, (tuple, list)):
            vals = (vals,)
        assert len(vals) == n_out
        for i, (r, v) in enumerate(zip(refs[n_in:], vals)):
            v = v.astype(r.dtype)
            if i in acc:
                first = pl.program_id(acc_axis) == 0

                @pl.when(first)
                def _():
                    r[...] = v

                @pl.when(jnp.logical_not(first))
                def _():
                    r[...] += v
            else:
                r[...] = v

    return pl.pallas_call(
        body, grid=grid, in_specs=in_specs, out_specs=out_specs, out_shape=out_shapes, name=name,
        compiler_params=_params(len(grid)),
    )(*ins)


def _t(tr, tc, off=0):
    return pl.BlockSpec((tr, tc), lambda j, i: (i, j + off // tc))


def _tt(tr, tc):
    return pl.BlockSpec((tc, tr), lambda j, i: (j, i))


def _v(tc, off=0, rows=1):
    return pl.BlockSpec((rows, tc), lambda j, i: (0, j + off // tc))


HALO = 16


def _prev_rows(tr, tc, off=0):
    return pl.BlockSpec((HALO, tc), lambda j, i: (jnp.maximum(i * (tr // HALO) - 1, 0), j + off // tc))


def _next_rows(tr, tc, nrows, off=0):
    return pl.BlockSpec((HALO, tc),
                        lambda j, i: (jnp.minimum((i + 1) * (tr // HALO), nrows // HALO - 1), j + off // tc))


def _st(tr, tc):
    return pl.BlockSpec((2, tr, tc), lambda j, i: (0, i, j))


def _bc(v, rows):
    return jnp.broadcast_to(v, (rows, v.shape[-1]))


def _colsum(v):
    return jnp.sum(v, axis=0, keepdims=True)


def _matmul(name, a, b, mode, out_dtype=F32, tm=1024, tn=1024, tk=None, out_stack=None, also_bf16=False, dep=None,
            fold=1, epilogue=None):
    def dims(z):
        return (z.shape[-2], z.shape[-1] * (z.shape[0] if z.ndim == 3 else 1))

    ar, ac = dims(a)
    br, bc = dims(b)
    if mode == "nn":
        m, k, n = ar, ac, bc
        assert br == k
    elif mode == "nt":
        m, k, n = ar, ac, br
        assert bc == k
    else:
        m, k, n = ac, ar, bc
        assert br == k
    m_lim, k_lim, n_lim = [m], [k], [n]
    if a.ndim == 3:
        (m_lim if mode == "tn" else k_lim).append(a.shape[-1])
    if b.ndim == 3:
        (k_lim if mode == "nt" else n_lim).append(b.shape[-1])
    if out_stack:
        n_lim.append(n // out_stack)
    tm = _pick(functools.reduce(math.gcd, m_lim), tm)
    tn = _pick(functools.reduce(math.gcd, n_lim), tn)
    k_unit = functools.reduce(math.gcd, k_lim)
    if tk is None:
        sa, sb, so = a.dtype.itemsize, b.dtype.itemsize, jnp.dtype(out_dtype).itemsize + (2 if also_bf16 else 0)
        fits = [t for t in range(LANES, k_unit + 1, LANES) if k_unit % t == 0 and
                2 * t * (tm * sa + tn * sb) + tm * tn * (2 * so + (4 if t < k else 0)) <= MATMUL_VMEM_BUDGET]
        tk = max(fits) if fits else _pick(k_unit, 512)
    else:
        tk = _pick(k_unit, tk)
    assert (k // tk) % fold == 0
    nk = k // (tk * fold)

    def spec(z, brows, bcols, ridx, cidx):
        if z.ndim == 3:
            per = z.shape[-1] // bcols
            return pl.BlockSpec((None, brows, bcols),
                                lambda i, j, kk: (cidx(i, j, kk) // per, ridx(i, j, kk), cidx(i, j, kk) % per))
        return pl.BlockSpec((brows, bcols), lambda i, j, kk: (ridx(i, j, kk), cidx(i, j, kk)))

    gi = lambda i, j, kk: i
    gj = lambda i, j, kk: j
    a_specs, b_specs = [], []
    for f in range(fold):
        gk = lambda i, j, kk, f=f: fold * kk + f
        if mode == "nn":
            a_specs.append(spec(a, tm, tk, gi, gk))
            b_specs.append(spec(b, tk, tn, gk, gj))
            dn = (((1,), (0,)), ((), ()))
        elif mode == "nt":
            a_specs.append(spec(a, tm, tk, gi, gk))
            b_specs.append(spec(b, tn, tk, gj, gk))
            dn = (((1,), (1,)), ((), ()))
        else:
            a_specs.append(spec(a, tk, tm, gk, gi))
            b_specs.append(spec(b, tk, tn, gk, gj))
            dn = (((0,), (0,)), ((), ()))

    epi_fn, epi_ins, epi_outs = epilogue if epilogue else (None, [], [])
    n_out = len(epi_outs) if epilogue else (2 if also_bf16 else 1)

    deps = [] if dep is None else [dep]

    def body(*refs):
        a_refs, b_refs = refs[:fold], refs[fold:2 * fold]
        e_refs = refs[2 * fold:2 * fold + len(epi_ins)]
        rest = refs[2 * fold + len(epi_ins) + len(deps):]
        o_refs, acc = rest[:n_out], rest[n_out:]
        part = None
        for a_ref, b_ref in zip(a_refs, b_refs):
            one = lax.dot_general(a_ref[...].astype(BF16), b_ref[...].astype(BF16), dn, preferred_element_type=F32)
            part = one if part is None else part + one

        def emit(val):
            vals = epi_fn(val, *[r[...] for r in e_refs]) if epilogue else [val] * n_out
            for o_ref, v in zip(o_refs, vals):
                o_ref[...] = v.astype(o_ref.dtype)

        if nk == 1:
            emit(part)
            return
        acc_ref, = acc
        kk = pl.program_id(2)

        @pl.when(kk == 0)
        def _():
            acc_ref[...] = part

        @pl.when(kk > 0)
        def _():
            acc_ref[...] += part

        @pl.when(kk == nk - 1)
        def _():
            emit(acc_ref[...])

    if out_stack:
        per = (n // out_stack) // tn
        out_spec = pl.BlockSpec((None, tm, tn), lambda i, j, kk: (j // per, i, j % per))
        shape = (out_stack, m, n // out_stack)
    else:
        out_spec = pl.BlockSpec((tm, tn), lambda i, j, kk: (i, j))
        shape = (m, n)
    if epilogue:
        assert not out_stack and not also_bf16
        kinds = {False: (pl.BlockSpec((tm, tn), lambda i, j, kk: (i, j)), (m, n)),
                 True: (pl.BlockSpec((tn, tm), lambda i, j, kk: (j, i)), (n, m)),
                 'pair': (pl.BlockSpec((2, tm, tn), lambda i, j, kk: (0, i, j)), (2, m, n)),
                 'colsum': (pl.BlockSpec((None, 1, tn), lambda i, j, kk: (i, 0, j)), (m // tm, 1, n))}
        out_specs = [kinds[t][0] for _, t in epi_outs]
        out_shapes = [jax.ShapeDtypeStruct(kinds[t][1], dt) for dt, t in epi_outs]
    else:
        out_specs = [out_spec] * n_out
        out_shapes = [jax.ShapeDtypeStruct(shape, dt) for dt in [out_dtype, BF16][:n_out]]
    e_pairs = [z if isinstance(z, tuple) else (z, 0) for z in epi_ins]
    assert all(off % tn == 0 for _, off in e_pairs)
    e_specs = [pl.BlockSpec((1, tn) if z.shape[0] == 1 else (tm, tn),
                            lambda i, j, kk, ob=off // tn, row=z.shape[0] == 1: (0 if row else i, j + ob))
               for z, off in e_pairs]
    res = pl.pallas_call(
        body, grid=(m // tm, n // tn, nk),
        in_specs=a_specs + b_specs + e_specs + [pl.BlockSpec(memory_space=pl.ANY)] * len(deps),
        out_specs=out_specs, out_shape=out_shapes,
        scratch_shapes=[pltpu.VMEM((tm, tn), F32)] if nk > 1 else [], name=name, compiler_params=_params(3),
    )(*[a] * fold, *[b] * fold, *[z for z, _ in e_pairs], *deps)
    return res if (also_bf16 or epilogue) else res[0]


def _all_gather(name, arrs, dep=None):
    n = len(arrs)
    deps = [] if dep is None else [dep]

    def body(*refs):
        ins, outs = refs[:n], refs[n + len(deps):2 * n + len(deps)]
        send_sems, recv_sems, local_sems = refs[2 * n + len(deps):]
        x, y, c = _dev()
        me, sib = (x, y, c), (x, y, 1 - c)
        chips = [(1 - x, y), (x, 1 - y), (1 - x, 1 - y)]

        def slot(p):
            return 4 * p[0] + 2 * p[1] + p[2]

        def copy(a, k, block, to, src=None):
            dst = outs[a].at[slot(block)]
            return pltpu.make_async_remote_copy(
                src_ref=dst if src is None else src, dst_ref=dst,
                send_sem=send_sems.at[7 * a + k], recv_sem=recv_sems.at[7 * a + k],
                device_id=to, device_id_type=MESH)

        mine = [pltpu.make_async_copy(ins[a], outs[a].at[slot(me)], local_sems.at[a]) for a in range(n)]
        for cp in mine:
            cp.start()
        first = []
        for a in range(n):
            first.append(copy(a, 0, me, sib, src=ins[a]))
            first += [copy(a, 1 + j, me, (*chip, c), src=ins[a]) for j, chip in enumerate(chips)]
        for cp in first:
            cp.start()
        passed = []
        for j, chip in enumerate(chips):
            for a in range(n):
                copy(a, 1 + j, (*chip, c), me).wait_recv()
                cp = copy(a, 4 + j, (*chip, c), sib)
                cp.start()
                passed.append(cp)
        for a in range(n):
            copy(a, 0, sib, me).wait_recv()
            for j, chip in enumerate(chips):
                copy(a, 4 + j, (*chip, 1 - c), me).wait_recv()
        for cp in first + passed:
            cp.wait_send()
        for cp in mine:
            cp.wait()

    any_spec = pl.BlockSpec(memory_space=pl.ANY)
    return pl.pallas_call(
        body, in_specs=[any_spec] * (n + len(deps)), out_specs=[any_spec] * n,
        out_shape=[jax.ShapeDtypeStruct((N_DEV,) + a.shape, a.dtype) for a in arrs],
        scratch_shapes=[pltpu.SemaphoreType.DMA((7 * n,)), pltpu.SemaphoreType.DMA((7 * n,)),
                        pltpu.SemaphoreType.DMA((n,))],
        name=name,
    )(*arrs, *deps)


def _grad_to_sibling(gds):
    n = len(gds)

    def body(*refs):
        g_refs, r_refs = refs[:n], refs[n:2 * n]
        send_sems, recv_sems = refs[2 * n:]
        x, y, c = _dev()
        cps = []
        for a in range(n):
            for k in range(4):
                cp = pltpu.make_async_remote_copy(
                    src_ref=g_refs[a].at[2 * k + (1 - c)], dst_ref=r_refs[a].at[k],
                    send_sem=send_sems.at[4 * a + k], recv_sem=recv_sems.at[4 * a + k],
                    device_id=(x, y, 1 - c), device_id_type=MESH)
                cp.start()
                cps.append(cp)
        for cp in cps:
            cp.wait()

    any_spec = pl.BlockSpec(memory_space=pl.ANY)
    return pl.pallas_call(
        body, in_specs=[any_spec] * n, out_specs=[any_spec] * n,
        out_shape=[jax.ShapeDtypeStruct((4,) + g.shape[1:], g.dtype) for g in gds],
        scratch_shapes=[pltpu.SemaphoreType.DMA((4 * n,)), pltpu.SemaphoreType.DMA((4 * n,))],
        name="grad_to_sibling",
    )(*gds)


def _chip_sum(name, gd, from_sib, c_arr):
    _, k, n = gd.shape
    tr = _pick(k, max(16, (1 << 20) // (4 * n)), 16)

    def body(c_ref, a_ref, b_ref, o_ref):
        o_ref[...] = (a_ref[...] + b_ref[...]).astype(o_ref.dtype)

    return pl.pallas_call(
        body,
        grid_spec=pltpu.PrefetchScalarGridSpec(
            num_scalar_prefetch=1, grid=(4, k // tr),
            in_specs=[pl.BlockSpec((1, tr, n), lambda q, i, cr: (2 * q + cr[0], i, 0)),
                      pl.BlockSpec((1, tr, n), lambda q, i, cr: (q, i, 0))],
            out_specs=pl.BlockSpec((1, tr, n), lambda q, i, cr: (q, i, 0))),
        out_shape=jax.ShapeDtypeStruct((4, k, n), BF16), name=name, compiler_params=_params(2),
    )(c_arr, gd, from_sib)


def _grad_to_chips(sums):
    n = len(sums)

    def body(*refs):
        s_refs, p_refs = refs[:n], refs[n:2 * n]
        send_sems, recv_sems, local_sems = refs[2 * n:]
        x, y, c = _dev()
        my_chip = 2 * x + y
        cps = []
        for a in range(n):
            local = pltpu.make_async_copy(s_refs[a].at[my_chip], p_refs[a].at[my_chip], local_sems.at[a])
            local.start()
            cps.append(local)
            for j, (px, py) in enumerate([(1 - x, y), (x, 1 - y), (1 - x, 1 - y)]):
                cp = pltpu.make_async_remote_copy(
                    src_ref=s_refs[a].at[2 * px + py], dst_ref=p_refs[a].at[my_chip],
                    send_sem=send_sems.at[3 * a + j], recv_sem=recv_sems.at[3 * a + j],
                    device_id=(px, py, c), device_id_type=MESH)
                cp.start()
                cps.append(cp)
        for cp in cps:
            cp.wait()

    any_spec = pl.BlockSpec(memory_space=pl.ANY)
    return pl.pallas_call(
        body, in_specs=[any_spec] * n, out_specs=[any_spec] * n,
        out_shape=[jax.ShapeDtypeStruct(s.shape, s.dtype) for s in sums],
        scratch_shapes=[pltpu.SemaphoreType.DMA((3 * n,)), pltpu.SemaphoreType.DMA((3 * n,)),
                        pltpu.SemaphoreType.DMA((n,))],
        name="grad_to_chips",
    )(*sums)


FLIPS = [(0, 0, 1), (0, 1, 0), (1, 0, 0), (0, 1, 1), (1, 0, 1), (1, 1, 0), (1, 1, 1)]
N_PEERS = len(FLIPS)
_HBM = pl.BlockSpec(memory_space=pltpu.HBM)
_SEM = pl.BlockSpec(memory_space=pltpu.SEMAPHORE)
_EFFECT = pltpu.SideEffectType.DATAFLOW_SIDE_EFFECTING


def _flip(x, y, c, f):
    return (1 - x if f[0] else x, 1 - y if f[1] else y, 1 - c if f[2] else c)


def _slot(p):
    return 4 * p[0] + 2 * p[1] + p[2]


def _exchange_copies(src_refs, land_refs, send_sems, recv_sems, gather):
    x, y, c = _dev()
    mine = _slot((x, y, c))
    cps = []
    for a, (src, land) in enumerate(zip(src_refs, land_refs)):
        for k, f in enumerate(FLIPS):
            peer = _flip(x, y, c, f)
            cps.append(pltpu.make_async_remote_copy(
                src_ref=src if gather else src.at[_slot(peer)], dst_ref=land.at[mine],
                send_sem=send_sems.at[N_PEERS * a + k], recv_sem=recv_sems.at[N_PEERS * a + k],
                device_id=peer, device_id_type=MESH))
    return cps


def _exchange_start(name, srcs, gather, after):
    n = len(srcs)
    lands = [lax.empty(((N_DEV,) + s.shape) if gather else s.shape, s.dtype) for s in srcs]

    def body(*refs):
        src_refs, land_refs = refs[:n], refs[n:2 * n]
        send_sems, recv_sems, local_sems = refs[2 * n + 1:2 * n + 4]
        token = refs[-1]
        if gather:
            x, y, c = _dev()
            for a in range(n):
                pltpu.make_async_copy(src_refs[a], land_refs[a].at[_slot((x, y, c))], local_sems.at[a]).start()
        for cp in _exchange_copies(src_refs, land_refs, send_sems, recv_sems, gather):
            cp.start()
        token[...] = jnp.zeros_like(token)

    hbm = lambda z: pltpu.HBM(z.shape, z.dtype)
    outs = pl.pallas_call(
        body, name=name,
        out_shape=(pltpu.SemaphoreType.DMA((N_PEERS * n,)), pltpu.SemaphoreType.DMA((N_PEERS * n,)),
                   pltpu.SemaphoreType.DMA((n,)), *[hbm(s) for s in srcs], *[hbm(z) for z in lands],
                   jax.ShapeDtypeStruct((8, LANES), F32)),
        in_specs=[_HBM] * (2 * n) + [pl.BlockSpec(memory_space=pl.ANY)],
        out_specs=(_SEM, _SEM, _SEM, *[_HBM] * (2 * n), pl.BlockSpec(memory_space=pltpu.VMEM)),
        input_output_aliases={i: 3 + i for i in range(2 * n)},
        compiler_params=pltpu.CompilerParams(has_side_effects=_EFFECT),
    )(*[pltpu.with_memory_space_constraint(z, pltpu.HBM) for z in list(srcs) + lands], after)
    return (outs[:3], outs[3:3 + n], outs[3 + n:3 + 2 * n], gather), outs[-1]


def _exchange_wait(name, handles, after):
    sems, srcs, lands, gather = handles
    n = len(srcs)

    def body(*refs):
        src_refs, land_refs = refs[:n], refs[n:2 * n]
        send_sems, recv_sems, local_sems = refs[2 * n:2 * n + 3]
        if gather:
            for a in range(n):
                pltpu.make_async_copy(src_refs[a], land_refs[a].at[0], local_sems.at[a]).wait()
        for cp in _exchange_copies(src_refs, land_refs, send_sems, recv_sems, gather):
            cp.wait_send()
            cp.wait_recv()

    hbm = lambda z: pltpu.HBM(z.shape, z.dtype)
    outs = pl.pallas_call(
        body, name=name, out_shape=tuple(hbm(z) for z in list(srcs) + list(lands)),
        in_specs=[_HBM] * (2 * n) + [_SEM] * 3 + [pl.BlockSpec(memory_space=pl.ANY)],
        out_specs=tuple([_HBM] * (2 * n)), input_output_aliases={i: i for i in range(2 * n)},
        compiler_params=pltpu.CompilerParams(has_side_effects=_EFFECT),
    )(*srcs, *lands, *sems, after)
    return list(outs[n:])


def _pack_rows(sizes, width, row_align):
    offs, r = [], 0
    for s in sizes:
        offs.append(r)
        r += -(-s // width)
    total = -(-r // row_align) * row_align
    return offs, total


def _pack(items, width, row_align, lead=()):
    nl = len(lead)
    sizes = [int(jnp.size(a)) // max(1, functools.reduce(lambda p, q: p * q, lead, 1)) for a in items]
    offs, total = _pack_rows(sizes, width, row_align)
    flat = []
    used = 0
    for a, s in zip(items, sizes):
        f = a.reshape(lead + (s,))
        pad = -(-s // width) * width - s
        if pad:
            f = jnp.pad(f, [(0, 0)] * nl + [(0, pad)])
        flat.append(f)
        used += s + pad
    tail = total * width - used
    if tail:
        flat.append(jnp.zeros(lead + (tail,), items[0].dtype))
    return jnp.concatenate(flat, axis=-1).reshape(lead + (total, width)), offs


def _unpack(packed, off, shape, lead=()):
    nl = len(lead)
    size = functools.reduce(lambda p, q: p * q, shape, 1)
    width = packed.shape[-1]
    rows = -(-size // width)
    blk = lax.slice_in_dim(packed, off, off + rows, axis=nl).reshape(lead + (rows * width,))
    return lax.slice_in_dim(blk, 0, size, axis=nl).reshape(lead + tuple(shape))


def _rms(x, g):
    return (x * lax.rsqrt(jnp.mean(x * x, axis=-1, keepdims=True) + RMS_EPS)) * g


def _norm_mod(x, g, sc, sh):
    return _rms(x, g) * (1.0 + sc) + sh


def _mix_fn(glu_a, glu_b, attn, ga, gs):
    return jax.nn.sigmoid(ga) * attn + jax.nn.sigmoid(gs) * (glu_a * jax.nn.sigmoid(glu_b))


def _s5_disc_fn(a_re, a_im, log_dt, b_re, b_im):
    dt = jnp.exp(log_dt)
    mag = jnp.exp(a_re * dt)
    lr, li = mag * jnp.cos(a_im * dt), mag * jnp.sin(a_im * dt)
    den = a_re * a_re + a_im * a_im
    zr = ((lr - 1.0) * a_re + li * a_im) / den
    zi = (li * a_re - (lr - 1.0) * a_im) / den
    return lr, li, zr[None] * b_re - zi[None] * b_im, zr[None] * b_im + zi[None] * b_re


def _adamw_fn(w, g, m, v):
    m = ADAM_B1 * m + (1.0 - ADAM_B1) * g
    v = ADAM_B2 * v + (1.0 - ADAM_B2) * jnp.square(g)
    m_hat = m / (1.0 - ADAM_B1 ** ADAM_STEP)
    v_hat = v / (1.0 - ADAM_B2 ** ADAM_STEP)
    delta = -ADAM_LR * (m_hat / (jnp.sqrt(v_hat) + ADAM_EPS) + ADAM_WD * w)
    return delta, m, v


def _adamw(name, parts, w, m, v):
    p, r, c = parts.shape
    tr = _pick(r, max(8, (1 << 21) // (4 * c * max(p, 2))), 8)

    def fn(pv, wv, mv, vv):
        g = pv[0]
        for i in range(1, p):
            g = g + pv[i]
        d, m2, v2 = _adamw_fn(wv, g, mv, vv)
        return g, d, m2, v2

    spec = pl.BlockSpec((tr, c), lambda i: (i, 0))
    return _tile_call(
        name, fn, (r // tr,), [parts, w, m, v],
        [pl.BlockSpec((p, tr, c), lambda i: (0, i, 0)), spec, spec, spec],
        [jax.ShapeDtypeStruct((r, c), F32)] * 4, [spec] * 4)


def _adamw_sharded(name, parts, own_src, w, m, v, place):
    _, k, n = parts.shape
    tr = _pick(k, max(16, (1 << 19) // (4 * n)), 16)

    def body(pl_ref, p_ref, a_ref, w_ref, m_ref, v_ref, g_ref, d_ref, m2_ref, v2_ref):
        own = a_ref[0]
        g = None
        for q in range(N_DEV):
            term = jnp.where(pl_ref[0] == q, own, p_ref[q].astype(F32))
            g = term if g is None else g + term
        d, m2, v2 = _adamw_fn(w_ref[...], g, m_ref[...], v_ref[...])
        g_ref[...] = g
        d_ref[...] = d
        m2_ref[...] = m2
        v2_ref[...] = v2

    spec = pl.BlockSpec((tr, n), lambda i, pr: (i, 0))
    return pl.pallas_call(
        body,
        grid_spec=pltpu.PrefetchScalarGridSpec(
            num_scalar_prefetch=1, grid=(k // tr,),
            in_specs=[pl.BlockSpec((N_DEV, tr, n), lambda i, pr: (0, i, 0)),
                      pl.BlockSpec((1, tr, n), lambda i, pr: (pr[1], i, 0)),
                      spec, spec, spec],
            out_specs=[spec] * 4),
        out_shape=[jax.ShapeDtypeStruct((k, n), F32)] * 4, name=name, compiler_params=_params(1),
    )(place, parts, own_src, w, m, v)


def _attn_mask(n, rows):
    qi = lax.broadcasted_iota(jnp.int32, (rows, 2 * ATT_BLOCK), 0) & (ATT_BLOCK - 1)
    kj = lax.broadcasted_iota(jnp.int32, (rows, 2 * ATT_BLOCK), 1)
    rel = qi + ATT_BLOCK - kj
    return (rel >= 0) & (rel < ATT_BLOCK) & ((kj >= ATT_BLOCK) | (n > 0))


def _attn_probs(q, k, sink, mask):
    s = lax.dot_general(q, k, (((1,), (1,)), ((), ())), preferred_element_type=F32) * (HEAD_DIM ** -0.5)
    s = jnp.where(mask, s, NEG_INF)
    m = jnp.maximum(jnp.max(s, axis=-1, keepdims=True), sink)
    p = jnp.exp(s - m)
    e_sink = jnp.exp(sink - m)
    inv = 1.0 / (jnp.sum(p, axis=-1, keepdims=True) + e_sink)
    return p * inv, e_sink * inv


def _attn_specs(qpk):
    blk = ATT_BLOCK
    q_spec = pl.BlockSpec((qpk, blk, HEAD_DIM), lambda h, n: (h, n, 0))
    cur = pl.BlockSpec((1, blk, HEAD_DIM), lambda h, n: (h, n, 0))
    prev = pl.BlockSpec((1, blk, HEAD_DIM), lambda h, n: (h, jnp.maximum(n - 1, 0), 0))
    sink_spec = pl.BlockSpec((1, qpk * blk, 1), lambda h, n: (h, 0, 0))
    return q_spec, cur, prev, sink_spec


def _attn_fwd(q, k, v, sinks):
    hq, l, _ = q.shape
    qpk = hq // N_KV_HEADS
    nb = l // ATT_BLOCK
    rows = qpk * ATT_BLOCK
    q_spec, cur, prev, sink_spec = _attn_specs(qpk)

    def body(q_ref, kp_ref, kc_ref, vp_ref, vc_ref, sink_ref, o_ref):
        mask = _attn_mask(pl.program_id(1), rows)
        kk = jnp.concatenate([kp_ref[0], kc_ref[0]], axis=0).astype(BF16)
        vv = jnp.concatenate([vp_ref[0], vc_ref[0]], axis=0).astype(BF16)
        p, _ = _attn_probs(q_ref[...].reshape(rows, HEAD_DIM).astype(BF16), kk, sink_ref[0], mask)
        o = jnp.dot(p.astype(BF16), vv, preferred_element_type=F32)
        o_ref[...] = o.reshape(qpk, ATT_BLOCK, HEAD_DIM).astype(o_ref.dtype)

    return pl.pallas_call(
        body, grid=(N_KV_HEADS, nb), in_specs=[q_spec, prev, cur, prev, cur, sink_spec],
        out_specs=q_spec, out_shape=jax.ShapeDtypeStruct((hq, l, HEAD_DIM), BF16),
        name="attn_fwd", compiler_params=_params(2),
    )(q, k, k, v, v, sinks)


def _attn_bwd(q, k, v, sinks, do):
    hq, l, _ = q.shape
    qpk = hq // N_KV_HEADS
    nb = l // ATT_BLOCK
    blk = ATT_BLOCK
    rows = qpk * blk
    q_spec, cur, prev, sink_spec = _attn_specs(qpk)
    part_spec = pl.BlockSpec((1, 1, 2 * blk, HEAD_DIM), lambda h, n: (h, n, 0, 0))
    dsink_spec = pl.BlockSpec((qpk, 1, LANES), lambda h, n: (h, 0, 0))
    tn = (((0,), (0,)), ((), ()))

    def body(q_ref, do_ref, kp_ref, kc_ref, vp_ref, vc_ref, sink_ref, dq_ref, dkp_ref, dvp_ref, dsink_ref):
        n = pl.program_id(1)
        mask = _attn_mask(n, rows)
        kk = jnp.concatenate([kp_ref[0], kc_ref[0]], axis=0).astype(BF16)
        vv = jnp.concatenate([vp_ref[0], vc_ref[0]], axis=0).astype(BF16)
        qb = q_ref[...].reshape(rows, HEAD_DIM).astype(BF16)
        do32 = do_ref[...].astype(F32).reshape(rows, HEAD_DIM)
        dob = do32.astype(BF16)
        p, p_sink = _attn_probs(qb, kk, sink_ref[0], mask)
        pb = p.astype(BF16)
        o = jnp.dot(pb, vv, preferred_element_type=F32)
        delta = jnp.sum(do32 * o, axis=-1, keepdims=True)
        dp = lax.dot_general(dob, vv, (((1,), (1,)), ((), ())), preferred_element_type=F32)
        ds = (p * (dp - delta) * (HEAD_DIM ** -0.5)).astype(BF16)
        dq = jnp.dot(ds, kk, preferred_element_type=F32)
        dq_ref[...] = dq.reshape(qpk, blk, HEAD_DIM).astype(dq_ref.dtype)
        dkp_ref[0, 0] = lax.dot_general(ds, qb, tn, preferred_element_type=F32)
        dvp_ref[0, 0] = lax.dot_general(pb, dob, tn, preferred_element_type=F32)
        dsr = p_sink * delta
        for g in range(qpk):
            dsg = jnp.broadcast_to(-_colsum(dsr[g * blk:(g + 1) * blk]), (1, LANES))

            @pl.when(n == 0)
            def _():
                dsink_ref[g] = dsg

            @pl.when(n > 0)
            def _():
                dsink_ref[g] += dsg


    part_shape = jax.ShapeDtypeStruct((N_KV_HEADS, nb, 2 * blk, HEAD_DIM), F32)
    dq, dkp, dvp, dsink = pl.pallas_call(
        body, grid=(N_KV_HEADS, nb), in_specs=[q_spec, q_spec, prev, cur, prev, cur, sink_spec],
        out_specs=[q_spec, part_spec, part_spec, dsink_spec],
        out_shape=[jax.ShapeDtypeStruct((hq, l, HEAD_DIM), BF16), part_shape, part_shape,
                   jax.ShapeDtypeStruct((hq, 1, LANES), F32)],
        name="attn_bwd", compiler_params=_params(2),
    )(q, do, k, k, v, v, sinks)

    def combine(a_cur, a_nxt, b_cur, b_nxt):
        last = pl.program_id(1) == nb - 1
        keep = jnp.where(last, 0.0, 1.0)
        return (a_cur[0, 0, blk:] + keep * a_nxt[0, 0, :blk])[None], (b_cur[0, 0, blk:] + keep * b_nxt[0, 0, :blk])[None]

    nxt_spec = pl.BlockSpec((1, 1, 2 * blk, HEAD_DIM), lambda h, n: (h, jnp.minimum(n + 1, nb - 1), 0, 0))
    kv_shape = jax.ShapeDtypeStruct((N_KV_HEADS, l, HEAD_DIM), BF16)
    dk, dv = _tile_call("attn_dkv", combine, (N_KV_HEADS, nb), [dkp, dkp, dvp, dvp],
                        [part_spec, nxt_spec, part_spec, nxt_spec], [kv_shape, kv_shape], [cur, cur])
    return dq, dk, dv, dsink


def _block_diag(m):
    j, gl, a, b = m.shape
    eye = jnp.eye(gl, dtype=m.dtype)
    return (m[:, :, :, None, :] * eye[None, :, None, :, None]).reshape(j, gl * a, gl * b)


def _diag_blocks(z, a):
    j = z.shape[0]
    gl = z.shape[1] // a
    b = z.shape[2] // gl
    d = jnp.diagonal(z.reshape(j, gl, a, gl, b), axis1=1, axis2=3)
    return d.transpose(0, 3, 1, 2)


def _s5_permute(src_ref, dst_ref, t_len):
    seg = t_len // 8
    for k in range(seg):
        dst_ref[8 * k:8 * k + 8, :] = src_ref[pl.ds(k, 8, stride=seg), :]


def _s5_unpermute(perm_ref, t_len, emit):
    per_seg = t_len // 64
    for m in range(t_len // 8):
        emit(8 * m, perm_ref[pl.ds(64 * (m % per_seg) + m // per_seg, 8, stride=8), :])


def _s5_powers(p_ref, lr, li, seg):
    hs = TILE_STATES

    def step(k, carry):
        pr, pi = carry
        p_ref[pl.ds(k, 1), 0:hs] = pr
        p_ref[pl.ds(k, 1), hs:2 * hs] = pi
        return lr * pr - li * pi, lr * pi + li * pr

    lax.fori_loop(0, seg, step, (lr, li))


def _s5_local_scan(x_ref, base, lr, li, seg, reverse):
    hs = TILE_STATES
    lr8, li8 = jnp.broadcast_to(lr, (8, hs)), jnp.broadcast_to(li, (8, hs))
    if reverse:
        li8 = -li8

    def step(i, carry):
        hr, hi = carry
        k = seg - 1 - i if reverse else i
        rows = pl.ds(pl.multiple_of(base + 8 * k, 8), 8)
        nr = lr8 * hr - li8 * hi + x_ref[rows, 0:hs]
        ni = lr8 * hi + li8 * hr + x_ref[rows, hs:2 * hs]
        x_ref[rows, 0:hs] = nr
        x_ref[rows, hs:2 * hs] = ni
        return nr, ni

    zero = jnp.zeros((8, hs), F32)
    return lax.fori_loop(0, seg, step, (zero, zero), unroll=2)


def _s5_carries(c_ref, e_ref, ends, start, pw_r, pw_i, reverse):
    hs = TILE_STATES
    e_ref[:, 0:hs] = ends[0]
    e_ref[:, hs:2 * hs] = ends[1]
    cr, ci = start
    if reverse:
        pw_i = -pw_i
    for s in (range(7, -1, -1) if reverse else range(8)):
        c_ref[s:s + 1, 0:hs] = cr
        c_ref[s:s + 1, hs:2 * hs] = ci
        er, ei = e_ref[s:s + 1, 0:hs], e_ref[s:s + 1, hs:2 * hs]
        cr, ci = er + pw_r * cr - pw_i * ci, ei + pw_r * ci + pw_i * cr
    return cr, ci


def _s5_states(u_perm_b16, bd_ref, x_ref, base, c_ref, e_ref, p_ref, lr, li, h_in, t_len):
    hs = TILE_STATES
    seg = t_len // 8
    x_ref[pl.ds(base, t_len), :] = jnp.dot(u_perm_b16, bd_ref[0], preferred_element_type=F32)
    ends = _s5_local_scan(x_ref, base, lr, li, seg, False)
    pw_r, pw_i = p_ref[seg - 1:seg, 0:hs], p_ref[seg - 1:seg, hs:2 * hs]
    h_out = _s5_carries(c_ref, e_ref, ends, h_in, pw_r, pw_i, False)
    cr, ci = c_ref[:, 0:hs], c_ref[:, hs:2 * hs]

    def fix(k, carry):
        rows = pl.ds(pl.multiple_of(base + 8 * k, 8), 8)
        pr, pi = p_ref[pl.ds(k, 1), 0:hs], p_ref[pl.ds(k, 1), hs:2 * hs]
        x_ref[rows, 0:hs] += pr * cr - pi * ci
        x_ref[rows, hs:2 * hs] += pr * ci + pi * cr
        return carry

    lax.fori_loop(0, seg, fix, 0, unroll=2)
    return h_out


def _s5_fwd(proj, u_off, bd, cbd, lam, dvec, t_len):
    l = proj.shape[0]
    nj = bd.shape[0]
    nch = l // t_len
    hs = TILE_STATES
    ub = u_off // LANES
    seg = t_len // 8
    assert t_len % 64 == 0

    def body(u_ref, bd_ref, cbd_ref, lam_ref, d_ref, y_ref, hst_ref, x_ref, h_ref, p_ref, c_ref, e_ref, up_ref, yp_ref):
        lr, li = lam_ref[0, 0:1, :], lam_ref[0, 1:2, :]

        @pl.when(pl.program_id(1) == 0)
        def _():
            h_ref[...] = jnp.zeros_like(h_ref)
            _s5_powers(p_ref, lr, li, seg)

        hst_ref[0, 0] = h_ref[...]
        _s5_permute(u_ref, up_ref, t_len)
        h_out = _s5_states(up_ref[...].astype(BF16), bd_ref, x_ref, 0, c_ref, e_ref, p_ref, lr, li,
                           (h_ref[:, 0:hs], h_ref[:, hs:2 * hs]), t_len)
        h_ref[:, 0:hs] = h_out[0]
        h_ref[:, hs:2 * hs] = h_out[1]
        yp_ref[...] = jnp.dot(x_ref[...].astype(BF16), cbd_ref[0], preferred_element_type=F32)
        dv = d_ref[0]

        def out(r0, rows):
            y_ref[r0:r0 + 8, :] = rows + dv * u_ref[r0:r0 + 8, :]

        _s5_unpermute(yp_ref, t_len, out)

    return pl.pallas_call(
        body, grid=(nj, nch),
        in_specs=[pl.BlockSpec((t_len, LANES), lambda j, c: (c, ub + j)),
                  pl.BlockSpec((1, LANES, 2 * hs), lambda j, c: (j, 0, 0)),
                  pl.BlockSpec((1, 2 * hs, LANES), lambda j, c: (j, 0, 0)),
                  pl.BlockSpec((1, 2, hs), lambda j, c: (j, 0, 0)),
                  pl.BlockSpec((1, 1, LANES), lambda j, c: (j, 0, 0))],
        out_specs=[pl.BlockSpec((t_len, LANES), lambda j, c: (c, j)),
                   pl.BlockSpec((1, 1, 1, 2 * hs), lambda j, c: (j, c, 0, 0))],
        out_shape=[jax.ShapeDtypeStruct((l, nj * LANES), F32),
                   jax.ShapeDtypeStruct((nj, nch, 1, 2 * hs), F32)],
        scratch_shapes=[pltpu.VMEM((t_len, 2 * hs), F32), pltpu.VMEM((1, 2 * hs), F32),
                        pltpu.VMEM((seg, 2 * hs), F32), pltpu.VMEM((8, 2 * hs), F32), pltpu.VMEM((8, 2 * hs), F32),
                        pltpu.VMEM((t_len, LANES), F32), pltpu.VMEM((t_len, LANES), F32)],
        name="s5_fwd", compiler_params=_params(2),
    )(proj, bd, cbd, lam, dvec)


def _s5_bwd(proj, u_off, dy, hst, bd, bdt, cbdt, lam, dvec, t_len):
    l = proj.shape[0]
    nj = bd.shape[0]
    nch = l // t_len
    hs = TILE_STATES
    ub = u_off // LANES
    seg = t_len // 8
    tn = (((0,), (0,)), ((), ()))
    assert t_len % 64 == 0

    def body(u_ref, dy_ref, hst_ref, bd_ref, bdt_ref, cbdt_ref, lam_ref, d_ref,
             du_ref, dbd_ref, dcbdt_ref, dlam_ref, dd_ref,
             x_ref, g_ref, gc_ref, p_ref, c_ref, e_ref, up_ref, dyp_ref, dup_ref):
        first = pl.program_id(1) == 0
        lr, li = lam_ref[0, 0:1, :], lam_ref[0, 1:2, :]

        @pl.when(first)
        def _():
            gc_ref[...] = jnp.zeros_like(gc_ref)
            _s5_powers(p_ref, lr, li, seg)

        _s5_permute(u_ref, up_ref, t_len)
        _s5_permute(dy_ref, dyp_ref, t_len)
        ub16, dyb16 = up_ref[...].astype(BF16), dyp_ref[...].astype(BF16)
        h0 = hst_ref[0, 0]
        _s5_states(ub16, bd_ref, x_ref, 8, c_ref, e_ref, p_ref, lr, li, (h0[:, 0:hs], h0[:, hs:2 * hs]), t_len)
        x_ref[0:8, :] = c_ref[...]
        g_ref[...] = jnp.dot(dyb16, cbdt_ref[0], preferred_element_type=F32)
        starts = _s5_local_scan(g_ref, 0, lr, li, seg, True)
        pw_r, pw_i = p_ref[seg - 1:seg, 0:hs], p_ref[seg - 1:seg, hs:2 * hs]
        g_out = _s5_carries(c_ref, e_ref, starts, (gc_ref[:, 0:hs], gc_ref[:, hs:2 * hs]), pw_r, pw_i, True)
        gc_ref[:, 0:hs] = g_out[0]
        gc_ref[:, hs:2 * hs] = g_out[1]
        cr, ci = c_ref[:, 0:hs], c_ref[:, hs:2 * hs]

        def fix(k, carry):
            alr, ali = carry
            rows = pl.ds(pl.multiple_of(8 * k, 8), 8)
            pr, pi = p_ref[pl.ds(seg - 1 - k, 1), 0:hs], p_ref[pl.ds(seg - 1 - k, 1), hs:2 * hs]
            gr = g_ref[rows, 0:hs] + pr * cr + pi * ci
            gi = g_ref[rows, hs:2 * hs] + pr * ci - pi * cr
            g_ref[rows, 0:hs] = gr
            g_ref[rows, hs:2 * hs] = gi
            hpr, hpi = x_ref[rows, 0:hs], x_ref[rows, hs:2 * hs]
            return alr + gr * hpr + gi * hpi, ali + gi * hpr - gr * hpi

        zero = jnp.zeros((8, hs), F32)
        alr, ali = lax.fori_loop(0, seg, fix, (zero, zero), unroll=2)
        alr, ali = _colsum(alr), _colsum(ali)
        g = g_ref[...].astype(BF16)
        h = x_ref[pl.ds(8, t_len), :].astype(BF16)
        dup_ref[...] = jnp.dot(g, bdt_ref[0], preferred_element_type=F32)
        dv = d_ref[0]

        def out(r0, rows):
            du_ref[r0:r0 + 8, :] = (rows + dv * dy_ref[r0:r0 + 8, :]).astype(du_ref.dtype)

        _s5_unpermute(dup_ref, t_len, out)
        sign = jnp.where(lax.broadcasted_iota(jnp.int32, (1, 2 * hs), 1) < hs, 1.0, -1.0)
        dbd = lax.dot_general(ub16, g, tn, preferred_element_type=F32)
        dcbdt = lax.dot_general(dyb16, h, tn, preferred_element_type=F32) * sign
        ddv = _colsum(dy_ref[...] * u_ref[...])

        @pl.when(first)
        def _():
            dbd_ref[0] = dbd
            dcbdt_ref[0] = dcbdt
            dlam_ref[0, 0:1, :] = alr
            dlam_ref[0, 1:2, :] = ali
            dd_ref[0] = ddv

        @pl.when(jnp.logical_not(first))
        def _():
            dbd_ref[0] += dbd
            dcbdt_ref[0] += dcbdt
            dlam_ref[0, 0:1, :] += alr
            dlam_ref[0, 1:2, :] += ali
            dd_ref[0] += ddv

    rev = lambda c: nch - 1 - c
    wide = pl.BlockSpec((1, LANES, 2 * hs), lambda j, c: (j, 0, 0))
    tall = pl.BlockSpec((1, 2 * hs, LANES), lambda j, c: (j, 0, 0))
    return pl.pallas_call(
        body, grid=(nj, nch),
        in_specs=[pl.BlockSpec((t_len, LANES), lambda j, c: (rev(c), ub + j)),
                  pl.BlockSpec((t_len, LANES), lambda j, c: (rev(c), j)),
                  pl.BlockSpec((1, 1, 1, 2 * hs), lambda j, c: (j, rev(c), 0, 0)),
                  wide, tall, wide,
                  pl.BlockSpec((1, 2, hs), lambda j, c: (j, 0, 0)),
                  pl.BlockSpec((1, 1, LANES), lambda j, c: (j, 0, 0))],
        out_specs=[pl.BlockSpec((t_len, LANES), lambda j, c: (rev(c), j)),
                   wide, wide,
                   pl.BlockSpec((1, 2, hs), lambda j, c: (j, 0, 0)),
                   pl.BlockSpec((1, 1, LANES), lambda j, c: (j, 0, 0))],
        out_shape=[jax.ShapeDtypeStruct((l, nj * LANES), BF16),
                   jax.ShapeDtypeStruct((nj, LANES, 2 * hs), F32),
                   jax.ShapeDtypeStruct((nj, LANES, 2 * hs), F32),
                   jax.ShapeDtypeStruct((nj, 2, hs), F32),
                   jax.ShapeDtypeStruct((nj, 1, LANES), F32)],
        scratch_shapes=[pltpu.VMEM((t_len + 8, 2 * hs), F32), pltpu.VMEM((t_len, 2 * hs), F32),
                        pltpu.VMEM((1, 2 * hs), F32), pltpu.VMEM((seg, 2 * hs), F32),
                        pltpu.VMEM((8, 2 * hs), F32), pltpu.VMEM((8, 2 * hs), F32),
                        pltpu.VMEM((t_len, LANES), F32), pltpu.VMEM((t_len, LANES), F32),
                        pltpu.VMEM((t_len, LANES), F32)],
        name="s5_bwd", compiler_params=_params(2),
    )(proj, dy, hst, bd, bdt, cbdt, lam, dvec)


def _full_spec(shape):
    nd = len(shape)
    return pl.BlockSpec(tuple(shape), lambda i: (0,) * nd)


def _sds(shape, dtype=F32):
    return jax.ShapeDtypeStruct(tuple(shape), dtype)


def kernel(x, c, ada_w, ada_b, norm_mix_g, w_in, attn_sinks, w_attn_proj, ssm_a_re, ssm_a_im, ssm_log_dt, ssm_b_re, ssm_b_im, ssm_c_re, ssm_c_im, ssm_d, w_ssm_glu, w_out, norm_ffn_g, w_ffn_up, ffn_conv_w, ffn_conv_b, w_ffn_down, final_g, loss_target, m_ada_w, m_ada_b, m_norm_mix_g, m_w_in, m_attn_sinks, m_w_attn_proj, m_ssm_a_re, m_ssm_a_im, m_ssm_log_dt, m_ssm_b_re, m_ssm_b_im, m_ssm_c_re, m_ssm_c_im, m_ssm_d, m_w_ssm_glu, m_w_out, m_norm_ffn_g, m_w_ffn_up, m_ffn_conv_w, m_ffn_conv_b, m_w_ffn_down, m_final_g, v_ada_w, v_ada_b, v_norm_mix_g, v_w_in, v_attn_sinks, v_w_attn_proj, v_ssm_a_re, v_ssm_a_im, v_ssm_log_dt, v_ssm_b_re, v_ssm_b_im, v_ssm_c_re, v_ssm_c_im, v_ssm_d, v_w_ssm_glu, v_w_out, v_norm_ffn_g, v_w_ffn_up, v_ffn_conv_w, v_ffn_conv_b, v_w_ffn_down, v_final_g):
    given = dict(locals())
    names = ['ada_w', 'ada_b', 'norm_mix_g', 'w_in', 'attn_sinks', 'w_attn_proj', 'ssm_a_re', 'ssm_a_im',
             'ssm_log_dt', 'ssm_b_re', 'ssm_b_im', 'ssm_c_re', 'ssm_c_im', 'ssm_d', 'w_ssm_glu', 'w_out',
             'norm_ffn_g', 'w_ffn_up', 'ffn_conv_w', 'ffn_conv_b', 'w_ffn_down', 'final_g']

    xs = x[0]
    tgt = loss_target[0]
    l, d = xs.shape
    attn_w = w_attn_proj.shape[1]
    ssm_w = w_ssm_glu.shape[1]
    hq = attn_sinks.shape[1]
    qpk = hq // N_KV_HEADS
    kv_w = N_KV_HEADS * HEAD_DIM
    n_groups = ssm_a_re.shape[1]
    dff = ffn_conv_b.shape[1]
    in_w = attn_w + 2 * kv_w + ssm_w + 2 * d
    nj = ssm_w // LANES
    off_k, off_v, off_u = attn_w, attn_w + kv_w, attn_w + 2 * kv_w
    off_ga, off_gs = off_u + ssm_w, off_u + ssm_w + d
    assert hq * HEAD_DIM == attn_w and n_groups * SSM_P == ssm_w and l % ATT_BLOCK == 0

    xi, yi, ci = _dev()
    idx = 4 * xi + 2 * yi + ci

    row_sharded = {'w_out': (d, d), 'w_ffn_down': (dff, d)}
    big = ['w_in', 'w_attn_proj', 'w_ssm_glu', 'w_out', 'w_ffn_up', 'w_ffn_down']
    spack, s_offs = _pack([c, ffn_conv_w[0]], LANES, 8)
    w16 = {k: given[k][0].astype(BF16) for k in big}
    wg_in, sg = _all_gather("gather_first", [w16['w_in'], spack])
    mixer_w = ['w_attn_proj', 'w_ssm_glu', 'w_out']
    h_mixer, tok = _exchange_start("gather_mixer_start", [w16[k] for k in mixer_w], True, wg_in)
    h_up, tok = _exchange_start("gather_ffn_up_start", [w16['w_ffn_up']], True, tok)
    h_down, tok = _exchange_start("gather_ffn_down_start", [w16['w_ffn_down']], True, tok)
    full = {'w_in': wg_in.transpose(1, 0, 2).reshape(d, in_w)}
    c_all = _unpack(sg, s_offs[0], (d,), lead=(N_DEV,))
    conv_w = _unpack(sg, s_offs[1], ffn_conv_w.shape[1:], lead=(N_DEV,)).transpose(1, 0, 2).reshape(3, dff)
    conv_b = ffn_conv_b

    mod_n = ada_w.shape[2]
    tcm = _pick(mod_n, 512)
    ada_b_mine = lax.dynamic_slice_in_dim(ada_b, idx * mod_n, mod_n, axis=1)

    def modpart_fn(cv, wv, bv):
        cond = cv * jax.nn.sigmoid(cv)
        return jnp.dot(cond.astype(BF16), wv.astype(BF16), preferred_element_type=F32) + bv, cond

    modp, cond_all = _tile_call(
        "ada_rows", modpart_fn, (mod_n // tcm,), [c_all, ada_w[0], ada_b_mine],
        [pl.BlockSpec((N_DEV, d), lambda j: (0, 0)), pl.BlockSpec((d, tcm), lambda j: (0, j)),
         pl.BlockSpec((1, tcm), lambda j: (0, j))],
        [_sds((N_DEV, mod_n)), _sds((N_DEV, d))],
        [pl.BlockSpec((N_DEV, tcm), lambda j: (0, j)), pl.BlockSpec((N_DEV, d), lambda j: (0, 0))])
    (modg,) = _all_gather("gather_ada_rows", [modp])
    mod = lax.dynamic_index_in_dim(modg, idx, axis=1, keepdims=False).reshape(1, N_DEV * mod_n)
    sh1, sc1, g1, sh2, sc2, g2 = [mod[:, i * d:(i + 1) * d] for i in range(6)]

    tr = _pick(l, 256, 8)
    trh = _pick(l, 128, 8)
    nr, nrh = l // tr, l // trh
    g_mix, g_ffn, g_fin = norm_mix_g + tok[0:1, 0:1], norm_ffn_g, final_g.reshape(1, d)

    def with_t(fn):
        def wrapped(*vals):
            out = fn(*vals)
            out = out if isinstance(out, tuple) else (out,)
            return out + (out[-1].T,)
        return wrapped

    h1, h1_t = _tile_call("norm_mod_mix", with_t(_norm_mod), (1, nr), [xs, g_mix, sc1, sh1],
                          [_t(tr, d), _v(d), _v(d), _v(d)], [_sds((l, d), BF16), _sds((d, l), BF16)],
                          [_t(tr, d), _tt(tr, d)])
    proj = _matmul("proj_in", h1, full['w_in'], "nn", tn=1280)

    def heads(z, n):
        return z.reshape(l, n, HEAD_DIM).transpose(1, 0, 2)

    qh = heads(proj[:, :attn_w], hq)
    kh = heads(proj[:, off_k:off_k + kv_w], N_KV_HEADS)
    vh = heads(proj[:, off_v:off_v + kv_w], N_KV_HEADS)
    sinks3 = jnp.repeat(attn_sinks.reshape(N_KV_HEADS, qpk), ATT_BLOCK, axis=1)[..., None]
    o_h = _attn_fwd(qh, kh, vh, sinks3)
    o2 = o_h.transpose(1, 0, 2).reshape(l, attn_w)

    gn = (n_groups, SSM_N)
    pgn = (SSM_P, n_groups, SSM_N)
    a_re, a_im, log_dt = ssm_a_re[0], ssm_a_im[0], ssm_log_dt[0].reshape(n_groups, 1)
    b_re, b_im = ssm_b_re[0].transpose(2, 0, 1), ssm_b_im[0].transpose(2, 0, 1)
    disc_ins = [a_re, a_im, log_dt, b_re, b_im]
    disc_specs = [_full_spec(gn), _full_spec(gn), _full_spec((n_groups, 1)), _full_spec(pgn), _full_spec(pgn)]
    lam_r, lam_i, bb_r, bb_i = _tile_call(
        "s5_discretise", _s5_disc_fn, (1,), disc_ins, disc_specs,
        [_sds(gn), _sds(gn), _sds(pgn), _sds(pgn)],
        [_full_spec(gn), _full_spec(gn), _full_spec(pgn), _full_spec(pgn)])

    def tiles_gpn(z):
        return z.reshape(SSM_P, nj, TILE_GROUPS, SSM_N).transpose(1, 2, 0, 3)

    bd = jnp.concatenate([_block_diag(tiles_gpn(bb_r)), _block_diag(tiles_gpn(bb_i))], axis=2).astype(BF16)
    c_r = ssm_c_re[0].reshape(nj, TILE_GROUPS, SSM_P, SSM_N).transpose(0, 1, 3, 2)
    c_i = (-ssm_c_im[0]).reshape(nj, TILE_GROUPS, SSM_P, SSM_N).transpose(0, 1, 3, 2)
    cbd = jnp.concatenate([_block_diag(c_r), _block_diag(c_i)], axis=1).astype(BF16)
    bdt, cbdt = bd.transpose(0, 2, 1), cbd.transpose(0, 2, 1)
    lam = jnp.stack([lam_r.reshape(nj, TILE_STATES), lam_i.reshape(nj, TILE_STATES)], axis=1)
    dvec = ssm_d[0].reshape(nj, 1, LANES)
    t_len = _pick(l, 512, 8)
    y, hst = _s5_fwd(proj, off_u, bd, cbd, lam, dvec, t_len)

    tcs, trg = _pick(ssm_w, 1024), _pick(l, 512, 8)
    gy = _tile_call("gelu", lambda v: jax.nn.gelu(v), (ssm_w // tcs, l // trg), [y], [_t(trg, tcs)],
                    [_sds((l, ssm_w), BF16)], [_t(trg, tcs)])[0]
    full.update(zip(mixer_w, _exchange_wait("gather_mixer_wait", h_mixer, gy)))
    full['w_out'] = full['w_out'].reshape(row_sharded['w_out'])
    full['w_attn_proj'] = full['w_attn_proj'].transpose(1, 0, 2).reshape(attn_w, d)
    full['w_ssm_glu'] = full['w_ssm_glu'].transpose(1, 0, 2).reshape(ssm_w, 2 * d)
    glu = _matmul("ssm_glu", gy, full['w_ssm_glu'], "nn")

    tcd = 256 if d % 256 == 0 and off_ga % 256 == 0 else LANES
    assert d % tcd == 0 and off_ga % tcd == 0 and off_gs % tcd == 0
    gate_ins = [(glu, 0), (glu, d), (proj, off_ga), (proj, off_gs)]

    def mix_epilogue(at, ga_, gb_, pa, ps):
        return at, _mix_fn(ga_, gb_, at, pa, ps)

    attn, mixed = _matmul("attn_proj_gate_mix", o2, full['w_attn_proj'], "nn", tn=tcd,
                          epilogue=(mix_epilogue, gate_ins, [(F32, False), (BF16, False)]))
    def res_norm_fn(xv, mo, g1v, gv, scv, shv):
        x2v = xv + g1v * mo
        return x2v, _norm_mod(x2v, gv, scv, shv)

    def res_norm_epilogue(mo, xv, g1v, gv, scv, shv):
        x2v, h2v = res_norm_fn(xv, mo, g1v, gv, scv, shv)
        return mo, x2v, h2v, h2v.T

    mixout, x2, h2, h2_t = _matmul(
        "mix_out_residual_norm_mod_ffn", mixed, full['w_out'], "nn", tm=256, tn=d,
        epilogue=(res_norm_epilogue, [xs, g1, g_ffn, sc2, sh2], [(F32, False), (F32, False), (BF16, False), (BF16, True)]))
    full['w_ffn_up'], = _exchange_wait("gather_ffn_up_wait", h_up, h2)
    up = _matmul("ffn_up", h2, full['w_ffn_up'], "nn", out_dtype=BF16, tn=1408)

    tcf, trc = _pick(dff, 1408), _pick(l, 512, 8)
    assert dff % tcf == 0
    ncf = dff // tcf

    taps = [conv_w[i:i + 1] for i in range(3)]

    def conv_gate(gp, gp_prev, w0, w1, w2, bv):
        gp = gp.astype(F32)
        prev = jnp.where(pl.program_id(1) == 0, 0.0, 1.0) * gp_prev.astype(F32)
        ext = jnp.concatenate([prev, gp], axis=0)
        m1 = pltpu.roll(ext, 1, 0)[HALO:]
        m2 = pltpu.roll(ext, 2, 0)[HALO:]
        return w0 * m2 + w1 * m1 + w2 * gp + bv, m1, m2

    def convglu_fn(gp, gp_prev, val, w0, w1, w2, bv):
        gate, _, _ = conv_gate(gp, gp_prev, w0, w1, w2, bv)
        return gate * jax.nn.sigmoid(gate) * val.astype(F32)

    act, act_t = _tile_call("conv_swiglu", with_t(convglu_fn), (ncf, l // trc), [up, up, up] + taps + [conv_b],
                            [_t(trc, tcf), _prev_rows(trc, tcf), _t(trc, tcf, dff)] + [_v(tcf)] * 4,
                            [_sds((l, dff), BF16), _sds((dff, l), BF16)], [_t(trc, tcf), _tt(trc, tcf)])
    full['w_ffn_down'] = _exchange_wait("gather_ffn_down_wait", h_down, act)[0].reshape(row_sharded['w_ffn_down'])

    def final_fn(x2v, fv, g2v, gv, tv):
        rows = x2v.shape[0]

        def loss_of(x2a, fa, g2a, ga):
            out = _rms(x2a + g2a * fa, ga)
            err = out - tv
            return 0.5 * _colsum(jnp.mean(err * err, axis=-1, keepdims=True))

        loss, vjp = jax.vjp(loss_of, x2v, fv, _bc(g2v, rows), _bc(gv, rows))
        dx3, dffn, dg2, dgf = vjp(jnp.ones((1, 1), F32))
        return jnp.broadcast_to(loss, (1, x2v.shape[1])), dx3, dffn, _colsum(dg2), _colsum(dgf)

    def final_epilogue(fv, x2v, tv, g2v, gv):
        return final_fn(x2v, fv, g2v, gv, tv)

    loss_rows, dx3, dffn, dg2_rows, dgf_rows = _matmul(
        "ffn_down_loss_final_norm", act, full['w_ffn_down'], "nn", tm=256, tn=d, tk=1408,
        epilogue=(final_epilogue, [x2, tgt, g2, g_fin],
                  [(F32, 'colsum'), (F32, False), (BF16, False), (F32, 'colsum'), (F32, 'colsum')]))

    def add_row_blocks(*parts):
        return tuple(jnp.sum(p, axis=0) for p in parts)

    loss_p, dg2, dg_fin = _tile_call(
        "sum_row_blocks_final", add_row_blocks, (1,), [loss_rows, dg2_rows, dgf_rows],
        [_full_spec(loss_rows.shape)] * 3, [_sds((1, d))] * 3, [_full_spec((1, d))] * 3)
    loss = lax.psum(loss_p[0, 0], ("x", "y", "c"))

    dact = _matmul("d_act", dffn, full['w_ffn_down'], "nt", out_dtype=BF16, tn=1408, dep=loss.reshape(1, 1))
    gd, gd16, pending = {}, {}, []
    dw_down, dw_down16 = _matmul("dw_ffn_down", act_t, dffn, "nn", tm=512, also_bf16=True)
    gd['w_ffn_down'], gd16['w_ffn_down'] = [z.reshape((N_DEV,) + w_ffn_down.shape[1:]) for z in (dw_down, dw_down16)]
    handle, tok = _exchange_start("grad_ffn_down_start", [gd16['w_ffn_down']], False, loss.reshape(1, 1))
    pending.append((['w_ffn_down'], handle))
    conv_b_bwd = conv_b + tok[0:1, 0:1]

    def convglu_bwd_fn(gp, gp_prev, gp_next, val, val_next, da, da_next, w0, w1, w2, bv):
        rows = gp.shape[0]
        i = pl.program_id(1)
        gp, val, da = gp.astype(F32), val.astype(F32), da.astype(F32)
        prev = jnp.where(i == 0, 0.0, 1.0) * gp_prev.astype(F32)
        more = jnp.where(i == pl.num_programs(1) - 1, 0.0, 1.0)
        ext = jnp.concatenate([prev, gp, gp_next.astype(F32)], axis=0)
        cur = ext[HALO:]
        m1 = pltpu.roll(ext, 1, 0)[HALO:]
        m2 = pltpu.roll(ext, 2, 0)[HALO:]
        gate = w0 * m2 + w1 * m1 + w2 * cur + bv
        sg = jax.nn.sigmoid(gate)
        val_e = jnp.concatenate([val, val_next.astype(F32)], axis=0)
        da_e = jnp.concatenate([da, more * da_next.astype(F32)], axis=0)
        dgate = da_e * val_e * (sg * (1.0 + gate * (1.0 - sg)))
        p1 = pltpu.roll(dgate, rows + HALO - 1, 0)[:rows]
        p2 = pltpu.roll(dgate, rows + HALO - 2, 0)[:rows]
        dg = dgate[:rows]
        dgp = w2 * dg + w1 * p1 + w0 * p2
        dval = da * (gate[:rows] * sg[:rows])
        return (jnp.stack([dgp, dval], axis=0), _colsum(dg), _colsum(dg * m2[:rows]), _colsum(dg * m1[:rows]),
                _colsum(dg * gp))

    dup, dconv_b, dcw0, dcw1, dcw2 = _tile_call(
        "conv_swiglu_bwd", convglu_bwd_fn, (ncf, nr), [up, up, up, up, up, dact, dact] + taps + [conv_b_bwd],
        [_t(tr, tcf), _prev_rows(tr, tcf), _next_rows(tr, tcf, l), _t(tr, tcf, dff), _next_rows(tr, tcf, l, dff),
         _t(tr, tcf), _next_rows(tr, tcf, l)] + [_v(tcf)] * 4,
        [_sds((2, l, dff), BF16)] + [_sds((1, dff))] * 4, [_st(tr, tcf)] + [_v(tcf)] * 4, acc=(1, 2, 3, 4))
    dh2 = _matmul("d_h2", dup, full['w_ffn_up'], "nt", tm=512, fold=4)
    gd['w_ffn_up'], gd16['w_ffn_up'] = _matmul("dw_ffn_up", h2_t, dup, "nn", tm=512, tn=1408, out_stack=N_DEV, also_bf16=True)
    handle, tok = _exchange_start("grad_ffn_up_start", [gd16['w_ffn_up']], False, gd['w_ffn_up'])
    pending.append((['w_ffn_up'], handle))
    g_ffn_bwd = g_ffn + tok[0:1, 0:1]

    def res_norm_bwd_fn(xv, mo, g1v, gv, scv, shv, dhv, dxv):
        rows = xv.shape[0]
        _, vjp = jax.vjp(res_norm_fn, xv, mo, _bc(g1v, rows), _bc(gv, rows), _bc(scv, rows), _bc(shv, rows))
        dx, dmo, dg1v, dgv, dscv, dshv = vjp((dxv, dhv))
        return dx, dmo, _colsum(dg1v), _colsum(dgv), _colsum(dscv), _colsum(dshv)

    dx2, dmixout, dg1, dg_ffn, dsc2, dsh2 = _tile_call(
        "residual_norm_mod_ffn_bwd", res_norm_bwd_fn, (1, nrh), [xs, mixout, g1, g_ffn_bwd, sc2, sh2, dh2, dx3],
        [_t(trh, d), _t(trh, d), _v(d), _v(d), _v(d), _v(d), _t(trh, d), _t(trh, d)],
        [_sds((l, d)), _sds((l, d), BF16)] + [_sds((1, d))] * 4,
        [_t(trh, d), _t(trh, d)] + [_v(d)] * 4, acc=(2, 3, 4, 5))

    def mix_bwd_epilogue(dm, ga_, gb_, pa, ps, at):
        _, vjp = jax.vjp(_mix_fn, ga_, gb_, at, pa, ps)
        da, db, dat, dpa, dps = vjp(dm)
        return jnp.stack([da, db], axis=0), dat, dpa, dps

    dglu, dattn, dga, dgs = _matmul(
        "d_mixed_gate_mix_bwd", dmixout, full['w_out'], "nt", tn=tcd,
        epilogue=(mix_bwd_epilogue, gate_ins + [attn], [(BF16, 'pair')] + [(BF16, False)] * 3))
    dw_out, dw_out16 = _matmul("dw_out", mixed, dmixout, "tn", also_bf16=True)
    gd['w_out'], gd16['w_out'] = [z.reshape((N_DEV,) + w_out.shape[1:]) for z in (dw_out, dw_out16)]

    def gelu_bwd_epilogue(dgy, yv):
        _, vjp = jax.vjp(lambda z: jax.nn.gelu(z), yv)
        return (vjp(dgy)[0],)

    dy, = _matmul("d_gelu_y_gelu_bwd", dglu, full['w_ssm_glu'], "nt", epilogue=(gelu_bwd_epilogue, [y], [(F32, False)]))
    gd['w_ssm_glu'], gd16['w_ssm_glu'] = _matmul("dw_ssm_glu", gy, dglu, "tn", out_stack=N_DEV, also_bf16=True)
    du, dbd, dcbdt, dlam, dd_tiles = _s5_bwd(proj, off_u, dy, hst, bd, bdt, cbdt, lam, dvec, t_len)

    def gpn_of(z):
        return z.transpose(2, 0, 1, 3).reshape(pgn)

    dbb_r = gpn_of(_diag_blocks(dbd[:, :, :TILE_STATES], SSM_P))
    dbb_i = gpn_of(_diag_blocks(dbd[:, :, TILE_STATES:], SSM_P))
    dc_re = _diag_blocks(dcbdt[:, :, :TILE_STATES], SSM_P).reshape(n_groups, SSM_P, SSM_N)
    dc_im = _diag_blocks(dcbdt[:, :, TILE_STATES:], SSM_P).reshape(n_groups, SSM_P, SSM_N)
    dlam_r, dlam_i = dlam[:, 0].reshape(gn), dlam[:, 1].reshape(gn)

    def disc_bwd_fn(ar, ai, ld, br, bi, dlr, dli, dbr, dbi):
        _, vjp = jax.vjp(_s5_disc_fn, ar, ai, ld, br, bi)
        return vjp((dlr, dli, dbr, dbi))

    da_re, da_im, dlog_dt, db_re, db_im = _tile_call(
        "s5_discretise_bwd", disc_bwd_fn, (1,), disc_ins + [dlam_r, dlam_i, dbb_r, dbb_i],
        disc_specs + [_full_spec(gn), _full_spec(gn), _full_spec(pgn), _full_spec(pgn)],
        [_sds(gn), _sds(gn), _sds((n_groups, 1)), _sds(pgn), _sds(pgn)], disc_specs)

    do2 = _matmul("d_attn_heads", dattn, full['w_attn_proj'], "nt")
    gd['w_attn_proj'], gd16['w_attn_proj'] = _matmul("dw_attn_proj", o2, dattn, "tn", out_stack=N_DEV, also_bf16=True)
    handle, tok = _exchange_start("grad_mixer_start", [gd16[k] for k in mixer_w], False, gd['w_attn_proj'])
    pending.append((mixer_w, handle))
    do_h = heads(do2.astype(BF16), hq)
    dq_h, dk_h, dv_h, dsink = _attn_bwd(qh, kh, vh, sinks3 + tok[0:1, 0:1], do_h)

    def unheads(z):
        return z.transpose(1, 0, 2).reshape(l, z.shape[0] * HEAD_DIM)

    early = ['attn_sinks', 'ssm_a_re', 'ssm_a_im', 'ssm_log_dt', 'ssm_b_re', 'ssm_b_im', 'ssm_c_re', 'ssm_c_im',
             'ssm_d', 'norm_ffn_g', 'ffn_conv_b', 'final_g']
    early_grads = {
        'attn_sinks': dsink[:, 0, 0], 'ssm_a_re': da_re, 'ssm_a_im': da_im, 'ssm_log_dt': dlog_dt,
        'ssm_b_re': db_re.transpose(1, 2, 0), 'ssm_b_im': db_im.transpose(1, 2, 0), 'ssm_c_re': dc_re,
        'ssm_c_im': dc_im, 'ssm_d': dd_tiles, 'norm_ffn_g': dg_ffn, 'ffn_conv_b': dconv_b, 'final_g': dg_fin}
    ge_pack, e_offs = _pack([jnp.concatenate([dg1, dsh2, dsc2, dg2], axis=1)] + [early_grads[k] for k in early],
                            LANES, 8)
    h_early, tok = _exchange_start("gather_small_early_start", [ge_pack], True, dsink)

    dproj = jnp.concatenate([unheads(dq_h), unheads(dk_h), unheads(dv_h), du, dga, dgs], axis=1)
    dw_in, dw_in16 = _matmul("dw_in", h1_t, dproj, "nn", tm=512, tn=1280, also_bf16=True, dep=tok)
    dcw = jnp.concatenate([dcw0, dcw1, dcw2], axis=0)
    shard_in, shard_cw = w_in.shape[1:], ffn_conv_w.shape[1:]
    gd16['w_in'] = dw_in16.reshape(shard_in[0], N_DEV, shard_in[1]).transpose(1, 0, 2)
    own_in = lax.dynamic_slice_in_dim(dw_in, idx * shard_in[1], shard_in[1], axis=1)[None]
    gd['ffn_conv_w'] = dcw.reshape(shard_cw[0], N_DEV, shard_cw[1]).transpose(1, 0, 2)
    gd16['ffn_conv_w'] = gd['ffn_conv_w'].astype(BF16)
    handle, tok = _exchange_start("grad_in_start", [gd16['w_in'], gd16['ffn_conv_w']], False, dw_in)
    pending.append((['w_in', 'ffn_conv_w'], handle))
    def norm_bwd_epilogue(dhv, xv, dxv, gv, scv, shv):
        rows = xv.shape[0]
        _, vjp = jax.vjp(_norm_mod, xv, _bc(gv, rows), _bc(scv, rows), _bc(shv, rows))
        dx, dgv, dscv, dshv = vjp(dhv)
        return dx + dxv, _colsum(dgv), _colsum(dscv), _colsum(dshv)

    grad_x, dg_rows, dsc_rows, dsh_rows = _matmul(
        "d_h1_norm_mod_mix_bwd", dproj, full['w_in'], "nt", tm=256, tn=d, tk=1280, dep=tok,
        epilogue=(norm_bwd_epilogue, [xs, dx2, g_mix, sc1, sh1], [(F32, False)] + [(F32, 'colsum')] * 3))
    dg_mix, dsc1, dsh1 = _tile_call(
        "sum_row_blocks_mix", add_row_blocks, (1,), [dg_rows, dsc_rows, dsh_rows],
        [_full_spec(dg_rows.shape)] * 3, [_sds((1, d))] * 3, [_full_spec((1, d))] * 3)

    gl_pack, l_offs = _pack([jnp.concatenate([dsh1, dsc1], axis=1), dg_mix], LANES, 8)
    h_late, tok = _exchange_start("gather_small_late_start", [gl_pack], True, grad_x)

    sharded = big + ['ffn_conv_w']
    sharded_out = {}

    def finish(group, handle, after):
        for k, parts in zip(group, _exchange_wait("grad_" + group[0] + "_wait", handle, after)):
            own_src, own_at = (own_in, 0 * idx) if k == 'w_in' else (gd[k], idx)
            sharded_out[k] = _adamw_sharded("adamw_" + k, parts, own_src, given[k][0], given['m_' + k][0],
                                            given['v_' + k][0], jnp.stack([idx, own_at]).astype(jnp.int32))

    for group, handle in pending[:-1]:
        finish(group, handle, tok)
    done = functools.reduce(lambda p, q: p + q, [sharded_out[k][1][0:1, 0:1] for g_, _ in pending[:-1] for k in g_])
    finish(*pending[-1], done)

    ge_all, = _exchange_wait("gather_small_early_wait", h_early, done)
    gl_all, = _exchange_wait("gather_small_late_wait", h_late, sharded_out['w_in'][1])
    gs_all = jnp.concatenate([ge_all, gl_all], axis=1)
    rows_e = ge_pack.shape[0]

    def small_pack(prefix):
        ab = given[prefix + 'ada_b']
        p_early, _ = _pack([ab[:, 2 * d:]] + [given[prefix + k] for k in early], LANES, 8)
        p_late, _ = _pack([ab[:, :2 * d], given[prefix + 'norm_mix_g']], LANES, 8)
        return jnp.concatenate([p_early, p_late], axis=0)

    small_out = _adamw("adamw_replicated", gs_all, small_pack(''), small_pack('m_'), small_pack('v_'))

    dmod_all = jnp.concatenate([_unpack(gl_all, l_offs[0], (2 * d,), lead=(N_DEV,)),
                                _unpack(ge_all, e_offs[0], (4 * d,), lead=(N_DEV,))], axis=1)
    dmod_mine = lax.dynamic_slice_in_dim(dmod_all, idx * mod_n, mod_n, axis=1)
    kpad = LANES - N_DEV
    cond_t = jnp.pad(cond_all.T, ((0, 0), (0, kpad)))
    dmod_pad = jnp.pad(dmod_mine, ((0, kpad), (0, 0)))
    g_ada_w = _matmul("dw_ada", cond_t, dmod_pad, "nn")
    ada_out = _adamw("adamw_ada_w", g_ada_w[None], ada_w[0], m_ada_w[0], v_ada_w[0])

    results = [{}, {}, {}, {}]
    for which in range(4):
        out = small_out[which]
        results[which]['ada_b'] = jnp.concatenate([_unpack(out, rows_e + l_offs[0], (1, 2 * d)),
                                                   _unpack(out, e_offs[0], (1, 4 * d))], axis=1)
        results[which]['norm_mix_g'] = _unpack(out, rows_e + l_offs[1], norm_mix_g.shape)
        for k, off in zip(early, e_offs[1:]):
            results[which][k] = _unpack(out, off, given[k].shape)
        for k in sharded:
            results[which][k] = sharded_out[k][which][None]
        results[which]['ada_w'] = ada_out[which][None]
    outs = [loss, grad_x[None]]
    for which in range(4):
        outs += [results[which][k] for k in names]
    return tuple(outs)
```

```python
import functools
import math

import jax
import jax.numpy as jnp
from jax import lax
from jax.experimental import pallas as pl
from jax.experimental.pallas import tpu as pltpu

F32, BF16 = jnp.float32, jnp.bfloat16
MESH = pl.DeviceIdType.MESH
N_DEV = 8

HEAD_DIM = 64
N_KV_HEADS = 2
ATT_BLOCK = 128
NEG_INF = -1e30
SSM_P = 16
SSM_N = 64
LANES = 128
TILE_GROUPS = LANES // SSM_P
TILE_STATES = TILE_GROUPS * SSM_N
RMS_EPS = 1e-6
ADAM_LR, ADAM_B1, ADAM_B2, ADAM_EPS, ADAM_WD, ADAM_STEP = 0.001, 0.9, 0.999, 1e-08, 0.01, 10
VMEM_LIMIT = 56 * 1024 * 1024
MATMUL_VMEM_BUDGET = 44 * 1024 * 1024


def _params(n_axes):
    return pltpu.CompilerParams(dimension_semantics=("arbitrary",) * n_axes, vmem_limit_bytes=VMEM_LIMIT)


def _pick(dim, pref, align=128):
    if dim <= align:
        return dim
    t = (min(pref, dim) // align) * align
    while t > align and dim % t:
        t -= align
    assert dim % t == 0, (dim, pref, align)
    return t


def _dev():
    return lax.axis_index("x"), lax.axis_index("y"), lax.axis_index("c")


def _tile_call(name, fn, grid, ins, in_specs, out_shapes, out_specs, acc=()):
    n_in, n_out = len(ins), len(out_shapes)
    acc_axis = len(grid) - 1

    def body(*refs):
        vals = fn(*[r[...] for r in refs[:n_in]])
        if not isinstance(vals, (tuple, list)):
            vals = (vals,)
        assert len(vals) == n_out
        for i, (r, v) in enumerate(zip(refs[n_in:], vals)):
            v = v.astype(r.dtype)
            if i in acc:
                first = pl.program_id(acc_axis) == 0

                @pl.when(first)
                def _():
                    r[...] = v

                @pl.when(jnp.logical_not(first))
                def _():
                    r[...] += v
            else:
                r[...] = v

    return pl.pallas_call(
        body, grid=grid, in_specs=in_specs, out_specs=out_specs, out_shape=out_shapes, name=name,
        compiler_params=_params(len(grid)),
    )(*ins)


def _t(tr, tc, off=0):
    return pl.BlockSpec((tr, tc), lambda j, i: (i, j + off // tc))


def _tt(tr, tc):
    return pl.BlockSpec((tc, tr), lambda j, i: (j, i))


def _v(tc, off=0, rows=1):
    return pl.BlockSpec((rows, tc), lambda j, i: (0, j + off // tc))


HALO = 16


def _prev_rows(tr, tc, off=0):
    return pl.BlockSpec((HALO, tc), lambda j, i: (jnp.maximum(i * (tr // HALO) - 1, 0), j + off // tc))


def _next_rows(tr, tc, nrows, off=0):
    return pl.BlockSpec((HALO, tc),
                        lambda j, i: (jnp.minimum((i + 1) * (tr // HALO), nrows // HALO - 1), j + off // tc))


def _st(tr, tc):
    return pl.BlockSpec((2, tr, tc), lambda j, i: (0, i, j))


def _bc(v, rows):
    return jnp.broadcast_to(v, (rows, v.shape[-1]))


def _colsum(v):
    return jnp.sum(v, axis=0, keepdims=True)


def _matmul(name, a, b, mode, out_dtype=F32, tm=1024, tn=1024, tk=None, out_stack=None, also_bf16=False, dep=None,
            fold=1, epilogue=None):
    def dims(z):
        return (z.shape[-2], z.shape[-1] * (z.shape[0] if z.ndim == 3 else 1))

    ar, ac = dims(a)
    br, bc = dims(b)
    if mode == "nn":
        m, k, n = ar, ac, bc
        assert br == k
    elif mode == "nt":
        m, k, n = ar, ac, br
        assert bc == k
    else:
        m, k, n = ac, ar, bc
        assert br == k
    m_lim, k_lim, n_lim = [m], [k], [n]
    if a.ndim == 3:
        (m_lim if mode == "tn" else k_lim).append(a.shape[-1])
    if b.ndim == 3:
        (k_lim if mode == "nt" else n_lim).append(b.shape[-1])
    if out_stack:
        n_lim.append(n // out_stack)
    tm = _pick(functools.reduce(math.gcd, m_lim), tm)
    tn = _pick(functools.reduce(math.gcd, n_lim), tn)
    k_unit = functools.reduce(math.gcd, k_lim)
    if tk is None:
        sa, sb, so = a.dtype.itemsize, b.dtype.itemsize, jnp.dtype(out_dtype).itemsize + (2 if also_bf16 else 0)
        fits = [t for t in range(LANES, k_unit + 1, LANES) if k_unit % t == 0 and
                2 * t * (tm * sa + tn * sb) + tm * tn * (2 * so + (4 if t < k else 0)) <= MATMUL_VMEM_BUDGET]
        tk = max(fits) if fits else _pick(k_unit, 512)
    else:
        tk = _pick(k_unit, tk)
    assert (k // tk) % fold == 0
    nk = k // (tk * fold)

    def spec(z, brows, bcols, ridx, cidx):
        if z.ndim == 3:
            per = z.shape[-1] // bcols
            return pl.BlockSpec((None, brows, bcols),
                                lambda i, j, kk: (cidx(i, j, kk) // per, ridx(i, j, kk), cidx(i, j, kk) % per))
        return pl.BlockSpec((brows, bcols), lambda i, j, kk: (ridx(i, j, kk), cidx(i, j, kk)))

    gi = lambda i, j, kk: i
    gj = lambda i, j, kk: j
    a_specs, b_specs = [], []
    for f in range(fold):
        gk = lambda i, j, kk, f=f: fold * kk + f
        if mode == "nn":
            a_specs.append(spec(a, tm, tk, gi, gk))
            b_specs.append(spec(b, tk, tn, gk, gj))
            dn = (((1,), (0,)), ((), ()))
        elif mode == "nt":
            a_specs.append(spec(a, tm, tk, gi, gk))
            b_specs.append(spec(b, tn, tk, gj, gk))
            dn = (((1,), (1,)), ((), ()))
        else:
            a_specs.append(spec(a, tk, tm, gk, gi))
            b_specs.append(spec(b, tk, tn, gk, gj))
            dn = (((0,), (0,)), ((), ()))

    epi_fn, epi_ins, epi_outs = epilogue if epilogue else (None, [], [])
    n_out = len(epi_outs) if epilogue else (2 if also_bf16 else 1)

    deps = [] if dep is None else [dep]

    def body(*refs):
        a_refs, b_refs = refs[:fold], refs[fold:2 * fold]
        e_refs = refs[2 * fold:2 * fold + len(epi_ins)]
        rest = refs[2 * fold + len(epi_ins) + len(deps):]
        o_refs, acc = rest[:n_out], rest[n_out:]
        part = None
        for a_ref, b_ref in zip(a_refs, b_refs):
            one = lax.dot_general(a_ref[...].astype(BF16), b_ref[...].astype(BF16), dn, preferred_element_type=F32)
            part = one if part is None else part + one

        def emit(val):
            vals = epi_fn(val, *[r[...] for r in e_refs]) if epilogue else [val] * n_out
            for o_ref, v in zip(o_refs, vals):
                o_ref[...] = v.astype(o_ref.dtype)

        if nk == 1:
            emit(part)
            return
        acc_ref, = acc
        kk = pl.program_id(2)

        @pl.when(kk == 0)
        def _():
            acc_ref[...] = part

        @pl.when(kk > 0)
        def _():
            acc_ref[...] += part

        @pl.when(kk == nk - 1)
        def _():
            emit(acc_ref[...])

    if out_stack:
        per = (n // out_stack) // tn
        out_spec = pl.BlockSpec((None, tm, tn), lambda i, j, kk: (j // per, i, j % per))
        shape = (out_stack, m, n // out_stack)
    else:
        out_spec = pl.BlockSpec((tm, tn), lambda i, j, kk: (i, j))
        shape = (m, n)
    if epilogue:
        assert not out_stack and not also_bf16
        kinds = {False: (pl.BlockSpec((tm, tn), lambda i, j, kk: (i, j)), (m, n)),
                 True: (pl.BlockSpec((tn, tm), lambda i, j, kk: (j, i)), (n, m)),
                 'pair': (pl.BlockSpec((2, tm, tn), lambda i, j, kk: (0, i, j)), (2, m, n))}
        out_specs = [kinds[t][0] for _, t in epi_outs]
        out_shapes = [jax.ShapeDtypeStruct(kinds[t][1], dt) for dt, t in epi_outs]
    else:
        out_specs = [out_spec] * n_out
        out_shapes = [jax.ShapeDtypeStruct(shape, dt) for dt in [out_dtype, BF16][:n_out]]
    e_pairs = [z if isinstance(z, tuple) else (z, 0) for z in epi_ins]
    assert all(off % tn == 0 for _, off in e_pairs)
    e_specs = [pl.BlockSpec((1, tn) if z.shape[0] == 1 else (tm, tn),
                            lambda i, j, kk, ob=off // tn, row=z.shape[0] == 1: (0 if row else i, j + ob))
               for z, off in e_pairs]
    res = pl.pallas_call(
        body, grid=(m // tm, n // tn, nk),
        in_specs=a_specs + b_specs + e_specs + [pl.BlockSpec(memory_space=pl.ANY)] * len(deps),
        out_specs=out_specs, out_shape=out_shapes,
        scratch_shapes=[pltpu.VMEM((tm, tn), F32)] if nk > 1 else [], name=name, compiler_params=_params(3),
    )(*[a] * fold, *[b] * fold, *[z for z, _ in e_pairs], *deps)
    return res if (also_bf16 or epilogue) else res[0]


def _all_gather(name, arrs, dep=None):
    n = len(arrs)
    deps = [] if dep is None else [dep]

    def body(*refs):
        ins, outs = refs[:n], refs[n + len(deps):2 * n + len(deps)]
        send_sems, recv_sems, local_sems = refs[2 * n + len(deps):]
        x, y, c = _dev()
        me, sib = (x, y, c), (x, y, 1 - c)
        x_nbr, y_nbr, diag = (1 - x, y), (x, 1 - y), (1 - x, 1 - y)
        north = c == 1
        relay_from = (jnp.where(north, 1 - x, x), jnp.where(north, y, 1 - y))
        relay_to = (jnp.where(north, x, 1 - x), jnp.where(north, 1 - y, y))

        def slot(p):
            return 4 * p[0] + 2 * p[1] + p[2]

        def copy(a, k, block, to, src=None):
            dst = outs[a].at[slot(block)]
            return pltpu.make_async_remote_copy(
                src_ref=dst if src is None else src, dst_ref=dst,
                send_sem=send_sems.at[7 * a + k], recv_sem=recv_sems.at[7 * a + k],
                device_id=to, device_id_type=MESH)

        mine = [pltpu.make_async_copy(ins[a], outs[a].at[slot(me)], local_sems.at[a]) for a in range(n)]
        for cp in mine:
            cp.start()
        sent = []
        for a in range(n):
            sent += [copy(a, 0, me, sib, src=ins[a]), copy(a, 1, me, (*x_nbr, c), src=ins[a]),
                     copy(a, 2, me, (*y_nbr, c), src=ins[a])]
        for cp in sent:
            cp.start()
        relays = [copy(a, 3, (*relay_from, c), (*relay_to, c)) for a in range(n)]
        for k, chip, relay_here in ((1, x_nbr, north), (2, y_nbr, jnp.logical_not(north)), (3, diag, None)):
            for a in range(n):
                copy(a, k, (*chip, c), me).wait_recv()
                cp = copy(a, 3 + k, (*chip, c), sib)
                cp.start()
                sent.append(cp)
                if relay_here is not None:
                    pl.when(relay_here)(relays[a].start)
        for a in range(n):
            copy(a, 0, sib, me).wait_recv()
            for k, chip in ((4, x_nbr), (5, y_nbr), (6, diag)):
                copy(a, k, (*chip, 1 - c), me).wait_recv()
        for cp in sent + relays:
            cp.wait_send()
        for cp in mine:
            cp.wait()

    any_spec = pl.BlockSpec(memory_space=pl.ANY)
    return pl.pallas_call(
        body, in_specs=[any_spec] * (n + len(deps)), out_specs=[any_spec] * n,
        out_shape=[jax.ShapeDtypeStruct((N_DEV,) + a.shape, a.dtype) for a in arrs],
        scratch_shapes=[pltpu.SemaphoreType.DMA((7 * n,)), pltpu.SemaphoreType.DMA((7 * n,)),
                        pltpu.SemaphoreType.DMA((n,))],
        name=name,
    )(*arrs, *deps)


FLIPS = [(0, 0, 1), (0, 1, 0), (1, 0, 0), (0, 1, 1), (1, 0, 1), (1, 1, 0), (1, 1, 1)]
N_PEERS = len(FLIPS)
_HBM = pl.BlockSpec(memory_space=pltpu.HBM)
_SEM = pl.BlockSpec(memory_space=pltpu.SEMAPHORE)
_EFFECT = pltpu.SideEffectType.DATAFLOW_SIDE_EFFECTING


def _flip(x, y, c, f):
    return (1 - x if f[0] else x, 1 - y if f[1] else y, 1 - c if f[2] else c)


def _slot(p):
    return 4 * p[0] + 2 * p[1] + p[2]


def _exchange_copies(src_refs, land_refs, send_sems, recv_sems, gather):
    x, y, c = _dev()
    mine = _slot((x, y, c))
    cps = []
    for a, (src, land) in enumerate(zip(src_refs, land_refs)):
        for k, f in enumerate(FLIPS):
            peer = _flip(x, y, c, f)
            cps.append(pltpu.make_async_remote_copy(
                src_ref=src if gather else src.at[_slot(peer)], dst_ref=land.at[mine],
                send_sem=send_sems.at[N_PEERS * a + k], recv_sem=recv_sems.at[N_PEERS * a + k],
                device_id=peer, device_id_type=MESH))
    return cps


def _exchange_start(name, srcs, gather, after):
    n = len(srcs)
    lands = [lax.empty(((N_DEV,) + s.shape) if gather else s.shape, s.dtype) for s in srcs]

    def body(*refs):
        src_refs, land_refs = refs[:n], refs[n:2 * n]
        send_sems, recv_sems, local_sems = refs[2 * n + 1:2 * n + 4]
        token = refs[-1]
        if gather:
            x, y, c = _dev()
            for a in range(n):
                pltpu.make_async_copy(src_refs[a], land_refs[a].at[_slot((x, y, c))], local_sems.at[a]).start()
        for cp in _exchange_copies(src_refs, land_refs, send_sems, recv_sems, gather):
            cp.start()
        token[...] = jnp.zeros_like(token)

    hbm = lambda z: pltpu.HBM(z.shape, z.dtype)
    outs = pl.pallas_call(
        body, name=name,
        out_shape=(pltpu.SemaphoreType.DMA((N_PEERS * n,)), pltpu.SemaphoreType.DMA((N_PEERS * n,)),
                   pltpu.SemaphoreType.DMA((n,)), *[hbm(s) for s in srcs], *[hbm(z) for z in lands],
                   jax.ShapeDtypeStruct((8, LANES), F32)),
        in_specs=[_HBM] * (2 * n) + [pl.BlockSpec(memory_space=pl.ANY)],
        out_specs=(_SEM, _SEM, _SEM, *[_HBM] * (2 * n), pl.BlockSpec(memory_space=pltpu.VMEM)),
        input_output_aliases={i: 3 + i for i in range(2 * n)},
        compiler_params=pltpu.CompilerParams(has_side_effects=_EFFECT),
    )(*[pltpu.with_memory_space_constraint(z, pltpu.HBM) for z in list(srcs) + lands], after)
    return (outs[:3], outs[3:3 + n], outs[3 + n:3 + 2 * n], gather), outs[-1]


def _exchange_wait(name, handles, after):
    sems, srcs, lands, gather = handles
    n = len(srcs)

    def body(*refs):
        src_refs, land_refs = refs[:n], refs[n:2 * n]
        send_sems, recv_sems, local_sems = refs[2 * n:2 * n + 3]
        if gather:
            for a in range(n):
                pltpu.make_async_copy(src_refs[a], land_refs[a].at[0], local_sems.at[a]).wait()
        for cp in _exchange_copies(src_refs, land_refs, send_sems, recv_sems, gather):
            cp.wait_send()
            cp.wait_recv()

    hbm = lambda z: pltpu.HBM(z.shape, z.dtype)
    outs = pl.pallas_call(
        body, name=name, out_shape=tuple(hbm(z) for z in list(srcs) + list(lands)),
        in_specs=[_HBM] * (2 * n) + [_SEM] * 3 + [pl.BlockSpec(memory_space=pl.ANY)],
        out_specs=tuple([_HBM] * (2 * n)), input_output_aliases={i: i for i in range(2 * n)},
        compiler_params=pltpu.CompilerParams(has_side_effects=_EFFECT),
    )(*srcs, *lands, *sems, after)
    return list(outs[n:])


def _pack_rows(sizes, width, row_align):
    offs, r = [], 0
    for s in sizes:
        offs.append(r)
        r += -(-s // width)
    total = -(-r // row_align) * row_align
    return offs, total


def _pack(items, width, row_align, lead=()):
    nl = len(lead)
    sizes = [int(jnp.size(a)) // max(1, functools.reduce(lambda p, q: p * q, lead, 1)) for a in items]
    offs, total = _pack_rows(sizes, width, row_align)
    flat = []
    used = 0
    for a, s in zip(items, sizes):
        f = a.reshape(lead + (s,))
        pad = -(-s // width) * width - s
        if pad:
            f = jnp.pad(f, [(0, 0)] * nl + [(0, pad)])
        flat.append(f)
        used += s + pad
    tail = total * width - used
    if tail:
        flat.append(jnp.zeros(lead + (tail,), items[0].dtype))
    return jnp.concatenate(flat, axis=-1).reshape(lead + (total, width)), offs


def _unpack(packed, off, shape, lead=()):
    nl = len(lead)
    size = functools.reduce(lambda p, q: p * q, shape, 1)
    width = packed.shape[-1]
    rows = -(-size // width)
    blk = lax.slice_in_dim(packed, off, off + rows, axis=nl).reshape(lead + (rows * width,))
    return lax.slice_in_dim(blk, 0, size, axis=nl).reshape(lead + tuple(shape))


def _rms(x, g):
    return (x * lax.rsqrt(jnp.mean(x * x, axis=-1, keepdims=True) + RMS_EPS)) * g


def _norm_mod(x, g, sc, sh):
    return _rms(x, g) * (1.0 + sc) + sh


def _mix_fn(glu_a, glu_b, attn, ga, gs):
    return jax.nn.sigmoid(ga) * attn + jax.nn.sigmoid(gs) * (glu_a * jax.nn.sigmoid(glu_b))


def _s5_disc_fn(a_re, a_im, log_dt, b_re, b_im):
    dt = jnp.exp(log_dt)
    mag = jnp.exp(a_re * dt)
    lr, li = mag * jnp.cos(a_im * dt), mag * jnp.sin(a_im * dt)
    den = a_re * a_re + a_im * a_im
    zr = ((lr - 1.0) * a_re + li * a_im) / den
    zi = (li * a_re - (lr - 1.0) * a_im) / den
    return lr, li, zr[None] * b_re - zi[None] * b_im, zr[None] * b_im + zi[None] * b_re


def _adamw_fn(w, g, m, v):
    m = ADAM_B1 * m + (1.0 - ADAM_B1) * g
    v = ADAM_B2 * v + (1.0 - ADAM_B2) * jnp.square(g)
    m_hat = m / (1.0 - ADAM_B1 ** ADAM_STEP)
    v_hat = v / (1.0 - ADAM_B2 ** ADAM_STEP)
    delta = -ADAM_LR * (m_hat / (jnp.sqrt(v_hat) + ADAM_EPS) + ADAM_WD * w)
    return delta, m, v


def _adamw(name, parts, w, m, v):
    p, r, c = parts.shape
    tr = _pick(r, max(8, (1 << 21) // (4 * c * max(p, 2))), 8)

    def fn(pv, wv, mv, vv):
        g = pv[0]
        for i in range(1, p):
            g = g + pv[i]
        d, m2, v2 = _adamw_fn(wv, g, mv, vv)
        return g, d, m2, v2

    spec = pl.BlockSpec((tr, c), lambda i: (i, 0))
    return _tile_call(
        name, fn, (r // tr,), [parts, w, m, v],
        [pl.BlockSpec((p, tr, c), lambda i: (0, i, 0)), spec, spec, spec],
        [jax.ShapeDtypeStruct((r, c), F32)] * 4, [spec] * 4)


def _adamw_sharded(name, parts, own_src, w, m, v, place):
    _, k, n = parts.shape
    tr = _pick(k, max(16, (1 << 19) // (4 * n)), 16)

    def body(pl_ref, p_ref, a_ref, w_ref, m_ref, v_ref, g_ref, d_ref, m2_ref, v2_ref):
        own = a_ref[0]
        g = None
        for q in range(N_DEV):
            term = jnp.where(pl_ref[0] == q, own, p_ref[q].astype(F32))
            g = term if g is None else g + term
        d, m2, v2 = _adamw_fn(w_ref[...], g, m_ref[...], v_ref[...])
        g_ref[...] = g
        d_ref[...] = d
        m2_ref[...] = m2
        v2_ref[...] = v2

    spec = pl.BlockSpec((tr, n), lambda i, pr: (i, 0))
    return pl.pallas_call(
        body,
        grid_spec=pltpu.PrefetchScalarGridSpec(
            num_scalar_prefetch=1, grid=(k // tr,),
            in_specs=[pl.BlockSpec((N_DEV, tr, n), lambda i, pr: (0, i, 0)),
                      pl.BlockSpec((1, tr, n), lambda i, pr: (pr[1], i, 0)),
                      spec, spec, spec],
            out_specs=[spec] * 4),
        out_shape=[jax.ShapeDtypeStruct((k, n), F32)] * 4, name=name, compiler_params=_params(1),
    )(place, parts, own_src, w, m, v)


def _attn_mask(n, rows):
    qi = lax.broadcasted_iota(jnp.int32, (rows, 2 * ATT_BLOCK), 0) & (ATT_BLOCK - 1)
    kj = lax.broadcasted_iota(jnp.int32, (rows, 2 * ATT_BLOCK), 1)
    rel = qi + ATT_BLOCK - kj
    return (rel >= 0) & (rel < ATT_BLOCK) & ((kj >= ATT_BLOCK) | (n > 0))


def _attn_probs(q, k, sink, mask):
    s = lax.dot_general(q, k, (((1,), (1,)), ((), ())), preferred_element_type=F32) * (HEAD_DIM ** -0.5)
    s = jnp.where(mask, s, NEG_INF)
    m = jnp.maximum(jnp.max(s, axis=-1, keepdims=True), sink)
    p = jnp.exp(s - m)
    e_sink = jnp.exp(sink - m)
    inv = 1.0 / (jnp.sum(p, axis=-1, keepdims=True) + e_sink)
    return p * inv, e_sink * inv


def _attn_specs(qpk):
    blk = ATT_BLOCK
    q_spec = pl.BlockSpec((qpk, blk, HEAD_DIM), lambda h, n: (h, n, 0))
    cur = pl.BlockSpec((1, blk, HEAD_DIM), lambda h, n: (h, n, 0))
    prev = pl.BlockSpec((1, blk, HEAD_DIM), lambda h, n: (h, jnp.maximum(n - 1, 0), 0))
    sink_spec = pl.BlockSpec((1, qpk * blk, 1), lambda h, n: (h, 0, 0))
    return q_spec, cur, prev, sink_spec


def _attn_fwd(q, k, v, sinks):
    hq, l, _ = q.shape
    qpk = hq // N_KV_HEADS
    nb = l // ATT_BLOCK
    rows = qpk * ATT_BLOCK
    q_spec, cur, prev, sink_spec = _attn_specs(qpk)

    def body(q_ref, kp_ref, kc_ref, vp_ref, vc_ref, sink_ref, o_ref):
        mask = _attn_mask(pl.program_id(1), rows)
        kk = jnp.concatenate([kp_ref[0], kc_ref[0]], axis=0).astype(BF16)
        vv = jnp.concatenate([vp_ref[0], vc_ref[0]], axis=0).astype(BF16)
        p, _ = _attn_probs(q_ref[...].reshape(rows, HEAD_DIM).astype(BF16), kk, sink_ref[0], mask)
        o = jnp.dot(p.astype(BF16), vv, preferred_element_type=F32)
        o_ref[...] = o.reshape(qpk, ATT_BLOCK, HEAD_DIM).astype(o_ref.dtype)

    return pl.pallas_call(
        body, grid=(N_KV_HEADS, nb), in_specs=[q_spec, prev, cur, prev, cur, sink_spec],
        out_specs=q_spec, out_shape=jax.ShapeDtypeStruct((hq, l, HEAD_DIM), BF16),
        name="attn_fwd", compiler_params=_params(2),
    )(q, k, k, v, v, sinks)


def _attn_bwd(q, k, v, sinks, do):
    hq, l, _ = q.shape
    qpk = hq // N_KV_HEADS
    nb = l // ATT_BLOCK
    blk = ATT_BLOCK
    rows = qpk * blk
    q_spec, cur, prev, sink_spec = _attn_specs(qpk)
    part_spec = pl.BlockSpec((1, 1, 2 * blk, HEAD_DIM), lambda h, n: (h, n, 0, 0))
    dsink_spec = pl.BlockSpec((qpk, 1, LANES), lambda h, n: (h, 0, 0))
    tn = (((0,), (0,)), ((), ()))

    def body(q_ref, do_ref, kp_ref, kc_ref, vp_ref, vc_ref, sink_ref, dq_ref, dkp_ref, dvp_ref, dsink_ref):
        n = pl.program_id(1)
        mask = _attn_mask(n, rows)
        kk = jnp.concatenate([kp_ref[0], kc_ref[0]], axis=0).astype(BF16)
        vv = jnp.concatenate([vp_ref[0], vc_ref[0]], axis=0).astype(BF16)
        qb = q_ref[...].reshape(rows, HEAD_DIM).astype(BF16)
        do32 = do_ref[...].astype(F32).reshape(rows, HEAD_DIM)
        dob = do32.astype(BF16)
        p, p_sink = _attn_probs(qb, kk, sink_ref[0], mask)
        pb = p.astype(BF16)
        o = jnp.dot(pb, vv, preferred_element_type=F32)
        delta = jnp.sum(do32 * o, axis=-1, keepdims=True)
        dp = lax.dot_general(dob, vv, (((1,), (1,)), ((), ())), preferred_element_type=F32)
        ds = (p * (dp - delta) * (HEAD_DIM ** -0.5)).astype(BF16)
        dq = jnp.dot(ds, kk, preferred_element_type=F32)
        dq_ref[...] = dq.reshape(qpk, blk, HEAD_DIM).astype(dq_ref.dtype)
        dkp_ref[0, 0] = lax.dot_general(ds, qb, tn, preferred_element_type=F32)
        dvp_ref[0, 0] = lax.dot_general(pb, dob, tn, preferred_element_type=F32)
        dsr = p_sink * delta
        for g in range(qpk):
            dsg = jnp.broadcast_to(-_colsum(dsr[g * blk:(g + 1) * blk]), (1, LANES))

            @pl.when(n == 0)
            def _():
                dsink_ref[g] = dsg

            @pl.when(n > 0)
            def _():
                dsink_ref[g] += dsg


    part_shape = jax.ShapeDtypeStruct((N_KV_HEADS, nb, 2 * blk, HEAD_DIM), F32)
    dq, dkp, dvp, dsink = pl.pallas_call(
        body, grid=(N_KV_HEADS, nb), in_specs=[q_spec, q_spec, prev, cur, prev, cur, sink_spec],
        out_specs=[q_spec, part_spec, part_spec, dsink_spec],
        out_shape=[jax.ShapeDtypeStruct((hq, l, HEAD_DIM), BF16), part_shape, part_shape,
                   jax.ShapeDtypeStruct((hq, 1, LANES), F32)],
        name="attn_bwd", compiler_params=_params(2),
    )(q, do, k, k, v, v, sinks)

    def combine(a_cur, a_nxt, b_cur, b_nxt):
        last = pl.program_id(1) == nb - 1
        keep = jnp.where(last, 0.0, 1.0)
        return (a_cur[0, 0, blk:] + keep * a_nxt[0, 0, :blk])[None], (b_cur[0, 0, blk:] + keep * b_nxt[0, 0, :blk])[None]

    nxt_spec = pl.BlockSpec((1, 1, 2 * blk, HEAD_DIM), lambda h, n: (h, jnp.minimum(n + 1, nb - 1), 0, 0))
    kv_shape = jax.ShapeDtypeStruct((N_KV_HEADS, l, HEAD_DIM), BF16)
    dk, dv = _tile_call("attn_dkv", combine, (N_KV_HEADS, nb), [dkp, dkp, dvp, dvp],
                        [part_spec, nxt_spec, part_spec, nxt_spec], [kv_shape, kv_shape], [cur, cur])
    return dq, dk, dv, dsink


def _block_diag(m):
    j, gl, a, b = m.shape
    eye = jnp.eye(gl, dtype=m.dtype)
    return (m[:, :, :, None, :] * eye[None, :, None, :, None]).reshape(j, gl * a, gl * b)


def _diag_blocks(z, a):
    j = z.shape[0]
    gl = z.shape[1] // a
    b = z.shape[2] // gl
    d = jnp.diagonal(z.reshape(j, gl, a, gl, b), axis1=1, axis2=3)
    return d.transpose(0, 3, 1, 2)


def _s5_permute(src_ref, dst_ref, t_len):
    seg = t_len // 8
    for k in range(seg):
        dst_ref[8 * k:8 * k + 8, :] = src_ref[pl.ds(k, 8, stride=seg), :]


def _s5_unpermute(perm_ref, t_len, emit):
    per_seg = t_len // 64
    for m in range(t_len // 8):
        emit(8 * m, perm_ref[pl.ds(64 * (m % per_seg) + m // per_seg, 8, stride=8), :])


def _s5_powers(p_ref, lr, li, seg):
    hs = TILE_STATES

    def step(k, carry):
        pr, pi = carry
        p_ref[pl.ds(k, 1), 0:hs] = pr
        p_ref[pl.ds(k, 1), hs:2 * hs] = pi
        return lr * pr - li * pi, lr * pi + li * pr

    lax.fori_loop(0, seg, step, (lr, li))


def _s5_local_scan(x_ref, base, lr, li, seg, reverse):
    hs = TILE_STATES
    lr8, li8 = jnp.broadcast_to(lr, (8, hs)), jnp.broadcast_to(li, (8, hs))
    if reverse:
        li8 = -li8

    def step(i, carry):
        hr, hi = carry
        k = seg - 1 - i if reverse else i
        rows = pl.ds(pl.multiple_of(base + 8 * k, 8), 8)
        nr = lr8 * hr - li8 * hi + x_ref[rows, 0:hs]
        ni = lr8 * hi + li8 * hr + x_ref[rows, hs:2 * hs]
        x_ref[rows, 0:hs] = nr
        x_ref[rows, hs:2 * hs] = ni
        return nr, ni

    zero = jnp.zeros((8, hs), F32)
    return lax.fori_loop(0, seg, step, (zero, zero), unroll=2)


def _s5_carries(c_ref, e_ref, ends, start, pw_r, pw_i, reverse):
    hs = TILE_STATES
    e_ref[:, 0:hs] = ends[0]
    e_ref[:, hs:2 * hs] = ends[1]
    cr, ci = start
    if reverse:
        pw_i = -pw_i
    for s in (range(7, -1, -1) if reverse else range(8)):
        c_ref[s:s + 1, 0:hs] = cr
        c_ref[s:s + 1, hs:2 * hs] = ci
        er, ei = e_ref[s:s + 1, 0:hs], e_ref[s:s + 1, hs:2 * hs]
        cr, ci = er + pw_r * cr - pw_i * ci, ei + pw_r * ci + pw_i * cr
    return cr, ci


def _s5_states(u_perm_b16, bd_ref, x_ref, base, c_ref, e_ref, p_ref, lr, li, h_in, t_len):
    hs = TILE_STATES
    seg = t_len // 8
    x_ref[pl.ds(base, t_len), :] = jnp.dot(u_perm_b16, bd_ref[0], preferred_element_type=F32)
    ends = _s5_local_scan(x_ref, base, lr, li, seg, False)
    pw_r, pw_i = p_ref[seg - 1:seg, 0:hs], p_ref[seg - 1:seg, hs:2 * hs]
    h_out = _s5_carries(c_ref, e_ref, ends, h_in, pw_r, pw_i, False)
    cr, ci = c_ref[:, 0:hs], c_ref[:, hs:2 * hs]

    def fix(k, carry):
        rows = pl.ds(pl.multiple_of(base + 8 * k, 8), 8)
        pr, pi = p_ref[pl.ds(k, 1), 0:hs], p_ref[pl.ds(k, 1), hs:2 * hs]
        x_ref[rows, 0:hs] += pr * cr - pi * ci
        x_ref[rows, hs:2 * hs] += pr * ci + pi * cr
        return carry

    lax.fori_loop(0, seg, fix, 0, unroll=2)
    return h_out


def _s5_fwd(proj, u_off, bd, cbd, lam, dvec, t_len):
    l = proj.shape[0]
    nj = bd.shape[0]
    nch = l // t_len
    hs = TILE_STATES
    ub = u_off // LANES
    seg = t_len // 8
    assert t_len % 64 == 0

    def body(u_ref, bd_ref, cbd_ref, lam_ref, d_ref, y_ref, hst_ref, x_ref, h_ref, p_ref, c_ref, e_ref, up_ref, yp_ref):
        lr, li = lam_ref[0, 0:1, :], lam_ref[0, 1:2, :]

        @pl.when(pl.program_id(1) == 0)
        def _():
            h_ref[...] = jnp.zeros_like(h_ref)
            _s5_powers(p_ref, lr, li, seg)

        hst_ref[0, 0] = h_ref[...]
        _s5_permute(u_ref, up_ref, t_len)
        h_out = _s5_states(up_ref[...].astype(BF16), bd_ref, x_ref, 0, c_ref, e_ref, p_ref, lr, li,
                           (h_ref[:, 0:hs], h_ref[:, hs:2 * hs]), t_len)
        h_ref[:, 0:hs] = h_out[0]
        h_ref[:, hs:2 * hs] = h_out[1]
        yp_ref[...] = jnp.dot(x_ref[...].astype(BF16), cbd_ref[0], preferred_element_type=F32)
        dv = d_ref[0]

        def out(r0, rows):
            y_ref[r0:r0 + 8, :] = rows + dv * u_ref[r0:r0 + 8, :]

        _s5_unpermute(yp_ref, t_len, out)

    return pl.pallas_call(
        body, grid=(nj, nch),
        in_specs=[pl.BlockSpec((t_len, LANES), lambda j, c: (c, ub + j)),
                  pl.BlockSpec((1, LANES, 2 * hs), lambda j, c: (j, 0, 0)),
                  pl.BlockSpec((1, 2 * hs, LANES), lambda j, c: (j, 0, 0)),
                  pl.BlockSpec((1, 2, hs), lambda j, c: (j, 0, 0)),
                  pl.BlockSpec((1, 1, LANES), lambda j, c: (j, 0, 0))],
        out_specs=[pl.BlockSpec((t_len, LANES), lambda j, c: (c, j)),
                   pl.BlockSpec((1, 1, 1, 2 * hs), lambda j, c: (j, c, 0, 0))],
        out_shape=[jax.ShapeDtypeStruct((l, nj * LANES), F32),
                   jax.ShapeDtypeStruct((nj, nch, 1, 2 * hs), F32)],
        scratch_shapes=[pltpu.VMEM((t_len, 2 * hs), F32), pltpu.VMEM((1, 2 * hs), F32),
                        pltpu.VMEM((seg, 2 * hs), F32), pltpu.VMEM((8, 2 * hs), F32), pltpu.VMEM((8, 2 * hs), F32),
                        pltpu.VMEM((t_len, LANES), F32), pltpu.VMEM((t_len, LANES), F32)],
        name="s5_fwd", compiler_params=_params(2),
    )(proj, bd, cbd, lam, dvec)


def _s5_bwd(proj, u_off, dy, hst, bd, bdt, cbdt, lam, dvec, t_len):
    l = proj.shape[0]
    nj = bd.shape[0]
    nch = l // t_len
    hs = TILE_STATES
    ub = u_off // LANES
    seg = t_len // 8
    tn = (((0,), (0,)), ((), ()))
    assert t_len % 64 == 0

    def body(u_ref, dy_ref, hst_ref, bd_ref, bdt_ref, cbdt_ref, lam_ref, d_ref,
             du_ref, dbd_ref, dcbdt_ref, dlam_ref, dd_ref,
             x_ref, g_ref, gc_ref, p_ref, c_ref, e_ref, up_ref, dyp_ref, dup_ref):
        first = pl.program_id(1) == 0
        lr, li = lam_ref[0, 0:1, :], lam_ref[0, 1:2, :]

        @pl.when(first)
        def _():
            gc_ref[...] = jnp.zeros_like(gc_ref)
            _s5_powers(p_ref, lr, li, seg)

        _s5_permute(u_ref, up_ref, t_len)
        _s5_permute(dy_ref, dyp_ref, t_len)
        ub16, dyb16 = up_ref[...].astype(BF16), dyp_ref[...].astype(BF16)
        h0 = hst_ref[0, 0]
        _s5_states(ub16, bd_ref, x_ref, 8, c_ref, e_ref, p_ref, lr, li, (h0[:, 0:hs], h0[:, hs:2 * hs]), t_len)
        x_ref[0:8, :] = c_ref[...]
        g_ref[...] = jnp.dot(dyb16, cbdt_ref[0], preferred_element_type=F32)
        starts = _s5_local_scan(g_ref, 0, lr, li, seg, True)
        pw_r, pw_i = p_ref[seg - 1:seg, 0:hs], p_ref[seg - 1:seg, hs:2 * hs]
        g_out = _s5_carries(c_ref, e_ref, starts, (gc_ref[:, 0:hs], gc_ref[:, hs:2 * hs]), pw_r, pw_i, True)
        gc_ref[:, 0:hs] = g_out[0]
        gc_ref[:, hs:2 * hs] = g_out[1]
        cr, ci = c_ref[:, 0:hs], c_ref[:, hs:2 * hs]

        def fix(k, carry):
            alr, ali = carry
            rows = pl.ds(pl.multiple_of(8 * k, 8), 8)
            pr, pi = p_ref[pl.ds(seg - 1 - k, 1), 0:hs], p_ref[pl.ds(seg - 1 - k, 1), hs:2 * hs]
            gr = g_ref[rows, 0:hs] + pr * cr + pi * ci
            gi = g_ref[rows, hs:2 * hs] + pr * ci - pi * cr
            g_ref[rows, 0:hs] = gr
            g_ref[rows, hs:2 * hs] = gi
            hpr, hpi = x_ref[rows, 0:hs], x_ref[rows, hs:2 * hs]
            return alr + gr * hpr + gi * hpi, ali + gi * hpr - gr * hpi

        zero = jnp.zeros((8, hs), F32)
        alr, ali = lax.fori_loop(0, seg, fix, (zero, zero), unroll=2)
        alr, ali = _colsum(alr), _colsum(ali)
        g = g_ref[...].astype(BF16)
        h = x_ref[pl.ds(8, t_len), :].astype(BF16)
        dup_ref[...] = jnp.dot(g, bdt_ref[0], preferred_element_type=F32)
        dv = d_ref[0]

        def out(r0, rows):
            du_ref[r0:r0 + 8, :] = (rows + dv * dy_ref[r0:r0 + 8, :]).astype(du_ref.dtype)

        _s5_unpermute(dup_ref, t_len, out)
        sign = jnp.where(lax.broadcasted_iota(jnp.int32, (1, 2 * hs), 1) < hs, 1.0, -1.0)
        dbd = lax.dot_general(ub16, g, tn, preferred_element_type=F32)
        dcbdt = lax.dot_general(dyb16, h, tn, preferred_element_type=F32) * sign
        ddv = _colsum(dy_ref[...] * u_ref[...])

        @pl.when(first)
        def _():
            dbd_ref[0] = dbd
            dcbdt_ref[0] = dcbdt
            dlam_ref[0, 0:1, :] = alr
            dlam_ref[0, 1:2, :] = ali
            dd_ref[0] = ddv

        @pl.when(jnp.logical_not(first))
        def _():
            dbd_ref[0] += dbd
            dcbdt_ref[0] += dcbdt
            dlam_ref[0, 0:1, :] += alr
            dlam_ref[0, 1:2, :] += ali
            dd_ref[0] += ddv

    rev = lambda c: nch - 1 - c
    wide = pl.BlockSpec((1, LANES, 2 * hs), lambda j, c: (j, 0, 0))
    tall = pl.BlockSpec((1, 2 * hs, LANES), lambda j, c: (j, 0, 0))
    return pl.pallas_call(
        body, grid=(nj, nch),
        in_specs=[pl.BlockSpec((t_len, LANES), lambda j, c: (rev(c), ub + j)),
                  pl.BlockSpec((t_len, LANES), lambda j, c: (rev(c), j)),
                  pl.BlockSpec((1, 1, 1, 2 * hs), lambda j, c: (j, rev(c), 0, 0)),
                  wide, tall, wide,
                  pl.BlockSpec((1, 2, hs), lambda j, c: (j, 0, 0)),
                  pl.BlockSpec((1, 1, LANES), lambda j, c: (j, 0, 0))],
        out_specs=[pl.BlockSpec((t_len, LANES), lambda j, c: (rev(c), j)),
                   wide, wide,
                   pl.BlockSpec((1, 2, hs), lambda j, c: (j, 0, 0)),
                   pl.BlockSpec((1, 1, LANES), lambda j, c: (j, 0, 0))],
        out_shape=[jax.ShapeDtypeStruct((l, nj * LANES), BF16),
                   jax.ShapeDtypeStruct((nj, LANES, 2 * hs), F32),
                   jax.ShapeDtypeStruct((nj, LANES, 2 * hs), F32),
                   jax.ShapeDtypeStruct((nj, 2, hs), F32),
                   jax.ShapeDtypeStruct((nj, 1, LANES), F32)],
        scratch_shapes=[pltpu.VMEM((t_len + 8, 2 * hs), F32), pltpu.VMEM((t_len, 2 * hs), F32),
                        pltpu.VMEM((1, 2 * hs), F32), pltpu.VMEM((seg, 2 * hs), F32),
                        pltpu.VMEM((8, 2 * hs), F32), pltpu.VMEM((8, 2 * hs), F32),
                        pltpu.VMEM((t_len, LANES), F32), pltpu.VMEM((t_len, LANES), F32),
                        pltpu.VMEM((t_len, LANES), F32)],
        name="s5_bwd", compiler_params=_params(2),
    )(proj, dy, hst, bd, bdt, cbdt, lam, dvec)


def _full_spec(shape):
    nd = len(shape)
    return pl.BlockSpec(tuple(shape), lambda i: (0,) * nd)


def _sds(shape, dtype=F32):
    return jax.ShapeDtypeStruct(tuple(shape), dtype)


def kernel(x, c, ada_w, ada_b, norm_mix_g, w_in, attn_sinks, w_attn_proj, ssm_a_re, ssm_a_im, ssm_log_dt, ssm_b_re, ssm_b_im, ssm_c_re, ssm_c_im, ssm_d, w_ssm_glu, w_out, norm_ffn_g, w_ffn_up, ffn_conv_w, ffn_conv_b, w_ffn_down, final_g, loss_target, m_ada_w, m_ada_b, m_norm_mix_g, m_w_in, m_attn_sinks, m_w_attn_proj, m_ssm_a_re, m_ssm_a_im, m_ssm_log_dt, m_ssm_b_re, m_ssm_b_im, m_ssm_c_re, m_ssm_c_im, m_ssm_d, m_w_ssm_glu, m_w_out, m_norm_ffn_g, m_w_ffn_up, m_ffn_conv_w, m_ffn_conv_b, m_w_ffn_down, m_final_g, v_ada_w, v_ada_b, v_norm_mix_g, v_w_in, v_attn_sinks, v_w_attn_proj, v_ssm_a_re, v_ssm_a_im, v_ssm_log_dt, v_ssm_b_re, v_ssm_b_im, v_ssm_c_re, v_ssm_c_im, v_ssm_d, v_w_ssm_glu, v_w_out, v_norm_ffn_g, v_w_ffn_up, v_ffn_conv_w, v_ffn_conv_b, v_w_ffn_down, v_final_g):
    given = dict(locals())
    names = ['ada_w', 'ada_b', 'norm_mix_g', 'w_in', 'attn_sinks', 'w_attn_proj', 'ssm_a_re', 'ssm_a_im',
             'ssm_log_dt', 'ssm_b_re', 'ssm_b_im', 'ssm_c_re', 'ssm_c_im', 'ssm_d', 'w_ssm_glu', 'w_out',
             'norm_ffn_g', 'w_ffn_up', 'ffn_conv_w', 'ffn_conv_b', 'w_ffn_down', 'final_g']

    xs = x[0]
    tgt = loss_target[0]
    l, d = xs.shape
    attn_w = w_attn_proj.shape[1]
    ssm_w = w_ssm_glu.shape[1]
    hq = attn_sinks.shape[1]
    qpk = hq // N_KV_HEADS
    kv_w = N_KV_HEADS * HEAD_DIM
    n_groups = ssm_a_re.shape[1]
    dff = ffn_conv_b.shape[1]
    in_w = attn_w + 2 * kv_w + ssm_w + 2 * d
    nj = ssm_w // LANES
    off_k, off_v, off_u = attn_w, attn_w + kv_w, attn_w + 2 * kv_w
    off_ga, off_gs = off_u + ssm_w, off_u + ssm_w + d
    assert hq * HEAD_DIM == attn_w and n_groups * SSM_P == ssm_w and l % ATT_BLOCK == 0

    xi, yi, ci = _dev()
    idx = 4 * xi + 2 * yi + ci

    row_sharded = {'w_out': (d, d), 'w_ffn_down': (dff, d)}
    big = ['w_in', 'w_attn_proj', 'w_ssm_glu', 'w_out', 'w_ffn_up', 'w_ffn_down']
    spack, s_offs = _pack([c, ffn_conv_w[0]], LANES, 8)
    w16 = {k: given[k][0].astype(BF16) for k in big}
    wg_in, sg = _all_gather("gather_first", [w16['w_in'], spack])
    mixer_w = ['w_attn_proj', 'w_ssm_glu', 'w_out']
    h_mixer, tok = _exchange_start("gather_mixer_start", [w16[k] for k in mixer_w], True, wg_in)
    h_up, tok = _exchange_start("gather_ffn_up_start", [w16['w_ffn_up']], True, tok)
    h_down, tok = _exchange_start("gather_ffn_down_start", [w16['w_ffn_down']], True, tok)
    full = {'w_in': wg_in.transpose(1, 0, 2).reshape(d, in_w)}
    c_all = _unpack(sg, s_offs[0], (d,), lead=(N_DEV,))
    conv_w = _unpack(sg, s_offs[1], ffn_conv_w.shape[1:], lead=(N_DEV,)).transpose(1, 0, 2).reshape(3, dff)
    conv_b = ffn_conv_b

    mod_n = ada_w.shape[2]
    tcm = _pick(mod_n, 512)
    ada_b_mine = lax.dynamic_slice_in_dim(ada_b, idx * mod_n, mod_n, axis=1)

    def modpart_fn(cv, wv, bv):
        cond = cv * jax.nn.sigmoid(cv)
        return jnp.dot(cond.astype(BF16), wv.astype(BF16), preferred_element_type=F32) + bv, cond

    modp, cond_all = _tile_call(
        "ada_rows", modpart_fn, (mod_n // tcm,), [c_all, ada_w[0], ada_b_mine],
        [pl.BlockSpec((N_DEV, d), lambda j: (0, 0)), pl.BlockSpec((d, tcm), lambda j: (0, j)),
         pl.BlockSpec((1, tcm), lambda j: (0, j))],
        [_sds((N_DEV, mod_n)), _sds((N_DEV, d))],
        [pl.BlockSpec((N_DEV, tcm), lambda j: (0, j)), pl.BlockSpec((N_DEV, d), lambda j: (0, 0))])
    (modg,) = _all_gather("gather_ada_rows", [modp])
    mod = lax.dynamic_index_in_dim(modg, idx, axis=1, keepdims=False).reshape(1, N_DEV * mod_n)
    sh1, sc1, g1, sh2, sc2, g2 = [mod[:, i * d:(i + 1) * d] for i in range(6)]

    tr = _pick(l, 256, 8)
    trh = _pick(l, 128, 8)
    nr, nrh = l // tr, l // trh
    g_mix, g_ffn, g_fin = norm_mix_g + tok[0:1, 0:1], norm_ffn_g, final_g.reshape(1, d)

    def with_t(fn):
        def wrapped(*vals):
            out = fn(*vals)
            out = out if isinstance(out, tuple) else (out,)
            return out + (out[-1].T,)
        return wrapped

    h1, h1_t = _tile_call("norm_mod_mix", with_t(_norm_mod), (1, nr), [xs, g_mix, sc1, sh1],
                          [_t(tr, d), _v(d), _v(d), _v(d)], [_sds((l, d), BF16), _sds((d, l), BF16)],
                          [_t(tr, d), _tt(tr, d)])
    proj = _matmul("proj_in", h1, full['w_in'], "nn", tn=1280)

    def heads(z, n):
        return z.reshape(l, n, HEAD_DIM).transpose(1, 0, 2)

    qh = heads(proj[:, :attn_w], hq)
    kh = heads(proj[:, off_k:off_k + kv_w], N_KV_HEADS)
    vh = heads(proj[:, off_v:off_v + kv_w], N_KV_HEADS)
    sinks3 = jnp.repeat(attn_sinks.reshape(N_KV_HEADS, qpk), ATT_BLOCK, axis=1)[..., None]
    o_h = _attn_fwd(qh, kh, vh, sinks3)
    o2 = o_h.transpose(1, 0, 2).reshape(l, attn_w)

    gn = (n_groups, SSM_N)
    pgn = (SSM_P, n_groups, SSM_N)
    a_re, a_im, log_dt = ssm_a_re[0], ssm_a_im[0], ssm_log_dt[0].reshape(n_groups, 1)
    b_re, b_im = ssm_b_re[0].transpose(2, 0, 1), ssm_b_im[0].transpose(2, 0, 1)
    disc_ins = [a_re, a_im, log_dt, b_re, b_im]
    disc_specs = [_full_spec(gn), _full_spec(gn), _full_spec((n_groups, 1)), _full_spec(pgn), _full_spec(pgn)]
    lam_r, lam_i, bb_r, bb_i = _tile_call(
        "s5_discretise", _s5_disc_fn, (1,), disc_ins, disc_specs,
        [_sds(gn), _sds(gn), _sds(pgn), _sds(pgn)],
        [_full_spec(gn), _full_spec(gn), _full_spec(pgn), _full_spec(pgn)])

    def tiles_gpn(z):
        return z.reshape(SSM_P, nj, TILE_GROUPS, SSM_N).transpose(1, 2, 0, 3)

    bd = jnp.concatenate([_block_diag(tiles_gpn(bb_r)), _block_diag(tiles_gpn(bb_i))], axis=2).astype(BF16)
    c_r = ssm_c_re[0].reshape(nj, TILE_GROUPS, SSM_P, SSM_N).transpose(0, 1, 3, 2)
    c_i = (-ssm_c_im[0]).reshape(nj, TILE_GROUPS, SSM_P, SSM_N).transpose(0, 1, 3, 2)
    cbd = jnp.concatenate([_block_diag(c_r), _block_diag(c_i)], axis=1).astype(BF16)
    bdt, cbdt = bd.transpose(0, 2, 1), cbd.transpose(0, 2, 1)
    lam = jnp.stack([lam_r.reshape(nj, TILE_STATES), lam_i.reshape(nj, TILE_STATES)], axis=1)
    dvec = ssm_d[0].reshape(nj, 1, LANES)
    t_len = _pick(l, 512, 8)
    y, hst = _s5_fwd(proj, off_u, bd, cbd, lam, dvec, t_len)

    tcs, trg = _pick(ssm_w, 1024), _pick(l, 512, 8)
    gy = _tile_call("gelu", lambda v: jax.nn.gelu(v), (ssm_w // tcs, l // trg), [y], [_t(trg, tcs)],
                    [_sds((l, ssm_w), BF16)], [_t(trg, tcs)])[0]
    full.update(zip(mixer_w, _exchange_wait("gather_mixer_wait", h_mixer, gy)))
    full['w_out'] = full['w_out'].reshape(row_sharded['w_out'])
    full['w_attn_proj'] = full['w_attn_proj'].transpose(1, 0, 2).reshape(attn_w, d)
    full['w_ssm_glu'] = full['w_ssm_glu'].transpose(1, 0, 2).reshape(ssm_w, 2 * d)
    glu = _matmul("ssm_glu", gy, full['w_ssm_glu'], "nn")

    tcd = 256 if d % 256 == 0 and off_ga % 256 == 0 else LANES
    assert d % tcd == 0 and off_ga % tcd == 0 and off_gs % tcd == 0
    gate_ins = [(glu, 0), (glu, d), (proj, off_ga), (proj, off_gs)]

    def mix_epilogue(at, ga_, gb_, pa, ps):
        return at, _mix_fn(ga_, gb_, at, pa, ps)

    attn, mixed = _matmul("attn_proj_gate_mix", o2, full['w_attn_proj'], "nn", tn=tcd,
                          epilogue=(mix_epilogue, gate_ins, [(F32, False), (BF16, False)]))
    def res_norm_fn(xv, mo, g1v, gv, scv, shv):
        x2v = xv + g1v * mo
        return x2v, _norm_mod(x2v, gv, scv, shv)

    def res_norm_epilogue(mo, xv, g1v, gv, scv, shv):
        x2v, h2v = res_norm_fn(xv, mo, g1v, gv, scv, shv)
        return mo, x2v, h2v, h2v.T

    mixout, x2, h2, h2_t = _matmul(
        "mix_out_residual_norm_mod_ffn", mixed, full['w_out'], "nn", tm=256, tn=d,
        epilogue=(res_norm_epilogue, [xs, g1, g_ffn, sc2, sh2], [(F32, False), (F32, False), (BF16, False), (BF16, True)]))
    full['w_ffn_up'], = _exchange_wait("gather_ffn_up_wait", h_up, h2)
    up = _matmul("ffn_up", h2, full['w_ffn_up'], "nn", out_dtype=BF16, tn=1408)

    tcf, trc = _pick(dff, 1408), _pick(l, 512, 8)
    assert dff % tcf == 0
    ncf = dff // tcf

    taps = [conv_w[i:i + 1] for i in range(3)]

    def conv_gate(gp, gp_prev, w0, w1, w2, bv):
        gp = gp.astype(F32)
        prev = jnp.where(pl.program_id(1) == 0, 0.0, 1.0) * gp_prev.astype(F32)
        ext = jnp.concatenate([prev, gp], axis=0)
        m1 = pltpu.roll(ext, 1, 0)[HALO:]
        m2 = pltpu.roll(ext, 2, 0)[HALO:]
        return w0 * m2 + w1 * m1 + w2 * gp + bv, m1, m2

    def convglu_fn(gp, gp_prev, val, w0, w1, w2, bv):
        gate, _, _ = conv_gate(gp, gp_prev, w0, w1, w2, bv)
        return gate * jax.nn.sigmoid(gate) * val.astype(F32)

    act, act_t = _tile_call("conv_swiglu", with_t(convglu_fn), (ncf, l // trc), [up, up, up] + taps + [conv_b],
                            [_t(trc, tcf), _prev_rows(trc, tcf), _t(trc, tcf, dff)] + [_v(tcf)] * 4,
                            [_sds((l, dff), BF16), _sds((dff, l), BF16)], [_t(trc, tcf), _tt(trc, tcf)])
    full['w_ffn_down'] = _exchange_wait("gather_ffn_down_wait", h_down, act)[0].reshape(row_sharded['w_ffn_down'])
    ffn = _matmul("ffn_down", act, full['w_ffn_down'], "nn", tm=512)

    def final_fn(x2v, fv, g2v, gv, tv):
        rows = x2v.shape[0]

        def loss_of(x2a, fa, g2a, ga):
            out = _rms(x2a + g2a * fa, ga)
            err = out - tv
            return 0.5 * _colsum(jnp.mean(err * err, axis=-1, keepdims=True))

        loss, vjp = jax.vjp(loss_of, x2v, fv, _bc(g2v, rows), _bc(gv, rows))
        dx3, dffn, dg2, dgf = vjp(jnp.ones((1, 1), F32))
        return jnp.broadcast_to(loss, (1, LANES)), dx3, dffn, _colsum(dg2), _colsum(dgf)

    loss_p, dx3, dffn, dg2, dg_fin = _tile_call(
        "loss_final_norm", final_fn, (1, nrh), [x2, ffn, g2, g_fin, tgt],
        [_t(trh, d), _t(trh, d), _v(d), _v(d), _t(trh, d)],
        [_sds((1, LANES)), _sds((l, d)), _sds((l, d), BF16), _sds((1, d)), _sds((1, d))],
        [_v(LANES), _t(trh, d), _t(trh, d), _v(d), _v(d)], acc=(0, 3, 4))
    loss = lax.psum(loss_p[0, 0], ("x", "y", "c"))

    dact = _matmul("d_act", dffn, full['w_ffn_down'], "nt", out_dtype=BF16, tn=1408, dep=loss.reshape(1, 1))
    gd, gd16, pending = {}, {}, []
    dw_down, dw_down16 = _matmul("dw_ffn_down", act_t, dffn, "nn", tm=512, also_bf16=True)
    gd['w_ffn_down'], gd16['w_ffn_down'] = [z.reshape((N_DEV,) + w_ffn_down.shape[1:]) for z in (dw_down, dw_down16)]
    handle, tok = _exchange_start("grad_ffn_down_start", [gd16['w_ffn_down']], False, loss.reshape(1, 1))
    pending.append((['w_ffn_down'], handle))
    conv_b_bwd = conv_b + tok[0:1, 0:1]

    def convglu_bwd_fn(gp, gp_prev, gp_next, val, val_next, da, da_next, w0, w1, w2, bv):
        rows = gp.shape[0]
        i = pl.program_id(1)
        gp, val, da = gp.astype(F32), val.astype(F32), da.astype(F32)
        prev = jnp.where(i == 0, 0.0, 1.0) * gp_prev.astype(F32)
        more = jnp.where(i == pl.num_programs(1) - 1, 0.0, 1.0)
        ext = jnp.concatenate([prev, gp, gp_next.astype(F32)], axis=0)
        cur = ext[HALO:]
        m1 = pltpu.roll(ext, 1, 0)[HALO:]
        m2 = pltpu.roll(ext, 2, 0)[HALO:]
        gate = w0 * m2 + w1 * m1 + w2 * cur + bv
        sg = jax.nn.sigmoid(gate)
        val_e = jnp.concatenate([val, val_next.astype(F32)], axis=0)
        da_e = jnp.concatenate([da, more * da_next.astype(F32)], axis=0)
        dgate = da_e * val_e * (sg * (1.0 + gate * (1.0 - sg)))
        p1 = pltpu.roll(dgate, rows + HALO - 1, 0)[:rows]
        p2 = pltpu.roll(dgate, rows + HALO - 2, 0)[:rows]
        dg = dgate[:rows]
        dgp = w2 * dg + w1 * p1 + w0 * p2
        dval = da * (gate[:rows] * sg[:rows])
        return (jnp.stack([dgp, dval], axis=0), _colsum(dg), _colsum(dg * m2[:rows]), _colsum(dg * m1[:rows]),
                _colsum(dg * gp))

    dup, dconv_b, dcw0, dcw1, dcw2 = _tile_call(
        "conv_swiglu_bwd", convglu_bwd_fn, (ncf, nr), [up, up, up, up, up, dact, dact] + taps + [conv_b_bwd],
        [_t(tr, tcf), _prev_rows(tr, tcf), _next_rows(tr, tcf, l), _t(tr, tcf, dff), _next_rows(tr, tcf, l, dff),
         _t(tr, tcf), _next_rows(tr, tcf, l)] + [_v(tcf)] * 4,
        [_sds((2, l, dff), BF16)] + [_sds((1, dff))] * 4, [_st(tr, tcf)] + [_v(tcf)] * 4, acc=(1, 2, 3, 4))
    dh2 = _matmul("d_h2", dup, full['w_ffn_up'], "nt", tm=512, fold=4)
    gd['w_ffn_up'], gd16['w_ffn_up'] = _matmul("dw_ffn_up", h2_t, dup, "nn", tm=512, tn=1408, out_stack=N_DEV, also_bf16=True)
    handle, tok = _exchange_start("grad_ffn_up_start", [gd16['w_ffn_up']], False, gd['w_ffn_up'])
    pending.append((['w_ffn_up'], handle))
    g_ffn_bwd = g_ffn + tok[0:1, 0:1]

    def res_norm_bwd_fn(xv, mo, g1v, gv, scv, shv, dhv, dxv):
        rows = xv.shape[0]
        _, vjp = jax.vjp(res_norm_fn, xv, mo, _bc(g1v, rows), _bc(gv, rows), _bc(scv, rows), _bc(shv, rows))
        dx, dmo, dg1v, dgv, dscv, dshv = vjp((dxv, dhv))
        return dx, dmo, _colsum(dg1v), _colsum(dgv), _colsum(dscv), _colsum(dshv)

    dx2, dmixout, dg1, dg_ffn, dsc2, dsh2 = _tile_call(
        "residual_norm_mod_ffn_bwd", res_norm_bwd_fn, (1, nrh), [xs, mixout, g1, g_ffn_bwd, sc2, sh2, dh2, dx3],
        [_t(trh, d), _t(trh, d), _v(d), _v(d), _v(d), _v(d), _t(trh, d), _t(trh, d)],
        [_sds((l, d)), _sds((l, d), BF16)] + [_sds((1, d))] * 4,
        [_t(trh, d), _t(trh, d)] + [_v(d)] * 4, acc=(2, 3, 4, 5))

    def mix_bwd_epilogue(dm, ga_, gb_, pa, ps, at):
        _, vjp = jax.vjp(_mix_fn, ga_, gb_, at, pa, ps)
        da, db, dat, dpa, dps = vjp(dm)
        return jnp.stack([da, db], axis=0), dat, dpa, dps

    dglu, dattn, dga, dgs = _matmul(
        "d_mixed_gate_mix_bwd", dmixout, full['w_out'], "nt", tn=tcd,
        epilogue=(mix_bwd_epilogue, gate_ins + [attn], [(BF16, 'pair')] + [(BF16, False)] * 3))
    dw_out, dw_out16 = _matmul("dw_out", mixed, dmixout, "tn", also_bf16=True)
    gd['w_out'], gd16['w_out'] = [z.reshape((N_DEV,) + w_out.shape[1:]) for z in (dw_out, dw_out16)]

    def gelu_bwd_epilogue(dgy, yv):
        _, vjp = jax.vjp(lambda z: jax.nn.gelu(z), yv)
        return (vjp(dgy)[0],)

    dy, = _matmul("d_gelu_y_gelu_bwd", dglu, full['w_ssm_glu'], "nt", epilogue=(gelu_bwd_epilogue, [y], [(F32, False)]))
    gd['w_ssm_glu'], gd16['w_ssm_glu'] = _matmul("dw_ssm_glu", gy, dglu, "tn", out_stack=N_DEV, also_bf16=True)
    du, dbd, dcbdt, dlam, dd_tiles = _s5_bwd(proj, off_u, dy, hst, bd, bdt, cbdt, lam, dvec, t_len)

    def gpn_of(z):
        return z.transpose(2, 0, 1, 3).reshape(pgn)

    dbb_r = gpn_of(_diag_blocks(dbd[:, :, :TILE_STATES], SSM_P))
    dbb_i = gpn_of(_diag_blocks(dbd[:, :, TILE_STATES:], SSM_P))
    dc_re = _diag_blocks(dcbdt[:, :, :TILE_STATES], SSM_P).reshape(n_groups, SSM_P, SSM_N)
    dc_im = _diag_blocks(dcbdt[:, :, TILE_STATES:], SSM_P).reshape(n_groups, SSM_P, SSM_N)
    dlam_r, dlam_i = dlam[:, 0].reshape(gn), dlam[:, 1].reshape(gn)

    def disc_bwd_fn(ar, ai, ld, br, bi, dlr, dli, dbr, dbi):
        _, vjp = jax.vjp(_s5_disc_fn, ar, ai, ld, br, bi)
        return vjp((dlr, dli, dbr, dbi))

    da_re, da_im, dlog_dt, db_re, db_im = _tile_call(
        "s5_discretise_bwd", disc_bwd_fn, (1,), disc_ins + [dlam_r, dlam_i, dbb_r, dbb_i],
        disc_specs + [_full_spec(gn), _full_spec(gn), _full_spec(pgn), _full_spec(pgn)],
        [_sds(gn), _sds(gn), _sds((n_groups, 1)), _sds(pgn), _sds(pgn)], disc_specs)

    do2 = _matmul("d_attn_heads", dattn, full['w_attn_proj'], "nt")
    gd['w_attn_proj'], gd16['w_attn_proj'] = _matmul("dw_attn_proj", o2, dattn, "tn", out_stack=N_DEV, also_bf16=True)
    handle, tok = _exchange_start("grad_mixer_start", [gd16[k] for k in mixer_w], False, gd['w_attn_proj'])
    pending.append((mixer_w, handle))
    do_h = heads(do2.astype(BF16), hq)
    dq_h, dk_h, dv_h, dsink = _attn_bwd(qh, kh, vh, sinks3 + tok[0:1, 0:1], do_h)

    def unheads(z):
        return z.transpose(1, 0, 2).reshape(l, z.shape[0] * HEAD_DIM)

    early = ['attn_sinks', 'ssm_a_re', 'ssm_a_im', 'ssm_log_dt', 'ssm_b_re', 'ssm_b_im', 'ssm_c_re', 'ssm_c_im',
             'ssm_d', 'norm_ffn_g', 'ffn_conv_b', 'final_g']
    early_grads = {
        'attn_sinks': dsink[:, 0, 0], 'ssm_a_re': da_re, 'ssm_a_im': da_im, 'ssm_log_dt': dlog_dt,
        'ssm_b_re': db_re.transpose(1, 2, 0), 'ssm_b_im': db_im.transpose(1, 2, 0), 'ssm_c_re': dc_re,
        'ssm_c_im': dc_im, 'ssm_d': dd_tiles, 'norm_ffn_g': dg_ffn, 'ffn_conv_b': dconv_b, 'final_g': dg_fin}
    ge_pack, e_offs = _pack([jnp.concatenate([dg1, dsh2, dsc2, dg2], axis=1)] + [early_grads[k] for k in early],
                            LANES, 8)
    h_early, tok = _exchange_start("gather_small_early_start", [ge_pack], True, dsink)

    dproj = jnp.concatenate([unheads(dq_h), unheads(dk_h), unheads(dv_h), du, dga, dgs], axis=1)
    dw_in, dw_in16 = _matmul("dw_in", h1_t, dproj, "nn", tm=512, tn=1280, also_bf16=True, dep=tok)
    dcw = jnp.concatenate([dcw0, dcw1, dcw2], axis=0)
    shard_in, shard_cw = w_in.shape[1:], ffn_conv_w.shape[1:]
    gd16['w_in'] = dw_in16.reshape(shard_in[0], N_DEV, shard_in[1]).transpose(1, 0, 2)
    own_in = lax.dynamic_slice_in_dim(dw_in, idx * shard_in[1], shard_in[1], axis=1)[None]
    gd['ffn_conv_w'] = dcw.reshape(shard_cw[0], N_DEV, shard_cw[1]).transpose(1, 0, 2)
    gd16['ffn_conv_w'] = gd['ffn_conv_w'].astype(BF16)
    handle, tok = _exchange_start("grad_in_start", [gd16['w_in'], gd16['ffn_conv_w']], False, dw_in)
    pending.append((['w_in', 'ffn_conv_w'], handle))
    dh1 = _matmul("d_h1", dproj, full['w_in'], "nt", tm=512, dep=tok)

    def norm_bwd_fn(xv, gv, scv, shv, dhv, dxv):
        rows = xv.shape[0]
        _, vjp = jax.vjp(_norm_mod, xv, _bc(gv, rows), _bc(scv, rows), _bc(shv, rows))
        dx, dgv, dscv, dshv = vjp(dhv)
        return dx + dxv, _colsum(dgv), _colsum(dscv), _colsum(dshv)

    grad_x, dg_mix, dsc1, dsh1 = _tile_call(
        "norm_mod_mix_bwd", norm_bwd_fn, (1, nrh), [xs, g_mix, sc1, sh1, dh1, dx2],
        [_t(trh, d), _v(d), _v(d), _v(d), _t(trh, d), _t(trh, d)],
        [_sds((l, d))] + [_sds((1, d))] * 3, [_t(trh, d)] + [_v(d)] * 3, acc=(1, 2, 3))

    gl_pack, l_offs = _pack([jnp.concatenate([dsh1, dsc1], axis=1), dg_mix], LANES, 8)
    h_late, tok = _exchange_start("gather_small_late_start", [gl_pack], True, grad_x)

    sharded = big + ['ffn_conv_w']
    sharded_out = {}

    def finish(group, handle, after):
        for k, parts in zip(group, _exchange_wait("grad_" + group[0] + "_wait", handle, after)):
            own_src, own_at = (own_in, 0 * idx) if k == 'w_in' else (gd[k], idx)
            sharded_out[k] = _adamw_sharded("adamw_" + k, parts, own_src, given[k][0], given['m_' + k][0],
                                            given['v_' + k][0], jnp.stack([idx, own_at]).astype(jnp.int32))

    for group, handle in pending[:-1]:
        finish(group, handle, tok)
    done = functools.reduce(lambda p, q: p + q, [sharded_out[k][1][0:1, 0:1] for g_, _ in pending[:-1] for k in g_])
    finish(*pending[-1], done)

    ge_all, = _exchange_wait("gather_small_early_wait", h_early, done)
    gl_all, = _exchange_wait("gather_small_late_wait", h_late, sharded_out['w_in'][1])
    gs_all = jnp.concatenate([ge_all, gl_all], axis=1)
    rows_e = ge_pack.shape[0]

    def small_pack(prefix):
        ab = given[prefix + 'ada_b']
        p_early, _ = _pack([ab[:, 2 * d:]] + [given[prefix + k] for k in early], LANES, 8)
        p_late, _ = _pack([ab[:, :2 * d], given[prefix + 'norm_mix_g']], LANES, 8)
        return jnp.concatenate([p_early, p_late], axis=0)

    small_out = _adamw("adamw_replicated", gs_all, small_pack(''), small_pack('m_'), small_pack('v_'))

    dmod_all = jnp.concatenate([_unpack(gl_all, l_offs[0], (2 * d,), lead=(N_DEV,)),
                                _unpack(ge_all, e_offs[0], (4 * d,), lead=(N_DEV,))], axis=1)
    dmod_mine = lax.dynamic_slice_in_dim(dmod_all, idx * mod_n, mod_n, axis=1)
    kpad = LANES - N_DEV
    cond_t = jnp.pad(cond_all.T, ((0, 0), (0, kpad)))
    dmod_pad = jnp.pad(dmod_mine, ((0, kpad), (0, 0)))
    g_ada_w = _matmul("dw_ada", cond_t, dmod_pad, "nn")
    ada_out = _adamw("adamw_ada_w", g_ada_w[None], ada_w[0], m_ada_w[0], v_ada_w[0])

    results = [{}, {}, {}, {}]
    for which in range(4):
        out = small_out[which]
        results[which]['ada_b'] = jnp.concatenate([_unpack(out, rows_e + l_offs[0], (1, 2 * d)),
                                                   _unpack(out, e_offs[0], (1, 4 * d))], axis=1)
        results[which]['norm_mix_g'] = _unpack(out, rows_e + l_offs[1], norm_mix_g.shape)
        for k, off in zip(early, e_offs[1:]):
            results[which][k] = _unpack(out, off, given[k].shape)
        for k in sharded:
            results[which][k] = sharded_out[k][which][None]
        results[which]['ada_w'] = ada_out[which][None]
    outs = [loss, grad_x[None]]
    for which in range(4):
        outs += [results[which][k] for k in names]
    return tuple(outs)
```

```python
import functools
import math

import jax
import jax.numpy as jnp
from jax import lax
from jax.experimental import pallas as pl
from jax.experimental.pallas import tpu as pltpu

F32, BF16 = jnp.float32, jnp.bfloat16
MESH = pl.DeviceIdType.MESH
N_DEV = 8

HEAD_DIM = 64
N_KV_HEADS = 2
ATT_BLOCK = 128
NEG_INF = -1e30
SSM_P = 16
SSM_N = 64
LANES = 128
TILE_GROUPS = LANES // SSM_P
TILE_STATES = TILE_GROUPS * SSM_N
RMS_EPS = 1e-6
ADAM_LR, ADAM_B1, ADAM_B2, ADAM_EPS, ADAM_WD, ADAM_STEP = 0.001, 0.9, 0.999, 1e-08, 0.01, 10
VMEM_LIMIT = 56 * 1024 * 1024
MATMUL_VMEM_BUDGET = 44 * 1024 * 1024


def _params(n_axes):
    return pltpu.CompilerParams(dimension_semantics=("arbitrary",) * n_axes, vmem_limit_bytes=VMEM_LIMIT)


def _pick(dim, pref, align=128):
    if dim <= align:
        return dim
    t = (min(pref, dim) // align) * align
    while t > align and dim % t:
        t -= align
    assert dim % t == 0, (dim, pref, align)
    return t


def _dev():
    return lax.axis_index("x"), lax.axis_index("y"), lax.axis_index("c")


def _tile_call(name, fn, grid, ins, in_specs, out_shapes, out_specs, acc=()):
    n_in, n_out = len(ins), len(out_shapes)
    acc_axis = len(grid) - 1

    def body(*refs):
        vals = fn(*[r[...] for r in refs[:n_in]])
        if not isinstance(vals, (tuple, list)):
            vals = (vals,)
        assert len(vals) == n_out
        for i, (r, v) in enumerate(zip(refs[n_in:], vals)):
            v = v.astype(r.dtype)
            if i in acc:
                first = pl.program_id(acc_axis) == 0

                @pl.when(first)
                def _():
                    r[...] = v

                @pl.when(jnp.logical_not(first))
                def _():
                    r[...] += v
            else:
                r[...] = v

    return pl.pallas_call(
        body, grid=grid, in_specs=in_specs, out_specs=out_specs, out_shape=out_shapes, name=name,
        compiler_params=_params(len(grid)),
    )(*ins)


def _t(tr, tc, off=0):
    return pl.BlockSpec((tr, tc), lambda j, i: (i, j + off // tc))


def _tt(tr, tc):
    return pl.BlockSpec((tc, tr), lambda j, i: (j, i))


def _v(tc, off=0, rows=1):
    return pl.BlockSpec((rows, tc), lambda j, i: (0, j + off // tc))


HALO = 16


def _prev_rows(tr, tc, off=0):
    return pl.BlockSpec((HALO, tc), lambda j, i: (jnp.maximum(i * (tr // HALO) - 1, 0), j + off // tc))


def _next_rows(tr, tc, nrows, off=0):
    return pl.BlockSpec((HALO, tc),
                        lambda j, i: (jnp.minimum((i + 1) * (tr // HALO), nrows // HALO - 1), j + off // tc))


def _st(tr, tc):
    return pl.BlockSpec((2, tr, tc), lambda j, i: (0, i, j))


def _bc(v, rows):
    return jnp.broadcast_to(v, (rows, v.shape[-1]))


def _colsum(v):
    return jnp.sum(v, axis=0, keepdims=True)


def _matmul(name, a, b, mode, out_dtype=F32, tm=1024, tn=1024, tk=None, out_stack=None, also_bf16=False, dep=None,
            fold=1, epilogue=None):
    def dims(z):
        return (z.shape[-2], z.shape[-1] * (z.shape[0] if z.ndim == 3 else 1))

    ar, ac = dims(a)
    br, bc = dims(b)
    if mode == "nn":
        m, k, n = ar, ac, bc
        assert br == k
    elif mode == "nt":
        m, k, n = ar, ac, br
        assert bc == k
    else:
        m, k, n = ac, ar, bc
        assert br == k
    m_lim, k_lim, n_lim = [m], [k], [n]
    if a.ndim == 3:
        (m_lim if mode == "tn" else k_lim).append(a.shape[-1])
    if b.ndim == 3:
        (k_lim if mode == "nt" else n_lim).append(b.shape[-1])
    if out_stack:
        n_lim.append(n // out_stack)
    tm = _pick(functools.reduce(math.gcd, m_lim), tm)
    tn = _pick(functools.reduce(math.gcd, n_lim), tn)
    k_unit = functools.reduce(math.gcd, k_lim)
    if tk is None:
        sa, sb, so = a.dtype.itemsize, b.dtype.itemsize, jnp.dtype(out_dtype).itemsize + (2 if also_bf16 else 0)
        fits = [t for t in range(LANES, k_unit + 1, LANES) if k_unit % t == 0 and
                2 * t * (tm * sa + tn * sb) + tm * tn * (2 * so + (4 if t < k else 0)) <= MATMUL_VMEM_BUDGET]
        tk = max(fits) if fits else _pick(k_unit, 512)
    else:
        tk = _pick(k_unit, tk)
    assert (k // tk) % fold == 0
    nk = k // (tk * fold)

    def spec(z, brows, bcols, ridx, cidx):
        if z.ndim == 3:
            per = z.shape[-1] // bcols
            return pl.BlockSpec((None, brows, bcols),
                                lambda i, j, kk: (cidx(i, j, kk) // per, ridx(i, j, kk), cidx(i, j, kk) % per))
        return pl.BlockSpec((brows, bcols), lambda i, j, kk: (ridx(i, j, kk), cidx(i, j, kk)))

    gi = lambda i, j, kk: i
    gj = lambda i, j, kk: j
    a_specs, b_specs = [], []
    for f in range(fold):
        gk = lambda i, j, kk, f=f: fold * kk + f
        if mode == "nn":
            a_specs.append(spec(a, tm, tk, gi, gk))
            b_specs.append(spec(b, tk, tn, gk, gj))
            dn = (((1,), (0,)), ((), ()))
        elif mode == "nt":
            a_specs.append(spec(a, tm, tk, gi, gk))
            b_specs.append(spec(b, tn, tk, gj, gk))
            dn = (((1,), (1,)), ((), ()))
        else:
            a_specs.append(spec(a, tk, tm, gk, gi))
            b_specs.append(spec(b, tk, tn, gk, gj))
            dn = (((0,), (0,)), ((), ()))

    epi_fn, epi_ins, epi_outs = epilogue if epilogue else (None, [], [])
    n_out = len(epi_outs) if epilogue else (2 if also_bf16 else 1)

    deps = [] if dep is None else [dep]

    def body(*refs):
        a_refs, b_refs = refs[:fold], refs[fold:2 * fold]
        e_refs = refs[2 * fold:2 * fold + len(epi_ins)]
        rest = refs[2 * fold + len(epi_ins) + len(deps):]
        o_refs, acc = rest[:n_out], rest[n_out:]
        part = None
        for a_ref, b_ref in zip(a_refs, b_refs):
            one = lax.dot_general(a_ref[...].astype(BF16), b_ref[...].astype(BF16), dn, preferred_element_type=F32)
            part = one if part is None else part + one

        def emit(val):
            vals = epi_fn(val, *[r[...] for r in e_refs]) if epilogue else [val] * n_out
            for o_ref, v in zip(o_refs, vals):
                o_ref[...] = v.astype(o_ref.dtype)

        if nk == 1:
            emit(part)
            return
        acc_ref, = acc
        kk = pl.program_id(2)

        @pl.when(kk == 0)
        def _():
            acc_ref[...] = part

        @pl.when(kk > 0)
        def _():
            acc_ref[...] += part

        @pl.when(kk == nk - 1)
        def _():
            emit(acc_ref[...])

    if out_stack:
        per = (n // out_stack) // tn
        out_spec = pl.BlockSpec((None, tm, tn), lambda i, j, kk: (j // per, i, j % per))
        shape = (out_stack, m, n // out_stack)
    else:
        out_spec = pl.BlockSpec((tm, tn), lambda i, j, kk: (i, j))
        shape = (m, n)
    if epilogue:
        assert not out_stack and not also_bf16
        kinds = {False: (pl.BlockSpec((tm, tn), lambda i, j, kk: (i, j)), (m, n)),
                 True: (pl.BlockSpec((tn, tm), lambda i, j, kk: (j, i)), (n, m)),
                 'pair': (pl.BlockSpec((2, tm, tn), lambda i, j, kk: (0, i, j)), (2, m, n))}
        out_specs = [kinds[t][0] for _, t in epi_outs]
        out_shapes = [jax.ShapeDtypeStruct(kinds[t][1], dt) for dt, t in epi_outs]
    else:
        out_specs = [out_spec] * n_out
        out_shapes = [jax.ShapeDtypeStruct(shape, dt) for dt in [out_dtype, BF16][:n_out]]
    e_pairs = [z if isinstance(z, tuple) else (z, 0) for z in epi_ins]
    assert all(off % tn == 0 for _, off in e_pairs)
    e_specs = [pl.BlockSpec((1, tn) if z.shape[0] == 1 else (tm, tn),
                            lambda i, j, kk, ob=off // tn, row=z.shape[0] == 1: (0 if row else i, j + ob))
               for z, off in e_pairs]
    res = pl.pallas_call(
        body, grid=(m // tm, n // tn, nk),
        in_specs=a_specs + b_specs + e_specs + [pl.BlockSpec(memory_space=pl.ANY)] * len(deps),
        out_specs=out_specs, out_shape=out_shapes,
        scratch_shapes=[pltpu.VMEM((tm, tn), F32)] if nk > 1 else [], name=name, compiler_params=_params(3),
    )(*[a] * fold, *[b] * fold, *[z for z, _ in e_pairs], *deps)
    return res if (also_bf16 or epilogue) else res[0]


def _all_gather(name, arrs, dep=None):
    n = len(arrs)
    deps = [] if dep is None else [dep]

    def body(*refs):
        ins, outs = refs[:n], refs[n + len(deps):2 * n + len(deps)]
        send_sems, recv_sems, local_sems = refs[2 * n + len(deps):]
        x, y, c = _dev()
        me, sib = (x, y, c), (x, y, 1 - c)
        x_nbr, y_nbr, diag = (1 - x, y), (x, 1 - y), (1 - x, 1 - y)
        north = c == 1
        relay_from = (jnp.where(north, 1 - x, x), jnp.where(north, y, 1 - y))
        relay_to = (jnp.where(north, x, 1 - x), jnp.where(north, 1 - y, y))

        def slot(p):
            return 4 * p[0] + 2 * p[1] + p[2]

        def copy(a, k, block, to, src=None):
            dst = outs[a].at[slot(block)]
            return pltpu.make_async_remote_copy(
                src_ref=dst if src is None else src, dst_ref=dst,
                send_sem=send_sems.at[7 * a + k], recv_sem=recv_sems.at[7 * a + k],
                device_id=to, device_id_type=MESH)

        mine = [pltpu.make_async_copy(ins[a], outs[a].at[slot(me)], local_sems.at[a]) for a in range(n)]
        for cp in mine:
            cp.start()
        sent = []
        for a in range(n):
            sent += [copy(a, 0, me, sib, src=ins[a]), copy(a, 1, me, (*x_nbr, c), src=ins[a]),
                     copy(a, 2, me, (*y_nbr, c), src=ins[a])]
        for cp in sent:
            cp.start()
        relays = [copy(a, 3, (*relay_from, c), (*relay_to, c)) for a in range(n)]
        def arrived(a, k, chip):
            copy(a, k, (*chip, c), me).wait_recv()
            cp = copy(a, 3 + k, (*chip, c), sib)
            cp.start()
            sent.append(cp)

        for a in range(n):
            arrived(a, 1, x_nbr)
            pl.when(north)(relays[a].start)
            arrived(a, 2, y_nbr)
            pl.when(jnp.logical_not(north))(relays[a].start)
        for a in range(n):
            arrived(a, 3, diag)
        for a in range(n):
            copy(a, 0, sib, me).wait_recv()
            for k, chip in ((4, x_nbr), (5, y_nbr), (6, diag)):
                copy(a, k, (*chip, 1 - c), me).wait_recv()
        for cp in sent + relays:
            cp.wait_send()
        for cp in mine:
            cp.wait()

    any_spec = pl.BlockSpec(memory_space=pl.ANY)
    return pl.pallas_call(
        body, in_specs=[any_spec] * (n + len(deps)), out_specs=[any_spec] * n,
        out_shape=[jax.ShapeDtypeStruct((N_DEV,) + a.shape, a.dtype) for a in arrs],
        scratch_shapes=[pltpu.SemaphoreType.DMA((7 * n,)), pltpu.SemaphoreType.DMA((7 * n,)),
                        pltpu.SemaphoreType.DMA((n,))],
        name=name,
    )(*arrs, *deps)


FLIPS = [(0, 0, 1), (0, 1, 0), (1, 0, 0), (0, 1, 1), (1, 0, 1), (1, 1, 0), (1, 1, 1)]
N_PEERS = len(FLIPS)
_HBM = pl.BlockSpec(memory_space=pltpu.HBM)
_SEM = pl.BlockSpec(memory_space=pltpu.SEMAPHORE)
_EFFECT = pltpu.SideEffectType.DATAFLOW_SIDE_EFFECTING


def _flip(x, y, c, f):
    return (1 - x if f[0] else x, 1 - y if f[1] else y, 1 - c if f[2] else c)


def _slot(p):
    return 4 * p[0] + 2 * p[1] + p[2]


def _exchange_copies(src_refs, land_refs, send_sems, recv_sems, gather):
    x, y, c = _dev()
    mine = _slot((x, y, c))
    cps = []
    for a, (src, land) in enumerate(zip(src_refs, land_refs)):
        for k, f in enumerate(FLIPS):
            peer = _flip(x, y, c, f)
            cps.append(pltpu.make_async_remote_copy(
                src_ref=src if gather else src.at[_slot(peer)], dst_ref=land.at[mine],
                send_sem=send_sems.at[N_PEERS * a + k], recv_sem=recv_sems.at[N_PEERS * a + k],
                device_id=peer, device_id_type=MESH))
    return cps


def _exchange_start(name, srcs, gather, after):
    n = len(srcs)
    lands = [lax.empty(((N_DEV,) + s.shape) if gather else s.shape, s.dtype) for s in srcs]

    def body(*refs):
        src_refs, land_refs = refs[:n], refs[n:2 * n]
        send_sems, recv_sems, local_sems = refs[2 * n + 1:2 * n + 4]
        token = refs[-1]
        if gather:
            x, y, c = _dev()
            for a in range(n):
                pltpu.make_async_copy(src_refs[a], land_refs[a].at[_slot((x, y, c))], local_sems.at[a]).start()
        for cp in _exchange_copies(src_refs, land_refs, send_sems, recv_sems, gather):
            cp.start()
        token[...] = jnp.zeros_like(token)

    hbm = lambda z: pltpu.HBM(z.shape, z.dtype)
    outs = pl.pallas_call(
        body, name=name,
        out_shape=(pltpu.SemaphoreType.DMA((N_PEERS * n,)), pltpu.SemaphoreType.DMA((N_PEERS * n,)),
                   pltpu.SemaphoreType.DMA((n,)), *[hbm(s) for s in srcs], *[hbm(z) for z in lands],
                   jax.ShapeDtypeStruct((8, LANES), F32)),
        in_specs=[_HBM] * (2 * n) + [pl.BlockSpec(memory_space=pl.ANY)],
        out_specs=(_SEM, _SEM, _SEM, *[_HBM] * (2 * n), pl.BlockSpec(memory_space=pltpu.VMEM)),
        input_output_aliases={i: 3 + i for i in range(2 * n)},
        compiler_params=pltpu.CompilerParams(has_side_effects=_EFFECT),
    )(*[pltpu.with_memory_space_constraint(z, pltpu.HBM) for z in list(srcs) + lands], after)
    return (outs[:3], outs[3:3 + n], outs[3 + n:3 + 2 * n], gather), outs[-1]


def _exchange_wait(name, handles, after):
    sems, srcs, lands, gather = handles
    n = len(srcs)

    def body(*refs):
        src_refs, land_refs = refs[:n], refs[n:2 * n]
        send_sems, recv_sems, local_sems = refs[2 * n:2 * n + 3]
        if gather:
            for a in range(n):
                pltpu.make_async_copy(src_refs[a], land_refs[a].at[0], local_sems.at[a]).wait()
        for cp in _exchange_copies(src_refs, land_refs, send_sems, recv_sems, gather):
            cp.wait_send()
            cp.wait_recv()

    hbm = lambda z: pltpu.HBM(z.shape, z.dtype)
    outs = pl.pallas_call(
        body, name=name, out_shape=tuple(hbm(z) for z in list(srcs) + list(lands)),
        in_specs=[_HBM] * (2 * n) + [_SEM] * 3 + [pl.BlockSpec(memory_space=pl.ANY)],
        out_specs=tuple([_HBM] * (2 * n)), input_output_aliases={i: i for i in range(2 * n)},
        compiler_params=pltpu.CompilerParams(has_side_effects=_EFFECT),
    )(*srcs, *lands, *sems, after)
    return list(outs[n:])


def _pack_rows(sizes, width, row_align):
    offs, r = [], 0
    for s in sizes:
        offs.append(r)
        r += -(-s // width)
    total = -(-r // row_align) * row_align
    return offs, total


def _pack(items, width, row_align, lead=()):
    nl = len(lead)
    sizes = [int(jnp.size(a)) // max(1, functools.reduce(lambda p, q: p * q, lead, 1)) for a in items]
    offs, total = _pack_rows(sizes, width, row_align)
    flat = []
    used = 0
    for a, s in zip(items, sizes):
        f = a.reshape(lead + (s,))
        pad = -(-s // width) * width - s
        if pad:
            f = jnp.pad(f, [(0, 0)] * nl + [(0, pad)])
        flat.append(f)
        used += s + pad
    tail = total * width - used
    if tail:
        flat.append(jnp.zeros(lead + (tail,), items[0].dtype))
    return jnp.concatenate(flat, axis=-1).reshape(lead + (total, width)), offs


def _unpack(packed, off, shape, lead=()):
    nl = len(lead)
    size = functools.reduce(lambda p, q: p * q, shape, 1)
    width = packed.shape[-1]
    rows = -(-size // width)
    blk = lax.slice_in_dim(packed, off, off + rows, axis=nl).reshape(lead + (rows * width,))
    return lax.slice_in_dim(blk, 0, size, axis=nl).reshape(lead + tuple(shape))


def _rms(x, g):
    return (x * lax.rsqrt(jnp.mean(x * x, axis=-1, keepdims=True) + RMS_EPS)) * g


def _norm_mod(x, g, sc, sh):
    return _rms(x, g) * (1.0 + sc) + sh


def _mix_fn(glu_a, glu_b, attn, ga, gs):
    return jax.nn.sigmoid(ga) * attn + jax.nn.sigmoid(gs) * (glu_a * jax.nn.sigmoid(glu_b))


def _s5_disc_fn(a_re, a_im, log_dt, b_re, b_im):
    dt = jnp.exp(log_dt)
    mag = jnp.exp(a_re * dt)
    lr, li = mag * jnp.cos(a_im * dt), mag * jnp.sin(a_im * dt)
    den = a_re * a_re + a_im * a_im
    zr = ((lr - 1.0) * a_re + li * a_im) / den
    zi = (li * a_re - (lr - 1.0) * a_im) / den
    return lr, li, zr[None] * b_re - zi[None] * b_im, zr[None] * b_im + zi[None] * b_re


def _adamw_fn(w, g, m, v):
    m = ADAM_B1 * m + (1.0 - ADAM_B1) * g
    v = ADAM_B2 * v + (1.0 - ADAM_B2) * jnp.square(g)
    m_hat = m / (1.0 - ADAM_B1 ** ADAM_STEP)
    v_hat = v / (1.0 - ADAM_B2 ** ADAM_STEP)
    delta = -ADAM_LR * (m_hat / (jnp.sqrt(v_hat) + ADAM_EPS) + ADAM_WD * w)
    return delta, m, v


def _adamw(name, parts, w, m, v):
    p, r, c = parts.shape
    tr = _pick(r, max(8, (1 << 21) // (4 * c * max(p, 2))), 8)

    def fn(pv, wv, mv, vv):
        g = pv[0]
        for i in range(1, p):
            g = g + pv[i]
        d, m2, v2 = _adamw_fn(wv, g, mv, vv)
        return g, d, m2, v2

    spec = pl.BlockSpec((tr, c), lambda i: (i, 0))
    return _tile_call(
        name, fn, (r // tr,), [parts, w, m, v],
        [pl.BlockSpec((p, tr, c), lambda i: (0, i, 0)), spec, spec, spec],
        [jax.ShapeDtypeStruct((r, c), F32)] * 4, [spec] * 4)


def _adamw_sharded(name, parts, own_src, w, m, v, place):
    _, k, n = parts.shape
    tr = _pick(k, max(16, (1 << 19) // (4 * n)), 16)

    def body(pl_ref, p_ref, a_ref, w_ref, m_ref, v_ref, g_ref, d_ref, m2_ref, v2_ref):
        own = a_ref[0]
        g = None
        for q in range(N_DEV):
            term = jnp.where(pl_ref[0] == q, own, p_ref[q].astype(F32))
            g = term if g is None else g + term
        d, m2, v2 = _adamw_fn(w_ref[...], g, m_ref[...], v_ref[...])
        g_ref[...] = g
        d_ref[...] = d
        m2_ref[...] = m2
        v2_ref[...] = v2

    spec = pl.BlockSpec((tr, n), lambda i, pr: (i, 0))
    return pl.pallas_call(
        body,
        grid_spec=pltpu.PrefetchScalarGridSpec(
            num_scalar_prefetch=1, grid=(k // tr,),
            in_specs=[pl.BlockSpec((N_DEV, tr, n), lambda i, pr: (0, i, 0)),
                      pl.BlockSpec((1, tr, n), lambda i, pr: (pr[1], i, 0)),
                      spec, spec, spec],
            out_specs=[spec] * 4),
        out_shape=[jax.ShapeDtypeStruct((k, n), F32)] * 4, name=name, compiler_params=_params(1),
    )(place, parts, own_src, w, m, v)


def _attn_mask(n, rows):
    qi = lax.broadcasted_iota(jnp.int32, (rows, 2 * ATT_BLOCK), 0) & (ATT_BLOCK - 1)
    kj = lax.broadcasted_iota(jnp.int32, (rows, 2 * ATT_BLOCK), 1)
    rel = qi + ATT_BLOCK - kj
    return (rel >= 0) & (rel < ATT_BLOCK) & ((kj >= ATT_BLOCK) | (n > 0))


def _attn_probs(q, k, sink, mask):
    s = lax.dot_general(q, k, (((1,), (1,)), ((), ())), preferred_element_type=F32) * (HEAD_DIM ** -0.5)
    s = jnp.where(mask, s, NEG_INF)
    m = jnp.maximum(jnp.max(s, axis=-1, keepdims=True), sink)
    p = jnp.exp(s - m)
    e_sink = jnp.exp(sink - m)
    inv = 1.0 / (jnp.sum(p, axis=-1, keepdims=True) + e_sink)
    return p * inv, e_sink * inv


def _attn_specs(qpk):
    blk = ATT_BLOCK
    q_spec = pl.BlockSpec((qpk, blk, HEAD_DIM), lambda h, n: (h, n, 0))
    cur = pl.BlockSpec((1, blk, HEAD_DIM), lambda h, n: (h, n, 0))
    prev = pl.BlockSpec((1, blk, HEAD_DIM), lambda h, n: (h, jnp.maximum(n - 1, 0), 0))
    sink_spec = pl.BlockSpec((1, qpk * blk, 1), lambda h, n: (h, 0, 0))
    return q_spec, cur, prev, sink_spec


def _attn_fwd(q, k, v, sinks):
    hq, l, _ = q.shape
    qpk = hq // N_KV_HEADS
    nb = l // ATT_BLOCK
    rows = qpk * ATT_BLOCK
    q_spec, cur, prev, sink_spec = _attn_specs(qpk)

    def body(q_ref, kp_ref, kc_ref, vp_ref, vc_ref, sink_ref, o_ref):
        mask = _attn_mask(pl.program_id(1), rows)
        kk = jnp.concatenate([kp_ref[0], kc_ref[0]], axis=0).astype(BF16)
        vv = jnp.concatenate([vp_ref[0], vc_ref[0]], axis=0).astype(BF16)
        p, _ = _attn_probs(q_ref[...].reshape(rows, HEAD_DIM).astype(BF16), kk, sink_ref[0], mask)
        o = jnp.dot(p.astype(BF16), vv, preferred_element_type=F32)
        o_ref[...] = o.reshape(qpk, ATT_BLOCK, HEAD_DIM).astype(o_ref.dtype)

    return pl.pallas_call(
        body, grid=(N_KV_HEADS, nb), in_specs=[q_spec, prev, cur, prev, cur, sink_spec],
        out_specs=q_spec, out_shape=jax.ShapeDtypeStruct((hq, l, HEAD_DIM), BF16),
        name="attn_fwd", compiler_params=_params(2),
    )(q, k, k, v, v, sinks)


def _attn_bwd(q, k, v, sinks, do):
    hq, l, _ = q.shape
    qpk = hq // N_KV_HEADS
    nb = l // ATT_BLOCK
    blk = ATT_BLOCK
    rows = qpk * blk
    q_spec, cur, prev, sink_spec = _attn_specs(qpk)
    part_spec = pl.BlockSpec((1, 1, 2 * blk, HEAD_DIM), lambda h, n: (h, n, 0, 0))
    dsink_spec = pl.BlockSpec((qpk, 1, LANES), lambda h, n: (h, 0, 0))
    tn = (((0,), (0,)), ((), ()))

    def body(q_ref, do_ref, kp_ref, kc_ref, vp_ref, vc_ref, sink_ref, dq_ref, dkp_ref, dvp_ref, dsink_ref):
        n = pl.program_id(1)
        mask = _attn_mask(n, rows)
        kk = jnp.concatenate([kp_ref[0], kc_ref[0]], axis=0).astype(BF16)
        vv = jnp.concatenate([vp_ref[0], vc_ref[0]], axis=0).astype(BF16)
        qb = q_ref[...].reshape(rows, HEAD_DIM).astype(BF16)
        do32 = do_ref[...].astype(F32).reshape(rows, HEAD_DIM)
        dob = do32.astype(BF16)
        p, p_sink = _attn_probs(qb, kk, sink_ref[0], mask)
        pb = p.astype(BF16)
        o = jnp.dot(pb, vv, preferred_element_type=F32)
        delta = jnp.sum(do32 * o, axis=-1, keepdims=True)
        dp = lax.dot_general(dob, vv, (((1,), (1,)), ((), ())), preferred_element_type=F32)
        ds = (p * (dp - delta) * (HEAD_DIM ** -0.5)).astype(BF16)
        dq = jnp.dot(ds, kk, preferred_element_type=F32)
        dq_ref[...] = dq.reshape(qpk, blk, HEAD_DIM).astype(dq_ref.dtype)
        dkp_ref[0, 0] = lax.dot_general(ds, qb, tn, preferred_element_type=F32)
        dvp_ref[0, 0] = lax.dot_general(pb, dob, tn, preferred_element_type=F32)
        dsr = p_sink * delta
        for g in range(qpk):
            dsg = jnp.broadcast_to(-_colsum(dsr[g * blk:(g + 1) * blk]), (1, LANES))

            @pl.when(n == 0)
            def _():
                dsink_ref[g] = dsg

            @pl.when(n > 0)
            def _():
                dsink_ref[g] += dsg


    part_shape = jax.ShapeDtypeStruct((N_KV_HEADS, nb, 2 * blk, HEAD_DIM), F32)
    dq, dkp, dvp, dsink = pl.pallas_call(
        body, grid=(N_KV_HEADS, nb), in_specs=[q_spec, q_spec, prev, cur, prev, cur, sink_spec],
        out_specs=[q_spec, part_spec, part_spec, dsink_spec],
        out_shape=[jax.ShapeDtypeStruct((hq, l, HEAD_DIM), BF16), part_shape, part_shape,
                   jax.ShapeDtypeStruct((hq, 1, LANES), F32)],
        name="attn_bwd", compiler_params=_params(2),
    )(q, do, k, k, v, v, sinks)

    def combine(a_cur, a_nxt, b_cur, b_nxt):
        last = pl.program_id(1) == nb - 1
        keep = jnp.where(last, 0.0, 1.0)
        return (a_cur[0, 0, blk:] + keep * a_nxt[0, 0, :blk])[None], (b_cur[0, 0, blk:] + keep * b_nxt[0, 0, :blk])[None]

    nxt_spec = pl.BlockSpec((1, 1, 2 * blk, HEAD_DIM), lambda h, n: (h, jnp.minimum(n + 1, nb - 1), 0, 0))
    kv_shape = jax.ShapeDtypeStruct((N_KV_HEADS, l, HEAD_DIM), BF16)
    dk, dv = _tile_call("attn_dkv", combine, (N_KV_HEADS, nb), [dkp, dkp, dvp, dvp],
                        [part_spec, nxt_spec, part_spec, nxt_spec], [kv_shape, kv_shape], [cur, cur])
    return dq, dk, dv, dsink


def _block_diag(m):
    j, gl, a, b = m.shape
    eye = jnp.eye(gl, dtype=m.dtype)
    return (m[:, :, :, None, :] * eye[None, :, None, :, None]).reshape(j, gl * a, gl * b)


def _diag_blocks(z, a):
    j = z.shape[0]
    gl = z.shape[1] // a
    b = z.shape[2] // gl
    d = jnp.diagonal(z.reshape(j, gl, a, gl, b), axis1=1, axis2=3)
    return d.transpose(0, 3, 1, 2)


def _s5_permute(src_ref, dst_ref, t_len):
    seg = t_len // 8
    for k in range(seg):
        dst_ref[8 * k:8 * k + 8, :] = src_ref[pl.ds(k, 8, stride=seg), :]


def _s5_unpermute(perm_ref, t_len, emit):
    per_seg = t_len // 64
    for m in range(t_len // 8):
        emit(8 * m, perm_ref[pl.ds(64 * (m % per_seg) + m // per_seg, 8, stride=8), :])


def _s5_powers(p_ref, lr, li, seg):
    hs = TILE_STATES

    def step(k, carry):
        pr, pi = carry
        p_ref[pl.ds(k, 1), 0:hs] = pr
        p_ref[pl.ds(k, 1), hs:2 * hs] = pi
        return lr * pr - li * pi, lr * pi + li * pr

    lax.fori_loop(0, seg, step, (lr, li))


def _s5_local_scan(x_ref, base, lr, li, seg, reverse):
    hs = TILE_STATES
    lr8, li8 = jnp.broadcast_to(lr, (8, hs)), jnp.broadcast_to(li, (8, hs))
    if reverse:
        li8 = -li8

    def step(i, carry):
        hr, hi = carry
        k = seg - 1 - i if reverse else i
        rows = pl.ds(pl.multiple_of(base + 8 * k, 8), 8)
        nr = lr8 * hr - li8 * hi + x_ref[rows, 0:hs]
        ni = lr8 * hi + li8 * hr + x_ref[rows, hs:2 * hs]
        x_ref[rows, 0:hs] = nr
        x_ref[rows, hs:2 * hs] = ni
        return nr, ni

    zero = jnp.zeros((8, hs), F32)
    return lax.fori_loop(0, seg, step, (zero, zero), unroll=2)


def _s5_carries(c_ref, e_ref, ends, start, pw_r, pw_i, reverse):
    hs = TILE_STATES
    e_ref[:, 0:hs] = ends[0]
    e_ref[:, hs:2 * hs] = ends[1]
    cr, ci = start
    if reverse:
        pw_i = -pw_i
    for s in (range(7, -1, -1) if reverse else range(8)):
        c_ref[s:s + 1, 0:hs] = cr
        c_ref[s:s + 1, hs:2 * hs] = ci
        er, ei = e_ref[s:s + 1, 0:hs], e_ref[s:s + 1, hs:2 * hs]
        cr, ci = er + pw_r * cr - pw_i * ci, ei + pw_r * ci + pw_i * cr
    return cr, ci


def _s5_states(u_perm_b16, bd_ref, x_ref, base, c_ref, e_ref, p_ref, lr, li, h_in, t_len):
    hs = TILE_STATES
    seg = t_len // 8
    x_ref[pl.ds(base, t_len), :] = jnp.dot(u_perm_b16, bd_ref[0], preferred_element_type=F32)
    ends = _s5_local_scan(x_ref, base, lr, li, seg, False)
    pw_r, pw_i = p_ref[seg - 1:seg, 0:hs], p_ref[seg - 1:seg, hs:2 * hs]
    h_out = _s5_carries(c_ref, e_ref, ends, h_in, pw_r, pw_i, False)
    cr, ci = c_ref[:, 0:hs], c_ref[:, hs:2 * hs]

    def fix(k, carry):
        rows = pl.ds(pl.multiple_of(base + 8 * k, 8), 8)
        pr, pi = p_ref[pl.ds(k, 1), 0:hs], p_ref[pl.ds(k, 1), hs:2 * hs]
        x_ref[rows, 0:hs] += pr * cr - pi * ci
        x_ref[rows, hs:2 * hs] += pr * ci + pi * cr
        return carry

    lax.fori_loop(0, seg, fix, 0, unroll=2)
    return h_out


def _s5_fwd(proj, u_off, bd, cbd, lam, dvec, t_len):
    l = proj.shape[0]
    nj = bd.shape[0]
    nch = l // t_len
    hs = TILE_STATES
    ub = u_off // LANES
    seg = t_len // 8
    assert t_len % 64 == 0

    def body(u_ref, bd_ref, cbd_ref, lam_ref, d_ref, y_ref, hst_ref, x_ref, h_ref, p_ref, c_ref, e_ref, up_ref, yp_ref):
        lr, li = lam_ref[0, 0:1, :], lam_ref[0, 1:2, :]

        @pl.when(pl.program_id(1) == 0)
        def _():
            h_ref[...] = jnp.zeros_like(h_ref)
            _s5_powers(p_ref, lr, li, seg)

        hst_ref[0, 0] = h_ref[...]
        _s5_permute(u_ref, up_ref, t_len)
        h_out = _s5_states(up_ref[...].astype(BF16), bd_ref, x_ref, 0, c_ref, e_ref, p_ref, lr, li,
                           (h_ref[:, 0:hs], h_ref[:, hs:2 * hs]), t_len)
        h_ref[:, 0:hs] = h_out[0]
        h_ref[:, hs:2 * hs] = h_out[1]
        yp_ref[...] = jnp.dot(x_ref[...].astype(BF16), cbd_ref[0], preferred_element_type=F32)
        dv = d_ref[0]

        def out(r0, rows):
            y_ref[r0:r0 + 8, :] = rows + dv * u_ref[r0:r0 + 8, :]

        _s5_unpermute(yp_ref, t_len, out)

    return pl.pallas_call(
        body, grid=(nj, nch),
        in_specs=[pl.BlockSpec((t_len, LANES), lambda j, c: (c, ub + j)),
                  pl.BlockSpec((1, LANES, 2 * hs), lambda j, c: (j, 0, 0)),
                  pl.BlockSpec((1, 2 * hs, LANES), lambda j, c: (j, 0, 0)),
                  pl.BlockSpec((1, 2, hs), lambda j, c: (j, 0, 0)),
                  pl.BlockSpec((1, 1, LANES), lambda j, c: (j, 0, 0))],
        out_specs=[pl.BlockSpec((t_len, LANES), lambda j, c: (c, j)),
                   pl.BlockSpec((1, 1, 1, 2 * hs), lambda j, c: (j, c, 0, 0))],
        out_shape=[jax.ShapeDtypeStruct((l, nj * LANES), F32),
                   jax.ShapeDtypeStruct((nj, nch, 1, 2 * hs), F32)],
        scratch_shapes=[pltpu.VMEM((t_len, 2 * hs), F32), pltpu.VMEM((1, 2 * hs), F32),
                        pltpu.VMEM((seg, 2 * hs), F32), pltpu.VMEM((8, 2 * hs), F32), pltpu.VMEM((8, 2 * hs), F32),
                        pltpu.VMEM((t_len, LANES), F32), pltpu.VMEM((t_len, LANES), F32)],
        name="s5_fwd", compiler_params=_params(2),
    )(proj, bd, cbd, lam, dvec)


def _s5_bwd(proj, u_off, dy, hst, bd, bdt, cbdt, lam, dvec, t_len):
    l = proj.shape[0]
    nj = bd.shape[0]
    nch = l // t_len
    hs = TILE_STATES
    ub = u_off // LANES
    seg = t_len // 8
    tn = (((0,), (0,)), ((), ()))
    assert t_len % 64 == 0

    def body(u_ref, dy_ref, hst_ref, bd_ref, bdt_ref, cbdt_ref, lam_ref, d_ref,
             du_ref, dbd_ref, dcbdt_ref, dlam_ref, dd_ref,
             x_ref, g_ref, gc_ref, p_ref, c_ref, e_ref, up_ref, dyp_ref, dup_ref):
        first = pl.program_id(1) == 0
        lr, li = lam_ref[0, 0:1, :], lam_ref[0, 1:2, :]

        @pl.when(first)
        def _():
            gc_ref[...] = jnp.zeros_like(gc_ref)
            _s5_powers(p_ref, lr, li, seg)

        _s5_permute(u_ref, up_ref, t_len)
        _s5_permute(dy_ref, dyp_ref, t_len)
        ub16, dyb16 = up_ref[...].astype(BF16), dyp_ref[...].astype(BF16)
        h0 = hst_ref[0, 0]
        _s5_states(ub16, bd_ref, x_ref, 8, c_ref, e_ref, p_ref, lr, li, (h0[:, 0:hs], h0[:, hs:2 * hs]), t_len)
        x_ref[0:8, :] = c_ref[...]
        g_ref[...] = jnp.dot(dyb16, cbdt_ref[0], preferred_element_type=F32)
        starts = _s5_local_scan(g_ref, 0, lr, li, seg, True)
        pw_r, pw_i = p_ref[seg - 1:seg, 0:hs], p_ref[seg - 1:seg, hs:2 * hs]
        g_out = _s5_carries(c_ref, e_ref, starts, (gc_ref[:, 0:hs], gc_ref[:, hs:2 * hs]), pw_r, pw_i, True)
        gc_ref[:, 0:hs] = g_out[0]
        gc_ref[:, hs:2 * hs] = g_out[1]
        cr, ci = c_ref[:, 0:hs], c_ref[:, hs:2 * hs]

        def fix(k, carry):
            alr, ali = carry
            rows = pl.ds(pl.multiple_of(8 * k, 8), 8)
            pr, pi = p_ref[pl.ds(seg - 1 - k, 1), 0:hs], p_ref[pl.ds(seg - 1 - k, 1), hs:2 * hs]
            gr = g_ref[rows, 0:hs] + pr * cr + pi * ci
            gi = g_ref[rows, hs:2 * hs] + pr * ci - pi * cr
            g_ref[rows, 0:hs] = gr
            g_ref[rows, hs:2 * hs] = gi
            hpr, hpi = x_ref[rows, 0:hs], x_ref[rows, hs:2 * hs]
            return alr + gr * hpr + gi * hpi, ali + gi * hpr - gr * hpi

        zero = jnp.zeros((8, hs), F32)
        alr, ali = lax.fori_loop(0, seg, fix, (zero, zero), unroll=2)
        alr, ali = _colsum(alr), _colsum(ali)
        g = g_ref[...].astype(BF16)
        h = x_ref[pl.ds(8, t_len), :].astype(BF16)
        dup_ref[...] = jnp.dot(g, bdt_ref[0], preferred_element_type=F32)
        dv = d_ref[0]

        def out(r0, rows):
            du_ref[r0:r0 + 8, :] = (rows + dv * dy_ref[r0:r0 + 8, :]).astype(du_ref.dtype)

        _s5_unpermute(dup_ref, t_len, out)
        sign = jnp.where(lax.broadcasted_iota(jnp.int32, (1, 2 * hs), 1) < hs, 1.0, -1.0)
        dbd = lax.dot_general(ub16, g, tn, preferred_element_type=F32)
        dcbdt = lax.dot_general(dyb16, h, tn, preferred_element_type=F32) * sign
        ddv = _colsum(dy_ref[...] * u_ref[...])

        @pl.when(first)
        def _():
            dbd_ref[0] = dbd
            dcbdt_ref[0] = dcbdt
            dlam_ref[0, 0:1, :] = alr
            dlam_ref[0, 1:2, :] = ali
            dd_ref[0] = ddv

        @pl.when(jnp.logical_not(first))
        def _():
            dbd_ref[0] += dbd
            dcbdt_ref[0] += dcbdt
            dlam_ref[0, 0:1, :] += alr
            dlam_ref[0, 1:2, :] += ali
            dd_ref[0] += ddv

    rev = lambda c: nch - 1 - c
    wide = pl.BlockSpec((1, LANES, 2 * hs), lambda j, c: (j, 0, 0))
    tall = pl.BlockSpec((1, 2 * hs, LANES), lambda j, c: (j, 0, 0))
    return pl.pallas_call(
        body, grid=(nj, nch),
        in_specs=[pl.BlockSpec((t_len, LANES), lambda j, c: (rev(c), ub + j)),
                  pl.BlockSpec((t_len, LANES), lambda j, c: (rev(c), j)),
                  pl.BlockSpec((1, 1, 1, 2 * hs), lambda j, c: (j, rev(c), 0, 0)),
                  wide, tall, wide,
                  pl.BlockSpec((1, 2, hs), lambda j, c: (j, 0, 0)),
                  pl.BlockSpec((1, 1, LANES), lambda j, c: (j, 0, 0))],
        out_specs=[pl.BlockSpec((t_len, LANES), lambda j, c: (rev(c), j)),
                   wide, wide,
                   pl.BlockSpec((1, 2, hs), lambda j, c: (j, 0, 0)),
                   pl.BlockSpec((1, 1, LANES), lambda j, c: (j, 0, 0))],
        out_shape=[jax.ShapeDtypeStruct((l, nj * LANES), BF16),
                   jax.ShapeDtypeStruct((nj, LANES, 2 * hs), F32),
                   jax.ShapeDtypeStruct((nj, LANES, 2 * hs), F32),
                   jax.ShapeDtypeStruct((nj, 2, hs), F32),
                   jax.ShapeDtypeStruct((nj, 1, LANES), F32)],
        scratch_shapes=[pltpu.VMEM((t_len + 8, 2 * hs), F32), pltpu.VMEM((t_len, 2 * hs), F32),
                        pltpu.VMEM((1, 2 * hs), F32), pltpu.VMEM((seg, 2 * hs), F32),
                        pltpu.VMEM((8, 2 * hs), F32), pltpu.VMEM((8, 2 * hs), F32),
                        pltpu.VMEM((t_len, LANES), F32), pltpu.VMEM((t_len, LANES), F32),
                        pltpu.VMEM((t_len, LANES), F32)],
        name="s5_bwd", compiler_params=_params(2),
    )(proj, dy, hst, bd, bdt, cbdt, lam, dvec)


def _full_spec(shape):
    nd = len(shape)
    return pl.BlockSpec(tuple(shape), lambda i: (0,) * nd)


def _sds(shape, dtype=F32):
    return jax.ShapeDtypeStruct(tuple(shape), dtype)


def kernel(x, c, ada_w, ada_b, norm_mix_g, w_in, attn_sinks, w_attn_proj, ssm_a_re, ssm_a_im, ssm_log_dt, ssm_b_re, ssm_b_im, ssm_c_re, ssm_c_im, ssm_d, w_ssm_glu, w_out, norm_ffn_g, w_ffn_up, ffn_conv_w, ffn_conv_b, w_ffn_down, final_g, loss_target, m_ada_w, m_ada_b, m_norm_mix_g, m_w_in, m_attn_sinks, m_w_attn_proj, m_ssm_a_re, m_ssm_a_im, m_ssm_log_dt, m_ssm_b_re, m_ssm_b_im, m_ssm_c_re, m_ssm_c_im, m_ssm_d, m_w_ssm_glu, m_w_out, m_norm_ffn_g, m_w_ffn_up, m_ffn_conv_w, m_ffn_conv_b, m_w_ffn_down, m_final_g, v_ada_w, v_ada_b, v_norm_mix_g, v_w_in, v_attn_sinks, v_w_attn_proj, v_ssm_a_re, v_ssm_a_im, v_ssm_log_dt, v_ssm_b_re, v_ssm_b_im, v_ssm_c_re, v_ssm_c_im, v_ssm_d, v_w_ssm_glu, v_w_out, v_norm_ffn_g, v_w_ffn_up, v_ffn_conv_w, v_ffn_conv_b, v_w_ffn_down, v_final_g):
    given = dict(locals())
    names = ['ada_w', 'ada_b', 'norm_mix_g', 'w_in', 'attn_sinks', 'w_attn_proj', 'ssm_a_re', 'ssm_a_im',
             'ssm_log_dt', 'ssm_b_re', 'ssm_b_im', 'ssm_c_re', 'ssm_c_im', 'ssm_d', 'w_ssm_glu', 'w_out',
             'norm_ffn_g', 'w_ffn_up', 'ffn_conv_w', 'ffn_conv_b', 'w_ffn_down', 'final_g']

    xs = x[0]
    tgt = loss_target[0]
    l, d = xs.shape
    attn_w = w_attn_proj.shape[1]
    ssm_w = w_ssm_glu.shape[1]
    hq = attn_sinks.shape[1]
    qpk = hq // N_KV_HEADS
    kv_w = N_KV_HEADS * HEAD_DIM
    n_groups = ssm_a_re.shape[1]
    dff = ffn_conv_b.shape[1]
    in_w = attn_w + 2 * kv_w + ssm_w + 2 * d
    nj = ssm_w // LANES
    off_k, off_v, off_u = attn_w, attn_w + kv_w, attn_w + 2 * kv_w
    off_ga, off_gs = off_u + ssm_w, off_u + ssm_w + d
    assert hq * HEAD_DIM == attn_w and n_groups * SSM_P == ssm_w and l % ATT_BLOCK == 0

    xi, yi, ci = _dev()
    idx = 4 * xi + 2 * yi + ci

    row_sharded = {'w_out': (d, d), 'w_ffn_down': (dff, d)}
    big = ['w_in', 'w_attn_proj', 'w_ssm_glu', 'w_out', 'w_ffn_up', 'w_ffn_down']
    spack, s_offs = _pack([c, ffn_conv_w[0]], LANES, 8)
    w16 = {k: given[k][0].astype(BF16) for k in big}
    half = d // 2
    wg_lo, wg_hi, sg = _all_gather("gather_first", [w16['w_in'][:half], w16['w_in'][half:], spack])
    mixer_w = ['w_attn_proj', 'w_ssm_glu', 'w_out']
    h_mixer, tok = _exchange_start("gather_mixer_start", [w16[k] for k in mixer_w], True, wg_hi)
    h_up, tok = _exchange_start("gather_ffn_up_start", [w16['w_ffn_up']], True, tok)
    h_down, tok = _exchange_start("gather_ffn_down_start", [w16['w_ffn_down']], True, tok)
    full = {'w_in': jnp.concatenate([z.transpose(1, 0, 2).reshape(half, in_w) for z in (wg_lo, wg_hi)], axis=0)}
    c_all = _unpack(sg, s_offs[0], (d,), lead=(N_DEV,))
    conv_w = _unpack(sg, s_offs[1], ffn_conv_w.shape[1:], lead=(N_DEV,)).transpose(1, 0, 2).reshape(3, dff)
    conv_b = ffn_conv_b

    mod_n = ada_w.shape[2]
    tcm = _pick(mod_n, 512)
    ada_b_mine = lax.dynamic_slice_in_dim(ada_b, idx * mod_n, mod_n, axis=1)

    def modpart_fn(cv, wv, bv):
        cond = cv * jax.nn.sigmoid(cv)
        return jnp.dot(cond.astype(BF16), wv.astype(BF16), preferred_element_type=F32) + bv, cond

    modp, cond_all = _tile_call(
        "ada_rows", modpart_fn, (mod_n // tcm,), [c_all, ada_w[0], ada_b_mine],
        [pl.BlockSpec((N_DEV, d), lambda j: (0, 0)), pl.BlockSpec((d, tcm), lambda j: (0, j)),
         pl.BlockSpec((1, tcm), lambda j: (0, j))],
        [_sds((N_DEV, mod_n)), _sds((N_DEV, d))],
        [pl.BlockSpec((N_DEV, tcm), lambda j: (0, j)), pl.BlockSpec((N_DEV, d), lambda j: (0, 0))])
    (modg,) = _all_gather("gather_ada_rows", [modp])
    mod = lax.dynamic_index_in_dim(modg, idx, axis=1, keepdims=False).reshape(1, N_DEV * mod_n)
    sh1, sc1, g1, sh2, sc2, g2 = [mod[:, i * d:(i + 1) * d] for i in range(6)]

    tr = _pick(l, 256, 8)
    trh = _pick(l, 128, 8)
    nr, nrh = l // tr, l // trh
    g_mix, g_ffn, g_fin = norm_mix_g + tok[0:1, 0:1], norm_ffn_g, final_g.reshape(1, d)

    def with_t(fn):
        def wrapped(*vals):
            out = fn(*vals)
            out = out if isinstance(out, tuple) else (out,)
            return out + (out[-1].T,)
        return wrapped

    h1, h1_t = _tile_call("norm_mod_mix", with_t(_norm_mod), (1, nr), [xs, g_mix, sc1, sh1],
                          [_t(tr, d), _v(d), _v(d), _v(d)], [_sds((l, d), BF16), _sds((d, l), BF16)],
                          [_t(tr, d), _tt(tr, d)])
    proj = _matmul("proj_in", h1, full['w_in'], "nn", tn=1280)

    def heads(z, n):
        return z.reshape(l, n, HEAD_DIM).transpose(1, 0, 2)

    qh = heads(proj[:, :attn_w], hq)
    kh = heads(proj[:, off_k:off_k + kv_w], N_KV_HEADS)
    vh = heads(proj[:, off_v:off_v + kv_w], N_KV_HEADS)
    sinks3 = jnp.repeat(attn_sinks.reshape(N_KV_HEADS, qpk), ATT_BLOCK, axis=1)[..., None]
    o_h = _attn_fwd(qh, kh, vh, sinks3)
    o2 = o_h.transpose(1, 0, 2).reshape(l, attn_w)

    gn = (n_groups, SSM_N)
    pgn = (SSM_P, n_groups, SSM_N)
    a_re, a_im, log_dt = ssm_a_re[0], ssm_a_im[0], ssm_log_dt[0].reshape(n_groups, 1)
    b_re, b_im = ssm_b_re[0].transpose(2, 0, 1), ssm_b_im[0].transpose(2, 0, 1)
    disc_ins = [a_re, a_im, log_dt, b_re, b_im]
    disc_specs = [_full_spec(gn), _full_spec(gn), _full_spec((n_groups, 1)), _full_spec(pgn), _full_spec(pgn)]
    lam_r, lam_i, bb_r, bb_i = _tile_call(
        "s5_discretise", _s5_disc_fn, (1,), disc_ins, disc_specs,
        [_sds(gn), _sds(gn), _sds(pgn), _sds(pgn)],
        [_full_spec(gn), _full_spec(gn), _full_spec(pgn), _full_spec(pgn)])

    def tiles_gpn(z):
        return z.reshape(SSM_P, nj, TILE_GROUPS, SSM_N).transpose(1, 2, 0, 3)

    bd = jnp.concatenate([_block_diag(tiles_gpn(bb_r)), _block_diag(tiles_gpn(bb_i))], axis=2).astype(BF16)
    c_r = ssm_c_re[0].reshape(nj, TILE_GROUPS, SSM_P, SSM_N).transpose(0, 1, 3, 2)
    c_i = (-ssm_c_im[0]).reshape(nj, TILE_GROUPS, SSM_P, SSM_N).transpose(0, 1, 3, 2)
    cbd = jnp.concatenate([_block_diag(c_r), _block_diag(c_i)], axis=1).astype(BF16)
    bdt, cbdt = bd.transpose(0, 2, 1), cbd.transpose(0, 2, 1)
    lam = jnp.stack([lam_r.reshape(nj, TILE_STATES), lam_i.reshape(nj, TILE_STATES)], axis=1)
    dvec = ssm_d[0].reshape(nj, 1, LANES)
    t_len = _pick(l, 512, 8)
    y, hst = _s5_fwd(proj, off_u, bd, cbd, lam, dvec, t_len)

    tcs, trg = _pick(ssm_w, 1024), _pick(l, 512, 8)
    gy = _tile_call("gelu", lambda v: jax.nn.gelu(v), (ssm_w // tcs, l // trg), [y], [_t(trg, tcs)],
                    [_sds((l, ssm_w), BF16)], [_t(trg, tcs)])[0]
    full.update(zip(mixer_w, _exchange_wait("gather_mixer_wait", h_mixer, gy)))
    full['w_out'] = full['w_out'].reshape(row_sharded['w_out'])
    full['w_attn_proj'] = full['w_attn_proj'].transpose(1, 0, 2).reshape(attn_w, d)
    full['w_ssm_glu'] = full['w_ssm_glu'].transpose(1, 0, 2).reshape(ssm_w, 2 * d)
    glu = _matmul("ssm_glu", gy, full['w_ssm_glu'], "nn")

    tcd = 256 if d % 256 == 0 and off_ga % 256 == 0 else LANES
    assert d % tcd == 0 and off_ga % tcd == 0 and off_gs % tcd == 0
    gate_ins = [(glu, 0), (glu, d), (proj, off_ga), (proj, off_gs)]

    def mix_epilogue(at, ga_, gb_, pa, ps):
        return at, _mix_fn(ga_, gb_, at, pa, ps)

    attn, mixed = _matmul("attn_proj_gate_mix", o2, full['w_attn_proj'], "nn", tn=tcd,
                          epilogue=(mix_epilogue, gate_ins, [(F32, False), (BF16, False)]))
    def res_norm_fn(xv, mo, g1v, gv, scv, shv):
        x2v = xv + g1v * mo
        return x2v, _norm_mod(x2v, gv, scv, shv)

    def res_norm_epilogue(mo, xv, g1v, gv, scv, shv):
        x2v, h2v = res_norm_fn(xv, mo, g1v, gv, scv, shv)
        return mo, x2v, h2v, h2v.T

    mixout, x2, h2, h2_t = _matmul(
        "mix_out_residual_norm_mod_ffn", mixed, full['w_out'], "nn", tm=256, tn=d,
        epilogue=(res_norm_epilogue, [xs, g1, g_ffn, sc2, sh2], [(F32, False), (F32, False), (BF16, False), (BF16, True)]))
    full['w_ffn_up'], = _exchange_wait("gather_ffn_up_wait", h_up, h2)
    up = _matmul("ffn_up", h2, full['w_ffn_up'], "nn", out_dtype=BF16, tn=1408)

    tcf, trc = _pick(dff, 1408), _pick(l, 512, 8)
    assert dff % tcf == 0
    ncf = dff // tcf

    taps = [conv_w[i:i + 1] for i in range(3)]

    def conv_gate(gp, gp_prev, w0, w1, w2, bv):
        gp = gp.astype(F32)
        prev = jnp.where(pl.program_id(1) == 0, 0.0, 1.0) * gp_prev.astype(F32)
        ext = jnp.concatenate([prev, gp], axis=0)
        m1 = pltpu.roll(ext, 1, 0)[HALO:]
        m2 = pltpu.roll(ext, 2, 0)[HALO:]
        return w0 * m2 + w1 * m1 + w2 * gp + bv, m1, m2

    def convglu_fn(gp, gp_prev, val, w0, w1, w2, bv):
        gate, _, _ = conv_gate(gp, gp_prev, w0, w1, w2, bv)
        return gate * jax.nn.sigmoid(gate) * val.astype(F32)

    act, act_t = _tile_call("conv_swiglu", with_t(convglu_fn), (ncf, l // trc), [up, up, up] + taps + [conv_b],
                            [_t(trc, tcf), _prev_rows(trc, tcf), _t(trc, tcf, dff)] + [_v(tcf)] * 4,
                            [_sds((l, dff), BF16), _sds((dff, l), BF16)], [_t(trc, tcf), _tt(trc, tcf)])
    full['w_ffn_down'] = _exchange_wait("gather_ffn_down_wait", h_down, act)[0].reshape(row_sharded['w_ffn_down'])
    ffn = _matmul("ffn_down", act, full['w_ffn_down'], "nn", tm=512)

    def final_fn(x2v, fv, g2v, gv, tv):
        rows = x2v.shape[0]

        def loss_of(x2a, fa, g2a, ga):
            out = _rms(x2a + g2a * fa, ga)
            err = out - tv
            return 0.5 * _colsum(jnp.mean(err * err, axis=-1, keepdims=True))

        loss, vjp = jax.vjp(loss_of, x2v, fv, _bc(g2v, rows), _bc(gv, rows))
        dx3, dffn, dg2, dgf = vjp(jnp.ones((1, 1), F32))
        return jnp.broadcast_to(loss, (1, LANES)), dx3, dffn, _colsum(dg2), _colsum(dgf)

    loss_p, dx3, dffn, dg2, dg_fin = _tile_call(
        "loss_final_norm", final_fn, (1, nrh), [x2, ffn, g2, g_fin, tgt],
        [_t(trh, d), _t(trh, d), _v(d), _v(d), _t(trh, d)],
        [_sds((1, LANES)), _sds((l, d)), _sds((l, d), BF16), _sds((1, d)), _sds((1, d))],
        [_v(LANES), _t(trh, d), _t(trh, d), _v(d), _v(d)], acc=(0, 3, 4))
    loss = lax.psum(loss_p[0, 0], ("x", "y", "c"))

    dact = _matmul("d_act", dffn, full['w_ffn_down'], "nt", out_dtype=BF16, tn=1408, dep=loss.reshape(1, 1))
    gd, gd16, pending = {}, {}, []
    dw_down, dw_down16 = _matmul("dw_ffn_down", act_t, dffn, "nn", tm=512, also_bf16=True)
    gd['w_ffn_down'], gd16['w_ffn_down'] = [z.reshape((N_DEV,) + w_ffn_down.shape[1:]) for z in (dw_down, dw_down16)]
    handle, tok = _exchange_start("grad_ffn_down_start", [gd16['w_ffn_down']], False, loss.reshape(1, 1))
    pending.append((['w_ffn_down'], handle))
    conv_b_bwd = conv_b + tok[0:1, 0:1]

    def convglu_bwd_fn(gp, gp_prev, gp_next, val, val_next, da, da_next, w0, w1, w2, bv):
        rows = gp.shape[0]
        i = pl.program_id(1)
        gp, val, da = gp.astype(F32), val.astype(F32), da.astype(F32)
        prev = jnp.where(i == 0, 0.0, 1.0) * gp_prev.astype(F32)
        more = jnp.where(i == pl.num_programs(1) - 1, 0.0, 1.0)
        ext = jnp.concatenate([prev, gp, gp_next.astype(F32)], axis=0)
        cur = ext[HALO:]
        m1 = pltpu.roll(ext, 1, 0)[HALO:]
        m2 = pltpu.roll(ext, 2, 0)[HALO:]
        gate = w0 * m2 + w1 * m1 + w2 * cur + bv
        sg = jax.nn.sigmoid(gate)
        val_e = jnp.concatenate([val, val_next.astype(F32)], axis=0)
        da_e = jnp.concatenate([da, more * da_next.astype(F32)], axis=0)
        dgate = da_e * val_e * (sg * (1.0 + gate * (1.0 - sg)))
        p1 = pltpu.roll(dgate, rows + HALO - 1, 0)[:rows]
        p2 = pltpu.roll(dgate, rows + HALO - 2, 0)[:rows]
        dg = dgate[:rows]
        dgp = w2 * dg + w1 * p1 + w0 * p2
        dval = da * (gate[:rows] * sg[:rows])
        return (jnp.stack([dgp, dval], axis=0), _colsum(dg), _colsum(dg * m2[:rows]), _colsum(dg * m1[:rows]),
                _colsum(dg * gp))

    dup, dconv_b, dcw0, dcw1, dcw2 = _tile_call(
        "conv_swiglu_bwd", convglu_bwd_fn, (ncf, nr), [up, up, up, up, up, dact, dact] + taps + [conv_b_bwd],
        [_t(tr, tcf), _prev_rows(tr, tcf), _next_rows(tr, tcf, l), _t(tr, tcf, dff), _next_rows(tr, tcf, l, dff),
         _t(tr, tcf), _next_rows(tr, tcf, l)] + [_v(tcf)] * 4,
        [_sds((2, l, dff), BF16)] + [_sds((1, dff))] * 4, [_st(tr, tcf)] + [_v(tcf)] * 4, acc=(1, 2, 3, 4))
    dh2 = _matmul("d_h2", dup, full['w_ffn_up'], "nt", tm=512, fold=4)
    gd['w_ffn_up'], gd16['w_ffn_up'] = _matmul("dw_ffn_up", h2_t, dup, "nn", tm=512, tn=1408, out_stack=N_DEV, also_bf16=True)
    handle, tok = _exchange_start("grad_ffn_up_start", [gd16['w_ffn_up']], False, gd['w_ffn_up'])
    pending.append((['w_ffn_up'], handle))
    g_ffn_bwd = g_ffn + tok[0:1, 0:1]

    def res_norm_bwd_fn(xv, mo, g1v, gv, scv, shv, dhv, dxv):
        rows = xv.shape[0]
        _, vjp = jax.vjp(res_norm_fn, xv, mo, _bc(g1v, rows), _bc(gv, rows), _bc(scv, rows), _bc(shv, rows))
        dx, dmo, dg1v, dgv, dscv, dshv = vjp((dxv, dhv))
        return dx, dmo, _colsum(dg1v), _colsum(dgv), _colsum(dscv), _colsum(dshv)

    dx2, dmixout, dg1, dg_ffn, dsc2, dsh2 = _tile_call(
        "residual_norm_mod_ffn_bwd", res_norm_bwd_fn, (1, nrh), [xs, mixout, g1, g_ffn_bwd, sc2, sh2, dh2, dx3],
        [_t(trh, d), _t(trh, d), _v(d), _v(d), _v(d), _v(d), _t(trh, d), _t(trh, d)],
        [_sds((l, d)), _sds((l, d), BF16)] + [_sds((1, d))] * 4,
        [_t(trh, d), _t(trh, d)] + [_v(d)] * 4, acc=(2, 3, 4, 5))

    def mix_bwd_epilogue(dm, ga_, gb_, pa, ps, at):
        _, vjp = jax.vjp(_mix_fn, ga_, gb_, at, pa, ps)
        da, db, dat, dpa, dps = vjp(dm)
        return jnp.stack([da, db], axis=0), dat, dpa, dps

    dglu, dattn, dga, dgs = _matmul(
        "d_mixed_gate_mix_bwd", dmixout, full['w_out'], "nt", tn=tcd,
        epilogue=(mix_bwd_epilogue, gate_ins + [attn], [(BF16, 'pair')] + [(BF16, False)] * 3))
    dw_out, dw_out16 = _matmul("dw_out", mixed, dmixout, "tn", also_bf16=True)
    gd['w_out'], gd16['w_out'] = [z.reshape((N_DEV,) + w_out.shape[1:]) for z in (dw_out, dw_out16)]

    def gelu_bwd_epilogue(dgy, yv):
        _, vjp = jax.vjp(lambda z: jax.nn.gelu(z), yv)
        return (vjp(dgy)[0],)

    dy, = _matmul("d_gelu_y_gelu_bwd", dglu, full['w_ssm_glu'], "nt", epilogue=(gelu_bwd_epilogue, [y], [(F32, False)]))
    gd['w_ssm_glu'], gd16['w_ssm_glu'] = _matmul("dw_ssm_glu", gy, dglu, "tn", out_stack=N_DEV, also_bf16=True)
    du, dbd, dcbdt, dlam, dd_tiles = _s5_bwd(proj, off_u, dy, hst, bd, bdt, cbdt, lam, dvec, t_len)

    def gpn_of(z):
        return z.transpose(2, 0, 1, 3).reshape(pgn)

    dbb_r = gpn_of(_diag_blocks(dbd[:, :, :TILE_STATES], SSM_P))
    dbb_i = gpn_of(_diag_blocks(dbd[:, :, TILE_STATES:], SSM_P))
    dc_re = _diag_blocks(dcbdt[:, :, :TILE_STATES], SSM_P).reshape(n_groups, SSM_P, SSM_N)
    dc_im = _diag_blocks(dcbdt[:, :, TILE_STATES:], SSM_P).reshape(n_groups, SSM_P, SSM_N)
    dlam_r, dlam_i = dlam[:, 0].reshape(gn), dlam[:, 1].reshape(gn)

    def disc_bwd_fn(ar, ai, ld, br, bi, dlr, dli, dbr, dbi):
        _, vjp = jax.vjp(_s5_disc_fn, ar, ai, ld, br, bi)
        return vjp((dlr, dli, dbr, dbi))

    da_re, da_im, dlog_dt, db_re, db_im = _tile_call(
        "s5_discretise_bwd", disc_bwd_fn, (1,), disc_ins + [dlam_r, dlam_i, dbb_r, dbb_i],
        disc_specs + [_full_spec(gn), _full_spec(gn), _full_spec(pgn), _full_spec(pgn)],
        [_sds(gn), _sds(gn), _sds((n_groups, 1)), _sds(pgn), _sds(pgn)], disc_specs)

    do2 = _matmul("d_attn_heads", dattn, full['w_attn_proj'], "nt")
    gd['w_attn_proj'], gd16['w_attn_proj'] = _matmul("dw_attn_proj", o2, dattn, "tn", out_stack=N_DEV, also_bf16=True)
    handle, tok = _exchange_start("grad_mixer_start", [gd16[k] for k in mixer_w], False, gd['w_attn_proj'])
    pending.append((mixer_w, handle))
    do_h = heads(do2.astype(BF16), hq)
    dq_h, dk_h, dv_h, dsink = _attn_bwd(qh, kh, vh, sinks3 + tok[0:1, 0:1], do_h)

    def unheads(z):
        return z.transpose(1, 0, 2).reshape(l, z.shape[0] * HEAD_DIM)

    early = ['attn_sinks', 'ssm_a_re', 'ssm_a_im', 'ssm_log_dt', 'ssm_b_re', 'ssm_b_im', 'ssm_c_re', 'ssm_c_im',
             'ssm_d', 'norm_ffn_g', 'ffn_conv_b', 'final_g']
    early_grads = {
        'attn_sinks': dsink[:, 0, 0], 'ssm_a_re': da_re, 'ssm_a_im': da_im, 'ssm_log_dt': dlog_dt,
        'ssm_b_re': db_re.transpose(1, 2, 0), 'ssm_b_im': db_im.transpose(1, 2, 0), 'ssm_c_re': dc_re,
        'ssm_c_im': dc_im, 'ssm_d': dd_tiles, 'norm_ffn_g': dg_ffn, 'ffn_conv_b': dconv_b, 'final_g': dg_fin}
    ge_pack, e_offs = _pack([jnp.concatenate([dg1, dsh2, dsc2, dg2], axis=1)] + [early_grads[k] for k in early],
                            LANES, 8)
    h_early, tok = _exchange_start("gather_small_early_start", [ge_pack], True, dsink)

    dproj = jnp.concatenate([unheads(dq_h), unheads(dk_h), unheads(dv_h), du, dga, dgs], axis=1)
    dw_in, dw_in16 = _matmul("dw_in", h1_t, dproj, "nn", tm=512, tn=1280, also_bf16=True, dep=tok)
    dcw = jnp.concatenate([dcw0, dcw1, dcw2], axis=0)
    shard_in, shard_cw = w_in.shape[1:], ffn_conv_w.shape[1:]
    gd16['w_in'] = dw_in16.reshape(shard_in[0], N_DEV, shard_in[1]).transpose(1, 0, 2)
    own_in = lax.dynamic_slice_in_dim(dw_in, idx * shard_in[1], shard_in[1], axis=1)[None]
    gd['ffn_conv_w'] = dcw.reshape(shard_cw[0], N_DEV, shard_cw[1]).transpose(1, 0, 2)
    gd16['ffn_conv_w'] = gd['ffn_conv_w'].astype(BF16)
    handle, tok = _exchange_start("grad_in_start", [gd16['w_in'], gd16['ffn_conv_w']], False, dw_in)
    pending.append((['w_in', 'ffn_conv_w'], handle))
    dh1 = _matmul("d_h1", dproj, full['w_in'], "nt", tm=512, dep=tok)

    def norm_bwd_fn(xv, gv, scv, shv, dhv, dxv):
        rows = xv.shape[0]
        _, vjp = jax.vjp(_norm_mod, xv, _bc(gv, rows), _bc(scv, rows), _bc(shv, rows))
        dx, dgv, dscv, dshv = vjp(dhv)
        return dx + dxv, _colsum(dgv), _colsum(dscv), _colsum(dshv)

    grad_x, dg_mix, dsc1, dsh1 = _tile_call(
        "norm_mod_mix_bwd", norm_bwd_fn, (1, nrh), [xs, g_mix, sc1, sh1, dh1, dx2],
        [_t(trh, d), _v(d), _v(d), _v(d), _t(trh, d), _t(trh, d)],
        [_sds((l, d))] + [_sds((1, d))] * 3, [_t(trh, d)] + [_v(d)] * 3, acc=(1, 2, 3))

    gl_pack, l_offs = _pack([jnp.concatenate([dsh1, dsc1], axis=1), dg_mix], LANES, 8)
    h_late, tok = _exchange_start("gather_small_late_start", [gl_pack], True, grad_x)

    sharded = big + ['ffn_conv_w']
    sharded_out = {}

    def finish(group, handle, after):
        for k, parts in zip(group, _exchange_wait("grad_" + group[0] + "_wait", handle, after)):
            own_src, own_at = (own_in, 0 * idx) if k == 'w_in' else (gd[k], idx)
            sharded_out[k] = _adamw_sharded("adamw_" + k, parts, own_src, given[k][0], given['m_' + k][0],
                                            given['v_' + k][0], jnp.stack([idx, own_at]).astype(jnp.int32))

    for group, handle in pending[:-1]:
        finish(group, handle, tok)
    done = functools.reduce(lambda p, q: p + q, [sharded_out[k][1][0:1, 0:1] for g_, _ in pending[:-1] for k in g_])
    finish(*pending[-1], done)

    ge_all, = _exchange_wait("gather_small_early_wait", h_early, done)
    gl_all, = _exchange_wait("gather_small_late_wait", h_late, sharded_out['w_in'][1])
    gs_all = jnp.concatenate([ge_all, gl_all], axis=1)
    rows_e = ge_pack.shape[0]

    def small_pack(prefix):
        ab = given[prefix + 'ada_b']
        p_early, _ = _pack([ab[:, 2 * d:]] + [given[prefix + k] for k in early], LANES, 8)
        p_late, _ = _pack([ab[:, :2 * d], given[prefix + 'norm_mix_g']], LANES, 8)
        return jnp.concatenate([p_early, p_late], axis=0)

    small_out = _adamw("adamw_replicated", gs_all, small_pack(''), small_pack('m_'), small_pack('v_'))

    dmod_all = jnp.concatenate([_unpack(gl_all, l_offs[0], (2 * d,), lead=(N_DEV,)),
                                _unpack(ge_all, e_offs[0], (4 * d,), lead=(N_DEV,))], axis=1)
    dmod_mine = lax.dynamic_slice_in_dim(dmod_all, idx * mod_n, mod_n, axis=1)
    kpad = LANES - N_DEV
    cond_t = jnp.pad(cond_all.T, ((0, 0), (0, kpad)))
    dmod_pad = jnp.pad(dmod_mine, ((0, kpad), (0, 0)))
    g_ada_w = _matmul("dw_ada", cond_t, dmod_pad, "nn")
    ada_out = _adamw("adamw_ada_w", g_ada_w[None], ada_w[0], m_ada_w[0], v_ada_w[0])

    results = [{}, {}, {}, {}]
    for which in range(4):
        out = small_out[which]
        results[which]['ada_b'] = jnp.concatenate([_unpack(out, rows_e + l_offs[0], (1, 2 * d)),
                                                   _unpack(out, e_offs[0], (1, 4 * d))], axis=1)
        results[which]['norm_mix_g'] = _unpack(out, rows_e + l_offs[1], norm_mix_g.shape)
        for k, off in zip(early, e_offs[1:]):
            results[which][k] = _unpack(out, off, given[k].shape)
        for k in sharded:
            results[which][k] = sharded_out[k][which][None]
        results[which]['ada_w'] = ada_out[which][None]
    outs = [loss, grad_x[None]]
    for which in range(4):
        outs += [results[which][k] for k in names]
    return tuple(outs)
```

```python
import functools
import math

import jax
import jax.numpy as jnp
from jax import lax
from jax.experimental import pallas as pl
from jax.experimental.pallas import tpu as pltpu

F32, BF16 = jnp.float32, jnp.bfloat16
MESH = pl.DeviceIdType.MESH
N_DEV = 8

HEAD_DIM = 64
N_KV_HEADS = 2
ATT_BLOCK = 128
NEG_INF = -1e30
SSM_P = 16
SSM_N = 64
LANES = 128
TILE_GROUPS = LANES // SSM_P
TILE_STATES = TILE_GROUPS * SSM_N
RMS_EPS = 1e-6
ADAM_LR, ADAM_B1, ADAM_B2, ADAM_EPS, ADAM_WD, ADAM_STEP = 0.001, 0.9, 0.999, 1e-08, 0.01, 10
VMEM_LIMIT = 56 * 1024 * 1024
MATMUL_VMEM_BUDGET = 44 * 1024 * 1024


def _params(n_axes):
    return pltpu.CompilerParams(dimension_semantics=("arbitrary",) * n_axes, vmem_limit_bytes=VMEM_LIMIT)


def _pick(dim, pref, align=128):
    if dim <= align:
        return dim
    t = (min(pref, dim) // align) * align
    while t > align and dim % t:
        t -= align
    assert dim % t == 0, (dim, pref, align)
    return t


def _dev():
    return lax.axis_index("x"), lax.axis_index("y"), lax.axis_index("c")


def _tile_call(name, fn, grid, ins, in_specs, out_shapes, out_specs, acc=()):
    n_in, n_out = len(ins), len(out_shapes)
    acc_axis = len(grid) - 1

    def body(*refs):
        vals = fn(*[r[...] for r in refs[:n_in]])
        if not isinstance(vals, (tuple, list)):
            vals = (vals,)
        assert len(vals) == n_out
        for i, (r, v) in enumerate(zip(refs[n_in:], vals)):
            v = v.astype(r.dtype)
            if i in acc:
                first = pl.program_id(acc_axis) == 0

                @pl.when(first)
                def _():
                    r[...] = v

                @pl.when(jnp.logical_not(first))
                def _():
                    r[...] += v
            else:
                r[...] = v

    return pl.pallas_call(
        body, grid=grid, in_specs=in_specs, out_specs=out_specs, out_shape=out_shapes, name=name,
        compiler_params=_params(len(grid)),
    )(*ins)


def _t(tr, tc, off=0):
    return pl.BlockSpec((tr, tc), lambda j, i: (i, j + off // tc))


def _tt(tr, tc):
    return pl.BlockSpec((tc, tr), lambda j, i: (j, i))


def _v(tc, off=0, rows=1):
    return pl.BlockSpec((rows, tc), lambda j, i: (0, j + off // tc))


HALO = 16


def _prev_rows(tr, tc, off=0):
    return pl.BlockSpec((HALO, tc), lambda j, i: (jnp.maximum(i * (tr // HALO) - 1, 0), j + off // tc))


def _next_rows(tr, tc, nrows, off=0):
    return pl.BlockSpec((HALO, tc),
                        lambda j, i: (jnp.minimum((i + 1) * (tr // HALO), nrows // HALO - 1), j + off // tc))


def _st(tr, tc):
    return pl.BlockSpec((2, tr, tc), lambda j, i: (0, i, j))


def _bc(v, rows):
    return jnp.broadcast_to(v, (rows, v.shape[-1]))


def _colsum(v):
    return jnp.sum(v, axis=0, keepdims=True)


def _matmul(name, a, b, mode, out_dtype=F32, tm=1024, tn=1024, tk=None, out_stack=None, also_bf16=False, dep=None,
            fold=1, epilogue=None):
    def dims(z):
        return (z.shape[-2], z.shape[-1] * (z.shape[0] if z.ndim == 3 else 1))

    ar, ac = dims(a)
    br, bc = dims(b)
    if mode == "nn":
        m, k, n = ar, ac, bc
        assert br == k
    elif mode == "nt":
        m, k, n = ar, ac, br
        assert bc == k
    else:
        m, k, n = ac, ar, bc
        assert br == k
    m_lim, k_lim, n_lim = [m], [k], [n]
    if a.ndim == 3:
        (m_lim if mode == "tn" else k_lim).append(a.shape[-1])
    if b.ndim == 3:
        (k_lim if mode == "nt" else n_lim).append(b.shape[-1])
    if out_stack:
        n_lim.append(n // out_stack)
    tm = _pick(functools.reduce(math.gcd, m_lim), tm)
    tn = _pick(functools.reduce(math.gcd, n_lim), tn)
    k_unit = functools.reduce(math.gcd, k_lim)
    if tk is None:
        sa, sb, so = a.dtype.itemsize, b.dtype.itemsize, jnp.dtype(out_dtype).itemsize + (2 if also_bf16 else 0)
        fits = [t for t in range(LANES, k_unit + 1, LANES) if k_unit % t == 0 and
                2 * t * (tm * sa + tn * sb) + tm * tn * (2 * so + (4 if t < k else 0)) <= MATMUL_VMEM_BUDGET]
        tk = max(fits) if fits else _pick(k_unit, 512)
    else:
        tk = _pick(k_unit, tk)
    assert (k // tk) % fold == 0
    nk = k // (tk * fold)

    def spec(z, brows, bcols, ridx, cidx):
        if z.ndim == 3:
            per = z.shape[-1] // bcols
            return pl.BlockSpec((None, brows, bcols),
                                lambda i, j, kk: (cidx(i, j, kk) // per, ridx(i, j, kk), cidx(i, j, kk) % per))
        return pl.BlockSpec((brows, bcols), lambda i, j, kk: (ridx(i, j, kk), cidx(i, j, kk)))

    gi = lambda i, j, kk: i
    gj = lambda i, j, kk: j
    a_specs, b_specs = [], []
    for f in range(fold):
        gk = lambda i, j, kk, f=f: fold * kk + f
        if mode == "nn":
            a_specs.append(spec(a, tm, tk, gi, gk))
            b_specs.append(spec(b, tk, tn, gk, gj))
            dn = (((1,), (0,)), ((), ()))
        elif mode == "nt":
            a_specs.append(spec(a, tm, tk, gi, gk))
            b_specs.append(spec(b, tn, tk, gj, gk))
            dn = (((1,), (1,)), ((), ()))
        else:
            a_specs.append(spec(a, tk, tm, gk, gi))
            b_specs.append(spec(b, tk, tn, gk, gj))
            dn = (((0,), (0,)), ((), ()))

    epi_fn, epi_ins, epi_outs = epilogue if epilogue else (None, [], [])
    n_out = len(epi_outs) if epilogue else (2 if also_bf16 else 1)

    deps = [] if dep is None else [dep]

    def body(*refs):
        a_refs, b_refs = refs[:fold], refs[fold:2 * fold]
        e_refs = refs[2 * fold:2 * fold + len(epi_ins)]
        rest = refs[2 * fold + len(epi_ins) + len(deps):]
        o_refs, acc = rest[:n_out], rest[n_out:]
        part = None
        for a_ref, b_ref in zip(a_refs, b_refs):
            one = lax.dot_general(a_ref[...].astype(BF16), b_ref[...].astype(BF16), dn, preferred_element_type=F32)
            part = one if part is None else part + one

        def emit(val):
            vals = epi_fn(val, *[r[...] for r in e_refs]) if epilogue else [val] * n_out
            for o_ref, v in zip(o_refs, vals):
                o_ref[...] = v.astype(o_ref.dtype)

        if nk == 1:
            emit(part)
            return
        acc_ref, = acc
        kk = pl.program_id(2)

        @pl.when(kk == 0)
        def _():
            acc_ref[...] = part

        @pl.when(kk > 0)
        def _():
            acc_ref[...] += part

        @pl.when(kk == nk - 1)
        def _():
            emit(acc_ref[...])

    if out_stack:
        per = (n // out_stack) // tn
        out_spec = pl.BlockSpec((None, tm, tn), lambda i, j, kk: (j // per, i, j % per))
        shape = (out_stack, m, n // out_stack)
    else:
        out_spec = pl.BlockSpec((tm, tn), lambda i, j, kk: (i, j))
        shape = (m, n)
    if epilogue:
        assert not out_stack and not also_bf16
        kinds = {False: (pl.BlockSpec((tm, tn), lambda i, j, kk: (i, j)), (m, n)),
                 True: (pl.BlockSpec((tn, tm), lambda i, j, kk: (j, i)), (n, m)),
                 'pair': (pl.BlockSpec((2, tm, tn), lambda i, j, kk: (0, i, j)), (2, m, n))}
        out_specs = [kinds[t][0] for _, t in epi_outs]
        out_shapes = [jax.ShapeDtypeStruct(kinds[t][1], dt) for dt, t in epi_outs]
    else:
        out_specs = [out_spec] * n_out
        out_shapes = [jax.ShapeDtypeStruct(shape, dt) for dt in [out_dtype, BF16][:n_out]]
    e_pairs = [z if isinstance(z, tuple) else (z, 0) for z in epi_ins]
    assert all(off % tn == 0 for _, off in e_pairs)
    e_specs = [pl.BlockSpec((1, tn) if z.shape[0] == 1 else (tm, tn),
                            lambda i, j, kk, ob=off // tn, row=z.shape[0] == 1: (0 if row else i, j + ob))
               for z, off in e_pairs]
    res = pl.pallas_call(
        body, grid=(m // tm, n // tn, nk),
        in_specs=a_specs + b_specs + e_specs + [pl.BlockSpec(memory_space=pl.ANY)] * len(deps),
        out_specs=out_specs, out_shape=out_shapes,
        scratch_shapes=[pltpu.VMEM((tm, tn), F32)] if nk > 1 else [], name=name, compiler_params=_params(3),
    )(*[a] * fold, *[b] * fold, *[z for z, _ in e_pairs], *deps)
    return res if (also_bf16 or epilogue) else res[0]


def _all_gather(name, arrs, dep=None):
    n = len(arrs)
    deps = [] if dep is None else [dep]

    def body(*refs):
        ins, outs = refs[:n], refs[n + len(deps):2 * n + len(deps)]
        send_sems, recv_sems, local_sems = refs[2 * n + len(deps):]
        x, y, c = _dev()
        me, sib = (x, y, c), (x, y, 1 - c)
        x_nbr, y_nbr, diag = (1 - x, y), (x, 1 - y), (1 - x, 1 - y)
        north = c == 1
        relay_from = (jnp.where(north, 1 - x, x), jnp.where(north, y, 1 - y))
        relay_to = (jnp.where(north, x, 1 - x), jnp.where(north, 1 - y, y))

        def slot(p):
            return 4 * p[0] + 2 * p[1] + p[2]

        def copy(a, k, block, to, src=None):
            dst = outs[a].at[slot(block)]
            return pltpu.make_async_remote_copy(
                src_ref=dst if src is None else src, dst_ref=dst,
                send_sem=send_sems.at[7 * a + k], recv_sem=recv_sems.at[7 * a + k],
                device_id=to, device_id_type=MESH)

        mine = [pltpu.make_async_copy(ins[a], outs[a].at[slot(me)], local_sems.at[a]) for a in range(n)]
        for cp in mine:
            cp.start()
        sent = []
        for a in range(n):
            sent += [copy(a, 0, me, sib, src=ins[a]), copy(a, 1, me, (*x_nbr, c), src=ins[a]),
                     copy(a, 2, me, (*y_nbr, c), src=ins[a])]
        for cp in sent:
            cp.start()
        relays = [copy(a, 3, (*relay_from, c), (*relay_to, c)) for a in range(n)]
        for k, chip, relay_here in ((1, x_nbr, north), (2, y_nbr, jnp.logical_not(north)), (3, diag, None)):
            for a in range(n):
                copy(a, k, (*chip, c), me).wait_recv()
                cp = copy(a, 3 + k, (*chip, c), sib)
                cp.start()
                sent.append(cp)
                if relay_here is not None:
                    pl.when(relay_here)(relays[a].start)
        for a in range(n):
            copy(a, 0, sib, me).wait_recv()
            for k, chip in ((4, x_nbr), (5, y_nbr), (6, diag)):
                copy(a, k, (*chip, 1 - c), me).wait_recv()
        for cp in sent + relays:
            cp.wait_send()
        for cp in mine:
            cp.wait()

    any_spec = pl.BlockSpec(memory_space=pl.ANY)
    return pl.pallas_call(
        body, in_specs=[any_spec] * (n + len(deps)), out_specs=[any_spec] * n,
        out_shape=[jax.ShapeDtypeStruct((N_DEV,) + a.shape, a.dtype) for a in arrs],
        scratch_shapes=[pltpu.SemaphoreType.DMA((7 * n,)), pltpu.SemaphoreType.DMA((7 * n,)),
                        pltpu.SemaphoreType.DMA((n,))],
        name=name,
    )(*arrs, *deps)


FLIPS = [(0, 0, 1), (0, 1, 0), (1, 0, 0), (0, 1, 1), (1, 0, 1), (1, 1, 0), (1, 1, 1)]
N_PEERS = len(FLIPS)
_HBM = pl.BlockSpec(memory_space=pltpu.HBM)
_SEM = pl.BlockSpec(memory_space=pltpu.SEMAPHORE)
_EFFECT = pltpu.SideEffectType.DATAFLOW_SIDE_EFFECTING


def _flip(x, y, c, f):
    return (1 - x if f[0] else x, 1 - y if f[1] else y, 1 - c if f[2] else c)


def _slot(p):
    return 4 * p[0] + 2 * p[1] + p[2]


def _exchange_copies(src_refs, land_refs, send_sems, recv_sems, gather):
    x, y, c = _dev()
    mine = _slot((x, y, c))
    cps = []
    for a, (src, land) in enumerate(zip(src_refs, land_refs)):
        for k, f in enumerate(FLIPS):
            peer = _flip(x, y, c, f)
            cps.append(pltpu.make_async_remote_copy(
                src_ref=src if gather else src.at[_slot(peer)], dst_ref=land.at[mine],
                send_sem=send_sems.at[N_PEERS * a + k], recv_sem=recv_sems.at[N_PEERS * a + k],
                device_id=peer, device_id_type=MESH))
    return cps


def _exchange_start(name, srcs, gather, after):
    n = len(srcs)
    lands = [lax.empty(((N_DEV,) + s.shape) if gather else s.shape, s.dtype) for s in srcs]

    def body(*refs):
        src_refs, land_refs = refs[:n], refs[n:2 * n]
        send_sems, recv_sems, local_sems = refs[2 * n + 1:2 * n + 4]
        token = refs[-1]
        if gather:
            x, y, c = _dev()
            for a in range(n):
                pltpu.make_async_copy(src_refs[a], land_refs[a].at[_slot((x, y, c))], local_sems.at[a]).start()
        for cp in _exchange_copies(src_refs, land_refs, send_sems, recv_sems, gather):
            cp.start()
        token[...] = jnp.zeros_like(token)

    hbm = lambda z: pltpu.HBM(z.shape, z.dtype)
    outs = pl.pallas_call(
        body, name=name,
        out_shape=(pltpu.SemaphoreType.DMA((N_PEERS * n,)), pltpu.SemaphoreType.DMA((N_PEERS * n,)),
                   pltpu.SemaphoreType.DMA((n,)), *[hbm(s) for s in srcs], *[hbm(z) for z in lands],
                   jax.ShapeDtypeStruct((8, LANES), F32)),
        in_specs=[_HBM] * (2 * n) + [pl.BlockSpec(memory_space=pl.ANY)],
        out_specs=(_SEM, _SEM, _SEM, *[_HBM] * (2 * n), pl.BlockSpec(memory_space=pltpu.VMEM)),
        input_output_aliases={i: 3 + i for i in range(2 * n)},
        compiler_params=pltpu.CompilerParams(has_side_effects=_EFFECT),
    )(*[pltpu.with_memory_space_constraint(z, pltpu.HBM) for z in list(srcs) + lands], after)
    return (outs[:3], outs[3:3 + n], outs[3 + n:3 + 2 * n], gather), outs[-1]


def _exchange_wait(name, handles, after):
    sems, srcs, lands, gather = handles
    n = len(srcs)

    def body(*refs):
        src_refs, land_refs = refs[:n], refs[n:2 * n]
        send_sems, recv_sems, local_sems = refs[2 * n:2 * n + 3]
        if gather:
            for a in range(n):
                pltpu.make_async_copy(src_refs[a], land_refs[a].at[0], local_sems.at[a]).wait()
        for cp in _exchange_copies(src_refs, land_refs, send_sems, recv_sems, gather):
            cp.wait_send()
            cp.wait_recv()

    hbm = lambda z: pltpu.HBM(z.shape, z.dtype)
    outs = pl.pallas_call(
        body, name=name, out_shape=tuple(hbm(z) for z in list(srcs) + list(lands)),
        in_specs=[_HBM] * (2 * n) + [_SEM] * 3 + [pl.BlockSpec(memory_space=pl.ANY)],
        out_specs=tuple([_HBM] * (2 * n)), input_output_aliases={i: i for i in range(2 * n)},
        compiler_params=pltpu.CompilerParams(has_side_effects=_EFFECT),
    )(*srcs, *lands, *sems, after)
    return list(outs[n:])


def _pack_rows(sizes, width, row_align):
    offs, r = [], 0
    for s in sizes:
        offs.append(r)
        r += -(-s // width)
    total = -(-r // row_align) * row_align
    return offs, total


def _pack(items, width, row_align, lead=()):
    nl = len(lead)
    sizes = [int(jnp.size(a)) // max(1, functools.reduce(lambda p, q: p * q, lead, 1)) for a in items]
    offs, total = _pack_rows(sizes, width, row_align)
    flat = []
    used = 0
    for a, s in zip(items, sizes):
        f = a.reshape(lead + (s,))
        pad = -(-s // width) * width - s
        if pad:
            f = jnp.pad(f, [(0, 0)] * nl + [(0, pad)])
        flat.append(f)
        used += s + pad
    tail = total * width - used
    if tail:
        flat.append(jnp.zeros(lead + (tail,), items[0].dtype))
    return jnp.concatenate(flat, axis=-1).reshape(lead + (total, width)), offs


def _unpack(packed, off, shape, lead=()):
    nl = len(lead)
    size = functools.reduce(lambda p, q: p * q, shape, 1)
    width = packed.shape[-1]
    rows = -(-size // width)
    blk = lax.slice_in_dim(packed, off, off + rows, axis=nl).reshape(lead + (rows * width,))
    return lax.slice_in_dim(blk, 0, size, axis=nl).reshape(lead + tuple(shape))


def _rms(x, g):
    return (x * lax.rsqrt(jnp.mean(x * x, axis=-1, keepdims=True) + RMS_EPS)) * g


def _norm_mod(x, g, sc, sh):
    return _rms(x, g) * (1.0 + sc) + sh


def _mix_fn(glu_a, glu_b, attn, ga, gs):
    return jax.nn.sigmoid(ga) * attn + jax.nn.sigmoid(gs) * (glu_a * jax.nn.sigmoid(glu_b))


def _s5_disc_fn(a_re, a_im, log_dt, b_re, b_im):
    dt = jnp.exp(log_dt)
    mag = jnp.exp(a_re * dt)
    lr, li = mag * jnp.cos(a_im * dt), mag * jnp.sin(a_im * dt)
    den = a_re * a_re + a_im * a_im
    zr = ((lr - 1.0) * a_re + li * a_im) / den
    zi = (li * a_re - (lr - 1.0) * a_im) / den
    return lr, li, zr[None] * b_re - zi[None] * b_im, zr[None] * b_im + zi[None] * b_re


def _adamw_fn(w, g, m, v):
    m = ADAM_B1 * m + (1.0 - ADAM_B1) * g
    v = ADAM_B2 * v + (1.0 - ADAM_B2) * jnp.square(g)
    m_hat = m / (1.0 - ADAM_B1 ** ADAM_STEP)
    v_hat = v / (1.0 - ADAM_B2 ** ADAM_STEP)
    delta = -ADAM_LR * (m_hat / (jnp.sqrt(v_hat) + ADAM_EPS) + ADAM_WD * w)
    return delta, m, v


def _adamw(name, parts, w, m, v):
    p, r, c = parts.shape
    tr = _pick(r, max(8, (1 << 21) // (4 * c * max(p, 2))), 8)

    def fn(pv, wv, mv, vv):
        g = pv[0]
        for i in range(1, p):
            g = g + pv[i]
        d, m2, v2 = _adamw_fn(wv, g, mv, vv)
        return g, d, m2, v2

    spec = pl.BlockSpec((tr, c), lambda i: (i, 0))
    return _tile_call(
        name, fn, (r // tr,), [parts, w, m, v],
        [pl.BlockSpec((p, tr, c), lambda i: (0, i, 0)), spec, spec, spec],
        [jax.ShapeDtypeStruct((r, c), F32)] * 4, [spec] * 4)


def _adamw_sharded(name, parts, own_src, w, m, v, place):
    _, k, n = parts.shape
    tr = _pick(k, max(16, (1 << 19) // (4 * n)), 16)

    def body(pl_ref, p_ref, a_ref, w_ref, m_ref, v_ref, g_ref, d_ref, m2_ref, v2_ref):
        own = a_ref[0]
        g = None
        for q in range(N_DEV):
            term = jnp.where(pl_ref[0] == q, own, p_ref[q].astype(F32))
            g = term if g is None else g + term
        d, m2, v2 = _adamw_fn(w_ref[...], g, m_ref[...], v_ref[...])
        g_ref[...] = g
        d_ref[...] = d
        m2_ref[...] = m2
        v2_ref[...] = v2

    spec = pl.BlockSpec((tr, n), lambda i, pr: (i, 0))
    return pl.pallas_call(
        body,
        grid_spec=pltpu.PrefetchScalarGridSpec(
            num_scalar_prefetch=1, grid=(k // tr,),
            in_specs=[pl.BlockSpec((N_DEV, tr, n), lambda i, pr: (0, i, 0)),
                      pl.BlockSpec((1, tr, n), lambda i, pr: (pr[1], i, 0)),
                      spec, spec, spec],
            out_specs=[spec] * 4),
        out_shape=[jax.ShapeDtypeStruct((k, n), F32)] * 4, name=name, compiler_params=_params(1),
    )(place, parts, own_src, w, m, v)


def _attn_mask(n, rows):
    qi = lax.broadcasted_iota(jnp.int32, (rows, 2 * ATT_BLOCK), 0) & (ATT_BLOCK - 1)
    kj = lax.broadcasted_iota(jnp.int32, (rows, 2 * ATT_BLOCK), 1)
    rel = qi + ATT_BLOCK - kj
    return (rel >= 0) & (rel < ATT_BLOCK) & ((kj >= ATT_BLOCK) | (n > 0))


def _attn_probs(q, k, sink, mask):
    s = lax.dot_general(q, k, (((1,), (1,)), ((), ())), preferred_element_type=F32) * (HEAD_DIM ** -0.5)
    s = jnp.where(mask, s, NEG_INF)
    m = jnp.maximum(jnp.max(s, axis=-1, keepdims=True), sink)
    p = jnp.exp(s - m)
    e_sink = jnp.exp(sink - m)
    inv = 1.0 / (jnp.sum(p, axis=-1, keepdims=True) + e_sink)
    return p * inv, e_sink * inv


def _attn_specs(qpk):
    blk = ATT_BLOCK
    q_spec = pl.BlockSpec((qpk, blk, HEAD_DIM), lambda h, n: (h, n, 0))
    cur = pl.BlockSpec((1, blk, HEAD_DIM), lambda h, n: (h, n, 0))
    prev = pl.BlockSpec((1, blk, HEAD_DIM), lambda h, n: (h, jnp.maximum(n - 1, 0), 0))
    sink_spec = pl.BlockSpec((1, qpk * blk, 1), lambda h, n: (h, 0, 0))
    return q_spec, cur, prev, sink_spec


def _attn_fwd(q, k, v, sinks):
    hq, l, _ = q.shape
    qpk = hq // N_KV_HEADS
    nb = l // ATT_BLOCK
    rows = qpk * ATT_BLOCK
    q_spec, cur, prev, sink_spec = _attn_specs(qpk)

    def body(q_ref, kp_ref, kc_ref, vp_ref, vc_ref, sink_ref, o_ref):
        mask = _attn_mask(pl.program_id(1), rows)
        kk = jnp.concatenate([kp_ref[0], kc_ref[0]], axis=0).astype(BF16)
        vv = jnp.concatenate([vp_ref[0], vc_ref[0]], axis=0).astype(BF16)
        p, _ = _attn_probs(q_ref[...].reshape(rows, HEAD_DIM).astype(BF16), kk, sink_ref[0], mask)
        o = jnp.dot(p.astype(BF16), vv, preferred_element_type=F32)
        o_ref[...] = o.reshape(qpk, ATT_BLOCK, HEAD_DIM).astype(o_ref.dtype)

    return pl.pallas_call(
        body, grid=(N_KV_HEADS, nb), in_specs=[q_spec, prev, cur, prev, cur, sink_spec],
        out_specs=q_spec, out_shape=jax.ShapeDtypeStruct((hq, l, HEAD_DIM), BF16),
        name="attn_fwd", compiler_params=_params(2),
    )(q, k, k, v, v, sinks)


def _attn_bwd(q, k, v, sinks, do):
    hq, l, _ = q.shape
    qpk = hq // N_KV_HEADS
    nb = l // ATT_BLOCK
    blk = ATT_BLOCK
    rows = qpk * blk
    q_spec, cur, prev, sink_spec = _attn_specs(qpk)
    part_spec = pl.BlockSpec((1, 1, 2 * blk, HEAD_DIM), lambda h, n: (h, n, 0, 0))
    dsink_spec = pl.BlockSpec((qpk, 1, LANES), lambda h, n: (h, 0, 0))
    tn = (((0,), (0,)), ((), ()))

    def body(q_ref, do_ref, kp_ref, kc_ref, vp_ref, vc_ref, sink_ref, dq_ref, dkp_ref, dvp_ref, dsink_ref):
        n = pl.program_id(1)
        mask = _attn_mask(n, rows)
        kk = jnp.concatenate([kp_ref[0], kc_ref[0]], axis=0).astype(BF16)
        vv = jnp.concatenate([vp_ref[0], vc_ref[0]], axis=0).astype(BF16)
        qb = q_ref[...].reshape(rows, HEAD_DIM).astype(BF16)
        do32 = do_ref[...].astype(F32).reshape(rows, HEAD_DIM)
        dob = do32.astype(BF16)
        p, p_sink = _attn_probs(qb, kk, sink_ref[0], mask)
        pb = p.astype(BF16)
        o = jnp.dot(pb, vv, preferred_element_type=F32)
        delta = jnp.sum(do32 * o, axis=-1, keepdims=True)
        dp = lax.dot_general(dob, vv, (((1,), (1,)), ((), ())), preferred_element_type=F32)
        ds = (p * (dp - delta) * (HEAD_DIM ** -0.5)).astype(BF16)
        dq = jnp.dot(ds, kk, preferred_element_type=F32)
        dq_ref[...] = dq.reshape(qpk, blk, HEAD_DIM).astype(dq_ref.dtype)
        dkp_ref[0, 0] = lax.dot_general(ds, qb, tn, preferred_element_type=F32)
        dvp_ref[0, 0] = lax.dot_general(pb, dob, tn, preferred_element_type=F32)
        dsr = p_sink * delta
        for g in range(qpk):
            dsg = jnp.broadcast_to(-_colsum(dsr[g * blk:(g + 1) * blk]), (1, LANES))

            @pl.when(n == 0)
            def _():
                dsink_ref[g] = dsg

            @pl.when(n > 0)
            def _():
                dsink_ref[g] += dsg


    part_shape = jax.ShapeDtypeStruct((N_KV_HEADS, nb, 2 * blk, HEAD_DIM), F32)
    dq, dkp, dvp, dsink = pl.pallas_call(
        body, grid=(N_KV_HEADS, nb), in_specs=[q_spec, q_spec, prev, cur, prev, cur, sink_spec],
        out_specs=[q_spec, part_spec, part_spec, dsink_spec],
        out_shape=[jax.ShapeDtypeStruct((hq, l, HEAD_DIM), BF16), part_shape, part_shape,
                   jax.ShapeDtypeStruct((hq, 1, LANES), F32)],
        name="attn_bwd", compiler_params=_params(2),
    )(q, do, k, k, v, v, sinks)

    def combine(a_cur, a_nxt, b_cur, b_nxt):
        last = pl.program_id(1) == nb - 1
        keep = jnp.where(last, 0.0, 1.0)
        return (a_cur[0, 0, blk:] + keep * a_nxt[0, 0, :blk])[None], (b_cur[0, 0, blk:] + keep * b_nxt[0, 0, :blk])[None]

    nxt_spec = pl.BlockSpec((1, 1, 2 * blk, HEAD_DIM), lambda h, n: (h, jnp.minimum(n + 1, nb - 1), 0, 0))
    kv_shape = jax.ShapeDtypeStruct((N_KV_HEADS, l, HEAD_DIM), BF16)
    dk, dv = _tile_call("attn_dkv", combine, (N_KV_HEADS, nb), [dkp, dkp, dvp, dvp],
                        [part_spec, nxt_spec, part_spec, nxt_spec], [kv_shape, kv_shape], [cur, cur])
    return dq, dk, dv, dsink


def _block_diag(m):
    j, gl, a, b = m.shape
    eye = jnp.eye(gl, dtype=m.dtype)
    return (m[:, :, :, None, :] * eye[None, :, None, :, None]).reshape(j, gl * a, gl * b)


def _diag_blocks(z, a):
    j = z.shape[0]
    gl = z.shape[1] // a
    b = z.shape[2] // gl
    d = jnp.diagonal(z.reshape(j, gl, a, gl, b), axis1=1, axis2=3)
    return d.transpose(0, 3, 1, 2)


def _s5_permute(src_ref, dst_ref, t_len):
    seg = t_len // 8
    for k in range(seg):
        dst_ref[8 * k:8 * k + 8, :] = src_ref[pl.ds(k, 8, stride=seg), :]


def _s5_unpermute(perm_ref, t_len, emit):
    per_seg = t_len // 64
    for m in range(t_len // 8):
        emit(8 * m, perm_ref[pl.ds(64 * (m % per_seg) + m // per_seg, 8, stride=8), :])


def _s5_powers(p_ref, lr, li, seg):
    hs = TILE_STATES

    def step(k, carry):
        pr, pi = carry
        p_ref[pl.ds(k, 1), 0:hs] = pr
        p_ref[pl.ds(k, 1), hs:2 * hs] = pi
        return lr * pr - li * pi, lr * pi + li * pr

    lax.fori_loop(0, seg, step, (lr, li))


def _s5_local_scan(x_ref, base, lr, li, seg, reverse):
    hs = TILE_STATES
    lr8, li8 = jnp.broadcast_to(lr, (8, hs)), jnp.broadcast_to(li, (8, hs))
    if reverse:
        li8 = -li8

    def step(i, carry):
        hr, hi = carry
        k = seg - 1 - i if reverse else i
        rows = pl.ds(pl.multiple_of(base + 8 * k, 8), 8)
        nr = lr8 * hr - li8 * hi + x_ref[rows, 0:hs]
        ni = lr8 * hi + li8 * hr + x_ref[rows, hs:2 * hs]
        x_ref[rows, 0:hs] = nr
        x_ref[rows, hs:2 * hs] = ni
        return nr, ni

    zero = jnp.zeros((8, hs), F32)
    return lax.fori_loop(0, seg, step, (zero, zero), unroll=2)


def _s5_carries(c_ref, e_ref, ends, start, pw_r, pw_i, reverse):
    hs = TILE_STATES
    e_ref[:, 0:hs] = ends[0]
    e_ref[:, hs:2 * hs] = ends[1]
    cr, ci = start
    if reverse:
        pw_i = -pw_i
    for s in (range(7, -1, -1) if reverse else range(8)):
        c_ref[s:s + 1, 0:hs] = cr
        c_ref[s:s + 1, hs:2 * hs] = ci
        er, ei = e_ref[s:s + 1, 0:hs], e_ref[s:s + 1, hs:2 * hs]
        cr, ci = er + pw_r * cr - pw_i * ci, ei + pw_r * ci + pw_i * cr
    return cr, ci


def _s5_states(u_perm_b16, bd_ref, x_ref, base, c_ref, e_ref, p_ref, lr, li, h_in, t_len):
    hs = TILE_STATES
    seg = t_len // 8
    x_ref[pl.ds(base, t_len), :] = jnp.dot(u_perm_b16, bd_ref[0], preferred_element_type=F32)
    ends = _s5_local_scan(x_ref, base, lr, li, seg, False)
    pw_r, pw_i = p_ref[seg - 1:seg, 0:hs], p_ref[seg - 1:seg, hs:2 * hs]
    h_out = _s5_carries(c_ref, e_ref, ends, h_in, pw_r, pw_i, False)
    cr, ci = c_ref[:, 0:hs], c_ref[:, hs:2 * hs]

    def fix(k, carry):
        rows = pl.ds(pl.multiple_of(base + 8 * k, 8), 8)
        pr, pi = p_ref[pl.ds(k, 1), 0:hs], p_ref[pl.ds(k, 1), hs:2 * hs]
        x_ref[rows, 0:hs] += pr * cr - pi * ci
        x_ref[rows, hs:2 * hs] += pr * ci + pi * cr
        return carry

    lax.fori_loop(0, seg, fix, 0, unroll=2)
    return h_out


def _s5_fwd(proj, u_off, bd, cbd, lam, dvec, t_len):
    l = proj.shape[0]
    nj = bd.shape[0]
    nch = l // t_len
    hs = TILE_STATES
    ub = u_off // LANES
    seg = t_len // 8
    assert t_len % 64 == 0

    def body(u_ref, bd_ref, cbd_ref, lam_ref, d_ref, y_ref, hst_ref, x_ref, h_ref, p_ref, c_ref, e_ref, up_ref, yp_ref):
        lr, li = lam_ref[0, 0:1, :], lam_ref[0, 1:2, :]

        @pl.when(pl.program_id(1) == 0)
        def _():
            h_ref[...] = jnp.zeros_like(h_ref)
            _s5_powers(p_ref, lr, li, seg)

        hst_ref[0, 0] = h_ref[...]
        _s5_permute(u_ref, up_ref, t_len)
        h_out = _s5_states(up_ref[...].astype(BF16), bd_ref, x_ref, 0, c_ref, e_ref, p_ref, lr, li,
                           (h_ref[:, 0:hs], h_ref[:, hs:2 * hs]), t_len)
        h_ref[:, 0:hs] = h_out[0]
        h_ref[:, hs:2 * hs] = h_out[1]
        yp_ref[...] = jnp.dot(x_ref[...].astype(BF16), cbd_ref[0], preferred_element_type=F32)
        dv = d_ref[0]

        def out(r0, rows):
            y_ref[r0:r0 + 8, :] = rows + dv * u_ref[r0:r0 + 8, :]

        _s5_unpermute(yp_ref, t_len, out)

    return pl.pallas_call(
        body, grid=(nj, nch),
        in_specs=[pl.BlockSpec((t_len, LANES), lambda j, c: (c, ub + j)),
                  pl.BlockSpec((1, LANES, 2 * hs), lambda j, c: (j, 0, 0)),
                  pl.BlockSpec((1, 2 * hs, LANES), lambda j, c: (j, 0, 0)),
                  pl.BlockSpec((1, 2, hs), lambda j, c: (j, 0, 0)),
                  pl.BlockSpec((1, 1, LANES), lambda j, c: (j, 0, 0))],
        out_specs=[pl.BlockSpec((t_len, LANES), lambda j, c: (c, j)),
                   pl.BlockSpec((1, 1, 1, 2 * hs), lambda j, c: (j, c, 0, 0))],
        out_shape=[jax.ShapeDtypeStruct((l, nj * LANES), F32),
                   jax.ShapeDtypeStruct((nj, nch, 1, 2 * hs), F32)],
        scratch_shapes=[pltpu.VMEM((t_len, 2 * hs), F32), pltpu.VMEM((1, 2 * hs), F32),
                        pltpu.VMEM((seg, 2 * hs), F32), pltpu.VMEM((8, 2 * hs), F32), pltpu.VMEM((8, 2 * hs), F32),
                        pltpu.VMEM((t_len, LANES), F32), pltpu.VMEM((t_len, LANES), F32)],
        name="s5_fwd", compiler_params=_params(2),
    )(proj, bd, cbd, lam, dvec)


def _s5_bwd(proj, u_off, dy, hst, bd, bdt, cbdt, lam, dvec, t_len):
    l = proj.shape[0]
    nj = bd.shape[0]
    nch = l // t_len
    hs = TILE_STATES
    ub = u_off // LANES
    seg = t_len // 8
    tn = (((0,), (0,)), ((), ()))
    assert t_len % 64 == 0

    def body(u_ref, dy_ref, hst_ref, bd_ref, bdt_ref, cbdt_ref, lam_ref, d_ref,
             du_ref, dbd_ref, dcbdt_ref, dlam_ref, dd_ref,
             x_ref, g_ref, gc_ref, p_ref, c_ref, e_ref, up_ref, dyp_ref, dup_ref):
        first = pl.program_id(1) == 0
        lr, li = lam_ref[0, 0:1, :], lam_ref[0, 1:2, :]

        @pl.when(first)
        def _():
            gc_ref[...] = jnp.zeros_like(gc_ref)
            _s5_powers(p_ref, lr, li, seg)

        _s5_permute(u_ref, up_ref, t_len)
        _s5_permute(dy_ref, dyp_ref, t_len)
        ub16, dyb16 = up_ref[...].astype(BF16), dyp_ref[...].astype(BF16)
        h0 = hst_ref[0, 0]
        _s5_states(ub16, bd_ref, x_ref, 8, c_ref, e_ref, p_ref, lr, li, (h0[:, 0:hs], h0[:, hs:2 * hs]), t_len)
        x_ref[0:8, :] = c_ref[...]
        g_ref[...] = jnp.dot(dyb16, cbdt_ref[0], preferred_element_type=F32)
        starts = _s5_local_scan(g_ref, 0, lr, li, seg, True)
        pw_r, pw_i = p_ref[seg - 1:seg, 0:hs], p_ref[seg - 1:seg, hs:2 * hs]
        g_out = _s5_carries(c_ref, e_ref, starts, (gc_ref[:, 0:hs], gc_ref[:, hs:2 * hs]), pw_r, pw_i, True)
        gc_ref[:, 0:hs] = g_out[0]
        gc_ref[:, hs:2 * hs] = g_out[1]
        cr, ci = c_ref[:, 0:hs], c_ref[:, hs:2 * hs]

        def fix(k, carry):
            alr, ali = carry
            rows = pl.ds(pl.multiple_of(8 * k, 8), 8)
            pr, pi = p_ref[pl.ds(seg - 1 - k, 1), 0:hs], p_ref[pl.ds(seg - 1 - k, 1), hs:2 * hs]
            gr = g_ref[rows, 0:hs] + pr * cr + pi * ci
            gi = g_ref[rows, hs:2 * hs] + pr * ci - pi * cr
            g_ref[rows, 0:hs] = gr
            g_ref[rows, hs:2 * hs] = gi
            hpr, hpi = x_ref[rows, 0:hs], x_ref[rows, hs:2 * hs]
            return alr + gr * hpr + gi * hpi, ali + gi * hpr - gr * hpi

        zero = jnp.zeros((8, hs), F32)
        alr, ali = lax.fori_loop(0, seg, fix, (zero, zero), unroll=2)
        alr, ali = _colsum(alr), _colsum(ali)
        g = g_ref[...].astype(BF16)
        h = x_ref[pl.ds(8, t_len), :].astype(BF16)
        dup_ref[...] = jnp.dot(g, bdt_ref[0], preferred_element_type=F32)
        dv = d_ref[0]

        def out(r0, rows):
            du_ref[r0:r0 + 8, :] = (rows + dv * dy_ref[r0:r0 + 8, :]).astype(du_ref.dtype)

        _s5_unpermute(dup_ref, t_len, out)
        sign = jnp.where(lax.broadcasted_iota(jnp.int32, (1, 2 * hs), 1) < hs, 1.0, -1.0)
        dbd = lax.dot_general(ub16, g, tn, preferred_element_type=F32)
        dcbdt = lax.dot_general(dyb16, h, tn, preferred_element_type=F32) * sign
        ddv = _colsum(dy_ref[...] * u_ref[...])

        @pl.when(first)
        def _():
            dbd_ref[0] = dbd
            dcbdt_ref[0] = dcbdt
            dlam_ref[0, 0:1, :] = alr
            dlam_ref[0, 1:2, :] = ali
            dd_ref[0] = ddv

        @pl.when(jnp.logical_not(first))
        def _():
            dbd_ref[0] += dbd
            dcbdt_ref[0] += dcbdt
            dlam_ref[0, 0:1, :] += alr
            dlam_ref[0, 1:2, :] += ali
            dd_ref[0] += ddv

    rev = lambda c: nch - 1 - c
    wide = pl.BlockSpec((1, LANES, 2 * hs), lambda j, c: (j, 0, 0))
    tall = pl.BlockSpec((1, 2 * hs, LANES), lambda j, c: (j, 0, 0))
    return pl.pallas_call(
        body, grid=(nj, nch),
        in_specs=[pl.BlockSpec((t_len, LANES), lambda j, c: (rev(c), ub + j)),
                  pl.BlockSpec((t_len, LANES), lambda j, c: (rev(c), j)),
                  pl.BlockSpec((1, 1, 1, 2 * hs), lambda j, c: (j, rev(c), 0, 0)),
                  wide, tall, wide,
                  pl.BlockSpec((1, 2, hs), lambda j, c: (j, 0, 0)),
                  pl.BlockSpec((1, 1, LANES), lambda j, c: (j, 0, 0))],
        out_specs=[pl.BlockSpec((t_len, LANES), lambda j, c: (rev(c), j)),
                   wide, wide,
                   pl.BlockSpec((1, 2, hs), lambda j, c: (j, 0, 0)),
                   pl.BlockSpec((1, 1, LANES), lambda j, c: (j, 0, 0))],
        out_shape=[jax.ShapeDtypeStruct((l, nj * LANES), BF16),
                   jax.ShapeDtypeStruct((nj, LANES, 2 * hs), F32),
                   jax.ShapeDtypeStruct((nj, LANES, 2 * hs), F32),
                   jax.ShapeDtypeStruct((nj, 2, hs), F32),
                   jax.ShapeDtypeStruct((nj, 1, LANES), F32)],
        scratch_shapes=[pltpu.VMEM((t_len + 8, 2 * hs), F32), pltpu.VMEM((t_len, 2 * hs), F32),
                        pltpu.VMEM((1, 2 * hs), F32), pltpu.VMEM((seg, 2 * hs), F32),
                        pltpu.VMEM((8, 2 * hs), F32), pltpu.VMEM((8, 2 * hs), F32),
                        pltpu.VMEM((t_len, LANES), F32), pltpu.VMEM((t_len, LANES), F32),
                        pltpu.VMEM((t_len, LANES), F32)],
        name="s5_bwd", compiler_params=_params(2),
    )(proj, dy, hst, bd, bdt, cbdt, lam, dvec)


def _full_spec(shape):
    nd = len(shape)
    return pl.BlockSpec(tuple(shape), lambda i: (0,) * nd)


def _sds(shape, dtype=F32):
    return jax.ShapeDtypeStruct(tuple(shape), dtype)


def kernel(x, c, ada_w, ada_b, norm_mix_g, w_in, attn_sinks, w_attn_proj, ssm_a_re, ssm_a_im, ssm_log_dt, ssm_b_re, ssm_b_im, ssm_c_re, ssm_c_im, ssm_d, w_ssm_glu, w_out, norm_ffn_g, w_ffn_up, ffn_conv_w, ffn_conv_b, w_ffn_down, final_g, loss_target, m_ada_w, m_ada_b, m_norm_mix_g, m_w_in, m_attn_sinks, m_w_attn_proj, m_ssm_a_re, m_ssm_a_im, m_ssm_log_dt, m_ssm_b_re, m_ssm_b_im, m_ssm_c_re, m_ssm_c_im, m_ssm_d, m_w_ssm_glu, m_w_out, m_norm_ffn_g, m_w_ffn_up, m_ffn_conv_w, m_ffn_conv_b, m_w_ffn_down, m_final_g, v_ada_w, v_ada_b, v_norm_mix_g, v_w_in, v_attn_sinks, v_w_attn_proj, v_ssm_a_re, v_ssm_a_im, v_ssm_log_dt, v_ssm_b_re, v_ssm_b_im, v_ssm_c_re, v_ssm_c_im, v_ssm_d, v_w_ssm_glu, v_w_out, v_norm_ffn_g, v_w_ffn_up, v_ffn_conv_w, v_ffn_conv_b, v_w_ffn_down, v_final_g):
    given = dict(locals())
    names = ['ada_w', 'ada_b', 'norm_mix_g', 'w_in', 'attn_sinks', 'w_attn_proj', 'ssm_a_re', 'ssm_a_im',
             'ssm_log_dt', 'ssm_b_re', 'ssm_b_im', 'ssm_c_re', 'ssm_c_im', 'ssm_d', 'w_ssm_glu', 'w_out',
             'norm_ffn_g', 'w_ffn_up', 'ffn_conv_w', 'ffn_conv_b', 'w_ffn_down', 'final_g']

    xs = x[0]
    tgt = loss_target[0]
    l, d = xs.shape
    attn_w = w_attn_proj.shape[1]
    ssm_w = w_ssm_glu.shape[1]
    hq = attn_sinks.shape[1]
    qpk = hq // N_KV_HEADS
    kv_w = N_KV_HEADS * HEAD_DIM
    n_groups = ssm_a_re.shape[1]
    dff = ffn_conv_b.shape[1]
    in_w = attn_w + 2 * kv_w + ssm_w + 2 * d
    nj = ssm_w // LANES
    off_k, off_v, off_u = attn_w, attn_w + kv_w, attn_w + 2 * kv_w
    off_ga, off_gs = off_u + ssm_w, off_u + ssm_w + d
    assert hq * HEAD_DIM == attn_w and n_groups * SSM_P == ssm_w and l % ATT_BLOCK == 0

    xi, yi, ci = _dev()
    idx = 4 * xi + 2 * yi + ci

    row_sharded = {'w_out': (d, d), 'w_ffn_down': (dff, d)}
    big = ['w_in', 'w_attn_proj', 'w_ssm_glu', 'w_out', 'w_ffn_up', 'w_ffn_down']
    spack, s_offs = _pack([c, ffn_conv_w[0]], LANES, 8)
    w16 = {k: given[k][0].astype(BF16) for k in big}
    wg_in, sg = _all_gather("gather_first", [w16['w_in'], spack])
    mixer_w = ['w_attn_proj', 'w_ssm_glu', 'w_out']
    h_mixer, tok = _exchange_start("gather_mixer_start", [w16[k] for k in mixer_w], True, wg_in)
    h_up, tok = _exchange_start("gather_ffn_up_start", [w16['w_ffn_up']], True, tok)
    h_down, tok = _exchange_start("gather_ffn_down_start", [w16['w_ffn_down']], True, tok)
    full = {'w_in': wg_in.transpose(1, 0, 2).reshape(d, in_w)}
    c_all = _unpack(sg, s_offs[0], (d,), lead=(N_DEV,))
    conv_w = _unpack(sg, s_offs[1], ffn_conv_w.shape[1:], lead=(N_DEV,)).transpose(1, 0, 2).reshape(3, dff)
    conv_b = ffn_conv_b

    mod_n = ada_w.shape[2]
    tcm = _pick(mod_n, 512)
    ada_b_mine = lax.dynamic_slice_in_dim(ada_b, idx * mod_n, mod_n, axis=1)

    def modpart_fn(cv, wv, bv):
        cond = cv * jax.nn.sigmoid(cv)
        return jnp.dot(cond.astype(BF16), wv.astype(BF16), preferred_element_type=F32) + bv, cond

    modp, cond_all = _tile_call(
        "ada_rows", modpart_fn, (mod_n // tcm,), [c_all, ada_w[0], ada_b_mine],
        [pl.BlockSpec((N_DEV, d), lambda j: (0, 0)), pl.BlockSpec((d, tcm), lambda j: (0, j)),
         pl.BlockSpec((1, tcm), lambda j: (0, j))],
        [_sds((N_DEV, mod_n)), _sds((N_DEV, d))],
        [pl.BlockSpec((N_DEV, tcm), lambda j: (0, j)), pl.BlockSpec((N_DEV, d), lambda j: (0, 0))])
    (modg,) = _all_gather("gather_ada_rows", [modp])
    mod = lax.dynamic_index_in_dim(modg, idx, axis=1, keepdims=False).reshape(1, N_DEV * mod_n)
    sh1, sc1, g1, sh2, sc2, g2 = [mod[:, i * d:(i + 1) * d] for i in range(6)]

    tr = _pick(l, 256, 8)
    trh = _pick(l, 128, 8)
    nr, nrh = l // tr, l // trh
    g_mix, g_ffn, g_fin = norm_mix_g + tok[0:1, 0:1], norm_ffn_g, final_g.reshape(1, d)

    def with_t(fn):
        def wrapped(*vals):
            out = fn(*vals)
            out = out if isinstance(out, tuple) else (out,)
            return out + (out[-1].T,)
        return wrapped

    h1, h1_t = _tile_call("norm_mod_mix", with_t(_norm_mod), (1, nr), [xs, g_mix, sc1, sh1],
                          [_t(tr, d), _v(d), _v(d), _v(d)], [_sds((l, d), BF16), _sds((d, l), BF16)],
                          [_t(tr, d), _tt(tr, d)])
    proj = _matmul("proj_in", h1, full['w_in'], "nn", tn=1280)

    def heads(z, n):
        return z.reshape(l, n, HEAD_DIM).transpose(1, 0, 2)

    qh = heads(proj[:, :attn_w], hq)
    kh = heads(proj[:, off_k:off_k + kv_w], N_KV_HEADS)
    vh = heads(proj[:, off_v:off_v + kv_w], N_KV_HEADS)
    sinks3 = jnp.repeat(attn_sinks.reshape(N_KV_HEADS, qpk), ATT_BLOCK, axis=1)[..., None]
    o_h = _attn_fwd(qh, kh, vh, sinks3)
    o2 = o_h.transpose(1, 0, 2).reshape(l, attn_w)

    gn = (n_groups, SSM_N)
    pgn = (SSM_P, n_groups, SSM_N)
    a_re, a_im, log_dt = ssm_a_re[0], ssm_a_im[0], ssm_log_dt[0].reshape(n_groups, 1)
    b_re, b_im = ssm_b_re[0].transpose(2, 0, 1), ssm_b_im[0].transpose(2, 0, 1)
    disc_ins = [a_re, a_im, log_dt, b_re, b_im]
    disc_specs = [_full_spec(gn), _full_spec(gn), _full_spec((n_groups, 1)), _full_spec(pgn), _full_spec(pgn)]
    lam_r, lam_i, bb_r, bb_i = _tile_call(
        "s5_discretise", _s5_disc_fn, (1,), disc_ins, disc_specs,
        [_sds(gn), _sds(gn), _sds(pgn), _sds(pgn)],
        [_full_spec(gn), _full_spec(gn), _full_spec(pgn), _full_spec(pgn)])

    def tiles_gpn(z):
        return z.reshape(SSM_P, nj, TILE_GROUPS, SSM_N).transpose(1, 2, 0, 3)

    bd = jnp.concatenate([_block_diag(tiles_gpn(bb_r)), _block_diag(tiles_gpn(bb_i))], axis=2).astype(BF16)
    c_r = ssm_c_re[0].reshape(nj, TILE_GROUPS, SSM_P, SSM_N).transpose(0, 1, 3, 2)
    c_i = (-ssm_c_im[0]).reshape(nj, TILE_GROUPS, SSM_P, SSM_N).transpose(0, 1, 3, 2)
    cbd = jnp.concatenate([_block_diag(c_r), _block_diag(c_i)], axis=1).astype(BF16)
    bdt, cbdt = bd.transpose(0, 2, 1), cbd.transpose(0, 2, 1)
    lam = jnp.stack([lam_r.reshape(nj, TILE_STATES), lam_i.reshape(nj, TILE_STATES)], axis=1)
    dvec = ssm_d[0].reshape(nj, 1, LANES)
    t_len = _pick(l, 1024, 8)
    y, hst = _s5_fwd(proj, off_u, bd, cbd, lam, dvec, t_len)

    tcs, trg = _pick(ssm_w, 1024), _pick(l, 512, 8)
    gy = _tile_call("gelu", lambda v: jax.nn.gelu(v), (ssm_w // tcs, l // trg), [y], [_t(trg, tcs)],
                    [_sds((l, ssm_w), BF16)], [_t(trg, tcs)])[0]
    full.update(zip(mixer_w, _exchange_wait("gather_mixer_wait", h_mixer, gy)))
    full['w_out'] = full['w_out'].reshape(row_sharded['w_out'])
    full['w_attn_proj'] = full['w_attn_proj'].transpose(1, 0, 2).reshape(attn_w, d)
    full['w_ssm_glu'] = full['w_ssm_glu'].transpose(1, 0, 2).reshape(ssm_w, 2 * d)
    glu = _matmul("ssm_glu", gy, full['w_ssm_glu'], "nn")

    tcd = 256 if d % 256 == 0 and off_ga % 256 == 0 else LANES
    assert d % tcd == 0 and off_ga % tcd == 0 and off_gs % tcd == 0
    gate_ins = [(glu, 0), (glu, d), (proj, off_ga), (proj, off_gs)]

    def mix_epilogue(at, ga_, gb_, pa, ps):
        return at, _mix_fn(ga_, gb_, at, pa, ps)

    attn, mixed = _matmul("attn_proj_gate_mix", o2, full['w_attn_proj'], "nn", tn=tcd,
                          epilogue=(mix_epilogue, gate_ins, [(F32, False), (BF16, False)]))
    def res_norm_fn(xv, mo, g1v, gv, scv, shv):
        x2v = xv + g1v * mo
        return x2v, _norm_mod(x2v, gv, scv, shv)

    def res_norm_epilogue(mo, xv, g1v, gv, scv, shv):
        x2v, h2v = res_norm_fn(xv, mo, g1v, gv, scv, shv)
        return mo, x2v, h2v, h2v.T

    mixout, x2, h2, h2_t = _matmul(
        "mix_out_residual_norm_mod_ffn", mixed, full['w_out'], "nn", tm=256, tn=d,
        epilogue=(res_norm_epilogue, [xs, g1, g_ffn, sc2, sh2], [(F32, False), (F32, False), (BF16, False), (BF16, True)]))
    full['w_ffn_up'], = _exchange_wait("gather_ffn_up_wait", h_up, h2)
    up = _matmul("ffn_up", h2, full['w_ffn_up'], "nn", out_dtype=BF16, tn=1408)

    tcf, trc = _pick(dff, 1408), _pick(l, 512, 8)
    assert dff % tcf == 0
    ncf = dff // tcf

    taps = [conv_w[i:i + 1] for i in range(3)]

    def conv_gate(gp, gp_prev, w0, w1, w2, bv):
        gp = gp.astype(F32)
        prev = jnp.where(pl.program_id(1) == 0, 0.0, 1.0) * gp_prev.astype(F32)
        ext = jnp.concatenate([prev, gp], axis=0)
        m1 = pltpu.roll(ext, 1, 0)[HALO:]
        m2 = pltpu.roll(ext, 2, 0)[HALO:]
        return w0 * m2 + w1 * m1 + w2 * gp + bv, m1, m2

    def convglu_fn(gp, gp_prev, val, w0, w1, w2, bv):
        gate, _, _ = conv_gate(gp, gp_prev, w0, w1, w2, bv)
        return gate * jax.nn.sigmoid(gate) * val.astype(F32)

    act, act_t = _tile_call("conv_swiglu", with_t(convglu_fn), (ncf, l // trc), [up, up, up] + taps + [conv_b],
                            [_t(trc, tcf), _prev_rows(trc, tcf), _t(trc, tcf, dff)] + [_v(tcf)] * 4,
                            [_sds((l, dff), BF16), _sds((dff, l), BF16)], [_t(trc, tcf), _tt(trc, tcf)])
    full['w_ffn_down'] = _exchange_wait("gather_ffn_down_wait", h_down, act)[0].reshape(row_sharded['w_ffn_down'])
    ffn = _matmul("ffn_down", act, full['w_ffn_down'], "nn", tm=512)

    def final_fn(x2v, fv, g2v, gv, tv):
        rows = x2v.shape[0]

        def loss_of(x2a, fa, g2a, ga):
            out = _rms(x2a + g2a * fa, ga)
            err = out - tv
            return 0.5 * _colsum(jnp.mean(err * err, axis=-1, keepdims=True))

        loss, vjp = jax.vjp(loss_of, x2v, fv, _bc(g2v, rows), _bc(gv, rows))
        dx3, dffn, dg2, dgf = vjp(jnp.ones((1, 1), F32))
        return jnp.broadcast_to(loss, (1, LANES)), dx3, dffn, _colsum(dg2), _colsum(dgf)

    loss_p, dx3, dffn, dg2, dg_fin = _tile_call(
        "loss_final_norm", final_fn, (1, nrh), [x2, ffn, g2, g_fin, tgt],
        [_t(trh, d), _t(trh, d), _v(d), _v(d), _t(trh, d)],
        [_sds((1, LANES)), _sds((l, d)), _sds((l, d), BF16), _sds((1, d)), _sds((1, d))],
        [_v(LANES), _t(trh, d), _t(trh, d), _v(d), _v(d)], acc=(0, 3, 4))
    loss = lax.psum(loss_p[0, 0], ("x", "y", "c"))

    dact = _matmul("d_act", dffn, full['w_ffn_down'], "nt", out_dtype=BF16, tn=1408, dep=loss.reshape(1, 1))
    gd, gd16, pending = {}, {}, []
    dw_down, dw_down16 = _matmul("dw_ffn_down", act_t, dffn, "nn", tm=512, also_bf16=True)
    gd['w_ffn_down'], gd16['w_ffn_down'] = [z.reshape((N_DEV,) + w_ffn_down.shape[1:]) for z in (dw_down, dw_down16)]
    handle, tok = _exchange_start("grad_ffn_down_start", [gd16['w_ffn_down']], False, loss.reshape(1, 1))
    pending.append((['w_ffn_down'], handle))
    conv_b_bwd = conv_b + tok[0:1, 0:1]

    def convglu_bwd_fn(gp, gp_prev, gp_next, val, val_next, da, da_next, w0, w1, w2, bv):
        rows = gp.shape[0]
        i = pl.program_id(1)
        gp, val, da = gp.astype(F32), val.astype(F32), da.astype(F32)
        prev = jnp.where(i == 0, 0.0, 1.0) * gp_prev.astype(F32)
        more = jnp.where(i == pl.num_programs(1) - 1, 0.0, 1.0)
        ext = jnp.concatenate([prev, gp, gp_next.astype(F32)], axis=0)
        cur = ext[HALO:]
        m1 = pltpu.roll(ext, 1, 0)[HALO:]
        m2 = pltpu.roll(ext, 2, 0)[HALO:]
        gate = w0 * m2 + w1 * m1 + w2 * cur + bv
        sg = jax.nn.sigmoid(gate)
        val_e = jnp.concatenate([val, val_next.astype(F32)], axis=0)
        da_e = jnp.concatenate([da, more * da_next.astype(F32)], axis=0)
        dgate = da_e * val_e * (sg * (1.0 + gate * (1.0 - sg)))
        p1 = pltpu.roll(dgate, rows + HALO - 1, 0)[:rows]
        p2 = pltpu.roll(dgate, rows + HALO - 2, 0)[:rows]
        dg = dgate[:rows]
        dgp = w2 * dg + w1 * p1 + w0 * p2
        dval = da * (gate[:rows] * sg[:rows])
        return (jnp.stack([dgp, dval], axis=0), _colsum(dg), _colsum(dg * m2[:rows]), _colsum(dg * m1[:rows]),
                _colsum(dg * gp))

    dup, dconv_b, dcw0, dcw1, dcw2 = _tile_call(
        "conv_swiglu_bwd", convglu_bwd_fn, (ncf, nr), [up, up, up, up, up, dact, dact] + taps + [conv_b_bwd],
        [_t(tr, tcf), _prev_rows(tr, tcf), _next_rows(tr, tcf, l), _t(tr, tcf, dff), _next_rows(tr, tcf, l, dff),
         _t(tr, tcf), _next_rows(tr, tcf, l)] + [_v(tcf)] * 4,
        [_sds((2, l, dff), BF16)] + [_sds((1, dff))] * 4, [_st(tr, tcf)] + [_v(tcf)] * 4, acc=(1, 2, 3, 4))
    dh2 = _matmul("d_h2", dup, full['w_ffn_up'], "nt", tm=512, fold=4)
    gd['w_ffn_up'], gd16['w_ffn_up'] = _matmul("dw_ffn_up", h2_t, dup, "nn", tm=512, tn=1408, out_stack=N_DEV, also_bf16=True)
    handle, tok = _exchange_start("grad_ffn_up_start", [gd16['w_ffn_up']], False, gd['w_ffn_up'])
    pending.append((['w_ffn_up'], handle))
    g_ffn_bwd = g_ffn + tok[0:1, 0:1]

    def res_norm_bwd_fn(xv, mo, g1v, gv, scv, shv, dhv, dxv):
        rows = xv.shape[0]
        _, vjp = jax.vjp(res_norm_fn, xv, mo, _bc(g1v, rows), _bc(gv, rows), _bc(scv, rows), _bc(shv, rows))
        dx, dmo, dg1v, dgv, dscv, dshv = vjp((dxv, dhv))
        return dx, dmo, _colsum(dg1v), _colsum(dgv), _colsum(dscv), _colsum(dshv)

    dx2, dmixout, dg1, dg_ffn, dsc2, dsh2 = _tile_call(
        "residual_norm_mod_ffn_bwd", res_norm_bwd_fn, (1, nrh), [xs, mixout, g1, g_ffn_bwd, sc2, sh2, dh2, dx3],
        [_t(trh, d), _t(trh, d), _v(d), _v(d), _v(d), _v(d), _t(trh, d), _t(trh, d)],
        [_sds((l, d)), _sds((l, d), BF16)] + [_sds((1, d))] * 4,
        [_t(trh, d), _t(trh, d)] + [_v(d)] * 4, acc=(2, 3, 4, 5))

    def mix_bwd_epilogue(dm, ga_, gb_, pa, ps, at):
        _, vjp = jax.vjp(_mix_fn, ga_, gb_, at, pa, ps)
        da, db, dat, dpa, dps = vjp(dm)
        return jnp.stack([da, db], axis=0), dat, dpa, dps

    dglu, dattn, dga, dgs = _matmul(
        "d_mixed_gate_mix_bwd", dmixout, full['w_out'], "nt", tn=tcd,
        epilogue=(mix_bwd_epilogue, gate_ins + [attn], [(BF16, 'pair')] + [(BF16, False)] * 3))
    dw_out, dw_out16 = _matmul("dw_out", mixed, dmixout, "tn", also_bf16=True)
    gd['w_out'], gd16['w_out'] = [z.reshape((N_DEV,) + w_out.shape[1:]) for z in (dw_out, dw_out16)]

    def gelu_bwd_epilogue(dgy, yv):
        _, vjp = jax.vjp(lambda z: jax.nn.gelu(z), yv)
        return (vjp(dgy)[0],)

    dy, = _matmul("d_gelu_y_gelu_bwd", dglu, full['w_ssm_glu'], "nt", epilogue=(gelu_bwd_epilogue, [y], [(F32, False)]))
    gd['w_ssm_glu'], gd16['w_ssm_glu'] = _matmul("dw_ssm_glu", gy, dglu, "tn", out_stack=N_DEV, also_bf16=True)
    du, dbd, dcbdt, dlam, dd_tiles = _s5_bwd(proj, off_u, dy, hst, bd, bdt, cbdt, lam, dvec, t_len)

    def gpn_of(z):
        return z.transpose(2, 0, 1, 3).reshape(pgn)

    dbb_r = gpn_of(_diag_blocks(dbd[:, :, :TILE_STATES], SSM_P))
    dbb_i = gpn_of(_diag_blocks(dbd[:, :, TILE_STATES:], SSM_P))
    dc_re = _diag_blocks(dcbdt[:, :, :TILE_STATES], SSM_P).reshape(n_groups, SSM_P, SSM_N)
    dc_im = _diag_blocks(dcbdt[:, :, TILE_STATES:], SSM_P).reshape(n_groups, SSM_P, SSM_N)
    dlam_r, dlam_i = dlam[:, 0].reshape(gn), dlam[:, 1].reshape(gn)

    def disc_bwd_fn(ar, ai, ld, br, bi, dlr, dli, dbr, dbi):
        _, vjp = jax.vjp(_s5_disc_fn, ar, ai, ld, br, bi)
        return vjp((dlr, dli, dbr, dbi))

    da_re, da_im, dlog_dt, db_re, db_im = _tile_call(
        "s5_discretise_bwd", disc_bwd_fn, (1,), disc_ins + [dlam_r, dlam_i, dbb_r, dbb_i],
        disc_specs + [_full_spec(gn), _full_spec(gn), _full_spec(pgn), _full_spec(pgn)],
        [_sds(gn), _sds(gn), _sds((n_groups, 1)), _sds(pgn), _sds(pgn)], disc_specs)

    do2 = _matmul("d_attn_heads", dattn, full['w_attn_proj'], "nt")
    gd['w_attn_proj'], gd16['w_attn_proj'] = _matmul("dw_attn_proj", o2, dattn, "tn", out_stack=N_DEV, also_bf16=True)
    handle, tok = _exchange_start("grad_mixer_start", [gd16[k] for k in mixer_w], False, gd['w_attn_proj'])
    pending.append((mixer_w, handle))
    do_h = heads(do2.astype(BF16), hq)
    dq_h, dk_h, dv_h, dsink = _attn_bwd(qh, kh, vh, sinks3 + tok[0:1, 0:1], do_h)

    def unheads(z):
        return z.transpose(1, 0, 2).reshape(l, z.shape[0] * HEAD_DIM)

    early = ['attn_sinks', 'ssm_a_re', 'ssm_a_im', 'ssm_log_dt', 'ssm_b_re', 'ssm_b_im', 'ssm_c_re', 'ssm_c_im',
             'ssm_d', 'norm_ffn_g', 'ffn_conv_b', 'final_g']
    early_grads = {
        'attn_sinks': dsink[:, 0, 0], 'ssm_a_re': da_re, 'ssm_a_im': da_im, 'ssm_log_dt': dlog_dt,
        'ssm_b_re': db_re.transpose(1, 2, 0), 'ssm_b_im': db_im.transpose(1, 2, 0), 'ssm_c_re': dc_re,
        'ssm_c_im': dc_im, 'ssm_d': dd_tiles, 'norm_ffn_g': dg_ffn, 'ffn_conv_b': dconv_b, 'final_g': dg_fin}
    ge_pack, e_offs = _pack([jnp.concatenate([dg1, dsh2, dsc2, dg2], axis=1)] + [early_grads[k] for k in early],
                            LANES, 8)
    h_early, tok = _exchange_start("gather_small_early_start", [ge_pack], True, dsink)

    dproj = jnp.concatenate([unheads(dq_h), unheads(dk_h), unheads(dv_h), du, dga, dgs], axis=1)
    dw_in, dw_in16 = _matmul("dw_in", h1_t, dproj, "nn", tm=512, tn=1280, also_bf16=True, dep=tok)
    dcw = jnp.concatenate([dcw0, dcw1, dcw2], axis=0)
    shard_in, shard_cw = w_in.shape[1:], ffn_conv_w.shape[1:]
    gd16['w_in'] = dw_in16.reshape(shard_in[0], N_DEV, shard_in[1]).transpose(1, 0, 2)
    own_in = lax.dynamic_slice_in_dim(dw_in, idx * shard_in[1], shard_in[1], axis=1)[None]
    gd['ffn_conv_w'] = dcw.reshape(shard_cw[0], N_DEV, shard_cw[1]).transpose(1, 0, 2)
    gd16['ffn_conv_w'] = gd['ffn_conv_w'].astype(BF16)
    handle, tok = _exchange_start("grad_in_start", [gd16['w_in'], gd16['ffn_conv_w']], False, dw_in)
    pending.append((['w_in', 'ffn_conv_w'], handle))
    dh1 = _matmul("d_h1", dproj, full['w_in'], "nt", tm=512, dep=tok)

    def norm_bwd_fn(xv, gv, scv, shv, dhv, dxv):
        rows = xv.shape[0]
        _, vjp = jax.vjp(_norm_mod, xv, _bc(gv, rows), _bc(scv, rows), _bc(shv, rows))
        dx, dgv, dscv, dshv = vjp(dhv)
        return dx + dxv, _colsum(dgv), _colsum(dscv), _colsum(dshv)

    grad_x, dg_mix, dsc1, dsh1 = _tile_call(
        "norm_mod_mix_bwd", norm_bwd_fn, (1, nrh), [xs, g_mix, sc1, sh1, dh1, dx2],
        [_t(trh, d), _v(d), _v(d), _v(d), _t(trh, d), _t(trh, d)],
        [_sds((l, d))] + [_sds((1, d))] * 3, [_t(trh, d)] + [_v(d)] * 3, acc=(1, 2, 3))

    gl_pack, l_offs = _pack([jnp.concatenate([dsh1, dsc1], axis=1), dg_mix], LANES, 8)
    h_late, tok = _exchange_start("gather_small_late_start", [gl_pack], True, grad_x)

    sharded = big + ['ffn_conv_w']
    sharded_out = {}

    def finish(group, handle, after):
        for k, parts in zip(group, _exchange_wait("grad_" + group[0] + "_wait", handle, after)):
            own_src, own_at = (own_in, 0 * idx) if k == 'w_in' else (gd[k], idx)
            sharded_out[k] = _adamw_sharded("adamw_" + k, parts, own_src, given[k][0], given['m_' + k][0],
                                            given['v_' + k][0], jnp.stack([idx, own_at]).astype(jnp.int32))

    for group, handle in pending[:-1]:
        finish(group, handle, tok)
    done = functools.reduce(lambda p, q: p + q, [sharded_out[k][1][0:1, 0:1] for g_, _ in pending[:-1] for k in g_])
    finish(*pending[-1], done)

    ge_all, = _exchange_wait("gather_small_early_wait", h_early, done)
    gl_all, = _exchange_wait("gather_small_late_wait", h_late, sharded_out['w_in'][1])
    gs_all = jnp.concatenate([ge_all, gl_all], axis=1)
    rows_e = ge_pack.shape[0]

    def small_pack(prefix):
        ab = given[prefix + 'ada_b']
        p_early, _ = _pack([ab[:, 2 * d:]] + [given[prefix + k] for k in early], LANES, 8)
        p_late, _ = _pack([ab[:, :2 * d], given[prefix + 'norm_mix_g']], LANES, 8)
        return jnp.concatenate([p_early, p_late], axis=0)

    small_out = _adamw("adamw_replicated", gs_all, small_pack(''), small_pack('m_'), small_pack('v_'))

    dmod_all = jnp.concatenate([_unpack(gl_all, l_offs[0], (2 * d,), lead=(N_DEV,)),
                                _unpack(ge_all, e_offs[0], (4 * d,), lead=(N_DEV,))], axis=1)
    dmod_mine = lax.dynamic_slice_in_dim(dmod_all, idx * mod_n, mod_n, axis=1)
    kpad = LANES - N_DEV
    cond_t = jnp.pad(cond_all.T, ((0, 0), (0, kpad)))
    dmod_pad = jnp.pad(dmod_mine, ((0, kpad), (0, 0)))
    g_ada_w = _matmul("dw_ada", cond_t, dmod_pad, "nn")
    ada_out = _adamw("adamw_ada_w", g_ada_w[None], ada_w[0], m_ada_w[0], v_ada_w[0])

    results = [{}, {}, {}, {}]
    for which in range(4):
        out = small_out[which]
        results[which]['ada_b'] = jnp.concatenate([_unpack(out, rows_e + l_offs[0], (1, 2 * d)),
                                                   _unpack(out, e_offs[0], (1, 4 * d))], axis=1)
        results[which]['norm_mix_g'] = _unpack(out, rows_e + l_offs[1], norm_mix_g.shape)
        for k, off in zip(early, e_offs[1:]):
            results[which][k] = _unpack(out, off, given[k].shape)
        for k in sharded:
            results[which][k] = sharded_out[k][which][None]
        results[which]['ada_w'] = ada_out[which][None]
    outs = [loss, grad_x[None]]
    for which in range(4):
        outs += [results[which][k] for k in names]
    return tuple(outs)
```

```python
import functools
import math

import jax
import jax.numpy as jnp
from jax import lax
from jax.experimental import pallas as pl
from jax.experimental.pallas import tpu as pltpu

F32, BF16 = jnp.float32, jnp.bfloat16
MESH = pl.DeviceIdType.MESH
N_DEV = 8

HEAD_DIM = 64
N_KV_HEADS = 2
ATT_BLOCK = 128
NEG_INF = -1e30
SSM_P = 16
SSM_N = 64
LANES = 128
TILE_GROUPS = LANES // SSM_P
TILE_STATES = TILE_GROUPS * SSM_N
RMS_EPS = 1e-6
ADAM_LR, ADAM_B1, ADAM_B2, ADAM_EPS, ADAM_WD, ADAM_STEP = 0.001, 0.9, 0.999, 1e-08, 0.01, 10
VMEM_LIMIT = 56 * 1024 * 1024
MATMUL_VMEM_BUDGET = 44 * 1024 * 1024


def _params(n_axes):
    return pltpu.CompilerParams(dimension_semantics=("arbitrary",) * n_axes, vmem_limit_bytes=VMEM_LIMIT)


def _pick(dim, pref, align=128):
    if dim <= align:
        return dim
    t = (min(pref, dim) // align) * align
    while t > align and dim % t:
        t -= align
    assert dim % t == 0, (dim, pref, align)
    return t


def _dev():
    return lax.axis_index("x"), lax.axis_index("y"), lax.axis_index("c")


def _tile_call(name, fn, grid, ins, in_specs, out_shapes, out_specs, acc=()):
    n_in, n_out = len(ins), len(out_shapes)
    acc_axis = len(grid) - 1

    def body(*refs):
        vals = fn(*[r[...] for r in refs[:n_in]])
        if not isinstance(vals, (tuple, list)):
            vals = (vals,)
        assert len(vals) == n_out
        for i, (r, v) in enumerate(zip(refs[n_in:], vals)):
            v = v.astype(r.dtype)
            if i in acc:
                first = pl.program_id(acc_axis) == 0

                @pl.when(first)
                def _():
                    r[...] = v

                @pl.when(jnp.logical_not(first))
                def _():
                    r[...] += v
            else:
                r[...] = v

    return pl.pallas_call(
        body, grid=grid, in_specs=in_specs, out_specs=out_specs, out_shape=out_shapes, name=name,
        compiler_params=_params(len(grid)),
    )(*ins)


def _t(tr, tc, off=0):
    return pl.BlockSpec((tr, tc), lambda j, i: (i, j + off // tc))


def _tt(tr, tc):
    return pl.BlockSpec((tc, tr), lambda j, i: (j, i))


def _v(tc, off=0, rows=1):
    return pl.BlockSpec((rows, tc), lambda j, i: (0, j + off // tc))


HALO = 16


def _prev_rows(tr, tc, off=0):
    return pl.BlockSpec((HALO, tc), lambda j, i: (jnp.maximum(i * (tr // HALO) - 1, 0), j + off // tc))


def _next_rows(tr, tc, nrows, off=0):
    return pl.BlockSpec((HALO, tc),
                        lambda j, i: (jnp.minimum((i + 1) * (tr // HALO), nrows // HALO - 1), j + off // tc))


def _st(tr, tc):
    return pl.BlockSpec((2, tr, tc), lambda j, i: (0, i, j))


def _bc(v, rows):
    return jnp.broadcast_to(v, (rows, v.shape[-1]))


def _colsum(v):
    return jnp.sum(v, axis=0, keepdims=True)


def _matmul(name, a, b, mode, out_dtype=F32, tm=1024, tn=1024, tk=None, out_stack=None, also_bf16=False, dep=None,
            fold=1, epilogue=None):
    def dims(z):
        return (z.shape[-2], z.shape[-1] * (z.shape[0] if z.ndim == 3 else 1))

    ar, ac = dims(a)
    br, bc = dims(b)
    if mode == "nn":
        m, k, n = ar, ac, bc
        assert br == k
    elif mode == "nt":
        m, k, n = ar, ac, br
        assert bc == k
    else:
        m, k, n = ac, ar, bc
        assert br == k
    m_lim, k_lim, n_lim = [m], [k], [n]
    if a.ndim == 3:
        (m_lim if mode == "tn" else k_lim).append(a.shape[-1])
    if b.ndim == 3:
        (k_lim if mode == "nt" else n_lim).append(b.shape[-1])
    if out_stack:
        n_lim.append(n // out_stack)
    tm = _pick(functools.reduce(math.gcd, m_lim), tm)
    tn = _pick(functools.reduce(math.gcd, n_lim), tn)
    k_unit = functools.reduce(math.gcd, k_lim)
    if tk is None:
        sa, sb, so = a.dtype.itemsize, b.dtype.itemsize, jnp.dtype(out_dtype).itemsize + (2 if also_bf16 else 0)
        fits = [t for t in range(LANES, k_unit + 1, LANES) if k_unit % t == 0 and
                2 * t * (tm * sa + tn * sb) + tm * tn * (2 * so + (4 if t < k else 0)) <= MATMUL_VMEM_BUDGET]
        tk = max(fits) if fits else _pick(k_unit, 512)
    else:
        tk = _pick(k_unit, tk)
    assert (k // tk) % fold == 0
    nk = k // (tk * fold)

    def spec(z, brows, bcols, ridx, cidx):
        if z.ndim == 3:
            per = z.shape[-1] // bcols
            return pl.BlockSpec((None, brows, bcols),
                                lambda i, j, kk: (cidx(i, j, kk) // per, ridx(i, j, kk), cidx(i, j, kk) % per))
        return pl.BlockSpec((brows, bcols), lambda i, j, kk: (ridx(i, j, kk), cidx(i, j, kk)))

    gi = lambda i, j, kk: i
    gj = lambda i, j, kk: j
    a_specs, b_specs = [], []
    for f in range(fold):
        gk = lambda i, j, kk, f=f: fold * kk + f
        if mode == "nn":
            a_specs.append(spec(a, tm, tk, gi, gk))
            b_specs.append(spec(b, tk, tn, gk, gj))
            dn = (((1,), (0,)), ((), ()))
        elif mode == "nt":
            a_specs.append(spec(a, tm, tk, gi, gk))
            b_specs.append(spec(b, tn, tk, gj, gk))
            dn = (((1,), (1,)), ((), ()))
        else:
            a_specs.append(spec(a, tk, tm, gk, gi))
            b_specs.append(spec(b, tk, tn, gk, gj))
            dn = (((0,), (0,)), ((), ()))

    epi_fn, epi_ins, epi_outs = epilogue if epilogue else (None, [], [])
    n_out = len(epi_outs) if epilogue else (2 if also_bf16 else 1)

    deps = [] if dep is None else [dep]

    def body(*refs):
        a_refs, b_refs = refs[:fold], refs[fold:2 * fold]
        e_refs = refs[2 * fold:2 * fold + len(epi_ins)]
        rest = refs[2 * fold + len(epi_ins) + len(deps):]
        o_refs, acc = rest[:n_out], rest[n_out:]
        part = None
        for a_ref, b_ref in zip(a_refs, b_refs):
            one = lax.dot_general(a_ref[...].astype(BF16), b_ref[...].astype(BF16), dn, preferred_element_type=F32)
            part = one if part is None else part + one

        def emit(val):
            vals = epi_fn(val, *[r[...] for r in e_refs]) if epilogue else [val] * n_out
            for o_ref, v in zip(o_refs, vals):
                o_ref[...] = v.astype(o_ref.dtype)

        if nk == 1:
            emit(part)
            return
        acc_ref, = acc
        kk = pl.program_id(2)

        @pl.when(kk == 0)
        def _():
            acc_ref[...] = part

        @pl.when(kk > 0)
        def _():
            acc_ref[...] += part

        @pl.when(kk == nk - 1)
        def _():
            emit(acc_ref[...])

    if out_stack:
        per = (n // out_stack) // tn
        out_spec = pl.BlockSpec((None, tm, tn), lambda i, j, kk: (j // per, i, j % per))
        shape = (out_stack, m, n // out_stack)
    else:
        out_spec = pl.BlockSpec((tm, tn), lambda i, j, kk: (i, j))
        shape = (m, n)
    if epilogue:
        assert not out_stack and not also_bf16
        kinds = {False: (pl.BlockSpec((tm, tn), lambda i, j, kk: (i, j)), (m, n)),
                 True: (pl.BlockSpec((tn, tm), lambda i, j, kk: (j, i)), (n, m)),
                 'pair': (pl.BlockSpec((2, tm, tn), lambda i, j, kk: (0, i, j)), (2, m, n))}
        out_specs = [kinds[t][0] for _, t in epi_outs]
        out_shapes = [jax.ShapeDtypeStruct(kinds[t][1], dt) for dt, t in epi_outs]
    else:
        out_specs = [out_spec] * n_out
        out_shapes = [jax.ShapeDtypeStruct(shape, dt) for dt in [out_dtype, BF16][:n_out]]
    e_pairs = [z if isinstance(z, tuple) else (z, 0) for z in epi_ins]
    assert all(off % tn == 0 for _, off in e_pairs)
    e_specs = [pl.BlockSpec((1, tn) if z.shape[0] == 1 else (tm, tn),
                            lambda i, j, kk, ob=off // tn, row=z.shape[0] == 1: (0 if row else i, j + ob))
               for z, off in e_pairs]
    res = pl.pallas_call(
        body, grid=(m // tm, n // tn, nk),
        in_specs=a_specs + b_specs + e_specs + [pl.BlockSpec(memory_space=pl.ANY)] * len(deps),
        out_specs=out_specs, out_shape=out_shapes,
        scratch_shapes=[pltpu.VMEM((tm, tn), F32)] if nk > 1 else [], name=name, compiler_params=_params(3),
    )(*[a] * fold, *[b] * fold, *[z for z, _ in e_pairs], *deps)
    return res if (also_bf16 or epilogue) else res[0]


def _all_gather(name, arrs, dep=None):
    n = len(arrs)
    deps = [] if dep is None else [dep]

    def body(*refs):
        ins, outs = refs[:n], refs[n + len(deps):2 * n + len(deps)]
        send_sems, recv_sems, local_sems = refs[2 * n + len(deps):]
        x, y, c = _dev()
        me, sib = (x, y, c), (x, y, 1 - c)
        x_nbr, y_nbr, diag = (1 - x, y), (x, 1 - y), (1 - x, 1 - y)
        north = c == 1
        relay_from = (jnp.where(north, 1 - x, x), jnp.where(north, y, 1 - y))
        relay_to = (jnp.where(north, x, 1 - x), jnp.where(north, 1 - y, y))

        def slot(p):
            return 4 * p[0] + 2 * p[1] + p[2]

        def copy(a, k, block, to, src=None):
            dst = outs[a].at[slot(block)]
            return pltpu.make_async_remote_copy(
                src_ref=dst if src is None else src, dst_ref=dst,
                send_sem=send_sems.at[7 * a + k], recv_sem=recv_sems.at[7 * a + k],
                device_id=to, device_id_type=MESH)

        mine = [pltpu.make_async_copy(ins[a], outs[a].at[slot(me)], local_sems.at[a]) for a in range(n)]
        for cp in mine:
            cp.start()
        sent = []
        for a in range(n):
            sent += [copy(a, 0, me, sib, src=ins[a]), copy(a, 1, me, (*x_nbr, c), src=ins[a]),
                     copy(a, 2, me, (*y_nbr, c), src=ins[a])]
        for cp in sent:
            cp.start()
        relays = [copy(a, 3, (*relay_from, c), (*relay_to, c)) for a in range(n)]
        for k, chip, relay_here in ((1, x_nbr, north), (2, y_nbr, jnp.logical_not(north)), (3, diag, None)):
            for a in range(n):
                copy(a, k, (*chip, c), me).wait_recv()
                cp = copy(a, 3 + k, (*chip, c), sib)
                cp.start()
                sent.append(cp)
                if relay_here is not None:
                    pl.when(relay_here)(relays[a].start)
        for a in range(n):
            copy(a, 0, sib, me).wait_recv()
            for k, chip in ((4, x_nbr), (5, y_nbr), (6, diag)):
                copy(a, k, (*chip, 1 - c), me).wait_recv()
        for cp in sent + relays:
            cp.wait_send()
        for cp in mine:
            cp.wait()

    any_spec = pl.BlockSpec(memory_space=pl.ANY)
    return pl.pallas_call(
        body, in_specs=[any_spec] * (n + len(deps)), out_specs=[any_spec] * n,
        out_shape=[jax.ShapeDtypeStruct((N_DEV,) + a.shape, a.dtype) for a in arrs],
        scratch_shapes=[pltpu.SemaphoreType.DMA((7 * n,)), pltpu.SemaphoreType.DMA((7 * n,)),
                        pltpu.SemaphoreType.DMA((n,))],
        name=name,
    )(*arrs, *deps)


FLIPS = [(0, 0, 1), (0, 1, 0), (1, 0, 0), (0, 1, 1), (1, 0, 1), (1, 1, 0), (1, 1, 1)]
N_PEERS = len(FLIPS)
_HBM = pl.BlockSpec(memory_space=pltpu.HBM)
_SEM = pl.BlockSpec(memory_space=pltpu.SEMAPHORE)
_EFFECT = pltpu.SideEffectType.DATAFLOW_SIDE_EFFECTING


def _flip(x, y, c, f):
    return (1 - x if f[0] else x, 1 - y if f[1] else y, 1 - c if f[2] else c)


def _slot(p):
    return 4 * p[0] + 2 * p[1] + p[2]


def _exchange_copies(src_refs, land_refs, send_sems, recv_sems, gather):
    x, y, c = _dev()
    mine = _slot((x, y, c))
    cps = []
    for a, (src, land) in enumerate(zip(src_refs, land_refs)):
        for k, f in enumerate(FLIPS):
            peer = _flip(x, y, c, f)
            cps.append(pltpu.make_async_remote_copy(
                src_ref=src if gather else src.at[_slot(peer)], dst_ref=land.at[mine],
                send_sem=send_sems.at[N_PEERS * a + k], recv_sem=recv_sems.at[N_PEERS * a + k],
                device_id=peer, device_id_type=MESH))
    return cps


def _exchange_start(name, srcs, gather, after):
    n = len(srcs)
    lands = [lax.empty(((N_DEV,) + s.shape) if gather else s.shape, s.dtype) for s in srcs]

    def body(*refs):
        src_refs, land_refs = refs[:n], refs[n:2 * n]
        send_sems, recv_sems, local_sems = refs[2 * n + 1:2 * n + 4]
        token = refs[-1]
        if gather:
            x, y, c = _dev()
            for a in range(n):
                pltpu.make_async_copy(src_refs[a], land_refs[a].at[_slot((x, y, c))], local_sems.at[a]).start()
        for cp in _exchange_copies(src_refs, land_refs, send_sems, recv_sems, gather):
            cp.start()
        token[...] = jnp.zeros_like(token)

    hbm = lambda z: pltpu.HBM(z.shape, z.dtype)
    outs = pl.pallas_call(
        body, name=name,
        out_shape=(pltpu.SemaphoreType.DMA((N_PEERS * n,)), pltpu.SemaphoreType.DMA((N_PEERS * n,)),
                   pltpu.SemaphoreType.DMA((n,)), *[hbm(s) for s in srcs], *[hbm(z) for z in lands],
                   jax.ShapeDtypeStruct((8, LANES), F32)),
        in_specs=[_HBM] * (2 * n) + [pl.BlockSpec(memory_space=pl.ANY)],
        out_specs=(_SEM, _SEM, _SEM, *[_HBM] * (2 * n), pl.BlockSpec(memory_space=pltpu.VMEM)),
        input_output_aliases={i: 3 + i for i in range(2 * n)},
        compiler_params=pltpu.CompilerParams(has_side_effects=_EFFECT),
    )(*[pltpu.with_memory_space_constraint(z, pltpu.HBM) for z in list(srcs) + lands], after)
    return (outs[:3], outs[3:3 + n], outs[3 + n:3 + 2 * n], gather), outs[-1]


def _exchange_wait(name, handles, after):
    sems, srcs, lands, gather = handles
    n = len(srcs)

    def body(*refs):
        src_refs, land_refs = refs[:n], refs[n:2 * n]
        send_sems, recv_sems, local_sems = refs[2 * n:2 * n + 3]
        if gather:
            for a in range(n):
                pltpu.make_async_copy(src_refs[a], land_refs[a].at[0], local_sems.at[a]).wait()
        for cp in _exchange_copies(src_refs, land_refs, send_sems, recv_sems, gather):
            cp.wait_send()
            cp.wait_recv()

    hbm = lambda z: pltpu.HBM(z.shape, z.dtype)
    outs = pl.pallas_call(
        body, name=name, out_shape=tuple(hbm(z) for z in list(srcs) + list(lands)),
        in_specs=[_HBM] * (2 * n) + [_SEM] * 3 + [pl.BlockSpec(memory_space=pl.ANY)],
        out_specs=tuple([_HBM] * (2 * n)), input_output_aliases={i: i for i in range(2 * n)},
        compiler_params=pltpu.CompilerParams(has_side_effects=_EFFECT),
    )(*srcs, *lands, *sems, after)
    return list(outs[n:])


def _pack_rows(sizes, width, row_align):
    offs, r = [], 0
    for s in sizes:
        offs.append(r)
        r += -(-s // width)
    total = -(-r // row_align) * row_align
    return offs, total


def _pack(items, width, row_align, lead=()):
    nl = len(lead)
    sizes = [int(jnp.size(a)) // max(1, functools.reduce(lambda p, q: p * q, lead, 1)) for a in items]
    offs, total = _pack_rows(sizes, width, row_align)
    flat = []
    used = 0
    for a, s in zip(items, sizes):
        f = a.reshape(lead + (s,))
        pad = -(-s // width) * width - s
        if pad:
            f = jnp.pad(f, [(0, 0)] * nl + [(0, pad)])
        flat.append(f)
        used += s + pad
    tail = total * width - used
    if tail:
        flat.append(jnp.zeros(lead + (tail,), items[0].dtype))
    return jnp.concatenate(flat, axis=-1).reshape(lead + (total, width)), offs


def _unpack(packed, off, shape, lead=()):
    nl = len(lead)
    size = functools.reduce(lambda p, q: p * q, shape, 1)
    width = packed.shape[-1]
    rows = -(-size // width)
    blk = lax.slice_in_dim(packed, off, off + rows, axis=nl).reshape(lead + (rows * width,))
    return lax.slice_in_dim(blk, 0, size, axis=nl).reshape(lead + tuple(shape))


def _rms(x, g):
    return (x * lax.rsqrt(jnp.mean(x * x, axis=-1, keepdims=True) + RMS_EPS)) * g


def _norm_mod(x, g, sc, sh):
    return _rms(x, g) * (1.0 + sc) + sh


def _mix_fn(glu_a, glu_b, attn, ga, gs):
    return jax.nn.sigmoid(ga) * attn + jax.nn.sigmoid(gs) * (glu_a * jax.nn.sigmoid(glu_b))


def _s5_disc_fn(a_re, a_im, log_dt, b_re, b_im):
    dt = jnp.exp(log_dt)
    mag = jnp.exp(a_re * dt)
    lr, li = mag * jnp.cos(a_im * dt), mag * jnp.sin(a_im * dt)
    den = a_re * a_re + a_im * a_im
    zr = ((lr - 1.0) * a_re + li * a_im) / den
    zi = (li * a_re - (lr - 1.0) * a_im) / den
    return lr, li, zr[None] * b_re - zi[None] * b_im, zr[None] * b_im + zi[None] * b_re


def _adamw_fn(w, g, m, v):
    m = ADAM_B1 * m + (1.0 - ADAM_B1) * g
    v = ADAM_B2 * v + (1.0 - ADAM_B2) * jnp.square(g)
    m_hat = m / (1.0 - ADAM_B1 ** ADAM_STEP)
    v_hat = v / (1.0 - ADAM_B2 ** ADAM_STEP)
    delta = -ADAM_LR * (m_hat / (jnp.sqrt(v_hat) + ADAM_EPS) + ADAM_WD * w)
    return delta, m, v


def _adamw(name, parts, w, m, v):
    p, r, c = parts.shape
    tr = _pick(r, max(8, (1 << 21) // (4 * c * max(p, 2))), 8)

    def fn(pv, wv, mv, vv):
        g = pv[0]
        for i in range(1, p):
            g = g + pv[i]
        d, m2, v2 = _adamw_fn(wv, g, mv, vv)
        return g, d, m2, v2

    spec = pl.BlockSpec((tr, c), lambda i: (i, 0))
    return _tile_call(
        name, fn, (r // tr,), [parts, w, m, v],
        [pl.BlockSpec((p, tr, c), lambda i: (0, i, 0)), spec, spec, spec],
        [jax.ShapeDtypeStruct((r, c), F32)] * 4, [spec] * 4)


def _adamw_sharded(name, parts, own_src, w, m, v, place):
    _, k, n = parts.shape
    tr = _pick(k, max(16, (1 << 19) // (4 * n)), 16)

    def body(pl_ref, p_ref, a_ref, w_ref, m_ref, v_ref, g_ref, d_ref, m2_ref, v2_ref):
        own = a_ref[0]
        g = None
        for q in range(N_DEV):
            term = jnp.where(pl_ref[0] == q, own, p_ref[q].astype(F32))
            g = term if g is None else g + term
        d, m2, v2 = _adamw_fn(w_ref[...], g, m_ref[...], v_ref[...])
        g_ref[...] = g
        d_ref[...] = d
        m2_ref[...] = m2
        v2_ref[...] = v2

    spec = pl.BlockSpec((tr, n), lambda i, pr: (i, 0))
    return pl.pallas_call(
        body,
        grid_spec=pltpu.PrefetchScalarGridSpec(
            num_scalar_prefetch=1, grid=(k // tr,),
            in_specs=[pl.BlockSpec((N_DEV, tr, n), lambda i, pr: (0, i, 0)),
                      pl.BlockSpec((1, tr, n), lambda i, pr: (pr[1], i, 0)),
                      spec, spec, spec],
            out_specs=[spec] * 4),
        out_shape=[jax.ShapeDtypeStruct((k, n), F32)] * 4, name=name, compiler_params=_params(1),
    )(place, parts, own_src, w, m, v)


def _attn_mask(n, rows):
    qi = lax.broadcasted_iota(jnp.int32, (rows, 2 * ATT_BLOCK), 0) & (ATT_BLOCK - 1)
    kj = lax.broadcasted_iota(jnp.int32, (rows, 2 * ATT_BLOCK), 1)
    rel = qi + ATT_BLOCK - kj
    return (rel >= 0) & (rel < ATT_BLOCK) & ((kj >= ATT_BLOCK) | (n > 0))


def _attn_probs(q, k, sink, mask):
    s = lax.dot_general(q, k, (((1,), (1,)), ((), ())), preferred_element_type=F32) * (HEAD_DIM ** -0.5)
    s = jnp.where(mask, s, NEG_INF)
    m = jnp.maximum(jnp.max(s, axis=-1, keepdims=True), sink)
    p = jnp.exp(s - m)
    e_sink = jnp.exp(sink - m)
    inv = 1.0 / (jnp.sum(p, axis=-1, keepdims=True) + e_sink)
    return p * inv, e_sink * inv


def _attn_specs(qpk):
    blk = ATT_BLOCK
    q_spec = pl.BlockSpec((qpk, blk, HEAD_DIM), lambda h, n: (h, n, 0))
    cur = pl.BlockSpec((1, blk, HEAD_DIM), lambda h, n: (h, n, 0))
    prev = pl.BlockSpec((1, blk, HEAD_DIM), lambda h, n: (h, jnp.maximum(n - 1, 0), 0))
    sink_spec = pl.BlockSpec((1, qpk * blk, 1), lambda h, n: (h, 0, 0))
    return q_spec, cur, prev, sink_spec


def _attn_fwd(q, k, v, sinks):
    hq, l, _ = q.shape
    qpk = hq // N_KV_HEADS
    nb = l // ATT_BLOCK
    rows = qpk * ATT_BLOCK
    q_spec, cur, prev, sink_spec = _attn_specs(qpk)

    def body(q_ref, kp_ref, kc_ref, vp_ref, vc_ref, sink_ref, o_ref):
        mask = _attn_mask(pl.program_id(1), rows)
        kk = jnp.concatenate([kp_ref[0], kc_ref[0]], axis=0).astype(BF16)
        vv = jnp.concatenate([vp_ref[0], vc_ref[0]], axis=0).astype(BF16)
        p, _ = _attn_probs(q_ref[...].reshape(rows, HEAD_DIM).astype(BF16), kk, sink_ref[0], mask)
        o = jnp.dot(p.astype(BF16), vv, preferred_element_type=F32)
        o_ref[...] = o.reshape(qpk, ATT_BLOCK, HEAD_DIM).astype(o_ref.dtype)

    return pl.pallas_call(
        body, grid=(N_KV_HEADS, nb), in_specs=[q_spec, prev, cur, prev, cur, sink_spec],
        out_specs=q_spec, out_shape=jax.ShapeDtypeStruct((hq, l, HEAD_DIM), BF16),
        name="attn_fwd", compiler_params=_params(2),
    )(q, k, k, v, v, sinks)


def _attn_bwd(q, k, v, sinks, do):
    hq, l, _ = q.shape
    qpk = hq // N_KV_HEADS
    nb = l // ATT_BLOCK
    blk = ATT_BLOCK
    rows = qpk * blk
    q_spec, cur, prev, sink_spec = _attn_specs(qpk)
    part_spec = pl.BlockSpec((1, 1, 2 * blk, HEAD_DIM), lambda h, n: (h, n, 0, 0))
    dsink_spec = pl.BlockSpec((qpk, 1, LANES), lambda h, n: (h, 0, 0))
    tn = (((0,), (0,)), ((), ()))

    def body(q_ref, do_ref, kp_ref, kc_ref, vp_ref, vc_ref, sink_ref, dq_ref, dkp_ref, dvp_ref, dsink_ref):
        n = pl.program_id(1)
        mask = _attn_mask(n, rows)
        kk = jnp.concatenate([kp_ref[0], kc_ref[0]], axis=0).astype(BF16)
        vv = jnp.concatenate([vp_ref[0], vc_ref[0]], axis=0).astype(BF16)
        qb = q_ref[...].reshape(rows, HEAD_DIM).astype(BF16)
        do32 = do_ref[...].astype(F32).reshape(rows, HEAD_DIM)
        dob = do32.astype(BF16)
        p, p_sink = _attn_probs(qb, kk, sink_ref[0], mask)
        pb = p.astype(BF16)
        o = jnp.dot(pb, vv, preferred_element_type=F32)
        delta = jnp.sum(do32 * o, axis=-1, keepdims=True)
        dp = lax.dot_general(dob, vv, (((1,), (1,)), ((), ())), preferred_element_type=F32)
        ds = (p * (dp - delta) * (HEAD_DIM ** -0.5)).astype(BF16)
        dq = jnp.dot(ds, kk, preferred_element_type=F32)
        dq_ref[...] = dq.reshape(qpk, blk, HEAD_DIM).astype(dq_ref.dtype)
        dkp_ref[0, 0] = lax.dot_general(ds, qb, tn, preferred_element_type=F32)
        dvp_ref[0, 0] = lax.dot_general(pb, dob, tn, preferred_element_type=F32)
        dsr = p_sink * delta
        for g in range(qpk):
            dsg = jnp.broadcast_to(-_colsum(dsr[g * blk:(g + 1) * blk]), (1, LANES))

            @pl.when(n == 0)
            def _():
                dsink_ref[g] = dsg

            @pl.when(n > 0)
            def _():
                dsink_ref[g] += dsg


    part_shape = jax.ShapeDtypeStruct((N_KV_HEADS, nb, 2 * blk, HEAD_DIM), F32)
    dq, dkp, dvp, dsink = pl.pallas_call(
        body, grid=(N_KV_HEADS, nb), in_specs=[q_spec, q_spec, prev, cur, prev, cur, sink_spec],
        out_specs=[q_spec, part_spec, part_spec, dsink_spec],
        out_shape=[jax.ShapeDtypeStruct((hq, l, HEAD_DIM), BF16), part_shape, part_shape,
                   jax.ShapeDtypeStruct((hq, 1, LANES), F32)],
        name="attn_bwd", compiler_params=_params(2),
    )(q, do, k, k, v, v, sinks)

    def combine(a_cur, a_nxt, b_cur, b_nxt):
        last = pl.program_id(1) == nb - 1
        keep = jnp.where(last, 0.0, 1.0)
        return (a_cur[0, 0, blk:] + keep * a_nxt[0, 0, :blk])[None], (b_cur[0, 0, blk:] + keep * b_nxt[0, 0, :blk])[None]

    nxt_spec = pl.BlockSpec((1, 1, 2 * blk, HEAD_DIM), lambda h, n: (h, jnp.minimum(n + 1, nb - 1), 0, 0))
    kv_shape = jax.ShapeDtypeStruct((N_KV_HEADS, l, HEAD_DIM), BF16)
    dk, dv = _tile_call("attn_dkv", combine, (N_KV_HEADS, nb), [dkp, dkp, dvp, dvp],
                        [part_spec, nxt_spec, part_spec, nxt_spec], [kv_shape, kv_shape], [cur, cur])
    return dq, dk, dv, dsink


def _block_diag(m):
    j, gl, a, b = m.shape
    eye = jnp.eye(gl, dtype=m.dtype)
    return (m[:, :, :, None, :] * eye[None, :, None, :, None]).reshape(j, gl * a, gl * b)


def _diag_blocks(z, a):
    j = z.shape[0]
    gl = z.shape[1] // a
    b = z.shape[2] // gl
    d = jnp.diagonal(z.reshape(j, gl, a, gl, b), axis1=1, axis2=3)
    return d.transpose(0, 3, 1, 2)


def _s5_permute(src_ref, dst_ref, t_len):
    seg = t_len // 8
    for k in range(seg):
        dst_ref[8 * k:8 * k + 8, :] = src_ref[pl.ds(k, 8, stride=seg), :]


def _s5_unpermute(perm_ref, t_len, emit):
    per_seg = t_len // 64
    for m in range(t_len // 8):
        emit(8 * m, perm_ref[pl.ds(64 * (m % per_seg) + m // per_seg, 8, stride=8), :])


def _s5_powers(p_ref, lr, li, seg):
    hs = TILE_STATES

    def step(k, carry):
        pr, pi = carry
        p_ref[pl.ds(k, 1), 0:hs] = pr
        p_ref[pl.ds(k, 1), hs:2 * hs] = pi
        return lr * pr - li * pi, lr * pi + li * pr

    lax.fori_loop(0, seg, step, (lr, li))


def _s5_local_scan(x_ref, base, lr, li, seg, reverse):
    hs = TILE_STATES
    lr8, li8 = jnp.broadcast_to(lr, (8, hs)), jnp.broadcast_to(li, (8, hs))
    if reverse:
        li8 = -li8

    def step(i, carry):
        hr, hi = carry
        k = seg - 1 - i if reverse else i
        rows = pl.ds(pl.multiple_of(base + 8 * k, 8), 8)
        nr = lr8 * hr - li8 * hi + x_ref[rows, 0:hs]
        ni = lr8 * hi + li8 * hr + x_ref[rows, hs:2 * hs]
        x_ref[rows, 0:hs] = nr
        x_ref[rows, hs:2 * hs] = ni
        return nr, ni

    zero = jnp.zeros((8, hs), F32)
    return lax.fori_loop(0, seg, step, (zero, zero), unroll=2)


def _s5_carries(c_ref, e_ref, ends, start, pw_r, pw_i, reverse):
    hs = TILE_STATES
    e_ref[:, 0:hs] = ends[0]
    e_ref[:, hs:2 * hs] = ends[1]
    cr, ci = start
    if reverse:
        pw_i = -pw_i
    for s in (range(7, -1, -1) if reverse else range(8)):
        c_ref[s:s + 1, 0:hs] = cr
        c_ref[s:s + 1, hs:2 * hs] = ci
        er, ei = e_ref[s:s + 1, 0:hs], e_ref[s:s + 1, hs:2 * hs]
        cr, ci = er + pw_r * cr - pw_i * ci, ei + pw_r * ci + pw_i * cr
    return cr, ci


def _s5_states(u_perm_b16, bd_ref, x_ref, base, c_ref, e_ref, p_ref, lr, li, h_in, t_len):
    hs = TILE_STATES
    seg = t_len // 8
    x_ref[pl.ds(base, t_len), :] = jnp.dot(u_perm_b16, bd_ref[0], preferred_element_type=F32)
    ends = _s5_local_scan(x_ref, base, lr, li, seg, False)
    pw_r, pw_i = p_ref[seg - 1:seg, 0:hs], p_ref[seg - 1:seg, hs:2 * hs]
    h_out = _s5_carries(c_ref, e_ref, ends, h_in, pw_r, pw_i, False)
    cr, ci = c_ref[:, 0:hs], c_ref[:, hs:2 * hs]

    def fix(k, carry):
        rows = pl.ds(pl.multiple_of(base + 8 * k, 8), 8)
        pr, pi = p_ref[pl.ds(k, 1), 0:hs], p_ref[pl.ds(k, 1), hs:2 * hs]
        x_ref[rows, 0:hs] += pr * cr - pi * ci
        x_ref[rows, hs:2 * hs] += pr * ci + pi * cr
        return carry

    lax.fori_loop(0, seg, fix, 0, unroll=2)
    return h_out


def _s5_fwd(proj, u_off, bd, cbd, lam, dvec, t_len):
    l = proj.shape[0]
    nj = bd.shape[0]
    nch = l // t_len
    hs = TILE_STATES
    ub = u_off // LANES
    seg = t_len // 8
    assert t_len % 64 == 0

    def body(u_ref, bd_ref, cbd_ref, lam_ref, d_ref, y_ref, hst_ref, x_ref, h_ref, p_ref, c_ref, e_ref, up_ref, yp_ref):
        lr, li = lam_ref[0, 0:1, :], lam_ref[0, 1:2, :]

        @pl.when(pl.program_id(1) == 0)
        def _():
            h_ref[...] = jnp.zeros_like(h_ref)
            _s5_powers(p_ref, lr, li, seg)

        hst_ref[0, 0] = h_ref[...]
        _s5_permute(u_ref, up_ref, t_len)
        h_out = _s5_states(up_ref[...].astype(BF16), bd_ref, x_ref, 0, c_ref, e_ref, p_ref, lr, li,
                           (h_ref[:, 0:hs], h_ref[:, hs:2 * hs]), t_len)
        h_ref[:, 0:hs] = h_out[0]
        h_ref[:, hs:2 * hs] = h_out[1]
        yp_ref[...] = jnp.dot(x_ref[...].astype(BF16), cbd_ref[0], preferred_element_type=F32)
        dv = d_ref[0]

        def out(r0, rows):
            y_ref[r0:r0 + 8, :] = rows + dv * u_ref[r0:r0 + 8, :]

        _s5_unpermute(yp_ref, t_len, out)

    return pl.pallas_call(
        body, grid=(nj, nch),
        in_specs=[pl.BlockSpec((t_len, LANES), lambda j, c: (c, ub + j)),
                  pl.BlockSpec((1, LANES, 2 * hs), lambda j, c: (j, 0, 0)),
                  pl.BlockSpec((1, 2 * hs, LANES), lambda j, c: (j, 0, 0)),
                  pl.BlockSpec((1, 2, hs), lambda j, c: (j, 0, 0)),
                  pl.BlockSpec((1, 1, LANES), lambda j, c: (j, 0, 0))],
        out_specs=[pl.BlockSpec((t_len, LANES), lambda j, c: (c, j)),
                   pl.BlockSpec((1, 1, 1, 2 * hs), lambda j, c: (j, c, 0, 0))],
        out_shape=[jax.ShapeDtypeStruct((l, nj * LANES), F32),
                   jax.ShapeDtypeStruct((nj, nch, 1, 2 * hs), F32)],
        scratch_shapes=[pltpu.VMEM((t_len, 2 * hs), F32), pltpu.VMEM((1, 2 * hs), F32),
                        pltpu.VMEM((seg, 2 * hs), F32), pltpu.VMEM((8, 2 * hs), F32), pltpu.VMEM((8, 2 * hs), F32),
                        pltpu.VMEM((t_len, LANES), F32), pltpu.VMEM((t_len, LANES), F32)],
        name="s5_fwd", compiler_params=_params(2),
    )(proj, bd, cbd, lam, dvec)


def _s5_bwd(proj, u_off, dy, hst, bd, bdt, cbdt, lam, dvec, t_len):
    l = proj.shape[0]
    nj = bd.shape[0]
    nch = l // t_len
    hs = TILE_STATES
    ub = u_off // LANES
    seg = t_len // 8
    tn = (((0,), (0,)), ((), ()))
    assert t_len % 64 == 0

    def body(u_ref, dy_ref, hst_ref, bd_ref, bdt_ref, cbdt_ref, lam_ref, d_ref,
             du_ref, dbd_ref, dcbdt_ref, dlam_ref, dd_ref,
             x_ref, g_ref, gc_ref, p_ref, c_ref, e_ref, up_ref, dyp_ref, dup_ref):
        first = pl.program_id(1) == 0
        lr, li = lam_ref[0, 0:1, :], lam_ref[0, 1:2, :]

        @pl.when(first)
        def _():
            gc_ref[...] = jnp.zeros_like(gc_ref)
            _s5_powers(p_ref, lr, li, seg)

        _s5_permute(u_ref, up_ref, t_len)
        _s5_permute(dy_ref, dyp_ref, t_len)
        ub16, dyb16 = up_ref[...].astype(BF16), dyp_ref[...].astype(BF16)
        h0 = hst_ref[0, 0]
        _s5_states(ub16, bd_ref, x_ref, 8, c_ref, e_ref, p_ref, lr, li, (h0[:, 0:hs], h0[:, hs:2 * hs]), t_len)
        x_ref[0:8, :] = c_ref[...]
        g_ref[...] = jnp.dot(dyb16, cbdt_ref[0], preferred_element_type=F32)
        starts = _s5_local_scan(g_ref, 0, lr, li, seg, True)
        pw_r, pw_i = p_ref[seg - 1:seg, 0:hs], p_ref[seg - 1:seg, hs:2 * hs]
        g_out = _s5_carries(c_ref, e_ref, starts, (gc_ref[:, 0:hs], gc_ref[:, hs:2 * hs]), pw_r, pw_i, True)
        gc_ref[:, 0:hs] = g_out[0]
        gc_ref[:, hs:2 * hs] = g_out[1]
        cr, ci = c_ref[:, 0:hs], c_ref[:, hs:2 * hs]

        def fix(k, carry):
            alr, ali = carry
            rows = pl.ds(pl.multiple_of(8 * k, 8), 8)
            pr, pi = p_ref[pl.ds(seg - 1 - k, 1), 0:hs], p_ref[pl.ds(seg - 1 - k, 1), hs:2 * hs]
            gr = g_ref[rows, 0:hs] + pr * cr + pi * ci
            gi = g_ref[rows, hs:2 * hs] + pr * ci - pi * cr
            g_ref[rows, 0:hs] = gr
            g_ref[rows, hs:2 * hs] = gi
            hpr, hpi = x_ref[rows, 0:hs], x_ref[rows, hs:2 * hs]
            return alr + gr * hpr + gi * hpi, ali + gi * hpr - gr * hpi

        zero = jnp.zeros((8, hs), F32)
        alr, ali = lax.fori_loop(0, seg, fix, (zero, zero), unroll=2)
        alr, ali = _colsum(alr), _colsum(ali)
        g = g_ref[...].astype(BF16)
        h = x_ref[pl.ds(8, t_len), :].astype(BF16)
        dup_ref[...] = jnp.dot(g, bdt_ref[0], preferred_element_type=F32)
        dv = d_ref[0]

        def out(r0, rows):
            du_ref[r0:r0 + 8, :] = (rows + dv * dy_ref[r0:r0 + 8, :]).astype(du_ref.dtype)

        _s5_unpermute(dup_ref, t_len, out)
        sign = jnp.where(lax.broadcasted_iota(jnp.int32, (1, 2 * hs), 1) < hs, 1.0, -1.0)
        dbd = lax.dot_general(ub16, g, tn, preferred_element_type=F32)
        dcbdt = lax.dot_general(dyb16, h, tn, preferred_element_type=F32) * sign
        ddv = _colsum(dy_ref[...] * u_ref[...])

        @pl.when(first)
        def _():
            dbd_ref[0] = dbd
            dcbdt_ref[0] = dcbdt
            dlam_ref[0, 0:1, :] = alr
            dlam_ref[0, 1:2, :] = ali
            dd_ref[0] = ddv

        @pl.when(jnp.logical_not(first))
        def _():
            dbd_ref[0] += dbd
            dcbdt_ref[0] += dcbdt
            dlam_ref[0, 0:1, :] += alr
            dlam_ref[0, 1:2, :] += ali
            dd_ref[0] += ddv

    rev = lambda c: nch - 1 - c
    wide = pl.BlockSpec((1, LANES, 2 * hs), lambda j, c: (j, 0, 0))
    tall = pl.BlockSpec((1, 2 * hs, LANES), lambda j, c: (j, 0, 0))
    return pl.pallas_call(
        body, grid=(nj, nch),
        in_specs=[pl.BlockSpec((t_len, LANES), lambda j, c: (rev(c), ub + j)),
                  pl.BlockSpec((t_len, LANES), lambda j, c: (rev(c), j)),
                  pl.BlockSpec((1, 1, 1, 2 * hs), lambda j, c: (j, rev(c), 0, 0)),
                  wide, tall, wide,
                  pl.BlockSpec((1, 2, hs), lambda j, c: (j, 0, 0)),
                  pl.BlockSpec((1, 1, LANES), lambda j, c: (j, 0, 0))],
        out_specs=[pl.BlockSpec((t_len, LANES), lambda j, c: (rev(c), j)),
                   wide, wide,
                   pl.BlockSpec((1, 2, hs), lambda j, c: (j, 0, 0)),
                   pl.BlockSpec((1, 1, LANES), lambda j, c: (j, 0, 0))],
        out_shape=[jax.ShapeDtypeStruct((l, nj * LANES), BF16),
                   jax.ShapeDtypeStruct((nj, LANES, 2 * hs), F32),
                   jax.ShapeDtypeStruct((nj, LANES, 2 * hs), F32),
                   jax.ShapeDtypeStruct((nj, 2, hs), F32),
                   jax.ShapeDtypeStruct((nj, 1, LANES), F32)],
        scratch_shapes=[pltpu.VMEM((t_len + 8, 2 * hs), F32), pltpu.VMEM((t_len, 2 * hs), F32),
                        pltpu.VMEM((1, 2 * hs), F32), pltpu.VMEM((seg, 2 * hs), F32),
                        pltpu.VMEM((8, 2 * hs), F32), pltpu.VMEM((8, 2 * hs), F32),
                        pltpu.VMEM((t_len, LANES), F32), pltpu.VMEM((t_len, LANES), F32),
                        pltpu.VMEM((t_len, LANES), F32)],
        name="s5_bwd", compiler_params=_params(2),
    )(proj, dy, hst, bd, bdt, cbdt, lam, dvec)


def _full_spec(shape):
    nd = len(shape)
    return pl.BlockSpec(tuple(shape), lambda i: (0,) * nd)


def _sds(shape, dtype=F32):
    return jax.ShapeDtypeStruct(tuple(shape), dtype)


def kernel(x, c, ada_w, ada_b, norm_mix_g, w_in, attn_sinks, w_attn_proj, ssm_a_re, ssm_a_im, ssm_log_dt, ssm_b_re, ssm_b_im, ssm_c_re, ssm_c_im, ssm_d, w_ssm_glu, w_out, norm_ffn_g, w_ffn_up, ffn_conv_w, ffn_conv_b, w_ffn_down, final_g, loss_target, m_ada_w, m_ada_b, m_norm_mix_g, m_w_in, m_attn_sinks, m_w_attn_proj, m_ssm_a_re, m_ssm_a_im, m_ssm_log_dt, m_ssm_b_re, m_ssm_b_im, m_ssm_c_re, m_ssm_c_im, m_ssm_d, m_w_ssm_glu, m_w_out, m_norm_ffn_g, m_w_ffn_up, m_ffn_conv_w, m_ffn_conv_b, m_w_ffn_down, m_final_g, v_ada_w, v_ada_b, v_norm_mix_g, v_w_in, v_attn_sinks, v_w_attn_proj, v_ssm_a_re, v_ssm_a_im, v_ssm_log_dt, v_ssm_b_re, v_ssm_b_im, v_ssm_c_re, v_ssm_c_im, v_ssm_d, v_w_ssm_glu, v_w_out, v_norm_ffn_g, v_w_ffn_up, v_ffn_conv_w, v_ffn_conv_b, v_w_ffn_down, v_final_g):
    given = dict(locals())
    names = ['ada_w', 'ada_b', 'norm_mix_g', 'w_in', 'attn_sinks', 'w_attn_proj', 'ssm_a_re', 'ssm_a_im',
             'ssm_log_dt', 'ssm_b_re', 'ssm_b_im', 'ssm_c_re', 'ssm_c_im', 'ssm_d', 'w_ssm_glu', 'w_out',
             'norm_ffn_g', 'w_ffn_up', 'ffn_conv_w', 'ffn_conv_b', 'w_ffn_down', 'final_g']

    xs = x[0]
    tgt = loss_target[0]
    l, d = xs.shape
    attn_w = w_attn_proj.shape[1]
    ssm_w = w_ssm_glu.shape[1]
    hq = attn_sinks.shape[1]
    qpk = hq // N_KV_HEADS
    kv_w = N_KV_HEADS * HEAD_DIM
    n_groups = ssm_a_re.shape[1]
    dff = ffn_conv_b.shape[1]
    in_w = attn_w + 2 * kv_w + ssm_w + 2 * d
    nj = ssm_w // LANES
    off_k, off_v, off_u = attn_w, attn_w + kv_w, attn_w + 2 * kv_w
    off_ga, off_gs = off_u + ssm_w, off_u + ssm_w + d
    assert hq * HEAD_DIM == attn_w and n_groups * SSM_P == ssm_w and l % ATT_BLOCK == 0

    xi, yi, ci = _dev()
    idx = 4 * xi + 2 * yi + ci

    row_sharded = {'w_out': (d, d), 'w_ffn_down': (dff, d)}
    big = ['w_in', 'w_attn_proj', 'w_ssm_glu', 'w_out', 'w_ffn_up', 'w_ffn_down']
    spack, s_offs = _pack([c, ffn_conv_w[0]], LANES, 8)
    w16 = {k: given[k][0].astype(BF16) for k in big}
    wg_in, sg = _all_gather("gather_first", [w16['w_in'], spack])
    mixer_w = ['w_attn_proj', 'w_ssm_glu', 'w_out']
    h_mixer, tok = _exchange_start("gather_mixer_start", [w16[k] for k in mixer_w], True, wg_in)
    h_up, tok = _exchange_start("gather_ffn_up_start", [w16['w_ffn_up']], True, tok)
    h_down, tok = _exchange_start("gather_ffn_down_start", [w16['w_ffn_down']], True, tok)
    full = {'w_in': wg_in.transpose(1, 0, 2).reshape(d, in_w)}
    c_all = _unpack(sg, s_offs[0], (d,), lead=(N_DEV,))
    conv_w = _unpack(sg, s_offs[1], ffn_conv_w.shape[1:], lead=(N_DEV,)).transpose(1, 0, 2).reshape(3, dff)
    conv_b = ffn_conv_b

    mod_n = ada_w.shape[2]
    tcm = _pick(mod_n, 512)
    ada_b_mine = lax.dynamic_slice_in_dim(ada_b, idx * mod_n, mod_n, axis=1)

    def modpart_fn(cv, wv, bv):
        cond = cv * jax.nn.sigmoid(cv)
        return jnp.dot(cond.astype(BF16), wv.astype(BF16), preferred_element_type=F32) + bv, cond

    modp, cond_all = _tile_call(
        "ada_rows", modpart_fn, (mod_n // tcm,), [c_all, ada_w[0], ada_b_mine],
        [pl.BlockSpec((N_DEV, d), lambda j: (0, 0)), pl.BlockSpec((d, tcm), lambda j: (0, j)),
         pl.BlockSpec((1, tcm), lambda j: (0, j))],
        [_sds((N_DEV, mod_n)), _sds((N_DEV, d))],
        [pl.BlockSpec((N_DEV, tcm), lambda j: (0, j)), pl.BlockSpec((N_DEV, d), lambda j: (0, 0))])
    (modg,) = _all_gather("gather_ada_rows", [modp])
    mod = lax.dynamic_index_in_dim(modg, idx, axis=1, keepdims=False).reshape(1, N_DEV * mod_n)
    sh1, sc1, g1, sh2, sc2, g2 = [mod[:, i * d:(i + 1) * d] for i in range(6)]

    tr = _pick(l, 256, 8)
    trh = _pick(l, 128, 8)
    nr, nrh = l // tr, l // trh
    g_mix, g_ffn, g_fin = norm_mix_g + tok[0:1, 0:1], norm_ffn_g, final_g.reshape(1, d)

    def with_t(fn):
        def wrapped(*vals):
            out = fn(*vals)
            out = out if isinstance(out, tuple) else (out,)
            return out + (out[-1].T,)
        return wrapped

    h1, h1_t = _tile_call("norm_mod_mix", with_t(_norm_mod), (1, nr), [xs, g_mix, sc1, sh1],
                          [_t(tr, d), _v(d), _v(d), _v(d)], [_sds((l, d), BF16), _sds((d, l), BF16)],
                          [_t(tr, d), _tt(tr, d)])
    proj = _matmul("proj_in", h1, full['w_in'], "nn", tn=1280)

    def heads(z, n):
        return z.reshape(l, n, HEAD_DIM).transpose(1, 0, 2)

    qh = heads(proj[:, :attn_w], hq)
    kh = heads(proj[:, off_k:off_k + kv_w], N_KV_HEADS)
    vh = heads(proj[:, off_v:off_v + kv_w], N_KV_HEADS)
    sinks3 = jnp.repeat(attn_sinks.reshape(N_KV_HEADS, qpk), ATT_BLOCK, axis=1)[..., None]
    o_h = _attn_fwd(qh, kh, vh, sinks3)
    o2 = o_h.transpose(1, 0, 2).reshape(l, attn_w)

    gn = (n_groups, SSM_N)
    pgn = (SSM_P, n_groups, SSM_N)
    a_re, a_im, log_dt = ssm_a_re[0], ssm_a_im[0], ssm_log_dt[0].reshape(n_groups, 1)
    b_re, b_im = ssm_b_re[0].transpose(2, 0, 1), ssm_b_im[0].transpose(2, 0, 1)
    disc_ins = [a_re, a_im, log_dt, b_re, b_im]
    disc_specs = [_full_spec(gn), _full_spec(gn), _full_spec((n_groups, 1)), _full_spec(pgn), _full_spec(pgn)]
    lam_r, lam_i, bb_r, bb_i = _tile_call(
        "s5_discretise", _s5_disc_fn, (1,), disc_ins, disc_specs,
        [_sds(gn), _sds(gn), _sds(pgn), _sds(pgn)],
        [_full_spec(gn), _full_spec(gn), _full_spec(pgn), _full_spec(pgn)])

    def tiles_gpn(z):
        return z.reshape(SSM_P, nj, TILE_GROUPS, SSM_N).transpose(1, 2, 0, 3)

    bd = jnp.concatenate([_block_diag(tiles_gpn(bb_r)), _block_diag(tiles_gpn(bb_i))], axis=2).astype(BF16)
    c_r = ssm_c_re[0].reshape(nj, TILE_GROUPS, SSM_P, SSM_N).transpose(0, 1, 3, 2)
    c_i = (-ssm_c_im[0]).reshape(nj, TILE_GROUPS, SSM_P, SSM_N).transpose(0, 1, 3, 2)
    cbd = jnp.concatenate([_block_diag(c_r), _block_diag(c_i)], axis=1).astype(BF16)
    bdt, cbdt = bd.transpose(0, 2, 1), cbd.transpose(0, 2, 1)
    lam = jnp.stack([lam_r.reshape(nj, TILE_STATES), lam_i.reshape(nj, TILE_STATES)], axis=1)
    dvec = ssm_d[0].reshape(nj, 1, LANES)
    t_len = _pick(l, 2048, 8)
    y, hst = _s5_fwd(proj, off_u, bd, cbd, lam, dvec, t_len)

    tcs, trg = _pick(ssm_w, 1024), _pick(l, 512, 8)
    gy = _tile_call("gelu", lambda v: jax.nn.gelu(v), (ssm_w // tcs, l // trg), [y], [_t(trg, tcs)],
                    [_sds((l, ssm_w), BF16)], [_t(trg, tcs)])[0]
    full.update(zip(mixer_w, _exchange_wait("gather_mixer_wait", h_mixer, gy)))
    full['w_out'] = full['w_out'].reshape(row_sharded['w_out'])
    full['w_attn_proj'] = full['w_attn_proj'].transpose(1, 0, 2).reshape(attn_w, d)
    full['w_ssm_glu'] = full['w_ssm_glu'].transpose(1, 0, 2).reshape(ssm_w, 2 * d)
    glu = _matmul("ssm_glu", gy, full['w_ssm_glu'], "nn")

    tcd = 256 if d % 256 == 0 and off_ga % 256 == 0 else LANES
    assert d % tcd == 0 and off_ga % tcd == 0 and off_gs % tcd == 0
    gate_ins = [(glu, 0), (glu, d), (proj, off_ga), (proj, off_gs)]

    def mix_epilogue(at, ga_, gb_, pa, ps):
        return at, _mix_fn(ga_, gb_, at, pa, ps)

    attn, mixed = _matmul("attn_proj_gate_mix", o2, full['w_attn_proj'], "nn", tn=tcd,
                          epilogue=(mix_epilogue, gate_ins, [(F32, False), (BF16, False)]))
    def res_norm_fn(xv, mo, g1v, gv, scv, shv):
        x2v = xv + g1v * mo
        return x2v, _norm_mod(x2v, gv, scv, shv)

    def res_norm_epilogue(mo, xv, g1v, gv, scv, shv):
        x2v, h2v = res_norm_fn(xv, mo, g1v, gv, scv, shv)
        return mo, x2v, h2v, h2v.T

    mixout, x2, h2, h2_t = _matmul(
        "mix_out_residual_norm_mod_ffn", mixed, full['w_out'], "nn", tm=256, tn=d,
        epilogue=(res_norm_epilogue, [xs, g1, g_ffn, sc2, sh2], [(F32, False), (F32, False), (BF16, False), (BF16, True)]))
    full['w_ffn_up'], = _exchange_wait("gather_ffn_up_wait", h_up, h2)
    up = _matmul("ffn_up", h2, full['w_ffn_up'], "nn", out_dtype=BF16, tn=1408)

    tcf, trc = _pick(dff, 1408), _pick(l, 512, 8)
    assert dff % tcf == 0
    ncf = dff // tcf

    taps = [conv_w[i:i + 1] for i in range(3)]

    def conv_gate(gp, gp_prev, w0, w1, w2, bv):
        gp = gp.astype(F32)
        prev = jnp.where(pl.program_id(1) == 0, 0.0, 1.0) * gp_prev.astype(F32)
        ext = jnp.concatenate([prev, gp], axis=0)
        m1 = pltpu.roll(ext, 1, 0)[HALO:]
        m2 = pltpu.roll(ext, 2, 0)[HALO:]
        return w0 * m2 + w1 * m1 + w2 * gp + bv, m1, m2

    def convglu_fn(gp, gp_prev, val, w0, w1, w2, bv):
        gate, _, _ = conv_gate(gp, gp_prev, w0, w1, w2, bv)
        return gate * jax.nn.sigmoid(gate) * val.astype(F32)

    act, act_t = _tile_call("conv_swiglu", with_t(convglu_fn), (ncf, l // trc), [up, up, up] + taps + [conv_b],
                            [_t(trc, tcf), _prev_rows(trc, tcf), _t(trc, tcf, dff)] + [_v(tcf)] * 4,
                            [_sds((l, dff), BF16), _sds((dff, l), BF16)], [_t(trc, tcf), _tt(trc, tcf)])
    full['w_ffn_down'] = _exchange_wait("gather_ffn_down_wait", h_down, act)[0].reshape(row_sharded['w_ffn_down'])
    ffn = _matmul("ffn_down", act, full['w_ffn_down'], "nn", tm=512)

    def final_fn(x2v, fv, g2v, gv, tv):
        rows = x2v.shape[0]

        def loss_of(x2a, fa, g2a, ga):
            out = _rms(x2a + g2a * fa, ga)
            err = out - tv
            return 0.5 * _colsum(jnp.mean(err * err, axis=-1, keepdims=True))

        loss, vjp = jax.vjp(loss_of, x2v, fv, _bc(g2v, rows), _bc(gv, rows))
        dx3, dffn, dg2, dgf = vjp(jnp.ones((1, 1), F32))
        return jnp.broadcast_to(loss, (1, LANES)), dx3, dffn, _colsum(dg2), _colsum(dgf)

    loss_p, dx3, dffn, dg2, dg_fin = _tile_call(
        "loss_final_norm", final_fn, (1, nrh), [x2, ffn, g2, g_fin, tgt],
        [_t(trh, d), _t(trh, d), _v(d), _v(d), _t(trh, d)],
        [_sds((1, LANES)), _sds((l, d)), _sds((l, d), BF16), _sds((1, d)), _sds((1, d))],
        [_v(LANES), _t(trh, d), _t(trh, d), _v(d), _v(d)], acc=(0, 3, 4))
    loss = lax.psum(loss_p[0, 0], ("x", "y", "c"))

    dact = _matmul("d_act", dffn, full['w_ffn_down'], "nt", out_dtype=BF16, tn=1408, dep=loss.reshape(1, 1))
    gd, gd16, pending = {}, {}, []
    dw_down, dw_down16 = _matmul("dw_ffn_down", act_t, dffn, "nn", tm=512, also_bf16=True)
    gd['w_ffn_down'], gd16['w_ffn_down'] = [z.reshape((N_DEV,) + w_ffn_down.shape[1:]) for z in (dw_down, dw_down16)]
    handle, tok = _exchange_start("grad_ffn_down_start", [gd16['w_ffn_down']], False, loss.reshape(1, 1))
    pending.append((['w_ffn_down'], handle))
    conv_b_bwd = conv_b + tok[0:1, 0:1]

    def convglu_bwd_fn(gp, gp_prev, gp_next, val, val_next, da, da_next, w0, w1, w2, bv):
        rows = gp.shape[0]
        i = pl.program_id(1)
        gp, val, da = gp.astype(F32), val.astype(F32), da.astype(F32)
        prev = jnp.where(i == 0, 0.0, 1.0) * gp_prev.astype(F32)
        more = jnp.where(i == pl.num_programs(1) - 1, 0.0, 1.0)
        ext = jnp.concatenate([prev, gp, gp_next.astype(F32)], axis=0)
        cur = ext[HALO:]
        m1 = pltpu.roll(ext, 1, 0)[HALO:]
        m2 = pltpu.roll(ext, 2, 0)[HALO:]
        gate = w0 * m2 + w1 * m1 + w2 * cur + bv
        sg = jax.nn.sigmoid(gate)
        val_e = jnp.concatenate([val, val_next.astype(F32)], axis=0)
        da_e = jnp.concatenate([da, more * da_next.astype(F32)], axis=0)
        dgate = da_e * val_e * (sg * (1.0 + gate * (1.0 - sg)))
        p1 = pltpu.roll(dgate, rows + HALO - 1, 0)[:rows]
        p2 = pltpu.roll(dgate, rows + HALO - 2, 0)[:rows]
        dg = dgate[:rows]
        dgp = w2 * dg + w1 * p1 + w0 * p2
        dval = da * (gate[:rows] * sg[:rows])
        return (jnp.stack([dgp, dval], axis=0), _colsum(dg), _colsum(dg * m2[:rows]), _colsum(dg * m1[:rows]),
                _colsum(dg * gp))

    dup, dconv_b, dcw0, dcw1, dcw2 = _tile_call(
        "conv_swiglu_bwd", convglu_bwd_fn, (ncf, nr), [up, up, up, up, up, dact, dact] + taps + [conv_b_bwd],
        [_t(tr, tcf), _prev_rows(tr, tcf), _next_rows(tr, tcf, l), _t(tr, tcf, dff), _next_rows(tr, tcf, l, dff),
         _t(tr, tcf), _next_rows(tr, tcf, l)] + [_v(tcf)] * 4,
        [_sds((2, l, dff), BF16)] + [_sds((1, dff))] * 4, [_st(tr, tcf)] + [_v(tcf)] * 4, acc=(1, 2, 3, 4))
    dh2 = _matmul("d_h2", dup, full['w_ffn_up'], "nt", tm=512, fold=4)
    gd['w_ffn_up'], gd16['w_ffn_up'] = _matmul("dw_ffn_up", h2_t, dup, "nn", tm=512, tn=1408, out_stack=N_DEV, also_bf16=True)
    handle, tok = _exchange_start("grad_ffn_up_start", [gd16['w_ffn_up']], False, gd['w_ffn_up'])
    pending.append((['w_ffn_up'], handle))
    g_ffn_bwd = g_ffn + tok[0:1, 0:1]

    def res_norm_bwd_fn(xv, mo, g1v, gv, scv, shv, dhv, dxv):
        rows = xv.shape[0]
        _, vjp = jax.vjp(res_norm_fn, xv, mo, _bc(g1v, rows), _bc(gv, rows), _bc(scv, rows), _bc(shv, rows))
        dx, dmo, dg1v, dgv, dscv, dshv = vjp((dxv, dhv))
        return dx, dmo, _colsum(dg1v), _colsum(dgv), _colsum(dscv), _colsum(dshv)

    dx2, dmixout, dg1, dg_ffn, dsc2, dsh2 = _tile_call(
        "residual_norm_mod_ffn_bwd", res_norm_bwd_fn, (1, nrh), [xs, mixout, g1, g_ffn_bwd, sc2, sh2, dh2, dx3],
        [_t(trh, d), _t(trh, d), _v(d), _v(d), _v(d), _v(d), _t(trh, d), _t(trh, d)],
        [_sds((l, d)), _sds((l, d), BF16)] + [_sds((1, d))] * 4,
        [_t(trh, d), _t(trh, d)] + [_v(d)] * 4, acc=(2, 3, 4, 5))

    def mix_bwd_epilogue(dm, ga_, gb_, pa, ps, at):
        _, vjp = jax.vjp(_mix_fn, ga_, gb_, at, pa, ps)
        da, db, dat, dpa, dps = vjp(dm)
        return jnp.stack([da, db], axis=0), dat, dpa, dps

    dglu, dattn, dga, dgs = _matmul(
        "d_mixed_gate_mix_bwd", dmixout, full['w_out'], "nt", tn=tcd,
        epilogue=(mix_bwd_epilogue, gate_ins + [attn], [(BF16, 'pair')] + [(BF16, False)] * 3))
    dw_out, dw_out16 = _matmul("dw_out", mixed, dmixout, "tn", also_bf16=True)
    gd['w_out'], gd16['w_out'] = [z.reshape((N_DEV,) + w_out.shape[1:]) for z in (dw_out, dw_out16)]

    def gelu_bwd_epilogue(dgy, yv):
        _, vjp = jax.vjp(lambda z: jax.nn.gelu(z), yv)
        return (vjp(dgy)[0],)

    dy, = _matmul("d_gelu_y_gelu_bwd", dglu, full['w_ssm_glu'], "nt", epilogue=(gelu_bwd_epilogue, [y], [(F32, False)]))
    gd['w_ssm_glu'], gd16['w_ssm_glu'] = _matmul("dw_ssm_glu", gy, dglu, "tn", out_stack=N_DEV, also_bf16=True)
    du, dbd, dcbdt, dlam, dd_tiles = _s5_bwd(proj, off_u, dy, hst, bd, bdt, cbdt, lam, dvec, t_len)

    def gpn_of(z):
        return z.transpose(2, 0, 1, 3).reshape(pgn)

    dbb_r = gpn_of(_diag_blocks(dbd[:, :, :TILE_STATES], SSM_P))
    dbb_i = gpn_of(_diag_blocks(dbd[:, :, TILE_STATES:], SSM_P))
    dc_re = _diag_blocks(dcbdt[:, :, :TILE_STATES], SSM_P).reshape(n_groups, SSM_P, SSM_N)
    dc_im = _diag_blocks(dcbdt[:, :, TILE_STATES:], SSM_P).reshape(n_groups, SSM_P, SSM_N)
    dlam_r, dlam_i = dlam[:, 0].reshape(gn), dlam[:, 1].reshape(gn)

    def disc_bwd_fn(ar, ai, ld, br, bi, dlr, dli, dbr, dbi):
        _, vjp = jax.vjp(_s5_disc_fn, ar, ai, ld, br, bi)
        return vjp((dlr, dli, dbr, dbi))

    da_re, da_im, dlog_dt, db_re, db_im = _tile_call(
        "s5_discretise_bwd", disc_bwd_fn, (1,), disc_ins + [dlam_r, dlam_i, dbb_r, dbb_i],
        disc_specs + [_full_spec(gn), _full_spec(gn), _full_spec(pgn), _full_spec(pgn)],
        [_sds(gn), _sds(gn), _sds((n_groups, 1)), _sds(pgn), _sds(pgn)], disc_specs)

    do2 = _matmul("d_attn_heads", dattn, full['w_attn_proj'], "nt")
    gd['w_attn_proj'], gd16['w_attn_proj'] = _matmul("dw_attn_proj", o2, dattn, "tn", out_stack=N_DEV, also_bf16=True)
    handle, tok = _exchange_start("grad_mixer_start", [gd16[k] for k in mixer_w], False, gd['w_attn_proj'])
    pending.append((mixer_w, handle))
    do_h = heads(do2.astype(BF16), hq)
    dq_h, dk_h, dv_h, dsink = _attn_bwd(qh, kh, vh, sinks3 + tok[0:1, 0:1], do_h)

    def unheads(z):
        return z.transpose(1, 0, 2).reshape(l, z.shape[0] * HEAD_DIM)

    early = ['attn_sinks', 'ssm_a_re', 'ssm_a_im', 'ssm_log_dt', 'ssm_b_re', 'ssm_b_im', 'ssm_c_re', 'ssm_c_im',
             'ssm_d', 'norm_ffn_g', 'ffn_conv_b', 'final_g']
    early_grads = {
        'attn_sinks': dsink[:, 0, 0], 'ssm_a_re': da_re, 'ssm_a_im': da_im, 'ssm_log_dt': dlog_dt,
        'ssm_b_re': db_re.transpose(1, 2, 0), 'ssm_b_im': db_im.transpose(1, 2, 0), 'ssm_c_re': dc_re,
        'ssm_c_im': dc_im, 'ssm_d': dd_tiles, 'norm_ffn_g': dg_ffn, 'ffn_conv_b': dconv_b, 'final_g': dg_fin}
    ge_pack, e_offs = _pack([jnp.concatenate([dg1, dsh2, dsc2, dg2], axis=1)] + [early_grads[k] for k in early],
                            LANES, 8)
    h_early, tok = _exchange_start("gather_small_early_start", [ge_pack], True, dsink)

    dproj = jnp.concatenate([unheads(dq_h), unheads(dk_h), unheads(dv_h), du, dga, dgs], axis=1)
    dw_in, dw_in16 = _matmul("dw_in", h1_t, dproj, "nn", tm=512, tn=1280, also_bf16=True, dep=tok)
    dcw = jnp.concatenate([dcw0, dcw1, dcw2], axis=0)
    shard_in, shard_cw = w_in.shape[1:], ffn_conv_w.shape[1:]
    gd16['w_in'] = dw_in16.reshape(shard_in[0], N_DEV, shard_in[1]).transpose(1, 0, 2)
    own_in = lax.dynamic_slice_in_dim(dw_in, idx * shard_in[1], shard_in[1], axis=1)[None]
    gd['ffn_conv_w'] = dcw.reshape(shard_cw[0], N_DEV, shard_cw[1]).transpose(1, 0, 2)
    gd16['ffn_conv_w'] = gd['ffn_conv_w'].astype(BF16)
    handle, tok = _exchange_start("grad_in_start", [gd16['w_in'], gd16['ffn_conv_w']], False, dw_in)
    pending.append((['w_in', 'ffn_conv_w'], handle))
    dh1 = _matmul("d_h1", dproj, full['w_in'], "nt", tm=512, dep=tok)

    def norm_bwd_fn(xv, gv, scv, shv, dhv, dxv):
        rows = xv.shape[0]
        _, vjp = jax.vjp(_norm_mod, xv, _bc(gv, rows), _bc(scv, rows), _bc(shv, rows))
        dx, dgv, dscv, dshv = vjp(dhv)
        return dx + dxv, _colsum(dgv), _colsum(dscv), _colsum(dshv)

    grad_x, dg_mix, dsc1, dsh1 = _tile_call(
        "norm_mod_mix_bwd", norm_bwd_fn, (1, nrh), [xs, g_mix, sc1, sh1, dh1, dx2],
        [_t(trh, d), _v(d), _v(d), _v(d), _t(trh, d), _t(trh, d)],
        [_sds((l, d))] + [_sds((1, d))] * 3, [_t(trh, d)] + [_v(d)] * 3, acc=(1, 2, 3))

    gl_pack, l_offs = _pack([jnp.concatenate([dsh1, dsc1], axis=1), dg_mix], LANES, 8)
    h_late, tok = _exchange_start("gather_small_late_start", [gl_pack], True, grad_x)

    sharded = big + ['ffn_conv_w']
    sharded_out = {}

    def finish(group, handle, after):
        for k, parts in zip(group, _exchange_wait("grad_" + group[0] + "_wait", handle, after)):
            own_src, own_at = (own_in, 0 * idx) if k == 'w_in' else (gd[k], idx)
            sharded_out[k] = _adamw_sharded("adamw_" + k, parts, own_src, given[k][0], given['m_' + k][0],
                                            given['v_' + k][0], jnp.stack([idx, own_at]).astype(jnp.int32))

    for group, handle in pending[:-1]:
        finish(group, handle, tok)
    done = functools.reduce(lambda p, q: p + q, [sharded_out[k][1][0:1, 0:1] for g_, _ in pending[:-1] for k in g_])
    finish(*pending[-1], done)

    ge_all, = _exchange_wait("gather_small_early_wait", h_early, done)
    gl_all, = _exchange_wait("gather_small_late_wait", h_late, sharded_out['w_in'][1])
    gs_all = jnp.concatenate([ge_all, gl_all], axis=1)
    rows_e = ge_pack.shape[0]

    def small_pack(prefix):
        ab = given[prefix + 'ada_b']
        p_early, _ = _pack([ab[:, 2 * d:]] + [given[prefix + k] for k in early], LANES, 8)
        p_late, _ = _pack([ab[:, :2 * d], given[prefix + 'norm_mix_g']], LANES, 8)
        return jnp.concatenate([p_early, p_late], axis=0)

    small_out = _adamw("adamw_replicated", gs_all, small_pack(''), small_pack('m_'), small_pack('v_'))

    dmod_all = jnp.concatenate([_unpack(gl_all, l_offs[0], (2 * d,), lead=(N_DEV,)),
                                _unpack(ge_all, e_offs[0], (4 * d,), lead=(N_DEV,))], axis=1)
    dmod_mine = lax.dynamic_slice_in_dim(dmod_all, idx * mod_n, mod_n, axis=1)
    kpad = LANES - N_DEV
    cond_t = jnp.pad(cond_all.T, ((0, 0), (0, kpad)))
    dmod_pad = jnp.pad(dmod_mine, ((0, kpad), (0, 0)))
    g_ada_w = _matmul("dw_ada", cond_t, dmod_pad, "nn")
    ada_out = _adamw("adamw_ada_w", g_ada_w[None], ada_w[0], m_ada_w[0], v_ada_w[0])

    results = [{}, {}, {}, {}]
    for which in range(4):
        out = small_out[which]
        results[which]['ada_b'] = jnp.concatenate([_unpack(out, rows_e + l_offs[0], (1, 2 * d)),
                                                   _unpack(out, e_offs[0], (1, 4 * d))], axis=1)
        results[which]['norm_mix_g'] = _unpack(out, rows_e + l_offs[1], norm_mix_g.shape)
        for k, off in zip(early, e_offs[1:]):
            results[which][k] = _unpack(out, off, given[k].shape)
        for k in sharded:
            results[which][k] = sharded_out[k][which][None]
        results[which]['ada_w'] = ada_out[which][None]
    outs = [loss, grad_x[None]]
    for which in range(4):
        outs += [results[which][k] for k in names]
    return tuple(outs)
```

```python
import functools
import math

import jax
import jax.numpy as jnp
from jax import lax
from jax.experimental import pallas as pl
from jax.experimental.pallas import tpu as pltpu

F32, BF16 = jnp.float32, jnp.bfloat16
MESH = pl.DeviceIdType.MESH
N_DEV = 8

HEAD_DIM = 64
N_KV_HEADS = 2
ATT_BLOCK = 128
NEG_INF = -1e30
SSM_P = 16
SSM_N = 64
LANES = 128
TILE_GROUPS = LANES // SSM_P
TILE_STATES = TILE_GROUPS * SSM_N
RMS_EPS = 1e-6
ADAM_LR, ADAM_B1, ADAM_B2, ADAM_EPS, ADAM_WD, ADAM_STEP = 0.001, 0.9, 0.999, 1e-08, 0.01, 10
VMEM_LIMIT = 56 * 1024 * 1024
MATMUL_VMEM_BUDGET = 44 * 1024 * 1024


def _params(n_axes):
    return pltpu.CompilerParams(dimension_semantics=("arbitrary",) * n_axes, vmem_limit_bytes=VMEM_LIMIT)


def _pick(dim, pref, align=128):
    if dim <= align:
        return dim
    t = (min(pref, dim) // align) * align
    while t > align and dim % t:
        t -= align
    assert dim % t == 0, (dim, pref, align)
    return t


def _dev():
    return lax.axis_index("x"), lax.axis_index("y"), lax.axis_index("c")


def _tile_call(name, fn, grid, ins, in_specs, out_shapes, out_specs, acc=()):
    n_in, n_out = len(ins), len(out_shapes)
    acc_axis = len(grid) - 1

    def body(*refs):
        vals = fn(*[r[...] for r in refs[:n_in]])
        if not isinstance(vals, (tuple, list)):
            vals = (vals,)
        assert len(vals) == n_out
        for i, (r, v) in enumerate(zip(refs[n_in:], vals)):
            v = v.astype(r.dtype)
            if i in acc:
                first = pl.program_id(acc_axis) == 0

                @pl.when(first)
                def _():
                    r[...] = v

                @pl.when(jnp.logical_not(first))
                def _():
                    r[...] += v
            else:
                r[...] = v

    return pl.pallas_call(
        body, grid=grid, in_specs=in_specs, out_specs=out_specs, out_shape=out_shapes, name=name,
        compiler_params=_params(len(grid)),
    )(*ins)


def _t(tr, tc, off=0):
    return pl.BlockSpec((tr, tc), lambda j, i: (i, j + off // tc))


def _tt(tr, tc):
    return pl.BlockSpec((tc, tr), lambda j, i: (j, i))


def _v(tc, off=0, rows=1):
    return pl.BlockSpec((rows, tc), lambda j, i: (0, j + off // tc))


HALO = 16


def _prev_rows(tr, tc, off=0):
    return pl.BlockSpec((HALO, tc), lambda j, i: (jnp.maximum(i * (tr // HALO) - 1, 0), j + off // tc))


def _next_rows(tr, tc, nrows, off=0):
    return pl.BlockSpec((HALO, tc),
                        lambda j, i: (jnp.minimum((i + 1) * (tr // HALO), nrows // HALO - 1), j + off // tc))


def _st(tr, tc):
    return pl.BlockSpec((2, tr, tc), lambda j, i: (0, i, j))


def _bc(v, rows):
    return jnp.broadcast_to(v, (rows, v.shape[-1]))


def _colsum(v):
    return jnp.sum(v, axis=0, keepdims=True)


def _matmul(name, a, b, mode, out_dtype=F32, tm=1024, tn=1024, tk=None, out_stack=None, also_bf16=False, dep=None,
            fold=1, epilogue=None):
    def dims(z):
        return (z.shape[-2], z.shape[-1] * (z.shape[0] if z.ndim == 3 else 1))

    ar, ac = dims(a)
    br, bc = dims(b)
    if mode == "nn":
        m, k, n = ar, ac, bc
        assert br == k
    elif mode == "nt":
        m, k, n = ar, ac, br
        assert bc == k
    else:
        m, k, n = ac, ar, bc
        assert br == k
    m_lim, k_lim, n_lim = [m], [k], [n]
    if a.ndim == 3:
        (m_lim if mode == "tn" else k_lim).append(a.shape[-1])
    if b.ndim == 3:
        (k_lim if mode == "nt" else n_lim).append(b.shape[-1])
    if out_stack:
        n_lim.append(n // out_stack)
    tm = _pick(functools.reduce(math.gcd, m_lim), tm)
    tn = _pick(functools.reduce(math.gcd, n_lim), tn)
    k_unit = functools.reduce(math.gcd, k_lim)
    if tk is None:
        sa, sb, so = a.dtype.itemsize, b.dtype.itemsize, jnp.dtype(out_dtype).itemsize + (2 if also_bf16 else 0)
        fits = [t for t in range(LANES, k_unit + 1, LANES) if k_unit % t == 0 and
                2 * t * (tm * sa + tn * sb) + tm * tn * (2 * so + (4 if t < k else 0)) <= MATMUL_VMEM_BUDGET]
        tk = max(fits) if fits else _pick(k_unit, 512)
    else:
        tk = _pick(k_unit, tk)
    assert (k // tk) % fold == 0
    nk = k // (tk * fold)

    def spec(z, brows, bcols, ridx, cidx):
        if z.ndim == 3:
            per = z.shape[-1] // bcols
            return pl.BlockSpec((None, brows, bcols),
                                lambda i, j, kk: (cidx(i, j, kk) // per, ridx(i, j, kk), cidx(i, j, kk) % per))
        return pl.BlockSpec((brows, bcols), lambda i, j, kk: (ridx(i, j, kk), cidx(i, j, kk)))

    gi = lambda i, j, kk: i
    gj = lambda i, j, kk: j
    a_specs, b_specs = [], []
    for f in range(fold):
        gk = lambda i, j, kk, f=f: fold * kk + f
        if mode == "nn":
            a_specs.append(spec(a, tm, tk, gi, gk))
            b_specs.append(spec(b, tk, tn, gk, gj))
            dn = (((1,), (0,)), ((), ()))
        elif mode == "nt":
            a_specs.append(spec(a, tm, tk, gi, gk))
            b_specs.append(spec(b, tn, tk, gj, gk))
            dn = (((1,), (1,)), ((), ()))
        else:
            a_specs.append(spec(a, tk, tm, gk, gi))
            b_specs.append(spec(b, tk, tn, gk, gj))
            dn = (((0,), (0,)), ((), ()))

    epi_fn, epi_ins, epi_outs = epilogue if epilogue else (None, [], [])
    n_out = len(epi_outs) if epilogue else (2 if also_bf16 else 1)

    deps = [] if dep is None else [dep]

    def body(*refs):
        a_refs, b_refs = refs[:fold], refs[fold:2 * fold]
        e_refs = refs[2 * fold:2 * fold + len(epi_ins)]
        rest = refs[2 * fold + len(epi_ins) + len(deps):]
        o_refs, acc = rest[:n_out], rest[n_out:]
        part = None
        for a_ref, b_ref in zip(a_refs, b_refs):
            one = lax.dot_general(a_ref[...].astype(BF16), b_ref[...].astype(BF16), dn, preferred_element_type=F32)
            part = one if part is None else part + one

        def emit(val):
            vals = epi_fn(val, *[r[...] for r in e_refs]) if epilogue else [val] * n_out
            for o_ref, v in zip(o_refs, vals):
                o_ref[...] = v.astype(o_ref.dtype)

        if nk == 1:
            emit(part)
            return
        acc_ref, = acc
        kk = pl.program_id(2)

        @pl.when(kk == 0)
        def _():
            acc_ref[...] = part

        @pl.when(kk > 0)
        def _():
            acc_ref[...] += part

        @pl.when(kk == nk - 1)
        def _():
            emit(acc_ref[...])

    if out_stack:
        per = (n // out_stack) // tn
        out_spec = pl.BlockSpec((None, tm, tn), lambda i, j, kk: (j // per, i, j % per))
        shape = (out_stack, m, n // out_stack)
    else:
        out_spec = pl.BlockSpec((tm, tn), lambda i, j, kk: (i, j))
        shape = (m, n)
    if epilogue:
        assert not out_stack and not also_bf16
        kinds = {False: (pl.BlockSpec((tm, tn), lambda i, j, kk: (i, j)), (m, n)),
                 True: (pl.BlockSpec((tn, tm), lambda i, j, kk: (j, i)), (n, m)),
                 'pair': (pl.BlockSpec((2, tm, tn), lambda i, j, kk: (0, i, j)), (2, m, n))}
        out_specs = [kinds[t][0] for _, t in epi_outs]
        out_shapes = [jax.ShapeDtypeStruct(kinds[t][1], dt) for dt, t in epi_outs]
    else:
        out_specs = [out_spec] * n_out
        out_shapes = [jax.ShapeDtypeStruct(shape, dt) for dt in [out_dtype, BF16][:n_out]]
    e_pairs = [z if isinstance(z, tuple) else (z, 0) for z in epi_ins]
    assert all(off % tn == 0 for _, off in e_pairs)
    e_specs = [pl.BlockSpec((1, tn) if z.shape[0] == 1 else (tm, tn),
                            lambda i, j, kk, ob=off // tn, row=z.shape[0] == 1: (0 if row else i, j + ob))
               for z, off in e_pairs]
    res = pl.pallas_call(
        body, grid=(m // tm, n // tn, nk),
        in_specs=a_specs + b_specs + e_specs + [pl.BlockSpec(memory_space=pl.ANY)] * len(deps),
        out_specs=out_specs, out_shape=out_shapes,
        scratch_shapes=[pltpu.VMEM((tm, tn), F32)] if nk > 1 else [], name=name, compiler_params=_params(3),
    )(*[a] * fold, *[b] * fold, *[z for z, _ in e_pairs], *deps)
    return res if (also_bf16 or epilogue) else res[0]


def _all_gather(name, arrs, dep=None):
    n = len(arrs)
    deps = [] if dep is None else [dep]

    def body(*refs):
        ins, outs = refs[:n], refs[n + len(deps):2 * n + len(deps)]
        send_sems, recv_sems, local_sems = refs[2 * n + len(deps):]
        x, y, c = _dev()
        me, sib = (x, y, c), (x, y, 1 - c)
        x_nbr, y_nbr, diag = (1 - x, y), (x, 1 - y), (1 - x, 1 - y)
        north = c == 1
        relay_from = (jnp.where(north, 1 - x, x), jnp.where(north, y, 1 - y))
        relay_to = (jnp.where(north, x, 1 - x), jnp.where(north, 1 - y, y))

        def slot(p):
            return 4 * p[0] + 2 * p[1] + p[2]

        def copy(a, k, block, to, src=None):
            dst = outs[a].at[slot(block)]
            return pltpu.make_async_remote_copy(
                src_ref=dst if src is None else src, dst_ref=dst,
                send_sem=send_sems.at[7 * a + k], recv_sem=recv_sems.at[7 * a + k],
                device_id=to, device_id_type=MESH)

        mine = [pltpu.make_async_copy(ins[a], outs[a].at[slot(me)], local_sems.at[a]) for a in range(n)]
        for cp in mine:
            cp.start()
        sent = []
        for a in range(n):
            sent += [copy(a, 0, me, sib, src=ins[a]), copy(a, 1, me, (*x_nbr, c), src=ins[a]),
                     copy(a, 2, me, (*y_nbr, c), src=ins[a])]
        for cp in sent:
            cp.start()
        relays = [copy(a, 3, (*relay_from, c), (*relay_to, c)) for a in range(n)]
        for k, chip, relay_here in ((1, x_nbr, north), (2, y_nbr, jnp.logical_not(north)), (3, diag, None)):
            for a in range(n):
                copy(a, k, (*chip, c), me).wait_recv()
                cp = copy(a, 3 + k, (*chip, c), sib)
                cp.start()
                sent.append(cp)
                if relay_here is not None:
                    pl.when(relay_here)(relays[a].start)
        for a in range(n):
            copy(a, 0, sib, me).wait_recv()
            for k, chip in ((4, x_nbr), (5, y_nbr), (6, diag)):
                copy(a, k, (*chip, 1 - c), me).wait_recv()
        for cp in sent + relays:
            cp.wait_send()
        for cp in mine:
            cp.wait()

    any_spec = pl.BlockSpec(memory_space=pl.ANY)
    return pl.pallas_call(
        body, in_specs=[any_spec] * (n + len(deps)), out_specs=[any_spec] * n,
        out_shape=[jax.ShapeDtypeStruct((N_DEV,) + a.shape, a.dtype) for a in arrs],
        scratch_shapes=[pltpu.SemaphoreType.DMA((7 * n,)), pltpu.SemaphoreType.DMA((7 * n,)),
                        pltpu.SemaphoreType.DMA((n,))],
        name=name,
    )(*arrs, *deps)


FLIPS = [(0, 0, 1), (0, 1, 0), (1, 0, 0), (0, 1, 1), (1, 0, 1), (1, 1, 0), (1, 1, 1)]
N_PEERS = len(FLIPS)
_HBM = pl.BlockSpec(memory_space=pltpu.HBM)
_SEM = pl.BlockSpec(memory_space=pltpu.SEMAPHORE)
_EFFECT = pltpu.SideEffectType.DATAFLOW_SIDE_EFFECTING


def _flip(x, y, c, f):
    return (1 - x if f[0] else x, 1 - y if f[1] else y, 1 - c if f[2] else c)


def _slot(p):
    return 4 * p[0] + 2 * p[1] + p[2]


def _exchange_copies(src_refs, land_refs, send_sems, recv_sems, gather):
    x, y, c = _dev()
    mine = _slot((x, y, c))
    cps = []
    for a, (src, land) in enumerate(zip(src_refs, land_refs)):
        for k, f in enumerate(FLIPS):
            peer = _flip(x, y, c, f)
            cps.append(pltpu.make_async_remote_copy(
                src_ref=src if gather else src.at[_slot(peer)], dst_ref=land.at[mine],
                send_sem=send_sems.at[N_PEERS * a + k], recv_sem=recv_sems.at[N_PEERS * a + k],
                device_id=peer, device_id_type=MESH))
    return cps


def _exchange_start(name, srcs, gather, after):
    n = len(srcs)
    lands = [lax.empty(((N_DEV,) + s.shape) if gather else s.shape, s.dtype) for s in srcs]

    def body(*refs):
        src_refs, land_refs = refs[:n], refs[n:2 * n]
        send_sems, recv_sems, local_sems = refs[2 * n + 1:2 * n + 4]
        token = refs[-1]
        if gather:
            x, y, c = _dev()
            for a in range(n):
                pltpu.make_async_copy(src_refs[a], land_refs[a].at[_slot((x, y, c))], local_sems.at[a]).start()
        for cp in _exchange_copies(src_refs, land_refs, send_sems, recv_sems, gather):
            cp.start()
        token[...] = jnp.zeros_like(token)

    hbm = lambda z: pltpu.HBM(z.shape, z.dtype)
    outs = pl.pallas_call(
        body, name=name,
        out_shape=(pltpu.SemaphoreType.DMA((N_PEERS * n,)), pltpu.SemaphoreType.DMA((N_PEERS * n,)),
                   pltpu.SemaphoreType.DMA((n,)), *[hbm(s) for s in srcs], *[hbm(z) for z in lands],
                   jax.ShapeDtypeStruct((8, LANES), F32)),
        in_specs=[_HBM] * (2 * n) + [pl.BlockSpec(memory_space=pl.ANY)],
        out_specs=(_SEM, _SEM, _SEM, *[_HBM] * (2 * n), pl.BlockSpec(memory_space=pltpu.VMEM)),
        input_output_aliases={i: 3 + i for i in range(2 * n)},
        compiler_params=pltpu.CompilerParams(has_side_effects=_EFFECT),
    )(*[pltpu.with_memory_space_constraint(z, pltpu.HBM) for z in list(srcs) + lands], after)
    return (outs[:3], outs[3:3 + n], outs[3 + n:3 + 2 * n], gather), outs[-1]


def _exchange_wait(name, handles, after):
    sems, srcs, lands, gather = handles
    n = len(srcs)

    def body(*refs):
        src_refs, land_refs = refs[:n], refs[n:2 * n]
        send_sems, recv_sems, local_sems = refs[2 * n:2 * n + 3]
        if gather:
            for a in range(n):
                pltpu.make_async_copy(src_refs[a], land_refs[a].at[0], local_sems.at[a]).wait()
        for cp in _exchange_copies(src_refs, land_refs, send_sems, recv_sems, gather):
            cp.wait_send()
            cp.wait_recv()

    hbm = lambda z: pltpu.HBM(z.shape, z.dtype)
    outs = pl.pallas_call(
        body, name=name, out_shape=tuple(hbm(z) for z in list(srcs) + list(lands)),
        in_specs=[_HBM] * (2 * n) + [_SEM] * 3 + [pl.BlockSpec(memory_space=pl.ANY)],
        out_specs=tuple([_HBM] * (2 * n)), input_output_aliases={i: i for i in range(2 * n)},
        compiler_params=pltpu.CompilerParams(has_side_effects=_EFFECT),
    )(*srcs, *lands, *sems, after)
    return list(outs[n:])


def _pack_rows(sizes, width, row_align):
    offs, r = [], 0
    for s in sizes:
        offs.append(r)
        r += -(-s // width)
    total = -(-r // row_align) * row_align
    return offs, total


def _pack(items, width, row_align, lead=()):
    nl = len(lead)
    sizes = [int(jnp.size(a)) // max(1, functools.reduce(lambda p, q: p * q, lead, 1)) for a in items]
    offs, total = _pack_rows(sizes, width, row_align)
    flat = []
    used = 0
    for a, s in zip(items, sizes):
        f = a.reshape(lead + (s,))
        pad = -(-s // width) * width - s
        if pad:
            f = jnp.pad(f, [(0, 0)] * nl + [(0, pad)])
        flat.append(f)
        used += s + pad
    tail = total * width - used
    if tail:
        flat.append(jnp.zeros(lead + (tail,), items[0].dtype))
    return jnp.concatenate(flat, axis=-1).reshape(lead + (total, width)), offs


def _unpack(packed, off, shape, lead=()):
    nl = len(lead)
    size = functools.reduce(lambda p, q: p * q, shape, 1)
    width = packed.shape[-1]
    rows = -(-size // width)
    blk = lax.slice_in_dim(packed, off, off + rows, axis=nl).reshape(lead + (rows * width,))
    return lax.slice_in_dim(blk, 0, size, axis=nl).reshape(lead + tuple(shape))


def _rms(x, g):
    return (x * lax.rsqrt(jnp.mean(x * x, axis=-1, keepdims=True) + RMS_EPS)) * g


def _norm_mod(x, g, sc, sh):
    return _rms(x, g) * (1.0 + sc) + sh


def _mix_fn(glu_a, glu_b, attn, ga, gs):
    return jax.nn.sigmoid(ga) * attn + jax.nn.sigmoid(gs) * (glu_a * jax.nn.sigmoid(glu_b))


def _s5_disc_fn(a_re, a_im, log_dt, b_re, b_im):
    dt = jnp.exp(log_dt)
    mag = jnp.exp(a_re * dt)
    lr, li = mag * jnp.cos(a_im * dt), mag * jnp.sin(a_im * dt)
    den = a_re * a_re + a_im * a_im
    zr = ((lr - 1.0) * a_re + li * a_im) / den
    zi = (li * a_re - (lr - 1.0) * a_im) / den
    return lr, li, zr[None] * b_re - zi[None] * b_im, zr[None] * b_im + zi[None] * b_re


def _adamw_fn(w, g, m, v):
    m = ADAM_B1 * m + (1.0 - ADAM_B1) * g
    v = ADAM_B2 * v + (1.0 - ADAM_B2) * jnp.square(g)
    m_hat = m / (1.0 - ADAM_B1 ** ADAM_STEP)
    v_hat = v / (1.0 - ADAM_B2 ** ADAM_STEP)
    delta = -ADAM_LR * (m_hat / (jnp.sqrt(v_hat) + ADAM_EPS) + ADAM_WD * w)
    return delta, m, v


def _adamw(name, parts, w, m, v):
    p, r, c = parts.shape
    tr = _pick(r, max(8, (1 << 21) // (4 * c * max(p, 2))), 8)

    def fn(pv, wv, mv, vv):
        g = pv[0]
        for i in range(1, p):
            g = g + pv[i]
        d, m2, v2 = _adamw_fn(wv, g, mv, vv)
        return g, d, m2, v2

    spec = pl.BlockSpec((tr, c), lambda i: (i, 0))
    return _tile_call(
        name, fn, (r // tr,), [parts, w, m, v],
        [pl.BlockSpec((p, tr, c), lambda i: (0, i, 0)), spec, spec, spec],
        [jax.ShapeDtypeStruct((r, c), F32)] * 4, [spec] * 4)


def _adamw_sharded(name, parts, own_src, w, m, v, place):
    _, k, n = parts.shape
    tr = _pick(k, max(16, (1 << 19) // (4 * n)), 16)

    def body(pl_ref, p_ref, a_ref, w_ref, m_ref, v_ref, g_ref, d_ref, m2_ref, v2_ref):
        own = a_ref[0]
        g = None
        for q in range(N_DEV):
            term = jnp.where(pl_ref[0] == q, own, p_ref[q].astype(F32))
            g = term if g is None else g + term
        d, m2, v2 = _adamw_fn(w_ref[...], g, m_ref[...], v_ref[...])
        g_ref[...] = g
        d_ref[...] = d
        m2_ref[...] = m2
        v2_ref[...] = v2

    spec = pl.BlockSpec((tr, n), lambda i, pr: (i, 0))
    return pl.pallas_call(
        body,
        grid_spec=pltpu.PrefetchScalarGridSpec(
            num_scalar_prefetch=1, grid=(k // tr,),
            in_specs=[pl.BlockSpec((N_DEV, tr, n), lambda i, pr: (0, i, 0)),
                      pl.BlockSpec((1, tr, n), lambda i, pr: (pr[1], i, 0)),
                      spec, spec, spec],
            out_specs=[spec] * 4),
        out_shape=[jax.ShapeDtypeStruct((k, n), F32)] * 4, name=name, compiler_params=_params(1),
    )(place, parts, own_src, w, m, v)


def _attn_mask(n, rows):
    qi = lax.broadcasted_iota(jnp.int32, (rows, 2 * ATT_BLOCK), 0) & (ATT_BLOCK - 1)
    kj = lax.broadcasted_iota(jnp.int32, (rows, 2 * ATT_BLOCK), 1)
    rel = qi + ATT_BLOCK - kj
    return (rel >= 0) & (rel < ATT_BLOCK) & ((kj >= ATT_BLOCK) | (n > 0))


def _attn_probs(q, k, sink, mask):
    s = lax.dot_general(q, k, (((1,), (1,)), ((), ())), preferred_element_type=F32) * (HEAD_DIM ** -0.5)
    s = jnp.where(mask, s, NEG_INF)
    m = jnp.maximum(jnp.max(s, axis=-1, keepdims=True), sink)
    p = jnp.exp(s - m)
    e_sink = jnp.exp(sink - m)
    inv = 1.0 / (jnp.sum(p, axis=-1, keepdims=True) + e_sink)
    return p * inv, e_sink * inv


def _attn_specs(qpk):
    blk = ATT_BLOCK
    q_spec = pl.BlockSpec((qpk, blk, HEAD_DIM), lambda h, n: (h, n, 0))
    cur = pl.BlockSpec((1, blk, HEAD_DIM), lambda h, n: (h, n, 0))
    prev = pl.BlockSpec((1, blk, HEAD_DIM), lambda h, n: (h, jnp.maximum(n - 1, 0), 0))
    sink_spec = pl.BlockSpec((1, qpk * blk, 1), lambda h, n: (h, 0, 0))
    return q_spec, cur, prev, sink_spec


def _attn_fwd(q, k, v, sinks):
    hq, l, _ = q.shape
    qpk = hq // N_KV_HEADS
    nb = l // ATT_BLOCK
    rows = qpk * ATT_BLOCK
    q_spec, cur, prev, sink_spec = _attn_specs(qpk)

    def body(q_ref, kp_ref, kc_ref, vp_ref, vc_ref, sink_ref, o_ref):
        mask = _attn_mask(pl.program_id(1), rows)
        kk = jnp.concatenate([kp_ref[0], kc_ref[0]], axis=0).astype(BF16)
        vv = jnp.concatenate([vp_ref[0], vc_ref[0]], axis=0).astype(BF16)
        p, _ = _attn_probs(q_ref[...].reshape(rows, HEAD_DIM).astype(BF16), kk, sink_ref[0], mask)
        o = jnp.dot(p.astype(BF16), vv, preferred_element_type=F32)
        o_ref[...] = o.reshape(qpk, ATT_BLOCK, HEAD_DIM).astype(o_ref.dtype)

    return pl.pallas_call(
        body, grid=(N_KV_HEADS, nb), in_specs=[q_spec, prev, cur, prev, cur, sink_spec],
        out_specs=q_spec, out_shape=jax.ShapeDtypeStruct((hq, l, HEAD_DIM), BF16),
        name="attn_fwd", compiler_params=_params(2),
    )(q, k, k, v, v, sinks)


def _attn_bwd(q, k, v, sinks, do):
    hq, l, _ = q.shape
    qpk = hq // N_KV_HEADS
    nb = l // ATT_BLOCK
    blk = ATT_BLOCK
    rows = qpk * blk
    q_spec, cur, prev, sink_spec = _attn_specs(qpk)
    part_spec = pl.BlockSpec((1, 1, 2 * blk, HEAD_DIM), lambda h, n: (h, n, 0, 0))
    dsink_spec = pl.BlockSpec((qpk, 1, LANES), lambda h, n: (h, 0, 0))
    tn = (((0,), (0,)), ((), ()))

    def body(q_ref, do_ref, kp_ref, kc_ref, vp_ref, vc_ref, sink_ref, dq_ref, dkp_ref, dvp_ref, dsink_ref):
        n = pl.program_id(1)
        mask = _attn_mask(n, rows)
        kk = jnp.concatenate([kp_ref[0], kc_ref[0]], axis=0).astype(BF16)
        vv = jnp.concatenate([vp_ref[0], vc_ref[0]], axis=0).astype(BF16)
        qb = q_ref[...].reshape(rows, HEAD_DIM).astype(BF16)
        do32 = do_ref[...].astype(F32).reshape(rows, HEAD_DIM)
        dob = do32.astype(BF16)
        p, p_sink = _attn_probs(qb, kk, sink_ref[0], mask)
        pb = p.astype(BF16)
        o = jnp.dot(pb, vv, preferred_element_type=F32)
        delta = jnp.sum(do32 * o, axis=-1, keepdims=True)
        dp = lax.dot_general(dob, vv, (((1,), (1,)), ((), ())), preferred_element_type=F32)
        ds = (p * (dp - delta) * (HEAD_DIM ** -0.5)).astype(BF16)
        dq = jnp.dot(ds, kk, preferred_element_type=F32)
        dq_ref[...] = dq.reshape(qpk, blk, HEAD_DIM).astype(dq_ref.dtype)
        dkp_ref[0, 0] = lax.dot_general(ds, qb, tn, preferred_element_type=F32)
        dvp_ref[0, 0] = lax.dot_general(pb, dob, tn, preferred_element_type=F32)
        dsr = p_sink * delta
        for g in range(qpk):
            dsg = jnp.broadcast_to(-_colsum(dsr[g * blk:(g + 1) * blk]), (1, LANES))

            @pl.when(n == 0)
            def _():
                dsink_ref[g] = dsg

            @pl.when(n > 0)
            def _():
                dsink_ref[g] += dsg


    part_shape = jax.ShapeDtypeStruct((N_KV_HEADS, nb, 2 * blk, HEAD_DIM), F32)
    dq, dkp, dvp, dsink = pl.pallas_call(
        body, grid=(N_KV_HEADS, nb), in_specs=[q_spec, q_spec, prev, cur, prev, cur, sink_spec],
        out_specs=[q_spec, part_spec, part_spec, dsink_spec],
        out_shape=[jax.ShapeDtypeStruct((hq, l, HEAD_DIM), BF16), part_shape, part_shape,
                   jax.ShapeDtypeStruct((hq, 1, LANES), F32)],
        name="attn_bwd", compiler_params=_params(2),
    )(q, do, k, k, v, v, sinks)

    def combine(a_cur, a_nxt, b_cur, b_nxt):
        last = pl.program_id(1) == nb - 1
        keep = jnp.where(last, 0.0, 1.0)
        return (a_cur[0, 0, blk:] + keep * a_nxt[0, 0, :blk])[None], (b_cur[0, 0, blk:] + keep * b_nxt[0, 0, :blk])[None]

    nxt_spec = pl.BlockSpec((1, 1, 2 * blk, HEAD_DIM), lambda h, n: (h, jnp.minimum(n + 1, nb - 1), 0, 0))
    kv_shape = jax.ShapeDtypeStruct((N_KV_HEADS, l, HEAD_DIM), BF16)
    dk, dv = _tile_call("attn_dkv", combine, (N_KV_HEADS, nb), [dkp, dkp, dvp, dvp],
                        [part_spec, nxt_spec, part_spec, nxt_spec], [kv_shape, kv_shape], [cur, cur])
    return dq, dk, dv, dsink


def _block_diag(m):
    j, gl, a, b = m.shape
    eye = jnp.eye(gl, dtype=m.dtype)
    return (m[:, :, :, None, :] * eye[None, :, None, :, None]).reshape(j, gl * a, gl * b)


def _diag_blocks(z, a):
    j = z.shape[0]
    gl = z.shape[1] // a
    b = z.shape[2] // gl
    d = jnp.diagonal(z.reshape(j, gl, a, gl, b), axis1=1, axis2=3)
    return d.transpose(0, 3, 1, 2)


def _s5_permute(src_ref, dst_ref, t_len):
    seg = t_len // 8
    for k in range(seg):
        dst_ref[8 * k:8 * k + 8, :] = src_ref[pl.ds(k, 8, stride=seg), :]


def _s5_unpermute(perm_ref, t_len, emit):
    per_seg = t_len // 64
    for m in range(t_len // 8):
        emit(8 * m, perm_ref[pl.ds(64 * (m % per_seg) + m // per_seg, 8, stride=8), :])


def _s5_powers(p_ref, lr, li, seg):
    hs = TILE_STATES

    def step(k, carry):
        pr, pi = carry
        p_ref[pl.ds(k, 1), 0:hs] = pr
        p_ref[pl.ds(k, 1), hs:2 * hs] = pi
        return lr * pr - li * pi, lr * pi + li * pr

    lax.fori_loop(0, seg, step, (lr, li))


def _s5_local_scan(x_ref, base, lr, li, seg, reverse):
    hs = TILE_STATES
    lr8, li8 = jnp.broadcast_to(lr, (8, hs)), jnp.broadcast_to(li, (8, hs))
    if reverse:
        li8 = -li8

    def step(i, carry):
        hr, hi = carry
        k = seg - 1 - i if reverse else i
        rows = pl.ds(pl.multiple_of(base + 8 * k, 8), 8)
        nr = lr8 * hr - li8 * hi + x_ref[rows, 0:hs]
        ni = lr8 * hi + li8 * hr + x_ref[rows, hs:2 * hs]
        x_ref[rows, 0:hs] = nr
        x_ref[rows, hs:2 * hs] = ni
        return nr, ni

    zero = jnp.zeros((8, hs), F32)
    return lax.fori_loop(0, seg, step, (zero, zero), unroll=2)


def _s5_carries(c_ref, e_ref, ends, start, pw_r, pw_i, reverse):
    hs = TILE_STATES
    e_ref[:, 0:hs] = ends[0]
    e_ref[:, hs:2 * hs] = ends[1]
    cr, ci = start
    if reverse:
        pw_i = -pw_i
    for s in (range(7, -1, -1) if reverse else range(8)):
        c_ref[s:s + 1, 0:hs] = cr
        c_ref[s:s + 1, hs:2 * hs] = ci
        er, ei = e_ref[s:s + 1, 0:hs], e_ref[s:s + 1, hs:2 * hs]
        cr, ci = er + pw_r * cr - pw_i * ci, ei + pw_r * ci + pw_i * cr
    return cr, ci


def _s5_states(u_perm_b16, bd_ref, x_ref, base, c_ref, e_ref, p_ref, lr, li, h_in, t_len):
    hs = TILE_STATES
    seg = t_len // 8
    x_ref[pl.ds(base, t_len), :] = jnp.dot(u_perm_b16, bd_ref[0], preferred_element_type=F32)
    ends = _s5_local_scan(x_ref, base, lr, li, seg, False)
    pw_r, pw_i = p_ref[seg - 1:seg, 0:hs], p_ref[seg - 1:seg, hs:2 * hs]
    h_out = _s5_carries(c_ref, e_ref, ends, h_in, pw_r, pw_i, False)
    cr, ci = c_ref[:, 0:hs], c_ref[:, hs:2 * hs]

    def fix(k, carry):
        rows = pl.ds(pl.multiple_of(base + 8 * k, 8), 8)
        pr, pi = p_ref[pl.ds(k, 1), 0:hs], p_ref[pl.ds(k, 1), hs:2 * hs]
        x_ref[rows, 0:hs] += pr * cr - pi * ci
        x_ref[rows, hs:2 * hs] += pr * ci + pi * cr
        return carry

    lax.fori_loop(0, seg, fix, 0, unroll=2)
    return h_out


def _s5_fwd(proj, u_off, bd, cbd, lam, dvec, t_len):
    l = proj.shape[0]
    nj = bd.shape[0]
    nch = l // t_len
    hs = TILE_STATES
    ub = u_off // LANES
    seg = t_len // 8
    assert t_len % 64 == 0

    def body(u_ref, bd_ref, cbd_ref, lam_ref, d_ref, y_ref, hst_ref, x_ref, h_ref, p_ref, c_ref, e_ref, up_ref, yp_ref):
        lr, li = lam_ref[0, 0:1, :], lam_ref[0, 1:2, :]

        @pl.when(pl.program_id(1) == 0)
        def _():
            h_ref[...] = jnp.zeros_like(h_ref)
            _s5_powers(p_ref, lr, li, seg)

        hst_ref[0, 0] = h_ref[...]
        _s5_permute(u_ref, up_ref, t_len)
        h_out = _s5_states(up_ref[...].astype(BF16), bd_ref, x_ref, 0, c_ref, e_ref, p_ref, lr, li,
                           (h_ref[:, 0:hs], h_ref[:, hs:2 * hs]), t_len)
        h_ref[:, 0:hs] = h_out[0]
        h_ref[:, hs:2 * hs] = h_out[1]
        yp_ref[...] = jnp.dot(x_ref[...].astype(BF16), cbd_ref[0], preferred_element_type=F32)
        dv = d_ref[0]

        def out(r0, rows):
            y_ref[r0:r0 + 8, :] = rows + dv * u_ref[r0:r0 + 8, :]

        _s5_unpermute(yp_ref, t_len, out)

    return pl.pallas_call(
        body, grid=(nj, nch),
        in_specs=[pl.BlockSpec((t_len, LANES), lambda j, c: (c, ub + j)),
                  pl.BlockSpec((1, LANES, 2 * hs), lambda j, c: (j, 0, 0)),
                  pl.BlockSpec((1, 2 * hs, LANES), lambda j, c: (j, 0, 0)),
                  pl.BlockSpec((1, 2, hs), lambda j, c: (j, 0, 0)),
                  pl.BlockSpec((1, 1, LANES), lambda j, c: (j, 0, 0))],
        out_specs=[pl.BlockSpec((t_len, LANES), lambda j, c: (c, j)),
                   pl.BlockSpec((1, 1, 1, 2 * hs), lambda j, c: (j, c, 0, 0))],
        out_shape=[jax.ShapeDtypeStruct((l, nj * LANES), F32),
                   jax.ShapeDtypeStruct((nj, nch, 1, 2 * hs), F32)],
        scratch_shapes=[pltpu.VMEM((t_len, 2 * hs), F32), pltpu.VMEM((1, 2 * hs), F32),
                        pltpu.VMEM((seg, 2 * hs), F32), pltpu.VMEM((8, 2 * hs), F32), pltpu.VMEM((8, 2 * hs), F32),
                        pltpu.VMEM((t_len, LANES), F32), pltpu.VMEM((t_len, LANES), F32)],
        name="s5_fwd", compiler_params=_params(2),
    )(proj, bd, cbd, lam, dvec)


def _s5_bwd(proj, u_off, dy, hst, bd, bdt, cbdt, lam, dvec, t_len):
    l = proj.shape[0]
    nj = bd.shape[0]
    nch = l // t_len
    hs = TILE_STATES
    ub = u_off // LANES
    seg = t_len // 8
    tn = (((0,), (0,)), ((), ()))
    assert t_len % 64 == 0

    def body(u_ref, dy_ref, hst_ref, bd_ref, bdt_ref, cbdt_ref, lam_ref, d_ref,
             du_ref, dbd_ref, dcbdt_ref, dlam_ref, dd_ref,
             x_ref, g_ref, gc_ref, p_ref, c_ref, e_ref, up_ref, dyp_ref, dup_ref):
        first = pl.program_id(1) == 0
        lr, li = lam_ref[0, 0:1, :], lam_ref[0, 1:2, :]

        @pl.when(first)
        def _():
            gc_ref[...] = jnp.zeros_like(gc_ref)
            _s5_powers(p_ref, lr, li, seg)

        _s5_permute(u_ref, up_ref, t_len)
        _s5_permute(dy_ref, dyp_ref, t_len)
        ub16, dyb16 = up_ref[...].astype(BF16), dyp_ref[...].astype(BF16)
        h0 = hst_ref[0, 0]
        _s5_states(ub16, bd_ref, x_ref, 8, c_ref, e_ref, p_ref, lr, li, (h0[:, 0:hs], h0[:, hs:2 * hs]), t_len)
        x_ref[0:8, :] = c_ref[...]
        g_ref[...] = jnp.dot(dyb16, cbdt_ref[0], preferred_element_type=F32)
        starts = _s5_local_scan(g_ref, 0, lr, li, seg, True)
        pw_r, pw_i = p_ref[seg - 1:seg, 0:hs], p_ref[seg - 1:seg, hs:2 * hs]
        g_out = _s5_carries(c_ref, e_ref, starts, (gc_ref[:, 0:hs], gc_ref[:, hs:2 * hs]), pw_r, pw_i, True)
        gc_ref[:, 0:hs] = g_out[0]
        gc_ref[:, hs:2 * hs] = g_out[1]
        cr, ci = c_ref[:, 0:hs], c_ref[:, hs:2 * hs]

        def fix(k, carry):
            alr, ali = carry
            rows = pl.ds(pl.multiple_of(8 * k, 8), 8)
            pr, pi = p_ref[pl.ds(seg - 1 - k, 1), 0:hs], p_ref[pl.ds(seg - 1 - k, 1), hs:2 * hs]
            gr = g_ref[rows, 0:hs] + pr * cr + pi * ci
            gi = g_ref[rows, hs:2 * hs] + pr * ci - pi * cr
            g_ref[rows, 0:hs] = gr
            g_ref[rows, hs:2 * hs] = gi
            hpr, hpi = x_ref[rows, 0:hs], x_ref[rows, hs:2 * hs]
            return alr + gr * hpr + gi * hpi, ali + gi * hpr - gr * hpi

        zero = jnp.zeros((8, hs), F32)
        alr, ali = lax.fori_loop(0, seg, fix, (zero, zero), unroll=2)
        alr, ali = _colsum(alr), _colsum(ali)
        g = g_ref[...].astype(BF16)
        h = x_ref[pl.ds(8, t_len), :].astype(BF16)
        dup_ref[...] = jnp.dot(g, bdt_ref[0], preferred_element_type=F32)
        dv = d_ref[0]

        def out(r0, rows):
            du_ref[r0:r0 + 8, :] = (rows + dv * dy_ref[r0:r0 + 8, :]).astype(du_ref.dtype)

        _s5_unpermute(dup_ref, t_len, out)
        sign = jnp.where(lax.broadcasted_iota(jnp.int32, (1, 2 * hs), 1) < hs, 1.0, -1.0)
        dbd = lax.dot_general(ub16, g, tn, preferred_element_type=F32)
        dcbdt = lax.dot_general(dyb16, h, tn, preferred_element_type=F32) * sign
        ddv = _colsum(dy_ref[...] * u_ref[...])

        @pl.when(first)
        def _():
            dbd_ref[0] = dbd
            dcbdt_ref[0] = dcbdt
            dlam_ref[0, 0:1, :] = alr
            dlam_ref[0, 1:2, :] = ali
            dd_ref[0] = ddv

        @pl.when(jnp.logical_not(first))
        def _():
            dbd_ref[0] += dbd
            dcbdt_ref[0] += dcbdt
            dlam_ref[0, 0:1, :] += alr
            dlam_ref[0, 1:2, :] += ali
            dd_ref[0] += ddv

    rev = lambda c: nch - 1 - c
    wide = pl.BlockSpec((1, LANES, 2 * hs), lambda j, c: (j, 0, 0))
    tall = pl.BlockSpec((1, 2 * hs, LANES), lambda j, c: (j, 0, 0))
    return pl.pallas_call(
        body, grid=(nj, nch),
        in_specs=[pl.BlockSpec((t_len, LANES), lambda j, c: (rev(c), ub + j)),
                  pl.BlockSpec((t_len, LANES), lambda j, c: (rev(c), j)),
                  pl.BlockSpec((1, 1, 1, 2 * hs), lambda j, c: (j, rev(c), 0, 0)),
                  wide, tall, wide,
                  pl.BlockSpec((1, 2, hs), lambda j, c: (j, 0, 0)),
                  pl.BlockSpec((1, 1, LANES), lambda j, c: (j, 0, 0))],
        out_specs=[pl.BlockSpec((t_len, LANES), lambda j, c: (rev(c), j)),
                   wide, wide,
                   pl.BlockSpec((1, 2, hs), lambda j, c: (j, 0, 0)),
                   pl.BlockSpec((1, 1, LANES), lambda j, c: (j, 0, 0))],
        out_shape=[jax.ShapeDtypeStruct((l, nj * LANES), BF16),
                   jax.ShapeDtypeStruct((nj, LANES, 2 * hs), F32),
                   jax.ShapeDtypeStruct((nj, LANES, 2 * hs), F32),
                   jax.ShapeDtypeStruct((nj, 2, hs), F32),
                   jax.ShapeDtypeStruct((nj, 1, LANES), F32)],
        scratch_shapes=[pltpu.VMEM((t_len + 8, 2 * hs), F32), pltpu.VMEM((t_len, 2 * hs), F32),
                        pltpu.VMEM((1, 2 * hs), F32), pltpu.VMEM((seg, 2 * hs), F32),
                        pltpu.VMEM((8, 2 * hs), F32), pltpu.VMEM((8, 2 * hs), F32),
                        pltpu.VMEM((t_len, LANES), F32), pltpu.VMEM((t_len, LANES), F32),
                        pltpu.VMEM((t_len, LANES), F32)],
        name="s5_bwd", compiler_params=_params(2),
    )(proj, dy, hst, bd, bdt, cbdt, lam, dvec)


def _full_spec(shape):
    nd = len(shape)
    return pl.BlockSpec(tuple(shape), lambda i: (0,) * nd)


def _sds(shape, dtype=F32):
    return jax.ShapeDtypeStruct(tuple(shape), dtype)


def kernel(x, c, ada_w, ada_b, norm_mix_g, w_in, attn_sinks, w_attn_proj, ssm_a_re, ssm_a_im, ssm_log_dt, ssm_b_re, ssm_b_im, ssm_c_re, ssm_c_im, ssm_d, w_ssm_glu, w_out, norm_ffn_g, w_ffn_up, ffn_conv_w, ffn_conv_b, w_ffn_down, final_g, loss_target, m_ada_w, m_ada_b, m_norm_mix_g, m_w_in, m_attn_sinks, m_w_attn_proj, m_ssm_a_re, m_ssm_a_im, m_ssm_log_dt, m_ssm_b_re, m_ssm_b_im, m_ssm_c_re, m_ssm_c_im, m_ssm_d, m_w_ssm_glu, m_w_out, m_norm_ffn_g, m_w_ffn_up, m_ffn_conv_w, m_ffn_conv_b, m_w_ffn_down, m_final_g, v_ada_w, v_ada_b, v_norm_mix_g, v_w_in, v_attn_sinks, v_w_attn_proj, v_ssm_a_re, v_ssm_a_im, v_ssm_log_dt, v_ssm_b_re, v_ssm_b_im, v_ssm_c_re, v_ssm_c_im, v_ssm_d, v_w_ssm_glu, v_w_out, v_norm_ffn_g, v_w_ffn_up, v_ffn_conv_w, v_ffn_conv_b, v_w_ffn_down, v_final_g):
    given = dict(locals())
    names = ['ada_w', 'ada_b', 'norm_mix_g', 'w_in', 'attn_sinks', 'w_attn_proj', 'ssm_a_re', 'ssm_a_im',
             'ssm_log_dt', 'ssm_b_re', 'ssm_b_im', 'ssm_c_re', 'ssm_c_im', 'ssm_d', 'w_ssm_glu', 'w_out',
             'norm_ffn_g', 'w_ffn_up', 'ffn_conv_w', 'ffn_conv_b', 'w_ffn_down', 'final_g']

    xs = x[0]
    tgt = loss_target[0]
    l, d = xs.shape
    attn_w = w_attn_proj.shape[1]
    ssm_w = w_ssm_glu.shape[1]
    hq = attn_sinks.shape[1]
    qpk = hq // N_KV_HEADS
    kv_w = N_KV_HEADS * HEAD_DIM
    n_groups = ssm_a_re.shape[1]
    dff = ffn_conv_b.shape[1]
    in_w = attn_w + 2 * kv_w + ssm_w + 2 * d
    nj = ssm_w // LANES
    off_k, off_v, off_u = attn_w, attn_w + kv_w, attn_w + 2 * kv_w
    off_ga, off_gs = off_u + ssm_w, off_u + ssm_w + d
    assert hq * HEAD_DIM == attn_w and n_groups * SSM_P == ssm_w and l % ATT_BLOCK == 0

    xi, yi, ci = _dev()
    idx = 4 * xi + 2 * yi + ci

    row_sharded = {'w_out': (d, d), 'w_ffn_down': (dff, d)}
    big = ['w_in', 'w_attn_proj', 'w_ssm_glu', 'w_out', 'w_ffn_up', 'w_ffn_down']
    spack, s_offs = _pack([c, ffn_conv_w[0]], LANES, 8)
    w16 = {k: given[k][0].astype(BF16) for k in big}
    wg_in, sg = _all_gather("gather_first", [w16['w_in'], spack])
    mixer_w = ['w_attn_proj', 'w_ssm_glu', 'w_out']
    h_mixer, tok = _exchange_start("gather_mixer_start", [w16[k] for k in mixer_w], True, wg_in)
    h_up, tok = _exchange_start("gather_ffn_up_start", [w16['w_ffn_up']], True, tok)
    h_down, tok = _exchange_start("gather_ffn_down_start", [w16['w_ffn_down']], True, tok)
    full = {'w_in': wg_in.transpose(1, 0, 2).reshape(d, in_w)}
    c_all = _unpack(sg, s_offs[0], (d,), lead=(N_DEV,))
    conv_w = _unpack(sg, s_offs[1], ffn_conv_w.shape[1:], lead=(N_DEV,)).transpose(1, 0, 2).reshape(3, dff)
    conv_b = ffn_conv_b

    mod_n = ada_w.shape[2]
    tcm = _pick(mod_n, 512)
    ada_b_mine = lax.dynamic_slice_in_dim(ada_b, idx * mod_n, mod_n, axis=1)

    def modpart_fn(cv, wv, bv):
        cond = cv * jax.nn.sigmoid(cv)
        return jnp.dot(cond.astype(BF16), wv.astype(BF16), preferred_element_type=F32) + bv, cond

    modp, cond_all = _tile_call(
        "ada_rows", modpart_fn, (mod_n // tcm,), [c_all, ada_w[0], ada_b_mine],
        [pl.BlockSpec((N_DEV, d), lambda j: (0, 0)), pl.BlockSpec((d, tcm), lambda j: (0, j)),
         pl.BlockSpec((1, tcm), lambda j: (0, j))],
        [_sds((N_DEV, mod_n)), _sds((N_DEV, d))],
        [pl.BlockSpec((N_DEV, tcm), lambda j: (0, j)), pl.BlockSpec((N_DEV, d), lambda j: (0, 0))])
    (modg,) = _all_gather("gather_ada_rows", [modp])
    mod = lax.dynamic_index_in_dim(modg, idx, axis=1, keepdims=False).reshape(1, N_DEV * mod_n)
    sh1, sc1, g1, sh2, sc2, g2 = [mod[:, i * d:(i + 1) * d] for i in range(6)]

    tr = _pick(l, 256, 8)
    trh = _pick(l, 128, 8)
    nr, nrh = l // tr, l // trh
    g_mix, g_ffn, g_fin = norm_mix_g + tok[0:1, 0:1], norm_ffn_g, final_g.reshape(1, d)

    def with_t(fn):
        def wrapped(*vals):
            out = fn(*vals)
            out = out if isinstance(out, tuple) else (out,)
            return out + (out[-1].T,)
        return wrapped

    h1, h1_t = _tile_call("norm_mod_mix", with_t(_norm_mod), (1, nr), [xs, g_mix, sc1, sh1],
                          [_t(tr, d), _v(d), _v(d), _v(d)], [_sds((l, d), BF16), _sds((d, l), BF16)],
                          [_t(tr, d), _tt(tr, d)])
    proj = _matmul("proj_in", h1, full['w_in'], "nn", tn=1280)

    def heads(z, n):
        return z.reshape(l, n, HEAD_DIM).transpose(1, 0, 2)

    qh = heads(proj[:, :attn_w], hq)
    kh = heads(proj[:, off_k:off_k + kv_w], N_KV_HEADS)
    vh = heads(proj[:, off_v:off_v + kv_w], N_KV_HEADS)
    sinks3 = jnp.repeat(attn_sinks.reshape(N_KV_HEADS, qpk), ATT_BLOCK, axis=1)[..., None]
    o_h = _attn_fwd(qh, kh, vh, sinks3)
    o2 = o_h.transpose(1, 0, 2).reshape(l, attn_w)

    gn = (n_groups, SSM_N)
    pgn = (SSM_P, n_groups, SSM_N)
    a_re, a_im, log_dt = ssm_a_re[0], ssm_a_im[0], ssm_log_dt[0].reshape(n_groups, 1)
    b_re, b_im = ssm_b_re[0].transpose(2, 0, 1), ssm_b_im[0].transpose(2, 0, 1)
    disc_ins = [a_re, a_im, log_dt, b_re, b_im]
    disc_specs = [_full_spec(gn), _full_spec(gn), _full_spec((n_groups, 1)), _full_spec(pgn), _full_spec(pgn)]
    lam_r, lam_i, bb_r, bb_i = _tile_call(
        "s5_discretise", _s5_disc_fn, (1,), disc_ins, disc_specs,
        [_sds(gn), _sds(gn), _sds(pgn), _sds(pgn)],
        [_full_spec(gn), _full_spec(gn), _full_spec(pgn), _full_spec(pgn)])

    def tiles_gpn(z):
        return z.reshape(SSM_P, nj, TILE_GROUPS, SSM_N).transpose(1, 2, 0, 3)

    bd = jnp.concatenate([_block_diag(tiles_gpn(bb_r)), _block_diag(tiles_gpn(bb_i))], axis=2).astype(BF16)
    c_r = ssm_c_re[0].reshape(nj, TILE_GROUPS, SSM_P, SSM_N).transpose(0, 1, 3, 2)
    c_i = (-ssm_c_im[0]).reshape(nj, TILE_GROUPS, SSM_P, SSM_N).transpose(0, 1, 3, 2)
    cbd = jnp.concatenate([_block_diag(c_r), _block_diag(c_i)], axis=1).astype(BF16)
    bdt, cbdt = bd.transpose(0, 2, 1), cbd.transpose(0, 2, 1)
    lam = jnp.stack([lam_r.reshape(nj, TILE_STATES), lam_i.reshape(nj, TILE_STATES)], axis=1)
    dvec = ssm_d[0].reshape(nj, 1, LANES)
    t_len = _pick(l, 1024, 8)
    y, hst = _s5_fwd(proj, off_u, bd, cbd, lam, dvec, t_len)

    tcs, trg = _pick(ssm_w, 1024), _pick(l, 512, 8)
    gy = _tile_call("gelu", lambda v: jax.nn.gelu(v), (ssm_w // tcs, l // trg), [y], [_t(trg, tcs)],
                    [_sds((l, ssm_w), BF16)], [_t(trg, tcs)])[0]
    full.update(zip(mixer_w, _exchange_wait("gather_mixer_wait", h_mixer, gy)))
    full['w_out'] = full['w_out'].reshape(row_sharded['w_out'])
    full['w_attn_proj'] = full['w_attn_proj'].transpose(1, 0, 2).reshape(attn_w, d)
    full['w_ssm_glu'] = full['w_ssm_glu'].transpose(1, 0, 2).reshape(ssm_w, 2 * d)
    glu = _matmul("ssm_glu", gy, full['w_ssm_glu'], "nn")

    tcd = 256 if d % 256 == 0 and off_ga % 256 == 0 else LANES
    assert d % tcd == 0 and off_ga % tcd == 0 and off_gs % tcd == 0
    gate_ins = [(glu, 0), (glu, d), (proj, off_ga), (proj, off_gs)]

    def mix_epilogue(at, ga_, gb_, pa, ps):
        return at, _mix_fn(ga_, gb_, at, pa, ps)

    attn, mixed = _matmul("attn_proj_gate_mix", o2, full['w_attn_proj'], "nn", tn=tcd,
                          epilogue=(mix_epilogue, gate_ins, [(F32, False), (BF16, False)]))
    def res_norm_fn(xv, mo, g1v, gv, scv, shv):
        x2v = xv + g1v * mo
        return x2v, _norm_mod(x2v, gv, scv, shv)

    def res_norm_epilogue(mo, xv, g1v, gv, scv, shv):
        x2v, h2v = res_norm_fn(xv, mo, g1v, gv, scv, shv)
        return mo, x2v, h2v, h2v.T

    mixout, x2, h2, h2_t = _matmul(
        "mix_out_residual_norm_mod_ffn", mixed, full['w_out'], "nn", tm=256, tn=d,
        epilogue=(res_norm_epilogue, [xs, g1, g_ffn, sc2, sh2], [(F32, False), (F32, False), (BF16, False), (BF16, True)]))
    full['w_ffn_up'], = _exchange_wait("gather_ffn_up_wait", h_up, h2)
    up = _matmul("ffn_up", h2, full['w_ffn_up'], "nn", out_dtype=BF16, tn=1408)

    tcf, trc = _pick(dff, 1408), _pick(l, 512, 8)
    assert dff % tcf == 0
    ncf = dff // tcf

    taps = [conv_w[i:i + 1] for i in range(3)]

    def conv_gate(gp, gp_prev, w0, w1, w2, bv):
        gp = gp.astype(F32)
        prev = jnp.where(pl.program_id(1) == 0, 0.0, 1.0) * gp_prev.astype(F32)
        ext = jnp.concatenate([prev, gp], axis=0)
        m1 = pltpu.roll(ext, 1, 0)[HALO:]
        m2 = pltpu.roll(ext, 2, 0)[HALO:]
        return w0 * m2 + w1 * m1 + w2 * gp + bv, m1, m2

    def convglu_fn(gp, gp_prev, val, w0, w1, w2, bv):
        gate, _, _ = conv_gate(gp, gp_prev, w0, w1, w2, bv)
        return gate * jax.nn.sigmoid(gate) * val.astype(F32)

    act, act_t = _tile_call("conv_swiglu", with_t(convglu_fn), (ncf, l // trc), [up, up, up] + taps + [conv_b],
                            [_t(trc, tcf), _prev_rows(trc, tcf), _t(trc, tcf, dff)] + [_v(tcf)] * 4,
                            [_sds((l, dff), BF16), _sds((dff, l), BF16)], [_t(trc, tcf), _tt(trc, tcf)])
    full['w_ffn_down'] = _exchange_wait("gather_ffn_down_wait", h_down, act)[0].reshape(row_sharded['w_ffn_down'])
    ffn = _matmul("ffn_down", act, full['w_ffn_down'], "nn", tm=512)

    def final_fn(x2v, fv, g2v, gv, tv):
        rows = x2v.shape[0]

        def loss_of(x2a, fa, g2a, ga):
            out = _rms(x2a + g2a * fa, ga)
            err = out - tv
            return 0.5 * _colsum(jnp.mean(err * err, axis=-1, keepdims=True))

        loss, vjp = jax.vjp(loss_of, x2v, fv, _bc(g2v, rows), _bc(gv, rows))
        dx3, dffn, dg2, dgf = vjp(jnp.ones((1, 1), F32))
        return jnp.broadcast_to(loss, (1, LANES)), dx3, dffn, _colsum(dg2), _colsum(dgf)

    loss_p, dx3, dffn, dg2, dg_fin = _tile_call(
        "loss_final_norm", final_fn, (1, nrh), [x2, ffn, g2, g_fin, tgt],
        [_t(trh, d), _t(trh, d), _v(d), _v(d), _t(trh, d)],
        [_sds((1, LANES)), _sds((l, d)), _sds((l, d), BF16), _sds((1, d)), _sds((1, d))],
        [_v(LANES), _t(trh, d), _t(trh, d), _v(d), _v(d)], acc=(0, 3, 4))
    loss = lax.psum(loss_p[0, 0], ("x", "y", "c"))

    dact = _matmul("d_act", dffn, full['w_ffn_down'], "nt", out_dtype=BF16, tn=1408, dep=loss.reshape(1, 1))
    gd, gd16, pending = {}, {}, []
    dw_down, dw_down16 = _matmul("dw_ffn_down", act_t, dffn, "nn", tm=512, also_bf16=True)
    gd['w_ffn_down'], gd16['w_ffn_down'] = [z.reshape((N_DEV,) + w_ffn_down.shape[1:]) for z in (dw_down, dw_down16)]
    handle, tok = _exchange_start("grad_ffn_down_start", [gd16['w_ffn_down']], False, loss.reshape(1, 1))
    pending.append((['w_ffn_down'], handle))
    conv_b_bwd = conv_b + tok[0:1, 0:1]

    def convglu_bwd_fn(gp, gp_prev, gp_next, val, val_next, da, da_next, w0, w1, w2, bv):
        rows = gp.shape[0]
        i = pl.program_id(1)
        gp, val, da = gp.astype(F32), val.astype(F32), da.astype(F32)
        prev = jnp.where(i == 0, 0.0, 1.0) * gp_prev.astype(F32)
        more = jnp.where(i == pl.num_programs(1) - 1, 0.0, 1.0)
        ext = jnp.concatenate([prev, gp, gp_next.astype(F32)], axis=0)
        cur = ext[HALO:]
        m1 = pltpu.roll(ext, 1, 0)[HALO:]
        m2 = pltpu.roll(ext, 2, 0)[HALO:]
        gate = w0 * m2 + w1 * m1 + w2 * cur + bv
        sg = jax.nn.sigmoid(gate)
        val_e = jnp.concatenate([val, val_next.astype(F32)], axis=0)
        da_e = jnp.concatenate([da, more * da_next.astype(F32)], axis=0)
        dgate = da_e * val_e * (sg * (1.0 + gate * (1.0 - sg)))
        p1 = pltpu.roll(dgate, rows + HALO - 1, 0)[:rows]
        p2 = pltpu.roll(dgate, rows + HALO - 2, 0)[:rows]
        dg = dgate[:rows]
        dgp = w2 * dg + w1 * p1 + w0 * p2
        dval = da * (gate[:rows] * sg[:rows])
        return (jnp.stack([dgp, dval], axis=0), _colsum(dg), _colsum(dg * m2[:rows]), _colsum(dg * m1[:rows]),
                _colsum(dg * gp))

    dup, dconv_b, dcw0, dcw1, dcw2 = _tile_call(
        "conv_swiglu_bwd", convglu_bwd_fn, (ncf, l // trc), [up, up, up, up, up, dact, dact] + taps + [conv_b_bwd],
        [_t(trc, tcf), _prev_rows(trc, tcf), _next_rows(trc, tcf, l), _t(trc, tcf, dff), _next_rows(trc, tcf, l, dff),
         _t(trc, tcf), _next_rows(trc, tcf, l)] + [_v(tcf)] * 4,
        [_sds((2, l, dff), BF16)] + [_sds((1, dff))] * 4, [_st(trc, tcf)] + [_v(tcf)] * 4, acc=(1, 2, 3, 4))
    dh2 = _matmul("d_h2", dup, full['w_ffn_up'], "nt", tm=512, fold=4)
    gd['w_ffn_up'], gd16['w_ffn_up'] = _matmul("dw_ffn_up", h2_t, dup, "nn", tm=512, tn=1408, out_stack=N_DEV, also_bf16=True)
    handle, tok = _exchange_start("grad_ffn_up_start", [gd16['w_ffn_up']], False, gd['w_ffn_up'])
    pending.append((['w_ffn_up'], handle))
    g_ffn_bwd = g_ffn + tok[0:1, 0:1]

    def res_norm_bwd_fn(xv, mo, g1v, gv, scv, shv, dhv, dxv):
        rows = xv.shape[0]
        _, vjp = jax.vjp(res_norm_fn, xv, mo, _bc(g1v, rows), _bc(gv, rows), _bc(scv, rows), _bc(shv, rows))
        dx, dmo, dg1v, dgv, dscv, dshv = vjp((dxv, dhv))
        return dx, dmo, _colsum(dg1v), _colsum(dgv), _colsum(dscv), _colsum(dshv)

    dx2, dmixout, dg1, dg_ffn, dsc2, dsh2 = _tile_call(
        "residual_norm_mod_ffn_bwd", res_norm_bwd_fn, (1, nrh), [xs, mixout, g1, g_ffn_bwd, sc2, sh2, dh2, dx3],
        [_t(trh, d), _t(trh, d), _v(d), _v(d), _v(d), _v(d), _t(trh, d), _t(trh, d)],
        [_sds((l, d)), _sds((l, d), BF16)] + [_sds((1, d))] * 4,
        [_t(trh, d), _t(trh, d)] + [_v(d)] * 4, acc=(2, 3, 4, 5))

    def mix_bwd_epilogue(dm, ga_, gb_, pa, ps, at):
        _, vjp = jax.vjp(_mix_fn, ga_, gb_, at, pa, ps)
        da, db, dat, dpa, dps = vjp(dm)
        return jnp.stack([da, db], axis=0), dat, dpa, dps

    dglu, dattn, dga, dgs = _matmul(
        "d_mixed_gate_mix_bwd", dmixout, full['w_out'], "nt", tn=tcd,
        epilogue=(mix_bwd_epilogue, gate_ins + [attn], [(BF16, 'pair')] + [(BF16, False)] * 3))
    dw_out, dw_out16 = _matmul("dw_out", mixed, dmixout, "tn", also_bf16=True)
    gd['w_out'], gd16['w_out'] = [z.reshape((N_DEV,) + w_out.shape[1:]) for z in (dw_out, dw_out16)]

    def gelu_bwd_epilogue(dgy, yv):
        _, vjp = jax.vjp(lambda z: jax.nn.gelu(z), yv)
        return (vjp(dgy)[0],)

    dy, = _matmul("d_gelu_y_gelu_bwd", dglu, full['w_ssm_glu'], "nt", epilogue=(gelu_bwd_epilogue, [y], [(F32, False)]))
    gd['w_ssm_glu'], gd16['w_ssm_glu'] = _matmul("dw_ssm_glu", gy, dglu, "tn", out_stack=N_DEV, also_bf16=True)
    du, dbd, dcbdt, dlam, dd_tiles = _s5_bwd(proj, off_u, dy, hst, bd, bdt, cbdt, lam, dvec, t_len)

    def gpn_of(z):
        return z.transpose(2, 0, 1, 3).reshape(pgn)

    dbb_r = gpn_of(_diag_blocks(dbd[:, :, :TILE_STATES], SSM_P))
    dbb_i = gpn_of(_diag_blocks(dbd[:, :, TILE_STATES:], SSM_P))
    dc_re = _diag_blocks(dcbdt[:, :, :TILE_STATES], SSM_P).reshape(n_groups, SSM_P, SSM_N)
    dc_im = _diag_blocks(dcbdt[:, :, TILE_STATES:], SSM_P).reshape(n_groups, SSM_P, SSM_N)
    dlam_r, dlam_i = dlam[:, 0].reshape(gn), dlam[:, 1].reshape(gn)

    def disc_bwd_fn(ar, ai, ld, br, bi, dlr, dli, dbr, dbi):
        _, vjp = jax.vjp(_s5_disc_fn, ar, ai, ld, br, bi)
        return vjp((dlr, dli, dbr, dbi))

    da_re, da_im, dlog_dt, db_re, db_im = _tile_call(
        "s5_discretise_bwd", disc_bwd_fn, (1,), disc_ins + [dlam_r, dlam_i, dbb_r, dbb_i],
        disc_specs + [_full_spec(gn), _full_spec(gn), _full_spec(pgn), _full_spec(pgn)],
        [_sds(gn), _sds(gn), _sds((n_groups, 1)), _sds(pgn), _sds(pgn)], disc_specs)

    do2 = _matmul("d_attn_heads", dattn, full['w_attn_proj'], "nt")
    gd['w_attn_proj'], gd16['w_attn_proj'] = _matmul("dw_attn_proj", o2, dattn, "tn", out_stack=N_DEV, also_bf16=True)
    handle, tok = _exchange_start("grad_mixer_start", [gd16[k] for k in mixer_w], False, gd['w_attn_proj'])
    pending.append((mixer_w, handle))
    do_h = heads(do2.astype(BF16), hq)
    dq_h, dk_h, dv_h, dsink = _attn_bwd(qh, kh, vh, sinks3 + tok[0:1, 0:1], do_h)

    def unheads(z):
        return z.transpose(1, 0, 2).reshape(l, z.shape[0] * HEAD_DIM)

    early = ['attn_sinks', 'ssm_a_re', 'ssm_a_im', 'ssm_log_dt', 'ssm_b_re', 'ssm_b_im', 'ssm_c_re', 'ssm_c_im',
             'ssm_d', 'norm_ffn_g', 'ffn_conv_b', 'final_g']
    early_grads = {
        'attn_sinks': dsink[:, 0, 0], 'ssm_a_re': da_re, 'ssm_a_im': da_im, 'ssm_log_dt': dlog_dt,
        'ssm_b_re': db_re.transpose(1, 2, 0), 'ssm_b_im': db_im.transpose(1, 2, 0), 'ssm_c_re': dc_re,
        'ssm_c_im': dc_im, 'ssm_d': dd_tiles, 'norm_ffn_g': dg_ffn, 'ffn_conv_b': dconv_b, 'final_g': dg_fin}
    ge_pack, e_offs = _pack([jnp.concatenate([dg1, dsh2, dsc2, dg2], axis=1)] + [early_grads[k] for k in early],
                            LANES, 8)
    h_early, tok = _exchange_start("gather_small_early_start", [ge_pack], True, dsink)

    dproj = jnp.concatenate([unheads(dq_h), unheads(dk_h), unheads(dv_h), du, dga, dgs], axis=1)
    dw_in, dw_in16 = _matmul("dw_in", h1_t, dproj, "nn", tm=512, tn=1280, also_bf16=True, dep=tok)
    dcw = jnp.concatenate([dcw0, dcw1, dcw2], axis=0)
    shard_in, shard_cw = w_in.shape[1:], ffn_conv_w.shape[1:]
    gd16['w_in'] = dw_in16.reshape(shard_in[0], N_DEV, shard_in[1]).transpose(1, 0, 2)
    own_in = lax.dynamic_slice_in_dim(dw_in, idx * shard_in[1], shard_in[1], axis=1)[None]
    gd['ffn_conv_w'] = dcw.reshape(shard_cw[0], N_DEV, shard_cw[1]).transpose(1, 0, 2)
    gd16['ffn_conv_w'] = gd['ffn_conv_w'].astype(BF16)
    handle, tok = _exchange_start("grad_in_start", [gd16['w_in'], gd16['ffn_conv_w']], False, dw_in)
    pending.append((['w_in', 'ffn_conv_w'], handle))
    dh1 = _matmul("d_h1", dproj, full['w_in'], "nt", tm=512, dep=tok)

    def norm_bwd_fn(xv, gv, scv, shv, dhv, dxv):
        rows = xv.shape[0]
        _, vjp = jax.vjp(_norm_mod, xv, _bc(gv, rows), _bc(scv, rows), _bc(shv, rows))
        dx, dgv, dscv, dshv = vjp(dhv)
        return dx + dxv, _colsum(dgv), _colsum(dscv), _colsum(dshv)

    grad_x, dg_mix, dsc1, dsh1 = _tile_call(
        "norm_mod_mix_bwd", norm_bwd_fn, (1, nrh), [xs, g_mix, sc1, sh1, dh1, dx2],
        [_t(trh, d), _v(d), _v(d), _v(d), _t(trh, d), _t(trh, d)],
        [_sds((l, d))] + [_sds((1, d))] * 3, [_t(trh, d)] + [_v(d)] * 3, acc=(1, 2, 3))

    gl_pack, l_offs = _pack([jnp.concatenate([dsh1, dsc1], axis=1), dg_mix], LANES, 8)
    h_late, tok = _exchange_start("gather_small_late_start", [gl_pack], True, grad_x)

    sharded = big + ['ffn_conv_w']
    sharded_out = {}

    def finish(group, handle, after):
        for k, parts in zip(group, _exchange_wait("grad_" + group[0] + "_wait", handle, after)):
            own_src, own_at = (own_in, 0 * idx) if k == 'w_in' else (gd[k], idx)
            sharded_out[k] = _adamw_sharded("adamw_" + k, parts, own_src, given[k][0], given['m_' + k][0],
                                            given['v_' + k][0], jnp.stack([idx, own_at]).astype(jnp.int32))

    for group, handle in pending[:-1]:
        finish(group, handle, tok)
    done = functools.reduce(lambda p, q: p + q, [sharded_out[k][1][0:1, 0:1] for g_, _ in pending[:-1] for k in g_])
    finish(*pending[-1], done)

    ge_all, = _exchange_wait("gather_small_early_wait", h_early, done)
    gl_all, = _exchange_wait("gather_small_late_wait", h_late, sharded_out['w_in'][1])
    gs_all = jnp.concatenate([ge_all, gl_all], axis=1)
    rows_e = ge_pack.shape[0]

    def small_pack(prefix):
        ab = given[prefix + 'ada_b']
        p_early, _ = _pack([ab[:, 2 * d:]] + [given[prefix + k] for k in early], LANES, 8)
        p_late, _ = _pack([ab[:, :2 * d], given[prefix + 'norm_mix_g']], LANES, 8)
        return jnp.concatenate([p_early, p_late], axis=0)

    small_out = _adamw("adamw_replicated", gs_all, small_pack(''), small_pack('m_'), small_pack('v_'))

    dmod_all = jnp.concatenate([_unpack(gl_all, l_offs[0], (2 * d,), lead=(N_DEV,)),
                                _unpack(ge_all, e_offs[0], (4 * d,), lead=(N_DEV,))], axis=1)
    dmod_mine = lax.dynamic_slice_in_dim(dmod_all, idx * mod_n, mod_n, axis=1)
    kpad = LANES - N_DEV
    cond_t = jnp.pad(cond_all.T, ((0, 0), (0, kpad)))
    dmod_pad = jnp.pad(dmod_mine, ((0, kpad), (0, 0)))
    g_ada_w = _matmul("dw_ada", cond_t, dmod_pad, "nn")
    ada_out = _adamw("adamw_ada_w", g_ada_w[None], ada_w[0], m_ada_w[0], v_ada_w[0])

    results = [{}, {}, {}, {}]
    for which in range(4):
        out = small_out[which]
        results[which]['ada_b'] = jnp.concatenate([_unpack(out, rows_e + l_offs[0], (1, 2 * d)),
                                                   _unpack(out, e_offs[0], (1, 4 * d))], axis=1)
        results[which]['norm_mix_g'] = _unpack(out, rows_e + l_offs[1], norm_mix_g.shape)
        for k, off in zip(early, e_offs[1:]):
            results[which][k] = _unpack(out, off, given[k].shape)
        for k in sharded:
            results[which][k] = sharded_out[k][which][None]
        results[which]['ada_w'] = ada_out[which][None]
    outs = [loss, grad_x[None]]
    for which in range(4):
        outs += [results[which][k] for k in names]
    return tuple(outs)
```

```python
import functools
import math

import jax
import jax.numpy as jnp
from jax import lax
from jax.experimental import pallas as pl
from jax.experimental.pallas import tpu as pltpu

F32, BF16 = jnp.float32, jnp.bfloat16
MESH = pl.DeviceIdType.MESH
N_DEV = 8

HEAD_DIM = 64
N_KV_HEADS = 2
ATT_BLOCK = 128
NEG_INF = -1e30
SSM_P = 16
SSM_N = 64
LANES = 128
TILE_GROUPS = LANES // SSM_P
TILE_STATES = TILE_GROUPS * SSM_N
RMS_EPS = 1e-6
ADAM_LR, ADAM_B1, ADAM_B2, ADAM_EPS, ADAM_WD, ADAM_STEP = 0.001, 0.9, 0.999, 1e-08, 0.01, 10
VMEM_LIMIT = 56 * 1024 * 1024
MATMUL_VMEM_BUDGET = 44 * 1024 * 1024


def _params(n_axes):
    return pltpu.CompilerParams(dimension_semantics=("arbitrary",) * n_axes, vmem_limit_bytes=VMEM_LIMIT)


def _pick(dim, pref, align=128):
    if dim <= align:
        return dim
    t = (min(pref, dim) // align) * align
    while t > align and dim % t:
        t -= align
    assert dim % t == 0, (dim, pref, align)
    return t


def _dev():
    return lax.axis_index("x"), lax.axis_index("y"), lax.axis_index("c")


def _tile_call(name, fn, grid, ins, in_specs, out_shapes, out_specs, acc=()):
    n_in, n_out = len(ins), len(out_shapes)
    acc_axis = len(grid) - 1

    def body(*refs):
        vals = fn(*[r[...] for r in refs[:n_in]])
        if not isinstance(vals, (tuple, list)):
            vals = (vals,)
        assert len(vals) == n_out
        for i, (r, v) in enumerate(zip(refs[n_in:], vals)):
            v = v.astype(r.dtype)
            if i in acc:
                first = pl.program_id(acc_axis) == 0

                @pl.when(first)
                def _():
                    r[...] = v

                @pl.when(jnp.logical_not(first))
                def _():
                    r[...] += v
            else:
                r[...] = v

    return pl.pallas_call(
        body, grid=grid, in_specs=in_specs, out_specs=out_specs, out_shape=out_shapes, name=name,
        compiler_params=_params(len(grid)),
    )(*ins)


def _t(tr, tc, off=0):
    return pl.BlockSpec((tr, tc), lambda j, i: (i, j + off // tc))


def _tt(tr, tc):
    return pl.BlockSpec((tc, tr), lambda j, i: (j, i))


def _v(tc, off=0, rows=1):
    return pl.BlockSpec((rows, tc), lambda j, i: (0, j + off // tc))


HALO = 16


def _prev_rows(tr, tc, off=0):
    return pl.BlockSpec((HALO, tc), lambda j, i: (jnp.maximum(i * (tr // HALO) - 1, 0), j + off // tc))


def _next_rows(tr, tc, nrows, off=0):
    return pl.BlockSpec((HALO, tc),
                        lambda j, i: (jnp.minimum((i + 1) * (tr // HALO), nrows // HALO - 1), j + off // tc))


def _st(tr, tc):
    return pl.BlockSpec((2, tr, tc), lambda j, i: (0, i, j))


def _bc(v, rows):
    return jnp.broadcast_to(v, (rows, v.shape[-1]))


def _colsum(v):
    return jnp.sum(v, axis=0, keepdims=True)


def _matmul(name, a, b, mode, out_dtype=F32, tm=1024, tn=1024, tk=None, out_stack=None, also_bf16=False, dep=None,
            fold=1, epilogue=None):
    def dims(z):
        return (z.shape[-2], z.shape[-1] * (z.shape[0] if z.ndim == 3 else 1))

    ar, ac = dims(a)
    br, bc = dims(b)
    if mode == "nn":
        m, k, n = ar, ac, bc
        assert br == k
    elif mode == "nt":
        m, k, n = ar, ac, br
        assert bc == k
    else:
        m, k, n = ac, ar, bc
        assert br == k
    m_lim, k_lim, n_lim = [m], [k], [n]
    if a.ndim == 3:
        (m_lim if mode == "tn" else k_lim).append(a.shape[-1])
    if b.ndim == 3:
        (k_lim if mode == "nt" else n_lim).append(b.shape[-1])
    if out_stack:
        n_lim.append(n // out_stack)
    tm = _pick(functools.reduce(math.gcd, m_lim), tm)
    tn = _pick(functools.reduce(math.gcd, n_lim), tn)
    k_unit = functools.reduce(math.gcd, k_lim)
    if tk is None:
        sa, sb, so = a.dtype.itemsize, b.dtype.itemsize, jnp.dtype(out_dtype).itemsize + (2 if also_bf16 else 0)
        fits = [t for t in range(LANES, k_unit + 1, LANES) if k_unit % t == 0 and
                2 * t * (tm * sa + tn * sb) + tm * tn * (2 * so + (4 if t < k else 0)) <= MATMUL_VMEM_BUDGET]
        tk = max(fits) if fits else _pick(k_unit, 512)
    else:
        tk = _pick(k_unit, tk)
    assert (k // tk) % fold == 0
    nk = k // (tk * fold)

    def spec(z, brows, bcols, ridx, cidx):
        if z.ndim == 3:
            per = z.shape[-1] // bcols
            return pl.BlockSpec((None, brows, bcols),
                                lambda i, j, kk: (cidx(i, j, kk) // per, ridx(i, j, kk), cidx(i, j, kk) % per))
        return pl.BlockSpec((brows, bcols), lambda i, j, kk: (ridx(i, j, kk), cidx(i, j, kk)))

    gi = lambda i, j, kk: i
    gj = lambda i, j, kk: j
    a_specs, b_specs = [], []
    for f in range(fold):
        gk = lambda i, j, kk, f=f: fold * kk + f
        if mode == "nn":
            a_specs.append(spec(a, tm, tk, gi, gk))
            b_specs.append(spec(b, tk, tn, gk, gj))
            dn = (((1,), (0,)), ((), ()))
        elif mode == "nt":
            a_specs.append(spec(a, tm, tk, gi, gk))
            b_specs.append(spec(b, tn, tk, gj, gk))
            dn = (((1,), (1,)), ((), ()))
        else:
            a_specs.append(spec(a, tk, tm, gk, gi))
            b_specs.append(spec(b, tk, tn, gk, gj))
            dn = (((0,), (0,)), ((), ()))

    epi_fn, epi_ins, epi_outs = epilogue if epilogue else (None, [], [])
    n_out = len(epi_outs) if epilogue else (2 if also_bf16 else 1)

    deps = [] if dep is None else [dep]

    def body(*refs):
        a_refs, b_refs = refs[:fold], refs[fold:2 * fold]
        e_refs = refs[2 * fold:2 * fold + len(epi_ins)]
        rest = refs[2 * fold + len(epi_ins) + len(deps):]
        o_refs, acc = rest[:n_out], rest[n_out:]
        part = None
        for a_ref, b_ref in zip(a_refs, b_refs):
            one = lax.dot_general(a_ref[...].astype(BF16), b_ref[...].astype(BF16), dn, preferred_element_type=F32)
            part = one if part is None else part + one

        def emit(val):
            vals = epi_fn(val, *[r[...] for r in e_refs]) if epilogue else [val] * n_out
            for o_ref, v in zip(o_refs, vals):
                o_ref[...] = v.astype(o_ref.dtype)

        if nk == 1:
            emit(part)
            return
        acc_ref, = acc
        kk = pl.program_id(2)

        @pl.when(kk == 0)
        def _():
            acc_ref[...] = part

        @pl.when(kk > 0)
        def _():
            acc_ref[...] += part

        @pl.when(kk == nk - 1)
        def _():
            emit(acc_ref[...])

    if out_stack:
        per = (n // out_stack) // tn
        out_spec = pl.BlockSpec((None, tm, tn), lambda i, j, kk: (j // per, i, j % per))
        shape = (out_stack, m, n // out_stack)
    else:
        out_spec = pl.BlockSpec((tm, tn), lambda i, j, kk: (i, j))
        shape = (m, n)
    if epilogue:
        assert not out_stack and not also_bf16
        kinds = {False: (pl.BlockSpec((tm, tn), lambda i, j, kk: (i, j)), (m, n)),
                 True: (pl.BlockSpec((tn, tm), lambda i, j, kk: (j, i)), (n, m)),
                 'pair': (pl.BlockSpec((2, tm, tn), lambda i, j, kk: (0, i, j)), (2, m, n))}
        out_specs = [kinds[t][0] for _, t in epi_outs]
        out_shapes = [jax.ShapeDtypeStruct(kinds[t][1], dt) for dt, t in epi_outs]
    else:
        out_specs = [out_spec] * n_out
        out_shapes = [jax.ShapeDtypeStruct(shape, dt) for dt in [out_dtype, BF16][:n_out]]
    e_pairs = [z if isinstance(z, tuple) else (z, 0) for z in epi_ins]
    assert all(off % tn == 0 for _, off in e_pairs)
    e_specs = [pl.BlockSpec((1, tn) if z.shape[0] == 1 else (tm, tn),
                            lambda i, j, kk, ob=off // tn, row=z.shape[0] == 1: (0 if row else i, j + ob))
               for z, off in e_pairs]
    res = pl.pallas_call(
        body, grid=(m // tm, n // tn, nk),
        in_specs=a_specs + b_specs + e_specs + [pl.BlockSpec(memory_space=pl.ANY)] * len(deps),
        out_specs=out_specs, out_shape=out_shapes,
        scratch_shapes=[pltpu.VMEM((tm, tn), F32)] if nk > 1 else [], name=name, compiler_params=_params(3),
    )(*[a] * fold, *[b] * fold, *[z for z, _ in e_pairs], *deps)
    return res if (also_bf16 or epilogue) else res[0]


def _all_gather(name, arrs, dep=None):
    n = len(arrs)
    deps = [] if dep is None else [dep]

    def body(*refs):
        ins, outs = refs[:n], refs[n + len(deps):2 * n + len(deps)]
        send_sems, recv_sems, local_sems = refs[2 * n + len(deps):]
        x, y, c = _dev()
        me, sib = (x, y, c), (x, y, 1 - c)
        x_nbr, y_nbr, diag = (1 - x, y), (x, 1 - y), (1 - x, 1 - y)
        north = c == 1
        relay_from = (jnp.where(north, 1 - x, x), jnp.where(north, y, 1 - y))
        relay_to = (jnp.where(north, x, 1 - x), jnp.where(north, 1 - y, y))

        def slot(p):
            return 4 * p[0] + 2 * p[1] + p[2]

        def copy(a, k, block, to, src=None):
            dst = outs[a].at[slot(block)]
            return pltpu.make_async_remote_copy(
                src_ref=dst if src is None else src, dst_ref=dst,
                send_sem=send_sems.at[7 * a + k], recv_sem=recv_sems.at[7 * a + k],
                device_id=to, device_id_type=MESH)

        mine = [pltpu.make_async_copy(ins[a], outs[a].at[slot(me)], local_sems.at[a]) for a in range(n)]
        for cp in mine:
            cp.start()
        sent = []
        for a in range(n):
            sent += [copy(a, 0, me, sib, src=ins[a]), copy(a, 1, me, (*x_nbr, c), src=ins[a]),
                     copy(a, 2, me, (*y_nbr, c), src=ins[a])]
        for cp in sent:
            cp.start()
        relays = [copy(a, 3, (*relay_from, c), (*relay_to, c)) for a in range(n)]
        for k, chip, relay_here in ((1, x_nbr, north), (2, y_nbr, jnp.logical_not(north)), (3, diag, None)):
            for a in range(n):
                copy(a, k, (*chip, c), me).wait_recv()
                cp = copy(a, 3 + k, (*chip, c), sib)
                cp.start()
                sent.append(cp)
                if relay_here is not None:
                    pl.when(relay_here)(relays[a].start)
        for a in range(n):
            copy(a, 0, sib, me).wait_recv()
            for k, chip in ((4, x_nbr), (5, y_nbr), (6, diag)):
                copy(a, k, (*chip, 1 - c), me).wait_recv()
        for cp in sent + relays:
            cp.wait_send()
        for cp in mine:
            cp.wait()

    any_spec = pl.BlockSpec(memory_space=pl.ANY)
    return pl.pallas_call(
        body, in_specs=[any_spec] * (n + len(deps)), out_specs=[any_spec] * n,
        out_shape=[jax.ShapeDtypeStruct((N_DEV,) + a.shape, a.dtype) for a in arrs],
        scratch_shapes=[pltpu.SemaphoreType.DMA((7 * n,)), pltpu.SemaphoreType.DMA((7 * n,)),
                        pltpu.SemaphoreType.DMA((n,))],
        name=name,
    )(*arrs, *deps)


FLIPS = [(0, 0, 1), (0, 1, 0), (1, 0, 0), (0, 1, 1), (1, 0, 1), (1, 1, 0), (1, 1, 1)]
N_PEERS = len(FLIPS)
_HBM = pl.BlockSpec(memory_space=pltpu.HBM)
_SEM = pl.BlockSpec(memory_space=pltpu.SEMAPHORE)
_EFFECT = pltpu.SideEffectType.DATAFLOW_SIDE_EFFECTING


def _flip(x, y, c, f):
    return (1 - x if f[0] else x, 1 - y if f[1] else y, 1 - c if f[2] else c)


def _slot(p):
    return 4 * p[0] + 2 * p[1] + p[2]


def _exchange_copies(src_refs, land_refs, send_sems, recv_sems, gather):
    x, y, c = _dev()
    mine = _slot((x, y, c))
    cps = []
    for a, (src, land) in enumerate(zip(src_refs, land_refs)):
        for k, f in enumerate(FLIPS):
            peer = _flip(x, y, c, f)
            cps.append(pltpu.make_async_remote_copy(
                src_ref=src if gather else src.at[_slot(peer)], dst_ref=land.at[mine],
                send_sem=send_sems.at[N_PEERS * a + k], recv_sem=recv_sems.at[N_PEERS * a + k],
                device_id=peer, device_id_type=MESH))
    return cps


def _exchange_start(name, srcs, gather, after):
    n = len(srcs)
    lands = [lax.empty(((N_DEV,) + s.shape) if gather else s.shape, s.dtype) for s in srcs]

    def body(*refs):
        src_refs, land_refs = refs[:n], refs[n:2 * n]
        send_sems, recv_sems, local_sems = refs[2 * n + 1:2 * n + 4]
        token = refs[-1]
        if gather:
            x, y, c = _dev()
            for a in range(n):
                pltpu.make_async_copy(src_refs[a], land_refs[a].at[_slot((x, y, c))], local_sems.at[a]).start()
        for cp in _exchange_copies(src_refs, land_refs, send_sems, recv_sems, gather):
            cp.start()
        token[...] = jnp.zeros_like(token)

    hbm = lambda z: pltpu.HBM(z.shape, z.dtype)
    outs = pl.pallas_call(
        body, name=name,
        out_shape=(pltpu.SemaphoreType.DMA((N_PEERS * n,)), pltpu.SemaphoreType.DMA((N_PEERS * n,)),
                   pltpu.SemaphoreType.DMA((n,)), *[hbm(s) for s in srcs], *[hbm(z) for z in lands],
                   jax.ShapeDtypeStruct((8, LANES), F32)),
        in_specs=[_HBM] * (2 * n) + [pl.BlockSpec(memory_space=pl.ANY)],
        out_specs=(_SEM, _SEM, _SEM, *[_HBM] * (2 * n), pl.BlockSpec(memory_space=pltpu.VMEM)),
        input_output_aliases={i: 3 + i for i in range(2 * n)},
        compiler_params=pltpu.CompilerParams(has_side_effects=_EFFECT),
    )(*[pltpu.with_memory_space_constraint(z, pltpu.HBM) for z in list(srcs) + lands], after)
    return (outs[:3], outs[3:3 + n], outs[3 + n:3 + 2 * n], gather), outs[-1]


def _exchange_wait(name, handles, after):
    sems, srcs, lands, gather = handles
    n = len(srcs)

    def body(*refs):
        src_refs, land_refs = refs[:n], refs[n:2 * n]
        send_sems, recv_sems, local_sems = refs[2 * n:2 * n + 3]
        if gather:
            for a in range(n):
                pltpu.make_async_copy(src_refs[a], land_refs[a].at[0], local_sems.at[a]).wait()
        for cp in _exchange_copies(src_refs, land_refs, send_sems, recv_sems, gather):
            cp.wait_send()
            cp.wait_recv()

    hbm = lambda z: pltpu.HBM(z.shape, z.dtype)
    outs = pl.pallas_call(
        body, name=name, out_shape=tuple(hbm(z) for z in list(srcs) + list(lands)),
        in_specs=[_HBM] * (2 * n) + [_SEM] * 3 + [pl.BlockSpec(memory_space=pl.ANY)],
        out_specs=tuple([_HBM] * (2 * n)), input_output_aliases={i: i for i in range(2 * n)},
        compiler_params=pltpu.CompilerParams(has_side_effects=_EFFECT),
    )(*srcs, *lands, *sems, after)
    return list(outs[n:])


def _pack_rows(sizes, width, row_align):
    offs, r = [], 0
    for s in sizes:
        offs.append(r)
        r += -(-s // width)
    total = -(-r // row_align) * row_align
    return offs, total


def _pack(items, width, row_align, lead=()):
    nl = len(lead)
    sizes = [int(jnp.size(a)) // max(1, functools.reduce(lambda p, q: p * q, lead, 1)) for a in items]
    offs, total = _pack_rows(sizes, width, row_align)
    flat = []
    used = 0
    for a, s in zip(items, sizes):
        f = a.reshape(lead + (s,))
        pad = -(-s // width) * width - s
        if pad:
            f = jnp.pad(f, [(0, 0)] * nl + [(0, pad)])
        flat.append(f)
        used += s + pad
    tail = total * width - used
    if tail:
        flat.append(jnp.zeros(lead + (tail,), items[0].dtype))
    return jnp.concatenate(flat, axis=-1).reshape(lead + (total, width)), offs


def _unpack(packed, off, shape, lead=()):
    nl = len(lead)
    size = functools.reduce(lambda p, q: p * q, shape, 1)
    width = packed.shape[-1]
    rows = -(-size // width)
    blk = lax.slice_in_dim(packed, off, off + rows, axis=nl).reshape(lead + (rows * width,))
    return lax.slice_in_dim(blk, 0, size, axis=nl).reshape(lead + tuple(shape))


def _rms(x, g):
    return (x * lax.rsqrt(jnp.mean(x * x, axis=-1, keepdims=True) + RMS_EPS)) * g


def _norm_mod(x, g, sc, sh):
    return _rms(x, g) * (1.0 + sc) + sh


def _mix_fn(glu_a, glu_b, attn, ga, gs):
    return jax.nn.sigmoid(ga) * attn + jax.nn.sigmoid(gs) * (glu_a * jax.nn.sigmoid(glu_b))


def _s5_disc_fn(a_re, a_im, log_dt, b_re, b_im):
    dt = jnp.exp(log_dt)
    mag = jnp.exp(a_re * dt)
    lr, li = mag * jnp.cos(a_im * dt), mag * jnp.sin(a_im * dt)
    den = a_re * a_re + a_im * a_im
    zr = ((lr - 1.0) * a_re + li * a_im) / den
    zi = (li * a_re - (lr - 1.0) * a_im) / den
    return lr, li, zr[None] * b_re - zi[None] * b_im, zr[None] * b_im + zi[None] * b_re


def _adamw_fn(w, g, m, v):
    m = ADAM_B1 * m + (1.0 - ADAM_B1) * g
    v = ADAM_B2 * v + (1.0 - ADAM_B2) * jnp.square(g)
    m_hat = m / (1.0 - ADAM_B1 ** ADAM_STEP)
    v_hat = v / (1.0 - ADAM_B2 ** ADAM_STEP)
    delta = -ADAM_LR * (m_hat / (jnp.sqrt(v_hat) + ADAM_EPS) + ADAM_WD * w)
    return delta, m, v


def _adamw(name, parts, w, m, v):
    p, r, c = parts.shape
    tr = _pick(r, max(8, (1 << 21) // (4 * c * max(p, 2))), 8)

    def fn(pv, wv, mv, vv):
        g = pv[0]
        for i in range(1, p):
            g = g + pv[i]
        d, m2, v2 = _adamw_fn(wv, g, mv, vv)
        return g, d, m2, v2

    spec = pl.BlockSpec((tr, c), lambda i: (i, 0))
    return _tile_call(
        name, fn, (r // tr,), [parts, w, m, v],
        [pl.BlockSpec((p, tr, c), lambda i: (0, i, 0)), spec, spec, spec],
        [jax.ShapeDtypeStruct((r, c), F32)] * 4, [spec] * 4)


def _adamw_sharded(name, parts, own_src, w, m, v, place):
    _, k, n = parts.shape
    tr = _pick(k, max(16, (1 << 19) // (4 * n)), 16)

    def body(pl_ref, p_ref, a_ref, w_ref, m_ref, v_ref, g_ref, d_ref, m2_ref, v2_ref):
        own = a_ref[0]
        g = None
        for q in range(N_DEV):
            term = jnp.where(pl_ref[0] == q, own, p_ref[q].astype(F32))
            g = term if g is None else g + term
        d, m2, v2 = _adamw_fn(w_ref[...], g, m_ref[...], v_ref[...])
        g_ref[...] = g
        d_ref[...] = d
        m2_ref[...] = m2
        v2_ref[...] = v2

    spec = pl.BlockSpec((tr, n), lambda i, pr: (i, 0))
    return pl.pallas_call(
        body,
        grid_spec=pltpu.PrefetchScalarGridSpec(
            num_scalar_prefetch=1, grid=(k // tr,),
            in_specs=[pl.BlockSpec((N_DEV, tr, n), lambda i, pr: (0, i, 0)),
                      pl.BlockSpec((1, tr, n), lambda i, pr: (pr[1], i, 0)),
                      spec, spec, spec],
            out_specs=[spec] * 4),
        out_shape=[jax.ShapeDtypeStruct((k, n), F32)] * 4, name=name, compiler_params=_params(1),
    )(place, parts, own_src, w, m, v)


def _attn_mask(n, rows):
    qi = lax.broadcasted_iota(jnp.int32, (rows, 2 * ATT_BLOCK), 0) & (ATT_BLOCK - 1)
    kj = lax.broadcasted_iota(jnp.int32, (rows, 2 * ATT_BLOCK), 1)
    rel = qi + ATT_BLOCK - kj
    return (rel >= 0) & (rel < ATT_BLOCK) & ((kj >= ATT_BLOCK) | (n > 0))


def _attn_probs(q, k, sink, mask):
    s = lax.dot_general(q, k, (((1,), (1,)), ((), ())), preferred_element_type=F32) * (HEAD_DIM ** -0.5)
    s = jnp.where(mask, s, NEG_INF)
    m = jnp.maximum(jnp.max(s, axis=-1, keepdims=True), sink)
    p = jnp.exp(s - m)
    e_sink = jnp.exp(sink - m)
    inv = 1.0 / (jnp.sum(p, axis=-1, keepdims=True) + e_sink)
    return p * inv, e_sink * inv


def _attn_specs(qpk):
    blk = ATT_BLOCK
    q_spec = pl.BlockSpec((qpk, blk, HEAD_DIM), lambda h, n: (h, n, 0))
    cur = pl.BlockSpec((1, blk, HEAD_DIM), lambda h, n: (h, n, 0))
    prev = pl.BlockSpec((1, blk, HEAD_DIM), lambda h, n: (h, jnp.maximum(n - 1, 0), 0))
    sink_spec = pl.BlockSpec((1, qpk * blk, 1), lambda h, n: (h, 0, 0))
    return q_spec, cur, prev, sink_spec


def _attn_fwd(q, k, v, sinks):
    hq, l, _ = q.shape
    qpk = hq // N_KV_HEADS
    nb = l // ATT_BLOCK
    rows = qpk * ATT_BLOCK
    q_spec, cur, prev, sink_spec = _attn_specs(qpk)

    def body(q_ref, kp_ref, kc_ref, vp_ref, vc_ref, sink_ref, o_ref):
        mask = _attn_mask(pl.program_id(1), rows)
        kk = jnp.concatenate([kp_ref[0], kc_ref[0]], axis=0).astype(BF16)
        vv = jnp.concatenate([vp_ref[0], vc_ref[0]], axis=0).astype(BF16)
        p, _ = _attn_probs(q_ref[...].reshape(rows, HEAD_DIM).astype(BF16), kk, sink_ref[0], mask)
        o = jnp.dot(p.astype(BF16), vv, preferred_element_type=F32)
        o_ref[...] = o.reshape(qpk, ATT_BLOCK, HEAD_DIM).astype(o_ref.dtype)

    return pl.pallas_call(
        body, grid=(N_KV_HEADS, nb), in_specs=[q_spec, prev, cur, prev, cur, sink_spec],
        out_specs=q_spec, out_shape=jax.ShapeDtypeStruct((hq, l, HEAD_DIM), BF16),
        name="attn_fwd", compiler_params=_params(2),
    )(q, k, k, v, v, sinks)


def _attn_bwd(q, k, v, sinks, do):
    hq, l, _ = q.shape
    qpk = hq // N_KV_HEADS
    nb = l // ATT_BLOCK
    blk = ATT_BLOCK
    rows = qpk * blk
    q_spec, cur, prev, sink_spec = _attn_specs(qpk)
    part_spec = pl.BlockSpec((1, 1, 2 * blk, HEAD_DIM), lambda h, n: (h, n, 0, 0))
    dsink_spec = pl.BlockSpec((qpk, 1, LANES), lambda h, n: (h, 0, 0))
    tn = (((0,), (0,)), ((), ()))

    def body(q_ref, do_ref, kp_ref, kc_ref, vp_ref, vc_ref, sink_ref, dq_ref, dkp_ref, dvp_ref, dsink_ref):
        n = pl.program_id(1)
        mask = _attn_mask(n, rows)
        kk = jnp.concatenate([kp_ref[0], kc_ref[0]], axis=0).astype(BF16)
        vv = jnp.concatenate([vp_ref[0], vc_ref[0]], axis=0).astype(BF16)
        qb = q_ref[...].reshape(rows, HEAD_DIM).astype(BF16)
        do32 = do_ref[...].astype(F32).reshape(rows, HEAD_DIM)
        dob = do32.astype(BF16)
        p, p_sink = _attn_probs(qb, kk, sink_ref[0], mask)
        pb = p.astype(BF16)
        o = jnp.dot(pb, vv, preferred_element_type=F32)
        delta = jnp.sum(do32 * o, axis=-1, keepdims=True)
        dp = lax.dot_general(dob, vv, (((1,), (1,)), ((), ())), preferred_element_type=F32)
        ds = (p * (dp - delta) * (HEAD_DIM ** -0.5)).astype(BF16)
        dq = jnp.dot(ds, kk, preferred_element_type=F32)
        dq_ref[...] = dq.reshape(qpk, blk, HEAD_DIM).astype(dq_ref.dtype)
        dkp_ref[0, 0] = lax.dot_general(ds, qb, tn, preferred_element_type=F32)
        dvp_ref[0, 0] = lax.dot_general(pb, dob, tn, preferred_element_type=F32)
        dsr = p_sink * delta
        for g in range(qpk):
            dsg = jnp.broadcast_to(-_colsum(dsr[g * blk:(g + 1) * blk]), (1, LANES))

            @pl.when(n == 0)
            def _():
                dsink_ref[g] = dsg

            @pl.when(n > 0)
            def _():
                dsink_ref[g] += dsg


    part_shape = jax.ShapeDtypeStruct((N_KV_HEADS, nb, 2 * blk, HEAD_DIM), F32)
    dq, dkp, dvp, dsink = pl.pallas_call(
        body, grid=(N_KV_HEADS, nb), in_specs=[q_spec, q_spec, prev, cur, prev, cur, sink_spec],
        out_specs=[q_spec, part_spec, part_spec, dsink_spec],
        out_shape=[jax.ShapeDtypeStruct((hq, l, HEAD_DIM), BF16), part_shape, part_shape,
                   jax.ShapeDtypeStruct((hq, 1, LANES), F32)],
        name="attn_bwd", compiler_params=_params(2),
    )(q, do, k, k, v, v, sinks)

    def combine(a_cur, a_nxt, b_cur, b_nxt):
        last = pl.program_id(1) == nb - 1
        keep = jnp.where(last, 0.0, 1.0)
        return (a_cur[0, 0, blk:] + keep * a_nxt[0, 0, :blk])[None], (b_cur[0, 0, blk:] + keep * b_nxt[0, 0, :blk])[None]

    nxt_spec = pl.BlockSpec((1, 1, 2 * blk, HEAD_DIM), lambda h, n: (h, jnp.minimum(n + 1, nb - 1), 0, 0))
    kv_shape = jax.ShapeDtypeStruct((N_KV_HEADS, l, HEAD_DIM), BF16)
    dk, dv = _tile_call("attn_dkv", combine, (N_KV_HEADS, nb), [dkp, dkp, dvp, dvp],
                        [part_spec, nxt_spec, part_spec, nxt_spec], [kv_shape, kv_shape], [cur, cur])
    return dq, dk, dv, dsink


def _block_diag(m):
    j, gl, a, b = m.shape
    eye = jnp.eye(gl, dtype=m.dtype)
    return (m[:, :, :, None, :] * eye[None, :, None, :, None]).reshape(j, gl * a, gl * b)


def _diag_blocks(z, a):
    j = z.shape[0]
    gl = z.shape[1] // a
    b = z.shape[2] // gl
    d = jnp.diagonal(z.reshape(j, gl, a, gl, b), axis1=1, axis2=3)
    return d.transpose(0, 3, 1, 2)


def _s5_permute(src_ref, dst_ref, t_len):
    seg = t_len // 8
    for k in range(seg):
        dst_ref[8 * k:8 * k + 8, :] = src_ref[pl.ds(k, 8, stride=seg), :]


def _s5_unpermute(perm_ref, t_len, emit):
    per_seg = t_len // 64
    for m in range(t_len // 8):
        emit(8 * m, perm_ref[pl.ds(64 * (m % per_seg) + m // per_seg, 8, stride=8), :])


def _s5_powers(p_ref, lr, li, seg):
    hs = TILE_STATES

    def step(k, carry):
        pr, pi = carry
        p_ref[pl.ds(k, 1), 0:hs] = pr
        p_ref[pl.ds(k, 1), hs:2 * hs] = pi
        return lr * pr - li * pi, lr * pi + li * pr

    lax.fori_loop(0, seg, step, (lr, li))


def _s5_local_scan(x_ref, base, lr, li, seg, reverse):
    hs = TILE_STATES
    lr8, li8 = jnp.broadcast_to(lr, (8, hs)), jnp.broadcast_to(li, (8, hs))
    if reverse:
        li8 = -li8

    def step(i, carry):
        hr, hi = carry
        k = seg - 1 - i if reverse else i
        rows = pl.ds(pl.multiple_of(base + 8 * k, 8), 8)
        nr = lr8 * hr - li8 * hi + x_ref[rows, 0:hs]
        ni = lr8 * hi + li8 * hr + x_ref[rows, hs:2 * hs]
        x_ref[rows, 0:hs] = nr
        x_ref[rows, hs:2 * hs] = ni
        return nr, ni

    zero = jnp.zeros((8, hs), F32)
    return lax.fori_loop(0, seg, step, (zero, zero), unroll=2)


def _s5_carries(c_ref, e_ref, ends, start, pw_r, pw_i, reverse):
    hs = TILE_STATES
    e_ref[:, 0:hs] = ends[0]
    e_ref[:, hs:2 * hs] = ends[1]
    cr, ci = start
    if reverse:
        pw_i = -pw_i
    for s in (range(7, -1, -1) if reverse else range(8)):
        c_ref[s:s + 1, 0:hs] = cr
        c_ref[s:s + 1, hs:2 * hs] = ci
        er, ei = e_ref[s:s + 1, 0:hs], e_ref[s:s + 1, hs:2 * hs]
        cr, ci = er + pw_r * cr - pw_i * ci, ei + pw_r * ci + pw_i * cr
    return cr, ci


def _s5_states(u_perm_b16, bd_ref, x_ref, base, c_ref, e_ref, p_ref, lr, li, h_in, t_len):
    hs = TILE_STATES
    seg = t_len // 8
    x_ref[pl.ds(base, t_len), :] = jnp.dot(u_perm_b16, bd_ref[0], preferred_element_type=F32)
    ends = _s5_local_scan(x_ref, base, lr, li, seg, False)
    pw_r, pw_i = p_ref[seg - 1:seg, 0:hs], p_ref[seg - 1:seg, hs:2 * hs]
    h_out = _s5_carries(c_ref, e_ref, ends, h_in, pw_r, pw_i, False)
    cr, ci = c_ref[:, 0:hs], c_ref[:, hs:2 * hs]

    def fix(k, carry):
        rows = pl.ds(pl.multiple_of(base + 8 * k, 8), 8)
        pr, pi = p_ref[pl.ds(k, 1), 0:hs], p_ref[pl.ds(k, 1), hs:2 * hs]
        x_ref[rows, 0:hs] += pr * cr - pi * ci
        x_ref[rows, hs:2 * hs] += pr * ci + pi * cr
        return carry

    lax.fori_loop(0, seg, fix, 0, unroll=2)
    return h_out


def _s5_fwd(proj, u_off, bd, cbd, lam, dvec, t_len):
    l = proj.shape[0]
    nj = bd.shape[0]
    nch = l // t_len
    hs = TILE_STATES
    ub = u_off // LANES
    seg = t_len // 8
    assert t_len % 64 == 0

    def body(u_ref, bd_ref, cbd_ref, lam_ref, d_ref, y_ref, hst_ref, x_ref, h_ref, p_ref, c_ref, e_ref, up_ref, yp_ref):
        lr, li = lam_ref[0, 0:1, :], lam_ref[0, 1:2, :]

        @pl.when(pl.program_id(1) == 0)
        def _():
            h_ref[...] = jnp.zeros_like(h_ref)
            _s5_powers(p_ref, lr, li, seg)

        hst_ref[0, 0] = h_ref[...]
        _s5_permute(u_ref, up_ref, t_len)
        h_out = _s5_states(up_ref[...].astype(BF16), bd_ref, x_ref, 0, c_ref, e_ref, p_ref, lr, li,
                           (h_ref[:, 0:hs], h_ref[:, hs:2 * hs]), t_len)
        h_ref[:, 0:hs] = h_out[0]
        h_ref[:, hs:2 * hs] = h_out[1]
        yp_ref[...] = jnp.dot(x_ref[...].astype(BF16), cbd_ref[0], preferred_element_type=F32)
        dv = d_ref[0]

        def out(r0, rows):
            y_ref[r0:r0 + 8, :] = rows + dv * u_ref[r0:r0 + 8, :]

        _s5_unpermute(yp_ref, t_len, out)

    return pl.pallas_call(
        body, grid=(nj, nch),
        in_specs=[pl.BlockSpec((t_len, LANES), lambda j, c: (c, ub + j)),
                  pl.BlockSpec((1, LANES, 2 * hs), lambda j, c: (j, 0, 0)),
                  pl.BlockSpec((1, 2 * hs, LANES), lambda j, c: (j, 0, 0)),
                  pl.BlockSpec((1, 2, hs), lambda j, c: (j, 0, 0)),
                  pl.BlockSpec((1, 1, LANES), lambda j, c: (j, 0, 0))],
        out_specs=[pl.BlockSpec((t_len, LANES), lambda j, c: (c, j)),
                   pl.BlockSpec((1, 1, 1, 2 * hs), lambda j, c: (j, c, 0, 0))],
        out_shape=[jax.ShapeDtypeStruct((l, nj * LANES), F32),
                   jax.ShapeDtypeStruct((nj, nch, 1, 2 * hs), F32)],
        scratch_shapes=[pltpu.VMEM((t_len, 2 * hs), F32), pltpu.VMEM((1, 2 * hs), F32),
                        pltpu.VMEM((seg, 2 * hs), F32), pltpu.VMEM((8, 2 * hs), F32), pltpu.VMEM((8, 2 * hs), F32),
                        pltpu.VMEM((t_len, LANES), F32), pltpu.VMEM((t_len, LANES), F32)],
        name="s5_fwd", compiler_params=_params(2),
    )(proj, bd, cbd, lam, dvec)


def _s5_bwd(proj, u_off, dy, hst, bd, bdt, cbdt, lam, dvec, t_len):
    l = proj.shape[0]
    nj = bd.shape[0]
    nch = l // t_len
    hs = TILE_STATES
    ub = u_off // LANES
    seg = t_len // 8
    tn = (((0,), (0,)), ((), ()))
    assert t_len % 64 == 0

    def body(u_ref, dy_ref, hst_ref, bd_ref, bdt_ref, cbdt_ref, lam_ref, d_ref,
             du_ref, dbd_ref, dcbdt_ref, dlam_ref, dd_ref,
             x_ref, g_ref, gc_ref, p_ref, c_ref, e_ref, up_ref, dyp_ref, dup_ref):
        first = pl.program_id(1) == 0
        lr, li = lam_ref[0, 0:1, :], lam_ref[0, 1:2, :]

        @pl.when(first)
        def _():
            gc_ref[...] = jnp.zeros_like(gc_ref)
            _s5_powers(p_ref, lr, li, seg)

        _s5_permute(u_ref, up_ref, t_len)
        _s5_permute(dy_ref, dyp_ref, t_len)
        ub16, dyb16 = up_ref[...].astype(BF16), dyp_ref[...].astype(BF16)
        h0 = hst_ref[0, 0]
        _s5_states(ub16, bd_ref, x_ref, 8, c_ref, e_ref, p_ref, lr, li, (h0[:, 0:hs], h0[:, hs:2 * hs]), t_len)
        x_ref[0:8, :] = c_ref[...]
        g_ref[...] = jnp.dot(dyb16, cbdt_ref[0], preferred_element_type=F32)
        starts = _s5_local_scan(g_ref, 0, lr, li, seg, True)
        pw_r, pw_i = p_ref[seg - 1:seg, 0:hs], p_ref[seg - 1:seg, hs:2 * hs]
        g_out = _s5_carries(c_ref, e_ref, starts, (gc_ref[:, 0:hs], gc_ref[:, hs:2 * hs]), pw_r, pw_i, True)
        gc_ref[:, 0:hs] = g_out[0]
        gc_ref[:, hs:2 * hs] = g_out[1]
        cr, ci = c_ref[:, 0:hs], c_ref[:, hs:2 * hs]

        def fix(k, carry):
            alr, ali = carry
            rows = pl.ds(pl.multiple_of(8 * k, 8), 8)
            pr, pi = p_ref[pl.ds(seg - 1 - k, 1), 0:hs], p_ref[pl.ds(seg - 1 - k, 1), hs:2 * hs]
            gr = g_ref[rows, 0:hs] + pr * cr + pi * ci
            gi = g_ref[rows, hs:2 * hs] + pr * ci - pi * cr
            g_ref[rows, 0:hs] = gr
            g_ref[rows, hs:2 * hs] = gi
            hpr, hpi = x_ref[rows, 0:hs], x_ref[rows, hs:2 * hs]
            return alr + gr * hpr + gi * hpi, ali + gi * hpr - gr * hpi

        zero = jnp.zeros((8, hs), F32)
        alr, ali = lax.fori_loop(0, seg, fix, (zero, zero), unroll=2)
        alr, ali = _colsum(alr), _colsum(ali)
        g = g_ref[...].astype(BF16)
        h = x_ref[pl.ds(8, t_len), :].astype(BF16)
        dup_ref[...] = jnp.dot(g, bdt_ref[0], preferred_element_type=F32)
        dv = d_ref[0]

        def out(r0, rows):
            du_ref[r0:r0 + 8, :] = (rows + dv * dy_ref[r0:r0 + 8, :]).astype(du_ref.dtype)

        _s5_unpermute(dup_ref, t_len, out)
        sign = jnp.where(lax.broadcasted_iota(jnp.int32, (1, 2 * hs), 1) < hs, 1.0, -1.0)
        dbd = lax.dot_general(ub16, g, tn, preferred_element_type=F32)
        dcbdt = lax.dot_general(dyb16, h, tn, preferred_element_type=F32) * sign
        ddv = _colsum(dy_ref[...] * u_ref[...])

        @pl.when(first)
        def _():
            dbd_ref[0] = dbd
            dcbdt_ref[0] = dcbdt
            dlam_ref[0, 0:1, :] = alr
            dlam_ref[0, 1:2, :] = ali
            dd_ref[0] = ddv

        @pl.when(jnp.logical_not(first))
        def _():
            dbd_ref[0] += dbd
            dcbdt_ref[0] += dcbdt
            dlam_ref[0, 0:1, :] += alr
            dlam_ref[0, 1:2, :] += ali
            dd_ref[0] += ddv

    rev = lambda c: nch - 1 - c
    wide = pl.BlockSpec((1, LANES, 2 * hs), lambda j, c: (j, 0, 0))
    tall = pl.BlockSpec((1, 2 * hs, LANES), lambda j, c: (j, 0, 0))
    return pl.pallas_call(
        body, grid=(nj, nch),
        in_specs=[pl.BlockSpec((t_len, LANES), lambda j, c: (rev(c), ub + j)),
                  pl.BlockSpec((t_len, LANES), lambda j, c: (rev(c), j)),
                  pl.BlockSpec((1, 1, 1, 2 * hs), lambda j, c: (j, rev(c), 0, 0)),
                  wide, tall, wide,
                  pl.BlockSpec((1, 2, hs), lambda j, c: (j, 0, 0)),
                  pl.BlockSpec((1, 1, LANES), lambda j, c: (j, 0, 0))],
        out_specs=[pl.BlockSpec((t_len, LANES), lambda j, c: (rev(c), j)),
                   wide, wide,
                   pl.BlockSpec((1, 2, hs), lambda j, c: (j, 0, 0)),
                   pl.BlockSpec((1, 1, LANES), lambda j, c: (j, 0, 0))],
        out_shape=[jax.ShapeDtypeStruct((l, nj * LANES), BF16),
                   jax.ShapeDtypeStruct((nj, LANES, 2 * hs), F32),
                   jax.ShapeDtypeStruct((nj, LANES, 2 * hs), F32),
                   jax.ShapeDtypeStruct((nj, 2, hs), F32),
                   jax.ShapeDtypeStruct((nj, 1, LANES), F32)],
        scratch_shapes=[pltpu.VMEM((t_len + 8, 2 * hs), F32), pltpu.VMEM((t_len, 2 * hs), F32),
                        pltpu.VMEM((1, 2 * hs), F32), pltpu.VMEM((seg, 2 * hs), F32),
                        pltpu.VMEM((8, 2 * hs), F32), pltpu.VMEM((8, 2 * hs), F32),
                        pltpu.VMEM((t_len, LANES), F32), pltpu.VMEM((t_len, LANES), F32),
                        pltpu.VMEM((t_len, LANES), F32)],
        name="s5_bwd", compiler_params=_params(2),
    )(proj, dy, hst, bd, bdt, cbdt, lam, dvec)


def _full_spec(shape):
    nd = len(shape)
    return pl.BlockSpec(tuple(shape), lambda i: (0,) * nd)


def _sds(shape, dtype=F32):
    return jax.ShapeDtypeStruct(tuple(shape), dtype)


def kernel(x, c, ada_w, ada_b, norm_mix_g, w_in, attn_sinks, w_attn_proj, ssm_a_re, ssm_a_im, ssm_log_dt, ssm_b_re, ssm_b_im, ssm_c_re, ssm_c_im, ssm_d, w_ssm_glu, w_out, norm_ffn_g, w_ffn_up, ffn_conv_w, ffn_conv_b, w_ffn_down, final_g, loss_target, m_ada_w, m_ada_b, m_norm_mix_g, m_w_in, m_attn_sinks, m_w_attn_proj, m_ssm_a_re, m_ssm_a_im, m_ssm_log_dt, m_ssm_b_re, m_ssm_b_im, m_ssm_c_re, m_ssm_c_im, m_ssm_d, m_w_ssm_glu, m_w_out, m_norm_ffn_g, m_w_ffn_up, m_ffn_conv_w, m_ffn_conv_b, m_w_ffn_down, m_final_g, v_ada_w, v_ada_b, v_norm_mix_g, v_w_in, v_attn_sinks, v_w_attn_proj, v_ssm_a_re, v_ssm_a_im, v_ssm_log_dt, v_ssm_b_re, v_ssm_b_im, v_ssm_c_re, v_ssm_c_im, v_ssm_d, v_w_ssm_glu, v_w_out, v_norm_ffn_g, v_w_ffn_up, v_ffn_conv_w, v_ffn_conv_b, v_w_ffn_down, v_final_g):
    given = dict(locals())
    names = ['ada_w', 'ada_b', 'norm_mix_g', 'w_in', 'attn_sinks', 'w_attn_proj', 'ssm_a_re', 'ssm_a_im',
             'ssm_log_dt', 'ssm_b_re', 'ssm_b_im', 'ssm_c_re', 'ssm_c_im', 'ssm_d', 'w_ssm_glu', 'w_out',
             'norm_ffn_g', 'w_ffn_up', 'ffn_conv_w', 'ffn_conv_b', 'w_ffn_down', 'final_g']

    xs = x[0]
    tgt = loss_target[0]
    l, d = xs.shape
    attn_w = w_attn_proj.shape[1]
    ssm_w = w_ssm_glu.shape[1]
    hq = attn_sinks.shape[1]
    qpk = hq // N_KV_HEADS
    kv_w = N_KV_HEADS * HEAD_DIM
    n_groups = ssm_a_re.shape[1]
    dff = ffn_conv_b.shape[1]
    in_w = attn_w + 2 * kv_w + ssm_w + 2 * d
    nj = ssm_w // LANES
    off_k, off_v, off_u = attn_w, attn_w + kv_w, attn_w + 2 * kv_w
    off_ga, off_gs = off_u + ssm_w, off_u + ssm_w + d
    assert hq * HEAD_DIM == attn_w and n_groups * SSM_P == ssm_w and l % ATT_BLOCK == 0

    xi, yi, ci = _dev()
    idx = 4 * xi + 2 * yi + ci

    row_sharded = {'w_out': (d, d), 'w_ffn_down': (dff, d)}
    big = ['w_in', 'w_attn_proj', 'w_ssm_glu', 'w_out', 'w_ffn_up', 'w_ffn_down']
    spack, s_offs = _pack([c, ffn_conv_w[0]], LANES, 8)
    w16 = {k: given[k][0].astype(BF16) for k in big}
    wg_in, sg = _all_gather("gather_first", [w16['w_in'], spack])
    mixer_w = ['w_attn_proj', 'w_ssm_glu', 'w_out']
    h_mixer, tok = _exchange_start("gather_mixer_start", [w16[k] for k in mixer_w], True, wg_in)
    h_up, tok = _exchange_start("gather_ffn_up_start", [w16['w_ffn_up']], True, tok)
    h_down, tok = _exchange_start("gather_ffn_down_start", [w16['w_ffn_down']], True, tok)
    full = {'w_in': wg_in.transpose(1, 0, 2).reshape(d, in_w)}
    c_all = _unpack(sg, s_offs[0], (d,), lead=(N_DEV,))
    conv_w = _unpack(sg, s_offs[1], ffn_conv_w.shape[1:], lead=(N_DEV,)).transpose(1, 0, 2).reshape(3, dff)
    conv_b = ffn_conv_b

    mod_n = ada_w.shape[2]
    tcm = _pick(mod_n, 512)
    ada_b_mine = lax.dynamic_slice_in_dim(ada_b, idx * mod_n, mod_n, axis=1)

    def modpart_fn(cv, wv, bv):
        cond = cv * jax.nn.sigmoid(cv)
        return jnp.dot(cond.astype(BF16), wv.astype(BF16), preferred_element_type=F32) + bv, cond

    modp, cond_all = _tile_call(
        "ada_rows", modpart_fn, (mod_n // tcm,), [c_all, ada_w[0], ada_b_mine],
        [pl.BlockSpec((N_DEV, d), lambda j: (0, 0)), pl.BlockSpec((d, tcm), lambda j: (0, j)),
         pl.BlockSpec((1, tcm), lambda j: (0, j))],
        [_sds((N_DEV, mod_n)), _sds((N_DEV, d))],
        [pl.BlockSpec((N_DEV, tcm), lambda j: (0, j)), pl.BlockSpec((N_DEV, d), lambda j: (0, 0))])
    (modg,) = _all_gather("gather_ada_rows", [modp])
    mod = lax.dynamic_index_in_dim(modg, idx, axis=1, keepdims=False).reshape(1, N_DEV * mod_n)
    sh1, sc1, g1, sh2, sc2, g2 = [mod[:, i * d:(i + 1) * d] for i in range(6)]

    tr = _pick(l, 256, 8)
    trh = _pick(l, 256, 8)
    nr, nrh = l // tr, l // trh
    g_mix, g_ffn, g_fin = norm_mix_g + tok[0:1, 0:1], norm_ffn_g, final_g.reshape(1, d)

    def with_t(fn):
        def wrapped(*vals):
            out = fn(*vals)
            out = out if isinstance(out, tuple) else (out,)
            return out + (out[-1].T,)
        return wrapped

    h1, h1_t = _tile_call("norm_mod_mix", with_t(_norm_mod), (1, nr), [xs, g_mix, sc1, sh1],
                          [_t(tr, d), _v(d), _v(d), _v(d)], [_sds((l, d), BF16), _sds((d, l), BF16)],
                          [_t(tr, d), _tt(tr, d)])
    proj = _matmul("proj_in", h1, full['w_in'], "nn", tn=1280)

    def heads(z, n):
        return z.reshape(l, n, HEAD_DIM).transpose(1, 0, 2)

    qh = heads(proj[:, :attn_w], hq)
    kh = heads(proj[:, off_k:off_k + kv_w], N_KV_HEADS)
    vh = heads(proj[:, off_v:off_v + kv_w], N_KV_HEADS)
    sinks3 = jnp.repeat(attn_sinks.reshape(N_KV_HEADS, qpk), ATT_BLOCK, axis=1)[..., None]
    o_h = _attn_fwd(qh, kh, vh, sinks3)
    o2 = o_h.transpose(1, 0, 2).reshape(l, attn_w)

    gn = (n_groups, SSM_N)
    pgn = (SSM_P, n_groups, SSM_N)
    a_re, a_im, log_dt = ssm_a_re[0], ssm_a_im[0], ssm_log_dt[0].reshape(n_groups, 1)
    b_re, b_im = ssm_b_re[0].transpose(2, 0, 1), ssm_b_im[0].transpose(2, 0, 1)
    disc_ins = [a_re, a_im, log_dt, b_re, b_im]
    disc_specs = [_full_spec(gn), _full_spec(gn), _full_spec((n_groups, 1)), _full_spec(pgn), _full_spec(pgn)]
    lam_r, lam_i, bb_r, bb_i = _tile_call(
        "s5_discretise", _s5_disc_fn, (1,), disc_ins, disc_specs,
        [_sds(gn), _sds(gn), _sds(pgn), _sds(pgn)],
        [_full_spec(gn), _full_spec(gn), _full_spec(pgn), _full_spec(pgn)])

    def tiles_gpn(z):
        return z.reshape(SSM_P, nj, TILE_GROUPS, SSM_N).transpose(1, 2, 0, 3)

    bd = jnp.concatenate([_block_diag(tiles_gpn(bb_r)), _block_diag(tiles_gpn(bb_i))], axis=2).astype(BF16)
    c_r = ssm_c_re[0].reshape(nj, TILE_GROUPS, SSM_P, SSM_N).transpose(0, 1, 3, 2)
    c_i = (-ssm_c_im[0]).reshape(nj, TILE_GROUPS, SSM_P, SSM_N).transpose(0, 1, 3, 2)
    cbd = jnp.concatenate([_block_diag(c_r), _block_diag(c_i)], axis=1).astype(BF16)
    bdt, cbdt = bd.transpose(0, 2, 1), cbd.transpose(0, 2, 1)
    lam = jnp.stack([lam_r.reshape(nj, TILE_STATES), lam_i.reshape(nj, TILE_STATES)], axis=1)
    dvec = ssm_d[0].reshape(nj, 1, LANES)
    t_len = _pick(l, 1024, 8)
    y, hst = _s5_fwd(proj, off_u, bd, cbd, lam, dvec, t_len)

    tcs, trg = _pick(ssm_w, 1024), _pick(l, 512, 8)
    gy = _tile_call("gelu", lambda v: jax.nn.gelu(v), (ssm_w // tcs, l // trg), [y], [_t(trg, tcs)],
                    [_sds((l, ssm_w), BF16)], [_t(trg, tcs)])[0]
    full.update(zip(mixer_w, _exchange_wait("gather_mixer_wait", h_mixer, gy)))
    full['w_out'] = full['w_out'].reshape(row_sharded['w_out'])
    full['w_attn_proj'] = full['w_attn_proj'].transpose(1, 0, 2).reshape(attn_w, d)
    full['w_ssm_glu'] = full['w_ssm_glu'].transpose(1, 0, 2).reshape(ssm_w, 2 * d)
    glu = _matmul("ssm_glu", gy, full['w_ssm_glu'], "nn")

    tcd = 256 if d % 256 == 0 and off_ga % 256 == 0 else LANES
    assert d % tcd == 0 and off_ga % tcd == 0 and off_gs % tcd == 0
    gate_ins = [(glu, 0), (glu, d), (proj, off_ga), (proj, off_gs)]

    def mix_epilogue(at, ga_, gb_, pa, ps):
        return at, _mix_fn(ga_, gb_, at, pa, ps)

    attn, mixed = _matmul("attn_proj_gate_mix", o2, full['w_attn_proj'], "nn", tn=tcd,
                          epilogue=(mix_epilogue, gate_ins, [(F32, False), (BF16, False)]))
    def res_norm_fn(xv, mo, g1v, gv, scv, shv):
        x2v = xv + g1v * mo
        return x2v, _norm_mod(x2v, gv, scv, shv)

    def res_norm_epilogue(mo, xv, g1v, gv, scv, shv):
        x2v, h2v = res_norm_fn(xv, mo, g1v, gv, scv, shv)
        return mo, x2v, h2v, h2v.T

    mixout, x2, h2, h2_t = _matmul(
        "mix_out_residual_norm_mod_ffn", mixed, full['w_out'], "nn", tm=256, tn=d,
        epilogue=(res_norm_epilogue, [xs, g1, g_ffn, sc2, sh2], [(F32, False), (F32, False), (BF16, False), (BF16, True)]))
    full['w_ffn_up'], = _exchange_wait("gather_ffn_up_wait", h_up, h2)
    up = _matmul("ffn_up", h2, full['w_ffn_up'], "nn", out_dtype=BF16, tn=1408)

    tcf, trc = _pick(dff, 1408), _pick(l, 512, 8)
    assert dff % tcf == 0
    ncf = dff // tcf

    taps = [conv_w[i:i + 1] for i in range(3)]

    def conv_gate(gp, gp_prev, w0, w1, w2, bv):
        gp = gp.astype(F32)
        prev = jnp.where(pl.program_id(1) == 0, 0.0, 1.0) * gp_prev.astype(F32)
        ext = jnp.concatenate([prev, gp], axis=0)
        m1 = pltpu.roll(ext, 1, 0)[HALO:]
        m2 = pltpu.roll(ext, 2, 0)[HALO:]
        return w0 * m2 + w1 * m1 + w2 * gp + bv, m1, m2

    def convglu_fn(gp, gp_prev, val, w0, w1, w2, bv):
        gate, _, _ = conv_gate(gp, gp_prev, w0, w1, w2, bv)
        return gate * jax.nn.sigmoid(gate) * val.astype(F32)

    act, act_t = _tile_call("conv_swiglu", with_t(convglu_fn), (ncf, l // trc), [up, up, up] + taps + [conv_b],
                            [_t(trc, tcf), _prev_rows(trc, tcf), _t(trc, tcf, dff)] + [_v(tcf)] * 4,
                            [_sds((l, dff), BF16), _sds((dff, l), BF16)], [_t(trc, tcf), _tt(trc, tcf)])
    full['w_ffn_down'] = _exchange_wait("gather_ffn_down_wait", h_down, act)[0].reshape(row_sharded['w_ffn_down'])
    ffn = _matmul("ffn_down", act, full['w_ffn_down'], "nn", tm=512)

    def final_fn(x2v, fv, g2v, gv, tv):
        rows = x2v.shape[0]

        def loss_of(x2a, fa, g2a, ga):
            out = _rms(x2a + g2a * fa, ga)
            err = out - tv
            return 0.5 * _colsum(jnp.mean(err * err, axis=-1, keepdims=True))

        loss, vjp = jax.vjp(loss_of, x2v, fv, _bc(g2v, rows), _bc(gv, rows))
        dx3, dffn, dg2, dgf = vjp(jnp.ones((1, 1), F32))
        return jnp.broadcast_to(loss, (1, LANES)), dx3, dffn, _colsum(dg2), _colsum(dgf)

    loss_p, dx3, dffn, dg2, dg_fin = _tile_call(
        "loss_final_norm", final_fn, (1, nrh), [x2, ffn, g2, g_fin, tgt],
        [_t(trh, d), _t(trh, d), _v(d), _v(d), _t(trh, d)],
        [_sds((1, LANES)), _sds((l, d)), _sds((l, d), BF16), _sds((1, d)), _sds((1, d))],
        [_v(LANES), _t(trh, d), _t(trh, d), _v(d), _v(d)], acc=(0, 3, 4))
    loss = lax.psum(loss_p[0, 0], ("x", "y", "c"))

    dact = _matmul("d_act", dffn, full['w_ffn_down'], "nt", out_dtype=BF16, tn=1408, dep=loss.reshape(1, 1))
    gd, gd16, pending = {}, {}, []
    dw_down, dw_down16 = _matmul("dw_ffn_down", act_t, dffn, "nn", tm=512, also_bf16=True)
    gd['w_ffn_down'], gd16['w_ffn_down'] = [z.reshape((N_DEV,) + w_ffn_down.shape[1:]) for z in (dw_down, dw_down16)]
    handle, tok = _exchange_start("grad_ffn_down_start", [gd16['w_ffn_down']], False, loss.reshape(1, 1))
    pending.append((['w_ffn_down'], handle))
    conv_b_bwd = conv_b + tok[0:1, 0:1]

    def convglu_bwd_fn(gp, gp_prev, gp_next, val, val_next, da, da_next, w0, w1, w2, bv):
        rows = gp.shape[0]
        i = pl.program_id(1)
        gp, val, da = gp.astype(F32), val.astype(F32), da.astype(F32)
        prev = jnp.where(i == 0, 0.0, 1.0) * gp_prev.astype(F32)
        more = jnp.where(i == pl.num_programs(1) - 1, 0.0, 1.0)
        ext = jnp.concatenate([prev, gp, gp_next.astype(F32)], axis=0)
        cur = ext[HALO:]
        m1 = pltpu.roll(ext, 1, 0)[HALO:]
        m2 = pltpu.roll(ext, 2, 0)[HALO:]
        gate = w0 * m2 + w1 * m1 + w2 * cur + bv
        sg = jax.nn.sigmoid(gate)
        val_e = jnp.concatenate([val, val_next.astype(F32)], axis=0)
        da_e = jnp.concatenate([da, more * da_next.astype(F32)], axis=0)
        dgate = da_e * val_e * (sg * (1.0 + gate * (1.0 - sg)))
        p1 = pltpu.roll(dgate, rows + HALO - 1, 0)[:rows]
        p2 = pltpu.roll(dgate, rows + HALO - 2, 0)[:rows]
        dg = dgate[:rows]
        dgp = w2 * dg + w1 * p1 + w0 * p2
        dval = da * (gate[:rows] * sg[:rows])
        return (jnp.stack([dgp, dval], axis=0), _colsum(dg), _colsum(dg * m2[:rows]), _colsum(dg * m1[:rows]),
                _colsum(dg * gp))

    dup, dconv_b, dcw0, dcw1, dcw2 = _tile_call(
        "conv_swiglu_bwd", convglu_bwd_fn, (ncf, nr), [up, up, up, up, up, dact, dact] + taps + [conv_b_bwd],
        [_t(tr, tcf), _prev_rows(tr, tcf), _next_rows(tr, tcf, l), _t(tr, tcf, dff), _next_rows(tr, tcf, l, dff),
         _t(tr, tcf), _next_rows(tr, tcf, l)] + [_v(tcf)] * 4,
        [_sds((2, l, dff), BF16)] + [_sds((1, dff))] * 4, [_st(tr, tcf)] + [_v(tcf)] * 4, acc=(1, 2, 3, 4))
    dh2 = _matmul("d_h2", dup, full['w_ffn_up'], "nt", tm=512, fold=4)
    gd['w_ffn_up'], gd16['w_ffn_up'] = _matmul("dw_ffn_up", h2_t, dup, "nn", tm=512, tn=1408, out_stack=N_DEV, also_bf16=True)
    handle, tok = _exchange_start("grad_ffn_up_start", [gd16['w_ffn_up']], False, gd['w_ffn_up'])
    pending.append((['w_ffn_up'], handle))
    g_ffn_bwd = g_ffn + tok[0:1, 0:1]

    def res_norm_bwd_fn(xv, mo, g1v, gv, scv, shv, dhv, dxv):
        rows = xv.shape[0]
        _, vjp = jax.vjp(res_norm_fn, xv, mo, _bc(g1v, rows), _bc(gv, rows), _bc(scv, rows), _bc(shv, rows))
        dx, dmo, dg1v, dgv, dscv, dshv = vjp((dxv, dhv))
        return dx, dmo, _colsum(dg1v), _colsum(dgv), _colsum(dscv), _colsum(dshv)

    dx2, dmixout, dg1, dg_ffn, dsc2, dsh2 = _tile_call(
        "residual_norm_mod_ffn_bwd", res_norm_bwd_fn, (1, nrh), [xs, mixout, g1, g_ffn_bwd, sc2, sh2, dh2, dx3],
        [_t(trh, d), _t(trh, d), _v(d), _v(d), _v(d), _v(d), _t(trh, d), _t(trh, d)],
        [_sds((l, d)), _sds((l, d), BF16)] + [_sds((1, d))] * 4,
        [_t(trh, d), _t(trh, d)] + [_v(d)] * 4, acc=(2, 3, 4, 5))

    def mix_bwd_epilogue(dm, ga_, gb_, pa, ps, at):
        _, vjp = jax.vjp(_mix_fn, ga_, gb_, at, pa, ps)
        da, db, dat, dpa, dps = vjp(dm)
        return jnp.stack([da, db], axis=0), dat, dpa, dps

    dglu, dattn, dga, dgs = _matmul(
        "d_mixed_gate_mix_bwd", dmixout, full['w_out'], "nt", tn=tcd,
        epilogue=(mix_bwd_epilogue, gate_ins + [attn], [(BF16, 'pair')] + [(BF16, False)] * 3))
    dw_out, dw_out16 = _matmul("dw_out", mixed, dmixout, "tn", also_bf16=True)
    gd['w_out'], gd16['w_out'] = [z.reshape((N_DEV,) + w_out.shape[1:]) for z in (dw_out, dw_out16)]

    def gelu_bwd_epilogue(dgy, yv):
        _, vjp = jax.vjp(lambda z: jax.nn.gelu(z), yv)
        return (vjp(dgy)[0],)

    dy, = _matmul("d_gelu_y_gelu_bwd", dglu, full['w_ssm_glu'], "nt", epilogue=(gelu_bwd_epilogue, [y], [(F32, False)]))
    gd['w_ssm_glu'], gd16['w_ssm_glu'] = _matmul("dw_ssm_glu", gy, dglu, "tn", out_stack=N_DEV, also_bf16=True)
    du, dbd, dcbdt, dlam, dd_tiles = _s5_bwd(proj, off_u, dy, hst, bd, bdt, cbdt, lam, dvec, t_len)

    def gpn_of(z):
        return z.transpose(2, 0, 1, 3).reshape(pgn)

    dbb_r = gpn_of(_diag_blocks(dbd[:, :, :TILE_STATES], SSM_P))
    dbb_i = gpn_of(_diag_blocks(dbd[:, :, TILE_STATES:], SSM_P))
    dc_re = _diag_blocks(dcbdt[:, :, :TILE_STATES], SSM_P).reshape(n_groups, SSM_P, SSM_N)
    dc_im = _diag_blocks(dcbdt[:, :, TILE_STATES:], SSM_P).reshape(n_groups, SSM_P, SSM_N)
    dlam_r, dlam_i = dlam[:, 0].reshape(gn), dlam[:, 1].reshape(gn)

    def disc_bwd_fn(ar, ai, ld, br, bi, dlr, dli, dbr, dbi):
        _, vjp = jax.vjp(_s5_disc_fn, ar, ai, ld, br, bi)
        return vjp((dlr, dli, dbr, dbi))

    da_re, da_im, dlog_dt, db_re, db_im = _tile_call(
        "s5_discretise_bwd", disc_bwd_fn, (1,), disc_ins + [dlam_r, dlam_i, dbb_r, dbb_i],
        disc_specs + [_full_spec(gn), _full_spec(gn), _full_spec(pgn), _full_spec(pgn)],
        [_sds(gn), _sds(gn), _sds((n_groups, 1)), _sds(pgn), _sds(pgn)], disc_specs)

    do2 = _matmul("d_attn_heads", dattn, full['w_attn_proj'], "nt")
    gd['w_attn_proj'], gd16['w_attn_proj'] = _matmul("dw_attn_proj", o2, dattn, "tn", out_stack=N_DEV, also_bf16=True)
    handle, tok = _exchange_start("grad_mixer_start", [gd16[k] for k in mixer_w], False, gd['w_attn_proj'])
    pending.append((mixer_w, handle))
    do_h = heads(do2.astype(BF16), hq)
    dq_h, dk_h, dv_h, dsink = _attn_bwd(qh, kh, vh, sinks3 + tok[0:1, 0:1], do_h)

    def unheads(z):
        return z.transpose(1, 0, 2).reshape(l, z.shape[0] * HEAD_DIM)

    early = ['attn_sinks', 'ssm_a_re', 'ssm_a_im', 'ssm_log_dt', 'ssm_b_re', 'ssm_b_im', 'ssm_c_re', 'ssm_c_im',
             'ssm_d', 'norm_ffn_g', 'ffn_conv_b', 'final_g']
    early_grads = {
        'attn_sinks': dsink[:, 0, 0], 'ssm_a_re': da_re, 'ssm_a_im': da_im, 'ssm_log_dt': dlog_dt,
        'ssm_b_re': db_re.transpose(1, 2, 0), 'ssm_b_im': db_im.transpose(1, 2, 0), 'ssm_c_re': dc_re,
        'ssm_c_im': dc_im, 'ssm_d': dd_tiles, 'norm_ffn_g': dg_ffn, 'ffn_conv_b': dconv_b, 'final_g': dg_fin}
    ge_pack, e_offs = _pack([jnp.concatenate([dg1, dsh2, dsc2, dg2], axis=1)] + [early_grads[k] for k in early],
                            LANES, 8)
    h_early, tok = _exchange_start("gather_small_early_start", [ge_pack], True, dsink)

    dproj = jnp.concatenate([unheads(dq_h), unheads(dk_h), unheads(dv_h), du, dga, dgs], axis=1)
    dw_in, dw_in16 = _matmul("dw_in", h1_t, dproj, "nn", tm=512, tn=1280, also_bf16=True, dep=tok)
    dcw = jnp.concatenate([dcw0, dcw1, dcw2], axis=0)
    shard_in, shard_cw = w_in.shape[1:], ffn_conv_w.shape[1:]
    gd16['w_in'] = dw_in16.reshape(shard_in[0], N_DEV, shard_in[1]).transpose(1, 0, 2)
    own_in = lax.dynamic_slice_in_dim(dw_in, idx * shard_in[1], shard_in[1], axis=1)[None]
    gd['ffn_conv_w'] = dcw.reshape(shard_cw[0], N_DEV, shard_cw[1]).transpose(1, 0, 2)
    gd16['ffn_conv_w'] = gd['ffn_conv_w'].astype(BF16)
    handle, tok = _exchange_start("grad_in_start", [gd16['w_in'], gd16['ffn_conv_w']], False, dw_in)
    pending.append((['w_in', 'ffn_conv_w'], handle))
    dh1 = _matmul("d_h1", dproj, full['w_in'], "nt", tm=512, dep=tok)

    def norm_bwd_fn(xv, gv, scv, shv, dhv, dxv):
        rows = xv.shape[0]
        _, vjp = jax.vjp(_norm_mod, xv, _bc(gv, rows), _bc(scv, rows), _bc(shv, rows))
        dx, dgv, dscv, dshv = vjp(dhv)
        return dx + dxv, _colsum(dgv), _colsum(dscv), _colsum(dshv)

    grad_x, dg_mix, dsc1, dsh1 = _tile_call(
        "norm_mod_mix_bwd", norm_bwd_fn, (1, nrh), [xs, g_mix, sc1, sh1, dh1, dx2],
        [_t(trh, d), _v(d), _v(d), _v(d), _t(trh, d), _t(trh, d)],
        [_sds((l, d))] + [_sds((1, d))] * 3, [_t(trh, d)] + [_v(d)] * 3, acc=(1, 2, 3))

    gl_pack, l_offs = _pack([jnp.concatenate([dsh1, dsc1], axis=1), dg_mix], LANES, 8)
    h_late, tok = _exchange_start("gather_small_late_start", [gl_pack], True, grad_x)

    sharded = big + ['ffn_conv_w']
    sharded_out = {}

    def finish(group, handle, after):
        for k, parts in zip(group, _exchange_wait("grad_" + group[0] + "_wait", handle, after)):
            own_src, own_at = (own_in, 0 * idx) if k == 'w_in' else (gd[k], idx)
            sharded_out[k] = _adamw_sharded("adamw_" + k, parts, own_src, given[k][0], given['m_' + k][0],
                                            given['v_' + k][0], jnp.stack([idx, own_at]).astype(jnp.int32))

    for group, handle in pending[:-1]:
        finish(group, handle, tok)
    done = functools.reduce(lambda p, q: p + q, [sharded_out[k][1][0:1, 0:1] for g_, _ in pending[:-1] for k in g_])
    finish(*pending[-1], done)

    ge_all, = _exchange_wait("gather_small_early_wait", h_early, done)
    gl_all, = _exchange_wait("gather_small_late_wait", h_late, sharded_out['w_in'][1])
    gs_all = jnp.concatenate([ge_all, gl_all], axis=1)
    rows_e = ge_pack.shape[0]

    def small_pack(prefix):
        ab = given[prefix + 'ada_b']
        p_early, _ = _pack([ab[:, 2 * d:]] + [given[prefix + k] for k in early], LANES, 8)
        p_late, _ = _pack([ab[:, :2 * d], given[prefix + 'norm_mix_g']], LANES, 8)
        return jnp.concatenate([p_early, p_late], axis=0)

    small_out = _adamw("adamw_replicated", gs_all, small_pack(''), small_pack('m_'), small_pack('v_'))

    dmod_all = jnp.concatenate([_unpack(gl_all, l_offs[0], (2 * d,), lead=(N_DEV,)),
                                _unpack(ge_all, e_offs[0], (4 * d,), lead=(N_DEV,))], axis=1)
    dmod_mine = lax.dynamic_slice_in_dim(dmod_all, idx * mod_n, mod_n, axis=1)
    kpad = LANES - N_DEV
    cond_t = jnp.pad(cond_all.T, ((0, 0), (0, kpad)))
    dmod_pad = jnp.pad(dmod_mine, ((0, kpad), (0, 0)))
    g_ada_w = _matmul("dw_ada", cond_t, dmod_pad, "nn")
    ada_out = _adamw("adamw_ada_w", g_ada_w[None], ada_w[0], m_ada_w[0], v_ada_w[0])

    results = [{}, {}, {}, {}]
    for which in range(4):
        out = small_out[which]
        results[which]['ada_b'] = jnp.concatenate([_unpack(out, rows_e + l_offs[0], (1, 2 * d)),
                                                   _unpack(out, e_offs[0], (1, 4 * d))], axis=1)
        results[which]['norm_mix_g'] = _unpack(out, rows_e + l_offs[1], norm_mix_g.shape)
        for k, off in zip(early, e_offs[1:]):
            results[which][k] = _unpack(out, off, given[k].shape)
        for k in sharded:
            results[which][k] = sharded_out[k][which][None]
        results[which]['ada_w'] = ada_out[which][None]
    outs = [loss, grad_x[None]]
    for which in range(4):
        outs += [results[which][k] for k in names]
    return tuple(outs)
```

```python
import functools
import math

import jax
import jax.numpy as jnp
from jax import lax
from jax.experimental import pallas as pl
from jax.experimental.pallas import tpu as pltpu

F32, BF16 = jnp.float32, jnp.bfloat16
MESH = pl.DeviceIdType.MESH
N_DEV = 8

HEAD_DIM = 64
N_KV_HEADS = 2
ATT_BLOCK = 128
NEG_INF = -1e30
SSM_P = 16
SSM_N = 64
LANES = 128
TILE_GROUPS = LANES // SSM_P
TILE_STATES = TILE_GROUPS * SSM_N
RMS_EPS = 1e-6
ADAM_LR, ADAM_B1, ADAM_B2, ADAM_EPS, ADAM_WD, ADAM_STEP = 0.001, 0.9, 0.999, 1e-08, 0.01, 10
VMEM_LIMIT = 56 * 1024 * 1024
MATMUL_VMEM_BUDGET = 44 * 1024 * 1024


def _params(n_axes):
    return pltpu.CompilerParams(dimension_semantics=("arbitrary",) * n_axes, vmem_limit_bytes=VMEM_LIMIT)


def _pick(dim, pref, align=128):
    if dim <= align:
        return dim
    t = (min(pref, dim) // align) * align
    while t > align and dim % t:
        t -= align
    assert dim % t == 0, (dim, pref, align)
    return t


def _dev():
    return lax.axis_index("x"), lax.axis_index("y"), lax.axis_index("c")


def _tile_call(name, fn, grid, ins, in_specs, out_shapes, out_specs, acc=()):
    n_in, n_out = len(ins), len(out_shapes)
    acc_axis = len(grid) - 1

    def body(*refs):
        vals = fn(*[r[...] for r in refs[:n_in]])
        if not isinstance(vals, (tuple, list)):
            vals = (vals,)
        assert len(vals) == n_out
        for i, (r, v) in enumerate(zip(refs[n_in:], vals)):
            v = v.astype(r.dtype)
            if i in acc:
                first = pl.program_id(acc_axis) == 0

                @pl.when(first)
                def _():
                    r[...] = v

                @pl.when(jnp.logical_not(first))
                def _():
                    r[...] += v
            else:
                r[...] = v

    return pl.pallas_call(
        body, grid=grid, in_specs=in_specs, out_specs=out_specs, out_shape=out_shapes, name=name,
        compiler_params=_params(len(grid)),
    )(*ins)


def _t(tr, tc, off=0):
    return pl.BlockSpec((tr, tc), lambda j, i: (i, j + off // tc))


def _tt(tr, tc):
    return pl.BlockSpec((tc, tr), lambda j, i: (j, i))


def _v(tc, off=0, rows=1):
    return pl.BlockSpec((rows, tc), lambda j, i: (0, j + off // tc))


HALO = 16


def _prev_rows(tr, tc, off=0):
    return pl.BlockSpec((HALO, tc), lambda j, i: (jnp.maximum(i * (tr // HALO) - 1, 0), j + off // tc))


def _next_rows(tr, tc, nrows, off=0):
    return pl.BlockSpec((HALO, tc),
                        lambda j, i: (jnp.minimum((i + 1) * (tr // HALO), nrows // HALO - 1), j + off // tc))


def _st(tr, tc):
    return pl.BlockSpec((2, tr, tc), lambda j, i: (0, i, j))


def _bc(v, rows):
    return jnp.broadcast_to(v, (rows, v.shape[-1]))


def _colsum(v):
    return jnp.sum(v, axis=0, keepdims=True)


def _matmul(name, a, b, mode, out_dtype=F32, tm=1024, tn=1024, tk=None, out_stack=None, also_bf16=False, dep=None,
            fold=1, epilogue=None):
    def dims(z):
        return (z.shape[-2], z.shape[-1] * (z.shape[0] if z.ndim == 3 else 1))

    ar, ac = dims(a)
    br, bc = dims(b)
    if mode == "nn":
        m, k, n = ar, ac, bc
        assert br == k
    elif mode == "nt":
        m, k, n = ar, ac, br
        assert bc == k
    else:
        m, k, n = ac, ar, bc
        assert br == k
    m_lim, k_lim, n_lim = [m], [k], [n]
    if a.ndim == 3:
        (m_lim if mode == "tn" else k_lim).append(a.shape[-1])
    if b.ndim == 3:
        (k_lim if mode == "nt" else n_lim).append(b.shape[-1])
    if out_stack:
        n_lim.append(n // out_stack)
    tm = _pick(functools.reduce(math.gcd, m_lim), tm)
    tn = _pick(functools.reduce(math.gcd, n_lim), tn)
    k_unit = functools.reduce(math.gcd, k_lim)
    if tk is None:
        sa, sb, so = a.dtype.itemsize, b.dtype.itemsize, jnp.dtype(out_dtype).itemsize + (2 if also_bf16 else 0)
        fits = [t for t in range(LANES, k_unit + 1, LANES) if k_unit % t == 0 and
                2 * t * (tm * sa + tn * sb) + tm * tn * (2 * so + (4 if t < k else 0)) <= MATMUL_VMEM_BUDGET]
        tk = max(fits) if fits else _pick(k_unit, 512)
    else:
        tk = _pick(k_unit, tk)
    assert (k // tk) % fold == 0
    nk = k // (tk * fold)

    def spec(z, brows, bcols, ridx, cidx):
        if z.ndim == 3:
            per = z.shape[-1] // bcols
            return pl.BlockSpec((None, brows, bcols),
                                lambda i, j, kk: (cidx(i, j, kk) // per, ridx(i, j, kk), cidx(i, j, kk) % per))
        return pl.BlockSpec((brows, bcols), lambda i, j, kk: (ridx(i, j, kk), cidx(i, j, kk)))

    gi = lambda i, j, kk: i
    gj = lambda i, j, kk: j
    a_specs, b_specs = [], []
    for f in range(fold):
        gk = lambda i, j, kk, f=f: fold * kk + f
        if mode == "nn":
            a_specs.append(spec(a, tm, tk, gi, gk))
            b_specs.append(spec(b, tk, tn, gk, gj))
            dn = (((1,), (0,)), ((), ()))
        elif mode == "nt":
            a_specs.append(spec(a, tm, tk, gi, gk))
            b_specs.append(spec(b, tn, tk, gj, gk))
            dn = (((1,), (1,)), ((), ()))
        else:
            a_specs.append(spec(a, tk, tm, gk, gi))
            b_specs.append(spec(b, tk, tn, gk, gj))
            dn = (((0,), (0,)), ((), ()))

    epi_fn, epi_ins, epi_outs = epilogue if epilogue else (None, [], [])
    n_out = len(epi_outs) if epilogue else (2 if also_bf16 else 1)

    deps = [] if dep is None else [dep]

    def body(*refs):
        a_refs, b_refs = refs[:fold], refs[fold:2 * fold]
        e_refs = refs[2 * fold:2 * fold + len(epi_ins)]
        rest = refs[2 * fold + len(epi_ins) + len(deps):]
        o_refs, acc = rest[:n_out], rest[n_out:]
        part = None
        for a_ref, b_ref in zip(a_refs, b_refs):
            one = lax.dot_general(a_ref[...].astype(BF16), b_ref[...].astype(BF16), dn, preferred_element_type=F32)
            part = one if part is None else part + one

        def emit(val):
            vals = epi_fn(val, *[r[...] for r in e_refs]) if epilogue else [val] * n_out
            for o_ref, v in zip(o_refs, vals):
                o_ref[...] = v.astype(o_ref.dtype)

        if nk == 1:
            emit(part)
            return
        acc_ref, = acc
        kk = pl.program_id(2)

        @pl.when(kk == 0)
        def _():
            acc_ref[...] = part

        @pl.when(kk > 0)
        def _():
            acc_ref[...] += part

        @pl.when(kk == nk - 1)
        def _():
            emit(acc_ref[...])

    if out_stack:
        per = (n // out_stack) // tn
        out_spec = pl.BlockSpec((None, tm, tn), lambda i, j, kk: (j // per, i, j % per))
        shape = (out_stack, m, n // out_stack)
    else:
        out_spec = pl.BlockSpec((tm, tn), lambda i, j, kk: (i, j))
        shape = (m, n)
    if epilogue:
        assert not out_stack and not also_bf16
        kinds = {False: (pl.BlockSpec((tm, tn), lambda i, j, kk: (i, j)), (m, n)),
                 True: (pl.BlockSpec((tn, tm), lambda i, j, kk: (j, i)), (n, m)),
                 'pair': (pl.BlockSpec((2, tm, tn), lambda i, j, kk: (0, i, j)), (2, m, n))}
        out_specs = [kinds[t][0] for _, t in epi_outs]
        out_shapes = [jax.ShapeDtypeStruct(kinds[t][1], dt) for dt, t in epi_outs]
    else:
        out_specs = [out_spec] * n_out
        out_shapes = [jax.ShapeDtypeStruct(shape, dt) for dt in [out_dtype, BF16][:n_out]]
    e_pairs = [z if isinstance(z, tuple) else (z, 0) for z in epi_ins]
    assert all(off % tn == 0 for _, off in e_pairs)
    e_specs = [pl.BlockSpec((1, tn) if z.shape[0] == 1 else (tm, tn),
                            lambda i, j, kk, ob=off // tn, row=z.shape[0] == 1: (0 if row else i, j + ob))
               for z, off in e_pairs]
    res = pl.pallas_call(
        body, grid=(m // tm, n // tn, nk),
        in_specs=a_specs + b_specs + e_specs + [pl.BlockSpec(memory_space=pl.ANY)] * len(deps),
        out_specs=out_specs, out_shape=out_shapes,
        scratch_shapes=[pltpu.VMEM((tm, tn), F32)] if nk > 1 else [], name=name, compiler_params=_params(3),
    )(*[a] * fold, *[b] * fold, *[z for z, _ in e_pairs], *deps)
    return res if (also_bf16 or epilogue) else res[0]


def _all_gather(name, arrs, dep=None):
    n = len(arrs)
    deps = [] if dep is None else [dep]

    def body(*refs):
        ins, outs = refs[:n], refs[n + len(deps):2 * n + len(deps)]
        send_sems, recv_sems, local_sems = refs[2 * n + len(deps):]
        x, y, c = _dev()
        me, sib = (x, y, c), (x, y, 1 - c)
        x_nbr, y_nbr, diag = (1 - x, y), (x, 1 - y), (1 - x, 1 - y)
        north = c == 1
        relay_from = (jnp.where(north, 1 - x, x), jnp.where(north, y, 1 - y))
        relay_to = (jnp.where(north, x, 1 - x), jnp.where(north, 1 - y, y))

        def slot(p):
            return 4 * p[0] + 2 * p[1] + p[2]

        def copy(a, k, block, to, src=None):
            dst = outs[a].at[slot(block)]
            return pltpu.make_async_remote_copy(
                src_ref=dst if src is None else src, dst_ref=dst,
                send_sem=send_sems.at[7 * a + k], recv_sem=recv_sems.at[7 * a + k],
                device_id=to, device_id_type=MESH)

        mine = [pltpu.make_async_copy(ins[a], outs[a].at[slot(me)], local_sems.at[a]) for a in range(n)]
        for cp in mine:
            cp.start()
        sent = []
        for a in range(n):
            sent += [copy(a, 0, me, sib, src=ins[a]), copy(a, 1, me, (*x_nbr, c), src=ins[a]),
                     copy(a, 2, me, (*y_nbr, c), src=ins[a])]
        for cp in sent:
            cp.start()
        relays = [copy(a, 3, (*relay_from, c), (*relay_to, c)) for a in range(n)]
        for k, chip, relay_here in ((1, x_nbr, north), (2, y_nbr, jnp.logical_not(north)), (3, diag, None)):
            for a in range(n):
                copy(a, k, (*chip, c), me).wait_recv()
                cp = copy(a, 3 + k, (*chip, c), sib)
                cp.start()
                sent.append(cp)
                if relay_here is not None:
                    pl.when(relay_here)(relays[a].start)
        for a in range(n):
            copy(a, 0, sib, me).wait_recv()
            for k, chip in ((4, x_nbr), (5, y_nbr), (6, diag)):
                copy(a, k, (*chip, 1 - c), me).wait_recv()
        for cp in sent + relays:
            cp.wait_send()
        for cp in mine:
            cp.wait()

    any_spec = pl.BlockSpec(memory_space=pl.ANY)
    return pl.pallas_call(
        body, in_specs=[any_spec] * (n + len(deps)), out_specs=[any_spec] * n,
        out_shape=[jax.ShapeDtypeStruct((N_DEV,) + a.shape, a.dtype) for a in arrs],
        scratch_shapes=[pltpu.SemaphoreType.DMA((7 * n,)), pltpu.SemaphoreType.DMA((7 * n,)),
                        pltpu.SemaphoreType.DMA((n,))],
        name=name,
    )(*arrs, *deps)


FLIPS = [(0, 0, 1), (0, 1, 0), (1, 0, 0), (0, 1, 1), (1, 0, 1), (1, 1, 0), (1, 1, 1)]
N_PEERS = len(FLIPS)
_HBM = pl.BlockSpec(memory_space=pltpu.HBM)
_SEM = pl.BlockSpec(memory_space=pltpu.SEMAPHORE)
_EFFECT = pltpu.SideEffectType.DATAFLOW_SIDE_EFFECTING


def _flip(x, y, c, f):
    return (1 - x if f[0] else x, 1 - y if f[1] else y, 1 - c if f[2] else c)


def _slot(p):
    return 4 * p[0] + 2 * p[1] + p[2]


def _exchange_copies(src_refs, land_refs, send_sems, recv_sems, gather):
    x, y, c = _dev()
    mine = _slot((x, y, c))
    cps = []
    for a, (src, land) in enumerate(zip(src_refs, land_refs)):
        for k, f in enumerate(FLIPS):
            peer = _flip(x, y, c, f)
            cps.append(pltpu.make_async_remote_copy(
                src_ref=src if gather else src.at[_slot(peer)], dst_ref=land.at[mine],
                send_sem=send_sems.at[N_PEERS * a + k], recv_sem=recv_sems.at[N_PEERS * a + k],
                device_id=peer, device_id_type=MESH))
    return cps


def _exchange_start(name, srcs, gather, after):
    n = len(srcs)
    lands = [lax.empty(((N_DEV,) + s.shape) if gather else s.shape, s.dtype) for s in srcs]

    def body(*refs):
        src_refs, land_refs = refs[:n], refs[n:2 * n]
        send_sems, recv_sems, local_sems = refs[2 * n + 1:2 * n + 4]
        token = refs[-1]
        if gather:
            x, y, c = _dev()
            for a in range(n):
                pltpu.make_async_copy(src_refs[a], land_refs[a].at[_slot((x, y, c))], local_sems.at[a]).start()
        for cp in _exchange_copies(src_refs, land_refs, send_sems, recv_sems, gather):
            cp.start()
        token[...] = jnp.zeros_like(token)

    hbm = lambda z: pltpu.HBM(z.shape, z.dtype)
    outs = pl.pallas_call(
        body, name=name,
        out_shape=(pltpu.SemaphoreType.DMA((N_PEERS * n,)), pltpu.SemaphoreType.DMA((N_PEERS * n,)),
                   pltpu.SemaphoreType.DMA((n,)), *[hbm(s) for s in srcs], *[hbm(z) for z in lands],
                   jax.ShapeDtypeStruct((8, LANES), F32)),
        in_specs=[_HBM] * (2 * n) + [pl.BlockSpec(memory_space=pl.ANY)],
        out_specs=(_SEM, _SEM, _SEM, *[_HBM] * (2 * n), pl.BlockSpec(memory_space=pltpu.VMEM)),
        input_output_aliases={i: 3 + i for i in range(2 * n)},
        compiler_params=pltpu.CompilerParams(has_side_effects=_EFFECT),
    )(*[pltpu.with_memory_space_constraint(z, pltpu.HBM) for z in list(srcs) + lands], after)
    return (outs[:3], outs[3:3 + n], outs[3 + n:3 + 2 * n], gather), outs[-1]


def _exchange_wait(name, handles, after):
    sems, srcs, lands, gather = handles
    n = len(srcs)

    def body(*refs):
        src_refs, land_refs = refs[:n], refs[n:2 * n]
        send_sems, recv_sems, local_sems = refs[2 * n:2 * n + 3]
        if gather:
            for a in range(n):
                pltpu.make_async_copy(src_refs[a], land_refs[a].at[0], local_sems.at[a]).wait()
        for cp in _exchange_copies(src_refs, land_refs, send_sems, recv_sems, gather):
            cp.wait_send()
            cp.wait_recv()

    hbm = lambda z: pltpu.HBM(z.shape, z.dtype)
    outs = pl.pallas_call(
        body, name=name, out_shape=tuple(hbm(z) for z in list(srcs) + list(lands)),
        in_specs=[_HBM] * (2 * n) + [_SEM] * 3 + [pl.BlockSpec(memory_space=pl.ANY)],
        out_specs=tuple([_HBM] * (2 * n)), input_output_aliases={i: i for i in range(2 * n)},
        compiler_params=pltpu.CompilerParams(has_side_effects=_EFFECT),
    )(*srcs, *lands, *sems, after)
    return list(outs[n:])


def _pack_rows(sizes, width, row_align):
    offs, r = [], 0
    for s in sizes:
        offs.append(r)
        r += -(-s // width)
    total = -(-r // row_align) * row_align
    return offs, total


def _pack(items, width, row_align, lead=()):
    nl = len(lead)
    sizes = [int(jnp.size(a)) // max(1, functools.reduce(lambda p, q: p * q, lead, 1)) for a in items]
    offs, total = _pack_rows(sizes, width, row_align)
    flat = []
    used = 0
    for a, s in zip(items, sizes):
        f = a.reshape(lead + (s,))
        pad = -(-s // width) * width - s
        if pad:
            f = jnp.pad(f, [(0, 0)] * nl + [(0, pad)])
        flat.append(f)
        used += s + pad
    tail = total * width - used
    if tail:
        flat.append(jnp.zeros(lead + (tail,), items[0].dtype))
    return jnp.concatenate(flat, axis=-1).reshape(lead + (total, width)), offs


def _unpack(packed, off, shape, lead=()):
    nl = len(lead)
    size = functools.reduce(lambda p, q: p * q, shape, 1)
    width = packed.shape[-1]
    rows = -(-size // width)
    blk = lax.slice_in_dim(packed, off, off + rows, axis=nl).reshape(lead + (rows * width,))
    return lax.slice_in_dim(blk, 0, size, axis=nl).reshape(lead + tuple(shape))


def _rms(x, g):
    return (x * lax.rsqrt(jnp.mean(x * x, axis=-1, keepdims=True) + RMS_EPS)) * g


def _norm_mod(x, g, sc, sh):
    return _rms(x, g) * (1.0 + sc) + sh


def _mix_fn(glu_a, glu_b, attn, ga, gs):
    return jax.nn.sigmoid(ga) * attn + jax.nn.sigmoid(gs) * (glu_a * jax.nn.sigmoid(glu_b))


def _s5_disc_fn(a_re, a_im, log_dt, b_re, b_im):
    dt = jnp.exp(log_dt)
    mag = jnp.exp(a_re * dt)
    lr, li = mag * jnp.cos(a_im * dt), mag * jnp.sin(a_im * dt)
    den = a_re * a_re + a_im * a_im
    zr = ((lr - 1.0) * a_re + li * a_im) / den
    zi = (li * a_re - (lr - 1.0) * a_im) / den
    return lr, li, zr[None] * b_re - zi[None] * b_im, zr[None] * b_im + zi[None] * b_re


def _adamw_fn(w, g, m, v):
    m = ADAM_B1 * m + (1.0 - ADAM_B1) * g
    v = ADAM_B2 * v + (1.0 - ADAM_B2) * jnp.square(g)
    m_hat = m / (1.0 - ADAM_B1 ** ADAM_STEP)
    v_hat = v / (1.0 - ADAM_B2 ** ADAM_STEP)
    delta = -ADAM_LR * (m_hat / (jnp.sqrt(v_hat) + ADAM_EPS) + ADAM_WD * w)
    return delta, m, v


def _adamw(name, parts, w, m, v):
    p, r, c = parts.shape
    tr = _pick(r, max(8, (1 << 21) // (4 * c * max(p, 2))), 8)

    def fn(pv, wv, mv, vv):
        g = pv[0]
        for i in range(1, p):
            g = g + pv[i]
        d, m2, v2 = _adamw_fn(wv, g, mv, vv)
        return g, d, m2, v2

    spec = pl.BlockSpec((tr, c), lambda i: (i, 0))
    return _tile_call(
        name, fn, (r // tr,), [parts, w, m, v],
        [pl.BlockSpec((p, tr, c), lambda i: (0, i, 0)), spec, spec, spec],
        [jax.ShapeDtypeStruct((r, c), F32)] * 4, [spec] * 4)


def _adamw_sharded(name, parts, own_src, w, m, v, place):
    _, k, n = parts.shape
    tr = _pick(k, max(16, (1 << 19) // (4 * n)), 16)

    def body(pl_ref, p_ref, a_ref, w_ref, m_ref, v_ref, g_ref, d_ref, m2_ref, v2_ref):
        own = a_ref[0]
        g = None
        for q in range(N_DEV):
            term = jnp.where(pl_ref[0] == q, own, p_ref[q].astype(F32))
            g = term if g is None else g + term
        d, m2, v2 = _adamw_fn(w_ref[...], g, m_ref[...], v_ref[...])
        g_ref[...] = g
        d_ref[...] = d
        m2_ref[...] = m2
        v2_ref[...] = v2

    spec = pl.BlockSpec((tr, n), lambda i, pr: (i, 0))
    return pl.pallas_call(
        body,
        grid_spec=pltpu.PrefetchScalarGridSpec(
            num_scalar_prefetch=1, grid=(k // tr,),
            in_specs=[pl.BlockSpec((N_DEV, tr, n), lambda i, pr: (0, i, 0)),
                      pl.BlockSpec((1, tr, n), lambda i, pr: (pr[1], i, 0)),
                      spec, spec, spec],
            out_specs=[spec] * 4),
        out_shape=[jax.ShapeDtypeStruct((k, n), F32)] * 4, name=name, compiler_params=_params(1),
    )(place, parts, own_src, w, m, v)


def _attn_mask(n, rows):
    qi = lax.broadcasted_iota(jnp.int32, (rows, 2 * ATT_BLOCK), 0) & (ATT_BLOCK - 1)
    kj = lax.broadcasted_iota(jnp.int32, (rows, 2 * ATT_BLOCK), 1)
    rel = qi + ATT_BLOCK - kj
    return (rel >= 0) & (rel < ATT_BLOCK) & ((kj >= ATT_BLOCK) | (n > 0))


def _attn_probs(q, k, sink, mask):
    s = lax.dot_general(q, k, (((1,), (1,)), ((), ())), preferred_element_type=F32) * (HEAD_DIM ** -0.5)
    s = jnp.where(mask, s, NEG_INF)
    m = jnp.maximum(jnp.max(s, axis=-1, keepdims=True), sink)
    p = jnp.exp(s - m)
    e_sink = jnp.exp(sink - m)
    inv = 1.0 / (jnp.sum(p, axis=-1, keepdims=True) + e_sink)
    return p * inv, e_sink * inv


def _attn_specs(qpk):
    blk = ATT_BLOCK
    q_spec = pl.BlockSpec((qpk, blk, HEAD_DIM), lambda h, n: (h, n, 0))
    cur = pl.BlockSpec((1, blk, HEAD_DIM), lambda h, n: (h, n, 0))
    prev = pl.BlockSpec((1, blk, HEAD_DIM), lambda h, n: (h, jnp.maximum(n - 1, 0), 0))
    sink_spec = pl.BlockSpec((1, qpk * blk, 1), lambda h, n: (h, 0, 0))
    return q_spec, cur, prev, sink_spec


def _attn_fwd(q, k, v, sinks):
    hq, l, _ = q.shape
    qpk = hq // N_KV_HEADS
    nb = l // ATT_BLOCK
    rows = qpk * ATT_BLOCK
    q_spec, cur, prev, sink_spec = _attn_specs(qpk)

    def body(q_ref, kp_ref, kc_ref, vp_ref, vc_ref, sink_ref, o_ref):
        mask = _attn_mask(pl.program_id(1), rows)
        kk = jnp.concatenate([kp_ref[0], kc_ref[0]], axis=0).astype(BF16)
        vv = jnp.concatenate([vp_ref[0], vc_ref[0]], axis=0).astype(BF16)
        p, _ = _attn_probs(q_ref[...].reshape(rows, HEAD_DIM).astype(BF16), kk, sink_ref[0], mask)
        o = jnp.dot(p.astype(BF16), vv, preferred_element_type=F32)
        o_ref[...] = o.reshape(qpk, ATT_BLOCK, HEAD_DIM).astype(o_ref.dtype)

    return pl.pallas_call(
        body, grid=(N_KV_HEADS, nb), in_specs=[q_spec, prev, cur, prev, cur, sink_spec],
        out_specs=q_spec, out_shape=jax.ShapeDtypeStruct((hq, l, HEAD_DIM), BF16),
        name="attn_fwd", compiler_params=_params(2),
    )(q, k, k, v, v, sinks)


def _attn_bwd(q, k, v, sinks, do):
    hq, l, _ = q.shape
    qpk = hq // N_KV_HEADS
    nb = l // ATT_BLOCK
    blk = ATT_BLOCK
    rows = qpk * blk
    q_spec, cur, prev, sink_spec = _attn_specs(qpk)
    part_spec = pl.BlockSpec((1, 1, 2 * blk, HEAD_DIM), lambda h, n: (h, n, 0, 0))
    dsink_spec = pl.BlockSpec((qpk, 1, LANES), lambda h, n: (h, 0, 0))
    tn = (((0,), (0,)), ((), ()))

    def body(q_ref, do_ref, kp_ref, kc_ref, vp_ref, vc_ref, sink_ref, dq_ref, dkp_ref, dvp_ref, dsink_ref):
        n = pl.program_id(1)
        mask = _attn_mask(n, rows)
        kk = jnp.concatenate([kp_ref[0], kc_ref[0]], axis=0).astype(BF16)
        vv = jnp.concatenate([vp_ref[0], vc_ref[0]], axis=0).astype(BF16)
        qb = q_ref[...].reshape(rows, HEAD_DIM).astype(BF16)
        do32 = do_ref[...].astype(F32).reshape(rows, HEAD_DIM)
        dob = do32.astype(BF16)
        p, p_sink = _attn_probs(qb, kk, sink_ref[0], mask)
        pb = p.astype(BF16)
        o = jnp.dot(pb, vv, preferred_element_type=F32)
        delta = jnp.sum(do32 * o, axis=-1, keepdims=True)
        dp = lax.dot_general(dob, vv, (((1,), (1,)), ((), ())), preferred_element_type=F32)
        ds = (p * (dp - delta) * (HEAD_DIM ** -0.5)).astype(BF16)
        dq = jnp.dot(ds, kk, preferred_element_type=F32)
        dq_ref[...] = dq.reshape(qpk, blk, HEAD_DIM).astype(dq_ref.dtype)
        dkp_ref[0, 0] = lax.dot_general(ds, qb, tn, preferred_element_type=F32)
        dvp_ref[0, 0] = lax.dot_general(pb, dob, tn, preferred_element_type=F32)
        dsr = p_sink * delta
        for g in range(qpk):
            dsg = jnp.broadcast_to(-_colsum(dsr[g * blk:(g + 1) * blk]), (1, LANES))

            @pl.when(n == 0)
            def _():
                dsink_ref[g] = dsg

            @pl.when(n > 0)
            def _():
                dsink_ref[g] += dsg


    part_shape = jax.ShapeDtypeStruct((N_KV_HEADS, nb, 2 * blk, HEAD_DIM), F32)
    dq, dkp, dvp, dsink = pl.pallas_call(
        body, grid=(N_KV_HEADS, nb), in_specs=[q_spec, q_spec, prev, cur, prev, cur, sink_spec],
        out_specs=[q_spec, part_spec, part_spec, dsink_spec],
        out_shape=[jax.ShapeDtypeStruct((hq, l, HEAD_DIM), BF16), part_shape, part_shape,
                   jax.ShapeDtypeStruct((hq, 1, LANES), F32)],
        name="attn_bwd", compiler_params=_params(2),
    )(q, do, k, k, v, v, sinks)

    def combine(a_cur, a_nxt, b_cur, b_nxt):
        last = pl.program_id(1) == nb - 1
        keep = jnp.where(last, 0.0, 1.0)
        return (a_cur[0, 0, blk:] + keep * a_nxt[0, 0, :blk])[None], (b_cur[0, 0, blk:] + keep * b_nxt[0, 0, :blk])[None]

    nxt_spec = pl.BlockSpec((1, 1, 2 * blk, HEAD_DIM), lambda h, n: (h, jnp.minimum(n + 1, nb - 1), 0, 0))
    kv_shape = jax.ShapeDtypeStruct((N_KV_HEADS, l, HEAD_DIM), BF16)
    dk, dv = _tile_call("attn_dkv", combine, (N_KV_HEADS, nb), [dkp, dkp, dvp, dvp],
                        [part_spec, nxt_spec, part_spec, nxt_spec], [kv_shape, kv_shape], [cur, cur])
    return dq, dk, dv, dsink


def _block_diag(m):
    j, gl, a, b = m.shape
    eye = jnp.eye(gl, dtype=m.dtype)
    return (m[:, :, :, None, :] * eye[None, :, None, :, None]).reshape(j, gl * a, gl * b)


def _diag_blocks(z, a):
    j = z.shape[0]
    gl = z.shape[1] // a
    b = z.shape[2] // gl
    d = jnp.diagonal(z.reshape(j, gl, a, gl, b), axis1=1, axis2=3)
    return d.transpose(0, 3, 1, 2)


def _s5_permute(src_ref, dst_ref, t_len):
    seg = t_len // 8
    for k in range(seg):
        dst_ref[8 * k:8 * k + 8, :] = src_ref[pl.ds(k, 8, stride=seg), :]


def _s5_unpermute(perm_ref, t_len, emit):
    per_seg = t_len // 64
    for m in range(t_len // 8):
        emit(8 * m, perm_ref[pl.ds(64 * (m % per_seg) + m // per_seg, 8, stride=8), :])


def _s5_powers(p_ref, lr, li, seg):
    hs = TILE_STATES

    def step(k, carry):
        pr, pi = carry
        p_ref[pl.ds(k, 1), 0:hs] = pr
        p_ref[pl.ds(k, 1), hs:2 * hs] = pi
        return lr * pr - li * pi, lr * pi + li * pr

    lax.fori_loop(0, seg, step, (lr, li))


def _s5_local_scan(x_ref, base, lr, li, seg, reverse):
    hs = TILE_STATES
    lr8, li8 = jnp.broadcast_to(lr, (8, hs)), jnp.broadcast_to(li, (8, hs))
    if reverse:
        li8 = -li8

    def step(i, carry):
        hr, hi = carry
        k = seg - 1 - i if reverse else i
        rows = pl.ds(pl.multiple_of(base + 8 * k, 8), 8)
        nr = lr8 * hr - li8 * hi + x_ref[rows, 0:hs]
        ni = lr8 * hi + li8 * hr + x_ref[rows, hs:2 * hs]
        x_ref[rows, 0:hs] = nr
        x_ref[rows, hs:2 * hs] = ni
        return nr, ni

    zero = jnp.zeros((8, hs), F32)
    return lax.fori_loop(0, seg, step, (zero, zero), unroll=2)


def _s5_carries(c_ref, e_ref, ends, start, pw_r, pw_i, reverse):
    hs = TILE_STATES
    e_ref[:, 0:hs] = ends[0]
    e_ref[:, hs:2 * hs] = ends[1]
    cr, ci = start
    if reverse:
        pw_i = -pw_i
    for s in (range(7, -1, -1) if reverse else range(8)):
        c_ref[s:s + 1, 0:hs] = cr
        c_ref[s:s + 1, hs:2 * hs] = ci
        er, ei = e_ref[s:s + 1, 0:hs], e_ref[s:s + 1, hs:2 * hs]
        cr, ci = er + pw_r * cr - pw_i * ci, ei + pw_r * ci + pw_i * cr
    return cr, ci


def _s5_states(u_perm_b16, bd_ref, x_ref, base, c_ref, e_ref, p_ref, lr, li, h_in, t_len):
    hs = TILE_STATES
    seg = t_len // 8
    x_ref[pl.ds(base, t_len), :] = jnp.dot(u_perm_b16, bd_ref[0], preferred_element_type=F32)
    ends = _s5_local_scan(x_ref, base, lr, li, seg, False)
    pw_r, pw_i = p_ref[seg - 1:seg, 0:hs], p_ref[seg - 1:seg, hs:2 * hs]
    h_out = _s5_carries(c_ref, e_ref, ends, h_in, pw_r, pw_i, False)
    cr, ci = c_ref[:, 0:hs], c_ref[:, hs:2 * hs]

    def fix(k, carry):
        rows = pl.ds(pl.multiple_of(base + 8 * k, 8), 8)
        pr, pi = p_ref[pl.ds(k, 1), 0:hs], p_ref[pl.ds(k, 1), hs:2 * hs]
        x_ref[rows, 0:hs] += pr * cr - pi * ci
        x_ref[rows, hs:2 * hs] += pr * ci + pi * cr
        return carry

    lax.fori_loop(0, seg, fix, 0, unroll=2)
    return h_out


def _s5_fwd(proj, u_off, bd, cbd, lam, dvec, t_len):
    l = proj.shape[0]
    nj = bd.shape[0]
    nch = l // t_len
    hs = TILE_STATES
    ub = u_off // LANES
    seg = t_len // 8
    assert t_len % 64 == 0

    def body(u_ref, bd_ref, cbd_ref, lam_ref, d_ref, y_ref, hst_ref, x_ref, h_ref, p_ref, c_ref, e_ref, up_ref, yp_ref):
        lr, li = lam_ref[0, 0:1, :], lam_ref[0, 1:2, :]

        @pl.when(pl.program_id(1) == 0)
        def _():
            h_ref[...] = jnp.zeros_like(h_ref)
            _s5_powers(p_ref, lr, li, seg)

        hst_ref[0, 0] = h_ref[...]
        _s5_permute(u_ref, up_ref, t_len)
        h_out = _s5_states(up_ref[...].astype(BF16), bd_ref, x_ref, 0, c_ref, e_ref, p_ref, lr, li,
                           (h_ref[:, 0:hs], h_ref[:, hs:2 * hs]), t_len)
        h_ref[:, 0:hs] = h_out[0]
        h_ref[:, hs:2 * hs] = h_out[1]
        yp_ref[...] = jnp.dot(x_ref[...].astype(BF16), cbd_ref[0], preferred_element_type=F32)
        dv = d_ref[0]

        def out(r0, rows):
            y_ref[r0:r0 + 8, :] = rows + dv * u_ref[r0:r0 + 8, :]

        _s5_unpermute(yp_ref, t_len, out)

    return pl.pallas_call(
        body, grid=(nj, nch),
        in_specs=[pl.BlockSpec((t_len, LANES), lambda j, c: (c, ub + j)),
                  pl.BlockSpec((1, LANES, 2 * hs), lambda j, c: (j, 0, 0)),
                  pl.BlockSpec((1, 2 * hs, LANES), lambda j, c: (j, 0, 0)),
                  pl.BlockSpec((1, 2, hs), lambda j, c: (j, 0, 0)),
                  pl.BlockSpec((1, 1, LANES), lambda j, c: (j, 0, 0))],
        out_specs=[pl.BlockSpec((t_len, LANES), lambda j, c: (c, j)),
                   pl.BlockSpec((1, 1, 1, 2 * hs), lambda j, c: (j, c, 0, 0))],
        out_shape=[jax.ShapeDtypeStruct((l, nj * LANES), F32),
                   jax.ShapeDtypeStruct((nj, nch, 1, 2 * hs), F32)],
        scratch_shapes=[pltpu.VMEM((t_len, 2 * hs), F32), pltpu.VMEM((1, 2 * hs), F32),
                        pltpu.VMEM((seg, 2 * hs), F32), pltpu.VMEM((8, 2 * hs), F32), pltpu.VMEM((8, 2 * hs), F32),
                        pltpu.VMEM((t_len, LANES), F32), pltpu.VMEM((t_len, LANES), F32)],
        name="s5_fwd", compiler_params=_params(2),
    )(proj, bd, cbd, lam, dvec)


def _s5_bwd(proj, u_off, dy, hst, bd, bdt, cbdt, lam, dvec, t_len):
    l = proj.shape[0]
    nj = bd.shape[0]
    nch = l // t_len
    hs = TILE_STATES
    ub = u_off // LANES
    seg = t_len // 8
    tn = (((0,), (0,)), ((), ()))
    assert t_len % 64 == 0

    def body(u_ref, dy_ref, hst_ref, bd_ref, bdt_ref, cbdt_ref, lam_ref, d_ref,
             du_ref, dbd_ref, dcbdt_ref, dlam_ref, dd_ref,
             x_ref, g_ref, gc_ref, p_ref, c_ref, e_ref, up_ref, dyp_ref, dup_ref):
        first = pl.program_id(1) == 0
        lr, li = lam_ref[0, 0:1, :], lam_ref[0, 1:2, :]

        @pl.when(first)
        def _():
            gc_ref[...] = jnp.zeros_like(gc_ref)
            _s5_powers(p_ref, lr, li, seg)

        _s5_permute(u_ref, up_ref, t_len)
        _s5_permute(dy_ref, dyp_ref, t_len)
        ub16, dyb16 = up_ref[...].astype(BF16), dyp_ref[...].astype(BF16)
        h0 = hst_ref[0, 0]
        _s5_states(ub16, bd_ref, x_ref, 8, c_ref, e_ref, p_ref, lr, li, (h0[:, 0:hs], h0[:, hs:2 * hs]), t_len)
        x_ref[0:8, :] = c_ref[...]
        g_ref[...] = jnp.dot(dyb16, cbdt_ref[0], preferred_element_type=F32)
        starts = _s5_local_scan(g_ref, 0, lr, li, seg, True)
        pw_r, pw_i = p_ref[seg - 1:seg, 0:hs], p_ref[seg - 1:seg, hs:2 * hs]
        g_out = _s5_carries(c_ref, e_ref, starts, (gc_ref[:, 0:hs], gc_ref[:, hs:2 * hs]), pw_r, pw_i, True)
        gc_ref[:, 0:hs] = g_out[0]
        gc_ref[:, hs:2 * hs] = g_out[1]
        cr, ci = c_ref[:, 0:hs], c_ref[:, hs:2 * hs]

        def fix(k, carry):
            alr, ali = carry
            rows = pl.ds(pl.multiple_of(8 * k, 8), 8)
            pr, pi = p_ref[pl.ds(seg - 1 - k, 1), 0:hs], p_ref[pl.ds(seg - 1 - k, 1), hs:2 * hs]
            gr = g_ref[rows, 0:hs] + pr * cr + pi * ci
            gi = g_ref[rows, hs:2 * hs] + pr * ci - pi * cr
            g_ref[rows, 0:hs] = gr
            g_ref[rows, hs:2 * hs] = gi
            hpr, hpi = x_ref[rows, 0:hs], x_ref[rows, hs:2 * hs]
            return alr + gr * hpr + gi * hpi, ali + gi * hpr - gr * hpi

        zero = jnp.zeros((8, hs), F32)
        alr, ali = lax.fori_loop(0, seg, fix, (zero, zero), unroll=2)
        alr, ali = _colsum(alr), _colsum(ali)
        g = g_ref[...].astype(BF16)
        h = x_ref[pl.ds(8, t_len), :].astype(BF16)
        dup_ref[...] = jnp.dot(g, bdt_ref[0], preferred_element_type=F32)
        dv = d_ref[0]

        def out(r0, rows):
            du_ref[r0:r0 + 8, :] = (rows + dv * dy_ref[r0:r0 + 8, :]).astype(du_ref.dtype)

        _s5_unpermute(dup_ref, t_len, out)
        sign = jnp.where(lax.broadcasted_iota(jnp.int32, (1, 2 * hs), 1) < hs, 1.0, -1.0)
        dbd = lax.dot_general(ub16, g, tn, preferred_element_type=F32)
        dcbdt = lax.dot_general(dyb16, h, tn, preferred_element_type=F32) * sign
        ddv = _colsum(dy_ref[...] * u_ref[...])

        @pl.when(first)
        def _():
            dbd_ref[0] = dbd
            dcbdt_ref[0] = dcbdt
            dlam_ref[0, 0:1, :] = alr
            dlam_ref[0, 1:2, :] = ali
            dd_ref[0] = ddv

        @pl.when(jnp.logical_not(first))
        def _():
            dbd_ref[0] += dbd
            dcbdt_ref[0] += dcbdt
            dlam_ref[0, 0:1, :] += alr
            dlam_ref[0, 1:2, :] += ali
            dd_ref[0] += ddv

    rev = lambda c: nch - 1 - c
    wide = pl.BlockSpec((1, LANES, 2 * hs), lambda j, c: (j, 0, 0))
    tall = pl.BlockSpec((1, 2 * hs, LANES), lambda j, c: (j, 0, 0))
    return pl.pallas_call(
        body, grid=(nj, nch),
        in_specs=[pl.BlockSpec((t_len, LANES), lambda j, c: (rev(c), ub + j)),
                  pl.BlockSpec((t_len, LANES), lambda j, c: (rev(c), j)),
                  pl.BlockSpec((1, 1, 1, 2 * hs), lambda j, c: (j, rev(c), 0, 0)),
                  wide, tall, wide,
                  pl.BlockSpec((1, 2, hs), lambda j, c: (j, 0, 0)),
                  pl.BlockSpec((1, 1, LANES), lambda j, c: (j, 0, 0))],
        out_specs=[pl.BlockSpec((t_len, LANES), lambda j, c: (rev(c), j)),
                   wide, wide,
                   pl.BlockSpec((1, 2, hs), lambda j, c: (j, 0, 0)),
                   pl.BlockSpec((1, 1, LANES), lambda j, c: (j, 0, 0))],
        out_shape=[jax.ShapeDtypeStruct((l, nj * LANES), BF16),
                   jax.ShapeDtypeStruct((nj, LANES, 2 * hs), F32),
                   jax.ShapeDtypeStruct((nj, LANES, 2 * hs), F32),
                   jax.ShapeDtypeStruct((nj, 2, hs), F32),
                   jax.ShapeDtypeStruct((nj, 1, LANES), F32)],
        scratch_shapes=[pltpu.VMEM((t_len + 8, 2 * hs), F32), pltpu.VMEM((t_len, 2 * hs), F32),
                        pltpu.VMEM((1, 2 * hs), F32), pltpu.VMEM((seg, 2 * hs), F32),
                        pltpu.VMEM((8, 2 * hs), F32), pltpu.VMEM((8, 2 * hs), F32),
                        pltpu.VMEM((t_len, LANES), F32), pltpu.VMEM((t_len, LANES), F32),
                        pltpu.VMEM((t_len, LANES), F32)],
        name="s5_bwd", compiler_params=_params(2),
    )(proj, dy, hst, bd, bdt, cbdt, lam, dvec)


def _full_spec(shape):
    nd = len(shape)
    return pl.BlockSpec(tuple(shape), lambda i: (0,) * nd)


def _sds(shape, dtype=F32):
    return jax.ShapeDtypeStruct(tuple(shape), dtype)


def kernel(x, c, ada_w, ada_b, norm_mix_g, w_in, attn_sinks, w_attn_proj, ssm_a_re, ssm_a_im, ssm_log_dt, ssm_b_re, ssm_b_im, ssm_c_re, ssm_c_im, ssm_d, w_ssm_glu, w_out, norm_ffn_g, w_ffn_up, ffn_conv_w, ffn_conv_b, w_ffn_down, final_g, loss_target, m_ada_w, m_ada_b, m_norm_mix_g, m_w_in, m_attn_sinks, m_w_attn_proj, m_ssm_a_re, m_ssm_a_im, m_ssm_log_dt, m_ssm_b_re, m_ssm_b_im, m_ssm_c_re, m_ssm_c_im, m_ssm_d, m_w_ssm_glu, m_w_out, m_norm_ffn_g, m_w_ffn_up, m_ffn_conv_w, m_ffn_conv_b, m_w_ffn_down, m_final_g, v_ada_w, v_ada_b, v_norm_mix_g, v_w_in, v_attn_sinks, v_w_attn_proj, v_ssm_a_re, v_ssm_a_im, v_ssm_log_dt, v_ssm_b_re, v_ssm_b_im, v_ssm_c_re, v_ssm_c_im, v_ssm_d, v_w_ssm_glu, v_w_out, v_norm_ffn_g, v_w_ffn_up, v_ffn_conv_w, v_ffn_conv_b, v_w_ffn_down, v_final_g):
    given = dict(locals())
    names = ['ada_w', 'ada_b', 'norm_mix_g', 'w_in', 'attn_sinks', 'w_attn_proj', 'ssm_a_re', 'ssm_a_im',
             'ssm_log_dt', 'ssm_b_re', 'ssm_b_im', 'ssm_c_re', 'ssm_c_im', 'ssm_d', 'w_ssm_glu', 'w_out',
             'norm_ffn_g', 'w_ffn_up', 'ffn_conv_w', 'ffn_conv_b', 'w_ffn_down', 'final_g']

    xs = x[0]
    tgt = loss_target[0]
    l, d = xs.shape
    attn_w = w_attn_proj.shape[1]
    ssm_w = w_ssm_glu.shape[1]
    hq = attn_sinks.shape[1]
    qpk = hq // N_KV_HEADS
    kv_w = N_KV_HEADS * HEAD_DIM
    n_groups = ssm_a_re.shape[1]
    dff = ffn_conv_b.shape[1]
    in_w = attn_w + 2 * kv_w + ssm_w + 2 * d
    nj = ssm_w // LANES
    off_k, off_v, off_u = attn_w, attn_w + kv_w, attn_w + 2 * kv_w
    off_ga, off_gs = off_u + ssm_w, off_u + ssm_w + d
    assert hq * HEAD_DIM == attn_w and n_groups * SSM_P == ssm_w and l % ATT_BLOCK == 0

    xi, yi, ci = _dev()
    idx = 4 * xi + 2 * yi + ci

    row_sharded = {'w_out': (d, d), 'w_ffn_down': (dff, d)}
    big = ['w_in', 'w_attn_proj', 'w_ssm_glu', 'w_out', 'w_ffn_up', 'w_ffn_down']
    spack, s_offs = _pack([c, ffn_conv_w[0]], LANES, 8)
    w16 = {k: given[k][0].astype(BF16) for k in big}
    wg_in, sg = _all_gather("gather_first", [w16['w_in'], spack])
    mixer_w = ['w_attn_proj', 'w_ssm_glu', 'w_out']
    h_mixer, tok = _exchange_start("gather_mixer_start", [w16[k] for k in mixer_w], True, wg_in)
    h_up, tok = _exchange_start("gather_ffn_up_start", [w16['w_ffn_up']], True, tok)
    h_down, tok = _exchange_start("gather_ffn_down_start", [w16['w_ffn_down']], True, tok)
    full = {'w_in': wg_in.transpose(1, 0, 2).reshape(d, in_w)}
    c_all = _unpack(sg, s_offs[0], (d,), lead=(N_DEV,))
    conv_w = _unpack(sg, s_offs[1], ffn_conv_w.shape[1:], lead=(N_DEV,)).transpose(1, 0, 2).reshape(3, dff)
    conv_b = ffn_conv_b

    mod_n = ada_w.shape[2]
    tcm = _pick(mod_n, 512)
    ada_b_mine = lax.dynamic_slice_in_dim(ada_b, idx * mod_n, mod_n, axis=1)

    def modpart_fn(cv, wv, bv):
        cond = cv * jax.nn.sigmoid(cv)
        return jnp.dot(cond.astype(BF16), wv.astype(BF16), preferred_element_type=F32) + bv, cond

    modp, cond_all = _tile_call(
        "ada_rows", modpart_fn, (mod_n // tcm,), [c_all, ada_w[0], ada_b_mine],
        [pl.BlockSpec((N_DEV, d), lambda j: (0, 0)), pl.BlockSpec((d, tcm), lambda j: (0, j)),
         pl.BlockSpec((1, tcm), lambda j: (0, j))],
        [_sds((N_DEV, mod_n)), _sds((N_DEV, d))],
        [pl.BlockSpec((N_DEV, tcm), lambda j: (0, j)), pl.BlockSpec((N_DEV, d), lambda j: (0, 0))])
    (modg,) = _all_gather("gather_ada_rows", [modp])
    mod = lax.dynamic_index_in_dim(modg, idx, axis=1, keepdims=False).reshape(1, N_DEV * mod_n)
    sh1, sc1, g1, sh2, sc2, g2 = [mod[:, i * d:(i + 1) * d] for i in range(6)]

    tr = _pick(l, 256, 8)
    trh = _pick(l, 256, 8)
    nr, nrh = l // tr, l // trh
    g_mix, g_ffn, g_fin = norm_mix_g + tok[0:1, 0:1], norm_ffn_g, final_g.reshape(1, d)

    def with_t(fn):
        def wrapped(*vals):
            out = fn(*vals)
            out = out if isinstance(out, tuple) else (out,)
            return out + (out[-1].T,)
        return wrapped

    h1, h1_t = _tile_call("norm_mod_mix", with_t(_norm_mod), (1, nr), [xs, g_mix, sc1, sh1],
                          [_t(tr, d), _v(d), _v(d), _v(d)], [_sds((l, d), BF16), _sds((d, l), BF16)],
                          [_t(tr, d), _tt(tr, d)])
    proj = _matmul("proj_in", h1, full['w_in'], "nn", tn=1280)

    def heads(z, n):
        return z.reshape(l, n, HEAD_DIM).transpose(1, 0, 2)

    qh = heads(proj[:, :attn_w], hq)
    kh = heads(proj[:, off_k:off_k + kv_w], N_KV_HEADS)
    vh = heads(proj[:, off_v:off_v + kv_w], N_KV_HEADS)
    sinks3 = jnp.repeat(attn_sinks.reshape(N_KV_HEADS, qpk), ATT_BLOCK, axis=1)[..., None]
    o_h = _attn_fwd(qh, kh, vh, sinks3)
    o2 = o_h.transpose(1, 0, 2).reshape(l, attn_w)

    gn = (n_groups, SSM_N)
    pgn = (SSM_P, n_groups, SSM_N)
    a_re, a_im, log_dt = ssm_a_re[0], ssm_a_im[0], ssm_log_dt[0].reshape(n_groups, 1)
    b_re, b_im = ssm_b_re[0].transpose(2, 0, 1), ssm_b_im[0].transpose(2, 0, 1)
    disc_ins = [a_re, a_im, log_dt, b_re, b_im]
    disc_specs = [_full_spec(gn), _full_spec(gn), _full_spec((n_groups, 1)), _full_spec(pgn), _full_spec(pgn)]
    lam_r, lam_i, bb_r, bb_i = _tile_call(
        "s5_discretise", _s5_disc_fn, (1,), disc_ins, disc_specs,
        [_sds(gn), _sds(gn), _sds(pgn), _sds(pgn)],
        [_full_spec(gn), _full_spec(gn), _full_spec(pgn), _full_spec(pgn)])

    def tiles_gpn(z):
        return z.reshape(SSM_P, nj, TILE_GROUPS, SSM_N).transpose(1, 2, 0, 3)

    bd = jnp.concatenate([_block_diag(tiles_gpn(bb_r)), _block_diag(tiles_gpn(bb_i))], axis=2).astype(BF16)
    c_r = ssm_c_re[0].reshape(nj, TILE_GROUPS, SSM_P, SSM_N).transpose(0, 1, 3, 2)
    c_i = (-ssm_c_im[0]).reshape(nj, TILE_GROUPS, SSM_P, SSM_N).transpose(0, 1, 3, 2)
    cbd = jnp.concatenate([_block_diag(c_r), _block_diag(c_i)], axis=1).astype(BF16)
    bdt, cbdt = bd.transpose(0, 2, 1), cbd.transpose(0, 2, 1)
    lam = jnp.stack([lam_r.reshape(nj, TILE_STATES), lam_i.reshape(nj, TILE_STATES)], axis=1)
    dvec = ssm_d[0].reshape(nj, 1, LANES)
    t_len = _pick(l, 1024, 8)
    y, hst = _s5_fwd(proj, off_u, bd, cbd, lam, dvec, t_len)

    tcs, trg = _pick(ssm_w, 1024), _pick(l, 512, 8)
    gy = _tile_call("gelu", lambda v: jax.nn.gelu(v), (ssm_w // tcs, l // trg), [y], [_t(trg, tcs)],
                    [_sds((l, ssm_w), BF16)], [_t(trg, tcs)])[0]
    full.update(zip(mixer_w, _exchange_wait("gather_mixer_wait", h_mixer, gy)))
    full['w_out'] = full['w_out'].reshape(row_sharded['w_out'])
    full['w_attn_proj'] = full['w_attn_proj'].transpose(1, 0, 2).reshape(attn_w, d)
    full['w_ssm_glu'] = full['w_ssm_glu'].transpose(1, 0, 2).reshape(ssm_w, 2 * d)
    glu = _matmul("ssm_glu", gy, full['w_ssm_glu'], "nn")

    tcd = 256 if d % 256 == 0 and off_ga % 256 == 0 else LANES
    assert d % tcd == 0 and off_ga % tcd == 0 and off_gs % tcd == 0
    gate_ins = [(glu, 0), (glu, d), (proj, off_ga), (proj, off_gs)]

    def mix_epilogue(at, ga_, gb_, pa, ps):
        return at, _mix_fn(ga_, gb_, at, pa, ps)

    attn, mixed = _matmul("attn_proj_gate_mix", o2, full['w_attn_proj'], "nn", tn=tcd,
                          epilogue=(mix_epilogue, gate_ins, [(F32, False), (BF16, False)]))
    def res_norm_fn(xv, mo, g1v, gv, scv, shv):
        x2v = xv + g1v * mo
        return x2v, _norm_mod(x2v, gv, scv, shv)

    def res_norm_epilogue(mo, xv, g1v, gv, scv, shv):
        x2v, h2v = res_norm_fn(xv, mo, g1v, gv, scv, shv)
        return mo, x2v, h2v, h2v.T

    mixout, x2, h2, h2_t = _matmul(
        "mix_out_residual_norm_mod_ffn", mixed, full['w_out'], "nn", tm=256, tn=d,
        epilogue=(res_norm_epilogue, [xs, g1, g_ffn, sc2, sh2], [(F32, False), (F32, False), (BF16, False), (BF16, True)]))
    full['w_ffn_up'], = _exchange_wait("gather_ffn_up_wait", h_up, h2)
    up = _matmul("ffn_up", h2, full['w_ffn_up'], "nn", out_dtype=BF16, tn=1408)

    tcf, trc = _pick(dff, 1408), _pick(l, 512, 8)
    assert dff % tcf == 0
    ncf = dff // tcf

    taps = [conv_w[i:i + 1] for i in range(3)]

    def conv_gate(gp, gp_prev, w0, w1, w2, bv):
        gp = gp.astype(F32)
        prev = jnp.where(pl.program_id(1) == 0, 0.0, 1.0) * gp_prev.astype(F32)
        ext = jnp.concatenate([prev, gp], axis=0)
        m1 = pltpu.roll(ext, 1, 0)[HALO:]
        m2 = pltpu.roll(ext, 2, 0)[HALO:]
        return w0 * m2 + w1 * m1 + w2 * gp + bv, m1, m2

    def convglu_fn(gp, gp_prev, val, w0, w1, w2, bv):
        gate, _, _ = conv_gate(gp, gp_prev, w0, w1, w2, bv)
        return gate * jax.nn.sigmoid(gate) * val.astype(F32)

    act, act_t = _tile_call("conv_swiglu", with_t(convglu_fn), (ncf, l // trc), [up, up, up] + taps + [conv_b],
                            [_t(trc, tcf), _prev_rows(trc, tcf), _t(trc, tcf, dff)] + [_v(tcf)] * 4,
                            [_sds((l, dff), BF16), _sds((dff, l), BF16)], [_t(trc, tcf), _tt(trc, tcf)])
    full['w_ffn_down'] = _exchange_wait("gather_ffn_down_wait", h_down, act)[0].reshape(row_sharded['w_ffn_down'])
    ffn = _matmul("ffn_down", act, full['w_ffn_down'], "nn", tm=512)

    def final_fn(x2v, fv, g2v, gv, tv):
        rows = x2v.shape[0]

        def loss_of(x2a, fa, g2a, ga):
            out = _rms(x2a + g2a * fa, ga)
            err = out - tv
            return 0.5 * _colsum(jnp.mean(err * err, axis=-1, keepdims=True))

        loss, vjp = jax.vjp(loss_of, x2v, fv, _bc(g2v, rows), _bc(gv, rows))
        dx3, dffn, dg2, dgf = vjp(jnp.ones((1, 1), F32))
        return jnp.broadcast_to(loss, (1, LANES)), dx3, dffn, _colsum(dg2), _colsum(dgf)

    loss_p, dx3, dffn, dg2, dg_fin = _tile_call(
        "loss_final_norm", final_fn, (1, nrh), [x2, ffn, g2, g_fin, tgt],
        [_t(trh, d), _t(trh, d), _v(d), _v(d), _t(trh, d)],
        [_sds((1, LANES)), _sds((l, d)), _sds((l, d), BF16), _sds((1, d)), _sds((1, d))],
        [_v(LANES), _t(trh, d), _t(trh, d), _v(d), _v(d)], acc=(0, 3, 4))
    loss = lax.psum(loss_p[0, 0], ("x", "y", "c"))

    dact = _matmul("d_act", dffn, full['w_ffn_down'], "nt", out_dtype=BF16, tn=1408, dep=loss.reshape(1, 1))
    gd, gd16, pending = {}, {}, []
    dw_down, dw_down16 = _matmul("dw_ffn_down", act_t, dffn, "nn", tm=512, also_bf16=True)
    gd['w_ffn_down'], gd16['w_ffn_down'] = [z.reshape((N_DEV,) + w_ffn_down.shape[1:]) for z in (dw_down, dw_down16)]
    handle, tok = _exchange_start("grad_ffn_down_start", [gd16['w_ffn_down']], False, loss.reshape(1, 1))
    pending.append((['w_ffn_down'], handle))
    conv_b_bwd = conv_b + tok[0:1, 0:1]

    def convglu_bwd_fn(gp, gp_prev, gp_next, val, val_next, da, da_next, w0, w1, w2, bv):
        rows = gp.shape[0]
        i = pl.program_id(1)
        gp, val, da = gp.astype(F32), val.astype(F32), da.astype(F32)
        prev = jnp.where(i == 0, 0.0, 1.0) * gp_prev.astype(F32)
        more = jnp.where(i == pl.num_programs(1) - 1, 0.0, 1.0)
        ext = jnp.concatenate([prev, gp, gp_next.astype(F32)], axis=0)
        cur = ext[HALO:]
        m1 = pltpu.roll(ext, 1, 0)[HALO:]
        m2 = pltpu.roll(ext, 2, 0)[HALO:]
        gate = w0 * m2 + w1 * m1 + w2 * cur + bv
        sg = jax.nn.sigmoid(gate)
        val_e = jnp.concatenate([val, val_next.astype(F32)], axis=0)
        da_e = jnp.concatenate([da, more * da_next.astype(F32)], axis=0)
        dgate = da_e * val_e * (sg * (1.0 + gate * (1.0 - sg)))
        p1 = pltpu.roll(dgate, rows + HALO - 1, 0)[:rows]
        p2 = pltpu.roll(dgate, rows + HALO - 2, 0)[:rows]
        dg = dgate[:rows]
        dgp = w2 * dg + w1 * p1 + w0 * p2
        dval = da * (gate[:rows] * sg[:rows])
        return (jnp.stack([dgp, dval], axis=0), _colsum(dg), _colsum(dg * m2[:rows]), _colsum(dg * m1[:rows]),
                _colsum(dg * gp))

    dup, dconv_b, dcw0, dcw1, dcw2 = _tile_call(
        "conv_swiglu_bwd", convglu_bwd_fn, (ncf, l // trc), [up, up, up, up, up, dact, dact] + taps + [conv_b_bwd],
        [_t(trc, tcf), _prev_rows(trc, tcf), _next_rows(trc, tcf, l), _t(trc, tcf, dff), _next_rows(trc, tcf, l, dff),
         _t(trc, tcf), _next_rows(trc, tcf, l)] + [_v(tcf)] * 4,
        [_sds((2, l, dff), BF16)] + [_sds((1, dff))] * 4, [_st(trc, tcf)] + [_v(tcf)] * 4, acc=(1, 2, 3, 4))
    dh2 = _matmul("d_h2", dup, full['w_ffn_up'], "nt", tm=512, fold=4)
    gd['w_ffn_up'], gd16['w_ffn_up'] = _matmul("dw_ffn_up", h2_t, dup, "nn", tm=512, tn=1408, out_stack=N_DEV, also_bf16=True)
    handle, tok = _exchange_start("grad_ffn_up_start", [gd16['w_ffn_up']], False, gd['w_ffn_up'])
    pending.append((['w_ffn_up'], handle))
    g_ffn_bwd = g_ffn + tok[0:1, 0:1]

    def res_norm_bwd_fn(xv, mo, g1v, gv, scv, shv, dhv, dxv):
        rows = xv.shape[0]
        _, vjp = jax.vjp(res_norm_fn, xv, mo, _bc(g1v, rows), _bc(gv, rows), _bc(scv, rows), _bc(shv, rows))
        dx, dmo, dg1v, dgv, dscv, dshv = vjp((dxv, dhv))
        return dx, dmo, _colsum(dg1v), _colsum(dgv), _colsum(dscv), _colsum(dshv)

    dx2, dmixout, dg1, dg_ffn, dsc2, dsh2 = _tile_call(
        "residual_norm_mod_ffn_bwd", res_norm_bwd_fn, (1, nrh), [xs, mixout, g1, g_ffn_bwd, sc2, sh2, dh2, dx3],
        [_t(trh, d), _t(trh, d), _v(d), _v(d), _v(d), _v(d), _t(trh, d), _t(trh, d)],
        [_sds((l, d)), _sds((l, d), BF16)] + [_sds((1, d))] * 4,
        [_t(trh, d), _t(trh, d)] + [_v(d)] * 4, acc=(2, 3, 4, 5))

    def mix_bwd_epilogue(dm, ga_, gb_, pa, ps, at):
        _, vjp = jax.vjp(_mix_fn, ga_, gb_, at, pa, ps)
        da, db, dat, dpa, dps = vjp(dm)
        return jnp.stack([da, db], axis=0), dat, dpa, dps

    dglu, dattn, dga, dgs = _matmul(
        "d_mixed_gate_mix_bwd", dmixout, full['w_out'], "nt", tn=tcd,
        epilogue=(mix_bwd_epilogue, gate_ins + [attn], [(BF16, 'pair')] + [(BF16, False)] * 3))
    dw_out, dw_out16 = _matmul("dw_out", mixed, dmixout, "tn", also_bf16=True)
    gd['w_out'], gd16['w_out'] = [z.reshape((N_DEV,) + w_out.shape[1:]) for z in (dw_out, dw_out16)]

    def gelu_bwd_epilogue(dgy, yv):
        _, vjp = jax.vjp(lambda z: jax.nn.gelu(z), yv)
        return (vjp(dgy)[0],)

    dy, = _matmul("d_gelu_y_gelu_bwd", dglu, full['w_ssm_glu'], "nt", epilogue=(gelu_bwd_epilogue, [y], [(F32, False)]))
    gd['w_ssm_glu'], gd16['w_ssm_glu'] = _matmul("dw_ssm_glu", gy, dglu, "tn", out_stack=N_DEV, also_bf16=True)
    du, dbd, dcbdt, dlam, dd_tiles = _s5_bwd(proj, off_u, dy, hst, bd, bdt, cbdt, lam, dvec, t_len)

    def gpn_of(z):
        return z.transpose(2, 0, 1, 3).reshape(pgn)

    dbb_r = gpn_of(_diag_blocks(dbd[:, :, :TILE_STATES], SSM_P))
    dbb_i = gpn_of(_diag_blocks(dbd[:, :, TILE_STATES:], SSM_P))
    dc_re = _diag_blocks(dcbdt[:, :, :TILE_STATES], SSM_P).reshape(n_groups, SSM_P, SSM_N)
    dc_im = _diag_blocks(dcbdt[:, :, TILE_STATES:], SSM_P).reshape(n_groups, SSM_P, SSM_N)
    dlam_r, dlam_i = dlam[:, 0].reshape(gn), dlam[:, 1].reshape(gn)

    def disc_bwd_fn(ar, ai, ld, br, bi, dlr, dli, dbr, dbi):
        _, vjp = jax.vjp(_s5_disc_fn, ar, ai, ld, br, bi)
        return vjp((dlr, dli, dbr, dbi))

    da_re, da_im, dlog_dt, db_re, db_im = _tile_call(
        "s5_discretise_bwd", disc_bwd_fn, (1,), disc_ins + [dlam_r, dlam_i, dbb_r, dbb_i],
        disc_specs + [_full_spec(gn), _full_spec(gn), _full_spec(pgn), _full_spec(pgn)],
        [_sds(gn), _sds(gn), _sds((n_groups, 1)), _sds(pgn), _sds(pgn)], disc_specs)

    do2 = _matmul("d_attn_heads", dattn, full['w_attn_proj'], "nt")
    gd['w_attn_proj'], gd16['w_attn_proj'] = _matmul("dw_attn_proj", o2, dattn, "tn", out_stack=N_DEV, also_bf16=True)
    handle, tok = _exchange_start("grad_mixer_start", [gd16[k] for k in mixer_w], False, gd['w_attn_proj'])
    pending.append((mixer_w, handle))
    do_h = heads(do2.astype(BF16), hq)
    dq_h, dk_h, dv_h, dsink = _attn_bwd(qh, kh, vh, sinks3 + tok[0:1, 0:1], do_h)

    def unheads(z):
        return z.transpose(1, 0, 2).reshape(l, z.shape[0] * HEAD_DIM)

    early = ['attn_sinks', 'ssm_a_re', 'ssm_a_im', 'ssm_log_dt', 'ssm_b_re', 'ssm_b_im', 'ssm_c_re', 'ssm_c_im',
             'ssm_d', 'norm_ffn_g', 'ffn_conv_b', 'final_g']
    early_grads = {
        'attn_sinks': dsink[:, 0, 0], 'ssm_a_re': da_re, 'ssm_a_im': da_im, 'ssm_log_dt': dlog_dt,
        'ssm_b_re': db_re.transpose(1, 2, 0), 'ssm_b_im': db_im.transpose(1, 2, 0), 'ssm_c_re': dc_re,
        'ssm_c_im': dc_im, 'ssm_d': dd_tiles, 'norm_ffn_g': dg_ffn, 'ffn_conv_b': dconv_b, 'final_g': dg_fin}
    ge_pack, e_offs = _pack([jnp.concatenate([dg1, dsh2, dsc2, dg2], axis=1)] + [early_grads[k] for k in early],
                            LANES, 8)
    h_early, tok = _exchange_start("gather_small_early_start", [ge_pack], True, dsink)

    dproj = jnp.concatenate([unheads(dq_h), unheads(dk_h), unheads(dv_h), du, dga, dgs], axis=1)
    dw_in, dw_in16 = _matmul("dw_in", h1_t, dproj, "nn", tm=512, tn=1280, also_bf16=True, dep=tok)
    dcw = jnp.concatenate([dcw0, dcw1, dcw2], axis=0)
    shard_in, shard_cw = w_in.shape[1:], ffn_conv_w.shape[1:]
    gd16['w_in'] = dw_in16.reshape(shard_in[0], N_DEV, shard_in[1]).transpose(1, 0, 2)
    own_in = lax.dynamic_slice_in_dim(dw_in, idx * shard_in[1], shard_in[1], axis=1)[None]
    gd['ffn_conv_w'] = dcw.reshape(shard_cw[0], N_DEV, shard_cw[1]).transpose(1, 0, 2)
    gd16['ffn_conv_w'] = gd['ffn_conv_w'].astype(BF16)
    handle, tok = _exchange_start("grad_in_start", [gd16['w_in'], gd16['ffn_conv_w']], False, dw_in)
    pending.append((['w_in', 'ffn_conv_w'], handle))
    dh1 = _matmul("d_h1", dproj, full['w_in'], "nt", tm=512, dep=tok)

    def norm_bwd_fn(xv, gv, scv, shv, dhv, dxv):
        rows = xv.shape[0]
        _, vjp = jax.vjp(_norm_mod, xv, _bc(gv, rows), _bc(scv, rows), _bc(shv, rows))
        dx, dgv, dscv, dshv = vjp(dhv)
        return dx + dxv, _colsum(dgv), _colsum(dscv), _colsum(dshv)

    grad_x, dg_mix, dsc1, dsh1 = _tile_call(
        "norm_mod_mix_bwd", norm_bwd_fn, (1, nrh), [xs, g_mix, sc1, sh1, dh1, dx2],
        [_t(trh, d), _v(d), _v(d), _v(d), _t(trh, d), _t(trh, d)],
        [_sds((l, d))] + [_sds((1, d))] * 3, [_t(trh, d)] + [_v(d)] * 3, acc=(1, 2, 3))

    gl_pack, l_offs = _pack([jnp.concatenate([dsh1, dsc1], axis=1), dg_mix], LANES, 8)
    h_late, tok = _exchange_start("gather_small_late_start", [gl_pack], True, grad_x)

    sharded = big + ['ffn_conv_w']
    sharded_out = {}

    def finish(group, handle, after):
        for k, parts in zip(group, _exchange_wait("grad_" + group[0] + "_wait", handle, after)):
            own_src, own_at = (own_in, 0 * idx) if k == 'w_in' else (gd[k], idx)
            sharded_out[k] = _adamw_sharded("adamw_" + k, parts, own_src, given[k][0], given['m_' + k][0],
                                            given['v_' + k][0], jnp.stack([idx, own_at]).astype(jnp.int32))

    for group, handle in pending[:-1]:
        finish(group, handle, tok)
    done = functools.reduce(lambda p, q: p + q, [sharded_out[k][1][0:1, 0:1] for g_, _ in pending[:-1] for k in g_])
    finish(*pending[-1], done)

    ge_all, = _exchange_wait("gather_small_early_wait", h_early, done)
    gl_all, = _exchange_wait("gather_small_late_wait", h_late, sharded_out['w_in'][1])
    gs_all = jnp.concatenate([ge_all, gl_all], axis=1)
    rows_e = ge_pack.shape[0]

    def small_pack(prefix):
        ab = given[prefix + 'ada_b']
        p_early, _ = _pack([ab[:, 2 * d:]] + [given[prefix + k] for k in early], LANES, 8)
        p_late, _ = _pack([ab[:, :2 * d], given[prefix + 'norm_mix_g']], LANES, 8)
        return jnp.concatenate([p_early, p_late], axis=0)

    small_out = _adamw("adamw_replicated", gs_all, small_pack(''), small_pack('m_'), small_pack('v_'))

    dmod_all = jnp.concatenate([_unpack(gl_all, l_offs[0], (2 * d,), lead=(N_DEV,)),
                                _unpack(ge_all, e_offs[0], (4 * d,), lead=(N_DEV,))], axis=1)
    dmod_mine = lax.dynamic_slice_in_dim(dmod_all, idx * mod_n, mod_n, axis=1)
    kpad = LANES - N_DEV
    cond_t = jnp.pad(cond_all.T, ((0, 0), (0, kpad)))
    dmod_pad = jnp.pad(dmod_mine, ((0, kpad), (0, 0)))
    g_ada_w = _matmul("dw_ada", cond_t, dmod_pad, "nn")
    ada_out = _adamw("adamw_ada_w", g_ada_w[None], ada_w[0], m_ada_w[0], v_ada_w[0])

    results = [{}, {}, {}, {}]
    for which in range(4):
        out = small_out[which]
        results[which]['ada_b'] = jnp.concatenate([_unpack(out, rows_e + l_offs[0], (1, 2 * d)),
                                                   _unpack(out, e_offs[0], (1, 4 * d))], axis=1)
        results[which]['norm_mix_g'] = _unpack(out, rows_e + l_offs[1], norm_mix_g.shape)
        for k, off in zip(early, e_offs[1:]):
            results[which][k] = _unpack(out, off, given[k].shape)
        for k in sharded:
            results[which][k] = sharded_out[k][which][None]
        results[which]['ada_w'] = ada_out[which][None]
    outs = [loss, grad_x[None]]
    for which in range(4):
        outs += [results[which][k] for k in names]
    return tuple(outs)
```

```python
import functools
import math

import jax
import jax.numpy as jnp
from jax import lax
from jax.experimental import pallas as pl
from jax.experimental.pallas import tpu as pltpu

F32, BF16 = jnp.float32, jnp.bfloat16
MESH = pl.DeviceIdType.MESH
N_DEV = 8

HEAD_DIM = 64
N_KV_HEADS = 2
ATT_BLOCK = 128
NEG_INF = -1e30
SSM_P = 16
SSM_N = 64
LANES = 128
TILE_GROUPS = LANES // SSM_P
TILE_STATES = TILE_GROUPS * SSM_N
RMS_EPS = 1e-6
ADAM_LR, ADAM_B1, ADAM_B2, ADAM_EPS, ADAM_WD, ADAM_STEP = 0.001, 0.9, 0.999, 1e-08, 0.01, 10
VMEM_LIMIT = 56 * 1024 * 1024
MATMUL_VMEM_BUDGET = 44 * 1024 * 1024


def _params(n_axes):
    return pltpu.CompilerParams(dimension_semantics=("arbitrary",) * n_axes, vmem_limit_bytes=VMEM_LIMIT)


def _pick(dim, pref, align=128):
    if dim <= align:
        return dim
    t = (min(pref, dim) // align) * align
    while t > align and dim % t:
        t -= align
    assert dim % t == 0, (dim, pref, align)
    return t


def _dev():
    return lax.axis_index("x"), lax.axis_index("y"), lax.axis_index("c")


def _tile_call(name, fn, grid, ins, in_specs, out_shapes, out_specs, acc=()):
    n_in, n_out = len(ins), len(out_shapes)
    acc_axis = len(grid) - 1

    def body(*refs):
        vals = fn(*[r[...] for r in refs[:n_in]])
        if not isinstance(vals, (tuple, list)):
            vals = (vals,)
        assert len(vals) == n_out
        for i, (r, v) in enumerate(zip(refs[n_in:], vals)):
            v = v.astype(r.dtype)
            if i in acc:
                first = pl.program_id(acc_axis) == 0

                @pl.when(first)
                def _():
                    r[...] = v

                @pl.when(jnp.logical_not(first))
                def _():
                    r[...] += v
            else:
                r[...] = v

    return pl.pallas_call(
        body, grid=grid, in_specs=in_specs, out_specs=out_specs, out_shape=out_shapes, name=name,
        compiler_params=_params(len(grid)),
    )(*ins)


def _t(tr, tc, off=0):
    return pl.BlockSpec((tr, tc), lambda j, i: (i, j + off // tc))


def _tt(tr, tc):
    return pl.BlockSpec((tc, tr), lambda j, i: (j, i))


def _v(tc, off=0, rows=1):
    return pl.BlockSpec((rows, tc), lambda j, i: (0, j + off // tc))


HALO = 16


def _prev_rows(tr, tc, off=0):
    return pl.BlockSpec((HALO, tc), lambda j, i: (jnp.maximum(i * (tr // HALO) - 1, 0), j + off // tc))


def _next_rows(tr, tc, nrows, off=0):
    return pl.BlockSpec((HALO, tc),
                        lambda j, i: (jnp.minimum((i + 1) * (tr // HALO), nrows // HALO - 1), j + off // tc))


def _st(tr, tc):
    return pl.BlockSpec((2, tr, tc), lambda j, i: (0, i, j))


def _bc(v, rows):
    return jnp.broadcast_to(v, (rows, v.shape[-1]))


def _colsum(v):
    return jnp.sum(v, axis=0, keepdims=True)


def _matmul(name, a, b, mode, out_dtype=F32, tm=1024, tn=1024, tk=None, out_stack=None, also_bf16=False, dep=None,
            fold=1, epilogue=None):
    def dims(z):
        return (z.shape[-2], z.shape[-1] * (z.shape[0] if z.ndim == 3 else 1))

    ar, ac = dims(a)
    br, bc = dims(b)
    if mode == "nn":
        m, k, n = ar, ac, bc
        assert br == k
    elif mode == "nt":
        m, k, n = ar, ac, br
        assert bc == k
    else:
        m, k, n = ac, ar, bc
        assert br == k
    m_lim, k_lim, n_lim = [m], [k], [n]
    if a.ndim == 3:
        (m_lim if mode == "tn" else k_lim).append(a.shape[-1])
    if b.ndim == 3:
        (k_lim if mode == "nt" else n_lim).append(b.shape[-1])
    if out_stack:
        n_lim.append(n // out_stack)
    tm = _pick(functools.reduce(math.gcd, m_lim), tm)
    tn = _pick(functools.reduce(math.gcd, n_lim), tn)
    k_unit = functools.reduce(math.gcd, k_lim)
    if tk is None:
        sa, sb, so = a.dtype.itemsize, b.dtype.itemsize, jnp.dtype(out_dtype).itemsize + (2 if also_bf16 else 0)
        fits = [t for t in range(LANES, k_unit + 1, LANES) if k_unit % t == 0 and
                2 * t * (tm * sa + tn * sb) + tm * tn * (2 * so + (4 if t < k else 0)) <= MATMUL_VMEM_BUDGET]
        tk = max(fits) if fits else _pick(k_unit, 512)
    else:
        tk = _pick(k_unit, tk)
    assert (k // tk) % fold == 0
    nk = k // (tk * fold)

    def spec(z, brows, bcols, ridx, cidx):
        if z.ndim == 3:
            per = z.shape[-1] // bcols
            return pl.BlockSpec((None, brows, bcols),
                                lambda i, j, kk: (cidx(i, j, kk) // per, ridx(i, j, kk), cidx(i, j, kk) % per))
        return pl.BlockSpec((brows, bcols), lambda i, j, kk: (ridx(i, j, kk), cidx(i, j, kk)))

    gi = lambda i, j, kk: i
    gj = lambda i, j, kk: j
    a_specs, b_specs = [], []
    for f in range(fold):
        gk = lambda i, j, kk, f=f: fold * kk + f
        if mode == "nn":
            a_specs.append(spec(a, tm, tk, gi, gk))
            b_specs.append(spec(b, tk, tn, gk, gj))
            dn = (((1,), (0,)), ((), ()))
        elif mode == "nt":
            a_specs.append(spec(a, tm, tk, gi, gk))
            b_specs.append(spec(b, tn, tk, gj, gk))
            dn = (((1,), (1,)), ((), ()))
        else:
            a_specs.append(spec(a, tk, tm, gk, gi))
            b_specs.append(spec(b, tk, tn, gk, gj))
            dn = (((0,), (0,)), ((), ()))

    epi_fn, epi_ins, epi_outs = epilogue if epilogue else (None, [], [])
    n_out = len(epi_outs) if epilogue else (2 if also_bf16 else 1)

    deps = [] if dep is None else [dep]

    def body(*refs):
        a_refs, b_refs = refs[:fold], refs[fold:2 * fold]
        e_refs = refs[2 * fold:2 * fold + len(epi_ins)]
        rest = refs[2 * fold + len(epi_ins) + len(deps):]
        o_refs, acc = rest[:n_out], rest[n_out:]
        part = None
        for a_ref, b_ref in zip(a_refs, b_refs):
            one = lax.dot_general(a_ref[...].astype(BF16), b_ref[...].astype(BF16), dn, preferred_element_type=F32)
            part = one if part is None else part + one

        def emit(val):
            vals = epi_fn(val, *[r[...] for r in e_refs]) if epilogue else [val] * n_out
            for o_ref, v in zip(o_refs, vals):
                o_ref[...] = v.astype(o_ref.dtype)

        if nk == 1:
            emit(part)
            return
        acc_ref, = acc
        kk = pl.program_id(2)

        @pl.when(kk == 0)
        def _():
            acc_ref[...] = part

        @pl.when(kk > 0)
        def _():
            acc_ref[...] += part

        @pl.when(kk == nk - 1)
        def _():
            emit(acc_ref[...])

    if out_stack:
        per = (n // out_stack) // tn
        out_spec = pl.BlockSpec((None, tm, tn), lambda i, j, kk: (j // per, i, j % per))
        shape = (out_stack, m, n // out_stack)
    else:
        out_spec = pl.BlockSpec((tm, tn), lambda i, j, kk: (i, j))
        shape = (m, n)
    if epilogue:
        assert not out_stack and not also_bf16
        kinds = {False: (pl.BlockSpec((tm, tn), lambda i, j, kk: (i, j)), (m, n)),
                 True: (pl.BlockSpec((tn, tm), lambda i, j, kk: (j, i)), (n, m)),
                 'pair': (pl.BlockSpec((2, tm, tn), lambda i, j, kk: (0, i, j)), (2, m, n))}
        out_specs = [kinds[t][0] for _, t in epi_outs]
        out_shapes = [jax.ShapeDtypeStruct(kinds[t][1], dt) for dt, t in epi_outs]
    else:
        out_specs = [out_spec] * n_out
        out_shapes = [jax.ShapeDtypeStruct(shape, dt) for dt in [out_dtype, BF16][:n_out]]
    e_pairs = [z if isinstance(z, tuple) else (z, 0) for z in epi_ins]
    assert all(off % tn == 0 for _, off in e_pairs)
    e_specs = [pl.BlockSpec((1, tn) if z.shape[0] == 1 else (tm, tn),
                            lambda i, j, kk, ob=off // tn, row=z.shape[0] == 1: (0 if row else i, j + ob))
               for z, off in e_pairs]
    res = pl.pallas_call(
        body, grid=(m // tm, n // tn, nk),
        in_specs=a_specs + b_specs + e_specs + [pl.BlockSpec(memory_space=pl.ANY)] * len(deps),
        out_specs=out_specs, out_shape=out_shapes,
        scratch_shapes=[pltpu.VMEM((tm, tn), F32)] if nk > 1 else [], name=name, compiler_params=_params(3),
    )(*[a] * fold, *[b] * fold, *[z for z, _ in e_pairs], *deps)
    return res if (also_bf16 or epilogue) else res[0]


def _all_gather(name, arrs, dep=None):
    n = len(arrs)
    deps = [] if dep is None else [dep]

    def body(*refs):
        ins, outs = refs[:n], refs[n + len(deps):2 * n + len(deps)]
        send_sems, recv_sems, local_sems = refs[2 * n + len(deps):]
        x, y, c = _dev()
        me, sib = (x, y, c), (x, y, 1 - c)
        x_nbr, y_nbr, diag = (1 - x, y), (x, 1 - y), (1 - x, 1 - y)
        north = c == 1
        relay_from = (jnp.where(north, 1 - x, x), jnp.where(north, y, 1 - y))
        relay_to = (jnp.where(north, x, 1 - x), jnp.where(north, 1 - y, y))

        def slot(p):
            return 4 * p[0] + 2 * p[1] + p[2]

        def copy(a, k, block, to, src=None):
            dst = outs[a].at[slot(block)]
            return pltpu.make_async_remote_copy(
                src_ref=dst if src is None else src, dst_ref=dst,
                send_sem=send_sems.at[7 * a + k], recv_sem=recv_sems.at[7 * a + k],
                device_id=to, device_id_type=MESH)

        mine = [pltpu.make_async_copy(ins[a], outs[a].at[slot(me)], local_sems.at[a]) for a in range(n)]
        for cp in mine:
            cp.start()
        sent = []
        for a in range(n):
            sent += [copy(a, 0, me, sib, src=ins[a]), copy(a, 1, me, (*x_nbr, c), src=ins[a]),
                     copy(a, 2, me, (*y_nbr, c), src=ins[a])]
        for cp in sent:
            cp.start()
        relays = [copy(a, 3, (*relay_from, c), (*relay_to, c)) for a in range(n)]
        for k, chip, relay_here in ((1, x_nbr, north), (2, y_nbr, jnp.logical_not(north)), (3, diag, None)):
            for a in range(n):
                copy(a, k, (*chip, c), me).wait_recv()
                cp = copy(a, 3 + k, (*chip, c), sib)
                cp.start()
                sent.append(cp)
                if relay_here is not None:
                    pl.when(relay_here)(relays[a].start)
        for a in range(n):
            copy(a, 0, sib, me).wait_recv()
            for k, chip in ((4, x_nbr), (5, y_nbr), (6, diag)):
                copy(a, k, (*chip, 1 - c), me).wait_recv()
        for cp in sent + relays:
            cp.wait_send()
        for cp in mine:
            cp.wait()

    any_spec = pl.BlockSpec(memory_space=pl.ANY)
    return pl.pallas_call(
        body, in_specs=[any_spec] * (n + len(deps)), out_specs=[any_spec] * n,
        out_shape=[jax.ShapeDtypeStruct((N_DEV,) + a.shape, a.dtype) for a in arrs],
        scratch_shapes=[pltpu.SemaphoreType.DMA((7 * n,)), pltpu.SemaphoreType.DMA((7 * n,)),
                        pltpu.SemaphoreType.DMA((n,))],
        name=name,
    )(*arrs, *deps)


FLIPS = [(0, 0, 1), (0, 1, 0), (1, 0, 0), (0, 1, 1), (1, 0, 1), (1, 1, 0), (1, 1, 1)]
N_PEERS = len(FLIPS)
_HBM = pl.BlockSpec(memory_space=pltpu.HBM)
_SEM = pl.BlockSpec(memory_space=pltpu.SEMAPHORE)
_EFFECT = pltpu.SideEffectType.DATAFLOW_SIDE_EFFECTING


def _flip(x, y, c, f):
    return (1 - x if f[0] else x, 1 - y if f[1] else y, 1 - c if f[2] else c)


def _slot(p):
    return 4 * p[0] + 2 * p[1] + p[2]


def _exchange_copies(src_refs, land_refs, send_sems, recv_sems, gather):
    x, y, c = _dev()
    mine = _slot((x, y, c))
    cps = []
    for a, (src, land) in enumerate(zip(src_refs, land_refs)):
        for k, f in enumerate(FLIPS):
            peer = _flip(x, y, c, f)
            cps.append(pltpu.make_async_remote_copy(
                src_ref=src if gather else src.at[_slot(peer)], dst_ref=land.at[mine],
                send_sem=send_sems.at[N_PEERS * a + k], recv_sem=recv_sems.at[N_PEERS * a + k],
                device_id=peer, device_id_type=MESH))
    return cps


def _exchange_start(name, srcs, gather, after):
    n = len(srcs)
    lands = [lax.empty(((N_DEV,) + s.shape) if gather else s.shape, s.dtype) for s in srcs]

    def body(*refs):
        src_refs, land_refs = refs[:n], refs[n:2 * n]
        send_sems, recv_sems, local_sems = refs[2 * n + 1:2 * n + 4]
        token = refs[-1]
        if gather:
            x, y, c = _dev()
            for a in range(n):
                pltpu.make_async_copy(src_refs[a], land_refs[a].at[_slot((x, y, c))], local_sems.at[a]).start()
        for cp in _exchange_copies(src_refs, land_refs, send_sems, recv_sems, gather):
            cp.start()
        token[...] = jnp.zeros_like(token)

    hbm = lambda z: pltpu.HBM(z.shape, z.dtype)
    outs = pl.pallas_call(
        body, name=name,
        out_shape=(pltpu.SemaphoreType.DMA((N_PEERS * n,)), pltpu.SemaphoreType.DMA((N_PEERS * n,)),
                   pltpu.SemaphoreType.DMA((n,)), *[hbm(s) for s in srcs], *[hbm(z) for z in lands],
                   jax.ShapeDtypeStruct((8, LANES), F32)),
        in_specs=[_HBM] * (2 * n) + [pl.BlockSpec(memory_space=pl.ANY)],
        out_specs=(_SEM, _SEM, _SEM, *[_HBM] * (2 * n), pl.BlockSpec(memory_space=pltpu.VMEM)),
        input_output_aliases={i: 3 + i for i in range(2 * n)},
        compiler_params=pltpu.CompilerParams(has_side_effects=_EFFECT),
    )(*[pltpu.with_memory_space_constraint(z, pltpu.HBM) for z in list(srcs) + lands], after)
    return (outs[:3], outs[3:3 + n], outs[3 + n:3 + 2 * n], gather), outs[-1]


def _exchange_wait(name, handles, after):
    sems, srcs, lands, gather = handles
    n = len(srcs)

    def body(*refs):
        src_refs, land_refs = refs[:n], refs[n:2 * n]
        send_sems, recv_sems, local_sems = refs[2 * n:2 * n + 3]
        if gather:
            for a in range(n):
                pltpu.make_async_copy(src_refs[a], land_refs[a].at[0], local_sems.at[a]).wait()
        for cp in _exchange_copies(src_refs, land_refs, send_sems, recv_sems, gather):
            cp.wait_send()
            cp.wait_recv()

    hbm = lambda z: pltpu.HBM(z.shape, z.dtype)
    outs = pl.pallas_call(
        body, name=name, out_shape=tuple(hbm(z) for z in list(srcs) + list(lands)),
        in_specs=[_HBM] * (2 * n) + [_SEM] * 3 + [pl.BlockSpec(memory_space=pl.ANY)],
        out_specs=tuple([_HBM] * (2 * n)), input_output_aliases={i: i for i in range(2 * n)},
        compiler_params=pltpu.CompilerParams(has_side_effects=_EFFECT),
    )(*srcs, *lands, *sems, after)
    return list(outs[n:])


def _pack_rows(sizes, width, row_align):
    offs, r = [], 0
    for s in sizes:
        offs.append(r)
        r += -(-s // width)
    total = -(-r // row_align) * row_align
    return offs, total


def _pack(items, width, row_align, lead=()):
    nl = len(lead)
    sizes = [int(jnp.size(a)) // max(1, functools.reduce(lambda p, q: p * q, lead, 1)) for a in items]
    offs, total = _pack_rows(sizes, width, row_align)
    flat = []
    used = 0
    for a, s in zip(items, sizes):
        f = a.reshape(lead + (s,))
        pad = -(-s // width) * width - s
        if pad:
            f = jnp.pad(f, [(0, 0)] * nl + [(0, pad)])
        flat.append(f)
        used += s + pad
    tail = total * width - used
    if tail:
        flat.append(jnp.zeros(lead + (tail,), items[0].dtype))
    return jnp.concatenate(flat, axis=-1).reshape(lead + (total, width)), offs


def _unpack(packed, off, shape, lead=()):
    nl = len(lead)
    size = functools.reduce(lambda p, q: p * q, shape, 1)
    width = packed.shape[-1]
    rows = -(-size // width)
    blk = lax.slice_in_dim(packed, off, off + rows, axis=nl).reshape(lead + (rows * width,))
    return lax.slice_in_dim(blk, 0, size, axis=nl).reshape(lead + tuple(shape))


def _rms(x, g):
    return (x * lax.rsqrt(jnp.mean(x * x, axis=-1, keepdims=True) + RMS_EPS)) * g


def _norm_mod(x, g, sc, sh):
    return _rms(x, g) * (1.0 + sc) + sh


def _mix_fn(glu_a, glu_b, attn, ga, gs):
    return jax.nn.sigmoid(ga) * attn + jax.nn.sigmoid(gs) * (glu_a * jax.nn.sigmoid(glu_b))


def _s5_disc_fn(a_re, a_im, log_dt, b_re, b_im):
    dt = jnp.exp(log_dt)
    mag = jnp.exp(a_re * dt)
    lr, li = mag * jnp.cos(a_im * dt), mag * jnp.sin(a_im * dt)
    den = a_re * a_re + a_im * a_im
    zr = ((lr - 1.0) * a_re + li * a_im) / den
    zi = (li * a_re - (lr - 1.0) * a_im) / den
    return lr, li, zr[None] * b_re - zi[None] * b_im, zr[None] * b_im + zi[None] * b_re


def _adamw_fn(w, g, m, v):
    m = ADAM_B1 * m + (1.0 - ADAM_B1) * g
    v = ADAM_B2 * v + (1.0 - ADAM_B2) * jnp.square(g)
    m_hat = m / (1.0 - ADAM_B1 ** ADAM_STEP)
    v_hat = v / (1.0 - ADAM_B2 ** ADAM_STEP)
    delta = -ADAM_LR * (m_hat / (jnp.sqrt(v_hat) + ADAM_EPS) + ADAM_WD * w)
    return delta, m, v


def _adamw(name, parts, w, m, v):
    p, r, c = parts.shape
    tr = _pick(r, max(8, (1 << 21) // (4 * c * max(p, 2))), 8)

    def fn(pv, wv, mv, vv):
        g = pv[0]
        for i in range(1, p):
            g = g + pv[i]
        d, m2, v2 = _adamw_fn(wv, g, mv, vv)
        return g, d, m2, v2

    spec = pl.BlockSpec((tr, c), lambda i: (i, 0))
    return _tile_call(
        name, fn, (r // tr,), [parts, w, m, v],
        [pl.BlockSpec((p, tr, c), lambda i: (0, i, 0)), spec, spec, spec],
        [jax.ShapeDtypeStruct((r, c), F32)] * 4, [spec] * 4)


def _adamw_sharded(name, parts, own_src, w, m, v, place):
    _, k, n = parts.shape
    tr = _pick(k, max(16, (1 << 19) // (4 * n)), 16)

    def body(pl_ref, p_ref, a_ref, w_ref, m_ref, v_ref, g_ref, d_ref, m2_ref, v2_ref):
        own = a_ref[0]
        g = None
        for q in range(N_DEV):
            term = jnp.where(pl_ref[0] == q, own, p_ref[q].astype(F32))
            g = term if g is None else g + term
        d, m2, v2 = _adamw_fn(w_ref[...], g, m_ref[...], v_ref[...])
        g_ref[...] = g
        d_ref[...] = d
        m2_ref[...] = m2
        v2_ref[...] = v2

    spec = pl.BlockSpec((tr, n), lambda i, pr: (i, 0))
    return pl.pallas_call(
        body,
        grid_spec=pltpu.PrefetchScalarGridSpec(
            num_scalar_prefetch=1, grid=(k // tr,),
            in_specs=[pl.BlockSpec((N_DEV, tr, n), lambda i, pr: (0, i, 0)),
                      pl.BlockSpec((1, tr, n), lambda i, pr: (pr[1], i, 0)),
                      spec, spec, spec],
            out_specs=[spec] * 4),
        out_shape=[jax.ShapeDtypeStruct((k, n), F32)] * 4, name=name, compiler_params=_params(1),
    )(place, parts, own_src, w, m, v)


def _attn_mask(n, rows):
    qi = lax.broadcasted_iota(jnp.int32, (rows, 2 * ATT_BLOCK), 0) & (ATT_BLOCK - 1)
    kj = lax.broadcasted_iota(jnp.int32, (rows, 2 * ATT_BLOCK), 1)
    rel = qi + ATT_BLOCK - kj
    return (rel >= 0) & (rel < ATT_BLOCK) & ((kj >= ATT_BLOCK) | (n > 0))


def _attn_probs(q, k, sink, mask):
    s = lax.dot_general(q, k, (((1,), (1,)), ((), ())), preferred_element_type=F32) * (HEAD_DIM ** -0.5)
    s = jnp.where(mask, s, NEG_INF)
    m = jnp.maximum(jnp.max(s, axis=-1, keepdims=True), sink)
    p = jnp.exp(s - m)
    e_sink = jnp.exp(sink - m)
    inv = 1.0 / (jnp.sum(p, axis=-1, keepdims=True) + e_sink)
    return p * inv, e_sink * inv


def _attn_specs(qpk):
    blk = ATT_BLOCK
    q_spec = pl.BlockSpec((qpk, blk, HEAD_DIM), lambda h, n: (h, n, 0))
    cur = pl.BlockSpec((1, blk, HEAD_DIM), lambda h, n: (h, n, 0))
    prev = pl.BlockSpec((1, blk, HEAD_DIM), lambda h, n: (h, jnp.maximum(n - 1, 0), 0))
    sink_spec = pl.BlockSpec((1, qpk * blk, 1), lambda h, n: (h, 0, 0))
    return q_spec, cur, prev, sink_spec


def _attn_fwd(q, k, v, sinks):
    hq, l, _ = q.shape
    qpk = hq // N_KV_HEADS
    nb = l // ATT_BLOCK
    rows = qpk * ATT_BLOCK
    q_spec, cur, prev, sink_spec = _attn_specs(qpk)

    def body(q_ref, kp_ref, kc_ref, vp_ref, vc_ref, sink_ref, o_ref):
        mask = _attn_mask(pl.program_id(1), rows)
        kk = jnp.concatenate([kp_ref[0], kc_ref[0]], axis=0).astype(BF16)
        vv = jnp.concatenate([vp_ref[0], vc_ref[0]], axis=0).astype(BF16)
        p, _ = _attn_probs(q_ref[...].reshape(rows, HEAD_DIM).astype(BF16), kk, sink_ref[0], mask)
        o = jnp.dot(p.astype(BF16), vv, preferred_element_type=F32)
        o_ref[...] = o.reshape(qpk, ATT_BLOCK, HEAD_DIM).astype(o_ref.dtype)

    return pl.pallas_call(
        body, grid=(N_KV_HEADS, nb), in_specs=[q_spec, prev, cur, prev, cur, sink_spec],
        out_specs=q_spec, out_shape=jax.ShapeDtypeStruct((hq, l, HEAD_DIM), BF16),
        name="attn_fwd", compiler_params=_params(2),
    )(q, k, k, v, v, sinks)


def _attn_bwd(q, k, v, sinks, do):
    hq, l, _ = q.shape
    qpk = hq // N_KV_HEADS
    nb = l // ATT_BLOCK
    blk = ATT_BLOCK
    rows = qpk * blk
    q_spec, cur, prev, sink_spec = _attn_specs(qpk)
    part_spec = pl.BlockSpec((1, 1, 2 * blk, HEAD_DIM), lambda h, n: (h, n, 0, 0))
    dsink_spec = pl.BlockSpec((qpk, 1, LANES), lambda h, n: (h, 0, 0))
    tn = (((0,), (0,)), ((), ()))

    def body(q_ref, do_ref, kp_ref, kc_ref, vp_ref, vc_ref, sink_ref, dq_ref, dkp_ref, dvp_ref, dsink_ref):
        n = pl.program_id(1)
        mask = _attn_mask(n, rows)
        kk = jnp.concatenate([kp_ref[0], kc_ref[0]], axis=0).astype(BF16)
        vv = jnp.concatenate([vp_ref[0], vc_ref[0]], axis=0).astype(BF16)
        qb = q_ref[...].reshape(rows, HEAD_DIM).astype(BF16)
        do32 = do_ref[...].astype(F32).reshape(rows, HEAD_DIM)
        dob = do32.astype(BF16)
        p, p_sink = _attn_probs(qb, kk, sink_ref[0], mask)
        pb = p.astype(BF16)
        o = jnp.dot(pb, vv, preferred_element_type=F32)
        delta = jnp.sum(do32 * o, axis=-1, keepdims=True)
        dp = lax.dot_general(dob, vv, (((1,), (1,)), ((), ())), preferred_element_type=F32)
        ds = (p * (dp - delta) * (HEAD_DIM ** -0.5)).astype(BF16)
        dq = jnp.dot(ds, kk, preferred_element_type=F32)
        dq_ref[...] = dq.reshape(qpk, blk, HEAD_DIM).astype(dq_ref.dtype)
        dkp_ref[0, 0] = lax.dot_general(ds, qb, tn, preferred_element_type=F32)
        dvp_ref[0, 0] = lax.dot_general(pb, dob, tn, preferred_element_type=F32)
        dsr = p_sink * delta
        for g in range(qpk):
            dsg = jnp.broadcast_to(-_colsum(dsr[g * blk:(g + 1) * blk]), (1, LANES))

            @pl.when(n == 0)
            def _():
                dsink_ref[g] = dsg

            @pl.when(n > 0)
            def _():
                dsink_ref[g] += dsg


    part_shape = jax.ShapeDtypeStruct((N_KV_HEADS, nb, 2 * blk, HEAD_DIM), F32)
    dq, dkp, dvp, dsink = pl.pallas_call(
        body, grid=(N_KV_HEADS, nb), in_specs=[q_spec, q_spec, prev, cur, prev, cur, sink_spec],
        out_specs=[q_spec, part_spec, part_spec, dsink_spec],
        out_shape=[jax.ShapeDtypeStruct((hq, l, HEAD_DIM), BF16), part_shape, part_shape,
                   jax.ShapeDtypeStruct((hq, 1, LANES), F32)],
        name="attn_bwd", compiler_params=_params(2),
    )(q, do, k, k, v, v, sinks)

    def combine(a_cur, a_nxt, b_cur, b_nxt):
        last = pl.program_id(1) == nb - 1
        keep = jnp.where(last, 0.0, 1.0)
        return (a_cur[0, 0, blk:] + keep * a_nxt[0, 0, :blk])[None], (b_cur[0, 0, blk:] + keep * b_nxt[0, 0, :blk])[None]

    nxt_spec = pl.BlockSpec((1, 1, 2 * blk, HEAD_DIM), lambda h, n: (h, jnp.minimum(n + 1, nb - 1), 0, 0))
    kv_shape = jax.ShapeDtypeStruct((N_KV_HEADS, l, HEAD_DIM), BF16)
    dk, dv = _tile_call("attn_dkv", combine, (N_KV_HEADS, nb), [dkp, dkp, dvp, dvp],
                        [part_spec, nxt_spec, part_spec, nxt_spec], [kv_shape, kv_shape], [cur, cur])
    return dq, dk, dv, dsink


def _block_diag(m):
    j, gl, a, b = m.shape
    eye = jnp.eye(gl, dtype=m.dtype)
    return (m[:, :, :, None, :] * eye[None, :, None, :, None]).reshape(j, gl * a, gl * b)


def _diag_blocks(z, a):
    j = z.shape[0]
    gl = z.shape[1] // a
    b = z.shape[2] // gl
    d = jnp.diagonal(z.reshape(j, gl, a, gl, b), axis1=1, axis2=3)
    return d.transpose(0, 3, 1, 2)


def _s5_permute(src_ref, dst_ref, t_len):
    seg = t_len // 8
    for k in range(seg):
        dst_ref[8 * k:8 * k + 8, :] = src_ref[pl.ds(k, 8, stride=seg), :]


def _s5_unpermute(perm_ref, t_len, emit):
    per_seg = t_len // 64
    for m in range(t_len // 8):
        emit(8 * m, perm_ref[pl.ds(64 * (m % per_seg) + m // per_seg, 8, stride=8), :])


def _s5_powers(p_ref, lr, li, seg):
    hs = TILE_STATES

    def step(k, carry):
        pr, pi = carry
        p_ref[pl.ds(k, 1), 0:hs] = pr
        p_ref[pl.ds(k, 1), hs:2 * hs] = pi
        return lr * pr - li * pi, lr * pi + li * pr

    lax.fori_loop(0, seg, step, (lr, li))


def _s5_local_scan(x_ref, base, lr, li, seg, reverse):
    hs = TILE_STATES
    lr8, li8 = jnp.broadcast_to(lr, (8, hs)), jnp.broadcast_to(li, (8, hs))
    if reverse:
        li8 = -li8

    def step(i, carry):
        hr, hi = carry
        k = seg - 1 - i if reverse else i
        rows = pl.ds(pl.multiple_of(base + 8 * k, 8), 8)
        nr = lr8 * hr - li8 * hi + x_ref[rows, 0:hs]
        ni = lr8 * hi + li8 * hr + x_ref[rows, hs:2 * hs]
        x_ref[rows, 0:hs] = nr
        x_ref[rows, hs:2 * hs] = ni
        return nr, ni

    zero = jnp.zeros((8, hs), F32)
    return lax.fori_loop(0, seg, step, (zero, zero), unroll=2)


def _s5_carries(c_ref, e_ref, ends, start, pw_r, pw_i, reverse):
    hs = TILE_STATES
    e_ref[:, 0:hs] = ends[0]
    e_ref[:, hs:2 * hs] = ends[1]
    cr, ci = start
    if reverse:
        pw_i = -pw_i
    for s in (range(7, -1, -1) if reverse else range(8)):
        c_ref[s:s + 1, 0:hs] = cr
        c_ref[s:s + 1, hs:2 * hs] = ci
        er, ei = e_ref[s:s + 1, 0:hs], e_ref[s:s + 1, hs:2 * hs]
        cr, ci = er + pw_r * cr - pw_i * ci, ei + pw_r * ci + pw_i * cr
    return cr, ci


def _s5_states(u_perm_b16, bd_ref, x_ref, base, c_ref, e_ref, p_ref, lr, li, h_in, t_len):
    hs = TILE_STATES
    seg = t_len // 8
    x_ref[pl.ds(base, t_len), :] = jnp.dot(u_perm_b16, bd_ref[0], preferred_element_type=F32)
    ends = _s5_local_scan(x_ref, base, lr, li, seg, False)
    pw_r, pw_i = p_ref[seg - 1:seg, 0:hs], p_ref[seg - 1:seg, hs:2 * hs]
    h_out = _s5_carries(c_ref, e_ref, ends, h_in, pw_r, pw_i, False)
    cr, ci = c_ref[:, 0:hs], c_ref[:, hs:2 * hs]

    def fix(k, carry):
        rows = pl.ds(pl.multiple_of(base + 8 * k, 8), 8)
        pr, pi = p_ref[pl.ds(k, 1), 0:hs], p_ref[pl.ds(k, 1), hs:2 * hs]
        x_ref[rows, 0:hs] += pr * cr - pi * ci
        x_ref[rows, hs:2 * hs] += pr * ci + pi * cr
        return carry

    lax.fori_loop(0, seg, fix, 0, unroll=2)
    return h_out


def _s5_fwd(proj, u_off, bd, cbd, lam, dvec, t_len):
    l = proj.shape[0]
    nj = bd.shape[0]
    nch = l // t_len
    hs = TILE_STATES
    ub = u_off // LANES
    seg = t_len // 8
    assert t_len % 64 == 0

    def body(u_ref, bd_ref, cbd_ref, lam_ref, d_ref, y_ref, hst_ref, x_ref, h_ref, p_ref, c_ref, e_ref, up_ref, yp_ref):
        lr, li = lam_ref[0, 0:1, :], lam_ref[0, 1:2, :]

        @pl.when(pl.program_id(1) == 0)
        def _():
            h_ref[...] = jnp.zeros_like(h_ref)
            _s5_powers(p_ref, lr, li, seg)

        hst_ref[0, 0] = h_ref[...]
        _s5_permute(u_ref, up_ref, t_len)
        h_out = _s5_states(up_ref[...].astype(BF16), bd_ref, x_ref, 0, c_ref, e_ref, p_ref, lr, li,
                           (h_ref[:, 0:hs], h_ref[:, hs:2 * hs]), t_len)
        h_ref[:, 0:hs] = h_out[0]
        h_ref[:, hs:2 * hs] = h_out[1]
        yp_ref[...] = jnp.dot(x_ref[...].astype(BF16), cbd_ref[0], preferred_element_type=F32)
        dv = d_ref[0]

        def out(r0, rows):
            y_ref[r0:r0 + 8, :] = rows + dv * u_ref[r0:r0 + 8, :]

        _s5_unpermute(yp_ref, t_len, out)

    return pl.pallas_call(
        body, grid=(nj, nch),
        in_specs=[pl.BlockSpec((t_len, LANES), lambda j, c: (c, ub + j)),
                  pl.BlockSpec((1, LANES, 2 * hs), lambda j, c: (j, 0, 0)),
                  pl.BlockSpec((1, 2 * hs, LANES), lambda j, c: (j, 0, 0)),
                  pl.BlockSpec((1, 2, hs), lambda j, c: (j, 0, 0)),
                  pl.BlockSpec((1, 1, LANES), lambda j, c: (j, 0, 0))],
        out_specs=[pl.BlockSpec((t_len, LANES), lambda j, c: (c, j)),
                   pl.BlockSpec((1, 1, 1, 2 * hs), lambda j, c: (j, c, 0, 0))],
        out_shape=[jax.ShapeDtypeStruct((l, nj * LANES), F32),
                   jax.ShapeDtypeStruct((nj, nch, 1, 2 * hs), F32)],
        scratch_shapes=[pltpu.VMEM((t_len, 2 * hs), F32), pltpu.VMEM((1, 2 * hs), F32),
                        pltpu.VMEM((seg, 2 * hs), F32), pltpu.VMEM((8, 2 * hs), F32), pltpu.VMEM((8, 2 * hs), F32),
                        pltpu.VMEM((t_len, LANES), F32), pltpu.VMEM((t_len, LANES), F32)],
        name="s5_fwd", compiler_params=_params(2),
    )(proj, bd, cbd, lam, dvec)


def _s5_bwd(proj, u_off, dy, hst, bd, bdt, cbdt, lam, dvec, t_len):
    l = proj.shape[0]
    nj = bd.shape[0]
    nch = l // t_len
    hs = TILE_STATES
    ub = u_off // LANES
    seg = t_len // 8
    tn = (((0,), (0,)), ((), ()))
    assert t_len % 64 == 0

    def body(u_ref, dy_ref, hst_ref, bd_ref, bdt_ref, cbdt_ref, lam_ref, d_ref,
             du_ref, dbd_ref, dcbdt_ref, dlam_ref, dd_ref,
             x_ref, g_ref, gc_ref, p_ref, c_ref, e_ref, up_ref, dyp_ref, dup_ref):
        first = pl.program_id(1) == 0
        lr, li = lam_ref[0, 0:1, :], lam_ref[0, 1:2, :]

        @pl.when(first)
        def _():
            gc_ref[...] = jnp.zeros_like(gc_ref)
            _s5_powers(p_ref, lr, li, seg)

        _s5_permute(u_ref, up_ref, t_len)
        _s5_permute(dy_ref, dyp_ref, t_len)
        ub16, dyb16 = up_ref[...].astype(BF16), dyp_ref[...].astype(BF16)
        h0 = hst_ref[0, 0]
        _s5_states(ub16, bd_ref, x_ref, 8, c_ref, e_ref, p_ref, lr, li, (h0[:, 0:hs], h0[:, hs:2 * hs]), t_len)
        x_ref[0:8, :] = c_ref[...]
        g_ref[...] = jnp.dot(dyb16, cbdt_ref[0], preferred_element_type=F32)
        starts = _s5_local_scan(g_ref, 0, lr, li, seg, True)
        pw_r, pw_i = p_ref[seg - 1:seg, 0:hs], p_ref[seg - 1:seg, hs:2 * hs]
        g_out = _s5_carries(c_ref, e_ref, starts, (gc_ref[:, 0:hs], gc_ref[:, hs:2 * hs]), pw_r, pw_i, True)
        gc_ref[:, 0:hs] = g_out[0]
        gc_ref[:, hs:2 * hs] = g_out[1]
        cr, ci = c_ref[:, 0:hs], c_ref[:, hs:2 * hs]

        def fix(k, carry):
            alr, ali = carry
            rows = pl.ds(pl.multiple_of(8 * k, 8), 8)
            pr, pi = p_ref[pl.ds(seg - 1 - k, 1), 0:hs], p_ref[pl.ds(seg - 1 - k, 1), hs:2 * hs]
            gr = g_ref[rows, 0:hs] + pr * cr + pi * ci
            gi = g_ref[rows, hs:2 * hs] + pr * ci - pi * cr
            g_ref[rows, 0:hs] = gr
            g_ref[rows, hs:2 * hs] = gi
            hpr, hpi = x_ref[rows, 0:hs], x_ref[rows, hs:2 * hs]
            return alr + gr * hpr + gi * hpi, ali + gi * hpr - gr * hpi

        zero = jnp.zeros((8, hs), F32)
        alr, ali = lax.fori_loop(0, seg, fix, (zero, zero), unroll=2)
        alr, ali = _colsum(alr), _colsum(ali)
        g = g_ref[...].astype(BF16)
        h = x_ref[pl.ds(8, t_len), :].astype(BF16)
        dup_ref[...] = jnp.dot(g, bdt_ref[0], preferred_element_type=F32)
        dv = d_ref[0]

        def out(r0, rows):
            du_ref[r0:r0 + 8, :] = (rows + dv * dy_ref[r0:r0 + 8, :]).astype(du_ref.dtype)

        _s5_unpermute(dup_ref, t_len, out)
        sign = jnp.where(lax.broadcasted_iota(jnp.int32, (1, 2 * hs), 1) < hs, 1.0, -1.0)
        dbd = lax.dot_general(ub16, g, tn, preferred_element_type=F32)
        dcbdt = lax.dot_general(dyb16, h, tn, preferred_element_type=F32) * sign
        ddv = _colsum(dy_ref[...] * u_ref[...])

        @pl.when(first)
        def _():
            dbd_ref[0] = dbd
            dcbdt_ref[0] = dcbdt
            dlam_ref[0, 0:1, :] = alr
            dlam_ref[0, 1:2, :] = ali
            dd_ref[0] = ddv

        @pl.when(jnp.logical_not(first))
        def _():
            dbd_ref[0] += dbd
            dcbdt_ref[0] += dcbdt
            dlam_ref[0, 0:1, :] += alr
            dlam_ref[0, 1:2, :] += ali
            dd_ref[0] += ddv

    rev = lambda c: nch - 1 - c
    wide = pl.BlockSpec((1, LANES, 2 * hs), lambda j, c: (j, 0, 0))
    tall = pl.BlockSpec((1, 2 * hs, LANES), lambda j, c: (j, 0, 0))
    return pl.pallas_call(
        body, grid=(nj, nch),
        in_specs=[pl.BlockSpec((t_len, LANES), lambda j, c: (rev(c), ub + j)),
                  pl.BlockSpec((t_len, LANES), lambda j, c: (rev(c), j)),
                  pl.BlockSpec((1, 1, 1, 2 * hs), lambda j, c: (j, rev(c), 0, 0)),
                  wide, tall, wide,
                  pl.BlockSpec((1, 2, hs), lambda j, c: (j, 0, 0)),
                  pl.BlockSpec((1, 1, LANES), lambda j, c: (j, 0, 0))],
        out_specs=[pl.BlockSpec((t_len, LANES), lambda j, c: (rev(c), j)),
                   wide, wide,
                   pl.BlockSpec((1, 2, hs), lambda j, c: (j, 0, 0)),
                   pl.BlockSpec((1, 1, LANES), lambda j, c: (j, 0, 0))],
        out_shape=[jax.ShapeDtypeStruct((l, nj * LANES), BF16),
                   jax.ShapeDtypeStruct((nj, LANES, 2 * hs), F32),
                   jax.ShapeDtypeStruct((nj, LANES, 2 * hs), F32),
                   jax.ShapeDtypeStruct((nj, 2, hs), F32),
                   jax.ShapeDtypeStruct((nj, 1, LANES), F32)],
        scratch_shapes=[pltpu.VMEM((t_len + 8, 2 * hs), F32), pltpu.VMEM((t_len, 2 * hs), F32),
                        pltpu.VMEM((1, 2 * hs), F32), pltpu.VMEM((seg, 2 * hs), F32),
                        pltpu.VMEM((8, 2 * hs), F32), pltpu.VMEM((8, 2 * hs), F32),
                        pltpu.VMEM((t_len, LANES), F32), pltpu.VMEM((t_len, LANES), F32),
                        pltpu.VMEM((t_len, LANES), F32)],
        name="s5_bwd", compiler_params=_params(2),
    )(proj, dy, hst, bd, bdt, cbdt, lam, dvec)


def _full_spec(shape):
    nd = len(shape)
    return pl.BlockSpec(tuple(shape), lambda i: (0,) * nd)


def _sds(shape, dtype=F32):
    return jax.ShapeDtypeStruct(tuple(shape), dtype)


def kernel(x, c, ada_w, ada_b, norm_mix_g, w_in, attn_sinks, w_attn_proj, ssm_a_re, ssm_a_im, ssm_log_dt, ssm_b_re, ssm_b_im, ssm_c_re, ssm_c_im, ssm_d, w_ssm_glu, w_out, norm_ffn_g, w_ffn_up, ffn_conv_w, ffn_conv_b, w_ffn_down, final_g, loss_target, m_ada_w, m_ada_b, m_norm_mix_g, m_w_in, m_attn_sinks, m_w_attn_proj, m_ssm_a_re, m_ssm_a_im, m_ssm_log_dt, m_ssm_b_re, m_ssm_b_im, m_ssm_c_re, m_ssm_c_im, m_ssm_d, m_w_ssm_glu, m_w_out, m_norm_ffn_g, m_w_ffn_up, m_ffn_conv_w, m_ffn_conv_b, m_w_ffn_down, m_final_g, v_ada_w, v_ada_b, v_norm_mix_g, v_w_in, v_attn_sinks, v_w_attn_proj, v_ssm_a_re, v_ssm_a_im, v_ssm_log_dt, v_ssm_b_re, v_ssm_b_im, v_ssm_c_re, v_ssm_c_im, v_ssm_d, v_w_ssm_glu, v_w_out, v_norm_ffn_g, v_w_ffn_up, v_ffn_conv_w, v_ffn_conv_b, v_w_ffn_down, v_final_g):
    given = dict(locals())
    names = ['ada_w', 'ada_b', 'norm_mix_g', 'w_in', 'attn_sinks', 'w_attn_proj', 'ssm_a_re', 'ssm_a_im',
             'ssm_log_dt', 'ssm_b_re', 'ssm_b_im', 'ssm_c_re', 'ssm_c_im', 'ssm_d', 'w_ssm_glu', 'w_out',
             'norm_ffn_g', 'w_ffn_up', 'ffn_conv_w', 'ffn_conv_b', 'w_ffn_down', 'final_g']

    xs = x[0]
    tgt = loss_target[0]
    l, d = xs.shape
    attn_w = w_attn_proj.shape[1]
    ssm_w = w_ssm_glu.shape[1]
    hq = attn_sinks.shape[1]
    qpk = hq // N_KV_HEADS
    kv_w = N_KV_HEADS * HEAD_DIM
    n_groups = ssm_a_re.shape[1]
    dff = ffn_conv_b.shape[1]
    in_w = attn_w + 2 * kv_w + ssm_w + 2 * d
    nj = ssm_w // LANES
    off_k, off_v, off_u = attn_w, attn_w + kv_w, attn_w + 2 * kv_w
    off_ga, off_gs = off_u + ssm_w, off_u + ssm_w + d
    assert hq * HEAD_DIM == attn_w and n_groups * SSM_P == ssm_w and l % ATT_BLOCK == 0

    xi, yi, ci = _dev()
    idx = 4 * xi + 2 * yi + ci

    row_sharded = {'w_out': (d, d), 'w_ffn_down': (dff, d)}
    big = ['w_in', 'w_attn_proj', 'w_ssm_glu', 'w_out', 'w_ffn_up', 'w_ffn_down']
    spack, s_offs = _pack([c, ffn_conv_w[0]], LANES, 8)
    w16 = {k: given[k][0].astype(BF16) for k in big}
    wg_in, sg = _all_gather("gather_first", [w16['w_in'], spack])
    mixer_w = ['w_attn_proj', 'w_ssm_glu', 'w_out']
    h_mixer, tok = _exchange_start("gather_mixer_start", [w16[k] for k in mixer_w], True, wg_in)
    h_up, tok = _exchange_start("gather_ffn_up_start", [w16['w_ffn_up']], True, tok)
    h_down, tok = _exchange_start("gather_ffn_down_start", [w16['w_ffn_down']], True, tok)
    full = {'w_in': wg_in.transpose(1, 0, 2).reshape(d, in_w)}
    c_all = _unpack(sg, s_offs[0], (d,), lead=(N_DEV,))
    conv_w = _unpack(sg, s_offs[1], ffn_conv_w.shape[1:], lead=(N_DEV,)).transpose(1, 0, 2).reshape(3, dff)
    conv_b = ffn_conv_b

    mod_n = ada_w.shape[2]
    tcm = _pick(mod_n, 512)
    ada_b_mine = lax.dynamic_slice_in_dim(ada_b, idx * mod_n, mod_n, axis=1)

    def modpart_fn(cv, wv, bv):
        cond = cv * jax.nn.sigmoid(cv)
        return jnp.dot(cond.astype(BF16), wv.astype(BF16), preferred_element_type=F32) + bv, cond

    modp, cond_all = _tile_call(
        "ada_rows", modpart_fn, (mod_n // tcm,), [c_all, ada_w[0], ada_b_mine],
        [pl.BlockSpec((N_DEV, d), lambda j: (0, 0)), pl.BlockSpec((d, tcm), lambda j: (0, j)),
         pl.BlockSpec((1, tcm), lambda j: (0, j))],
        [_sds((N_DEV, mod_n)), _sds((N_DEV, d))],
        [pl.BlockSpec((N_DEV, tcm), lambda j: (0, j)), pl.BlockSpec((N_DEV, d), lambda j: (0, 0))])
    (modg,) = _all_gather("gather_ada_rows", [modp])
    mod = lax.dynamic_index_in_dim(modg, idx, axis=1, keepdims=False).reshape(1, N_DEV * mod_n)
    sh1, sc1, g1, sh2, sc2, g2 = [mod[:, i * d:(i + 1) * d] for i in range(6)]

    tr = _pick(l, 256, 8)
    trh = _pick(l, 256, 8)
    nr, nrh = l // tr, l // trh
    g_mix, g_ffn, g_fin = norm_mix_g + tok[0:1, 0:1], norm_ffn_g, final_g.reshape(1, d)

    def with_t(fn):
        def wrapped(*vals):
            out = fn(*vals)
            out = out if isinstance(out, tuple) else (out,)
            return out + (out[-1].T,)
        return wrapped

    h1, h1_t = _tile_call("norm_mod_mix", with_t(_norm_mod), (1, nr), [xs, g_mix, sc1, sh1],
                          [_t(tr, d), _v(d), _v(d), _v(d)], [_sds((l, d), BF16), _sds((d, l), BF16)],
                          [_t(tr, d), _tt(tr, d)])
    proj = _matmul("proj_in", h1, full['w_in'], "nn", tn=1280)

    def heads(z, n):
        return z.reshape(l, n, HEAD_DIM).transpose(1, 0, 2)

    qh = heads(proj[:, :attn_w], hq)
    kh = heads(proj[:, off_k:off_k + kv_w], N_KV_HEADS)
    vh = heads(proj[:, off_v:off_v + kv_w], N_KV_HEADS)
    sinks3 = jnp.repeat(attn_sinks.reshape(N_KV_HEADS, qpk), ATT_BLOCK, axis=1)[..., None]
    o_h = _attn_fwd(qh, kh, vh, sinks3)
    o2 = o_h.transpose(1, 0, 2).reshape(l, attn_w)

    gn = (n_groups, SSM_N)
    pgn = (SSM_P, n_groups, SSM_N)
    a_re, a_im, log_dt = ssm_a_re[0], ssm_a_im[0], ssm_log_dt[0].reshape(n_groups, 1)
    b_re, b_im = ssm_b_re[0].transpose(2, 0, 1), ssm_b_im[0].transpose(2, 0, 1)
    disc_ins = [a_re, a_im, log_dt, b_re, b_im]
    disc_specs = [_full_spec(gn), _full_spec(gn), _full_spec((n_groups, 1)), _full_spec(pgn), _full_spec(pgn)]
    lam_r, lam_i, bb_r, bb_i = _tile_call(
        "s5_discretise", _s5_disc_fn, (1,), disc_ins, disc_specs,
        [_sds(gn), _sds(gn), _sds(pgn), _sds(pgn)],
        [_full_spec(gn), _full_spec(gn), _full_spec(pgn), _full_spec(pgn)])

    def tiles_gpn(z):
        return z.reshape(SSM_P, nj, TILE_GROUPS, SSM_N).transpose(1, 2, 0, 3)

    bd = jnp.concatenate([_block_diag(tiles_gpn(bb_r)), _block_diag(tiles_gpn(bb_i))], axis=2).astype(BF16)
    c_r = ssm_c_re[0].reshape(nj, TILE_GROUPS, SSM_P, SSM_N).transpose(0, 1, 3, 2)
    c_i = (-ssm_c_im[0]).reshape(nj, TILE_GROUPS, SSM_P, SSM_N).transpose(0, 1, 3, 2)
    cbd = jnp.concatenate([_block_diag(c_r), _block_diag(c_i)], axis=1).astype(BF16)
    bdt, cbdt = bd.transpose(0, 2, 1), cbd.transpose(0, 2, 1)
    lam = jnp.stack([lam_r.reshape(nj, TILE_STATES), lam_i.reshape(nj, TILE_STATES)], axis=1)
    dvec = ssm_d[0].reshape(nj, 1, LANES)
    t_len = _pick(l, 2048, 8)
    y, hst = _s5_fwd(proj, off_u, bd, cbd, lam, dvec, t_len)

    tcs, trg = _pick(ssm_w, 1024), _pick(l, 512, 8)
    gy = _tile_call("gelu", lambda v: jax.nn.gelu(v), (ssm_w // tcs, l // trg), [y], [_t(trg, tcs)],
                    [_sds((l, ssm_w), BF16)], [_t(trg, tcs)])[0]
    full.update(zip(mixer_w, _exchange_wait("gather_mixer_wait", h_mixer, gy)))
    full['w_out'] = full['w_out'].reshape(row_sharded['w_out'])
    full['w_attn_proj'] = full['w_attn_proj'].transpose(1, 0, 2).reshape(attn_w, d)
    full['w_ssm_glu'] = full['w_ssm_glu'].transpose(1, 0, 2).reshape(ssm_w, 2 * d)
    glu = _matmul("ssm_glu", gy, full['w_ssm_glu'], "nn")

    tcd = 256 if d % 256 == 0 and off_ga % 256 == 0 else LANES
    assert d % tcd == 0 and off_ga % tcd == 0 and off_gs % tcd == 0
    gate_ins = [(glu, 0), (glu, d), (proj, off_ga), (proj, off_gs)]

    def mix_epilogue(at, ga_, gb_, pa, ps):
        return at, _mix_fn(ga_, gb_, at, pa, ps)

    attn, mixed = _matmul("attn_proj_gate_mix", o2, full['w_attn_proj'], "nn", tn=tcd,
                          epilogue=(mix_epilogue, gate_ins, [(F32, False), (BF16, False)]))
    def res_norm_fn(xv, mo, g1v, gv, scv, shv):
        x2v = xv + g1v * mo
        return x2v, _norm_mod(x2v, gv, scv, shv)

    def res_norm_epilogue(mo, xv, g1v, gv, scv, shv):
        x2v, h2v = res_norm_fn(xv, mo, g1v, gv, scv, shv)
        return mo, x2v, h2v, h2v.T

    mixout, x2, h2, h2_t = _matmul(
        "mix_out_residual_norm_mod_ffn", mixed, full['w_out'], "nn", tm=256, tn=d,
        epilogue=(res_norm_epilogue, [xs, g1, g_ffn, sc2, sh2], [(F32, False), (F32, False), (BF16, False), (BF16, True)]))
    full['w_ffn_up'], = _exchange_wait("gather_ffn_up_wait", h_up, h2)
    up = _matmul("ffn_up", h2, full['w_ffn_up'], "nn", out_dtype=BF16, tn=1408)

    tcf, trc = _pick(dff, 1408), _pick(l, 512, 8)
    assert dff % tcf == 0
    ncf = dff // tcf

    taps = [conv_w[i:i + 1] for i in range(3)]

    def conv_gate(gp, gp_prev, w0, w1, w2, bv):
        gp = gp.astype(F32)
        prev = jnp.where(pl.program_id(1) == 0, 0.0, 1.0) * gp_prev.astype(F32)
        ext = jnp.concatenate([prev, gp], axis=0)
        m1 = pltpu.roll(ext, 1, 0)[HALO:]
        m2 = pltpu.roll(ext, 2, 0)[HALO:]
        return w0 * m2 + w1 * m1 + w2 * gp + bv, m1, m2

    def convglu_fn(gp, gp_prev, val, w0, w1, w2, bv):
        gate, _, _ = conv_gate(gp, gp_prev, w0, w1, w2, bv)
        return gate * jax.nn.sigmoid(gate) * val.astype(F32)

    act, act_t = _tile_call("conv_swiglu", with_t(convglu_fn), (ncf, l // trc), [up, up, up] + taps + [conv_b],
                            [_t(trc, tcf), _prev_rows(trc, tcf), _t(trc, tcf, dff)] + [_v(tcf)] * 4,
                            [_sds((l, dff), BF16), _sds((dff, l), BF16)], [_t(trc, tcf), _tt(trc, tcf)])
    full['w_ffn_down'] = _exchange_wait("gather_ffn_down_wait", h_down, act)[0].reshape(row_sharded['w_ffn_down'])
    ffn = _matmul("ffn_down", act, full['w_ffn_down'], "nn", tm=512)

    def final_fn(x2v, fv, g2v, gv, tv):
        rows = x2v.shape[0]

        def loss_of(x2a, fa, g2a, ga):
            out = _rms(x2a + g2a * fa, ga)
            err = out - tv
            return 0.5 * _colsum(jnp.mean(err * err, axis=-1, keepdims=True))

        loss, vjp = jax.vjp(loss_of, x2v, fv, _bc(g2v, rows), _bc(gv, rows))
        dx3, dffn, dg2, dgf = vjp(jnp.ones((1, 1), F32))
        return jnp.broadcast_to(loss, (1, LANES)), dx3, dffn, _colsum(dg2), _colsum(dgf)

    loss_p, dx3, dffn, dg2, dg_fin = _tile_call(
        "loss_final_norm", final_fn, (1, nrh), [x2, ffn, g2, g_fin, tgt],
        [_t(trh, d), _t(trh, d), _v(d), _v(d), _t(trh, d)],
        [_sds((1, LANES)), _sds((l, d)), _sds((l, d), BF16), _sds((1, d)), _sds((1, d))],
        [_v(LANES), _t(trh, d), _t(trh, d), _v(d), _v(d)], acc=(0, 3, 4))
    loss = lax.psum(loss_p[0, 0], ("x", "y", "c"))

    dact = _matmul("d_act", dffn, full['w_ffn_down'], "nt", out_dtype=BF16, tn=1408, dep=loss.reshape(1, 1))
    gd, gd16, pending = {}, {}, []
    dw_down, dw_down16 = _matmul("dw_ffn_down", act_t, dffn, "nn", tm=512, also_bf16=True)
    gd['w_ffn_down'], gd16['w_ffn_down'] = [z.reshape((N_DEV,) + w_ffn_down.shape[1:]) for z in (dw_down, dw_down16)]
    handle, tok = _exchange_start("grad_ffn_down_start", [gd16['w_ffn_down']], False, loss.reshape(1, 1))
    pending.append((['w_ffn_down'], handle))
    conv_b_bwd = conv_b + tok[0:1, 0:1]

    def convglu_bwd_fn(gp, gp_prev, gp_next, val, val_next, da, da_next, w0, w1, w2, bv):
        rows = gp.shape[0]
        i = pl.program_id(1)
        gp, val, da = gp.astype(F32), val.astype(F32), da.astype(F32)
        prev = jnp.where(i == 0, 0.0, 1.0) * gp_prev.astype(F32)
        more = jnp.where(i == pl.num_programs(1) - 1, 0.0, 1.0)
        ext = jnp.concatenate([prev, gp, gp_next.astype(F32)], axis=0)
        cur = ext[HALO:]
        m1 = pltpu.roll(ext, 1, 0)[HALO:]
        m2 = pltpu.roll(ext, 2, 0)[HALO:]
        gate = w0 * m2 + w1 * m1 + w2 * cur + bv
        sg = jax.nn.sigmoid(gate)
        val_e = jnp.concatenate([val, val_next.astype(F32)], axis=0)
        da_e = jnp.concatenate([da, more * da_next.astype(F32)], axis=0)
        dgate = da_e * val_e * (sg * (1.0 + gate * (1.0 - sg)))
        p1 = pltpu.roll(dgate, rows + HALO - 1, 0)[:rows]
        p2 = pltpu.roll(dgate, rows + HALO - 2, 0)[:rows]
        dg = dgate[:rows]
        dgp = w2 * dg + w1 * p1 + w0 * p2
        dval = da * (gate[:rows] * sg[:rows])
        return (jnp.stack([dgp, dval], axis=0), _colsum(dg), _colsum(dg * m2[:rows]), _colsum(dg * m1[:rows]),
                _colsum(dg * gp))

    dup, dconv_b, dcw0, dcw1, dcw2 = _tile_call(
        "conv_swiglu_bwd", convglu_bwd_fn, (ncf, l // trc), [up, up, up, up, up, dact, dact] + taps + [conv_b_bwd],
        [_t(trc, tcf), _prev_rows(trc, tcf), _next_rows(trc, tcf, l), _t(trc, tcf, dff), _next_rows(trc, tcf, l, dff),
         _t(trc, tcf), _next_rows(trc, tcf, l)] + [_v(tcf)] * 4,
        [_sds((2, l, dff), BF16)] + [_sds((1, dff))] * 4, [_st(trc, tcf)] + [_v(tcf)] * 4, acc=(1, 2, 3, 4))
    dh2 = _matmul("d_h2", dup, full['w_ffn_up'], "nt", tm=512, fold=4)
    gd['w_ffn_up'], gd16['w_ffn_up'] = _matmul("dw_ffn_up", h2_t, dup, "nn", tm=512, tn=1408, out_stack=N_DEV, also_bf16=True)
    handle, tok = _exchange_start("grad_ffn_up_start", [gd16['w_ffn_up']], False, gd['w_ffn_up'])
    pending.append((['w_ffn_up'], handle))
    g_ffn_bwd = g_ffn + tok[0:1, 0:1]

    def res_norm_bwd_fn(xv, mo, g1v, gv, scv, shv, dhv, dxv):
        rows = xv.shape[0]
        _, vjp = jax.vjp(res_norm_fn, xv, mo, _bc(g1v, rows), _bc(gv, rows), _bc(scv, rows), _bc(shv, rows))
        dx, dmo, dg1v, dgv, dscv, dshv = vjp((dxv, dhv))
        return dx, dmo, _colsum(dg1v), _colsum(dgv), _colsum(dscv), _colsum(dshv)

    dx2, dmixout, dg1, dg_ffn, dsc2, dsh2 = _tile_call(
        "residual_norm_mod_ffn_bwd", res_norm_bwd_fn, (1, nrh), [xs, mixout, g1, g_ffn_bwd, sc2, sh2, dh2, dx3],
        [_t(trh, d), _t(trh, d), _v(d), _v(d), _v(d), _v(d), _t(trh, d), _t(trh, d)],
        [_sds((l, d)), _sds((l, d), BF16)] + [_sds((1, d))] * 4,
        [_t(trh, d), _t(trh, d)] + [_v(d)] * 4, acc=(2, 3, 4, 5))

    def mix_bwd_epilogue(dm, ga_, gb_, pa, ps, at):
        _, vjp = jax.vjp(_mix_fn, ga_, gb_, at, pa, ps)
        da, db, dat, dpa, dps = vjp(dm)
        return jnp.stack([da, db], axis=0), dat, dpa, dps

    dglu, dattn, dga, dgs = _matmul(
        "d_mixed_gate_mix_bwd", dmixout, full['w_out'], "nt", tn=tcd,
        epilogue=(mix_bwd_epilogue, gate_ins + [attn], [(BF16, 'pair')] + [(BF16, False)] * 3))
    dw_out, dw_out16 = _matmul("dw_out", mixed, dmixout, "tn", also_bf16=True)
    gd['w_out'], gd16['w_out'] = [z.reshape((N_DEV,) + w_out.shape[1:]) for z in (dw_out, dw_out16)]

    def gelu_bwd_epilogue(dgy, yv):
        _, vjp = jax.vjp(lambda z: jax.nn.gelu(z), yv)
        return (vjp(dgy)[0],)

    dy, = _matmul("d_gelu_y_gelu_bwd", dglu, full['w_ssm_glu'], "nt", epilogue=(gelu_bwd_epilogue, [y], [(F32, False)]))
    gd['w_ssm_glu'], gd16['w_ssm_glu'] = _matmul("dw_ssm_glu", gy, dglu, "tn", out_stack=N_DEV, also_bf16=True)
    du, dbd, dcbdt, dlam, dd_tiles = _s5_bwd(proj, off_u, dy, hst, bd, bdt, cbdt, lam, dvec, t_len)

    def gpn_of(z):
        return z.transpose(2, 0, 1, 3).reshape(pgn)

    dbb_r = gpn_of(_diag_blocks(dbd[:, :, :TILE_STATES], SSM_P))
    dbb_i = gpn_of(_diag_blocks(dbd[:, :, TILE_STATES:], SSM_P))
    dc_re = _diag_blocks(dcbdt[:, :, :TILE_STATES], SSM_P).reshape(n_groups, SSM_P, SSM_N)
    dc_im = _diag_blocks(dcbdt[:, :, TILE_STATES:], SSM_P).reshape(n_groups, SSM_P, SSM_N)
    dlam_r, dlam_i = dlam[:, 0].reshape(gn), dlam[:, 1].reshape(gn)

    def disc_bwd_fn(ar, ai, ld, br, bi, dlr, dli, dbr, dbi):
        _, vjp = jax.vjp(_s5_disc_fn, ar, ai, ld, br, bi)
        return vjp((dlr, dli, dbr, dbi))

    da_re, da_im, dlog_dt, db_re, db_im = _tile_call(
        "s5_discretise_bwd", disc_bwd_fn, (1,), disc_ins + [dlam_r, dlam_i, dbb_r, dbb_i],
        disc_specs + [_full_spec(gn), _full_spec(gn), _full_spec(pgn), _full_spec(pgn)],
        [_sds(gn), _sds(gn), _sds((n_groups, 1)), _sds(pgn), _sds(pgn)], disc_specs)

    do2 = _matmul("d_attn_heads", dattn, full['w_attn_proj'], "nt")
    gd['w_attn_proj'], gd16['w_attn_proj'] = _matmul("dw_attn_proj", o2, dattn, "tn", out_stack=N_DEV, also_bf16=True)
    handle, tok = _exchange_start("grad_mixer_start", [gd16[k] for k in mixer_w], False, gd['w_attn_proj'])
    pending.append((mixer_w, handle))
    do_h = heads(do2.astype(BF16), hq)
    dq_h, dk_h, dv_h, dsink = _attn_bwd(qh, kh, vh, sinks3 + tok[0:1, 0:1], do_h)

    def unheads(z):
        return z.transpose(1, 0, 2).reshape(l, z.shape[0] * HEAD_DIM)

    early = ['attn_sinks', 'ssm_a_re', 'ssm_a_im', 'ssm_log_dt', 'ssm_b_re', 'ssm_b_im', 'ssm_c_re', 'ssm_c_im',
             'ssm_d', 'norm_ffn_g', 'ffn_conv_b', 'final_g']
    early_grads = {
        'attn_sinks': dsink[:, 0, 0], 'ssm_a_re': da_re, 'ssm_a_im': da_im, 'ssm_log_dt': dlog_dt,
        'ssm_b_re': db_re.transpose(1, 2, 0), 'ssm_b_im': db_im.transpose(1, 2, 0), 'ssm_c_re': dc_re,
        'ssm_c_im': dc_im, 'ssm_d': dd_tiles, 'norm_ffn_g': dg_ffn, 'ffn_conv_b': dconv_b, 'final_g': dg_fin}
    ge_pack, e_offs = _pack([jnp.concatenate([dg1, dsh2, dsc2, dg2], axis=1)] + [early_grads[k] for k in early],
                            LANES, 8)
    h_early, tok = _exchange_start("gather_small_early_start", [ge_pack], True, dsink)

    dproj = jnp.concatenate([unheads(dq_h), unheads(dk_h), unheads(dv_h), du, dga, dgs], axis=1)
    dw_in, dw_in16 = _matmul("dw_in", h1_t, dproj, "nn", tm=512, tn=1280, also_bf16=True, dep=tok)
    dcw = jnp.concatenate([dcw0, dcw1, dcw2], axis=0)
    shard_in, shard_cw = w_in.shape[1:], ffn_conv_w.shape[1:]
    gd16['w_in'] = dw_in16.reshape(shard_in[0], N_DEV, shard_in[1]).transpose(1, 0, 2)
    own_in = lax.dynamic_slice_in_dim(dw_in, idx * shard_in[1], shard_in[1], axis=1)[None]
    gd['ffn_conv_w'] = dcw.reshape(shard_cw[0], N_DEV, shard_cw[1]).transpose(1, 0, 2)
    gd16['ffn_conv_w'] = gd['ffn_conv_w'].astype(BF16)
    handle, tok = _exchange_start("grad_in_start", [gd16['w_in'], gd16['ffn_conv_w']], False, dw_in)
    pending.append((['w_in', 'ffn_conv_w'], handle))
    dh1 = _matmul("d_h1", dproj, full['w_in'], "nt", tm=512, dep=tok)

    def norm_bwd_fn(xv, gv, scv, shv, dhv, dxv):
        rows = xv.shape[0]
        _, vjp = jax.vjp(_norm_mod, xv, _bc(gv, rows), _bc(scv, rows), _bc(shv, rows))
        dx, dgv, dscv, dshv = vjp(dhv)
        return dx + dxv, _colsum(dgv), _colsum(dscv), _colsum(dshv)

    grad_x, dg_mix, dsc1, dsh1 = _tile_call(
        "norm_mod_mix_bwd", norm_bwd_fn, (1, nrh), [xs, g_mix, sc1, sh1, dh1, dx2],
        [_t(trh, d), _v(d), _v(d), _v(d), _t(trh, d), _t(trh, d)],
        [_sds((l, d))] + [_sds((1, d))] * 3, [_t(trh, d)] + [_v(d)] * 3, acc=(1, 2, 3))

    gl_pack, l_offs = _pack([jnp.concatenate([dsh1, dsc1], axis=1), dg_mix], LANES, 8)
    h_late, tok = _exchange_start("gather_small_late_start", [gl_pack], True, grad_x)

    sharded = big + ['ffn_conv_w']
    sharded_out = {}

    def finish(group, handle, after):
        for k, parts in zip(group, _exchange_wait("grad_" + group[0] + "_wait", handle, after)):
            own_src, own_at = (own_in, 0 * idx) if k == 'w_in' else (gd[k], idx)
            sharded_out[k] = _adamw_sharded("adamw_" + k, parts, own_src, given[k][0], given['m_' + k][0],
                                            given['v_' + k][0], jnp.stack([idx, own_at]).astype(jnp.int32))

    for group, handle in pending[:-1]:
        finish(group, handle, tok)
    done = functools.reduce(lambda p, q: p + q, [sharded_out[k][1][0:1, 0:1] for g_, _ in pending[:-1] for k in g_])
    finish(*pending[-1], done)

    ge_all, = _exchange_wait("gather_small_early_wait", h_early, done)
    gl_all, = _exchange_wait("gather_small_late_wait", h_late, sharded_out['w_in'][1])
    gs_all = jnp.concatenate([ge_all, gl_all], axis=1)
    rows_e = ge_pack.shape[0]

    def small_pack(prefix):
        ab = given[prefix + 'ada_b']
        p_early, _ = _pack([ab[:, 2 * d:]] + [given[prefix + k] for k in early], LANES, 8)
        p_late, _ = _pack([ab[:, :2 * d], given[prefix + 'norm_mix_g']], LANES, 8)
        return jnp.concatenate([p_early, p_late], axis=0)

    small_out = _adamw("adamw_replicated", gs_all, small_pack(''), small_pack('m_'), small_pack('v_'))

    dmod_all = jnp.concatenate([_unpack(gl_all, l_offs[0], (2 * d,), lead=(N_DEV,)),
                                _unpack(ge_all, e_offs[0], (4 * d,), lead=(N_DEV,))], axis=1)
    dmod_mine = lax.dynamic_slice_in_dim(dmod_all, idx * mod_n, mod_n, axis=1)
    kpad = LANES - N_DEV
    cond_t = jnp.pad(cond_all.T, ((0, 0), (0, kpad)))
    dmod_pad = jnp.pad(dmod_mine, ((0, kpad), (0, 0)))
    g_ada_w = _matmul("dw_ada", cond_t, dmod_pad, "nn")
    ada_out = _adamw("adamw_ada_w", g_ada_w[None], ada_w[0], m_ada_w[0], v_ada_w[0])

    results = [{}, {}, {}, {}]
    for which in range(4):
        out = small_out[which]
        results[which]['ada_b'] = jnp.concatenate([_unpack(out, rows_e + l_offs[0], (1, 2 * d)),
                                                   _unpack(out, e_offs[0], (1, 4 * d))], axis=1)
        results[which]['norm_mix_g'] = _unpack(out, rows_e + l_offs[1], norm_mix_g.shape)
        for k, off in zip(early, e_offs[1:]):
            results[which][k] = _unpack(out, off, given[k].shape)
        for k in sharded:
            results[which][k] = sharded_out[k][which][None]
        results[which]['ada_w'] = ada_out[which][None]
    outs = [loss, grad_x[None]]
    for which in range(4):
        outs += [results[which][k] for k in names]
    return tuple(outs)
```
